```python
import math
import jax, jax.numpy as jnp
from jax import lax
import numpy as np

D_MODEL = 2048
BATCH = 8
SEQ = 4096
DEPTH = 1

D_MIX = D_MODEL
HEAD_DIM = 128
ATTN_WIDTH = D_MIX // 2
ATTN_HEADS = ATTN_WIDTH // HEAD_DIM
SSM_WIDTH = D_MIX - ATTN_WIDTH
SSM_GROUP = 16
SSM_GROUPS = SSM_WIDTH // SSM_GROUP
SSM_STATE = 64
DILATED_PATTERNS = ((128, 1), (512, 4), (2048, 16))
BLK = 128
D_FF = 4 * D_MODEL
PLE_DIM = 256
RMS_EPS = 1e-6
NEG_INF = -1e30

kernel_name = "hybrid_dilated_attn_s5_block"


def _rmsnorm(x, g):
    xf = x.astype(jnp.float32)
    y = xf * lax.rsqrt(jnp.mean(xf * xf, axis=-1, keepdims=True) + RMS_EPS)
    return (y * g.astype(jnp.float32)).astype(x.dtype)


def _dilated_branch(q, k, v, window, dilation):
    B, S, H, E = q.shape
    M = S // dilation
    n_keys = window // dilation
    nb = -(-M // BLK)
    Mp = nb * BLK

    def blocks(t):
        t = t.reshape(B, M, dilation, H, E)
        t = jnp.pad(t, ((0, 0), (0, Mp - M), (0, 0), (0, 0), (0, 0)))
        return t.reshape(B, nb, BLK, dilation, H, E)

    def with_prev(t):
        prev = jnp.pad(t[:, :-1], ((0, 0), (1, 0), (0, 0), (0, 0), (0, 0), (0, 0)))
        return jnp.concatenate([prev, t], axis=2)

    qb = blocks(q)
    kk = with_prev(blocks(k))
    vv = with_prev(blocks(v))

    scale = 1.0 / math.sqrt(E)
    s = jnp.einsum('bnqrhe,bnkrhe->bnrhqk', qb, kk,
                   preferred_element_type=jnp.float32) * scale
    i = jnp.arange(BLK)
    j = jnp.arange(2 * BLK)
    blk = jnp.arange(nb)
    dist = BLK + i[:, None] - j[None, :]
    kpos = (blk[:, None] - 1) * BLK + j[None, :]
    mask = ((dist >= 0) & (dist <= n_keys))[None] & (kpos >= 0)[:, None, :]
    s = jnp.where(mask[None, :, None, None], s, NEG_INF)
    lse = jax.nn.logsumexp(s, axis=-1)
    prob = jnp.exp(s - lse[..., None])
    o = jnp.einsum('bnrhqk,bnkrhe->bnqrhe', prob, vv.astype(jnp.float32))
    o = o.reshape(B, Mp, dilation, H, E)[:, :M].reshape(B, S, H, E)
    lse = lse.transpose(0, 1, 4, 2, 3).reshape(B, Mp, dilation, H)[:, :M].reshape(B, S, H)
    return o, lse


def _dilated_attention(q, k, v):
    outs, lses = [], []
    for window, dilation in DILATED_PATTERNS:
        o, l = _dilated_branch(q, k, v, window, dilation)
        outs.append(o)
        lses.append(l)
    w = jax.nn.softmax(jnp.stack(lses, axis=0), axis=0)
    o = jnp.sum(w[..., None] * jnp.stack(outs, axis=0), axis=0)
    B, S, H, E = q.shape
    return o.reshape(B, S, H * E).astype(q.dtype)


def _ssm_combine(e1, e2):
    a1r, a1i, b1r, b1i = e1
    a2r, a2i, b2r, b2i = e2
    ar = a1r * a2r - a1i * a2i
    ai = a1r * a2i + a1i * a2r
    br = a2r * b1r - a2i * b1i + b2r
    bi = a2r * b1i + a2i * b1r + b2i
    return (ar, ai, br, bi)


def _s5(u, lam_re, lam_im, log_dt, b_re, b_im, c_re, c_im, d_skip, w_glu, b_glu):
    B, S, _ = u.shape
    uf = u.astype(jnp.float32).reshape(B, S, SSM_GROUPS, SSM_GROUP)
    lr0 = lam_re.astype(jnp.float32)
    li0 = lam_im.astype(jnp.float32)
    dt = jnp.exp(log_dt.astype(jnp.float32))[:, None]
    mag = jnp.exp(lr0 * dt)
    abar_r = mag * jnp.cos(li0 * dt)
    abar_i = mag * jnp.sin(li0 * dt)
    num_r = abar_r - 1.0
    num_i = abar_i
    den = lr0 * lr0 + li0 * li0
    coef_r = ((num_r * lr0 + num_i * li0) / den)[..., None]
    coef_i = ((num_i * lr0 - num_r * li0) / den)[..., None]
    br = b_re.astype(jnp.float32)
    bi = b_im.astype(jnp.float32)
    bbar_r = coef_r * br - coef_i * bi
    bbar_i = coef_r * bi + coef_i * br
    bu_r = jnp.einsum('bsgc,gpc->bsgp', uf, bbar_r)
    bu_i = jnp.einsum('bsgc,gpc->bsgp', uf, bbar_i)
    a_r = jnp.broadcast_to(abar_r[None, None], (1, S, SSM_GROUPS, SSM_STATE))
    a_i = jnp.broadcast_to(abar_i[None, None], (1, S, SSM_GROUPS, SSM_STATE))
    _, _, st_r, st_i = lax.associative_scan(_ssm_combine, (a_r, a_i, bu_r, bu_i), axis=1)
    y = (jnp.einsum('gcp,bsgp->bsgc', c_re.astype(jnp.float32), st_r)
         - jnp.einsum('gcp,bsgp->bsgc', c_im.astype(jnp.float32), st_i))
    y = y.reshape(B, S, SSM_WIDTH) + d_skip.astype(jnp.float32) * u.astype(jnp.float32)
    y = jax.nn.gelu(y).astype(u.dtype)
    gate = jax.nn.sigmoid(y @ w_glu + b_glu)
    return y * gate


def _fwd_setup_inputs(seed: int = 0) -> dict:
    key = jax.random.key(seed)
    ks = jax.random.split(key, 32)
    f32 = jnp.float32
    L = DEPTH

    def nrm(k, shape, scale):
        return jax.random.normal(k, shape, f32) * scale

    def gain(k, shape):
        return 1.0 + 0.02 * jax.random.normal(k, shape, f32)

    G, P, C = SSM_GROUPS, SSM_STATE, SSM_GROUP
    n_idx = jnp.arange(P, dtype=f32)
    return {
        "x": nrm(ks[0], (BATCH, SEQ, D_MODEL), 1.0),
        "p": nrm(ks[1], (DEPTH, BATCH, SEQ, PLE_DIM), 1.0),
        "mix_norm_pre": gain(ks[2], (L, D_MODEL)),
        "w_in": nrm(ks[3], (L, D_MODEL, 3 * ATTN_WIDTH + SSM_WIDTH), D_MODEL ** -0.5),
        "lam_re": -0.5 + 0.01 * jax.random.normal(ks[4], (L, G, P), f32),
        "lam_im": math.pi * n_idx + 0.01 * jax.random.normal(ks[5], (L, G, P), f32),
        "log_dt": jax.random.uniform(ks[6], (L, G), f32, math.log(1e-3), math.log(1e-1)),
        "ssm_b_re": nrm(ks[7], (L, G, P, C), (2 * C) ** -0.5),
        "ssm_b_im": nrm(ks[8], (L, G, P, C), (2 * C) ** -0.5),
        "ssm_c_re": nrm(ks[9], (L, G, C, P), (2 * P) ** -0.5),
        "ssm_c_im": nrm(ks[10], (L, G, C, P), (2 * P) ** -0.5),
        "ssm_d": nrm(ks[11], (L, SSM_WIDTH), 1.0),
        "w_glu": nrm(ks[12], (L, SSM_WIDTH, SSM_WIDTH), SSM_WIDTH ** -0.5),
        "b_glu": nrm(ks[13], (L, SSM_WIDTH), 0.01),
        "attn_out_norm": gain(ks[14], (L, ATTN_WIDTH)),
        "ssm_out_norm": gain(ks[15], (L, SSM_WIDTH)),
        "w_out": nrm(ks[16], (L, D_MIX, D_MODEL), D_MIX ** -0.5),
        "mix_norm_post": gain(ks[17], (L, D_MODEL)),
        "mlp_norm_pre": gain(ks[18], (L, D_MODEL)),
        "w_up": nrm(ks[19], (L, D_MODEL, D_FF), D_MODEL ** -0.5),
        "w_down": nrm(ks[20], (L, D_FF, D_MODEL), D_FF ** -0.5),
        "mlp_norm_post": gain(ks[21], (L, D_MODEL)),
        "ple_norm_pre": gain(ks[22], (L, D_MODEL)),
        "w_ple_gate": nrm(ks[23], (L, D_MODEL, D_MODEL), D_MODEL ** -0.5),
        "w_ple_proj": nrm(ks[24], (L, PLE_DIM, D_MODEL), PLE_DIM ** -0.5),
        "ple_norm_post": gain(ks[25], (L, D_MODEL)),
    }


def _fwd_reference(x, p, mix_norm_pre, w_in, lam_re, lam_im, log_dt, ssm_b_re, ssm_b_im,
              ssm_c_re, ssm_c_im, ssm_d, w_glu, b_glu, attn_out_norm, ssm_out_norm,
              w_out, mix_norm_post, mlp_norm_pre, w_up, w_down, mlp_norm_post,
              ple_norm_pre, w_ple_gate, w_ple_proj, ple_norm_post):
    B, S, _ = x.shape
    h = x
    for i in range(DEPTH):
        hn = _rmsnorm(h, mix_norm_pre[i])
        proj = hn @ w_in[i]
        q = proj[..., :ATTN_WIDTH].reshape(B, S, ATTN_HEADS, HEAD_DIM)
        k = proj[..., ATTN_WIDTH:2 * ATTN_WIDTH].reshape(B, S, ATTN_HEADS, HEAD_DIM)
        v = proj[..., 2 * ATTN_WIDTH:3 * ATTN_WIDTH].reshape(B, S, ATTN_HEADS, HEAD_DIM)
        u = proj[..., 3 * ATTN_WIDTH:]
        attn = _dilated_attention(q, k, v)
        ssm = _s5(u, lam_re[i], lam_im[i], log_dt[i], ssm_b_re[i], ssm_b_im[i],
                  ssm_c_re[i], ssm_c_im[i], ssm_d[i], w_glu[i], b_glu[i])
        mixed = jnp.concatenate([_rmsnorm(attn, attn_out_norm[i]),
                                 _rmsnorm(ssm, ssm_out_norm[i])], axis=-1)
        h = h + _rmsnorm(mixed @ w_out[i], mix_norm_post[i])
        hn = _rmsnorm(h, mlp_norm_pre[i])
        ff = jnp.square(jax.nn.relu(hn @ w_up[i])) @ w_down[i]
        h = h + _rmsnorm(ff, mlp_norm_post[i])
        gate = jax.nn.sigmoid(_rmsnorm(h, ple_norm_pre[i]) @ w_ple_gate[i])
        e = p[i] @ w_ple_proj[i]
        h = h + _rmsnorm(gate * e, ple_norm_post[i])
    return h


import jax as _jax
import jax.numpy as _jnp

TWIN_FORMAT = 'train_step'
FWD_PARAMS = ['x', 'p', 'mix_norm_pre', 'w_in', 'lam_re', 'lam_im', 'log_dt', 'ssm_b_re', 'ssm_b_im', 'ssm_c_re', 'ssm_c_im', 'ssm_d', 'w_glu', 'b_glu', 'attn_out_norm', 'ssm_out_norm', 'w_out', 'mix_norm_post', 'mlp_norm_pre', 'w_up', 'w_down', 'mlp_norm_post', 'ple_norm_pre', 'w_ple_gate', 'w_ple_proj', 'ple_norm_post']
TWIN_WEIGHTS = ['mix_norm_pre', 'w_in', 'lam_re', 'lam_im', 'log_dt', 'ssm_b_re', 'ssm_b_im', 'ssm_c_re', 'ssm_c_im', 'ssm_d', 'w_glu', 'b_glu', 'attn_out_norm', 'ssm_out_norm', 'w_out', 'mix_norm_post', 'mlp_norm_pre', 'w_up', 'w_down', 'mlp_norm_post', 'ple_norm_pre', 'w_ple_gate', 'w_ple_proj', 'ple_norm_post']
TWIN_DIFF_INPUT = 'x'
TWIN_INPUTS = ['x', 'p', 'mix_norm_pre', 'w_in', 'lam_re', 'lam_im', 'log_dt', 'ssm_b_re', 'ssm_b_im', 'ssm_c_re', 'ssm_c_im', 'ssm_d', 'w_glu', 'b_glu', 'attn_out_norm', 'ssm_out_norm', 'w_out', 'mix_norm_post', 'mlp_norm_pre', 'w_up', 'w_down', 'mlp_norm_post', 'ple_norm_pre', 'w_ple_gate', 'w_ple_proj', 'ple_norm_post', 'loss_target', 'm_mix_norm_pre', 'm_w_in', 'm_lam_re', 'm_lam_im', 'm_log_dt', 'm_ssm_b_re', 'm_ssm_b_im', 'm_ssm_c_re', 'm_ssm_c_im', 'm_ssm_d', 'm_w_glu', 'm_b_glu', 'm_attn_out_norm', 'm_ssm_out_norm', 'm_w_out', 'm_mix_norm_post', 'm_mlp_norm_pre', 'm_w_up', 'm_w_down', 'm_mlp_norm_post', 'm_ple_norm_pre', 'm_w_ple_gate', 'm_w_ple_proj', 'm_ple_norm_post', 'v_mix_norm_pre', 'v_w_in', 'v_lam_re', 'v_lam_im', 'v_log_dt', 'v_ssm_b_re', 'v_ssm_b_im', 'v_ssm_c_re', 'v_ssm_c_im', 'v_ssm_d', 'v_w_glu', 'v_b_glu', 'v_attn_out_norm', 'v_ssm_out_norm', 'v_w_out', 'v_mix_norm_post', 'v_mlp_norm_pre', 'v_w_up', 'v_w_down', 'v_mlp_norm_post', 'v_ple_norm_pre', 'v_w_ple_gate', 'v_w_ple_proj', 'v_ple_norm_post']
TWIN_OUTPUTS = ['loss', 'grad_x', 'grad_mix_norm_pre', 'grad_w_in', 'grad_lam_re', 'grad_lam_im', 'grad_log_dt', 'grad_ssm_b_re', 'grad_ssm_b_im', 'grad_ssm_c_re', 'grad_ssm_c_im', 'grad_ssm_d', 'grad_w_glu', 'grad_b_glu', 'grad_attn_out_norm', 'grad_ssm_out_norm', 'grad_w_out', 'grad_mix_norm_post', 'grad_mlp_norm_pre', 'grad_w_up', 'grad_w_down', 'grad_mlp_norm_post', 'grad_ple_norm_pre', 'grad_w_ple_gate', 'grad_w_ple_proj', 'grad_ple_norm_post', 'delta_mix_norm_pre', 'delta_w_in', 'delta_lam_re', 'delta_lam_im', 'delta_log_dt', 'delta_ssm_b_re', 'delta_ssm_b_im', 'delta_ssm_c_re', 'delta_ssm_c_im', 'delta_ssm_d', 'delta_w_glu', 'delta_b_glu', 'delta_attn_out_norm', 'delta_ssm_out_norm', 'delta_w_out', 'delta_mix_norm_post', 'delta_mlp_norm_pre', 'delta_w_up', 'delta_w_down', 'delta_mlp_norm_post', 'delta_ple_norm_pre', 'delta_w_ple_gate', 'delta_w_ple_proj', 'delta_ple_norm_post', 'new_m_mix_norm_pre', 'new_m_w_in', 'new_m_lam_re', 'new_m_lam_im', 'new_m_log_dt', 'new_m_ssm_b_re', 'new_m_ssm_b_im', 'new_m_ssm_c_re', 'new_m_ssm_c_im', 'new_m_ssm_d', 'new_m_w_glu', 'new_m_b_glu', 'new_m_attn_out_norm', 'new_m_ssm_out_norm', 'new_m_w_out', 'new_m_mix_norm_post', 'new_m_mlp_norm_pre', 'new_m_w_up', 'new_m_w_down', 'new_m_mlp_norm_post', 'new_m_ple_norm_pre', 'new_m_w_ple_gate', 'new_m_w_ple_proj', 'new_m_ple_norm_post', 'new_v_mix_norm_pre', 'new_v_w_in', 'new_v_lam_re', 'new_v_lam_im', 'new_v_log_dt', 'new_v_ssm_b_re', 'new_v_ssm_b_im', 'new_v_ssm_c_re', 'new_v_ssm_c_im', 'new_v_ssm_d', 'new_v_w_glu', 'new_v_b_glu', 'new_v_attn_out_norm', 'new_v_ssm_out_norm', 'new_v_w_out', 'new_v_mix_norm_post', 'new_v_mlp_norm_pre', 'new_v_w_up', 'new_v_w_down', 'new_v_mlp_norm_post', 'new_v_ple_norm_pre', 'new_v_w_ple_gate', 'new_v_w_ple_proj', 'new_v_ple_norm_post']
TWIN_LEAF_KINDS = {'loss': 'loss', 'grad_x': 'grad_x', 'grad_mix_norm_pre': 'grad_w', 'grad_w_in': 'grad_w', 'grad_lam_re': 'grad_w', 'grad_lam_im': 'grad_w', 'grad_log_dt': 'grad_w', 'grad_ssm_b_re': 'grad_w', 'grad_ssm_b_im': 'grad_w', 'grad_ssm_c_re': 'grad_w', 'grad_ssm_c_im': 'grad_w', 'grad_ssm_d': 'grad_w', 'grad_w_glu': 'grad_w', 'grad_b_glu': 'grad_w', 'grad_attn_out_norm': 'grad_w', 'grad_ssm_out_norm': 'grad_w', 'grad_w_out': 'grad_w', 'grad_mix_norm_post': 'grad_w', 'grad_mlp_norm_pre': 'grad_w', 'grad_w_up': 'grad_w', 'grad_w_down': 'grad_w', 'grad_mlp_norm_post': 'grad_w', 'grad_ple_norm_pre': 'grad_w', 'grad_w_ple_gate': 'grad_w', 'grad_w_ple_proj': 'grad_w', 'grad_ple_norm_post': 'grad_w', 'delta_mix_norm_pre': 'delta_w', 'delta_w_in': 'delta_w', 'delta_lam_re': 'delta_w', 'delta_lam_im': 'delta_w', 'delta_log_dt': 'delta_w', 'delta_ssm_b_re': 'delta_w', 'delta_ssm_b_im': 'delta_w', 'delta_ssm_c_re': 'delta_w', 'delta_ssm_c_im': 'delta_w', 'delta_ssm_d': 'delta_w', 'delta_w_glu': 'delta_w', 'delta_b_glu': 'delta_w', 'delta_attn_out_norm': 'delta_w', 'delta_ssm_out_norm': 'delta_w', 'delta_w_out': 'delta_w', 'delta_mix_norm_post': 'delta_w', 'delta_mlp_norm_pre': 'delta_w', 'delta_w_up': 'delta_w', 'delta_w_down': 'delta_w', 'delta_mlp_norm_post': 'delta_w', 'delta_ple_norm_pre': 'delta_w', 'delta_w_ple_gate': 'delta_w', 'delta_w_ple_proj': 'delta_w', 'delta_ple_norm_post': 'delta_w', 'new_m_mix_norm_pre': 'new_m', 'new_m_w_in': 'new_m', 'new_m_lam_re': 'new_m', 'new_m_lam_im': 'new_m', 'new_m_log_dt': 'new_m', 'new_m_ssm_b_re': 'new_m', 'new_m_ssm_b_im': 'new_m', 'new_m_ssm_c_re': 'new_m', 'new_m_ssm_c_im': 'new_m', 'new_m_ssm_d': 'new_m', 'new_m_w_glu': 'new_m', 'new_m_b_glu': 'new_m', 'new_m_attn_out_norm': 'new_m', 'new_m_ssm_out_norm': 'new_m', 'new_m_w_out': 'new_m', 'new_m_mix_norm_post': 'new_m', 'new_m_mlp_norm_pre': 'new_m', 'new_m_w_up': 'new_m', 'new_m_w_down': 'new_m', 'new_m_mlp_norm_post': 'new_m', 'new_m_ple_norm_pre': 'new_m', 'new_m_w_ple_gate': 'new_m', 'new_m_w_ple_proj': 'new_m', 'new_m_ple_norm_post': 'new_m', 'new_v_mix_norm_pre': 'new_v', 'new_v_w_in': 'new_v', 'new_v_lam_re': 'new_v', 'new_v_lam_im': 'new_v', 'new_v_log_dt': 'new_v', 'new_v_ssm_b_re': 'new_v', 'new_v_ssm_b_im': 'new_v', 'new_v_ssm_c_re': 'new_v', 'new_v_ssm_c_im': 'new_v', 'new_v_ssm_d': 'new_v', 'new_v_w_glu': 'new_v', 'new_v_b_glu': 'new_v', 'new_v_attn_out_norm': 'new_v', 'new_v_ssm_out_norm': 'new_v', 'new_v_w_out': 'new_v', 'new_v_mix_norm_post': 'new_v', 'new_v_mlp_norm_pre': 'new_v', 'new_v_w_up': 'new_v', 'new_v_w_down': 'new_v', 'new_v_mlp_norm_post': 'new_v', 'new_v_ple_norm_pre': 'new_v', 'new_v_w_ple_gate': 'new_v', 'new_v_w_ple_proj': 'new_v', 'new_v_ple_norm_post': 'new_v'}


def _forward(args):
    return _fwd_reference(*[args[k] for k in FWD_PARAMS])


def _output_shape():
    def fwd():
        inp = _fwd_setup_inputs(0)
        return _fwd_reference(*[inp[k] for k in FWD_PARAMS])
    out = _jax.eval_shape(fwd)
    return out.shape, out.dtype

N_MICROBATCH = 1
ADAM_LR = 0.001
ADAM_B1 = 0.9
ADAM_B2 = 0.999
ADAM_EPS = 1e-08
ADAM_WD = 0.01
ADAM_STEP = 10
PER_EXAMPLE_BATCH_AXIS = {'x': 0, 'p': 1, 'loss_target': 0}
SHARED_INPUTS = []
_WEIGHT_DTYPES = {'mix_norm_pre': _jnp.float32, 'w_in': _jnp.float32, 'lam_re': _jnp.float32, 'lam_im': _jnp.float32, 'log_dt': _jnp.float32, 'ssm_b_re': _jnp.float32, 'ssm_b_im': _jnp.float32, 'ssm_c_re': _jnp.float32, 'ssm_c_im': _jnp.float32, 'ssm_d': _jnp.float32, 'w_glu': _jnp.float32, 'b_glu': _jnp.float32, 'attn_out_norm': _jnp.float32, 'ssm_out_norm': _jnp.float32, 'w_out': _jnp.float32, 'mix_norm_post': _jnp.float32, 'mlp_norm_pre': _jnp.float32, 'w_up': _jnp.float32, 'w_down': _jnp.float32, 'mlp_norm_post': _jnp.float32, 'ple_norm_pre': _jnp.float32, 'w_ple_gate': _jnp.float32, 'w_ple_proj': _jnp.float32, 'ple_norm_post': _jnp.float32}
MOMENT_SCALE = {'mix_norm_pre': 5.510919e-01, 'w_in': 3.529891e-01, 'lam_re': 1.649103e-02, 'lam_im': 1.287121e-02, 'log_dt': 8.196339e+00, 'ssm_b_re': 9.797173e-03, 'ssm_b_im': 9.794247e-03, 'ssm_c_re': 1.993904e-02, 'ssm_c_im': 1.928379e-02, 'ssm_d': 3.695440e+00, 'w_glu': 4.634277e-01, 'b_glu': 1.477913e+00, 'attn_out_norm': 5.582613e-01, 'ssm_out_norm': 3.300860e+00, 'w_out': 2.350228e+00, 'mix_norm_post': 1.625451e+01, 'mlp_norm_pre': 8.751525e-01, 'w_up': 4.294851e-01, 'w_down': 2.261810e+00, 'mlp_norm_post': 1.657878e+01, 'ple_norm_pre': 8.612148e-02, 'w_ple_gate': 8.506550e-02, 'w_ple_proj': 1.792941e-01, 'ple_norm_post': 1.642579e+01}


def _to_microbatches(a, axis):
    t = _jnp.moveaxis(a, axis, 0)
    t = t.reshape((N_MICROBATCH, t.shape[0] // N_MICROBATCH) + t.shape[1:])
    return _jnp.moveaxis(t, 1, axis + 1)


def setup_inputs(seed: int = 0) -> dict:
    inp = _fwd_setup_inputs(seed)
    key = _jax.random.fold_in(_jax.random.key(seed), 7919)
    shape, _ = _output_shape()
    out = dict(inp)
    out["loss_target"] = _jax.random.normal(_jax.random.fold_in(key, 0), shape, _jnp.float32)
    for i, name in enumerate(TWIN_WEIGHTS):
        w = inp[name].astype(_jnp.float32)
        if MOMENT_SCALE is None:
            s = _jnp.sqrt(_jnp.mean(_jnp.square(w)) + 1e-30)
        else:
            s = MOMENT_SCALE[name]
        km, kv = _jax.random.split(_jax.random.fold_in(key, i + 1))
        out[name] = w
        out["m_" + name] = s * _jax.random.normal(km, w.shape, _jnp.float32)
        out["v_" + name] = (s * s) * _jax.random.uniform(kv, w.shape, _jnp.float32, 0.5, 1.5)
    if N_MICROBATCH > 1:
        for name, axis in PER_EXAMPLE_BATCH_AXIS.items():
            out[name] = _to_microbatches(out[name], axis)
    return {'x': out['x'], 'p': out['p'], 'mix_norm_pre': out['mix_norm_pre'], 'w_in': out['w_in'], 'lam_re': out['lam_re'], 'lam_im': out['lam_im'], 'log_dt': out['log_dt'], 'ssm_b_re': out['ssm_b_re'], 'ssm_b_im': out['ssm_b_im'], 'ssm_c_re': out['ssm_c_re'], 'ssm_c_im': out['ssm_c_im'], 'ssm_d': out['ssm_d'], 'w_glu': out['w_glu'], 'b_glu': out['b_glu'], 'attn_out_norm': out['attn_out_norm'], 'ssm_out_norm': out['ssm_out_norm'], 'w_out': out['w_out'], 'mix_norm_post': out['mix_norm_post'], 'mlp_norm_pre': out['mlp_norm_pre'], 'w_up': out['w_up'], 'w_down': out['w_down'], 'mlp_norm_post': out['mlp_norm_post'], 'ple_norm_pre': out['ple_norm_pre'], 'w_ple_gate': out['w_ple_gate'], 'w_ple_proj': out['w_ple_proj'], 'ple_norm_post': out['ple_norm_post'], 'loss_target': out['loss_target'], 'm_mix_norm_pre': out['m_mix_norm_pre'], 'm_w_in': out['m_w_in'], 'm_lam_re': out['m_lam_re'], 'm_lam_im': out['m_lam_im'], 'm_log_dt': out['m_log_dt'], 'm_ssm_b_re': out['m_ssm_b_re'], 'm_ssm_b_im': out['m_ssm_b_im'], 'm_ssm_c_re': out['m_ssm_c_re'], 'm_ssm_c_im': out['m_ssm_c_im'], 'm_ssm_d': out['m_ssm_d'], 'm_w_glu': out['m_w_glu'], 'm_b_glu': out['m_b_glu'], 'm_attn_out_norm': out['m_attn_out_norm'], 'm_ssm_out_norm': out['m_ssm_out_norm'], 'm_w_out': out['m_w_out'], 'm_mix_norm_post': out['m_mix_norm_post'], 'm_mlp_norm_pre': out['m_mlp_norm_pre'], 'm_w_up': out['m_w_up'], 'm_w_down': out['m_w_down'], 'm_mlp_norm_post': out['m_mlp_norm_post'], 'm_ple_norm_pre': out['m_ple_norm_pre'], 'm_w_ple_gate': out['m_w_ple_gate'], 'm_w_ple_proj': out['m_w_ple_proj'], 'm_ple_norm_post': out['m_ple_norm_post'], 'v_mix_norm_pre': out['v_mix_norm_pre'], 'v_w_in': out['v_w_in'], 'v_lam_re': out['v_lam_re'], 'v_lam_im': out['v_lam_im'], 'v_log_dt': out['v_log_dt'], 'v_ssm_b_re': out['v_ssm_b_re'], 'v_ssm_b_im': out['v_ssm_b_im'], 'v_ssm_c_re': out['v_ssm_c_re'], 'v_ssm_c_im': out['v_ssm_c_im'], 'v_ssm_d': out['v_ssm_d'], 'v_w_glu': out['v_w_glu'], 'v_b_glu': out['v_b_glu'], 'v_attn_out_norm': out['v_attn_out_norm'], 'v_ssm_out_norm': out['v_ssm_out_norm'], 'v_w_out': out['v_w_out'], 'v_mix_norm_post': out['v_mix_norm_post'], 'v_mlp_norm_pre': out['v_mlp_norm_pre'], 'v_w_up': out['v_w_up'], 'v_w_down': out['v_w_down'], 'v_mlp_norm_post': out['v_mlp_norm_post'], 'v_ple_norm_pre': out['v_ple_norm_pre'], 'v_w_ple_gate': out['v_w_ple_gate'], 'v_w_ple_proj': out['v_w_ple_proj'], 'v_ple_norm_post': out['v_ple_norm_post']}


def _loss(weights, diff, rest, loss_target):
    with _jax.named_scope("forward"):
        args = {**rest, TWIN_DIFF_INPUT: diff, **{k: w.astype(_WEIGHT_DTYPES[k]) for k, w in weights.items()}}
        y = _forward(args)
    with _jax.named_scope("loss_head"):
        err = _jnp.square(y.astype(_jnp.float32) - loss_target)
        return 0.5 * _jnp.sum(_jnp.mean(err, axis=-1)) if err.ndim else 0.5 * err


def _adamw(w, g, m, v):
    m = ADAM_B1 * m + (1.0 - ADAM_B1) * g
    v = ADAM_B2 * v + (1.0 - ADAM_B2) * _jnp.square(g)
    m_hat = m / (1.0 - ADAM_B1 ** ADAM_STEP)
    v_hat = v / (1.0 - ADAM_B2 ** ADAM_STEP)
    delta = -ADAM_LR * (m_hat / (_jnp.sqrt(v_hat) + ADAM_EPS) + ADAM_WD * w)
    return delta, m, v


def reference(x, p, mix_norm_pre, w_in, lam_re, lam_im, log_dt, ssm_b_re, ssm_b_im, ssm_c_re, ssm_c_im, ssm_d, w_glu, b_glu, attn_out_norm, ssm_out_norm, w_out, mix_norm_post, mlp_norm_pre, w_up, w_down, mlp_norm_post, ple_norm_pre, w_ple_gate, w_ple_proj, ple_norm_post, loss_target, m_mix_norm_pre, m_w_in, m_lam_re, m_lam_im, m_log_dt, m_ssm_b_re, m_ssm_b_im, m_ssm_c_re, m_ssm_c_im, m_ssm_d, m_w_glu, m_b_glu, m_attn_out_norm, m_ssm_out_norm, m_w_out, m_mix_norm_post, m_mlp_norm_pre, m_w_up, m_w_down, m_mlp_norm_post, m_ple_norm_pre, m_w_ple_gate, m_w_ple_proj, m_ple_norm_post, v_mix_norm_pre, v_w_in, v_lam_re, v_lam_im, v_log_dt, v_ssm_b_re, v_ssm_b_im, v_ssm_c_re, v_ssm_c_im, v_ssm_d, v_w_glu, v_b_glu, v_attn_out_norm, v_ssm_out_norm, v_w_out, v_mix_norm_post, v_mlp_norm_pre, v_w_up, v_w_down, v_mlp_norm_post, v_ple_norm_pre, v_w_ple_gate, v_w_ple_proj, v_ple_norm_post):
    given = dict(x=x, p=p, mix_norm_pre=mix_norm_pre, w_in=w_in, lam_re=lam_re, lam_im=lam_im, log_dt=log_dt, ssm_b_re=ssm_b_re, ssm_b_im=ssm_b_im, ssm_c_re=ssm_c_re, ssm_c_im=ssm_c_im, ssm_d=ssm_d, w_glu=w_glu, b_glu=b_glu, attn_out_norm=attn_out_norm, ssm_out_norm=ssm_out_norm, w_out=w_out, mix_norm_post=mix_norm_post, mlp_norm_pre=mlp_norm_pre, w_up=w_up, w_down=w_down, mlp_norm_post=mlp_norm_post, ple_norm_pre=ple_norm_pre, w_ple_gate=w_ple_gate, w_ple_proj=w_ple_proj, ple_norm_post=ple_norm_post, loss_target=loss_target, m_mix_norm_pre=m_mix_norm_pre, m_w_in=m_w_in, m_lam_re=m_lam_re, m_lam_im=m_lam_im, m_log_dt=m_log_dt, m_ssm_b_re=m_ssm_b_re, m_ssm_b_im=m_ssm_b_im, m_ssm_c_re=m_ssm_c_re, m_ssm_c_im=m_ssm_c_im, m_ssm_d=m_ssm_d, m_w_glu=m_w_glu, m_b_glu=m_b_glu, m_attn_out_norm=m_attn_out_norm, m_ssm_out_norm=m_ssm_out_norm, m_w_out=m_w_out, m_mix_norm_post=m_mix_norm_post, m_mlp_norm_pre=m_mlp_norm_pre, m_w_up=m_w_up, m_w_down=m_w_down, m_mlp_norm_post=m_mlp_norm_post, m_ple_norm_pre=m_ple_norm_pre, m_w_ple_gate=m_w_ple_gate, m_w_ple_proj=m_w_ple_proj, m_ple_norm_post=m_ple_norm_post, v_mix_norm_pre=v_mix_norm_pre, v_w_in=v_w_in, v_lam_re=v_lam_re, v_lam_im=v_lam_im, v_log_dt=v_log_dt, v_ssm_b_re=v_ssm_b_re, v_ssm_b_im=v_ssm_b_im, v_ssm_c_re=v_ssm_c_re, v_ssm_c_im=v_ssm_c_im, v_ssm_d=v_ssm_d, v_w_glu=v_w_glu, v_b_glu=v_b_glu, v_attn_out_norm=v_attn_out_norm, v_ssm_out_norm=v_ssm_out_norm, v_w_out=v_w_out, v_mix_norm_post=v_mix_norm_post, v_mlp_norm_pre=v_mlp_norm_pre, v_w_up=v_w_up, v_w_down=v_w_down, v_mlp_norm_post=v_mlp_norm_post, v_ple_norm_pre=v_ple_norm_pre, v_w_ple_gate=v_w_ple_gate, v_w_ple_proj=v_w_ple_proj, v_ple_norm_post=v_ple_norm_post)
    weights = {n: given[n] for n in TWIN_WEIGHTS}
    shared = {n: given[n] for n in SHARED_INPUTS}
    per_example = {n: given[n] for n in ['x', 'p']}
    grad_fn = _jax.value_and_grad(_loss, argnums=(0, 1))

    def one_microbatch(ex, loss_target):
        ex = dict(ex)
        diff = ex.pop(TWIN_DIFF_INPUT)
        return grad_fn(weights, diff, {**shared, **ex}, loss_target)

    if N_MICROBATCH == 1:
        loss, (grad_w, grad_x) = one_microbatch(per_example, given["loss_target"])
    else:
        def body(carry, xs):
            loss_sum, grad_sum = carry
            l_k, (gw_k, gx_k) = one_microbatch(xs[0], xs[1])
            with _jax.named_scope("update"):
                return (loss_sum + l_k, _jax.tree.map(_jnp.add, grad_sum, gw_k)), gx_k

        init = (_jnp.zeros((), _jnp.float32), _jax.tree.map(_jnp.zeros_like, weights))
        (loss, grad_w), grad_x = _jax.lax.scan(body, init, (per_example, given["loss_target"]))
    with _jax.named_scope("update"):
        delta_w, new_m, new_v = {}, {}, {}
        for n in TWIN_WEIGHTS:
            delta_w[n], new_m[n], new_v[n] = _adamw(weights[n], grad_w[n], given["m_" + n], given["v_" + n])
    return (loss, grad_x, *[grad_w[n] for n in TWIN_WEIGHTS], *[delta_w[n] for n in TWIN_WEIGHTS],
            *[new_m[n] for n in TWIN_WEIGHTS], *[new_v[n] for n in TWIN_WEIGHTS])
```

```python
import functools
import math

import jax
import jax.numpy as jnp
from jax import lax
from jax.experimental import pallas as pl
from jax.experimental.pallas import tpu as pltpu

F32 = jnp.float32
BF16 = jnp.bfloat16
MESH = pl.DeviceIdType.MESH

RMS_EPS = 1e-6
NEG_INF = -1e30
HEAD_DIM = 128
BLK = 128
DILATIONS = (1, 4, 16)
SSM_C = 16
SSM_P = 64
LANES = 128
GROUPS_PER_BLOCK = LANES // SSM_C
STATE_LANES = GROUPS_PER_BLOCK * SSM_P
SSM_CHUNK = 128
ADAM_LR, ADAM_B1, ADAM_B2, ADAM_EPS, ADAM_WD, ADAM_STEP = 1e-3, 0.9, 0.999, 1e-8, 0.01, 10
VMEM_LIMIT_BYTES = 56 * 1024 * 1024
N_CHIPS = 4
N_DEV = 8
PACK_ROWS = 256


def _cparams(*sem):
    return pltpu.CompilerParams(dimension_semantics=sem or None, vmem_limit_bytes=VMEM_LIMIT_BYTES)


def _rows(tr, w):
    return pl.BlockSpec((tr, w), lambda i: (i, 0))


def _vec(w):
    return pl.BlockSpec((1, w), lambda i: (0, 0))


def _sigmoid(x):
    return 1.0 / (1.0 + jnp.exp(-x))


def _gelu(x):
    c = math.sqrt(2.0 / math.pi)
    return 0.5 * x * (1.0 + jnp.tanh(c * (x + 0.044715 * x * x * x)))


def _gelu_grad(x):
    c = math.sqrt(2.0 / math.pi)
    th = jnp.tanh(c * (x + 0.044715 * x * x * x))
    return 0.5 * (1.0 + th) + 0.5 * x * (1.0 - th * th) * c * (1.0 + 3.0 * 0.044715 * x * x)


def _rms(x, g):
    r = lax.rsqrt(jnp.mean(x * x, axis=-1, keepdims=True) + RMS_EPS)
    return x * r * g


def _rms_bwd(dy, x, g):
    r = lax.rsqrt(jnp.mean(x * x, axis=-1, keepdims=True) + RMS_EPS)
    n = x * r
    dn = dy * g
    dx = r * (dn - n * jnp.mean(dn * n, axis=-1, keepdims=True))
    return dx, dy * n


def _colsum(a):
    return jnp.sum(a, axis=0, keepdims=True)


def _first(i):
    return i == 0


def _matmul(a, b, *, name, ta=False, tb=False, out_dtype=F32, b_shards=1, out_shards=1, b_cols=None,
            tm=1024, tn=1024, tk=512):
    if ta:
        K, M = a.shape
    else:
        M, K = a.shape
    if b_shards > 1:
        rows, cols = b.shape[1], b.shape[2] * b_shards
    else:
        rows, cols = b.shape
    N, Kb = (rows, cols) if tb else (cols, rows)
    assert K == Kb, (a.shape, b.shape, ta, tb)
    col0 = 0
    if b_cols is not None:
        assert not tb
        col0, N = b_cols
    tm, tn, tk = min(tm, M), min(tn, N), min(tk, K)
    if b_shards > 1:
        shard_cols = cols // b_shards
        if tb:
            tk = min(tk, shard_cols)
        else:
            tn = min(tn, shard_cols)
    if out_shards > 1:
        tn = min(tn, N // out_shards)
    assert M % tm == 0 and N % tn == 0 and K % tk == 0 and col0 % tn == 0
    nk = K // tk
    j0 = col0 // tn

    a_spec = (pl.BlockSpec((tk, tm), lambda i, j, k: (k, i)) if ta
              else pl.BlockSpec((tm, tk), lambda i, j, k: (i, k)))
    if b_shards > 1:
        if tb:
            per = shard_cols // tk
            b_spec = pl.BlockSpec((None, tn, tk), lambda i, j, k: (k // per, j, k % per))
        else:
            per = shard_cols // tn
            b_spec = pl.BlockSpec((None, tk, tn), lambda i, j, k: ((j + j0) // per, k, (j + j0) % per))
    else:
        b_spec = (pl.BlockSpec((tn, tk), lambda i, j, k: (j, k)) if tb
                  else pl.BlockSpec((tk, tn), lambda i, j, k: (k, j + j0)))
    if out_shards > 1:
        per_o = (N // out_shards) // tn
        out_shape = jax.ShapeDtypeStruct((out_shards, M, N // out_shards), out_dtype)
        out_spec = pl.BlockSpec((None, tm, tn), lambda i, j, k: (j // per_o, i, j % per_o))
    else:
        out_shape = jax.ShapeDtypeStruct((M, N), out_dtype)
        out_spec = pl.BlockSpec((tm, tn), lambda i, j, k: (i, j))
    dims = (((0 if ta else 1,), (1 if tb else 0,)), ((), ()))

    def body(a_ref, b_ref, o_ref, acc_ref):
        k = pl.program_id(2)

        @pl.when(k == 0)
        def _():
            acc_ref[...] = jnp.zeros_like(acc_ref)

        acc_ref[...] += lax.dot_general(a_ref[...], b_ref[...], dims, preferred_element_type=F32)

        @pl.when(k == nk - 1)
        def _():
            o_ref[...] = acc_ref[...].astype(o_ref.dtype)

    return pl.pallas_call(
        body, name=name, out_shape=out_shape, grid=(M // tm, N // tn, nk),
        in_specs=[a_spec, b_spec], out_specs=out_spec,
        scratch_shapes=[pltpu.VMEM((tm, tn), F32)],
        compiler_params=_cparams("parallel", "parallel", "arbitrary"),
    )(a, b)


def _norm_cast(x, g, *, name, tr=256):
    S, D = x.shape
    tr = min(tr, S)

    def body(x_ref, g_ref, o_ref):
        o_ref[...] = _rms(x_ref[...], g_ref[...]).astype(BF16)

    return pl.pallas_call(
        body, name=name, out_shape=jax.ShapeDtypeStruct((S, D), BF16), grid=(S // tr,),
        in_specs=[_rows(tr, D), _vec(D)], out_specs=_rows(tr, D),
        compiler_params=_cparams("parallel"))(x, g)


def _res_norm(res, y, g_post, g_next, *, name, tr=256):
    S, D = res.shape
    tr = min(tr, S)

    def body(res_ref, y_ref, gp_ref, gn_ref, h_ref, hn_ref):
        h = res_ref[...] + _rms(y_ref[...], gp_ref[...])
        h_ref[...] = h
        hn_ref[...] = _rms(h, gn_ref[...]).astype(BF16)

    return pl.pallas_call(
        body, name=name,
        out_shape=(jax.ShapeDtypeStruct((S, D), F32), jax.ShapeDtypeStruct((S, D), BF16)),
        grid=(S // tr,), in_specs=[_rows(tr, D), _rows(tr, D), _vec(D), _vec(D)],
        out_specs=(_rows(tr, D), _rows(tr, D)), compiler_params=_cparams("parallel"))(res, y, g_post, g_next)


def _relu2(up, *, tr=128):
    S, F = up.shape
    tr = min(tr, S)

    def body(u_ref, o_ref):
        r = jnp.maximum(u_ref[...], 0.0)
        o_ref[...] = (r * r).astype(BF16)

    return pl.pallas_call(
        body, name="relu2", out_shape=jax.ShapeDtypeStruct((S, F), BF16), grid=(S // tr,),
        in_specs=[_rows(tr, F)], out_specs=_rows(tr, F), compiler_params=_cparams("parallel"))(up)


def _relu2_bwd(dact, up, *, tr=128):
    S, F = up.shape
    tr = min(tr, S)

    def body(d_ref, u_ref, o_ref):
        o_ref[...] = (d_ref[...] * (2.0 * jnp.maximum(u_ref[...], 0.0))).astype(BF16)

    return pl.pallas_call(
        body, name="relu2_bwd", out_shape=jax.ShapeDtypeStruct((S, F), BF16), grid=(S // tr,),
        in_specs=[_rows(tr, F), _rows(tr, F)], out_specs=_rows(tr, F),
        compiler_params=_cparams("parallel"))(dact, up)


def _gelu_cast(y1, *, tr=256):
    S, W = y1.shape
    tr = min(tr, S)

    def body(y_ref, o_ref):
        o_ref[...] = _gelu(y_ref[...]).astype(BF16)

    return pl.pallas_call(
        body, name="gelu_cast", out_shape=jax.ShapeDtypeStruct((S, W), BF16), grid=(S // tr,),
        in_specs=[_rows(tr, W)], out_specs=_rows(tr, W), compiler_params=_cparams("parallel"))(y1)


def _mix_fwd(os, ls, y1, z, b_glu, g_attn, g_ssm, *, tr=128):
    S, AW = os[0].shape
    SW = y1.shape[1]
    tr = min(tr, S)

    def body(o1, o2, o3, l1, l2, l3, y_ref, z_ref, b_ref, ga_ref, gs_ref, attn_ref, lse_ref, mixed_ref):
        la, lb, lc = l1[...], l2[...], l3[...]
        m = jnp.maximum(jnp.maximum(la, lb), lc)
        ea, eb, ec = jnp.exp(la - m), jnp.exp(lb - m), jnp.exp(lc - m)
        tot = ea + eb + ec
        attn = (ea * o1[...] + eb * o2[...] + ec * o3[...]) / tot
        attn_ref[...] = attn
        lse_ref[...] = m + jnp.log(tot)
        ssm = _gelu(y_ref[...]) * _sigmoid(z_ref[...] + b_ref[...])
        mixed_ref[:, :AW] = _rms(attn, ga_ref[...]).astype(BF16)
        mixed_ref[:, AW:] = _rms(ssm, gs_ref[...]).astype(BF16)

    return pl.pallas_call(
        body, name="mix_fwd",
        out_shape=(jax.ShapeDtypeStruct((S, AW), F32), jax.ShapeDtypeStruct((S, AW), F32),
                   jax.ShapeDtypeStruct((S, AW + SW), BF16)),
        grid=(S // tr,),
        in_specs=[_rows(tr, AW)] * 6 + [_rows(tr, SW), _rows(tr, SW), _vec(SW), _vec(AW), _vec(SW)],
        out_specs=(_rows(tr, AW), _rows(tr, AW), _rows(tr, AW + SW)),
        compiler_params=_cparams("parallel"))(*os, *ls, y1, z, b_glu, g_attn, g_ssm)


def _final(h2, gl, e, g_post, target, *, tr=128):
    S, D = h2.shape
    tr = min(tr, S)

    def body(h_ref, gl_ref, e_ref, g_ref, t_ref, dh_ref, dgl_ref, de_ref, loss_ref, dg_ref):
        i = pl.program_id(0)
        gate = _sigmoid(gl_ref[...])
        e_ = e_ref[...]
        ge = gate * e_
        g = g_ref[...]
        diff = h_ref[...] + _rms(ge, g) - t_ref[...]
        dh = diff * (1.0 / D)
        dh_ref[...] = dh
        dge, dgrow = _rms_bwd(dh, ge, g)
        dgl_ref[...] = (dge * e_ * gate * (1.0 - gate)).astype(BF16)
        de_ref[...] = (dge * gate).astype(BF16)
        part = _colsum(0.5 * jnp.mean(diff * diff, axis=-1, keepdims=True))

        @pl.when(_first(i))
        def _():
            loss_ref[...] = jnp.zeros_like(loss_ref)
            dg_ref[...] = jnp.zeros_like(dg_ref)

        loss_ref[...] += part + jnp.zeros((1, LANES), F32)
        dg_ref[...] += _colsum(dgrow)

    return pl.pallas_call(
        body, name="final_fwd_bwd",
        out_shape=(jax.ShapeDtypeStruct((S, D), F32), jax.ShapeDtypeStruct((S, D), BF16),
                   jax.ShapeDtypeStruct((S, D), BF16), jax.ShapeDtypeStruct((1, LANES), F32),
                   jax.ShapeDtypeStruct((1, D), F32)),
        grid=(S // tr,),
        in_specs=[_rows(tr, D), _rows(tr, D), _rows(tr, D), _vec(D), _rows(tr, D)],
        out_specs=(_rows(tr, D), _rows(tr, D), _rows(tr, D), _vec(LANES), _vec(D)),
        compiler_params=_cparams("arbitrary"))(h2, gl, e, g_post, target)


def _bwd_res_norm(dh_out, dhn, h, g_next, y, g_post, *, name, tr=128):
    S, D = h.shape
    tr = min(tr, S)

    def body(dho_ref, dhn_ref, h_ref, gn_ref, y_ref, gp_ref, dh_ref, dy_ref, dgn_ref, dgp_ref):
        i = pl.program_id(0)
        dx, dgn_rows = _rms_bwd(dhn_ref[...], h_ref[...], gn_ref[...])
        dh = dho_ref[...] + dx
        dh_ref[...] = dh
        dy, dgp_rows = _rms_bwd(dh, y_ref[...], gp_ref[...])
        dy_ref[...] = dy.astype(BF16)

        @pl.when(_first(i))
        def _():
            dgn_ref[...] = jnp.zeros_like(dgn_ref)
            dgp_ref[...] = jnp.zeros_like(dgp_ref)

        dgn_ref[...] += _colsum(dgn_rows)
        dgp_ref[...] += _colsum(dgp_rows)

    return pl.pallas_call(
        body, name=name,
        out_shape=(jax.ShapeDtypeStruct((S, D), F32), jax.ShapeDtypeStruct((S, D), BF16),
                   jax.ShapeDtypeStruct((1, D), F32), jax.ShapeDtypeStruct((1, D), F32)),
        grid=(S // tr,),
        in_specs=[_rows(tr, D), _rows(tr, D), _rows(tr, D), _vec(D), _rows(tr, D), _vec(D)],
        out_specs=(_rows(tr, D), _rows(tr, D), _vec(D), _vec(D)),
        compiler_params=_cparams("arbitrary"))(dh_out, dhn, h, g_next, y, g_post)


def _bwd_first(dh1, dhn1, x, g1, *, tr=256):
    S, D = x.shape
    tr = min(tr, S)

    def body(dh_ref, dhn_ref, x_ref, g_ref, dx_ref, dg_ref):
        i = pl.program_id(0)
        dx, dg_rows = _rms_bwd(dhn_ref[...], x_ref[...], g_ref[...])
        dx_ref[...] = dh_ref[...] + dx

        @pl.when(_first(i))
        def _():
            dg_ref[...] = jnp.zeros_like(dg_ref)

        dg_ref[...] += _colsum(dg_rows)

    return pl.pallas_call(
        body, name="bwd_first",
        out_shape=(jax.ShapeDtypeStruct((S, D), F32), jax.ShapeDtypeStruct((1, D), F32)),
        grid=(S // tr,), in_specs=[_rows(tr, D), _rows(tr, D), _rows(tr, D), _vec(D)],
        out_specs=(_rows(tr, D), _vec(D)), compiler_params=_cparams("arbitrary"))(dh1, dhn1, x, g1)


def _mix_bwd(dmixed, attn, y1, z, b_glu, g_attn, g_ssm, *, tr=128):
    S, AW = attn.shape
    SW = y1.shape[1]
    tr = min(tr, S)
    heads = AW // HEAD_DIM

    def body(dm_ref, a_ref, y_ref, z_ref, b_ref, ga_ref, gs_ref,
             da_ref, dd_ref, dz_ref, dy2_ref, dga_ref, dgs_ref, db_ref):
        i = pl.program_id(0)
        attn_ = a_ref[...]
        dattn, dga_rows = _rms_bwd(dm_ref[:, :AW], attn_, ga_ref[...])
        da_ref[...] = dattn.astype(BF16)
        prod = dattn * attn_
        for h in range(heads):
            sl = slice(h * HEAD_DIM, (h + 1) * HEAD_DIM)
            dd_ref[:, sl] = jnp.broadcast_to(jnp.sum(prod[:, sl], axis=-1, keepdims=True), (tr, HEAD_DIM))
        y2 = _gelu(y_ref[...])
        gate = _sigmoid(z_ref[...] + b_ref[...])
        dssm, dgs_rows = _rms_bwd(dm_ref[:, AW:], y2 * gate, gs_ref[...])
        dz = dssm * y2 * gate * (1.0 - gate)
        dz_ref[...] = dz.astype(BF16)
        dy2_ref[...] = dssm * gate

        @pl.when(_first(i))
        def _():
            dga_ref[...] = jnp.zeros_like(dga_ref)
            dgs_ref[...] = jnp.zeros_like(dgs_ref)
            db_ref[...] = jnp.zeros_like(db_ref)

        dga_ref[...] += _colsum(dga_rows)
        dgs_ref[...] += _colsum(dgs_rows)
        db_ref[...] += _colsum(dz)

    return pl.pallas_call(
        body, name="mix_bwd",
        out_shape=(jax.ShapeDtypeStruct((S, AW), BF16), jax.ShapeDtypeStruct((S, AW), F32),
                   jax.ShapeDtypeStruct((S, SW), BF16), jax.ShapeDtypeStruct((S, SW), F32),
                   jax.ShapeDtypeStruct((1, AW), F32), jax.ShapeDtypeStruct((1, SW), F32),
                   jax.ShapeDtypeStruct((1, SW), F32)),
        grid=(S // tr,),
        in_specs=[_rows(tr, AW + SW), _rows(tr, AW), _rows(tr, SW), _rows(tr, SW), _vec(SW), _vec(AW), _vec(SW)],
        out_specs=(_rows(tr, AW), _rows(tr, AW), _rows(tr, SW), _rows(tr, SW), _vec(AW), _vec(SW), _vec(SW)),
        compiler_params=_cparams("arbitrary"))(dmixed, attn, y1, z, b_glu, g_attn, g_ssm)


def _attn_masks(i):
    row = lax.broadcasted_iota(jnp.int32, (BLK, BLK), 0)
    col = lax.broadcasted_iota(jnp.int32, (BLK, BLK), 1)
    return col <= row, jnp.logical_and(col >= row, i > 0)


_NT = (((1,), (1,)), ((), ()))
_TN = (((0,), (0,)), ((), ()))


def _attn_in_specs(heads):
    def at(part, prev):
        def index(r, h, i):
            return (jnp.maximum(i - 1, 0) if prev else i, r * 3 * heads + part * heads + h)
        return pl.BlockSpec((BLK, HEAD_DIM), index)
    return [at(0, False), at(1, False), at(1, True), at(2, False), at(2, True)]


def _attn_fwd(qkv, d, heads):
    M = qkv.shape[0]
    nb = M // BLK
    scale = 1.0 / math.sqrt(HEAD_DIM)

    def body(q_ref, kc_ref, kp_ref, vc_ref, vp_ref, o_ref, l_ref):
        i = pl.program_id(2)
        q = q_ref[...]
        mc, mp = _attn_masks(i)
        sc = jnp.where(mc, lax.dot_general(q, kc_ref[...], _NT, preferred_element_type=F32) * scale, NEG_INF)
        sp = jnp.where(mp, lax.dot_general(q, kp_ref[...], _NT, preferred_element_type=F32) * scale, NEG_INF)
        m = jnp.maximum(jnp.max(sc, axis=-1, keepdims=True), jnp.max(sp, axis=-1, keepdims=True))
        pc, pp = jnp.exp(sc - m), jnp.exp(sp - m)
        tot = jnp.sum(pc, axis=-1, keepdims=True) + jnp.sum(pp, axis=-1, keepdims=True)
        acc = (jnp.dot(pc.astype(BF16), vc_ref[...], preferred_element_type=F32)
               + jnp.dot(pp.astype(BF16), vp_ref[...], preferred_element_type=F32))
        o_ref[...] = acc / tot
        l_ref[...] = jnp.broadcast_to(m + jnp.log(tot), (BLK, HEAD_DIM))

    out_spec = pl.BlockSpec((BLK, HEAD_DIM), lambda r, h, i: (i, r * heads + h))
    shape = jax.ShapeDtypeStruct((M, d * heads * HEAD_DIM), F32)
    return pl.pallas_call(
        body, name=f"attn_fwd_d{d}", out_shape=(shape, shape), grid=(d, heads, nb),
        in_specs=_attn_in_specs(heads), out_specs=(out_spec, out_spec),
        compiler_params=_cparams("parallel", "parallel", "parallel"))(qkv, qkv, qkv, qkv, qkv)


def _attn_bwd(qkv, dattn, lse, dd, d, heads):
    M = qkv.shape[0]
    nb = M // BLK
    scale = 1.0 / math.sqrt(HEAD_DIM)

    def body(q_ref, kc_ref, kp_ref, vc_ref, vp_ref, da_ref, l_ref, dd_ref,
             dq_ref, dkc_ref, dkp_ref, dvc_ref, dvp_ref):
        i = pl.program_id(2)
        q, kc, kp, vc, vp, da = q_ref[...], kc_ref[...], kp_ref[...], vc_ref[...], vp_ref[...], da_ref[...]
        mc, mp = _attn_masks(i)
        lse_, dd_ = l_ref[...], dd_ref[...]
        sc = lax.dot_general(q, kc, _NT, preferred_element_type=F32) * scale
        sp = lax.dot_general(q, kp, _NT, preferred_element_type=F32) * scale
        pc = jnp.where(mc, jnp.exp(jnp.where(mc, sc, NEG_INF) - lse_), 0.0)
        pp = jnp.where(mp, jnp.exp(jnp.where(mp, sp, NEG_INF) - lse_), 0.0)
        dsc = (pc * (lax.dot_general(da, vc, _NT, preferred_element_type=F32) - dd_) * scale).astype(BF16)
        dsp = (pp * (lax.dot_general(da, vp, _NT, preferred_element_type=F32) - dd_) * scale).astype(BF16)
        dq_ref[...] = (jnp.dot(dsc, kc, preferred_element_type=F32) + jnp.dot(dsp, kp, preferred_element_type=F32))
        dkc_ref[...] = lax.dot_general(dsc, q, _TN, preferred_element_type=F32)
        dkp_ref[...] = lax.dot_general(dsp, q, _TN, preferred_element_type=F32)
        dvc_ref[...] = lax.dot_general(pc.astype(BF16), da, _TN, preferred_element_type=F32)
        dvp_ref[...] = lax.dot_general(pp.astype(BF16), da, _TN, preferred_element_type=F32)

    blk = pl.BlockSpec((BLK, HEAD_DIM), lambda r, h, i: (i, r * heads + h))
    shape = jax.ShapeDtypeStruct((M, d * heads * HEAD_DIM), F32)
    return pl.pallas_call(
        body, name=f"attn_bwd_d{d}", out_shape=(shape,) * 5, grid=(d, heads, nb),
        in_specs=_attn_in_specs(heads) + [blk, blk, blk], out_specs=(blk,) * 5,
        compiler_params=_cparams("parallel", "parallel", "parallel"))(qkv, qkv, qkv, qkv, qkv, dattn, lse, dd)


def _dproj_join(dqs, dkcs, dkps, dvcs, dvps, du):
    S, AW = dqs[0].shape
    SW = du.shape[1]
    nblk = S // BLK

    def cur():
        return pl.BlockSpec((BLK, AW), lambda i: (i, 0))

    def ahead(d):
        return pl.BlockSpec((BLK, AW), lambda i: (jnp.minimum(i + d, nblk - 1), 0))

    def body(*refs):
        dq_r, dkc_r, dkp_r, dvc_r, dvp_r = (refs[3 * n:3 * n + 3] for n in range(5))
        du_ref, out_ref = refs[15], refs[16]
        i = pl.program_id(0)
        dq = dq_r[0][...] + dq_r[1][...] + dq_r[2][...]
        dk = dkc_r[0][...] + dkc_r[1][...] + dkc_r[2][...]
        dv = dvc_r[0][...] + dvc_r[1][...] + dvc_r[2][...]
        for n, d in enumerate(DILATIONS):
            live = i + d < nblk
            dk = dk + jnp.where(live, dkp_r[n][...], 0.0)
            dv = dv + jnp.where(live, dvp_r[n][...], 0.0)
        out_ref[:, :AW] = dq.astype(BF16)
        out_ref[:, AW:2 * AW] = dk.astype(BF16)
        out_ref[:, 2 * AW:3 * AW] = dv.astype(BF16)
        out_ref[:, 3 * AW:] = du_ref[...].astype(BF16)

    in_specs = ([cur()] * 3 + [cur()] * 3 + [ahead(d) for d in DILATIONS]
                + [cur()] * 3 + [ahead(d) for d in DILATIONS] + [pl.BlockSpec((BLK, SW), lambda i: (i, 0))])
    return pl.pallas_call(
        body, name="dproj_join", out_shape=jax.ShapeDtypeStruct((S, 3 * AW + SW), BF16), grid=(nblk,),
        in_specs=in_specs, out_specs=pl.BlockSpec((BLK, 3 * AW + SW), lambda i: (i, 0)),
        compiler_params=_cparams("parallel"))(*dqs, *dkcs, *dkps, *dvcs, *dvps, du)


def _ssm_disc(lr, li, ldt):
    dt = jnp.exp(ldt)
    mag = jnp.exp(lr * dt)
    ar = mag * jnp.cos(li * dt)
    ai = mag * jnp.sin(li * dt)
    nr = ar - 1.0
    den = lr * lr + li * li
    return ar, ai, (nr * lr + ai * li) / den, (ai * lr - nr * li) / den


def _ssm_power_table(lr, li, ldt, n):
    dt = jnp.exp(ldt)
    mag = jnp.exp(n * (lr * dt))
    ang = n * (li * dt)
    return mag * jnp.cos(ang), mag * jnp.sin(ang)


def _cmul(ar, ai, br, bi):
    return ar * br - ai * bi, ar * bi + ai * br


def _scan(xr, xi, ar, ai, reverse):
    T = xr.shape[0]
    row = lax.broadcasted_iota(jnp.int32, xr.shape, 0)
    sh = 1
    while sh < T:
        if reverse:
            keep = row < T - sh
            sr, si = pltpu.roll(xr, T - sh, 0), pltpu.roll(xi, T - sh, 0)
        else:
            keep = row >= sh
            sr, si = pltpu.roll(xr, sh, 0), pltpu.roll(xi, sh, 0)
        sr, si = jnp.where(keep, sr, 0.0), jnp.where(keep, si, 0.0)
        pr, pi = _cmul(ar, ai, sr, si)
        xr, xi = xr + pr, xi + pi
        ar, ai = _cmul(ar, ai, ar, ai)
        sh *= 2
    return xr, xi


def _ssm_specs(T, nch, rev):
    def t_of(c):
        return nch - 1 - c if rev else c
    tok = pl.BlockSpec((T, LANES), lambda j, c: (t_of(c), j))
    par = pl.BlockSpec((None, 1, STATE_LANES), lambda j, c: (j, 0, 0))
    bmat = pl.BlockSpec((None, LANES, STATE_LANES), lambda j, c: (j, 0, 0))
    cmat = pl.BlockSpec((None, STATE_LANES, LANES), lambda j, c: (j, 0, 0))
    dvec = pl.BlockSpec((1, LANES), lambda j, c: (0, j))
    return tok, par, bmat, cmat, dvec


def _ssm_fwd(u, lr_e, li_e, ldt_e, bre_e, bim_e, cre_e, cim_e, d_skip):
    S, SW = u.shape
    T = min(SSM_CHUNK, S)
    nch, nbk = S // T, SW // LANES
    tok, par, bmat, cmat, dvec = _ssm_specs(T, nch, False)
    carry_spec = pl.BlockSpec((None, 1, STATE_LANES), lambda j, c: (c, 0, j))

    def body(u_ref, lr_ref, li_ref, ldt_ref, bre_ref, bim_ref, cre_ref, cim_ref, d_ref,
             y_ref, er_ref, ei_ref, bbr, bbi, pwr, pwi, st_scr, cr, ci):
        c = pl.program_id(1)
        lr, li, ldt = lr_ref[...], li_ref[...], ldt_ref[...]
        ar, ai, kr, ki = _ssm_disc(lr, li, ldt)

        @pl.when(c == 0)
        def _():
            bbr[...] = (kr * bre_ref[...] - ki * bim_ref[...]).astype(BF16)
            bbi[...] = (kr * bim_ref[...] + ki * bre_ref[...]).astype(BF16)
            n = (lax.broadcasted_iota(jnp.int32, (T, 1), 0) + 1).astype(F32)
            pwr[...], pwi[...] = _ssm_power_table(lr, li, ldt, n)
            cr[...] = jnp.zeros_like(cr)
            ci[...] = jnp.zeros_like(ci)

        u_ = u_ref[...]
        ub = u_.astype(BF16)
        xr, xi = _scan(jnp.dot(ub, bbr[...], preferred_element_type=F32),
                       jnp.dot(ub, bbi[...], preferred_element_type=F32), ar, ai, False)
        er, ei = _cmul(pwr[...], pwi[...], cr[...], ci[...])
        sr, si = xr + er, xi + ei
        st_scr[0] = sr
        st_scr[1] = si
        cr[...] = st_scr[0, pl.ds(T - 1, 1), :]
        ci[...] = st_scr[1, pl.ds(T - 1, 1), :]
        er_ref[...] = cr[...]
        ei_ref[...] = ci[...]
        y0 = (jnp.dot(sr.astype(BF16), cre_ref[...].astype(BF16), preferred_element_type=F32)
              - jnp.dot(si.astype(BF16), cim_ref[...].astype(BF16), preferred_element_type=F32))
        y_ref[...] = y0 + d_ref[...] * u_

    ends = jax.ShapeDtypeStruct((nch, 1, nbk * STATE_LANES), F32)
    return pl.pallas_call(
        body, name="ssm_fwd", out_shape=(jax.ShapeDtypeStruct((S, SW), F32), ends, ends),
        grid=(nbk, nch), in_specs=[tok, par, par, par, bmat, bmat, cmat, cmat, dvec],
        out_specs=(tok, carry_spec, carry_spec),
        scratch_shapes=[pltpu.VMEM((LANES, STATE_LANES), BF16), pltpu.VMEM((LANES, STATE_LANES), BF16),
                        pltpu.VMEM((T, STATE_LANES), F32), pltpu.VMEM((T, STATE_LANES), F32),
                        pltpu.VMEM((2, T, STATE_LANES), F32),
                        pltpu.VMEM((1, STATE_LANES), F32), pltpu.VMEM((1, STATE_LANES), F32)],
        compiler_params=_cparams("arbitrary", "arbitrary"),
    )(u, lr_e, li_e, ldt_e, bre_e, bim_e, cre_e, cim_e, d_skip)


def _ssm_bwd(u, y1, dy2a, dy2b, ends_r, ends_i, lr_e, li_e, ldt_e, bre_e, bim_e, cre_e, cim_e, d_skip):
    S, SW = u.shape
    T = min(SSM_CHUNK, S)
    nch, nbk = S // T, SW // LANES
    tok, par, bmat, cmat, dvec = _ssm_specs(T, nch, True)
    prev_spec = pl.BlockSpec((None, 1, STATE_LANES), lambda j, c: (jnp.maximum(nch - 2 - c, 0), 0, j))
    acc8 = pl.BlockSpec((None, 8, STATE_LANES), lambda j, c: (j, 0, 0))
    dd8 = pl.BlockSpec((None, 8, LANES), lambda j, c: (j, 0, 0))

    def body(u_ref, y_ref, da_ref, db_ref, pr_ref, pi_ref, lr_ref, li_ref, ldt_ref,
             bre_ref, bim_ref, cre_ref, cim_ref, d_ref,
             du_ref, dar_ref, dai_ref, dcr_ref, dci_ref, dbr_ref, dbi_ref, ddk_ref,
             bbr, bbi, pwr, pwi, qwr, qwi, g_scr, gr0, gi0):
        c = pl.program_id(1)
        lr, li, ldt = lr_ref[...], li_ref[...], ldt_ref[...]
        ar, ai, kr, ki = _ssm_disc(lr, li, ldt)

        @pl.when(c == 0)
        def _():
            bbr[...] = (kr * bre_ref[...] - ki * bim_ref[...]).astype(BF16)
            bbi[...] = (kr * bim_ref[...] + ki * bre_ref[...]).astype(BF16)
            n = lax.broadcasted_iota(jnp.int32, (T, 1), 0)
            pwr[...], pwi[...] = _ssm_power_table(lr, li, ldt, (n + 1).astype(F32))
            qr, qi = _ssm_power_table(lr, li, ldt, (T - n).astype(F32))
            qwr[...] = qr
            qwi[...] = -qi
            gr0[...] = jnp.zeros_like(gr0)
            gi0[...] = jnp.zeros_like(gi0)
            for ref in (dar_ref, dai_ref, dcr_ref, dci_ref, dbr_ref, dbi_ref, ddk_ref):
                ref[...] = jnp.zeros_like(ref)

        u_ = u_ref[...]
        ub = u_.astype(BF16)
        dy1 = (da_ref[...] + db_ref[...]) * _gelu_grad(y_ref[...])
        dyb = dy1.astype(BF16)

        has_prev = c < nch - 1
        s0r = jnp.where(has_prev, pr_ref[...], 0.0)
        s0i = jnp.where(has_prev, pi_ref[...], 0.0)
        xr, xi = _scan(jnp.dot(ub, bbr[...], preferred_element_type=F32),
                       jnp.dot(ub, bbi[...], preferred_element_type=F32), ar, ai, False)
        er, ei = _cmul(pwr[...], pwi[...], s0r, s0i)
        sr, si = xr + er, xi + ei

        cre_b, cim_b = cre_ref[...].astype(BF16), cim_ref[...].astype(BF16)
        hr, hi = _scan(lax.dot_general(dyb, cre_b, _NT, preferred_element_type=F32),
                       -lax.dot_general(dyb, cim_b, _NT, preferred_element_type=F32), ar, -ai, True)
        fr, fi = _cmul(qwr[...], qwi[...], gr0[...], gi0[...])
        gr, gi = hr + fr, hi + fi
        g_scr[0] = gr
        g_scr[1] = gi
        gr0[...] = g_scr[0, pl.ds(0, 1), :]
        gi0[...] = g_scr[1, pl.ds(0, 1), :]

        row = lax.broadcasted_iota(jnp.int32, (T, STATE_LANES), 0)
        spr = jnp.where(row == 0, s0r, pltpu.roll(sr, 1, 0))
        spi = jnp.where(row == 0, s0i, pltpu.roll(si, 1, 0))

        def fold(a):
            return jnp.sum(a.reshape(T // 8, 8, a.shape[-1]), axis=0)

        dar_ref[...] += fold(gr * spr + gi * spi)
        dai_ref[...] += fold(gi * spr - gr * spi)
        srb, sib, grb, gib = sr.astype(BF16), si.astype(BF16), gr.astype(BF16), gi.astype(BF16)
        dcr_ref[...] += lax.dot_general(srb, dyb, _TN, preferred_element_type=F32)
        dci_ref[...] -= lax.dot_general(sib, dyb, _TN, preferred_element_type=F32)
        dbr_ref[...] += lax.dot_general(ub, grb, _TN, preferred_element_type=F32)
        dbi_ref[...] += lax.dot_general(ub, gib, _TN, preferred_element_type=F32)
        du_ref[...] = (lax.dot_general(grb, bbr[...], _NT, preferred_element_type=F32)
                       + lax.dot_general(gib, bbi[...], _NT, preferred_element_type=F32)
                       + dy1 * d_ref[...])
        ddk_ref[...] += fold(dy1 * u_)

    return pl.pallas_call(
        body, name="ssm_bwd",
        out_shape=(jax.ShapeDtypeStruct((S, SW), F32),
                   jax.ShapeDtypeStruct((nbk, 8, STATE_LANES), F32), jax.ShapeDtypeStruct((nbk, 8, STATE_LANES), F32),
                   jax.ShapeDtypeStruct((nbk, STATE_LANES, LANES), F32), jax.ShapeDtypeStruct((nbk, STATE_LANES, LANES), F32),
                   jax.ShapeDtypeStruct((nbk, LANES, STATE_LANES), F32), jax.ShapeDtypeStruct((nbk, LANES, STATE_LANES), F32),
                   jax.ShapeDtypeStruct((nbk, 8, LANES), F32)),
        grid=(nbk, nch),
        in_specs=[tok, tok, tok, tok, prev_spec, prev_spec, par, par, par, bmat, bmat, cmat, cmat, dvec],
        out_specs=(tok, acc8, acc8, cmat, cmat, bmat, bmat, dd8),
        scratch_shapes=[pltpu.VMEM((LANES, STATE_LANES), BF16), pltpu.VMEM((LANES, STATE_LANES), BF16),
                        pltpu.VMEM((T, STATE_LANES), F32), pltpu.VMEM((T, STATE_LANES), F32),
                        pltpu.VMEM((T, STATE_LANES), F32), pltpu.VMEM((T, STATE_LANES), F32),
                        pltpu.VMEM((2, T, STATE_LANES), F32),
                        pltpu.VMEM((1, STATE_LANES), F32), pltpu.VMEM((1, STATE_LANES), F32)],
        compiler_params=_cparams("arbitrary", "arbitrary"),
    )(u, y1, dy2a, dy2b, ends_r, ends_i, lr_e, li_e, ldt_e, bre_e, bim_e, cre_e, cim_e, d_skip)


def _ssm_param_bwd(dar8, dai8, dbr_e, dbi_e, lr_e, li_e, ldt_e, bre_e, bim_e):
    nbk = lr_e.shape[0]
    par = pl.BlockSpec((None, 1, STATE_LANES), lambda j: (j, 0, 0))
    acc8 = pl.BlockSpec((None, 8, STATE_LANES), lambda j: (j, 0, 0))
    bmat = pl.BlockSpec((None, LANES, STATE_LANES), lambda j: (j, 0, 0))

    def body(dar_ref, dai_ref, dbr_ref, dbi_ref, lr_ref, li_ref, ldt_ref, bre_ref, bim_ref,
             dlr_ref, dli_ref, dldt_ref, dbre_ref, dbim_ref):
        lr, li, ldt = lr_ref[...], li_ref[...], ldt_ref[...]
        (ar, ai, kr, ki), vjp = jax.vjp(_ssm_disc, lr, li, ldt)
        dbr, dbi, bre, bim = dbr_ref[...], dbi_ref[...], bre_ref[...], bim_ref[...]
        dbre_ref[...] = kr * dbr + ki * dbi
        dbim_ref[...] = kr * dbi - ki * dbr
        dkr = _colsum(dbr * bre + dbi * bim)
        dki = _colsum(dbi * bre - dbr * bim)
        dlr, dli, dldt = vjp((_colsum(dar_ref[...]), _colsum(dai_ref[...]), dkr, dki))
        dlr_ref[...] = dlr
        dli_ref[...] = dli
        tot = jnp.broadcast_to(dldt, (8, STATE_LANES))
        sh = 1
        while sh < SSM_P:
            tot = tot + pltpu.roll(tot, STATE_LANES - sh, 1)
            sh *= 2
        dldt_ref[...] = tot[:1]

    vec = jax.ShapeDtypeStruct((nbk, 1, STATE_LANES), F32)
    mat = jax.ShapeDtypeStruct((nbk, LANES, STATE_LANES), F32)
    return pl.pallas_call(
        body, name="ssm_param_bwd", out_shape=(vec, vec, vec, mat, mat), grid=(nbk,),
        in_specs=[acc8, acc8, bmat, bmat, par, par, par, bmat, bmat],
        out_specs=(par, par, par, bmat, bmat), compiler_params=_cparams("parallel"),
    )(dar8, dai8, dbr_e, dbi_e, lr_e, li_e, ldt_e, bre_e, bim_e)


def _expand_b(b):
    G = b.shape[0]
    bt = b.transpose(0, 2, 1).reshape(G // GROUPS_PER_BLOCK, GROUPS_PER_BLOCK, SSM_C, SSM_P)
    eye = jnp.eye(GROUPS_PER_BLOCK, dtype=b.dtype)
    return (bt[:, :, :, None, :] * eye[None, :, None, :, None]).reshape(G // GROUPS_PER_BLOCK, LANES, STATE_LANES)


def _collapse_b(be):
    nbk = be.shape[0]
    eye = jnp.eye(GROUPS_PER_BLOCK, dtype=be.dtype)
    d5 = be.reshape(nbk, GROUPS_PER_BLOCK, SSM_C, GROUPS_PER_BLOCK, SSM_P)
    d4 = (d5 * eye[None, :, None, :, None]).sum(axis=3)
    return d4.transpose(0, 1, 3, 2).reshape(nbk * GROUPS_PER_BLOCK, SSM_P, SSM_C)


def _expand_c(cm):
    G = cm.shape[0]
    ct = cm.transpose(0, 2, 1).reshape(G // GROUPS_PER_BLOCK, GROUPS_PER_BLOCK, SSM_P, SSM_C)
    eye = jnp.eye(GROUPS_PER_BLOCK, dtype=cm.dtype)
    return (ct[:, :, :, None, :] * eye[None, :, None, :, None]).reshape(G // GROUPS_PER_BLOCK, STATE_LANES, LANES)


def _collapse_c(ce):
    nbk = ce.shape[0]
    eye = jnp.eye(GROUPS_PER_BLOCK, dtype=ce.dtype)
    d5 = ce.reshape(nbk, GROUPS_PER_BLOCK, SSM_P, GROUPS_PER_BLOCK, SSM_C)
    d4 = (d5 * eye[None, :, None, :, None]).sum(axis=3)
    return d4.transpose(0, 1, 3, 2).reshape(nbk * GROUPS_PER_BLOCK, SSM_C, SSM_P)


def _place():
    x, y, c = lax.axis_index("x"), lax.axis_index("y"), lax.axis_index("c")
    return x, y, c


def _other_chips(x, y):
    return [(1 - x, y), (x, 1 - y), (1 - x, 1 - y)]


_ANY = pl.BlockSpec(memory_space=pl.ANY)


def _all_gather_weights(shards):
    n = len(shards)

    def body(*refs):
        ins, outs = refs[:n], refs[n:2 * n]
        send_sems, recv_sems, local_sems = refs[2 * n:]
        x, y, c = _place()
        me = 2 * x + y
        started = []
        for w in range(n):
            local = pltpu.make_async_copy(ins[w], outs[w].at[me], local_sems.at[w])
            local.start()
            started.append(local)
            for j, (px, py) in enumerate(_other_chips(x, y)):
                cp = pltpu.make_async_remote_copy(
                    src_ref=ins[w], dst_ref=outs[w].at[me], send_sem=send_sems.at[3 * w + j],
                    recv_sem=recv_sems.at[3 * w + j], device_id=(px, py, c), device_id_type=MESH)
                cp.start()
        for w in range(n):
            for j, (px, py) in enumerate(_other_chips(x, y)):
                arrival = pltpu.make_async_remote_copy(
                    src_ref=ins[w], dst_ref=outs[w].at[2 * px + py], send_sem=send_sems.at[3 * w + j],
                    recv_sem=recv_sems.at[3 * w + j], device_id=(px, py, c), device_id_type=MESH)
                arrival.wait_recv()
                arrival.wait_send()
        for local in started:
            local.wait()

    return pl.pallas_call(
        body, name="all_gather_weights",
        out_shape=[jax.ShapeDtypeStruct((N_CHIPS,) + s.shape, s.dtype) for s in shards],
        in_specs=[_ANY] * n, out_specs=[_ANY] * n,
        scratch_shapes=[pltpu.SemaphoreType.DMA((3 * n,)), pltpu.SemaphoreType.DMA((3 * n,)),
                        pltpu.SemaphoreType.DMA((n,))],
    )(*shards)


def _scatter_partials(parts):
    n = len(parts)

    def body(*refs):
        ins, outs = refs[:n], refs[n:2 * n]
        send_sems, recv_sems, local_sems = refs[2 * n:]
        x, y, c = _place()
        me = 2 * x + y
        started = []
        for w in range(n):
            local = pltpu.make_async_copy(ins[w].at[me], outs[w].at[me], local_sems.at[w])
            local.start()
            started.append(local)
            for j, (px, py) in enumerate(_other_chips(x, y)):
                cp = pltpu.make_async_remote_copy(
                    src_ref=ins[w].at[2 * px + py], dst_ref=outs[w].at[me], send_sem=send_sems.at[3 * w + j],
                    recv_sem=recv_sems.at[3 * w + j], device_id=(px, py, c), device_id_type=MESH)
                cp.start()
        for w in range(n):
            for j, (px, py) in enumerate(_other_chips(x, y)):
                arrival = pltpu.make_async_remote_copy(
                    src_ref=ins[w].at[2 * px + py], dst_ref=outs[w].at[2 * px + py], send_sem=send_sems.at[3 * w + j],
                    recv_sem=recv_sems.at[3 * w + j], device_id=(px, py, c), device_id_type=MESH)
                arrival.wait_recv()
                arrival.wait_send()
        for local in started:
            local.wait()

    return pl.pallas_call(
        body, name="scatter_partials",
        out_shape=[jax.ShapeDtypeStruct(p.shape, p.dtype) for p in parts],
        in_specs=[_ANY] * n, out_specs=[_ANY] * n,
        scratch_shapes=[pltpu.SemaphoreType.DMA((3 * n,)), pltpu.SemaphoreType.DMA((3 * n,)),
                        pltpu.SemaphoreType.DMA((n,))],
    )(*parts)


def _sum_partials(land, *, name, tr=256):
    _, R, C = land.shape
    tr = min(tr, R)

    def body(l_ref, o_ref):
        acc = l_ref[0].astype(F32)
        for k in range(1, N_CHIPS):
            acc = acc + l_ref[k].astype(F32)
        o_ref[...] = acc

    return pl.pallas_call(
        body, name=name, out_shape=jax.ShapeDtypeStruct((R, C), F32), grid=(R // tr,),
        in_specs=[pl.BlockSpec((N_CHIPS, tr, C), lambda i: (0, i, 0))], out_specs=_rows(tr, C),
        compiler_params=_cparams("parallel"))(land)


def _swap_with_sibling(sums):
    n = len(sums)

    def body(*refs):
        ins, outs = refs[:n], refs[n:2 * n]
        send_sems, recv_sems = refs[2 * n:]
        x, y, c = _place()
        copies = [pltpu.make_async_remote_copy(
            src_ref=ins[w], dst_ref=outs[w], send_sem=send_sems.at[w], recv_sem=recv_sems.at[w],
            device_id=(x, y, 1 - c), device_id_type=MESH) for w in range(n)]
        for cp in copies:
            cp.start()
        for cp in copies:
            cp.wait_recv()
            cp.wait_send()

    return pl.pallas_call(
        body, name="swap_with_sibling",
        out_shape=[jax.ShapeDtypeStruct(s.shape, s.dtype) for s in sums],
        in_specs=[_ANY] * n, out_specs=[_ANY] * n,
        scratch_shapes=[pltpu.SemaphoreType.DMA((n,)), pltpu.SemaphoreType.DMA((n,))],
    )(*sums)


def _adamw_math(w, g, m, v):
    m = ADAM_B1 * m + (1.0 - ADAM_B1) * g
    v = ADAM_B2 * v + (1.0 - ADAM_B2) * (g * g)
    m_hat = m / (1.0 - ADAM_B1 ** ADAM_STEP)
    v_hat = v / (1.0 - ADAM_B2 ** ADAM_STEP)
    delta = -ADAM_LR * (m_hat / (jnp.sqrt(v_hat) + ADAM_EPS) + ADAM_WD * w)
    return delta, m, v


def _adamw_pair(mine, theirs, w, m, v, *, name, tr=128):
    R, C = w.shape
    tr = min(tr, R)

    def body(a_ref, b_ref, w_ref, m_ref, v_ref, g_ref, d_ref, nm_ref, nv_ref):
        g = a_ref[...] + b_ref[...]
        g_ref[...] = g
        d_ref[...], nm_ref[...], nv_ref[...] = _adamw_math(w_ref[...], g, m_ref[...], v_ref[...])

    shape = jax.ShapeDtypeStruct((R, C), F32)
    return pl.pallas_call(
        body, name=name, out_shape=(shape,) * 4, grid=(R // tr,),
        in_specs=[_rows(tr, C)] * 5, out_specs=(_rows(tr, C),) * 4,
        compiler_params=_cparams("parallel"))(mine, theirs, w, m, v)


def _all_gather_small(packed):
    R = packed.shape[0]

    def body(x_ref, out_ref, send_sems, recv_sems, local_sem):
        x, y, c = _place()

        def slot(px, py, pc):
            return out_ref.at[4 * px + 2 * py + pc]

        local = pltpu.make_async_copy(x_ref, slot(x, y, c), local_sem)
        local.start()
        peers = [(x ^ (k >> 2), y ^ ((k >> 1) & 1), c ^ (k & 1)) for k in range(1, N_DEV)]
        for k, peer in enumerate(peers):
            pltpu.make_async_remote_copy(
                src_ref=x_ref, dst_ref=slot(x, y, c), send_sem=send_sems.at[k], recv_sem=recv_sems.at[k],
                device_id=peer, device_id_type=MESH).start()
        for k, peer in enumerate(peers):
            arrival = pltpu.make_async_remote_copy(
                src_ref=x_ref, dst_ref=slot(*peer), send_sem=send_sems.at[k], recv_sem=recv_sems.at[k],
                device_id=peer, device_id_type=MESH)
            arrival.wait_recv()
            arrival.wait_send()
        local.wait()

    vm = pl.BlockSpec(memory_space=pltpu.VMEM)
    return pl.pallas_call(
        body, name="all_gather_small", out_shape=jax.ShapeDtypeStruct((N_DEV, R, LANES), F32),
        in_specs=[vm], out_specs=vm,
        scratch_shapes=[pltpu.SemaphoreType.DMA((N_DEV - 1,)), pltpu.SemaphoreType.DMA((N_DEV - 1,)),
                        pltpu.SemaphoreType.DMA],
        compiler_params=pltpu.CompilerParams(vmem_limit_bytes=VMEM_LIMIT_BYTES),
    )(packed)


def _adamw_small(gathered, w, m, v):
    _, R, _ = gathered.shape
    tr = PACK_ROWS

    def body(gs_ref, w_ref, m_ref, v_ref, g_ref, d_ref, nm_ref, nv_ref):
        g = gs_ref[0]
        for k in range(1, N_DEV):
            g = g + gs_ref[k]
        g_ref[...] = g
        d_ref[...], nm_ref[...], nv_ref[...] = _adamw_math(w_ref[...], g, m_ref[...], v_ref[...])

    shape = jax.ShapeDtypeStruct((R, LANES), F32)
    return pl.pallas_call(
        body, name="adamw_small", out_shape=(shape,) * 4, grid=(R // tr,),
        in_specs=[pl.BlockSpec((N_DEV, tr, LANES), lambda i: (0, i, 0))] + [_rows(tr, LANES)] * 3,
        out_specs=(_rows(tr, LANES),) * 4, compiler_params=_cparams("parallel"))(gathered, w, m, v)


def _pack(arrays):
    parts, layout = [], []
    for a in arrays:
        n = a.size
        rows = -(-n // (8 * LANES)) * 8
        flat = jnp.pad(a.reshape(-1).astype(F32), (0, rows * LANES - n))
        parts.append(flat.reshape(rows, LANES))
        layout.append((rows, n, a.shape))
    total = sum(r for r, _, _ in layout)
    parts.append(jnp.zeros((-total % PACK_ROWS, LANES), F32))
    return jnp.concatenate(parts, axis=0), layout


def _unpack(buf, layout):
    out, r0 = [], 0
    for rows, n, shape in layout:
        out.append(buf[r0:r0 + rows].reshape(-1)[:n].reshape(shape))
        r0 += rows
    return out


SMALL = ("mix_norm_pre", "lam_re", "lam_im", "log_dt", "ssm_b_re", "ssm_b_im", "ssm_c_re", "ssm_c_im",
         "ssm_d", "b_glu", "attn_out_norm", "ssm_out_norm", "mix_norm_post", "mlp_norm_pre",
         "mlp_norm_post", "ple_norm_pre", "ple_norm_post")
BIG = ("w_in", "w_glu", "w_out", "w_up", "w_down", "w_ple_gate", "w_ple_proj")
WEIGHTS = ("mix_norm_pre", "w_in", "lam_re", "lam_im", "log_dt", "ssm_b_re", "ssm_b_im", "ssm_c_re",
           "ssm_c_im", "ssm_d", "w_glu", "b_glu", "attn_out_norm", "ssm_out_norm", "w_out",
           "mix_norm_post", "mlp_norm_pre", "w_up", "w_down", "mlp_norm_post", "ple_norm_pre",
           "w_ple_gate", "w_ple_proj", "ple_norm_post")


def _to_branch(a, d):
    return a if d == 1 else a.reshape(a.shape[0] // d, d * a.shape[1])


def _from_branch(a, d, S):
    return a if d == 1 else a.reshape(S, a.shape[1] // d)


def kernel(x, p, mix_norm_pre, w_in, lam_re, lam_im, log_dt, ssm_b_re, ssm_b_im, ssm_c_re, ssm_c_im, ssm_d, w_glu, b_glu, attn_out_norm, ssm_out_norm, w_out, mix_norm_post, mlp_norm_pre, w_up, w_down, mlp_norm_post, ple_norm_pre, w_ple_gate, w_ple_proj, ple_norm_post, loss_target, m_mix_norm_pre, m_w_in, m_lam_re, m_lam_im, m_log_dt, m_ssm_b_re, m_ssm_b_im, m_ssm_c_re, m_ssm_c_im, m_ssm_d, m_w_glu, m_b_glu, m_attn_out_norm, m_ssm_out_norm, m_w_out, m_mix_norm_post, m_mlp_norm_pre, m_w_up, m_w_down, m_mlp_norm_post, m_ple_norm_pre, m_w_ple_gate, m_w_ple_proj, m_ple_norm_post, v_mix_norm_pre, v_w_in, v_lam_re, v_lam_im, v_log_dt, v_ssm_b_re, v_ssm_b_im, v_ssm_c_re, v_ssm_c_im, v_ssm_d, v_w_glu, v_b_glu, v_attn_out_norm, v_ssm_out_norm, v_w_out, v_mix_norm_post, v_mlp_norm_pre, v_w_up, v_w_down, v_mlp_norm_post, v_ple_norm_pre, v_w_ple_gate, v_w_ple_proj, v_ple_norm_post):
    args = dict(locals())
    W = {n: args[n][0] for n in WEIGHTS}
    Mo = {n: args["m_" + n][0] for n in WEIGHTS}
    Vo = {n: args["v_" + n][0] for n in WEIGHTS}
    xs, ps, tgt = x[0], p[0, 0], loss_target[0]
    S, D = xs.shape
    SW = W["ssm_d"].shape[0]
    AW = W["attn_out_norm"].shape[0]
    heads = AW // HEAD_DIM
    G = SW // SSM_C
    nbk = SW // LANES
    assert W["w_in"].shape[1] * N_CHIPS == 3 * AW + SW and AW == SW

    row = lambda a: a.reshape(1, -1)

    gathered = _all_gather_weights([W[n].astype(BF16) for n in BIG])
    full = dict(zip(BIG, gathered))
    w_in_f = full["w_in"]
    w_up_f = full["w_up"]
    w_pp_f = full["w_ple_proj"]
    w_glu_f = full["w_glu"].reshape(SW, SW)
    w_out_f = full["w_out"].reshape(AW + SW, D)
    w_down_f = full["w_down"].reshape(-1, D)
    w_pg_f = full["w_ple_gate"].reshape(D, D)

    lr_e = W["lam_re"].reshape(nbk, 1, STATE_LANES)
    li_e = W["lam_im"].reshape(nbk, 1, STATE_LANES)
    ldt_e = jnp.repeat(W["log_dt"], SSM_P).reshape(nbk, 1, STATE_LANES)
    bre_e, bim_e = _expand_b(W["ssm_b_re"]), _expand_b(W["ssm_b_im"])
    cre_e, cim_e = _expand_c(W["ssm_c_re"]), _expand_c(W["ssm_c_im"])
    d_row = row(W["ssm_d"])

    hn1 = _norm_cast(xs, row(W["mix_norm_pre"]), name="norm_in")
    qkv = _matmul(hn1, w_in_f, name="proj_qkv", out_dtype=BF16, b_shards=N_CHIPS, b_cols=(0, 3 * AW))
    u = _matmul(hn1, w_in_f, name="proj_u", b_shards=N_CHIPS, b_cols=(3 * AW, SW))
    qkv_b = [_to_branch(qkv, d) for d in DILATIONS]
    outs, lses = [], []
    for d, qb in zip(DILATIONS, qkv_b):
        o, l = _attn_fwd(qb, d, heads)
        outs.append(_from_branch(o, d, S))
        lses.append(_from_branch(l, d, S))
    y1, ends_r, ends_i = _ssm_fwd(u, lr_e, li_e, ldt_e, bre_e, bim_e, cre_e, cim_e, d_row)
    y2b = _gelu_cast(y1)
    z = _matmul(y2b, w_glu_f, name="glu_z")
    attn, lse, mixed = _mix_fwd(outs, lses, y1, z, row(W["b_glu"]), row(W["attn_out_norm"]), row(W["ssm_out_norm"]))
    mo = _matmul(mixed, w_out_f, name="mix_out")
    h1, hn2 = _res_norm(xs, mo, row(W["mix_norm_post"]), row(W["mlp_norm_pre"]), name="res_mix")
    up = _matmul(hn2, w_up_f, name="mlp_up", b_shards=N_CHIPS)
    act = _relu2(up)
    ff = _matmul(act, w_down_f, name="mlp_down")
    h2, hn3 = _res_norm(h1, ff, row(W["mlp_norm_post"]), row(W["ple_norm_pre"]), name="res_mlp")
    gl = _matmul(hn3, w_pg_f, name="ple_gate")
    e = _matmul(ps.astype(BF16), w_pp_f, name="ple_proj", b_shards=N_CHIPS)

    dh3, dgl, de, loss_part, dg_ple_post = _final(h2, gl, e, row(W["ple_norm_post"]), tgt)
    gW = {}
    gW["w_ple_proj"] = _matmul(ps.astype(BF16), de, name="d_w_ple_proj", ta=True, out_dtype=BF16, out_shards=N_CHIPS)
    gW["w_ple_gate"] = _matmul(hn3, dgl, name="d_w_ple_gate", ta=True, out_dtype=BF16)
    dhn3 = _matmul(dgl, w_pg_f, name="d_hn3", tb=True)
    dh2, dff, dg_ple_pre, dg_mlp_post = _bwd_res_norm(
        dh3, dhn3, h2, row(W["ple_norm_pre"]), ff, row(W["mlp_norm_post"]), name="bwd_res_mlp")
    gW["w_down"] = _matmul(act, dff, name="d_w_down", ta=True, out_dtype=BF16)
    dact = _matmul(dff, w_down_f, name="d_act", tb=True)
    dup = _relu2_bwd(dact, up)
    gW["w_up"] = _matmul(hn2, dup, name="d_w_up", ta=True, out_dtype=BF16, out_shards=N_CHIPS)
    dhn2 = _matmul(dup, w_up_f, name="d_hn2", tb=True, b_shards=N_CHIPS)
    dh1, dmo, dg_mlp_pre, dg_mix_post = _bwd_res_norm(
        dh2, dhn2, h1, row(W["mlp_norm_pre"]), mo, row(W["mix_norm_post"]), name="bwd_res_mix")
    gW["w_out"] = _matmul(mixed, dmo, name="d_w_out", ta=True, out_dtype=BF16)
    dmixed = _matmul(dmo, w_out_f, name="d_mixed", tb=True)
    dattn, dd, dz, dy2a, dg_attn, dg_ssm, db_glu = _mix_bwd(
        dmixed, attn, y1, z, row(W["b_glu"]), row(W["attn_out_norm"]), row(W["ssm_out_norm"]))
    gW["w_glu"] = _matmul(y2b, dz, name="d_w_glu", ta=True, out_dtype=BF16)
    dy2b = _matmul(dz, w_glu_f, name="d_y2", tb=True)
    du, dar8, dai8, dcr_e, dci_e, dbr_e, dbi_e, dd8 = _ssm_bwd(
        u, y1, dy2a, dy2b, ends_r, ends_i, lr_e, li_e, ldt_e, bre_e, bim_e, cre_e, cim_e, d_row)
    dlr_e, dli_e, dldt_e, dbre_e, dbim_e = _ssm_param_bwd(dar8, dai8, dbr_e, dbi_e, lr_e, li_e, ldt_e, bre_e, bim_e)

    grads5 = [[], [], [], [], []]
    for d, qb in zip(DILATIONS, qkv_b):
        res = _attn_bwd(qb, _to_branch(dattn, d), _to_branch(lse, d), _to_branch(dd, d), d, heads)
        for lst, a in zip(grads5, res):
            lst.append(_from_branch(a, d, S))
    dproj = _dproj_join(*grads5, du)
    gW["w_in"] = _matmul(hn1, dproj, name="d_w_in", ta=True, out_dtype=BF16, out_shards=N_CHIPS)
    dhn1 = _matmul(dproj, w_in_f, name="d_hn1", tb=True, b_shards=N_CHIPS)
    grad_x, dg_mix_pre = _bwd_first(dh1, dhn1, xs, row(W["mix_norm_pre"]))

    def as_shards(name):
        g = gW[name]
        return g if g.ndim == 3 else g.reshape((N_CHIPS, g.shape[0] // N_CHIPS, g.shape[1]))

    landed = _scatter_partials([as_shards(n) for n in BIG])
    sums = [_sum_partials(l, name="sum_" + n) for n, l in zip(BIG, landed)]
    theirs = _swap_with_sibling(sums)
    out_g, out_d, out_m, out_v = {}, {}, {}, {}
    for n, a, b in zip(BIG, sums, theirs):
        out_g[n], out_d[n], out_m[n], out_v[n] = _adamw_pair(a, b, W[n], Mo[n], Vo[n], name="adamw_" + n)

    small_g = {
        "mix_norm_pre": dg_mix_pre, "lam_re": dlr_e.reshape(G, SSM_P), "lam_im": dli_e.reshape(G, SSM_P),
        "log_dt": dldt_e.reshape(G, SSM_P)[:, 0], "ssm_b_re": _collapse_b(dbre_e), "ssm_b_im": _collapse_b(dbim_e),
        "ssm_c_re": _collapse_c(dcr_e), "ssm_c_im": _collapse_c(dci_e), "ssm_d": dd8.sum(axis=1).reshape(-1),
        "b_glu": db_glu, "attn_out_norm": dg_attn, "ssm_out_norm": dg_ssm, "mix_norm_post": dg_mix_post,
        "mlp_norm_pre": dg_mlp_pre, "mlp_norm_post": dg_mlp_post, "ple_norm_pre": dg_ple_pre,
        "ple_norm_post": dg_ple_post,
    }
    g_pack, layout = _pack([small_g[n].reshape(W[n].shape) for n in SMALL])
    w_pack, _ = _pack([W[n] for n in SMALL])
    m_pack, _ = _pack([Mo[n] for n in SMALL])
    v_pack, _ = _pack([Vo[n] for n in SMALL])
    packed = _adamw_small(_all_gather_small(g_pack), w_pack, m_pack, v_pack)
    for dst, buf in zip((out_g, out_d, out_m, out_v), packed):
        dst.update(zip(SMALL, _unpack(buf, layout)))

    loss = lax.psum(loss_part[0, 0], ("x", "y", "c"))
    lead = lambda a: a[None]
    return (loss, grad_x[None],
            *[lead(out_g[n]) for n in WEIGHTS], *[lead(out_d[n]) for n in WEIGHTS],
            *[lead(out_m[n]) for n in WEIGHTS], *[lead(out_v[n]) for n in WEIGHTS])
```

```python
import functools
import math

import jax
import jax.numpy as jnp
from jax import lax
from jax.experimental import pallas as pl
from jax.experimental.pallas import tpu as pltpu

F32 = jnp.float32
BF16 = jnp.bfloat16
MESH = pl.DeviceIdType.MESH

RMS_EPS = 1e-6
NEG_INF = -1e30
HEAD_DIM = 128
BLK = 128
DILATIONS = (1, 4, 16)
SSM_C = 16
SSM_P = 64
LANES = 128
GROUPS_PER_BLOCK = LANES // SSM_C
STATE_LANES = GROUPS_PER_BLOCK * SSM_P
SSM_CHUNK = 128
ADAM_LR, ADAM_B1, ADAM_B2, ADAM_EPS, ADAM_WD, ADAM_STEP = 1e-3, 0.9, 0.999, 1e-8, 0.01, 10
VMEM_LIMIT_BYTES = 56 * 1024 * 1024
N_CHIPS = 4
N_DEV = 8
PACK_ROWS = 256


def _cparams(*sem):
    return pltpu.CompilerParams(dimension_semantics=sem or None, vmem_limit_bytes=VMEM_LIMIT_BYTES)


def _rows(tr, w):
    return pl.BlockSpec((tr, w), lambda i: (i, 0))


def _vec(w):
    return pl.BlockSpec((1, w), lambda i: (0, 0))


def _sigmoid(x):
    return 1.0 / (1.0 + jnp.exp(-x))


def _gelu(x):
    c = math.sqrt(2.0 / math.pi)
    return 0.5 * x * (1.0 + jnp.tanh(c * (x + 0.044715 * x * x * x)))


def _gelu_grad(x):
    c = math.sqrt(2.0 / math.pi)
    th = jnp.tanh(c * (x + 0.044715 * x * x * x))
    return 0.5 * (1.0 + th) + 0.5 * x * (1.0 - th * th) * c * (1.0 + 3.0 * 0.044715 * x * x)


def _rms(x, g):
    r = lax.rsqrt(jnp.mean(x * x, axis=-1, keepdims=True) + RMS_EPS)
    return x * r * g


def _rms_bwd(dy, x, g):
    r = lax.rsqrt(jnp.mean(x * x, axis=-1, keepdims=True) + RMS_EPS)
    n = x * r
    dn = dy * g
    dx = r * (dn - n * jnp.mean(dn * n, axis=-1, keepdims=True))
    return dx, dy * n


def _colsum(a):
    return jnp.sum(a, axis=0, keepdims=True)


def _first(i):
    return i == 0


def _matmul(a, b, *, name, ta=False, tb=False, out_dtype=F32, b_shards=1, out_shards=1, b_cols=None,
            after=None, tm=1024, tn=1024, tk=512):
    if ta:
        K, M = a.shape
    else:
        M, K = a.shape
    if b_shards > 1:
        rows, cols = b.shape[1], b.shape[2] * b_shards
    else:
        rows, cols = b.shape
    N, Kb = (rows, cols) if tb else (cols, rows)
    assert K == Kb, (a.shape, b.shape, ta, tb)
    col0 = 0
    if b_cols is not None:
        assert not tb
        col0, N = b_cols
    tm, tn, tk = min(tm, M), min(tn, N), min(tk, K)
    if b_shards > 1:
        shard_cols = cols // b_shards
        if tb:
            tk = min(tk, shard_cols)
        else:
            tn = min(tn, shard_cols)
    if out_shards > 1:
        tn = min(tn, N // out_shards)
    assert M % tm == 0 and N % tn == 0 and K % tk == 0 and col0 % tn == 0
    nk = K // tk
    j0 = col0 // tn

    a_spec = (pl.BlockSpec((tk, tm), lambda i, j, k: (k, i)) if ta
              else pl.BlockSpec((tm, tk), lambda i, j, k: (i, k)))
    if b_shards > 1:
        if tb:
            per = shard_cols // tk
            b_spec = pl.BlockSpec((None, tn, tk), lambda i, j, k: (k // per, j, k % per))
        else:
            per = shard_cols // tn
            b_spec = pl.BlockSpec((None, tk, tn), lambda i, j, k: ((j + j0) // per, k, (j + j0) % per))
    else:
        b_spec = (pl.BlockSpec((tn, tk), lambda i, j, k: (j, k)) if tb
                  else pl.BlockSpec((tk, tn), lambda i, j, k: (k, j + j0)))
    if out_shards > 1:
        per_o = (N // out_shards) // tn
        out_shape = jax.ShapeDtypeStruct((out_shards, M, N // out_shards), out_dtype)
        out_spec = pl.BlockSpec((None, tm, tn), lambda i, j, k: (j // per_o, i, j % per_o))
    else:
        out_shape = jax.ShapeDtypeStruct((M, N), out_dtype)
        out_spec = pl.BlockSpec((tm, tn), lambda i, j, k: (i, j))
    dims = (((0 if ta else 1,), (1 if tb else 0,)), ((), ()))

    def body(a_ref, b_ref, *rest):
        o_ref, acc_ref = rest[-2:]
        k = pl.program_id(2)

        @pl.when(k == 0)
        def _():
            acc_ref[...] = jnp.zeros_like(acc_ref)

        acc_ref[...] += lax.dot_general(a_ref[...], b_ref[...], dims, preferred_element_type=F32)

        @pl.when(k == nk - 1)
        def _():
            o_ref[...] = acc_ref[...].astype(o_ref.dtype)

    extra = [] if after is None else [after]
    extra_specs = [pl.BlockSpec(after.shape, lambda i, j, k: (0, 0)) for after in extra]
    return pl.pallas_call(
        body, name=name, out_shape=out_shape, grid=(M // tm, N // tn, nk),
        in_specs=[a_spec, b_spec] + extra_specs, out_specs=out_spec,
        scratch_shapes=[pltpu.VMEM((tm, tn), F32)],
        compiler_params=_cparams("parallel", "parallel", "arbitrary"),
    )(a, b, *extra)


def _norm_cast(x, g, *, name, tr=256):
    S, D = x.shape
    tr = min(tr, S)

    def body(x_ref, g_ref, o_ref):
        o_ref[...] = _rms(x_ref[...], g_ref[...]).astype(BF16)

    return pl.pallas_call(
        body, name=name, out_shape=jax.ShapeDtypeStruct((S, D), BF16), grid=(S // tr,),
        in_specs=[_rows(tr, D), _vec(D)], out_specs=_rows(tr, D),
        compiler_params=_cparams("parallel"))(x, g)


def _res_norm(res, y, g_post, g_next, *, name, tr=256):
    S, D = res.shape
    tr = min(tr, S)

    def body(res_ref, y_ref, gp_ref, gn_ref, h_ref, hn_ref):
        h = res_ref[...] + _rms(y_ref[...], gp_ref[...])
        h_ref[...] = h
        hn_ref[...] = _rms(h, gn_ref[...]).astype(BF16)

    return pl.pallas_call(
        body, name=name,
        out_shape=(jax.ShapeDtypeStruct((S, D), F32), jax.ShapeDtypeStruct((S, D), BF16)),
        grid=(S // tr,), in_specs=[_rows(tr, D), _rows(tr, D), _vec(D), _vec(D)],
        out_specs=(_rows(tr, D), _rows(tr, D)), compiler_params=_cparams("parallel"))(res, y, g_post, g_next)


def _relu2(up, *, tr=128):
    S, F = up.shape
    tr = min(tr, S)

    def body(u_ref, o_ref):
        r = jnp.maximum(u_ref[...], 0.0)
        o_ref[...] = (r * r).astype(BF16)

    return pl.pallas_call(
        body, name="relu2", out_shape=jax.ShapeDtypeStruct((S, F), BF16), grid=(S // tr,),
        in_specs=[_rows(tr, F)], out_specs=_rows(tr, F), compiler_params=_cparams("parallel"))(up)


def _relu2_bwd(dact, up, *, tr=128):
    S, F = up.shape
    tr = min(tr, S)

    def body(d_ref, u_ref, o_ref):
        o_ref[...] = (d_ref[...] * (2.0 * jnp.maximum(u_ref[...], 0.0))).astype(BF16)

    return pl.pallas_call(
        body, name="relu2_bwd", out_shape=jax.ShapeDtypeStruct((S, F), BF16), grid=(S // tr,),
        in_specs=[_rows(tr, F), _rows(tr, F)], out_specs=_rows(tr, F),
        compiler_params=_cparams("parallel"))(dact, up)


def _gelu_cast(y1, *, tr=256):
    S, W = y1.shape
    tr = min(tr, S)

    def body(y_ref, o_ref):
        o_ref[...] = _gelu(y_ref[...]).astype(BF16)

    return pl.pallas_call(
        body, name="gelu_cast", out_shape=jax.ShapeDtypeStruct((S, W), BF16), grid=(S // tr,),
        in_specs=[_rows(tr, W)], out_specs=_rows(tr, W), compiler_params=_cparams("parallel"))(y1)


def _mix_fwd(os, ls, y1, z, b_glu, g_attn, g_ssm, *, tr=128):
    S, AW = os[0].shape
    SW = y1.shape[1]
    tr = min(tr, S)

    def body(o1, o2, o3, l1, l2, l3, y_ref, z_ref, b_ref, ga_ref, gs_ref, attn_ref, lse_ref, mixed_ref):
        la, lb, lc = l1[...], l2[...], l3[...]
        m = jnp.maximum(jnp.maximum(la, lb), lc)
        ea, eb, ec = jnp.exp(la - m), jnp.exp(lb - m), jnp.exp(lc - m)
        tot = ea + eb + ec
        attn = (ea * o1[...] + eb * o2[...] + ec * o3[...]) / tot
        attn_ref[...] = attn
        lse_ref[...] = m + jnp.log(tot)
        ssm = _gelu(y_ref[...]) * _sigmoid(z_ref[...] + b_ref[...])
        mixed_ref[:, :AW] = _rms(attn, ga_ref[...]).astype(BF16)
        mixed_ref[:, AW:] = _rms(ssm, gs_ref[...]).astype(BF16)

    return pl.pallas_call(
        body, name="mix_fwd",
        out_shape=(jax.ShapeDtypeStruct((S, AW), F32), jax.ShapeDtypeStruct((S, AW), F32),
                   jax.ShapeDtypeStruct((S, AW + SW), BF16)),
        grid=(S // tr,),
        in_specs=[_rows(tr, AW)] * 6 + [_rows(tr, SW), _rows(tr, SW), _vec(SW), _vec(AW), _vec(SW)],
        out_specs=(_rows(tr, AW), _rows(tr, AW), _rows(tr, AW + SW)),
        compiler_params=_cparams("parallel"))(*os, *ls, y1, z, b_glu, g_attn, g_ssm)


def _final(h2, gl, e, g_post, target, *, tr=128):
    S, D = h2.shape
    tr = min(tr, S)

    def body(h_ref, gl_ref, e_ref, g_ref, t_ref, dh_ref, dgl_ref, de_ref, loss_ref, dg_ref):
        i = pl.program_id(0)
        gate = _sigmoid(gl_ref[...])
        e_ = e_ref[...]
        ge = gate * e_
        g = g_ref[...]
        diff = h_ref[...] + _rms(ge, g) - t_ref[...]
        dh = diff * (1.0 / D)
        dh_ref[...] = dh
        dge, dgrow = _rms_bwd(dh, ge, g)
        dgl_ref[...] = (dge * e_ * gate * (1.0 - gate)).astype(BF16)
        de_ref[...] = (dge * gate).astype(BF16)
        part = _colsum(0.5 * jnp.mean(diff * diff, axis=-1, keepdims=True))

        @pl.when(_first(i))
        def _():
            loss_ref[...] = jnp.zeros_like(loss_ref)
            dg_ref[...] = jnp.zeros_like(dg_ref)

        loss_ref[...] += part + jnp.zeros((1, LANES), F32)
        dg_ref[...] += _colsum(dgrow)

    return pl.pallas_call(
        body, name="final_fwd_bwd",
        out_shape=(jax.ShapeDtypeStruct((S, D), F32), jax.ShapeDtypeStruct((S, D), BF16),
                   jax.ShapeDtypeStruct((S, D), BF16), jax.ShapeDtypeStruct((1, LANES), F32),
                   jax.ShapeDtypeStruct((1, D), F32)),
        grid=(S // tr,),
        in_specs=[_rows(tr, D), _rows(tr, D), _rows(tr, D), _vec(D), _rows(tr, D)],
        out_specs=(_rows(tr, D), _rows(tr, D), _rows(tr, D), _vec(LANES), _vec(D)),
        compiler_params=_cparams("arbitrary"))(h2, gl, e, g_post, target)


def _bwd_res_norm(dh_out, dhn, h, g_next, y, g_post, *, name, tr=128):
    S, D = h.shape
    tr = min(tr, S)

    def body(dho_ref, dhn_ref, h_ref, gn_ref, y_ref, gp_ref, dh_ref, dy_ref, dgn_ref, dgp_ref):
        i = pl.program_id(0)
        dx, dgn_rows = _rms_bwd(dhn_ref[...], h_ref[...], gn_ref[...])
        dh = dho_ref[...] + dx
        dh_ref[...] = dh
        dy, dgp_rows = _rms_bwd(dh, y_ref[...], gp_ref[...])
        dy_ref[...] = dy.astype(BF16)

        @pl.when(_first(i))
        def _():
            dgn_ref[...] = jnp.zeros_like(dgn_ref)
            dgp_ref[...] = jnp.zeros_like(dgp_ref)

        dgn_ref[...] += _colsum(dgn_rows)
        dgp_ref[...] += _colsum(dgp_rows)

    return pl.pallas_call(
        body, name=name,
        out_shape=(jax.ShapeDtypeStruct((S, D), F32), jax.ShapeDtypeStruct((S, D), BF16),
                   jax.ShapeDtypeStruct((1, D), F32), jax.ShapeDtypeStruct((1, D), F32)),
        grid=(S // tr,),
        in_specs=[_rows(tr, D), _rows(tr, D), _rows(tr, D), _vec(D), _rows(tr, D), _vec(D)],
        out_specs=(_rows(tr, D), _rows(tr, D), _vec(D), _vec(D)),
        compiler_params=_cparams("arbitrary"))(dh_out, dhn, h, g_next, y, g_post)


def _bwd_first(dh1, dhn1, x, g1, *, tr=256):
    S, D = x.shape
    tr = min(tr, S)

    def body(dh_ref, dhn_ref, x_ref, g_ref, dx_ref, dg_ref):
        i = pl.program_id(0)
        dx, dg_rows = _rms_bwd(dhn_ref[...], x_ref[...], g_ref[...])
        dx_ref[...] = dh_ref[...] + dx

        @pl.when(_first(i))
        def _():
            dg_ref[...] = jnp.zeros_like(dg_ref)

        dg_ref[...] += _colsum(dg_rows)

    return pl.pallas_call(
        body, name="bwd_first",
        out_shape=(jax.ShapeDtypeStruct((S, D), F32), jax.ShapeDtypeStruct((1, D), F32)),
        grid=(S // tr,), in_specs=[_rows(tr, D), _rows(tr, D), _rows(tr, D), _vec(D)],
        out_specs=(_rows(tr, D), _vec(D)), compiler_params=_cparams("arbitrary"))(dh1, dhn1, x, g1)


def _mix_bwd(dmixed, attn, y1, z, b_glu, g_attn, g_ssm, *, tr=128):
    S, AW = attn.shape
    SW = y1.shape[1]
    tr = min(tr, S)
    heads = AW // HEAD_DIM

    def body(dm_ref, a_ref, y_ref, z_ref, b_ref, ga_ref, gs_ref,
             da_ref, dd_ref, dz_ref, dy2_ref, dga_ref, dgs_ref, db_ref):
        i = pl.program_id(0)
        attn_ = a_ref[...]
        dattn, dga_rows = _rms_bwd(dm_ref[:, :AW], attn_, ga_ref[...])
        da_ref[...] = dattn.astype(BF16)
        prod = dattn * attn_
        for h in range(heads):
            sl = slice(h * HEAD_DIM, (h + 1) * HEAD_DIM)
            dd_ref[:, sl] = jnp.broadcast_to(jnp.sum(prod[:, sl], axis=-1, keepdims=True), (tr, HEAD_DIM))
        y2 = _gelu(y_ref[...])
        gate = _sigmoid(z_ref[...] + b_ref[...])
        dssm, dgs_rows = _rms_bwd(dm_ref[:, AW:], y2 * gate, gs_ref[...])
        dz = dssm * y2 * gate * (1.0 - gate)
        dz_ref[...] = dz.astype(BF16)
        dy2_ref[...] = dssm * gate

        @pl.when(_first(i))
        def _():
            dga_ref[...] = jnp.zeros_like(dga_ref)
            dgs_ref[...] = jnp.zeros_like(dgs_ref)
            db_ref[...] = jnp.zeros_like(db_ref)

        dga_ref[...] += _colsum(dga_rows)
        dgs_ref[...] += _colsum(dgs_rows)
        db_ref[...] += _colsum(dz)

    return pl.pallas_call(
        body, name="mix_bwd",
        out_shape=(jax.ShapeDtypeStruct((S, AW), BF16), jax.ShapeDtypeStruct((S, AW), F32),
                   jax.ShapeDtypeStruct((S, SW), BF16), jax.ShapeDtypeStruct((S, SW), F32),
                   jax.ShapeDtypeStruct((1, AW), F32), jax.ShapeDtypeStruct((1, SW), F32),
                   jax.ShapeDtypeStruct((1, SW), F32)),
        grid=(S // tr,),
        in_specs=[_rows(tr, AW + SW), _rows(tr, AW), _rows(tr, SW), _rows(tr, SW), _vec(SW), _vec(AW), _vec(SW)],
        out_specs=(_rows(tr, AW), _rows(tr, AW), _rows(tr, SW), _rows(tr, SW), _vec(AW), _vec(SW), _vec(SW)),
        compiler_params=_cparams("arbitrary"))(dmixed, attn, y1, z, b_glu, g_attn, g_ssm)


def _attn_masks(i):
    row = lax.broadcasted_iota(jnp.int32, (BLK, BLK), 0)
    col = lax.broadcasted_iota(jnp.int32, (BLK, BLK), 1)
    return col <= row, jnp.logical_and(col >= row, i > 0)


_NT = (((1,), (1,)), ((), ()))
_TN = (((0,), (0,)), ((), ()))


def _attn_in_specs(heads):
    def at(part, prev):
        def index(r, h, i):
            return (jnp.maximum(i - 1, 0) if prev else i, r * 3 * heads + part * heads + h)
        return pl.BlockSpec((BLK, HEAD_DIM), index)
    return [at(0, False), at(1, False), at(1, True), at(2, False), at(2, True)]


def _attn_fwd(qkv, d, heads):
    M = qkv.shape[0]
    nb = M // BLK
    scale = 1.0 / math.sqrt(HEAD_DIM)

    def body(q_ref, kc_ref, kp_ref, vc_ref, vp_ref, o_ref, l_ref):
        i = pl.program_id(2)
        q = q_ref[...]
        mc, mp = _attn_masks(i)
        sc = jnp.where(mc, lax.dot_general(q, kc_ref[...], _NT, preferred_element_type=F32) * scale, NEG_INF)
        sp = jnp.where(mp, lax.dot_general(q, kp_ref[...], _NT, preferred_element_type=F32) * scale, NEG_INF)
        m = jnp.maximum(jnp.max(sc, axis=-1, keepdims=True), jnp.max(sp, axis=-1, keepdims=True))
        pc, pp = jnp.exp(sc - m), jnp.exp(sp - m)
        tot = jnp.sum(pc, axis=-1, keepdims=True) + jnp.sum(pp, axis=-1, keepdims=True)
        acc = (jnp.dot(pc.astype(BF16), vc_ref[...], preferred_element_type=F32)
               + jnp.dot(pp.astype(BF16), vp_ref[...], preferred_element_type=F32))
        o_ref[...] = acc / tot
        l_ref[...] = jnp.broadcast_to(m + jnp.log(tot), (BLK, HEAD_DIM))

    out_spec = pl.BlockSpec((BLK, HEAD_DIM), lambda r, h, i: (i, r * heads + h))
    shape = jax.ShapeDtypeStruct((M, d * heads * HEAD_DIM), F32)
    return pl.pallas_call(
        body, name=f"attn_fwd_d{d}", out_shape=(shape, shape), grid=(d, heads, nb),
        in_specs=_attn_in_specs(heads), out_specs=(out_spec, out_spec),
        compiler_params=_cparams("parallel", "parallel", "parallel"))(qkv, qkv, qkv, qkv, qkv)


def _attn_bwd(qkv, dattn, lse, dd, d, heads):
    M = qkv.shape[0]
    nb = M // BLK
    scale = 1.0 / math.sqrt(HEAD_DIM)

    def body(q_ref, kc_ref, kp_ref, vc_ref, vp_ref, da_ref, l_ref, dd_ref,
             dq_ref, dkc_ref, dkp_ref, dvc_ref, dvp_ref):
        i = pl.program_id(2)
        q, kc, kp, vc, vp, da = q_ref[...], kc_ref[...], kp_ref[...], vc_ref[...], vp_ref[...], da_ref[...]
        mc, mp = _attn_masks(i)
        lse_, dd_ = l_ref[...], dd_ref[...]
        sc = lax.dot_general(q, kc, _NT, preferred_element_type=F32) * scale
        sp = lax.dot_general(q, kp, _NT, preferred_element_type=F32) * scale
        pc = jnp.where(mc, jnp.exp(jnp.where(mc, sc, NEG_INF) - lse_), 0.0)
        pp = jnp.where(mp, jnp.exp(jnp.where(mp, sp, NEG_INF) - lse_), 0.0)
        dsc = (pc * (lax.dot_general(da, vc, _NT, preferred_element_type=F32) - dd_) * scale).astype(BF16)
        dsp = (pp * (lax.dot_general(da, vp, _NT, preferred_element_type=F32) - dd_) * scale).astype(BF16)
        dq_ref[...] = (jnp.dot(dsc, kc, preferred_element_type=F32) + jnp.dot(dsp, kp, preferred_element_type=F32))
        dkc_ref[...] = lax.dot_general(dsc, q, _TN, preferred_element_type=F32)
        dkp_ref[...] = lax.dot_general(dsp, q, _TN, preferred_element_type=F32)
        dvc_ref[...] = lax.dot_general(pc.astype(BF16), da, _TN, preferred_element_type=F32)
        dvp_ref[...] = lax.dot_general(pp.astype(BF16), da, _TN, preferred_element_type=F32)

    blk = pl.BlockSpec((BLK, HEAD_DIM), lambda r, h, i: (i, r * heads + h))
    shape = jax.ShapeDtypeStruct((M, d * heads * HEAD_DIM), F32)
    return pl.pallas_call(
        body, name=f"attn_bwd_d{d}", out_shape=(shape,) * 5, grid=(d, heads, nb),
        in_specs=_attn_in_specs(heads) + [blk, blk, blk], out_specs=(blk,) * 5,
        compiler_params=_cparams("parallel", "parallel", "parallel"))(qkv, qkv, qkv, qkv, qkv, dattn, lse, dd)


def _dproj_join(dqs, dkcs, dkps, dvcs, dvps, du):
    S, AW = dqs[0].shape
    SW = du.shape[1]
    nblk = S // BLK

    def cur():
        return pl.BlockSpec((BLK, AW), lambda i: (i, 0))

    def ahead(d):
        return pl.BlockSpec((BLK, AW), lambda i: (jnp.minimum(i + d, nblk - 1), 0))

    def body(*refs):
        dq_r, dkc_r, dkp_r, dvc_r, dvp_r = (refs[3 * n:3 * n + 3] for n in range(5))
        du_ref, out_ref = refs[15], refs[16]
        i = pl.program_id(0)
        dq = dq_r[0][...] + dq_r[1][...] + dq_r[2][...]
        dk = dkc_r[0][...] + dkc_r[1][...] + dkc_r[2][...]
        dv = dvc_r[0][...] + dvc_r[1][...] + dvc_r[2][...]
        for n, d in enumerate(DILATIONS):
            live = i + d < nblk
            dk = dk + jnp.where(live, dkp_r[n][...], 0.0)
            dv = dv + jnp.where(live, dvp_r[n][...], 0.0)
        out_ref[:, :AW] = dq.astype(BF16)
        out_ref[:, AW:2 * AW] = dk.astype(BF16)
        out_ref[:, 2 * AW:3 * AW] = dv.astype(BF16)
        out_ref[:, 3 * AW:] = du_ref[...].astype(BF16)

    in_specs = ([cur()] * 3 + [cur()] * 3 + [ahead(d) for d in DILATIONS]
                + [cur()] * 3 + [ahead(d) for d in DILATIONS] + [pl.BlockSpec((BLK, SW), lambda i: (i, 0))])
    return pl.pallas_call(
        body, name="dproj_join", out_shape=jax.ShapeDtypeStruct((S, 3 * AW + SW), BF16), grid=(nblk,),
        in_specs=in_specs, out_specs=pl.BlockSpec((BLK, 3 * AW + SW), lambda i: (i, 0)),
        compiler_params=_cparams("parallel"))(*dqs, *dkcs, *dkps, *dvcs, *dvps, du)


def _ssm_disc(lr, li, ldt):
    dt = jnp.exp(ldt)
    mag = jnp.exp(lr * dt)
    ar = mag * jnp.cos(li * dt)
    ai = mag * jnp.sin(li * dt)
    nr = ar - 1.0
    den = lr * lr + li * li
    return ar, ai, (nr * lr + ai * li) / den, (ai * lr - nr * li) / den


def _ssm_power_table(lr, li, ldt, n):
    dt = jnp.exp(ldt)
    mag = jnp.exp(n * (lr * dt))
    ang = n * (li * dt)
    return mag * jnp.cos(ang), mag * jnp.sin(ang)


def _cmul(ar, ai, br, bi):
    return ar * br - ai * bi, ar * bi + ai * br


def _scan(xr, xi, ar, ai, reverse):
    T = xr.shape[0]
    row = lax.broadcasted_iota(jnp.int32, xr.shape, 0)
    sh = 1
    while sh < T:
        if reverse:
            keep = row < T - sh
            sr, si = pltpu.roll(xr, T - sh, 0), pltpu.roll(xi, T - sh, 0)
        else:
            keep = row >= sh
            sr, si = pltpu.roll(xr, sh, 0), pltpu.roll(xi, sh, 0)
        sr, si = jnp.where(keep, sr, 0.0), jnp.where(keep, si, 0.0)
        pr, pi = _cmul(ar, ai, sr, si)
        xr, xi = xr + pr, xi + pi
        ar, ai = _cmul(ar, ai, ar, ai)
        sh *= 2
    return xr, xi


def _ssm_specs(T, nch, rev):
    def t_of(c):
        return nch - 1 - c if rev else c
    tok = pl.BlockSpec((T, LANES), lambda j, c: (t_of(c), j))
    par = pl.BlockSpec((None, 1, STATE_LANES), lambda j, c: (j, 0, 0))
    bmat = pl.BlockSpec((None, LANES, STATE_LANES), lambda j, c: (j, 0, 0))
    cmat = pl.BlockSpec((None, STATE_LANES, LANES), lambda j, c: (j, 0, 0))
    dvec = pl.BlockSpec((1, LANES), lambda j, c: (0, j))
    return tok, par, bmat, cmat, dvec


def _ssm_fwd(u, lr_e, li_e, ldt_e, bre_e, bim_e, cre_e, cim_e, d_skip):
    S, SW = u.shape
    T = min(SSM_CHUNK, S)
    nch, nbk = S // T, SW // LANES
    tok, par, bmat, cmat, dvec = _ssm_specs(T, nch, False)
    carry_spec = pl.BlockSpec((None, 1, STATE_LANES), lambda j, c: (c, 0, j))

    def body(u_ref, lr_ref, li_ref, ldt_ref, bre_ref, bim_ref, cre_ref, cim_ref, d_ref,
             y_ref, er_ref, ei_ref, bbr, bbi, pwr, pwi, st_scr, cr, ci):
        c = pl.program_id(1)
        lr, li, ldt = lr_ref[...], li_ref[...], ldt_ref[...]
        ar, ai, kr, ki = _ssm_disc(lr, li, ldt)

        @pl.when(c == 0)
        def _():
            bbr[...] = (kr * bre_ref[...] - ki * bim_ref[...]).astype(BF16)
            bbi[...] = (kr * bim_ref[...] + ki * bre_ref[...]).astype(BF16)
            n = (lax.broadcasted_iota(jnp.int32, (T, 1), 0) + 1).astype(F32)
            pwr[...], pwi[...] = _ssm_power_table(lr, li, ldt, n)
            cr[...] = jnp.zeros_like(cr)
            ci[...] = jnp.zeros_like(ci)

        u_ = u_ref[...]
        ub = u_.astype(BF16)
        xr, xi = _scan(jnp.dot(ub, bbr[...], preferred_element_type=F32),
                       jnp.dot(ub, bbi[...], preferred_element_type=F32), ar, ai, False)
        er, ei = _cmul(pwr[...], pwi[...], cr[...], ci[...])
        sr, si = xr + er, xi + ei
        st_scr[0] = sr
        st_scr[1] = si
        cr[...] = st_scr[0, pl.ds(T - 1, 1), :]
        ci[...] = st_scr[1, pl.ds(T - 1, 1), :]
        er_ref[...] = cr[...]
        ei_ref[...] = ci[...]
        y0 = (jnp.dot(sr.astype(BF16), cre_ref[...].astype(BF16), preferred_element_type=F32)
              - jnp.dot(si.astype(BF16), cim_ref[...].astype(BF16), preferred_element_type=F32))
        y_ref[...] = y0 + d_ref[...] * u_

    ends = jax.ShapeDtypeStruct((nch, 1, nbk * STATE_LANES), F32)
    return pl.pallas_call(
        body, name="ssm_fwd", out_shape=(jax.ShapeDtypeStruct((S, SW), F32), ends, ends),
        grid=(nbk, nch), in_specs=[tok, par, par, par, bmat, bmat, cmat, cmat, dvec],
        out_specs=(tok, carry_spec, carry_spec),
        scratch_shapes=[pltpu.VMEM((LANES, STATE_LANES), BF16), pltpu.VMEM((LANES, STATE_LANES), BF16),
                        pltpu.VMEM((T, STATE_LANES), F32), pltpu.VMEM((T, STATE_LANES), F32),
                        pltpu.VMEM((2, T, STATE_LANES), F32),
                        pltpu.VMEM((1, STATE_LANES), F32), pltpu.VMEM((1, STATE_LANES), F32)],
        compiler_params=_cparams("arbitrary", "arbitrary"),
    )(u, lr_e, li_e, ldt_e, bre_e, bim_e, cre_e, cim_e, d_skip)


def _ssm_bwd(u, y1, dy2a, dy2b, ends_r, ends_i, lr_e, li_e, ldt_e, bre_e, bim_e, cre_e, cim_e, d_skip):
    S, SW = u.shape
    T = min(SSM_CHUNK, S)
    nch, nbk = S // T, SW // LANES
    tok, par, bmat, cmat, dvec = _ssm_specs(T, nch, True)
    prev_spec = pl.BlockSpec((None, 1, STATE_LANES), lambda j, c: (jnp.maximum(nch - 2 - c, 0), 0, j))
    acc8 = pl.BlockSpec((None, 8, STATE_LANES), lambda j, c: (j, 0, 0))
    dd8 = pl.BlockSpec((None, 8, LANES), lambda j, c: (j, 0, 0))

    def body(u_ref, y_ref, da_ref, db_ref, pr_ref, pi_ref, lr_ref, li_ref, ldt_ref,
             bre_ref, bim_ref, cre_ref, cim_ref, d_ref,
             du_ref, dar_ref, dai_ref, dcr_ref, dci_ref, dbr_ref, dbi_ref, ddk_ref,
             bbr, bbi, pwr, pwi, qwr, qwi, g_scr, gr0, gi0):
        c = pl.program_id(1)
        lr, li, ldt = lr_ref[...], li_ref[...], ldt_ref[...]
        ar, ai, kr, ki = _ssm_disc(lr, li, ldt)

        @pl.when(c == 0)
        def _():
            bbr[...] = (kr * bre_ref[...] - ki * bim_ref[...]).astype(BF16)
            bbi[...] = (kr * bim_ref[...] + ki * bre_ref[...]).astype(BF16)
            n = lax.broadcasted_iota(jnp.int32, (T, 1), 0)
            pwr[...], pwi[...] = _ssm_power_table(lr, li, ldt, (n + 1).astype(F32))
            qr, qi = _ssm_power_table(lr, li, ldt, (T - n).astype(F32))
            qwr[...] = qr
            qwi[...] = -qi
            gr0[...] = jnp.zeros_like(gr0)
            gi0[...] = jnp.zeros_like(gi0)
            for ref in (dar_ref, dai_ref, dcr_ref, dci_ref, dbr_ref, dbi_ref, ddk_ref):
                ref[...] = jnp.zeros_like(ref)

        u_ = u_ref[...]
        ub = u_.astype(BF16)
        dy1 = (da_ref[...] + db_ref[...]) * _gelu_grad(y_ref[...])
        dyb = dy1.astype(BF16)

        has_prev = c < nch - 1
        s0r = jnp.where(has_prev, pr_ref[...], 0.0)
        s0i = jnp.where(has_prev, pi_ref[...], 0.0)
        xr, xi = _scan(jnp.dot(ub, bbr[...], preferred_element_type=F32),
                       jnp.dot(ub, bbi[...], preferred_element_type=F32), ar, ai, False)
        er, ei = _cmul(pwr[...], pwi[...], s0r, s0i)
        sr, si = xr + er, xi + ei

        cre_b, cim_b = cre_ref[...].astype(BF16), cim_ref[...].astype(BF16)
        hr, hi = _scan(lax.dot_general(dyb, cre_b, _NT, preferred_element_type=F32),
                       -lax.dot_general(dyb, cim_b, _NT, preferred_element_type=F32), ar, -ai, True)
        fr, fi = _cmul(qwr[...], qwi[...], gr0[...], gi0[...])
        gr, gi = hr + fr, hi + fi
        g_scr[0] = gr
        g_scr[1] = gi
        gr0[...] = g_scr[0, pl.ds(0, 1), :]
        gi0[...] = g_scr[1, pl.ds(0, 1), :]

        row = lax.broadcasted_iota(jnp.int32, (T, STATE_LANES), 0)
        spr = jnp.where(row == 0, s0r, pltpu.roll(sr, 1, 0))
        spi = jnp.where(row == 0, s0i, pltpu.roll(si, 1, 0))

        def fold(a):
            return jnp.sum(a.reshape(T // 8, 8, a.shape[-1]), axis=0)

        dar_ref[...] += fold(gr * spr + gi * spi)
        dai_ref[...] += fold(gi * spr - gr * spi)
        srb, sib, grb, gib = sr.astype(BF16), si.astype(BF16), gr.astype(BF16), gi.astype(BF16)
        dcr_ref[...] += lax.dot_general(srb, dyb, _TN, preferred_element_type=F32)
        dci_ref[...] -= lax.dot_general(sib, dyb, _TN, preferred_element_type=F32)
        dbr_ref[...] += lax.dot_general(ub, grb, _TN, preferred_element_type=F32)
        dbi_ref[...] += lax.dot_general(ub, gib, _TN, preferred_element_type=F32)
        du_ref[...] = (lax.dot_general(grb, bbr[...], _NT, preferred_element_type=F32)
                       + lax.dot_general(gib, bbi[...], _NT, preferred_element_type=F32)
                       + dy1 * d_ref[...])
        ddk_ref[...] += fold(dy1 * u_)

    return pl.pallas_call(
        body, name="ssm_bwd",
        out_shape=(jax.ShapeDtypeStruct((S, SW), F32),
                   jax.ShapeDtypeStruct((nbk, 8, STATE_LANES), F32), jax.ShapeDtypeStruct((nbk, 8, STATE_LANES), F32),
                   jax.ShapeDtypeStruct((nbk, STATE_LANES, LANES), F32), jax.ShapeDtypeStruct((nbk, STATE_LANES, LANES), F32),
                   jax.ShapeDtypeStruct((nbk, LANES, STATE_LANES), F32), jax.ShapeDtypeStruct((nbk, LANES, STATE_LANES), F32),
                   jax.ShapeDtypeStruct((nbk, 8, LANES), F32)),
        grid=(nbk, nch),
        in_specs=[tok, tok, tok, tok, prev_spec, prev_spec, par, par, par, bmat, bmat, cmat, cmat, dvec],
        out_specs=(tok, acc8, acc8, cmat, cmat, bmat, bmat, dd8),
        scratch_shapes=[pltpu.VMEM((LANES, STATE_LANES), BF16), pltpu.VMEM((LANES, STATE_LANES), BF16),
                        pltpu.VMEM((T, STATE_LANES), F32), pltpu.VMEM((T, STATE_LANES), F32),
                        pltpu.VMEM((T, STATE_LANES), F32), pltpu.VMEM((T, STATE_LANES), F32),
                        pltpu.VMEM((2, T, STATE_LANES), F32),
                        pltpu.VMEM((1, STATE_LANES), F32), pltpu.VMEM((1, STATE_LANES), F32)],
        compiler_params=_cparams("arbitrary", "arbitrary"),
    )(u, y1, dy2a, dy2b, ends_r, ends_i, lr_e, li_e, ldt_e, bre_e, bim_e, cre_e, cim_e, d_skip)


def _ssm_param_bwd(dar8, dai8, dbr_e, dbi_e, lr_e, li_e, ldt_e, bre_e, bim_e):
    nbk = lr_e.shape[0]
    par = pl.BlockSpec((None, 1, STATE_LANES), lambda j: (j, 0, 0))
    acc8 = pl.BlockSpec((None, 8, STATE_LANES), lambda j: (j, 0, 0))
    bmat = pl.BlockSpec((None, LANES, STATE_LANES), lambda j: (j, 0, 0))

    def body(dar_ref, dai_ref, dbr_ref, dbi_ref, lr_ref, li_ref, ldt_ref, bre_ref, bim_ref,
             dlr_ref, dli_ref, dldt_ref, dbre_ref, dbim_ref):
        lr, li, ldt = lr_ref[...], li_ref[...], ldt_ref[...]
        (ar, ai, kr, ki), vjp = jax.vjp(_ssm_disc, lr, li, ldt)
        dbr, dbi, bre, bim = dbr_ref[...], dbi_ref[...], bre_ref[...], bim_ref[...]
        dbre_ref[...] = kr * dbr + ki * dbi
        dbim_ref[...] = kr * dbi - ki * dbr
        dkr = _colsum(dbr * bre + dbi * bim)
        dki = _colsum(dbi * bre - dbr * bim)
        dlr, dli, dldt = vjp((_colsum(dar_ref[...]), _colsum(dai_ref[...]), dkr, dki))
        dlr_ref[...] = dlr
        dli_ref[...] = dli
        tot = jnp.broadcast_to(dldt, (8, STATE_LANES))
        sh = 1
        while sh < SSM_P:
            tot = tot + pltpu.roll(tot, STATE_LANES - sh, 1)
            sh *= 2
        dldt_ref[...] = tot[:1]

    vec = jax.ShapeDtypeStruct((nbk, 1, STATE_LANES), F32)
    mat = jax.ShapeDtypeStruct((nbk, LANES, STATE_LANES), F32)
    return pl.pallas_call(
        body, name="ssm_param_bwd", out_shape=(vec, vec, vec, mat, mat), grid=(nbk,),
        in_specs=[acc8, acc8, bmat, bmat, par, par, par, bmat, bmat],
        out_specs=(par, par, par, bmat, bmat), compiler_params=_cparams("parallel"),
    )(dar8, dai8, dbr_e, dbi_e, lr_e, li_e, ldt_e, bre_e, bim_e)


def _expand_b(b):
    G = b.shape[0]
    bt = b.transpose(0, 2, 1).reshape(G // GROUPS_PER_BLOCK, GROUPS_PER_BLOCK, SSM_C, SSM_P)
    eye = jnp.eye(GROUPS_PER_BLOCK, dtype=b.dtype)
    return (bt[:, :, :, None, :] * eye[None, :, None, :, None]).reshape(G // GROUPS_PER_BLOCK, LANES, STATE_LANES)


def _collapse_b(be):
    nbk = be.shape[0]
    eye = jnp.eye(GROUPS_PER_BLOCK, dtype=be.dtype)
    d5 = be.reshape(nbk, GROUPS_PER_BLOCK, SSM_C, GROUPS_PER_BLOCK, SSM_P)
    d4 = (d5 * eye[None, :, None, :, None]).sum(axis=3)
    return d4.transpose(0, 1, 3, 2).reshape(nbk * GROUPS_PER_BLOCK, SSM_P, SSM_C)


def _expand_c(cm):
    G = cm.shape[0]
    ct = cm.transpose(0, 2, 1).reshape(G // GROUPS_PER_BLOCK, GROUPS_PER_BLOCK, SSM_P, SSM_C)
    eye = jnp.eye(GROUPS_PER_BLOCK, dtype=cm.dtype)
    return (ct[:, :, :, None, :] * eye[None, :, None, :, None]).reshape(G // GROUPS_PER_BLOCK, STATE_LANES, LANES)


def _collapse_c(ce):
    nbk = ce.shape[0]
    eye = jnp.eye(GROUPS_PER_BLOCK, dtype=ce.dtype)
    d5 = ce.reshape(nbk, GROUPS_PER_BLOCK, SSM_P, GROUPS_PER_BLOCK, SSM_C)
    d4 = (d5 * eye[None, :, None, :, None]).sum(axis=3)
    return d4.transpose(0, 1, 3, 2).reshape(nbk * GROUPS_PER_BLOCK, SSM_C, SSM_P)


def _place():
    x, y, c = lax.axis_index("x"), lax.axis_index("y"), lax.axis_index("c")
    return x, y, c


def _other_chips(x, y):
    return [(1 - x, y), (x, 1 - y), (1 - x, 1 - y)]


_ANY = pl.BlockSpec(memory_space=pl.ANY)


_HBM = pl.BlockSpec(memory_space=pltpu.HBM)
_SEM = pl.BlockSpec(memory_space=pltpu.SEMAPHORE)
_EFFECT = pltpu.SideEffectType.DATAFLOW_SIDE_EFFECTING
_TOKEN = jax.ShapeDtypeStruct((8, LANES), F32)


def _hbm(a):
    return pltpu.with_memory_space_constraint(a, pltpu.HBM)


def _place_own(srcs, *, gather, name):
    n = len(srcs)

    def body(*refs):
        ins, outs, sems = refs[:n], refs[n:2 * n], refs[2 * n]
        x, y, _ = _place()
        me = 2 * x + y
        copies = [pltpu.make_async_copy(ins[w] if gather else ins[w].at[me], outs[w].at[me], sems.at[w])
                  for w in range(n)]
        for cp in copies:
            cp.start()
        for cp in copies:
            cp.wait()

    return pl.pallas_call(
        body, name=name,
        out_shape=[jax.ShapeDtypeStruct(((N_CHIPS,) + s.shape) if gather else s.shape, s.dtype) for s in srcs],
        in_specs=[_ANY] * n, out_specs=[_ANY] * n, scratch_shapes=[pltpu.SemaphoreType.DMA((n,))],
    )(*srcs)


def _exchange_copy(src, land, send, recv, k, j, peer, c, gather):
    px, py = peer
    return pltpu.make_async_remote_copy(
        src_ref=src if gather else src.at[2 * px + py], dst_ref=land, send_sem=send.at[3 * k + j],
        recv_sem=recv.at[3 * k + j], device_id=(px, py, c), device_id_type=MESH)


def _exchange_start(srcs, lands, groups, *, gather, name):
    n, ng = len(srcs), len(groups)

    def body(*refs):
        ins, lnd, sems = refs[:n], refs[n:2 * n], refs[2 * n:2 * n + 2 * ng]
        token = refs[4 * n + 2 * ng]
        x, y, c = _place()
        me = 2 * x + y
        for gi, group in enumerate(groups):
            for k, w in enumerate(group):
                for j, peer in enumerate(_other_chips(x, y)):
                    _exchange_copy(ins[w], lnd[w].at[me], sems[2 * gi], sems[2 * gi + 1], k, j, peer, c, gather).start()
        token[...] = jnp.zeros_like(token)

    sem_shapes = [pltpu.SemaphoreType.DMA((3 * len(g),)) for g in groups for _ in range(2)]
    bufs = list(srcs) + list(lands)
    res = pl.pallas_call(
        body, name=name,
        out_shape=sem_shapes + [pltpu.HBM(a.shape, a.dtype) for a in bufs] + [_TOKEN],
        in_specs=[_HBM] * (2 * n),
        out_specs=[_SEM] * (2 * ng) + [_HBM] * (2 * n) + [pl.BlockSpec(memory_space=pltpu.VMEM)],
        input_output_aliases={i: 2 * ng + i for i in range(2 * n)},
        compiler_params=pltpu.CompilerParams(has_side_effects=_EFFECT),
    )(*[_hbm(a) for a in bufs])
    sems = [(res[2 * gi], res[2 * gi + 1]) for gi in range(ng)]
    return sems, res[2 * ng:2 * ng + n], res[2 * ng + n:2 * ng + 2 * n], res[-1]


def _exchange_wait(srcs, lands, sems, after, *, gather, name):
    n = len(srcs)
    send_sems, recv_sems = sems

    def body(*refs):
        ins, lnd, send, recv = refs[:n], refs[n:2 * n], refs[2 * n], refs[2 * n + 1]
        x, y, c = _place()
        for k in range(n):
            for j, peer in enumerate(_other_chips(x, y)):
                copy = _exchange_copy(ins[k], lnd[k].at[2 * peer[0] + peer[1]], send, recv, k, j, peer, c, gather)
                copy.wait_send()
                copy.wait_recv()

    bufs = list(srcs) + list(lands)
    res = pl.pallas_call(
        body, name=name, out_shape=[pltpu.HBM(a.shape, a.dtype) for a in bufs],
        in_specs=[_HBM] * (2 * n) + [_SEM, _SEM, _ANY], out_specs=[_HBM] * (2 * n),
        input_output_aliases={i: i for i in range(2 * n)},
        compiler_params=pltpu.CompilerParams(has_side_effects=_EFFECT),
    )(*bufs, send_sems, recv_sems, after)
    return res[n:]


def _sum_partials(land, *, name, tr=256):
    _, R, C = land.shape
    tr = min(tr, R)

    def body(l_ref, o_ref):
        acc = l_ref[0].astype(F32)
        for k in range(1, N_CHIPS):
            acc = acc + l_ref[k].astype(F32)
        o_ref[...] = acc

    return pl.pallas_call(
        body, name=name, out_shape=jax.ShapeDtypeStruct((R, C), F32), grid=(R // tr,),
        in_specs=[pl.BlockSpec((N_CHIPS, tr, C), lambda i: (0, i, 0))], out_specs=_rows(tr, C),
        compiler_params=_cparams("parallel"))(land)


def _swap_with_sibling(sums, *, name):
    n = len(sums)

    def body(*refs):
        ins, outs = refs[:n], refs[n:2 * n]
        send_sems, recv_sems = refs[2 * n:]
        x, y, c = _place()
        copies = [pltpu.make_async_remote_copy(
            src_ref=ins[w], dst_ref=outs[w], send_sem=send_sems.at[w], recv_sem=recv_sems.at[w],
            device_id=(x, y, 1 - c), device_id_type=MESH) for w in range(n)]
        for cp in copies:
            cp.start()
        for cp in copies:
            cp.wait_recv()
            cp.wait_send()

    return pl.pallas_call(
        body, name=name,
        out_shape=[jax.ShapeDtypeStruct(s.shape, s.dtype) for s in sums],
        in_specs=[_ANY] * n, out_specs=[_ANY] * n,
        scratch_shapes=[pltpu.SemaphoreType.DMA((n,)), pltpu.SemaphoreType.DMA((n,))],
    )(*sums)


def _adamw_math(w, g, m, v):
    m = ADAM_B1 * m + (1.0 - ADAM_B1) * g
    v = ADAM_B2 * v + (1.0 - ADAM_B2) * (g * g)
    m_hat = m / (1.0 - ADAM_B1 ** ADAM_STEP)
    v_hat = v / (1.0 - ADAM_B2 ** ADAM_STEP)
    delta = -ADAM_LR * (m_hat / (jnp.sqrt(v_hat) + ADAM_EPS) + ADAM_WD * w)
    return delta, m, v


def _adamw_pair(mine, theirs, w, m, v, *, name, tr=128):
    R, C = w.shape
    tr = min(tr, R)

    def body(a_ref, b_ref, w_ref, m_ref, v_ref, g_ref, d_ref, nm_ref, nv_ref):
        g = a_ref[...] + b_ref[...]
        g_ref[...] = g
        d_ref[...], nm_ref[...], nv_ref[...] = _adamw_math(w_ref[...], g, m_ref[...], v_ref[...])

    shape = jax.ShapeDtypeStruct((R, C), F32)
    return pl.pallas_call(
        body, name=name, out_shape=(shape,) * 4, grid=(R // tr,),
        in_specs=[_rows(tr, C)] * 5, out_specs=(_rows(tr, C),) * 4,
        compiler_params=_cparams("parallel"))(mine, theirs, w, m, v)


def _all_gather_small(packed):
    R = packed.shape[0]

    def body(x_ref, out_ref, send_sems, recv_sems, local_sem):
        x, y, c = _place()

        def slot(px, py, pc):
            return out_ref.at[4 * px + 2 * py + pc]

        local = pltpu.make_async_copy(x_ref, slot(x, y, c), local_sem)
        local.start()
        peers = [(x ^ (k >> 2), y ^ ((k >> 1) & 1), c ^ (k & 1)) for k in range(1, N_DEV)]
        for k, peer in enumerate(peers):
            pltpu.make_async_remote_copy(
                src_ref=x_ref, dst_ref=slot(x, y, c), send_sem=send_sems.at[k], recv_sem=recv_sems.at[k],
                device_id=peer, device_id_type=MESH).start()
        for k, peer in enumerate(peers):
            arrival = pltpu.make_async_remote_copy(
                src_ref=x_ref, dst_ref=slot(*peer), send_sem=send_sems.at[k], recv_sem=recv_sems.at[k],
                device_id=peer, device_id_type=MESH)
            arrival.wait_recv()
            arrival.wait_send()
        local.wait()

    vm = pl.BlockSpec(memory_space=pltpu.VMEM)
    return pl.pallas_call(
        body, name="all_gather_small", out_shape=jax.ShapeDtypeStruct((N_DEV, R, LANES), F32),
        in_specs=[vm], out_specs=vm,
        scratch_shapes=[pltpu.SemaphoreType.DMA((N_DEV - 1,)), pltpu.SemaphoreType.DMA((N_DEV - 1,)),
                        pltpu.SemaphoreType.DMA],
        compiler_params=pltpu.CompilerParams(vmem_limit_bytes=VMEM_LIMIT_BYTES),
    )(packed)


def _adamw_small(gathered, w, m, v):
    _, R, _ = gathered.shape
    tr = PACK_ROWS

    def body(gs_ref, w_ref, m_ref, v_ref, g_ref, d_ref, nm_ref, nv_ref):
        g = gs_ref[0]
        for k in range(1, N_DEV):
            g = g + gs_ref[k]
        g_ref[...] = g
        d_ref[...], nm_ref[...], nv_ref[...] = _adamw_math(w_ref[...], g, m_ref[...], v_ref[...])

    shape = jax.ShapeDtypeStruct((R, LANES), F32)
    return pl.pallas_call(
        body, name="adamw_small", out_shape=(shape,) * 4, grid=(R // tr,),
        in_specs=[pl.BlockSpec((N_DEV, tr, LANES), lambda i: (0, i, 0))] + [_rows(tr, LANES)] * 3,
        out_specs=(_rows(tr, LANES),) * 4, compiler_params=_cparams("parallel"))(gathered, w, m, v)


def _pack(arrays):
    parts, layout = [], []
    for a in arrays:
        n = a.size
        rows = -(-n // (8 * LANES)) * 8
        flat = jnp.pad(a.reshape(-1).astype(F32), (0, rows * LANES - n))
        parts.append(flat.reshape(rows, LANES))
        layout.append((rows, n, a.shape))
    total = sum(r for r, _, _ in layout)
    parts.append(jnp.zeros((-total % PACK_ROWS, LANES), F32))
    return jnp.concatenate(parts, axis=0), layout


def _unpack(buf, layout):
    out, r0 = [], 0
    for rows, n, shape in layout:
        out.append(buf[r0:r0 + rows].reshape(-1)[:n].reshape(shape))
        r0 += rows
    return out


SMALL = ("mix_norm_pre", "lam_re", "lam_im", "log_dt", "ssm_b_re", "ssm_b_im", "ssm_c_re", "ssm_c_im",
         "ssm_d", "b_glu", "attn_out_norm", "ssm_out_norm", "mix_norm_post", "mlp_norm_pre",
         "mlp_norm_post", "ple_norm_pre", "ple_norm_post")
BIG = ("w_in", "w_glu", "w_out", "w_up", "w_down", "w_ple_gate", "w_ple_proj")
WEIGHTS = ("mix_norm_pre", "w_in", "lam_re", "lam_im", "log_dt", "ssm_b_re", "ssm_b_im", "ssm_c_re",
           "ssm_c_im", "ssm_d", "w_glu", "b_glu", "attn_out_norm", "ssm_out_norm", "w_out",
           "mix_norm_post", "mlp_norm_pre", "w_up", "w_down", "mlp_norm_post", "ple_norm_pre",
           "w_ple_gate", "w_ple_proj", "ple_norm_post")


def _to_branch(a, d):
    return a if d == 1 else a.reshape(a.shape[0] // d, d * a.shape[1])


def _from_branch(a, d, S):
    return a if d == 1 else a.reshape(S, a.shape[1] // d)


def kernel(x, p, mix_norm_pre, w_in, lam_re, lam_im, log_dt, ssm_b_re, ssm_b_im, ssm_c_re, ssm_c_im, ssm_d, w_glu, b_glu, attn_out_norm, ssm_out_norm, w_out, mix_norm_post, mlp_norm_pre, w_up, w_down, mlp_norm_post, ple_norm_pre, w_ple_gate, w_ple_proj, ple_norm_post, loss_target, m_mix_norm_pre, m_w_in, m_lam_re, m_lam_im, m_log_dt, m_ssm_b_re, m_ssm_b_im, m_ssm_c_re, m_ssm_c_im, m_ssm_d, m_w_glu, m_b_glu, m_attn_out_norm, m_ssm_out_norm, m_w_out, m_mix_norm_post, m_mlp_norm_pre, m_w_up, m_w_down, m_mlp_norm_post, m_ple_norm_pre, m_w_ple_gate, m_w_ple_proj, m_ple_norm_post, v_mix_norm_pre, v_w_in, v_lam_re, v_lam_im, v_log_dt, v_ssm_b_re, v_ssm_b_im, v_ssm_c_re, v_ssm_c_im, v_ssm_d, v_w_glu, v_b_glu, v_attn_out_norm, v_ssm_out_norm, v_w_out, v_mix_norm_post, v_mlp_norm_pre, v_w_up, v_w_down, v_mlp_norm_post, v_ple_norm_pre, v_w_ple_gate, v_w_ple_proj, v_ple_norm_post):
    args = dict(locals())
    W = {n: args[n][0] for n in WEIGHTS}
    Mo = {n: args["m_" + n][0] for n in WEIGHTS}
    Vo = {n: args["v_" + n][0] for n in WEIGHTS}
    xs, ps, tgt = x[0], p[0, 0], loss_target[0]
    S, D = xs.shape
    SW = W["ssm_d"].shape[0]
    AW = W["attn_out_norm"].shape[0]
    heads = AW // HEAD_DIM
    G = SW // SSM_C
    nbk = SW // LANES
    assert W["w_in"].shape[1] * N_CHIPS == 3 * AW + SW and AW == SW

    row = lambda a: a.reshape(1, -1)

    ag_groups = (("w_in",), ("w_glu", "w_out", "w_up"), ("w_down", "w_ple_gate", "w_ple_proj"))
    ag_names = [n for g in ag_groups for n in g]
    shards = [W[n].astype(BF16) for n in ag_names]
    ag_sems, ag_src, ag_land, ag_token = _exchange_start(
        shards, _place_own(shards, gather=True, name="ag_place"),
        [[ag_names.index(n) for n in g] for g in ag_groups], gather=True, name="ag_start")

    def gathered(gi, after):
        idx = [ag_names.index(n) for n in ag_groups[gi]]
        got = _exchange_wait([ag_src[i] for i in idx], [ag_land[i] for i in idx], ag_sems[gi], after,
                             gather=True, name=f"ag_wait_{gi}")
        return dict(zip(ag_groups[gi], got))

    lr_e = W["lam_re"].reshape(nbk, 1, STATE_LANES)
    li_e = W["lam_im"].reshape(nbk, 1, STATE_LANES)
    ldt_e = jnp.repeat(W["log_dt"], SSM_P).reshape(nbk, 1, STATE_LANES)
    bre_e, bim_e = _expand_b(W["ssm_b_re"]), _expand_b(W["ssm_b_im"])
    cre_e, cim_e = _expand_c(W["ssm_c_re"]), _expand_c(W["ssm_c_im"])
    d_row = row(W["ssm_d"])

    hn1 = _norm_cast(xs, row(W["mix_norm_pre"]) + ag_token[0, 0], name="norm_in")
    w_in_f = gathered(0, hn1)["w_in"]
    qkv = _matmul(hn1, w_in_f, name="proj_qkv", out_dtype=BF16, b_shards=N_CHIPS, b_cols=(0, 3 * AW))
    u = _matmul(hn1, w_in_f, name="proj_u", b_shards=N_CHIPS, b_cols=(3 * AW, SW))
    qkv_b = [_to_branch(qkv, d) for d in DILATIONS]
    outs, lses = [], []
    for d, qb in zip(DILATIONS, qkv_b):
        o, l = _attn_fwd(qb, d, heads)
        outs.append(_from_branch(o, d, S))
        lses.append(_from_branch(l, d, S))
    y1, ends_r, ends_i = _ssm_fwd(u, lr_e, li_e, ldt_e, bre_e, bim_e, cre_e, cim_e, d_row)
    y2b = _gelu_cast(y1)
    full = gathered(1, y2b)
    w_glu_f = full["w_glu"].reshape(SW, SW)
    w_out_f = full["w_out"].reshape(AW + SW, D)
    w_up_f = full["w_up"]
    z = _matmul(y2b, w_glu_f, name="glu_z")
    attn, lse, mixed = _mix_fwd(outs, lses, y1, z, row(W["b_glu"]), row(W["attn_out_norm"]), row(W["ssm_out_norm"]))
    mo = _matmul(mixed, w_out_f, name="mix_out")
    h1, hn2 = _res_norm(xs, mo, row(W["mix_norm_post"]), row(W["mlp_norm_pre"]), name="res_mix")
    up = _matmul(hn2, w_up_f, name="mlp_up", b_shards=N_CHIPS)
    act = _relu2(up)
    full = gathered(2, act)
    w_down_f = full["w_down"].reshape(-1, D)
    w_pg_f = full["w_ple_gate"].reshape(D, D)
    w_pp_f = full["w_ple_proj"]
    ff = _matmul(act, w_down_f, name="mlp_down")
    h2, hn3 = _res_norm(h1, ff, row(W["mlp_norm_post"]), row(W["ple_norm_pre"]), name="res_mlp")
    gl = _matmul(hn3, w_pg_f, name="ple_gate")
    e = _matmul(ps.astype(BF16), w_pp_f, name="ple_proj", b_shards=N_CHIPS)

    dh3, dgl, de, loss_part, dg_ple_post = _final(h2, gl, e, row(W["ple_norm_post"]), tgt)
    gW = {}
    out_g, out_d, out_m, out_v = {}, {}, {}, {}

    def scatter_start(names, tag):
        parts = [gW[n] if gW[n].ndim == 3 else gW[n].reshape((N_CHIPS, -1, gW[n].shape[1])) for n in names]
        sems, src, land, token = _exchange_start(
            parts, _place_own(parts, gather=False, name=f"rs_place_{tag}"), [list(range(len(names)))],
            gather=False, name=f"rs_start_{tag}")
        return (names, sems[0], src, land), token

    def scatter_finish(batch, after, tag):
        names, sems, src, land = batch
        landed = _exchange_wait(src, land, sems, after, gather=False, name=f"rs_wait_{tag}")
        sums = [_sum_partials(l, name="sum_" + n) for n, l in zip(names, landed)]
        theirs = _swap_with_sibling(sums, name=f"swap_{tag}")
        for n, a, b in zip(names, sums, theirs):
            out_g[n], out_d[n], out_m[n], out_v[n] = _adamw_pair(a, b, W[n], Mo[n], Vo[n], name="adamw_" + n)

    gW["w_ple_proj"] = _matmul(ps.astype(BF16), de, name="d_w_ple_proj", ta=True, out_dtype=BF16, out_shards=N_CHIPS)
    gW["w_ple_gate"] = _matmul(hn3, dgl, name="d_w_ple_gate", ta=True, out_dtype=BF16)
    dhn3 = _matmul(dgl, w_pg_f, name="d_hn3", tb=True)
    dh2, dff, dg_ple_pre, dg_mlp_post = _bwd_res_norm(
        dh3, dhn3, h2, row(W["ple_norm_pre"]), ff, row(W["mlp_norm_post"]), name="bwd_res_mlp")
    gW["w_down"] = _matmul(act, dff, name="d_w_down", ta=True, out_dtype=BF16)
    batch1, token1 = scatter_start(("w_ple_proj", "w_ple_gate", "w_down"), 1)
    dact = _matmul(dff, w_down_f, name="d_act", tb=True, after=token1)
    dup = _relu2_bwd(dact, up)
    gW["w_up"] = _matmul(hn2, dup, name="d_w_up", ta=True, out_dtype=BF16, out_shards=N_CHIPS)
    dhn2 = _matmul(dup, w_up_f, name="d_hn2", tb=True, b_shards=N_CHIPS)
    dh1, dmo, dg_mlp_pre, dg_mix_post = _bwd_res_norm(
        dh2, dhn2, h1, row(W["mlp_norm_pre"]), mo, row(W["mix_norm_post"]), name="bwd_res_mix")
    gW["w_out"] = _matmul(mixed, dmo, name="d_w_out", ta=True, out_dtype=BF16)
    dmixed = _matmul(dmo, w_out_f, name="d_mixed", tb=True)
    dattn, dd, dz, dy2a, dg_attn, dg_ssm, db_glu = _mix_bwd(
        dmixed, attn, y1, z, row(W["b_glu"]), row(W["attn_out_norm"]), row(W["ssm_out_norm"]))
    gW["w_glu"] = _matmul(y2b, dz, name="d_w_glu", ta=True, out_dtype=BF16)
    batch2, token2 = scatter_start(("w_up", "w_out", "w_glu"), 2)
    dy2b = _matmul(dz, w_glu_f, name="d_y2", tb=True, after=token2)
    du, dar8, dai8, dcr_e, dci_e, dbr_e, dbi_e, dd8 = _ssm_bwd(
        u, y1, dy2a, dy2b, ends_r, ends_i, lr_e, li_e, ldt_e, bre_e, bim_e, cre_e, cim_e, d_row)
    scatter_finish(batch1, du, 1)
    dlr_e, dli_e, dldt_e, dbre_e, dbim_e = _ssm_param_bwd(dar8, dai8, dbr_e, dbi_e, lr_e, li_e, ldt_e, bre_e, bim_e)

    grads5 = [[], [], [], [], []]
    for d, qb in zip(DILATIONS, qkv_b):
        res = _attn_bwd(qb, _to_branch(dattn, d), _to_branch(lse, d), _to_branch(dd, d), d, heads)
        for lst, a in zip(grads5, res):
            lst.append(_from_branch(a, d, S))
    dproj = _dproj_join(*grads5, du)
    scatter_finish(batch2, dproj, 2)
    gW["w_in"] = _matmul(hn1, dproj, name="d_w_in", ta=True, out_dtype=BF16, out_shards=N_CHIPS)
    batch3, token3 = scatter_start(("w_in",), 3)
    dhn1 = _matmul(dproj, w_in_f, name="d_hn1", tb=True, b_shards=N_CHIPS, after=token3)
    grad_x, dg_mix_pre = _bwd_first(dh1, dhn1, xs, row(W["mix_norm_pre"]))
    scatter_finish(batch3, grad_x, 3)

    small_g = {
        "mix_norm_pre": dg_mix_pre, "lam_re": dlr_e.reshape(G, SSM_P), "lam_im": dli_e.reshape(G, SSM_P),
        "log_dt": dldt_e.reshape(G, SSM_P)[:, 0], "ssm_b_re": _collapse_b(dbre_e), "ssm_b_im": _collapse_b(dbim_e),
        "ssm_c_re": _collapse_c(dcr_e), "ssm_c_im": _collapse_c(dci_e), "ssm_d": dd8.sum(axis=1).reshape(-1),
        "b_glu": db_glu, "attn_out_norm": dg_attn, "ssm_out_norm": dg_ssm, "mix_norm_post": dg_mix_post,
        "mlp_norm_pre": dg_mlp_pre, "mlp_norm_post": dg_mlp_post, "ple_norm_pre": dg_ple_pre,
        "ple_norm_post": dg_ple_post,
    }
    g_pack, layout = _pack([small_g[n].reshape(W[n].shape) for n in SMALL])
    w_pack, _ = _pack([W[n] for n in SMALL])
    m_pack, _ = _pack([Mo[n] for n in SMALL])
    v_pack, _ = _pack([Vo[n] for n in SMALL])
    packed = _adamw_small(_all_gather_small(g_pack), w_pack, m_pack, v_pack)
    for dst, buf in zip((out_g, out_d, out_m, out_v), packed):
        dst.update(zip(SMALL, _unpack(buf, layout)))

    loss = lax.psum(loss_part[0, 0], ("x", "y", "c"))
    lead = lambda a: a[None]
    return (loss, grad_x[None],
            *[lead(out_g[n]) for n in WEIGHTS], *[lead(out_d[n]) for n in WEIGHTS],
            *[lead(out_m[n]) for n in WEIGHTS], *[lead(out_v[n]) for n in WEIGHTS])
```

```python
import functools
import math

import jax
import jax.numpy as jnp
from jax import lax
from jax.experimental import pallas as pl
from jax.experimental.pallas import tpu as pltpu

F32 = jnp.float32
BF16 = jnp.bfloat16
MESH = pl.DeviceIdType.MESH

RMS_EPS = 1e-6
NEG_INF = -1e30
HEAD_DIM = 128
BLK = 128
DILATIONS = (1, 4, 16)
SSM_C = 16
SSM_P = 64
LANES = 128
GROUPS_PER_BLOCK = LANES // SSM_C
STATE_LANES = GROUPS_PER_BLOCK * SSM_P
SSM_CHUNK = 128
ADAM_LR, ADAM_B1, ADAM_B2, ADAM_EPS, ADAM_WD, ADAM_STEP = 1e-3, 0.9, 0.999, 1e-8, 0.01, 10
VMEM_LIMIT_BYTES = 56 * 1024 * 1024
N_CHIPS = 4
N_DEV = 8
PACK_ROWS = 256


def _cparams(*sem):
    return pltpu.CompilerParams(dimension_semantics=sem or None, vmem_limit_bytes=VMEM_LIMIT_BYTES)


def _rows(tr, w):
    return pl.BlockSpec((tr, w), lambda i: (i, 0))


def _vec(w):
    return pl.BlockSpec((1, w), lambda i: (0, 0))


def _sigmoid(x):
    return 1.0 / (1.0 + jnp.exp(-x))


def _gelu(x):
    c = math.sqrt(2.0 / math.pi)
    return 0.5 * x * (1.0 + jnp.tanh(c * (x + 0.044715 * x * x * x)))


def _gelu_grad(x):
    c = math.sqrt(2.0 / math.pi)
    th = jnp.tanh(c * (x + 0.044715 * x * x * x))
    return 0.5 * (1.0 + th) + 0.5 * x * (1.0 - th * th) * c * (1.0 + 3.0 * 0.044715 * x * x)


def _rms(x, g):
    r = lax.rsqrt(jnp.mean(x * x, axis=-1, keepdims=True) + RMS_EPS)
    return x * r * g


def _rms_bwd(dy, x, g):
    r = lax.rsqrt(jnp.mean(x * x, axis=-1, keepdims=True) + RMS_EPS)
    n = x * r
    dn = dy * g
    dx = r * (dn - n * jnp.mean(dn * n, axis=-1, keepdims=True))
    return dx, dy * n


def _colsum(a):
    return jnp.sum(a, axis=0, keepdims=True)


def _first(i):
    return i == 0


def _matmul(a, b, *, name, ta=False, tb=False, out_dtype=F32, b_shards=1, out_shards=1, b_cols=None,
            after=None, tm=1024, tn=1024, tk=512):
    if ta:
        K, M = a.shape
    else:
        M, K = a.shape
    if b_shards > 1:
        rows, cols = b.shape[1], b.shape[2] * b_shards
    else:
        rows, cols = b.shape
    N, Kb = (rows, cols) if tb else (cols, rows)
    assert K == Kb, (a.shape, b.shape, ta, tb)
    col0 = 0
    if b_cols is not None:
        assert not tb
        col0, N = b_cols
    tm, tn, tk = min(tm, M), min(tn, N), min(tk, K)
    if b_shards > 1:
        shard_cols = cols // b_shards
        if tb:
            tk = min(tk, shard_cols)
        else:
            tn = min(tn, shard_cols)
    if out_shards > 1:
        tn = min(tn, N // out_shards)
    assert M % tm == 0 and N % tn == 0 and K % tk == 0 and col0 % tn == 0
    nk = K // tk
    j0 = col0 // tn

    a_spec = (pl.BlockSpec((tk, tm), lambda i, j, k: (k, i)) if ta
              else pl.BlockSpec((tm, tk), lambda i, j, k: (i, k)))
    if b_shards > 1:
        if tb:
            per = shard_cols // tk
            b_spec = pl.BlockSpec((None, tn, tk), lambda i, j, k: (k // per, j, k % per))
        else:
            per = shard_cols // tn
            b_spec = pl.BlockSpec((None, tk, tn), lambda i, j, k: ((j + j0) // per, k, (j + j0) % per))
    else:
        b_spec = (pl.BlockSpec((tn, tk), lambda i, j, k: (j, k)) if tb
                  else pl.BlockSpec((tk, tn), lambda i, j, k: (k, j + j0)))
    if out_shards > 1:
        per_o = (N // out_shards) // tn
        out_shape = jax.ShapeDtypeStruct((out_shards, M, N // out_shards), out_dtype)
        out_spec = pl.BlockSpec((None, tm, tn), lambda i, j, k: (j // per_o, i, j % per_o))
    else:
        out_shape = jax.ShapeDtypeStruct((M, N), out_dtype)
        out_spec = pl.BlockSpec((tm, tn), lambda i, j, k: (i, j))
    dims = (((0 if ta else 1,), (1 if tb else 0,)), ((), ()))

    def body(a_ref, b_ref, *rest):
        o_ref, acc_ref = rest[-2:]
        k = pl.program_id(2)

        @pl.when(k == 0)
        def _():
            acc_ref[...] = jnp.zeros_like(acc_ref)

        acc_ref[...] += lax.dot_general(a_ref[...], b_ref[...], dims, preferred_element_type=F32)

        @pl.when(k == nk - 1)
        def _():
            o_ref[...] = acc_ref[...].astype(o_ref.dtype)

    extra = [] if after is None else [after]
    extra_specs = [pl.BlockSpec(after.shape, lambda i, j, k: (0, 0)) for after in extra]
    return pl.pallas_call(
        body, name=name, out_shape=out_shape, grid=(M // tm, N // tn, nk),
        in_specs=[a_spec, b_spec] + extra_specs, out_specs=out_spec,
        scratch_shapes=[pltpu.VMEM((tm, tn), F32)],
        compiler_params=_cparams("parallel", "parallel", "arbitrary"),
    )(a, b, *extra)


def _norm_cast(x, g, *, name, tr=256):
    S, D = x.shape
    tr = min(tr, S)

    def body(x_ref, g_ref, o_ref):
        o_ref[...] = _rms(x_ref[...], g_ref[...]).astype(BF16)

    return pl.pallas_call(
        body, name=name, out_shape=jax.ShapeDtypeStruct((S, D), BF16), grid=(S // tr,),
        in_specs=[_rows(tr, D), _vec(D)], out_specs=_rows(tr, D),
        compiler_params=_cparams("parallel"))(x, g)


def _res_norm(res, y, g_post, g_next, *, name, tr=256):
    S, D = res.shape
    tr = min(tr, S)

    def body(res_ref, y_ref, gp_ref, gn_ref, h_ref, hn_ref):
        h = res_ref[...] + _rms(y_ref[...], gp_ref[...])
        h_ref[...] = h
        hn_ref[...] = _rms(h, gn_ref[...]).astype(BF16)

    return pl.pallas_call(
        body, name=name,
        out_shape=(jax.ShapeDtypeStruct((S, D), F32), jax.ShapeDtypeStruct((S, D), BF16)),
        grid=(S // tr,), in_specs=[_rows(tr, D), _rows(tr, D), _vec(D), _vec(D)],
        out_specs=(_rows(tr, D), _rows(tr, D)), compiler_params=_cparams("parallel"))(res, y, g_post, g_next)


def _relu2(up, *, tr=128):
    S, F = up.shape
    tr = min(tr, S)

    def body(u_ref, o_ref):
        r = jnp.maximum(u_ref[...], 0.0)
        o_ref[...] = (r * r).astype(BF16)

    return pl.pallas_call(
        body, name="relu2", out_shape=jax.ShapeDtypeStruct((S, F), BF16), grid=(S // tr,),
        in_specs=[_rows(tr, F)], out_specs=_rows(tr, F), compiler_params=_cparams("parallel"))(up)


def _relu2_bwd(dact, up, *, tr=128):
    S, F = up.shape
    tr = min(tr, S)

    def body(d_ref, u_ref, o_ref):
        o_ref[...] = (d_ref[...] * (2.0 * jnp.maximum(u_ref[...], 0.0))).astype(BF16)

    return pl.pallas_call(
        body, name="relu2_bwd", out_shape=jax.ShapeDtypeStruct((S, F), BF16), grid=(S // tr,),
        in_specs=[_rows(tr, F), _rows(tr, F)], out_specs=_rows(tr, F),
        compiler_params=_cparams("parallel"))(dact, up)


def _gelu_cast(y1, *, tr=256):
    S, W = y1.shape
    tr = min(tr, S)

    def body(y_ref, o_ref):
        o_ref[...] = _gelu(y_ref[...]).astype(BF16)

    return pl.pallas_call(
        body, name="gelu_cast", out_shape=jax.ShapeDtypeStruct((S, W), BF16), grid=(S // tr,),
        in_specs=[_rows(tr, W)], out_specs=_rows(tr, W), compiler_params=_cparams("parallel"))(y1)


def _mix_fwd(os, ls, y1, z, b_glu, g_attn, g_ssm, *, tr=128):
    S, AW = os[0].shape
    SW = y1.shape[1]
    tr = min(tr, S)

    def body(o1, o2, o3, l1, l2, l3, y_ref, z_ref, b_ref, ga_ref, gs_ref, attn_ref, lse_ref, mixed_ref):
        la, lb, lc = l1[...], l2[...], l3[...]
        m = jnp.maximum(jnp.maximum(la, lb), lc)
        ea, eb, ec = jnp.exp(la - m), jnp.exp(lb - m), jnp.exp(lc - m)
        tot = ea + eb + ec
        attn = (ea * o1[...] + eb * o2[...] + ec * o3[...]) / tot
        attn_ref[...] = attn
        lse_ref[...] = m + jnp.log(tot)
        ssm = _gelu(y_ref[...]) * _sigmoid(z_ref[...] + b_ref[...])
        mixed_ref[:, :AW] = _rms(attn, ga_ref[...]).astype(BF16)
        mixed_ref[:, AW:] = _rms(ssm, gs_ref[...]).astype(BF16)

    return pl.pallas_call(
        body, name="mix_fwd",
        out_shape=(jax.ShapeDtypeStruct((S, AW), F32), jax.ShapeDtypeStruct((S, AW), F32),
                   jax.ShapeDtypeStruct((S, AW + SW), BF16)),
        grid=(S // tr,),
        in_specs=[_rows(tr, AW)] * 6 + [_rows(tr, SW), _rows(tr, SW), _vec(SW), _vec(AW), _vec(SW)],
        out_specs=(_rows(tr, AW), _rows(tr, AW), _rows(tr, AW + SW)),
        compiler_params=_cparams("parallel"))(*os, *ls, y1, z, b_glu, g_attn, g_ssm)


def _final(h2, gl, e, g_post, target, *, tr=128):
    S, D = h2.shape
    tr = min(tr, S)

    def body(h_ref, gl_ref, e_ref, g_ref, t_ref, dh_ref, dgl_ref, de_ref, loss_ref, dg_ref):
        i = pl.program_id(0)
        gate = _sigmoid(gl_ref[...])
        e_ = e_ref[...]
        ge = gate * e_
        g = g_ref[...]
        diff = h_ref[...] + _rms(ge, g) - t_ref[...]
        dh = diff * (1.0 / D)
        dh_ref[...] = dh
        dge, dgrow = _rms_bwd(dh, ge, g)
        dgl_ref[...] = (dge * e_ * gate * (1.0 - gate)).astype(BF16)
        de_ref[...] = (dge * gate).astype(BF16)
        part = _colsum(0.5 * jnp.mean(diff * diff, axis=-1, keepdims=True))

        @pl.when(_first(i))
        def _():
            loss_ref[...] = jnp.zeros_like(loss_ref)
            dg_ref[...] = jnp.zeros_like(dg_ref)

        loss_ref[...] += part + jnp.zeros((1, LANES), F32)
        dg_ref[...] += _colsum(dgrow)

    return pl.pallas_call(
        body, name="final_fwd_bwd",
        out_shape=(jax.ShapeDtypeStruct((S, D), F32), jax.ShapeDtypeStruct((S, D), BF16),
                   jax.ShapeDtypeStruct((S, D), BF16), jax.ShapeDtypeStruct((1, LANES), F32),
                   jax.ShapeDtypeStruct((1, D), F32)),
        grid=(S // tr,),
        in_specs=[_rows(tr, D), _rows(tr, D), _rows(tr, D), _vec(D), _rows(tr, D)],
        out_specs=(_rows(tr, D), _rows(tr, D), _rows(tr, D), _vec(LANES), _vec(D)),
        compiler_params=_cparams("arbitrary"))(h2, gl, e, g_post, target)


def _bwd_res_norm(dh_out, dhn, h, g_next, y, g_post, *, name, tr=128):
    S, D = h.shape
    tr = min(tr, S)

    def body(dho_ref, dhn_ref, h_ref, gn_ref, y_ref, gp_ref, dh_ref, dy_ref, dgn_ref, dgp_ref):
        i = pl.program_id(0)
        dx, dgn_rows = _rms_bwd(dhn_ref[...], h_ref[...], gn_ref[...])
        dh = dho_ref[...] + dx
        dh_ref[...] = dh
        dy, dgp_rows = _rms_bwd(dh, y_ref[...], gp_ref[...])
        dy_ref[...] = dy.astype(BF16)

        @pl.when(_first(i))
        def _():
            dgn_ref[...] = jnp.zeros_like(dgn_ref)
            dgp_ref[...] = jnp.zeros_like(dgp_ref)

        dgn_ref[...] += _colsum(dgn_rows)
        dgp_ref[...] += _colsum(dgp_rows)

    return pl.pallas_call(
        body, name=name,
        out_shape=(jax.ShapeDtypeStruct((S, D), F32), jax.ShapeDtypeStruct((S, D), BF16),
                   jax.ShapeDtypeStruct((1, D), F32), jax.ShapeDtypeStruct((1, D), F32)),
        grid=(S // tr,),
        in_specs=[_rows(tr, D), _rows(tr, D), _rows(tr, D), _vec(D), _rows(tr, D), _vec(D)],
        out_specs=(_rows(tr, D), _rows(tr, D), _vec(D), _vec(D)),
        compiler_params=_cparams("arbitrary"))(dh_out, dhn, h, g_next, y, g_post)


def _bwd_first(dh1, dhn1, x, g1, *, tr=256):
    S, D = x.shape
    tr = min(tr, S)

    def body(dh_ref, dhn_ref, x_ref, g_ref, dx_ref, dg_ref):
        i = pl.program_id(0)
        dx, dg_rows = _rms_bwd(dhn_ref[...], x_ref[...], g_ref[...])
        dx_ref[...] = dh_ref[...] + dx

        @pl.when(_first(i))
        def _():
            dg_ref[...] = jnp.zeros_like(dg_ref)

        dg_ref[...] += _colsum(dg_rows)

    return pl.pallas_call(
        body, name="bwd_first",
        out_shape=(jax.ShapeDtypeStruct((S, D), F32), jax.ShapeDtypeStruct((1, D), F32)),
        grid=(S // tr,), in_specs=[_rows(tr, D), _rows(tr, D), _rows(tr, D), _vec(D)],
        out_specs=(_rows(tr, D), _vec(D)), compiler_params=_cparams("arbitrary"))(dh1, dhn1, x, g1)


def _mix_bwd(dmixed, attn, y1, z, b_glu, g_attn, g_ssm, *, tr=128):
    S, AW = attn.shape
    SW = y1.shape[1]
    tr = min(tr, S)
    heads = AW // HEAD_DIM

    def body(dm_ref, a_ref, y_ref, z_ref, b_ref, ga_ref, gs_ref,
             da_ref, dd_ref, dz_ref, dy2_ref, dga_ref, dgs_ref, db_ref):
        i = pl.program_id(0)
        attn_ = a_ref[...]
        dattn, dga_rows = _rms_bwd(dm_ref[:, :AW], attn_, ga_ref[...])
        da_ref[...] = dattn.astype(BF16)
        prod = dattn * attn_
        for h in range(heads):
            sl = slice(h * HEAD_DIM, (h + 1) * HEAD_DIM)
            dd_ref[:, sl] = jnp.broadcast_to(jnp.sum(prod[:, sl], axis=-1, keepdims=True), (tr, HEAD_DIM))
        y2 = _gelu(y_ref[...])
        gate = _sigmoid(z_ref[...] + b_ref[...])
        dssm, dgs_rows = _rms_bwd(dm_ref[:, AW:], y2 * gate, gs_ref[...])
        dz = dssm * y2 * gate * (1.0 - gate)
        dz_ref[...] = dz.astype(BF16)
        dy2_ref[...] = dssm * gate

        @pl.when(_first(i))
        def _():
            dga_ref[...] = jnp.zeros_like(dga_ref)
            dgs_ref[...] = jnp.zeros_like(dgs_ref)
            db_ref[...] = jnp.zeros_like(db_ref)

        dga_ref[...] += _colsum(dga_rows)
        dgs_ref[...] += _colsum(dgs_rows)
        db_ref[...] += _colsum(dz)

    return pl.pallas_call(
        body, name="mix_bwd",
        out_shape=(jax.ShapeDtypeStruct((S, AW), BF16), jax.ShapeDtypeStruct((S, AW), F32),
                   jax.ShapeDtypeStruct((S, SW), BF16), jax.ShapeDtypeStruct((S, SW), F32),
                   jax.ShapeDtypeStruct((1, AW), F32), jax.ShapeDtypeStruct((1, SW), F32),
                   jax.ShapeDtypeStruct((1, SW), F32)),
        grid=(S // tr,),
        in_specs=[_rows(tr, AW + SW), _rows(tr, AW), _rows(tr, SW), _rows(tr, SW), _vec(SW), _vec(AW), _vec(SW)],
        out_specs=(_rows(tr, AW), _rows(tr, AW), _rows(tr, SW), _rows(tr, SW), _vec(AW), _vec(SW), _vec(SW)),
        compiler_params=_cparams("arbitrary"))(dmixed, attn, y1, z, b_glu, g_attn, g_ssm)


def _attn_masks(i):
    row = lax.broadcasted_iota(jnp.int32, (BLK, BLK), 0)
    col = lax.broadcasted_iota(jnp.int32, (BLK, BLK), 1)
    return col <= row, jnp.logical_and(col >= row, i > 0)


_NT = (((1,), (1,)), ((), ()))
_TN = (((0,), (0,)), ((), ()))


def _attn_in_specs(heads):
    def at(part, prev):
        def index(r, h, i):
            return (jnp.maximum(i - 1, 0) if prev else i, r * 3 * heads + part * heads + h)
        return pl.BlockSpec((BLK, HEAD_DIM), index)
    return [at(0, False), at(1, False), at(1, True), at(2, False), at(2, True)]


def _attn_fwd(qkv, d, heads):
    M = qkv.shape[0]
    nb = M // BLK
    scale = 1.0 / math.sqrt(HEAD_DIM)

    def body(q_ref, kc_ref, kp_ref, vc_ref, vp_ref, o_ref, l_ref):
        i = pl.program_id(2)
        q = q_ref[...]
        mc, mp = _attn_masks(i)
        sc = jnp.where(mc, lax.dot_general(q, kc_ref[...], _NT, preferred_element_type=F32) * scale, NEG_INF)
        sp = jnp.where(mp, lax.dot_general(q, kp_ref[...], _NT, preferred_element_type=F32) * scale, NEG_INF)
        m = jnp.maximum(jnp.max(sc, axis=-1, keepdims=True), jnp.max(sp, axis=-1, keepdims=True))
        pc, pp = jnp.exp(sc - m), jnp.exp(sp - m)
        tot = jnp.sum(pc, axis=-1, keepdims=True) + jnp.sum(pp, axis=-1, keepdims=True)
        acc = (jnp.dot(pc.astype(BF16), vc_ref[...], preferred_element_type=F32)
               + jnp.dot(pp.astype(BF16), vp_ref[...], preferred_element_type=F32))
        o_ref[...] = acc / tot
        l_ref[...] = jnp.broadcast_to(m + jnp.log(tot), (BLK, HEAD_DIM))

    out_spec = pl.BlockSpec((BLK, HEAD_DIM), lambda r, h, i: (i, r * heads + h))
    shape = jax.ShapeDtypeStruct((M, d * heads * HEAD_DIM), F32)
    return pl.pallas_call(
        body, name=f"attn_fwd_d{d}", out_shape=(shape, shape), grid=(d, heads, nb),
        in_specs=_attn_in_specs(heads), out_specs=(out_spec, out_spec),
        compiler_params=_cparams("parallel", "parallel", "parallel"))(qkv, qkv, qkv, qkv, qkv)


def _attn_bwd(qkv, dattn, lse, dd, d, heads):
    M = qkv.shape[0]
    nb = M // BLK
    scale = 1.0 / math.sqrt(HEAD_DIM)

    def body(q_ref, kc_ref, kp_ref, vc_ref, vp_ref, da_ref, l_ref, dd_ref,
             dq_ref, dkc_ref, dkp_ref, dvc_ref, dvp_ref):
        i = pl.program_id(2)
        q, kc, kp, vc, vp, da = q_ref[...], kc_ref[...], kp_ref[...], vc_ref[...], vp_ref[...], da_ref[...]
        mc, mp = _attn_masks(i)
        lse_, dd_ = l_ref[...], dd_ref[...]
        sc = lax.dot_general(q, kc, _NT, preferred_element_type=F32) * scale
        sp = lax.dot_general(q, kp, _NT, preferred_element_type=F32) * scale
        pc = jnp.where(mc, jnp.exp(jnp.where(mc, sc, NEG_INF) - lse_), 0.0)
        pp = jnp.where(mp, jnp.exp(jnp.where(mp, sp, NEG_INF) - lse_), 0.0)
        dsc = (pc * (lax.dot_general(da, vc, _NT, preferred_element_type=F32) - dd_) * scale).astype(BF16)
        dsp = (pp * (lax.dot_general(da, vp, _NT, preferred_element_type=F32) - dd_) * scale).astype(BF16)
        dq_ref[...] = (jnp.dot(dsc, kc, preferred_element_type=F32) + jnp.dot(dsp, kp, preferred_element_type=F32))
        dkc_ref[...] = lax.dot_general(dsc, q, _TN, preferred_element_type=F32)
        dkp_ref[...] = lax.dot_general(dsp, q, _TN, preferred_element_type=F32)
        dvc_ref[...] = lax.dot_general(pc.astype(BF16), da, _TN, preferred_element_type=F32)
        dvp_ref[...] = lax.dot_general(pp.astype(BF16), da, _TN, preferred_element_type=F32)

    blk = pl.BlockSpec((BLK, HEAD_DIM), lambda r, h, i: (i, r * heads + h))
    shape = jax.ShapeDtypeStruct((M, d * heads * HEAD_DIM), F32)
    return pl.pallas_call(
        body, name=f"attn_bwd_d{d}", out_shape=(shape,) * 5, grid=(d, heads, nb),
        in_specs=_attn_in_specs(heads) + [blk, blk, blk], out_specs=(blk,) * 5,
        compiler_params=_cparams("parallel", "parallel", "parallel"))(qkv, qkv, qkv, qkv, qkv, dattn, lse, dd)


def _dproj_join(dqs, dkcs, dkps, dvcs, dvps, du):
    S, AW = dqs[0].shape
    SW = du.shape[1]
    nblk = S // BLK

    def cur():
        return pl.BlockSpec((BLK, AW), lambda i: (i, 0))

    def ahead(d):
        return pl.BlockSpec((BLK, AW), lambda i: (jnp.minimum(i + d, nblk - 1), 0))

    def body(*refs):
        dq_r, dkc_r, dkp_r, dvc_r, dvp_r = (refs[3 * n:3 * n + 3] for n in range(5))
        du_ref, out_ref = refs[15], refs[16]
        i = pl.program_id(0)
        dq = dq_r[0][...] + dq_r[1][...] + dq_r[2][...]
        dk = dkc_r[0][...] + dkc_r[1][...] + dkc_r[2][...]
        dv = dvc_r[0][...] + dvc_r[1][...] + dvc_r[2][...]
        for n, d in enumerate(DILATIONS):
            live = i + d < nblk
            dk = dk + jnp.where(live, dkp_r[n][...], 0.0)
            dv = dv + jnp.where(live, dvp_r[n][...], 0.0)
        out_ref[:, :AW] = dq.astype(BF16)
        out_ref[:, AW:2 * AW] = dk.astype(BF16)
        out_ref[:, 2 * AW:3 * AW] = dv.astype(BF16)
        out_ref[:, 3 * AW:] = du_ref[...].astype(BF16)

    in_specs = ([cur()] * 3 + [cur()] * 3 + [ahead(d) for d in DILATIONS]
                + [cur()] * 3 + [ahead(d) for d in DILATIONS] + [pl.BlockSpec((BLK, SW), lambda i: (i, 0))])
    return pl.pallas_call(
        body, name="dproj_join", out_shape=jax.ShapeDtypeStruct((S, 3 * AW + SW), BF16), grid=(nblk,),
        in_specs=in_specs, out_specs=pl.BlockSpec((BLK, 3 * AW + SW), lambda i: (i, 0)),
        compiler_params=_cparams("parallel"))(*dqs, *dkcs, *dkps, *dvcs, *dvps, du)


def _ssm_disc(lr, li, ldt):
    dt = jnp.exp(ldt)
    mag = jnp.exp(lr * dt)
    ar = mag * jnp.cos(li * dt)
    ai = mag * jnp.sin(li * dt)
    nr = ar - 1.0
    den = lr * lr + li * li
    return ar, ai, (nr * lr + ai * li) / den, (ai * lr - nr * li) / den


def _ssm_power_table(lr, li, ldt, n):
    dt = jnp.exp(ldt)
    mag = jnp.exp(n * (lr * dt))
    ang = n * (li * dt)
    return mag * jnp.cos(ang), mag * jnp.sin(ang)


def _cmul(ar, ai, br, bi):
    return ar * br - ai * bi, ar * bi + ai * br


def _scan(xr, xi, ar, ai, reverse):
    T = xr.shape[0]
    row = lax.broadcasted_iota(jnp.int32, xr.shape, 0)
    sh = 1
    while sh < T:
        if reverse:
            keep = row < T - sh
            sr, si = pltpu.roll(xr, T - sh, 0), pltpu.roll(xi, T - sh, 0)
        else:
            keep = row >= sh
            sr, si = pltpu.roll(xr, sh, 0), pltpu.roll(xi, sh, 0)
        sr, si = jnp.where(keep, sr, 0.0), jnp.where(keep, si, 0.0)
        pr, pi = _cmul(ar, ai, sr, si)
        xr, xi = xr + pr, xi + pi
        ar, ai = _cmul(ar, ai, ar, ai)
        sh *= 2
    return xr, xi


def _ssm_specs(T, nch, rev):
    def t_of(c):
        return nch - 1 - c if rev else c
    tok = pl.BlockSpec((T, LANES), lambda j, c: (t_of(c), j))
    par = pl.BlockSpec((None, 1, STATE_LANES), lambda j, c: (j, 0, 0))
    bmat = pl.BlockSpec((None, LANES, STATE_LANES), lambda j, c: (j, 0, 0))
    cmat = pl.BlockSpec((None, STATE_LANES, LANES), lambda j, c: (j, 0, 0))
    dvec = pl.BlockSpec((1, LANES), lambda j, c: (0, j))
    return tok, par, bmat, cmat, dvec


def _ssm_fwd(u, lr_e, li_e, ldt_e, bre_e, bim_e, cre_e, cim_e, d_skip):
    S, SW = u.shape
    T = min(SSM_CHUNK, S)
    nch, nbk = S // T, SW // LANES
    tok, par, bmat, cmat, dvec = _ssm_specs(T, nch, False)
    carry_spec = pl.BlockSpec((None, 1, STATE_LANES), lambda j, c: (c, 0, j))

    def body(u_ref, lr_ref, li_ref, ldt_ref, bre_ref, bim_ref, cre_ref, cim_ref, d_ref,
             y_ref, er_ref, ei_ref, bbr, bbi, pwr, pwi, st_scr, cr, ci):
        c = pl.program_id(1)
        lr, li, ldt = lr_ref[...], li_ref[...], ldt_ref[...]
        ar, ai, kr, ki = _ssm_disc(lr, li, ldt)

        @pl.when(c == 0)
        def _():
            bbr[...] = (kr * bre_ref[...] - ki * bim_ref[...]).astype(BF16)
            bbi[...] = (kr * bim_ref[...] + ki * bre_ref[...]).astype(BF16)
            n = (lax.broadcasted_iota(jnp.int32, (T, 1), 0) + 1).astype(F32)
            pwr[...], pwi[...] = _ssm_power_table(lr, li, ldt, n)
            cr[...] = jnp.zeros_like(cr)
            ci[...] = jnp.zeros_like(ci)

        u_ = u_ref[...]
        ub = u_.astype(BF16)
        xr, xi = _scan(jnp.dot(ub, bbr[...], preferred_element_type=F32),
                       jnp.dot(ub, bbi[...], preferred_element_type=F32), ar, ai, False)
        er, ei = _cmul(pwr[...], pwi[...], cr[...], ci[...])
        sr, si = xr + er, xi + ei
        st_scr[0] = sr
        st_scr[1] = si
        cr[...] = st_scr[0, pl.ds(T - 1, 1), :]
        ci[...] = st_scr[1, pl.ds(T - 1, 1), :]
        er_ref[...] = cr[...]
        ei_ref[...] = ci[...]
        y0 = (jnp.dot(sr.astype(BF16), cre_ref[...].astype(BF16), preferred_element_type=F32)
              - jnp.dot(si.astype(BF16), cim_ref[...].astype(BF16), preferred_element_type=F32))
        y_ref[...] = y0 + d_ref[...] * u_

    ends = jax.ShapeDtypeStruct((nch, 1, nbk * STATE_LANES), F32)
    return pl.pallas_call(
        body, name="ssm_fwd", out_shape=(jax.ShapeDtypeStruct((S, SW), F32), ends, ends),
        grid=(nbk, nch), in_specs=[tok, par, par, par, bmat, bmat, cmat, cmat, dvec],
        out_specs=(tok, carry_spec, carry_spec),
        scratch_shapes=[pltpu.VMEM((LANES, STATE_LANES), BF16), pltpu.VMEM((LANES, STATE_LANES), BF16),
                        pltpu.VMEM((T, STATE_LANES), F32), pltpu.VMEM((T, STATE_LANES), F32),
                        pltpu.VMEM((2, T, STATE_LANES), F32),
                        pltpu.VMEM((1, STATE_LANES), F32), pltpu.VMEM((1, STATE_LANES), F32)],
        compiler_params=_cparams("arbitrary", "arbitrary"),
    )(u, lr_e, li_e, ldt_e, bre_e, bim_e, cre_e, cim_e, d_skip)


def _ssm_bwd(u, y1, dy2a, dy2b, ends_r, ends_i, lr_e, li_e, ldt_e, bre_e, bim_e, cre_e, cim_e, d_skip):
    S, SW = u.shape
    T = min(SSM_CHUNK, S)
    nch, nbk = S // T, SW // LANES
    tok, par, bmat, cmat, dvec = _ssm_specs(T, nch, True)
    prev_spec = pl.BlockSpec((None, 1, STATE_LANES), lambda j, c: (jnp.maximum(nch - 2 - c, 0), 0, j))
    acc8 = pl.BlockSpec((None, 8, STATE_LANES), lambda j, c: (j, 0, 0))
    dd8 = pl.BlockSpec((None, 8, LANES), lambda j, c: (j, 0, 0))

    def body(u_ref, y_ref, da_ref, db_ref, pr_ref, pi_ref, lr_ref, li_ref, ldt_ref,
             bre_ref, bim_ref, cre_ref, cim_ref, d_ref,
             du_ref, dar_ref, dai_ref, dcr_ref, dci_ref, dbr_ref, dbi_ref, ddk_ref,
             bbr, bbi, pwr, pwi, qwr, qwi, g_scr, gr0, gi0):
        c = pl.program_id(1)
        lr, li, ldt = lr_ref[...], li_ref[...], ldt_ref[...]
        ar, ai, kr, ki = _ssm_disc(lr, li, ldt)

        @pl.when(c == 0)
        def _():
            bbr[...] = (kr * bre_ref[...] - ki * bim_ref[...]).astype(BF16)
            bbi[...] = (kr * bim_ref[...] + ki * bre_ref[...]).astype(BF16)
            n = lax.broadcasted_iota(jnp.int32, (T, 1), 0)
            pwr[...], pwi[...] = _ssm_power_table(lr, li, ldt, (n + 1).astype(F32))
            qr, qi = _ssm_power_table(lr, li, ldt, (T - n).astype(F32))
            qwr[...] = qr
            qwi[...] = -qi
            gr0[...] = jnp.zeros_like(gr0)
            gi0[...] = jnp.zeros_like(gi0)
            for ref in (dar_ref, dai_ref, dcr_ref, dci_ref, dbr_ref, dbi_ref, ddk_ref):
                ref[...] = jnp.zeros_like(ref)

        u_ = u_ref[...]
        ub = u_.astype(BF16)
        dy1 = (da_ref[...] + db_ref[...]) * _gelu_grad(y_ref[...])
        dyb = dy1.astype(BF16)

        has_prev = c < nch - 1
        s0r = jnp.where(has_prev, pr_ref[...], 0.0)
        s0i = jnp.where(has_prev, pi_ref[...], 0.0)
        xr, xi = _scan(jnp.dot(ub, bbr[...], preferred_element_type=F32),
                       jnp.dot(ub, bbi[...], preferred_element_type=F32), ar, ai, False)
        er, ei = _cmul(pwr[...], pwi[...], s0r, s0i)
        sr, si = xr + er, xi + ei

        cre_b, cim_b = cre_ref[...].astype(BF16), cim_ref[...].astype(BF16)
        hr, hi = _scan(lax.dot_general(dyb, cre_b, _NT, preferred_element_type=F32),
                       -lax.dot_general(dyb, cim_b, _NT, preferred_element_type=F32), ar, -ai, True)
        fr, fi = _cmul(qwr[...], qwi[...], gr0[...], gi0[...])
        gr, gi = hr + fr, hi + fi
        g_scr[0] = gr
        g_scr[1] = gi
        gr0[...] = g_scr[0, pl.ds(0, 1), :]
        gi0[...] = g_scr[1, pl.ds(0, 1), :]

        row = lax.broadcasted_iota(jnp.int32, (T, STATE_LANES), 0)
        spr = jnp.where(row == 0, s0r, pltpu.roll(sr, 1, 0))
        spi = jnp.where(row == 0, s0i, pltpu.roll(si, 1, 0))

        def fold(a):
            return jnp.sum(a.reshape(T // 8, 8, a.shape[-1]), axis=0)

        dar_ref[...] += fold(gr * spr + gi * spi)
        dai_ref[...] += fold(gi * spr - gr * spi)
        srb, sib, grb, gib = sr.astype(BF16), si.astype(BF16), gr.astype(BF16), gi.astype(BF16)
        dcr_ref[...] += lax.dot_general(srb, dyb, _TN, preferred_element_type=F32)
        dci_ref[...] -= lax.dot_general(sib, dyb, _TN, preferred_element_type=F32)
        dbr_ref[...] += lax.dot_general(ub, grb, _TN, preferred_element_type=F32)
        dbi_ref[...] += lax.dot_general(ub, gib, _TN, preferred_element_type=F32)
        du_ref[...] = (lax.dot_general(grb, bbr[...], _NT, preferred_element_type=F32)
                       + lax.dot_general(gib, bbi[...], _NT, preferred_element_type=F32)
                       + dy1 * d_ref[...])
        ddk_ref[...] += fold(dy1 * u_)

    return pl.pallas_call(
        body, name="ssm_bwd",
        out_shape=(jax.ShapeDtypeStruct((S, SW), F32),
                   jax.ShapeDtypeStruct((nbk, 8, STATE_LANES), F32), jax.ShapeDtypeStruct((nbk, 8, STATE_LANES), F32),
                   jax.ShapeDtypeStruct((nbk, STATE_LANES, LANES), F32), jax.ShapeDtypeStruct((nbk, STATE_LANES, LANES), F32),
                   jax.ShapeDtypeStruct((nbk, LANES, STATE_LANES), F32), jax.ShapeDtypeStruct((nbk, LANES, STATE_LANES), F32),
                   jax.ShapeDtypeStruct((nbk, 8, LANES), F32)),
        grid=(nbk, nch),
        in_specs=[tok, tok, tok, tok, prev_spec, prev_spec, par, par, par, bmat, bmat, cmat, cmat, dvec],
        out_specs=(tok, acc8, acc8, cmat, cmat, bmat, bmat, dd8),
        scratch_shapes=[pltpu.VMEM((LANES, STATE_LANES), BF16), pltpu.VMEM((LANES, STATE_LANES), BF16),
                        pltpu.VMEM((T, STATE_LANES), F32), pltpu.VMEM((T, STATE_LANES), F32),
                        pltpu.VMEM((T, STATE_LANES), F32), pltpu.VMEM((T, STATE_LANES), F32),
                        pltpu.VMEM((2, T, STATE_LANES), F32),
                        pltpu.VMEM((1, STATE_LANES), F32), pltpu.VMEM((1, STATE_LANES), F32)],
        compiler_params=_cparams("arbitrary", "arbitrary"),
    )(u, y1, dy2a, dy2b, ends_r, ends_i, lr_e, li_e, ldt_e, bre_e, bim_e, cre_e, cim_e, d_skip)


def _ssm_param_bwd(dar8, dai8, dbr_e, dbi_e, lr_e, li_e, ldt_e, bre_e, bim_e):
    nbk = lr_e.shape[0]
    par = pl.BlockSpec((None, 1, STATE_LANES), lambda j: (j, 0, 0))
    acc8 = pl.BlockSpec((None, 8, STATE_LANES), lambda j: (j, 0, 0))
    bmat = pl.BlockSpec((None, LANES, STATE_LANES), lambda j: (j, 0, 0))

    def body(dar_ref, dai_ref, dbr_ref, dbi_ref, lr_ref, li_ref, ldt_ref, bre_ref, bim_ref,
             dlr_ref, dli_ref, dldt_ref, dbre_ref, dbim_ref):
        lr, li, ldt = lr_ref[...], li_ref[...], ldt_ref[...]
        (ar, ai, kr, ki), vjp = jax.vjp(_ssm_disc, lr, li, ldt)
        dbr, dbi, bre, bim = dbr_ref[...], dbi_ref[...], bre_ref[...], bim_ref[...]
        dbre_ref[...] = kr * dbr + ki * dbi
        dbim_ref[...] = kr * dbi - ki * dbr
        dkr = _colsum(dbr * bre + dbi * bim)
        dki = _colsum(dbi * bre - dbr * bim)
        dlr, dli, dldt = vjp((_colsum(dar_ref[...]), _colsum(dai_ref[...]), dkr, dki))
        dlr_ref[...] = dlr
        dli_ref[...] = dli
        tot = jnp.broadcast_to(dldt, (8, STATE_LANES))
        sh = 1
        while sh < SSM_P:
            tot = tot + pltpu.roll(tot, STATE_LANES - sh, 1)
            sh *= 2
        dldt_ref[...] = tot[:1]

    vec = jax.ShapeDtypeStruct((nbk, 1, STATE_LANES), F32)
    mat = jax.ShapeDtypeStruct((nbk, LANES, STATE_LANES), F32)
    return pl.pallas_call(
        body, name="ssm_param_bwd", out_shape=(vec, vec, vec, mat, mat), grid=(nbk,),
        in_specs=[acc8, acc8, bmat, bmat, par, par, par, bmat, bmat],
        out_specs=(par, par, par, bmat, bmat), compiler_params=_cparams("parallel"),
    )(dar8, dai8, dbr_e, dbi_e, lr_e, li_e, ldt_e, bre_e, bim_e)


def _expand_b(b):
    G = b.shape[0]
    bt = b.transpose(0, 2, 1).reshape(G // GROUPS_PER_BLOCK, GROUPS_PER_BLOCK, SSM_C, SSM_P)
    eye = jnp.eye(GROUPS_PER_BLOCK, dtype=b.dtype)
    return (bt[:, :, :, None, :] * eye[None, :, None, :, None]).reshape(G // GROUPS_PER_BLOCK, LANES, STATE_LANES)


def _collapse_b(be):
    nbk = be.shape[0]
    eye = jnp.eye(GROUPS_PER_BLOCK, dtype=be.dtype)
    d5 = be.reshape(nbk, GROUPS_PER_BLOCK, SSM_C, GROUPS_PER_BLOCK, SSM_P)
    d4 = (d5 * eye[None, :, None, :, None]).sum(axis=3)
    return d4.transpose(0, 1, 3, 2).reshape(nbk * GROUPS_PER_BLOCK, SSM_P, SSM_C)


def _expand_c(cm):
    G = cm.shape[0]
    ct = cm.transpose(0, 2, 1).reshape(G // GROUPS_PER_BLOCK, GROUPS_PER_BLOCK, SSM_P, SSM_C)
    eye = jnp.eye(GROUPS_PER_BLOCK, dtype=cm.dtype)
    return (ct[:, :, :, None, :] * eye[None, :, None, :, None]).reshape(G // GROUPS_PER_BLOCK, STATE_LANES, LANES)


def _collapse_c(ce):
    nbk = ce.shape[0]
    eye = jnp.eye(GROUPS_PER_BLOCK, dtype=ce.dtype)
    d5 = ce.reshape(nbk, GROUPS_PER_BLOCK, SSM_P, GROUPS_PER_BLOCK, SSM_C)
    d4 = (d5 * eye[None, :, None, :, None]).sum(axis=3)
    return d4.transpose(0, 1, 3, 2).reshape(nbk * GROUPS_PER_BLOCK, SSM_C, SSM_P)


def _place():
    x, y, c = lax.axis_index("x"), lax.axis_index("y"), lax.axis_index("c")
    return x, y, c


def _other_chips(x, y):
    return [(1 - x, y), (x, 1 - y), (1 - x, 1 - y)]


_ANY = pl.BlockSpec(memory_space=pl.ANY)


_HBM = pl.BlockSpec(memory_space=pltpu.HBM)
_SEM = pl.BlockSpec(memory_space=pltpu.SEMAPHORE)
_EFFECT = pltpu.SideEffectType.DATAFLOW_SIDE_EFFECTING
_TOKEN = jax.ShapeDtypeStruct((8, LANES), F32)


def _hbm(a):
    return pltpu.with_memory_space_constraint(a, pltpu.HBM)


def _place_own(src, *, gather, name, tr=512):
    R, C = src.shape[-2:]
    tr = min(tr, R)
    x, y, _ = _place()
    me = (2 * x + y).astype(jnp.int32).reshape(1)

    def body(me_ref, s_ref, o_ref):
        o_ref[...] = s_ref[...].astype(BF16)

    own = pl.BlockSpec((None, tr, C), lambda i, me_ref: (me_ref[0], i, 0))
    grid_spec = pltpu.PrefetchScalarGridSpec(
        num_scalar_prefetch=1, grid=(R // tr,),
        in_specs=[pl.BlockSpec((tr, C), lambda i, me_ref: (i, 0)) if gather else own], out_specs=own)
    return pl.pallas_call(
        body, name=name, grid_spec=grid_spec, out_shape=jax.ShapeDtypeStruct((N_CHIPS, R, C), BF16),
        compiler_params=_cparams("parallel"))(me, src)


def _exchange_copy(src_slot, land_slot, send, recv, k, j, peer, c):
    return pltpu.make_async_remote_copy(
        src_ref=src_slot, dst_ref=land_slot, send_sem=send.at[3 * k + j], recv_sem=recv.at[3 * k + j],
        device_id=(peer[0], peer[1], c), device_id_type=MESH)


def _exchange_start(lands, srcs, groups, *, name):
    n, ng = len(lands), len(groups)
    bufs = list(lands) + list(srcs)
    nb = len(bufs)

    def body(*refs):
        lnd, src, sems = refs[:n], refs[n:nb], refs[nb:nb + 2 * ng]
        token = refs[2 * nb + 2 * ng]
        x, y, c = _place()
        me = 2 * x + y
        for gi, group in enumerate(groups):
            for k, w in enumerate(group):
                for j, peer in enumerate(_other_chips(x, y)):
                    sent = src[w].at[2 * peer[0] + peer[1]] if src else lnd[w].at[me]
                    _exchange_copy(sent, lnd[w].at[me], sems[2 * gi], sems[2 * gi + 1], k, j, peer, c).start()
        token[...] = jnp.zeros_like(token)

    sem_shapes = [pltpu.SemaphoreType.DMA((3 * len(g),)) for g in groups for _ in range(2)]
    res = pl.pallas_call(
        body, name=name,
        out_shape=sem_shapes + [pltpu.HBM(a.shape, a.dtype) for a in bufs] + [_TOKEN],
        in_specs=[_HBM] * nb,
        out_specs=[_SEM] * (2 * ng) + [_HBM] * nb + [pl.BlockSpec(memory_space=pltpu.VMEM)],
        input_output_aliases={i: 2 * ng + i for i in range(nb)},
        compiler_params=pltpu.CompilerParams(has_side_effects=_EFFECT),
    )(*[_hbm(a) for a in bufs])
    sems = [(res[2 * gi], res[2 * gi + 1]) for gi in range(ng)]
    return sems, res[2 * ng:2 * ng + n], res[2 * ng + n:2 * ng + nb], res[-1]


def _exchange_wait(lands, srcs, sems, after, *, name):
    n = len(lands)
    bufs = list(lands) + list(srcs)
    nb = len(bufs)
    send_sems, recv_sems = sems

    def body(*refs):
        lnd, src, send, recv = refs[:n], refs[n:nb], refs[nb], refs[nb + 1]
        x, y, c = _place()
        for k in range(n):
            for j, peer in enumerate(_other_chips(x, y)):
                slot = 2 * peer[0] + peer[1]
                copy = _exchange_copy((src[k] if src else lnd[k]).at[slot], lnd[k].at[slot], send, recv, k, j, peer, c)
                copy.wait_send()
                copy.wait_recv()

    res = pl.pallas_call(
        body, name=name, out_shape=[pltpu.HBM(a.shape, a.dtype) for a in bufs],
        in_specs=[_HBM] * nb + [_SEM, _SEM, _ANY], out_specs=[_HBM] * nb,
        input_output_aliases={i: i for i in range(nb)},
        compiler_params=pltpu.CompilerParams(has_side_effects=_EFFECT),
    )(*bufs, send_sems, recv_sems, after)
    return res[:n]


def _sum_partials(land, *, name, tr=256):
    _, R, C = land.shape
    tr = min(tr, R)

    def body(l_ref, o_ref):
        acc = l_ref[0].astype(F32)
        for k in range(1, N_CHIPS):
            acc = acc + l_ref[k].astype(F32)
        o_ref[...] = acc

    return pl.pallas_call(
        body, name=name, out_shape=jax.ShapeDtypeStruct((R, C), F32), grid=(R // tr,),
        in_specs=[pl.BlockSpec((N_CHIPS, tr, C), lambda i: (0, i, 0))], out_specs=_rows(tr, C),
        compiler_params=_cparams("parallel"))(land)


def _swap_with_sibling(sums, *, name):
    n = len(sums)

    def body(*refs):
        ins, outs = refs[:n], refs[n:2 * n]
        send_sems, recv_sems = refs[2 * n:]
        x, y, c = _place()
        copies = [pltpu.make_async_remote_copy(
            src_ref=ins[w], dst_ref=outs[w], send_sem=send_sems.at[w], recv_sem=recv_sems.at[w],
            device_id=(x, y, 1 - c), device_id_type=MESH) for w in range(n)]
        for cp in copies:
            cp.start()
        for cp in copies:
            cp.wait_recv()
            cp.wait_send()

    return pl.pallas_call(
        body, name=name,
        out_shape=[jax.ShapeDtypeStruct(s.shape, s.dtype) for s in sums],
        in_specs=[_ANY] * n, out_specs=[_ANY] * n,
        scratch_shapes=[pltpu.SemaphoreType.DMA((n,)), pltpu.SemaphoreType.DMA((n,))],
    )(*sums)


def _adamw_math(w, g, m, v):
    m = ADAM_B1 * m + (1.0 - ADAM_B1) * g
    v = ADAM_B2 * v + (1.0 - ADAM_B2) * (g * g)
    m_hat = m / (1.0 - ADAM_B1 ** ADAM_STEP)
    v_hat = v / (1.0 - ADAM_B2 ** ADAM_STEP)
    delta = -ADAM_LR * (m_hat / (jnp.sqrt(v_hat) + ADAM_EPS) + ADAM_WD * w)
    return delta, m, v


def _adamw_pair(mine, theirs, w, m, v, *, name, tr=128):
    R, C = w.shape
    tr = min(tr, R)

    def body(a_ref, b_ref, w_ref, m_ref, v_ref, g_ref, d_ref, nm_ref, nv_ref):
        g = a_ref[...] + b_ref[...]
        g_ref[...] = g
        d_ref[...], nm_ref[...], nv_ref[...] = _adamw_math(w_ref[...], g, m_ref[...], v_ref[...])

    shape = jax.ShapeDtypeStruct((R, C), F32)
    return pl.pallas_call(
        body, name=name, out_shape=(shape,) * 4, grid=(R // tr,),
        in_specs=[_rows(tr, C)] * 5, out_specs=(_rows(tr, C),) * 4,
        compiler_params=_cparams("parallel"))(mine, theirs, w, m, v)


def _all_gather_small(packed):
    R = packed.shape[0]

    def body(x_ref, out_ref, send_sems, recv_sems, local_sem):
        x, y, c = _place()

        def slot(px, py, pc):
            return out_ref.at[4 * px + 2 * py + pc]

        local = pltpu.make_async_copy(x_ref, slot(x, y, c), local_sem)
        local.start()
        peers = [(x ^ (k >> 2), y ^ ((k >> 1) & 1), c ^ (k & 1)) for k in range(1, N_DEV)]
        for k, peer in enumerate(peers):
            pltpu.make_async_remote_copy(
                src_ref=x_ref, dst_ref=slot(x, y, c), send_sem=send_sems.at[k], recv_sem=recv_sems.at[k],
                device_id=peer, device_id_type=MESH).start()
        for k, peer in enumerate(peers):
            arrival = pltpu.make_async_remote_copy(
                src_ref=x_ref, dst_ref=slot(*peer), send_sem=send_sems.at[k], recv_sem=recv_sems.at[k],
                device_id=peer, device_id_type=MESH)
            arrival.wait_recv()
            arrival.wait_send()
        local.wait()

    vm = pl.BlockSpec(memory_space=pltpu.VMEM)
    return pl.pallas_call(
        body, name="all_gather_small", out_shape=jax.ShapeDtypeStruct((N_DEV, R, LANES), F32),
        in_specs=[vm], out_specs=vm,
        scratch_shapes=[pltpu.SemaphoreType.DMA((N_DEV - 1,)), pltpu.SemaphoreType.DMA((N_DEV - 1,)),
                        pltpu.SemaphoreType.DMA],
        compiler_params=pltpu.CompilerParams(vmem_limit_bytes=VMEM_LIMIT_BYTES),
    )(packed)


def _adamw_small(gathered, w, m, v):
    _, R, _ = gathered.shape
    tr = PACK_ROWS

    def body(gs_ref, w_ref, m_ref, v_ref, g_ref, d_ref, nm_ref, nv_ref):
        g = gs_ref[0]
        for k in range(1, N_DEV):
            g = g + gs_ref[k]
        g_ref[...] = g
        d_ref[...], nm_ref[...], nv_ref[...] = _adamw_math(w_ref[...], g, m_ref[...], v_ref[...])

    shape = jax.ShapeDtypeStruct((R, LANES), F32)
    return pl.pallas_call(
        body, name="adamw_small", out_shape=(shape,) * 4, grid=(R // tr,),
        in_specs=[pl.BlockSpec((N_DEV, tr, LANES), lambda i: (0, i, 0))] + [_rows(tr, LANES)] * 3,
        out_specs=(_rows(tr, LANES),) * 4, compiler_params=_cparams("parallel"))(gathered, w, m, v)


def _pack(arrays):
    parts, layout = [], []
    for a in arrays:
        n = a.size
        rows = -(-n // (8 * LANES)) * 8
        flat = jnp.pad(a.reshape(-1).astype(F32), (0, rows * LANES - n))
        parts.append(flat.reshape(rows, LANES))
        layout.append((rows, n, a.shape))
    total = sum(r for r, _, _ in layout)
    parts.append(jnp.zeros((-total % PACK_ROWS, LANES), F32))
    return jnp.concatenate(parts, axis=0), layout


def _unpack(buf, layout):
    out, r0 = [], 0
    for rows, n, shape in layout:
        out.append(buf[r0:r0 + rows].reshape(-1)[:n].reshape(shape))
        r0 += rows
    return out


SMALL = ("mix_norm_pre", "lam_re", "lam_im", "log_dt", "ssm_b_re", "ssm_b_im", "ssm_c_re", "ssm_c_im",
         "ssm_d", "b_glu", "attn_out_norm", "ssm_out_norm", "mix_norm_post", "mlp_norm_pre",
         "mlp_norm_post", "ple_norm_pre", "ple_norm_post")
BIG = ("w_in", "w_glu", "w_out", "w_up", "w_down", "w_ple_gate", "w_ple_proj")
WEIGHTS = ("mix_norm_pre", "w_in", "lam_re", "lam_im", "log_dt", "ssm_b_re", "ssm_b_im", "ssm_c_re",
           "ssm_c_im", "ssm_d", "w_glu", "b_glu", "attn_out_norm", "ssm_out_norm", "w_out",
           "mix_norm_post", "mlp_norm_pre", "w_up", "w_down", "mlp_norm_post", "ple_norm_pre",
           "w_ple_gate", "w_ple_proj", "ple_norm_post")


def _to_branch(a, d):
    return a if d == 1 else a.reshape(a.shape[0] // d, d * a.shape[1])


def _from_branch(a, d, S):
    return a if d == 1 else a.reshape(S, a.shape[1] // d)


def kernel(x, p, mix_norm_pre, w_in, lam_re, lam_im, log_dt, ssm_b_re, ssm_b_im, ssm_c_re, ssm_c_im, ssm_d, w_glu, b_glu, attn_out_norm, ssm_out_norm, w_out, mix_norm_post, mlp_norm_pre, w_up, w_down, mlp_norm_post, ple_norm_pre, w_ple_gate, w_ple_proj, ple_norm_post, loss_target, m_mix_norm_pre, m_w_in, m_lam_re, m_lam_im, m_log_dt, m_ssm_b_re, m_ssm_b_im, m_ssm_c_re, m_ssm_c_im, m_ssm_d, m_w_glu, m_b_glu, m_attn_out_norm, m_ssm_out_norm, m_w_out, m_mix_norm_post, m_mlp_norm_pre, m_w_up, m_w_down, m_mlp_norm_post, m_ple_norm_pre, m_w_ple_gate, m_w_ple_proj, m_ple_norm_post, v_mix_norm_pre, v_w_in, v_lam_re, v_lam_im, v_log_dt, v_ssm_b_re, v_ssm_b_im, v_ssm_c_re, v_ssm_c_im, v_ssm_d, v_w_glu, v_b_glu, v_attn_out_norm, v_ssm_out_norm, v_w_out, v_mix_norm_post, v_mlp_norm_pre, v_w_up, v_w_down, v_mlp_norm_post, v_ple_norm_pre, v_w_ple_gate, v_w_ple_proj, v_ple_norm_post):
    args = dict(locals())
    W = {n: args[n][0] for n in WEIGHTS}
    Mo = {n: args["m_" + n][0] for n in WEIGHTS}
    Vo = {n: args["v_" + n][0] for n in WEIGHTS}
    xs, ps, tgt = x[0], p[0, 0], loss_target[0]
    S, D = xs.shape
    SW = W["ssm_d"].shape[0]
    AW = W["attn_out_norm"].shape[0]
    heads = AW // HEAD_DIM
    G = SW // SSM_C
    nbk = SW // LANES
    assert W["w_in"].shape[1] * N_CHIPS == 3 * AW + SW and AW == SW

    row = lambda a: a.reshape(1, -1)

    ag_groups = (("w_in",), ("w_glu", "w_out"), ("w_up",), ("w_down", "w_ple_gate", "w_ple_proj"))
    ag_names = [n for g in ag_groups for n in g]
    ag_sems, ag_land, _, ag_token = _exchange_start(
        [_place_own(W[n], gather=True, name="ag_place_" + n) for n in ag_names], [],
        [[ag_names.index(n) for n in g] for g in ag_groups], name="ag_start")

    def gathered(gi, after):
        got = _exchange_wait([ag_land[ag_names.index(n)] for n in ag_groups[gi]], [], ag_sems[gi], after,
                             name=f"ag_wait_{gi}")
        return dict(zip(ag_groups[gi], got))

    lr_e = W["lam_re"].reshape(nbk, 1, STATE_LANES)
    li_e = W["lam_im"].reshape(nbk, 1, STATE_LANES)
    ldt_e = jnp.repeat(W["log_dt"], SSM_P).reshape(nbk, 1, STATE_LANES)
    bre_e, bim_e = _expand_b(W["ssm_b_re"]), _expand_b(W["ssm_b_im"])
    cre_e, cim_e = _expand_c(W["ssm_c_re"]), _expand_c(W["ssm_c_im"])
    d_row = row(W["ssm_d"])

    hn1 = _norm_cast(xs, row(W["mix_norm_pre"]) + ag_token[0, 0], name="norm_in")
    w_in_f = gathered(0, hn1)["w_in"]
    qkv = _matmul(hn1, w_in_f, name="proj_qkv", out_dtype=BF16, b_shards=N_CHIPS, b_cols=(0, 3 * AW))
    u = _matmul(hn1, w_in_f, name="proj_u", b_shards=N_CHIPS, b_cols=(3 * AW, SW))
    qkv_b = [_to_branch(qkv, d) for d in DILATIONS]
    outs, lses = [], []
    for d, qb in zip(DILATIONS, qkv_b):
        o, l = _attn_fwd(qb, d, heads)
        outs.append(_from_branch(o, d, S))
        lses.append(_from_branch(l, d, S))
    y1, ends_r, ends_i = _ssm_fwd(u, lr_e, li_e, ldt_e, bre_e, bim_e, cre_e, cim_e, d_row)
    y2b = _gelu_cast(y1)
    full = gathered(1, y2b)
    w_glu_f = full["w_glu"].reshape(SW, SW)
    w_out_f = full["w_out"].reshape(AW + SW, D)
    z = _matmul(y2b, w_glu_f, name="glu_z")
    attn, lse, mixed = _mix_fwd(outs, lses, y1, z, row(W["b_glu"]), row(W["attn_out_norm"]), row(W["ssm_out_norm"]))
    mo = _matmul(mixed, w_out_f, name="mix_out")
    h1, hn2 = _res_norm(xs, mo, row(W["mix_norm_post"]), row(W["mlp_norm_pre"]), name="res_mix")
    w_up_f = gathered(2, hn2)["w_up"]
    up = _matmul(hn2, w_up_f, name="mlp_up", b_shards=N_CHIPS)
    act = _relu2(up)
    full = gathered(3, act)
    w_down_f = full["w_down"].reshape(-1, D)
    w_pg_f = full["w_ple_gate"].reshape(D, D)
    w_pp_f = full["w_ple_proj"]
    ff = _matmul(act, w_down_f, name="mlp_down")
    h2, hn3 = _res_norm(h1, ff, row(W["mlp_norm_post"]), row(W["ple_norm_pre"]), name="res_mlp")
    gl = _matmul(hn3, w_pg_f, name="ple_gate")
    e = _matmul(ps.astype(BF16), w_pp_f, name="ple_proj", b_shards=N_CHIPS)

    dh3, dgl, de, loss_part, dg_ple_post = _final(h2, gl, e, row(W["ple_norm_post"]), tgt)
    gW = {}
    out_g, out_d, out_m, out_v = {}, {}, {}, {}

    def scatter_start(names, tag):
        parts = [gW[n] if gW[n].ndim == 3 else gW[n].reshape((N_CHIPS, -1, gW[n].shape[1])) for n in names]
        sems, land, src, token = _exchange_start(
            [_place_own(part, gather=False, name="rs_place_" + n) for n, part in zip(names, parts)], parts,
            [list(range(len(names)))], name=f"rs_start_{tag}")
        return (names, sems[0], land, src), token

    def scatter_finish(batch, after, tag):
        names, sems, land, src = batch
        landed = _exchange_wait(land, src, sems, after, name=f"rs_wait_{tag}")
        sums = [_sum_partials(l, name="sum_" + n) for n, l in zip(names, landed)]
        theirs = _swap_with_sibling(sums, name=f"swap_{tag}")
        for n, a, b in zip(names, sums, theirs):
            out_g[n], out_d[n], out_m[n], out_v[n] = _adamw_pair(a, b, W[n], Mo[n], Vo[n], name="adamw_" + n)

    gW["w_ple_proj"] = _matmul(ps.astype(BF16), de, name="d_w_ple_proj", ta=True, out_dtype=BF16, out_shards=N_CHIPS)
    gW["w_ple_gate"] = _matmul(hn3, dgl, name="d_w_ple_gate", ta=True, out_dtype=BF16)
    dhn3 = _matmul(dgl, w_pg_f, name="d_hn3", tb=True)
    dh2, dff, dg_ple_pre, dg_mlp_post = _bwd_res_norm(
        dh3, dhn3, h2, row(W["ple_norm_pre"]), ff, row(W["mlp_norm_post"]), name="bwd_res_mlp")
    gW["w_down"] = _matmul(act, dff, name="d_w_down", ta=True, out_dtype=BF16)
    batch1, token1 = scatter_start(("w_ple_proj", "w_ple_gate", "w_down"), 1)
    dact = _matmul(dff, w_down_f, name="d_act", tb=True, after=token1)
    dup = _relu2_bwd(dact, up)
    gW["w_up"] = _matmul(hn2, dup, name="d_w_up", ta=True, out_dtype=BF16, out_shards=N_CHIPS)
    dhn2 = _matmul(dup, w_up_f, name="d_hn2", tb=True, b_shards=N_CHIPS)
    dh1, dmo, dg_mlp_pre, dg_mix_post = _bwd_res_norm(
        dh2, dhn2, h1, row(W["mlp_norm_pre"]), mo, row(W["mix_norm_post"]), name="bwd_res_mix")
    gW["w_out"] = _matmul(mixed, dmo, name="d_w_out", ta=True, out_dtype=BF16)
    dmixed = _matmul(dmo, w_out_f, name="d_mixed", tb=True)
    dattn, dd, dz, dy2a, dg_attn, dg_ssm, db_glu = _mix_bwd(
        dmixed, attn, y1, z, row(W["b_glu"]), row(W["attn_out_norm"]), row(W["ssm_out_norm"]))
    gW["w_glu"] = _matmul(y2b, dz, name="d_w_glu", ta=True, out_dtype=BF16)
    batch2, token2 = scatter_start(("w_up", "w_out", "w_glu"), 2)
    dy2b = _matmul(dz, w_glu_f, name="d_y2", tb=True, after=token2)
    du, dar8, dai8, dcr_e, dci_e, dbr_e, dbi_e, dd8 = _ssm_bwd(
        u, y1, dy2a, dy2b, ends_r, ends_i, lr_e, li_e, ldt_e, bre_e, bim_e, cre_e, cim_e, d_row)
    scatter_finish(batch1, du, 1)
    dlr_e, dli_e, dldt_e, dbre_e, dbim_e = _ssm_param_bwd(dar8, dai8, dbr_e, dbi_e, lr_e, li_e, ldt_e, bre_e, bim_e)

    grads5 = [[], [], [], [], []]
    for d, qb in zip(DILATIONS, qkv_b):
        res = _attn_bwd(qb, _to_branch(dattn, d), _to_branch(lse, d), _to_branch(dd, d), d, heads)
        for lst, a in zip(grads5, res):
            lst.append(_from_branch(a, d, S))
    dproj = _dproj_join(*grads5, du)
    scatter_finish(batch2, dproj, 2)
    gW["w_in"] = _matmul(hn1, dproj, name="d_w_in", ta=True, out_dtype=BF16, out_shards=N_CHIPS)
    batch3, token3 = scatter_start(("w_in",), 3)
    dhn1 = _matmul(dproj, w_in_f, name="d_hn1", tb=True, b_shards=N_CHIPS, after=token3)
    grad_x, dg_mix_pre = _bwd_first(dh1, dhn1, xs, row(W["mix_norm_pre"]))
    scatter_finish(batch3, grad_x, 3)

    small_g = {
        "mix_norm_pre": dg_mix_pre, "lam_re": dlr_e.reshape(G, SSM_P), "lam_im": dli_e.reshape(G, SSM_P),
        "log_dt": dldt_e.reshape(G, SSM_P)[:, 0], "ssm_b_re": _collapse_b(dbre_e), "ssm_b_im": _collapse_b(dbim_e),
        "ssm_c_re": _collapse_c(dcr_e), "ssm_c_im": _collapse_c(dci_e), "ssm_d": dd8.sum(axis=1).reshape(-1),
        "b_glu": db_glu, "attn_out_norm": dg_attn, "ssm_out_norm": dg_ssm, "mix_norm_post": dg_mix_post,
        "mlp_norm_pre": dg_mlp_pre, "mlp_norm_post": dg_mlp_post, "ple_norm_pre": dg_ple_pre,
        "ple_norm_post": dg_ple_post,
    }
    g_pack, layout = _pack([small_g[n].reshape(W[n].shape) for n in SMALL])
    w_pack, _ = _pack([W[n] for n in SMALL])
    m_pack, _ = _pack([Mo[n] for n in SMALL])
    v_pack, _ = _pack([Vo[n] for n in SMALL])
    packed = _adamw_small(_all_gather_small(g_pack), w_pack, m_pack, v_pack)
    for dst, buf in zip((out_g, out_d, out_m, out_v), packed):
        dst.update(zip(SMALL, _unpack(buf, layout)))

    loss = lax.psum(loss_part[0, 0], ("x", "y", "c"))
    lead = lambda a: a[None]
    return (loss, grad_x[None],
            *[lead(out_g[n]) for n in WEIGHTS], *[lead(out_d[n]) for n in WEIGHTS],
            *[lead(out_m[n]) for n in WEIGHTS], *[lead(out_v[n]) for n in WEIGHTS])
```

```python
import functools
import math

import jax
import jax.numpy as jnp
from jax import lax
from jax.experimental import pallas as pl
from jax.experimental.pallas import tpu as pltpu

F32 = jnp.float32
BF16 = jnp.bfloat16
MESH = pl.DeviceIdType.MESH

RMS_EPS = 1e-6
NEG_INF = -1e30
HEAD_DIM = 128
BLK = 128
DILATIONS = (1, 4, 16)
SSM_C = 16
SSM_P = 64
LANES = 128
GROUPS_PER_BLOCK = LANES // SSM_C
STATE_LANES = GROUPS_PER_BLOCK * SSM_P
SSM_CHUNK = 128
ADAM_LR, ADAM_B1, ADAM_B2, ADAM_EPS, ADAM_WD, ADAM_STEP = 1e-3, 0.9, 0.999, 1e-8, 0.01, 10
VMEM_LIMIT_BYTES = 56 * 1024 * 1024
N_CHIPS = 4
N_DEV = 8
PACK_ROWS = 256


def _cparams(*sem):
    return pltpu.CompilerParams(dimension_semantics=sem or None, vmem_limit_bytes=VMEM_LIMIT_BYTES)


def _rows(tr, w):
    return pl.BlockSpec((tr, w), lambda i: (i, 0))


def _vec(w):
    return pl.BlockSpec((1, w), lambda i: (0, 0))


def _sigmoid(x):
    return 1.0 / (1.0 + jnp.exp(-x))


def _gelu(x):
    c = math.sqrt(2.0 / math.pi)
    return 0.5 * x * (1.0 + jnp.tanh(c * (x + 0.044715 * x * x * x)))


def _gelu_grad(x):
    c = math.sqrt(2.0 / math.pi)
    th = jnp.tanh(c * (x + 0.044715 * x * x * x))
    return 0.5 * (1.0 + th) + 0.5 * x * (1.0 - th * th) * c * (1.0 + 3.0 * 0.044715 * x * x)


def _rms(x, g):
    r = lax.rsqrt(jnp.mean(x * x, axis=-1, keepdims=True) + RMS_EPS)
    return x * r * g


def _rms_bwd(dy, x, g):
    r = lax.rsqrt(jnp.mean(x * x, axis=-1, keepdims=True) + RMS_EPS)
    n = x * r
    dn = dy * g
    dx = r * (dn - n * jnp.mean(dn * n, axis=-1, keepdims=True))
    return dx, dy * n


def _colsum(a):
    return jnp.sum(a, axis=0, keepdims=True)


def _first(i):
    return i == 0


def _matmul(a, b, *, name, ta=False, tb=False, out_dtype=F32, b_shards=1, out_shards=1, b_cols=None,
            after=None, relu2=False, relu2_of=None, tm=1024, tn=1024, tk=2048):
    if ta:
        K, M = a.shape
    else:
        M, K = a.shape
    if b_shards > 1:
        rows, cols = b.shape[1], b.shape[2] * b_shards
    else:
        rows, cols = b.shape
    N, Kb = (rows, cols) if tb else (cols, rows)
    assert K == Kb, (a.shape, b.shape, ta, tb)
    col0 = 0
    if b_cols is not None:
        assert not tb
        col0, N = b_cols
    tm, tn, tk = min(tm, M), min(tn, N), min(tk, K)
    if b_shards > 1:
        shard_cols = cols // b_shards
        if tb:
            tk = min(tk, shard_cols)
        else:
            tn = min(tn, shard_cols)
    if out_shards > 1:
        tn = min(tn, N // out_shards)
    assert M % tm == 0 and N % tn == 0 and K % tk == 0 and col0 % tn == 0
    nk = K // tk
    j0 = col0 // tn

    a_spec = (pl.BlockSpec((tk, tm), lambda i, j, k: (k, i)) if ta
              else pl.BlockSpec((tm, tk), lambda i, j, k: (i, k)))
    if b_shards > 1:
        if tb:
            per = shard_cols // tk
            b_spec = pl.BlockSpec((None, tn, tk), lambda i, j, k: (k // per, j, k % per))
        else:
            per = shard_cols // tn
            b_spec = pl.BlockSpec((None, tk, tn), lambda i, j, k: ((j + j0) // per, k, (j + j0) % per))
    else:
        b_spec = (pl.BlockSpec((tn, tk), lambda i, j, k: (j, k)) if tb
                  else pl.BlockSpec((tk, tn), lambda i, j, k: (k, j + j0)))
    if out_shards > 1:
        per_o = (N // out_shards) // tn
        out_shape = jax.ShapeDtypeStruct((out_shards, M, N // out_shards), out_dtype)
        out_spec = pl.BlockSpec((None, tm, tn), lambda i, j, k: (j // per_o, i, j % per_o))
    else:
        out_shape = jax.ShapeDtypeStruct((M, N), out_dtype)
        out_spec = pl.BlockSpec((tm, tn), lambda i, j, k: (i, j))
    dims = (((0 if ta else 1,), (1 if tb else 0,)), ((), ()))

    extra, extra_specs = [], []
    if relu2_of is not None:
        assert out_shards == 1 and relu2_of.shape == (M, N)
        extra.append(relu2_of)
        extra_specs.append(pl.BlockSpec((tm, tn), lambda i, j, k: (i, j)))
    if after is not None:
        extra.append(after)
        extra_specs.append(pl.BlockSpec(after.shape, lambda i, j, k: (0, 0)))
    n_in = 2 + len(extra)
    if relu2:
        assert out_shards == 1
        out_shape = (out_shape, jax.ShapeDtypeStruct((M, N), BF16))
        out_spec = (out_spec, out_spec)

    def finish(acc, refs):
        o_ref = refs[n_in]
        if relu2_of is not None:
            acc = acc * (2.0 * jnp.maximum(refs[2][...], 0.0))
        o_ref[...] = acc.astype(o_ref.dtype)
        if relu2:
            r = jnp.maximum(acc, 0.0)
            refs[n_in + 1][...] = (r * r).astype(BF16)

    def body(*refs):
        prod = lax.dot_general(refs[0][...], refs[1][...], dims, preferred_element_type=F32)
        if nk == 1:
            finish(prod, refs)
            return
        acc_ref = refs[-1]
        k = pl.program_id(2)

        @pl.when(k == 0)
        def _():
            acc_ref[...] = prod

        @pl.when(k > 0)
        def _():
            acc_ref[...] += prod

        @pl.when(k == nk - 1)
        def _():
            finish(acc_ref[...], refs)

    return pl.pallas_call(
        body, name=name, out_shape=out_shape, grid=(M // tm, N // tn, nk),
        in_specs=[a_spec, b_spec] + extra_specs, out_specs=out_spec,
        scratch_shapes=[pltpu.VMEM((tm, tn), F32)] if nk > 1 else [],
        compiler_params=_cparams("parallel", "parallel", "arbitrary"),
    )(a, b, *extra)


def _norm_cast(x, g, *, name, tr=256):
    S, D = x.shape
    tr = min(tr, S)

    def body(x_ref, g_ref, o_ref):
        o_ref[...] = _rms(x_ref[...], g_ref[...]).astype(BF16)

    return pl.pallas_call(
        body, name=name, out_shape=jax.ShapeDtypeStruct((S, D), BF16), grid=(S // tr,),
        in_specs=[_rows(tr, D), _vec(D)], out_specs=_rows(tr, D),
        compiler_params=_cparams("parallel"))(x, g)


def _res_norm(res, y, g_post, g_next, *, name, tr=256):
    S, D = res.shape
    tr = min(tr, S)

    def body(res_ref, y_ref, gp_ref, gn_ref, h_ref, hn_ref):
        h = res_ref[...] + _rms(y_ref[...], gp_ref[...])
        h_ref[...] = h
        hn_ref[...] = _rms(h, gn_ref[...]).astype(BF16)

    return pl.pallas_call(
        body, name=name,
        out_shape=(jax.ShapeDtypeStruct((S, D), F32), jax.ShapeDtypeStruct((S, D), BF16)),
        grid=(S // tr,), in_specs=[_rows(tr, D), _rows(tr, D), _vec(D), _vec(D)],
        out_specs=(_rows(tr, D), _rows(tr, D)), compiler_params=_cparams("parallel"))(res, y, g_post, g_next)


def _gelu_cast(y1, *, tr=256):
    S, W = y1.shape
    tr = min(tr, S)

    def body(y_ref, o_ref):
        o_ref[...] = _gelu(y_ref[...]).astype(BF16)

    return pl.pallas_call(
        body, name="gelu_cast", out_shape=jax.ShapeDtypeStruct((S, W), BF16), grid=(S // tr,),
        in_specs=[_rows(tr, W)], out_specs=_rows(tr, W), compiler_params=_cparams("parallel"))(y1)


def _mix_fwd(os, ls, y1, z, b_glu, g_attn, g_ssm, *, tr=128):
    S, AW = os[0].shape
    SW = y1.shape[1]
    tr = min(tr, S)

    def body(o1, o2, o3, l1, l2, l3, y_ref, z_ref, b_ref, ga_ref, gs_ref, attn_ref, lse_ref, mixed_ref):
        la, lb, lc = l1[...], l2[...], l3[...]
        m = jnp.maximum(jnp.maximum(la, lb), lc)
        ea, eb, ec = jnp.exp(la - m), jnp.exp(lb - m), jnp.exp(lc - m)
        tot = ea + eb + ec
        attn = (ea * o1[...] + eb * o2[...] + ec * o3[...]) / tot
        attn_ref[...] = attn
        lse_ref[...] = m + jnp.log(tot)
        ssm = _gelu(y_ref[...]) * _sigmoid(z_ref[...] + b_ref[...])
        mixed_ref[:, :AW] = _rms(attn, ga_ref[...]).astype(BF16)
        mixed_ref[:, AW:] = _rms(ssm, gs_ref[...]).astype(BF16)

    return pl.pallas_call(
        body, name="mix_fwd",
        out_shape=(jax.ShapeDtypeStruct((S, AW), F32), jax.ShapeDtypeStruct((S, AW), F32),
                   jax.ShapeDtypeStruct((S, AW + SW), BF16)),
        grid=(S // tr,),
        in_specs=[_rows(tr, AW)] * 6 + [_rows(tr, SW), _rows(tr, SW), _vec(SW), _vec(AW), _vec(SW)],
        out_specs=(_rows(tr, AW), _rows(tr, AW), _rows(tr, AW + SW)),
        compiler_params=_cparams("parallel"))(*os, *ls, y1, z, b_glu, g_attn, g_ssm)


def _final(h2, gl, e, g_post, target, *, tr=128):
    S, D = h2.shape
    tr = min(tr, S)

    def body(h_ref, gl_ref, e_ref, g_ref, t_ref, dh_ref, dgl_ref, de_ref, loss_ref, dg_ref):
        i = pl.program_id(0)
        gate = _sigmoid(gl_ref[...])
        e_ = e_ref[...]
        ge = gate * e_
        g = g_ref[...]
        diff = h_ref[...] + _rms(ge, g) - t_ref[...]
        dh = diff * (1.0 / D)
        dh_ref[...] = dh
        dge, dgrow = _rms_bwd(dh, ge, g)
        dgl_ref[...] = (dge * e_ * gate * (1.0 - gate)).astype(BF16)
        de_ref[...] = (dge * gate).astype(BF16)
        part = _colsum(0.5 * jnp.mean(diff * diff, axis=-1, keepdims=True))

        @pl.when(_first(i))
        def _():
            loss_ref[...] = jnp.zeros_like(loss_ref)
            dg_ref[...] = jnp.zeros_like(dg_ref)

        loss_ref[...] += part + jnp.zeros((1, LANES), F32)
        dg_ref[...] += _colsum(dgrow)

    return pl.pallas_call(
        body, name="final_fwd_bwd",
        out_shape=(jax.ShapeDtypeStruct((S, D), F32), jax.ShapeDtypeStruct((S, D), BF16),
                   jax.ShapeDtypeStruct((S, D), BF16), jax.ShapeDtypeStruct((1, LANES), F32),
                   jax.ShapeDtypeStruct((1, D), F32)),
        grid=(S // tr,),
        in_specs=[_rows(tr, D), _rows(tr, D), _rows(tr, D), _vec(D), _rows(tr, D)],
        out_specs=(_rows(tr, D), _rows(tr, D), _rows(tr, D), _vec(LANES), _vec(D)),
        compiler_params=_cparams("arbitrary"))(h2, gl, e, g_post, target)


def _bwd_res_norm(dh_out, dhn, h, g_next, y, g_post, *, name, tr=128):
    S, D = h.shape
    tr = min(tr, S)

    def body(dho_ref, dhn_ref, h_ref, gn_ref, y_ref, gp_ref, dh_ref, dy_ref, dgn_ref, dgp_ref):
        i = pl.program_id(0)
        dx, dgn_rows = _rms_bwd(dhn_ref[...], h_ref[...], gn_ref[...])
        dh = dho_ref[...] + dx
        dh_ref[...] = dh
        dy, dgp_rows = _rms_bwd(dh, y_ref[...], gp_ref[...])
        dy_ref[...] = dy.astype(BF16)

        @pl.when(_first(i))
        def _():
            dgn_ref[...] = jnp.zeros_like(dgn_ref)
            dgp_ref[...] = jnp.zeros_like(dgp_ref)

        dgn_ref[...] += _colsum(dgn_rows)
        dgp_ref[...] += _colsum(dgp_rows)

    return pl.pallas_call(
        body, name=name,
        out_shape=(jax.ShapeDtypeStruct((S, D), F32), jax.ShapeDtypeStruct((S, D), BF16),
                   jax.ShapeDtypeStruct((1, D), F32), jax.ShapeDtypeStruct((1, D), F32)),
        grid=(S // tr,),
        in_specs=[_rows(tr, D), _rows(tr, D), _rows(tr, D), _vec(D), _rows(tr, D), _vec(D)],
        out_specs=(_rows(tr, D), _rows(tr, D), _vec(D), _vec(D)),
        compiler_params=_cparams("arbitrary"))(dh_out, dhn, h, g_next, y, g_post)


def _bwd_first(dh1, dhn1, x, g1, *, tr=256):
    S, D = x.shape
    tr = min(tr, S)

    def body(dh_ref, dhn_ref, x_ref, g_ref, dx_ref, dg_ref):
        i = pl.program_id(0)
        dx, dg_rows = _rms_bwd(dhn_ref[...], x_ref[...], g_ref[...])
        dx_ref[...] = dh_ref[...] + dx

        @pl.when(_first(i))
        def _():
            dg_ref[...] = jnp.zeros_like(dg_ref)

        dg_ref[...] += _colsum(dg_rows)

    return pl.pallas_call(
        body, name="bwd_first",
        out_shape=(jax.ShapeDtypeStruct((S, D), F32), jax.ShapeDtypeStruct((1, D), F32)),
        grid=(S // tr,), in_specs=[_rows(tr, D), _rows(tr, D), _rows(tr, D), _vec(D)],
        out_specs=(_rows(tr, D), _vec(D)), compiler_params=_cparams("arbitrary"))(dh1, dhn1, x, g1)


def _mix_bwd(dmixed, attn, y1, z, b_glu, g_attn, g_ssm, *, tr=128):
    S, AW = attn.shape
    SW = y1.shape[1]
    tr = min(tr, S)
    heads = AW // HEAD_DIM

    def body(dm_ref, a_ref, y_ref, z_ref, b_ref, ga_ref, gs_ref,
             da_ref, dd_ref, dz_ref, dy2_ref, dga_ref, dgs_ref, db_ref):
        i = pl.program_id(0)
        attn_ = a_ref[...]
        dattn, dga_rows = _rms_bwd(dm_ref[:, :AW], attn_, ga_ref[...])
        da_ref[...] = dattn.astype(BF16)
        prod = dattn * attn_
        for h in range(heads):
            sl = slice(h * HEAD_DIM, (h + 1) * HEAD_DIM)
            dd_ref[:, sl] = jnp.broadcast_to(jnp.sum(prod[:, sl], axis=-1, keepdims=True), (tr, HEAD_DIM))
        y2 = _gelu(y_ref[...])
        gate = _sigmoid(z_ref[...] + b_ref[...])
        dssm, dgs_rows = _rms_bwd(dm_ref[:, AW:], y2 * gate, gs_ref[...])
        dz = dssm * y2 * gate * (1.0 - gate)
        dz_ref[...] = dz.astype(BF16)
        dy2_ref[...] = dssm * gate

        @pl.when(_first(i))
        def _():
            dga_ref[...] = jnp.zeros_like(dga_ref)
            dgs_ref[...] = jnp.zeros_like(dgs_ref)
            db_ref[...] = jnp.zeros_like(db_ref)

        dga_ref[...] += _colsum(dga_rows)
        dgs_ref[...] += _colsum(dgs_rows)
        db_ref[...] += _colsum(dz)

    return pl.pallas_call(
        body, name="mix_bwd",
        out_shape=(jax.ShapeDtypeStruct((S, AW), BF16), jax.ShapeDtypeStruct((S, AW), F32),
                   jax.ShapeDtypeStruct((S, SW), BF16), jax.ShapeDtypeStruct((S, SW), F32),
                   jax.ShapeDtypeStruct((1, AW), F32), jax.ShapeDtypeStruct((1, SW), F32),
                   jax.ShapeDtypeStruct((1, SW), F32)),
        grid=(S // tr,),
        in_specs=[_rows(tr, AW + SW), _rows(tr, AW), _rows(tr, SW), _rows(tr, SW), _vec(SW), _vec(AW), _vec(SW)],
        out_specs=(_rows(tr, AW), _rows(tr, AW), _rows(tr, SW), _rows(tr, SW), _vec(AW), _vec(SW), _vec(SW)),
        compiler_params=_cparams("arbitrary"))(dmixed, attn, y1, z, b_glu, g_attn, g_ssm)


def _attn_masks(i):
    row = lax.broadcasted_iota(jnp.int32, (BLK, BLK), 0)
    col = lax.broadcasted_iota(jnp.int32, (BLK, BLK), 1)
    return col <= row, jnp.logical_and(col >= row, i > 0)


_NT = (((1,), (1,)), ((), ()))
_TN = (((0,), (0,)), ((), ()))


def _attn_in_specs(width):
    def at(part, prev):
        def index(r, i):
            return (jnp.maximum(i - 1, 0) if prev else i, r * 3 + part)
        return pl.BlockSpec((BLK, width), index)
    return [at(0, False), at(1, False), at(1, True), at(2, False), at(2, True)]


def _attn_fwd(qkv, d, heads):
    M = qkv.shape[0]
    nb = M // BLK
    width = heads * HEAD_DIM
    scale = 1.0 / math.sqrt(HEAD_DIM)

    def body(q_ref, kc_ref, kp_ref, vc_ref, vp_ref, o_ref, l_ref):
        mc, mp = _attn_masks(pl.program_id(1))
        for h in range(heads):
            sl = slice(h * HEAD_DIM, (h + 1) * HEAD_DIM)
            q = q_ref[:, sl]
            sc = jnp.where(mc, lax.dot_general(q, kc_ref[:, sl], _NT, preferred_element_type=F32) * scale, NEG_INF)
            sp = jnp.where(mp, lax.dot_general(q, kp_ref[:, sl], _NT, preferred_element_type=F32) * scale, NEG_INF)
            m = jnp.maximum(jnp.max(sc, axis=-1, keepdims=True), jnp.max(sp, axis=-1, keepdims=True))
            pc, pp = jnp.exp(sc - m), jnp.exp(sp - m)
            tot = jnp.sum(pc, axis=-1, keepdims=True) + jnp.sum(pp, axis=-1, keepdims=True)
            acc = (jnp.dot(pc.astype(BF16), vc_ref[:, sl], preferred_element_type=F32)
                   + jnp.dot(pp.astype(BF16), vp_ref[:, sl], preferred_element_type=F32))
            o_ref[:, sl] = acc / tot
            l_ref[:, sl] = jnp.broadcast_to(m + jnp.log(tot), (BLK, HEAD_DIM))

    out_spec = pl.BlockSpec((BLK, width), lambda r, i: (i, r))
    shape = jax.ShapeDtypeStruct((M, d * width), F32)
    return pl.pallas_call(
        body, name=f"attn_fwd_d{d}", out_shape=(shape, shape), grid=(d, nb),
        in_specs=_attn_in_specs(width), out_specs=(out_spec, out_spec),
        compiler_params=_cparams("parallel", "parallel"))(qkv, qkv, qkv, qkv, qkv)


def _attn_bwd(qkv, dattn, lse, dd, d, heads):
    M = qkv.shape[0]
    nb = M // BLK
    width = heads * HEAD_DIM
    scale = 1.0 / math.sqrt(HEAD_DIM)

    def body(q_ref, kc_ref, kp_ref, vc_ref, vp_ref, da_ref, l_ref, dd_ref,
             dq_ref, dkc_ref, dkp_ref, dvc_ref, dvp_ref):
        mc, mp = _attn_masks(pl.program_id(1))
        for h in range(heads):
            sl = slice(h * HEAD_DIM, (h + 1) * HEAD_DIM)
            q, kc, kp, vc, vp, da = q_ref[:, sl], kc_ref[:, sl], kp_ref[:, sl], vc_ref[:, sl], vp_ref[:, sl], da_ref[:, sl]
            lse_, dd_ = l_ref[:, sl], dd_ref[:, sl]
            sc = lax.dot_general(q, kc, _NT, preferred_element_type=F32) * scale
            sp = lax.dot_general(q, kp, _NT, preferred_element_type=F32) * scale
            pc = jnp.where(mc, jnp.exp(jnp.where(mc, sc, NEG_INF) - lse_), 0.0)
            pp = jnp.where(mp, jnp.exp(jnp.where(mp, sp, NEG_INF) - lse_), 0.0)
            dsc = (pc * (lax.dot_general(da, vc, _NT, preferred_element_type=F32) - dd_) * scale).astype(BF16)
            dsp = (pp * (lax.dot_general(da, vp, _NT, preferred_element_type=F32) - dd_) * scale).astype(BF16)
            dq_ref[:, sl] = (jnp.dot(dsc, kc, preferred_element_type=F32)
                             + jnp.dot(dsp, kp, preferred_element_type=F32))
            dkc_ref[:, sl] = lax.dot_general(dsc, q, _TN, preferred_element_type=F32)
            dkp_ref[:, sl] = lax.dot_general(dsp, q, _TN, preferred_element_type=F32)
            dvc_ref[:, sl] = lax.dot_general(pc.astype(BF16), da, _TN, preferred_element_type=F32)
            dvp_ref[:, sl] = lax.dot_general(pp.astype(BF16), da, _TN, preferred_element_type=F32)

    blk = pl.BlockSpec((BLK, width), lambda r, i: (i, r))
    shape = jax.ShapeDtypeStruct((M, d * width), F32)
    return pl.pallas_call(
        body, name=f"attn_bwd_d{d}", out_shape=(shape,) * 5, grid=(d, nb),
        in_specs=_attn_in_specs(width) + [blk, blk, blk], out_specs=(blk,) * 5,
        compiler_params=_cparams("parallel", "parallel"))(qkv, qkv, qkv, qkv, qkv, dattn, lse, dd)


def _dproj_join(dqs, dkcs, dkps, dvcs, dvps, du):
    S, AW = dqs[0].shape
    SW = du.shape[1]
    nblk = S // BLK

    def cur():
        return pl.BlockSpec((BLK, AW), lambda i: (i, 0))

    def ahead(d):
        return pl.BlockSpec((BLK, AW), lambda i: (jnp.minimum(i + d, nblk - 1), 0))

    def body(*refs):
        dq_r, dkc_r, dkp_r, dvc_r, dvp_r = (refs[3 * n:3 * n + 3] for n in range(5))
        du_ref, out_ref = refs[15], refs[16]
        i = pl.program_id(0)
        dq = dq_r[0][...] + dq_r[1][...] + dq_r[2][...]
        dk = dkc_r[0][...] + dkc_r[1][...] + dkc_r[2][...]
        dv = dvc_r[0][...] + dvc_r[1][...] + dvc_r[2][...]
        for n, d in enumerate(DILATIONS):
            live = i + d < nblk
            dk = dk + jnp.where(live, dkp_r[n][...], 0.0)
            dv = dv + jnp.where(live, dvp_r[n][...], 0.0)
        out_ref[:, :AW] = dq.astype(BF16)
        out_ref[:, AW:2 * AW] = dk.astype(BF16)
        out_ref[:, 2 * AW:3 * AW] = dv.astype(BF16)
        out_ref[:, 3 * AW:] = du_ref[...].astype(BF16)

    in_specs = ([cur()] * 3 + [cur()] * 3 + [ahead(d) for d in DILATIONS]
                + [cur()] * 3 + [ahead(d) for d in DILATIONS] + [pl.BlockSpec((BLK, SW), lambda i: (i, 0))])
    return pl.pallas_call(
        body, name="dproj_join", out_shape=jax.ShapeDtypeStruct((S, 3 * AW + SW), BF16), grid=(nblk,),
        in_specs=in_specs, out_specs=pl.BlockSpec((BLK, 3 * AW + SW), lambda i: (i, 0)),
        compiler_params=_cparams("parallel"))(*dqs, *dkcs, *dkps, *dvcs, *dvps, du)


def _ssm_disc(lr, li, ldt):
    dt = jnp.exp(ldt)
    mag = jnp.exp(lr * dt)
    ar = mag * jnp.cos(li * dt)
    ai = mag * jnp.sin(li * dt)
    nr = ar - 1.0
    den = lr * lr + li * li
    return ar, ai, (nr * lr + ai * li) / den, (ai * lr - nr * li) / den


def _ssm_power_table(lr, li, ldt, n):
    dt = jnp.exp(ldt)
    mag = jnp.exp(n * (lr * dt))
    ang = n * (li * dt)
    return mag * jnp.cos(ang), mag * jnp.sin(ang)


def _cmul(ar, ai, br, bi):
    return ar * br - ai * bi, ar * bi + ai * br


def _scan(xr, xi, ar, ai, reverse):
    T = xr.shape[0]
    row = lax.broadcasted_iota(jnp.int32, xr.shape, 0)
    sh = 1
    while sh < T:
        if reverse:
            keep = row < T - sh
            sr, si = pltpu.roll(xr, T - sh, 0), pltpu.roll(xi, T - sh, 0)
        else:
            keep = row >= sh
            sr, si = pltpu.roll(xr, sh, 0), pltpu.roll(xi, sh, 0)
        sr, si = jnp.where(keep, sr, 0.0), jnp.where(keep, si, 0.0)
        pr, pi = _cmul(ar, ai, sr, si)
        xr, xi = xr + pr, xi + pi
        ar, ai = _cmul(ar, ai, ar, ai)
        sh *= 2
    return xr, xi


def _ssm_specs(T, nch, rev):
    def t_of(c):
        return nch - 1 - c if rev else c
    tok = pl.BlockSpec((T, LANES), lambda j, c: (t_of(c), j))
    par = pl.BlockSpec((None, 1, STATE_LANES), lambda j, c: (j, 0, 0))
    bmat = pl.BlockSpec((None, LANES, STATE_LANES), lambda j, c: (j, 0, 0))
    cmat = pl.BlockSpec((None, STATE_LANES, LANES), lambda j, c: (j, 0, 0))
    dvec = pl.BlockSpec((1, LANES), lambda j, c: (0, j))
    return tok, par, bmat, cmat, dvec


def _ssm_fwd(u, lr_e, li_e, ldt_e, bre_e, bim_e, cre_e, cim_e, d_skip):
    S, SW = u.shape
    T = min(SSM_CHUNK, S)
    nch, nbk = S // T, SW // LANES
    tok, par, bmat, cmat, dvec = _ssm_specs(T, nch, False)
    carry_spec = pl.BlockSpec((None, 1, STATE_LANES), lambda j, c: (c, 0, j))

    def body(u_ref, lr_ref, li_ref, ldt_ref, bre_ref, bim_ref, cre_ref, cim_ref, d_ref,
             y_ref, er_ref, ei_ref, bbr, bbi, pwr, pwi, st_scr, cr, ci):
        c = pl.program_id(1)
        lr, li, ldt = lr_ref[...], li_ref[...], ldt_ref[...]
        ar, ai, kr, ki = _ssm_disc(lr, li, ldt)

        @pl.when(c == 0)
        def _():
            bbr[...] = (kr * bre_ref[...] - ki * bim_ref[...]).astype(BF16)
            bbi[...] = (kr * bim_ref[...] + ki * bre_ref[...]).astype(BF16)
            n = (lax.broadcasted_iota(jnp.int32, (T, 1), 0) + 1).astype(F32)
            pwr[...], pwi[...] = _ssm_power_table(lr, li, ldt, n)
            cr[...] = jnp.zeros_like(cr)
            ci[...] = jnp.zeros_like(ci)

        u_ = u_ref[...]
        ub = u_.astype(BF16)
        xr, xi = _scan(jnp.dot(ub, bbr[...], preferred_element_type=F32),
                       jnp.dot(ub, bbi[...], preferred_element_type=F32), ar, ai, False)
        er, ei = _cmul(pwr[...], pwi[...], cr[...], ci[...])
        sr, si = xr + er, xi + ei
        st_scr[0] = sr
        st_scr[1] = si
        cr[...] = st_scr[0, pl.ds(T - 1, 1), :]
        ci[...] = st_scr[1, pl.ds(T - 1, 1), :]
        er_ref[...] = cr[...]
        ei_ref[...] = ci[...]
        y0 = (jnp.dot(sr.astype(BF16), cre_ref[...].astype(BF16), preferred_element_type=F32)
              - jnp.dot(si.astype(BF16), cim_ref[...].astype(BF16), preferred_element_type=F32))
        y_ref[...] = y0 + d_ref[...] * u_

    ends = jax.ShapeDtypeStruct((nch, 1, nbk * STATE_LANES), F32)
    return pl.pallas_call(
        body, name="ssm_fwd", out_shape=(jax.ShapeDtypeStruct((S, SW), F32), ends, ends),
        grid=(nbk, nch), in_specs=[tok, par, par, par, bmat, bmat, cmat, cmat, dvec],
        out_specs=(tok, carry_spec, carry_spec),
        scratch_shapes=[pltpu.VMEM((LANES, STATE_LANES), BF16), pltpu.VMEM((LANES, STATE_LANES), BF16),
                        pltpu.VMEM((T, STATE_LANES), F32), pltpu.VMEM((T, STATE_LANES), F32),
                        pltpu.VMEM((2, T, STATE_LANES), F32),
                        pltpu.VMEM((1, STATE_LANES), F32), pltpu.VMEM((1, STATE_LANES), F32)],
        compiler_params=_cparams("arbitrary", "arbitrary"),
    )(u, lr_e, li_e, ldt_e, bre_e, bim_e, cre_e, cim_e, d_skip)


def _ssm_bwd(u, y1, dy2a, dy2b, ends_r, ends_i, lr_e, li_e, ldt_e, bre_e, bim_e, cre_e, cim_e, d_skip):
    S, SW = u.shape
    T = min(SSM_CHUNK, S)
    nch, nbk = S // T, SW // LANES
    tok, par, bmat, cmat, dvec = _ssm_specs(T, nch, True)
    prev_spec = pl.BlockSpec((None, 1, STATE_LANES), lambda j, c: (jnp.maximum(nch - 2 - c, 0), 0, j))
    acc8 = pl.BlockSpec((None, 8, STATE_LANES), lambda j, c: (j, 0, 0))
    dd8 = pl.BlockSpec((None, 8, LANES), lambda j, c: (j, 0, 0))

    def body(u_ref, y_ref, da_ref, db_ref, pr_ref, pi_ref, lr_ref, li_ref, ldt_ref,
             bre_ref, bim_ref, cre_ref, cim_ref, d_ref,
             du_ref, dar_ref, dai_ref, dcr_ref, dci_ref, dbr_ref, dbi_ref, ddk_ref,
             bbr, bbi, pwr, pwi, qwr, qwi, g_scr, gr0, gi0):
        c = pl.program_id(1)
        lr, li, ldt = lr_ref[...], li_ref[...], ldt_ref[...]
        ar, ai, kr, ki = _ssm_disc(lr, li, ldt)

        @pl.when(c == 0)
        def _():
            bbr[...] = (kr * bre_ref[...] - ki * bim_ref[...]).astype(BF16)
            bbi[...] = (kr * bim_ref[...] + ki * bre_ref[...]).astype(BF16)
            n = lax.broadcasted_iota(jnp.int32, (T, 1), 0)
            pwr[...], pwi[...] = _ssm_power_table(lr, li, ldt, (n + 1).astype(F32))
            qr, qi = _ssm_power_table(lr, li, ldt, (T - n).astype(F32))
            qwr[...] = qr
            qwi[...] = -qi
            gr0[...] = jnp.zeros_like(gr0)
            gi0[...] = jnp.zeros_like(gi0)
            for ref in (dar_ref, dai_ref, dcr_ref, dci_ref, dbr_ref, dbi_ref, ddk_ref):
                ref[...] = jnp.zeros_like(ref)

        u_ = u_ref[...]
        ub = u_.astype(BF16)
        dy1 = (da_ref[...] + db_ref[...]) * _gelu_grad(y_ref[...])
        dyb = dy1.astype(BF16)

        has_prev = c < nch - 1
        s0r = jnp.where(has_prev, pr_ref[...], 0.0)
        s0i = jnp.where(has_prev, pi_ref[...], 0.0)
        xr, xi = _scan(jnp.dot(ub, bbr[...], preferred_element_type=F32),
                       jnp.dot(ub, bbi[...], preferred_element_type=F32), ar, ai, False)
        er, ei = _cmul(pwr[...], pwi[...], s0r, s0i)
        sr, si = xr + er, xi + ei

        cre_b, cim_b = cre_ref[...].astype(BF16), cim_ref[...].astype(BF16)
        hr, hi = _scan(lax.dot_general(dyb, cre_b, _NT, preferred_element_type=F32),
                       -lax.dot_general(dyb, cim_b, _NT, preferred_element_type=F32), ar, -ai, True)
        fr, fi = _cmul(qwr[...], qwi[...], gr0[...], gi0[...])
        gr, gi = hr + fr, hi + fi
        g_scr[0] = gr
        g_scr[1] = gi
        gr0[...] = g_scr[0, pl.ds(0, 1), :]
        gi0[...] = g_scr[1, pl.ds(0, 1), :]

        row = lax.broadcasted_iota(jnp.int32, (T, STATE_LANES), 0)
        spr = jnp.where(row == 0, s0r, pltpu.roll(sr, 1, 0))
        spi = jnp.where(row == 0, s0i, pltpu.roll(si, 1, 0))

        def fold(a):
            return jnp.sum(a.reshape(T // 8, 8, a.shape[-1]), axis=0)

        dar_ref[...] += fold(gr * spr + gi * spi)
        dai_ref[...] += fold(gi * spr - gr * spi)
        srb, sib, grb, gib = sr.astype(BF16), si.astype(BF16), gr.astype(BF16), gi.astype(BF16)
        dcr_ref[...] += lax.dot_general(srb, dyb, _TN, preferred_element_type=F32)
        dci_ref[...] -= lax.dot_general(sib, dyb, _TN, preferred_element_type=F32)
        dbr_ref[...] += lax.dot_general(ub, grb, _TN, preferred_element_type=F32)
        dbi_ref[...] += lax.dot_general(ub, gib, _TN, preferred_element_type=F32)
        du_ref[...] = (lax.dot_general(grb, bbr[...], _NT, preferred_element_type=F32)
                       + lax.dot_general(gib, bbi[...], _NT, preferred_element_type=F32)
                       + dy1 * d_ref[...])
        ddk_ref[...] += fold(dy1 * u_)

    return pl.pallas_call(
        body, name="ssm_bwd",
        out_shape=(jax.ShapeDtypeStruct((S, SW), F32),
                   jax.ShapeDtypeStruct((nbk, 8, STATE_LANES), F32), jax.ShapeDtypeStruct((nbk, 8, STATE_LANES), F32),
                   jax.ShapeDtypeStruct((nbk, STATE_LANES, LANES), F32), jax.ShapeDtypeStruct((nbk, STATE_LANES, LANES), F32),
                   jax.ShapeDtypeStruct((nbk, LANES, STATE_LANES), F32), jax.ShapeDtypeStruct((nbk, LANES, STATE_LANES), F32),
                   jax.ShapeDtypeStruct((nbk, 8, LANES), F32)),
        grid=(nbk, nch),
        in_specs=[tok, tok, tok, tok, prev_spec, prev_spec, par, par, par, bmat, bmat, cmat, cmat, dvec],
        out_specs=(tok, acc8, acc8, cmat, cmat, bmat, bmat, dd8),
        scratch_shapes=[pltpu.VMEM((LANES, STATE_LANES), BF16), pltpu.VMEM((LANES, STATE_LANES), BF16),
                        pltpu.VMEM((T, STATE_LANES), F32), pltpu.VMEM((T, STATE_LANES), F32),
                        pltpu.VMEM((T, STATE_LANES), F32), pltpu.VMEM((T, STATE_LANES), F32),
                        pltpu.VMEM((2, T, STATE_LANES), F32),
                        pltpu.VMEM((1, STATE_LANES), F32), pltpu.VMEM((1, STATE_LANES), F32)],
        compiler_params=_cparams("arbitrary", "arbitrary"),
    )(u, y1, dy2a, dy2b, ends_r, ends_i, lr_e, li_e, ldt_e, bre_e, bim_e, cre_e, cim_e, d_skip)


def _ssm_param_bwd(dar8, dai8, dbr_e, dbi_e, lr_e, li_e, ldt_e, bre_e, bim_e):
    nbk = lr_e.shape[0]
    par = pl.BlockSpec((None, 1, STATE_LANES), lambda j: (j, 0, 0))
    acc8 = pl.BlockSpec((None, 8, STATE_LANES), lambda j: (j, 0, 0))
    bmat = pl.BlockSpec((None, LANES, STATE_LANES), lambda j: (j, 0, 0))

    def body(dar_ref, dai_ref, dbr_ref, dbi_ref, lr_ref, li_ref, ldt_ref, bre_ref, bim_ref,
             dlr_ref, dli_ref, dldt_ref, dbre_ref, dbim_ref):
        lr, li, ldt = lr_ref[...], li_ref[...], ldt_ref[...]
        (ar, ai, kr, ki), vjp = jax.vjp(_ssm_disc, lr, li, ldt)
        dbr, dbi, bre, bim = dbr_ref[...], dbi_ref[...], bre_ref[...], bim_ref[...]
        dbre_ref[...] = kr * dbr + ki * dbi
        dbim_ref[...] = kr * dbi - ki * dbr
        dkr = _colsum(dbr * bre + dbi * bim)
        dki = _colsum(dbi * bre - dbr * bim)
        dlr, dli, dldt = vjp((_colsum(dar_ref[...]), _colsum(dai_ref[...]), dkr, dki))
        dlr_ref[...] = dlr
        dli_ref[...] = dli
        tot = jnp.broadcast_to(dldt, (8, STATE_LANES))
        sh = 1
        while sh < SSM_P:
            tot = tot + pltpu.roll(tot, STATE_LANES - sh, 1)
            sh *= 2
        dldt_ref[...] = tot[:1]

    vec = jax.ShapeDtypeStruct((nbk, 1, STATE_LANES), F32)
    mat = jax.ShapeDtypeStruct((nbk, LANES, STATE_LANES), F32)
    return pl.pallas_call(
        body, name="ssm_param_bwd", out_shape=(vec, vec, vec, mat, mat), grid=(nbk,),
        in_specs=[acc8, acc8, bmat, bmat, par, par, par, bmat, bmat],
        out_specs=(par, par, par, bmat, bmat), compiler_params=_cparams("parallel"),
    )(dar8, dai8, dbr_e, dbi_e, lr_e, li_e, ldt_e, bre_e, bim_e)


def _expand_b(b):
    G = b.shape[0]
    bt = b.transpose(0, 2, 1).reshape(G // GROUPS_PER_BLOCK, GROUPS_PER_BLOCK, SSM_C, SSM_P)
    eye = jnp.eye(GROUPS_PER_BLOCK, dtype=b.dtype)
    return (bt[:, :, :, None, :] * eye[None, :, None, :, None]).reshape(G // GROUPS_PER_BLOCK, LANES, STATE_LANES)


def _collapse_b(be):
    nbk = be.shape[0]
    eye = jnp.eye(GROUPS_PER_BLOCK, dtype=be.dtype)
    d5 = be.reshape(nbk, GROUPS_PER_BLOCK, SSM_C, GROUPS_PER_BLOCK, SSM_P)
    d4 = (d5 * eye[None, :, None, :, None]).sum(axis=3)
    return d4.transpose(0, 1, 3, 2).reshape(nbk * GROUPS_PER_BLOCK, SSM_P, SSM_C)


def _expand_c(cm):
    G = cm.shape[0]
    ct = cm.transpose(0, 2, 1).reshape(G // GROUPS_PER_BLOCK, GROUPS_PER_BLOCK, SSM_P, SSM_C)
    eye = jnp.eye(GROUPS_PER_BLOCK, dtype=cm.dtype)
    return (ct[:, :, :, None, :] * eye[None, :, None, :, None]).reshape(G // GROUPS_PER_BLOCK, STATE_LANES, LANES)


def _collapse_c(ce):
    nbk = ce.shape[0]
    eye = jnp.eye(GROUPS_PER_BLOCK, dtype=ce.dtype)
    d5 = ce.reshape(nbk, GROUPS_PER_BLOCK, SSM_P, GROUPS_PER_BLOCK, SSM_C)
    d4 = (d5 * eye[None, :, None, :, None]).sum(axis=3)
    return d4.transpose(0, 1, 3, 2).reshape(nbk * GROUPS_PER_BLOCK, SSM_C, SSM_P)


def _place():
    x, y, c = lax.axis_index("x"), lax.axis_index("y"), lax.axis_index("c")
    return x, y, c


def _other_chips(x, y):
    return [(1 - x, y), (x, 1 - y), (1 - x, 1 - y)]


_ANY = pl.BlockSpec(memory_space=pl.ANY)


_HBM = pl.BlockSpec(memory_space=pltpu.HBM)
_SEM = pl.BlockSpec(memory_space=pltpu.SEMAPHORE)
_EFFECT = pltpu.SideEffectType.DATAFLOW_SIDE_EFFECTING
_TOKEN = jax.ShapeDtypeStruct((8, LANES), F32)


def _hbm(a):
    return pltpu.with_memory_space_constraint(a, pltpu.HBM)


def _place_own(src, *, gather, name, tr=512):
    R, C = src.shape[-2:]
    tr = min(tr, R)
    x, y, _ = _place()
    me = (2 * x + y).astype(jnp.int32).reshape(1)

    def body(me_ref, s_ref, o_ref):
        o_ref[...] = s_ref[...].astype(BF16)

    own = pl.BlockSpec((None, tr, C), lambda i, me_ref: (me_ref[0], i, 0))
    grid_spec = pltpu.PrefetchScalarGridSpec(
        num_scalar_prefetch=1, grid=(R // tr,),
        in_specs=[pl.BlockSpec((tr, C), lambda i, me_ref: (i, 0)) if gather else own], out_specs=own)
    return pl.pallas_call(
        body, name=name, grid_spec=grid_spec, out_shape=jax.ShapeDtypeStruct((N_CHIPS, R, C), BF16),
        compiler_params=_cparams("parallel"))(me, src)


def _exchange_copy(src_slot, land_slot, send, recv, k, j, peer, c):
    return pltpu.make_async_remote_copy(
        src_ref=src_slot, dst_ref=land_slot, send_sem=send.at[3 * k + j], recv_sem=recv.at[3 * k + j],
        device_id=(peer[0], peer[1], c), device_id_type=MESH)


def _exchange_start(lands, srcs, groups, *, name):
    n, ng = len(lands), len(groups)
    bufs = list(lands) + list(srcs)
    nb = len(bufs)

    def body(*refs):
        lnd, src, sems = refs[:n], refs[n:nb], refs[nb:nb + 2 * ng]
        token = refs[2 * nb + 2 * ng]
        x, y, c = _place()
        me = 2 * x + y
        for gi, group in enumerate(groups):
            for k, w in enumerate(group):
                for j, peer in enumerate(_other_chips(x, y)):
                    sent = src[w].at[2 * peer[0] + peer[1]] if src else lnd[w].at[me]
                    _exchange_copy(sent, lnd[w].at[me], sems[2 * gi], sems[2 * gi + 1], k, j, peer, c).start()
        token[...] = jnp.zeros_like(token)

    sem_shapes = [pltpu.SemaphoreType.DMA((3 * len(g),)) for g in groups for _ in range(2)]
    res = pl.pallas_call(
        body, name=name,
        out_shape=sem_shapes + [pltpu.HBM(a.shape, a.dtype) for a in bufs] + [_TOKEN],
        in_specs=[_HBM] * nb,
        out_specs=[_SEM] * (2 * ng) + [_HBM] * nb + [pl.BlockSpec(memory_space=pltpu.VMEM)],
        input_output_aliases={i: 2 * ng + i for i in range(nb)},
        compiler_params=pltpu.CompilerParams(has_side_effects=_EFFECT),
    )(*[_hbm(a) for a in bufs])
    sems = [(res[2 * gi], res[2 * gi + 1]) for gi in range(ng)]
    return sems, res[2 * ng:2 * ng + n], res[2 * ng + n:2 * ng + nb], res[-1]


def _exchange_wait(lands, srcs, sems, after, *, name):
    n = len(lands)
    bufs = list(lands) + list(srcs)
    nb = len(bufs)
    send_sems, recv_sems = sems

    def body(*refs):
        lnd, src, send, recv = refs[:n], refs[n:nb], refs[nb], refs[nb + 1]
        x, y, c = _place()
        for k in range(n):
            for j, peer in enumerate(_other_chips(x, y)):
                slot = 2 * peer[0] + peer[1]
                copy = _exchange_copy((src[k] if src else lnd[k]).at[slot], lnd[k].at[slot], send, recv, k, j, peer, c)
                copy.wait_send()
                copy.wait_recv()

    res = pl.pallas_call(
        body, name=name, out_shape=[pltpu.HBM(a.shape, a.dtype) for a in bufs],
        in_specs=[_HBM] * nb + [_SEM, _SEM, _ANY], out_specs=[_HBM] * nb,
        input_output_aliases={i: i for i in range(nb)},
        compiler_params=pltpu.CompilerParams(has_side_effects=_EFFECT),
    )(*bufs, send_sems, recv_sems, after)
    return res[:n]


def _sum_partials(land, *, name, tr=256):
    _, R, C = land.shape
    tr = min(tr, R)

    def body(l_ref, o_ref):
        acc = l_ref[0].astype(F32)
        for k in range(1, N_CHIPS):
            acc = acc + l_ref[k].astype(F32)
        o_ref[...] = acc

    return pl.pallas_call(
        body, name=name, out_shape=jax.ShapeDtypeStruct((R, C), F32), grid=(R // tr,),
        in_specs=[pl.BlockSpec((N_CHIPS, tr, C), lambda i: (0, i, 0))], out_specs=_rows(tr, C),
        compiler_params=_cparams("parallel"))(land)


def _swap_with_sibling(sums, *, name):
    n = len(sums)

    def body(*refs):
        ins, outs = refs[:n], refs[n:2 * n]
        send_sems, recv_sems = refs[2 * n:]
        x, y, c = _place()
        copies = [pltpu.make_async_remote_copy(
            src_ref=ins[w], dst_ref=outs[w], send_sem=send_sems.at[w], recv_sem=recv_sems.at[w],
            device_id=(x, y, 1 - c), device_id_type=MESH) for w in range(n)]
        for cp in copies:
            cp.start()
        for cp in copies:
            cp.wait_recv()
            cp.wait_send()

    return pl.pallas_call(
        body, name=name,
        out_shape=[jax.ShapeDtypeStruct(s.shape, s.dtype) for s in sums],
        in_specs=[_ANY] * n, out_specs=[_ANY] * n,
        scratch_shapes=[pltpu.SemaphoreType.DMA((n,)), pltpu.SemaphoreType.DMA((n,))],
    )(*sums)


def _adamw_math(w, g, m, v):
    m = ADAM_B1 * m + (1.0 - ADAM_B1) * g
    v = ADAM_B2 * v + (1.0 - ADAM_B2) * (g * g)
    m_hat = m / (1.0 - ADAM_B1 ** ADAM_STEP)
    v_hat = v / (1.0 - ADAM_B2 ** ADAM_STEP)
    delta = -ADAM_LR * (m_hat / (jnp.sqrt(v_hat) + ADAM_EPS) + ADAM_WD * w)
    return delta, m, v


def _adamw_pair(mine, theirs, w, m, v, *, name, tr=128):
    R, C = w.shape
    tr = min(tr, R)

    def body(a_ref, b_ref, w_ref, m_ref, v_ref, g_ref, d_ref, nm_ref, nv_ref):
        g = a_ref[...] + b_ref[...]
        g_ref[...] = g
        d_ref[...], nm_ref[...], nv_ref[...] = _adamw_math(w_ref[...], g, m_ref[...], v_ref[...])

    shape = jax.ShapeDtypeStruct((R, C), F32)
    return pl.pallas_call(
        body, name=name, out_shape=(shape,) * 4, grid=(R // tr,),
        in_specs=[_rows(tr, C)] * 5, out_specs=(_rows(tr, C),) * 4,
        compiler_params=_cparams("parallel"))(mine, theirs, w, m, v)


def _all_gather_small(packed):
    R = packed.shape[0]

    def body(x_ref, out_ref, send_sems, recv_sems, local_sem):
        x, y, c = _place()

        def slot(px, py, pc):
            return out_ref.at[4 * px + 2 * py + pc]

        local = pltpu.make_async_copy(x_ref, slot(x, y, c), local_sem)
        local.start()
        peers = [(x ^ (k >> 2), y ^ ((k >> 1) & 1), c ^ (k & 1)) for k in range(1, N_DEV)]
        for k, peer in enumerate(peers):
            pltpu.make_async_remote_copy(
                src_ref=x_ref, dst_ref=slot(x, y, c), send_sem=send_sems.at[k], recv_sem=recv_sems.at[k],
                device_id=peer, device_id_type=MESH).start()
        for k, peer in enumerate(peers):
            arrival = pltpu.make_async_remote_copy(
                src_ref=x_ref, dst_ref=slot(*peer), send_sem=send_sems.at[k], recv_sem=recv_sems.at[k],
                device_id=peer, device_id_type=MESH)
            arrival.wait_recv()
            arrival.wait_send()
        local.wait()

    vm = pl.BlockSpec(memory_space=pltpu.VMEM)
    return pl.pallas_call(
        body, name="all_gather_small", out_shape=jax.ShapeDtypeStruct((N_DEV, R, LANES), F32),
        in_specs=[vm], out_specs=vm,
        scratch_shapes=[pltpu.SemaphoreType.DMA((N_DEV - 1,)), pltpu.SemaphoreType.DMA((N_DEV - 1,)),
                        pltpu.SemaphoreType.DMA],
        compiler_params=pltpu.CompilerParams(vmem_limit_bytes=VMEM_LIMIT_BYTES),
    )(packed)


def _adamw_small(gathered, w, m, v):
    _, R, _ = gathered.shape
    tr = PACK_ROWS

    def body(gs_ref, w_ref, m_ref, v_ref, g_ref, d_ref, nm_ref, nv_ref):
        g = gs_ref[0]
        for k in range(1, N_DEV):
            g = g + gs_ref[k]
        g_ref[...] = g
        d_ref[...], nm_ref[...], nv_ref[...] = _adamw_math(w_ref[...], g, m_ref[...], v_ref[...])

    shape = jax.ShapeDtypeStruct((R, LANES), F32)
    return pl.pallas_call(
        body, name="adamw_small", out_shape=(shape,) * 4, grid=(R // tr,),
        in_specs=[pl.BlockSpec((N_DEV, tr, LANES), lambda i: (0, i, 0))] + [_rows(tr, LANES)] * 3,
        out_specs=(_rows(tr, LANES),) * 4, compiler_params=_cparams("parallel"))(gathered, w, m, v)


def _pack(arrays):
    parts, layout = [], []
    for a in arrays:
        n = a.size
        rows = -(-n // (8 * LANES)) * 8
        flat = jnp.pad(a.reshape(-1).astype(F32), (0, rows * LANES - n))
        parts.append(flat.reshape(rows, LANES))
        layout.append((rows, n, a.shape))
    total = sum(r for r, _, _ in layout)
    parts.append(jnp.zeros((-total % PACK_ROWS, LANES), F32))
    return jnp.concatenate(parts, axis=0), layout


def _unpack(buf, layout):
    out, r0 = [], 0
    for rows, n, shape in layout:
        out.append(buf[r0:r0 + rows].reshape(-1)[:n].reshape(shape))
        r0 += rows
    return out


SMALL = ("mix_norm_pre", "lam_re", "lam_im", "log_dt", "ssm_b_re", "ssm_b_im", "ssm_c_re", "ssm_c_im",
         "ssm_d", "b_glu", "attn_out_norm", "ssm_out_norm", "mix_norm_post", "mlp_norm_pre",
         "mlp_norm_post", "ple_norm_pre", "ple_norm_post")
BIG = ("w_in", "w_glu", "w_out", "w_up", "w_down", "w_ple_gate", "w_ple_proj")
WEIGHTS = ("mix_norm_pre", "w_in", "lam_re", "lam_im", "log_dt", "ssm_b_re", "ssm_b_im", "ssm_c_re",
           "ssm_c_im", "ssm_d", "w_glu", "b_glu", "attn_out_norm", "ssm_out_norm", "w_out",
           "mix_norm_post", "mlp_norm_pre", "w_up", "w_down", "mlp_norm_post", "ple_norm_pre",
           "w_ple_gate", "w_ple_proj", "ple_norm_post")


def _to_branch(a, d):
    return a if d == 1 else a.reshape(a.shape[0] // d, d * a.shape[1])


def _from_branch(a, d, S):
    return a if d == 1 else a.reshape(S, a.shape[1] // d)


def kernel(x, p, mix_norm_pre, w_in, lam_re, lam_im, log_dt, ssm_b_re, ssm_b_im, ssm_c_re, ssm_c_im, ssm_d, w_glu, b_glu, attn_out_norm, ssm_out_norm, w_out, mix_norm_post, mlp_norm_pre, w_up, w_down, mlp_norm_post, ple_norm_pre, w_ple_gate, w_ple_proj, ple_norm_post, loss_target, m_mix_norm_pre, m_w_in, m_lam_re, m_lam_im, m_log_dt, m_ssm_b_re, m_ssm_b_im, m_ssm_c_re, m_ssm_c_im, m_ssm_d, m_w_glu, m_b_glu, m_attn_out_norm, m_ssm_out_norm, m_w_out, m_mix_norm_post, m_mlp_norm_pre, m_w_up, m_w_down, m_mlp_norm_post, m_ple_norm_pre, m_w_ple_gate, m_w_ple_proj, m_ple_norm_post, v_mix_norm_pre, v_w_in, v_lam_re, v_lam_im, v_log_dt, v_ssm_b_re, v_ssm_b_im, v_ssm_c_re, v_ssm_c_im, v_ssm_d, v_w_glu, v_b_glu, v_attn_out_norm, v_ssm_out_norm, v_w_out, v_mix_norm_post, v_mlp_norm_pre, v_w_up, v_w_down, v_mlp_norm_post, v_ple_norm_pre, v_w_ple_gate, v_w_ple_proj, v_ple_norm_post):
    args = dict(locals())
    W = {n: args[n][0] for n in WEIGHTS}
    Mo = {n: args["m_" + n][0] for n in WEIGHTS}
    Vo = {n: args["v_" + n][0] for n in WEIGHTS}
    xs, ps, tgt = x[0], p[0, 0], loss_target[0]
    S, D = xs.shape
    SW = W["ssm_d"].shape[0]
    AW = W["attn_out_norm"].shape[0]
    heads = AW // HEAD_DIM
    G = SW // SSM_C
    nbk = SW // LANES
    assert W["w_in"].shape[1] * N_CHIPS == 3 * AW + SW and AW == SW

    row = lambda a: a.reshape(1, -1)

    ag_groups = (("w_in",), ("w_glu", "w_out"), ("w_up",), ("w_down", "w_ple_gate", "w_ple_proj"))
    ag_names = [n for g in ag_groups for n in g]
    ag_sems, ag_land, _, ag_token = _exchange_start(
        [_place_own(W[n], gather=True, name="ag_place_" + n) for n in ag_names], [],
        [[ag_names.index(n) for n in g] for g in ag_groups], name="ag_start")

    def gathered(gi, after):
        got = _exchange_wait([ag_land[ag_names.index(n)] for n in ag_groups[gi]], [], ag_sems[gi], after,
                             name=f"ag_wait_{gi}")
        return dict(zip(ag_groups[gi], got))

    lr_e = W["lam_re"].reshape(nbk, 1, STATE_LANES)
    li_e = W["lam_im"].reshape(nbk, 1, STATE_LANES)
    ldt_e = jnp.repeat(W["log_dt"], SSM_P).reshape(nbk, 1, STATE_LANES)
    bre_e, bim_e = _expand_b(W["ssm_b_re"]), _expand_b(W["ssm_b_im"])
    cre_e, cim_e = _expand_c(W["ssm_c_re"]), _expand_c(W["ssm_c_im"])
    d_row = row(W["ssm_d"])

    hn1 = _norm_cast(xs, row(W["mix_norm_pre"]) + ag_token[0, 0], name="norm_in")
    w_in_f = gathered(0, hn1)["w_in"]
    qkv = _matmul(hn1, w_in_f, name="proj_qkv", out_dtype=BF16, b_shards=N_CHIPS, b_cols=(0, 3 * AW))
    u = _matmul(hn1, w_in_f, name="proj_u", b_shards=N_CHIPS, b_cols=(3 * AW, SW))
    qkv_b = [_to_branch(qkv, d) for d in DILATIONS]
    outs, lses = [], []
    for d, qb in zip(DILATIONS, qkv_b):
        o, l = _attn_fwd(qb, d, heads)
        outs.append(_from_branch(o, d, S))
        lses.append(_from_branch(l, d, S))
    y1, ends_r, ends_i = _ssm_fwd(u, lr_e, li_e, ldt_e, bre_e, bim_e, cre_e, cim_e, d_row)
    y2b = _gelu_cast(y1)
    full = gathered(1, y2b)
    w_glu_f = full["w_glu"].reshape(SW, SW)
    w_out_f = full["w_out"].reshape(AW + SW, D)
    z = _matmul(y2b, w_glu_f, name="glu_z")
    attn, lse, mixed = _mix_fwd(outs, lses, y1, z, row(W["b_glu"]), row(W["attn_out_norm"]), row(W["ssm_out_norm"]))
    mo = _matmul(mixed, w_out_f, name="mix_out")
    h1, hn2 = _res_norm(xs, mo, row(W["mix_norm_post"]), row(W["mlp_norm_pre"]), name="res_mix")
    w_up_f = gathered(2, hn2)["w_up"]
    up, act = _matmul(hn2, w_up_f, name="mlp_up", b_shards=N_CHIPS, relu2=True)
    full = gathered(3, act)
    w_down_f = full["w_down"].reshape(-1, D)
    w_pg_f = full["w_ple_gate"].reshape(D, D)
    w_pp_f = full["w_ple_proj"]
    ff = _matmul(act, w_down_f, name="mlp_down")
    h2, hn3 = _res_norm(h1, ff, row(W["mlp_norm_post"]), row(W["ple_norm_pre"]), name="res_mlp")
    gl = _matmul(hn3, w_pg_f, name="ple_gate")
    e = _matmul(ps.astype(BF16), w_pp_f, name="ple_proj", b_shards=N_CHIPS)

    dh3, dgl, de, loss_part, dg_ple_post = _final(h2, gl, e, row(W["ple_norm_post"]), tgt)
    gW = {}
    out_g, out_d, out_m, out_v = {}, {}, {}, {}

    def scatter_start(names, tag):
        parts = [gW[n] if gW[n].ndim == 3 else gW[n].reshape((N_CHIPS, -1, gW[n].shape[1])) for n in names]
        sems, land, src, token = _exchange_start(
            [_place_own(part, gather=False, name="rs_place_" + n) for n, part in zip(names, parts)], parts,
            [list(range(len(names)))], name=f"rs_start_{tag}")
        return (names, sems[0], land, src), token

    def scatter_finish(batch, after, tag):
        names, sems, land, src = batch
        landed = _exchange_wait(land, src, sems, after, name=f"rs_wait_{tag}")
        sums = [_sum_partials(l, name="sum_" + n) for n, l in zip(names, landed)]
        theirs = _swap_with_sibling(sums, name=f"swap_{tag}")
        for n, a, b in zip(names, sums, theirs):
            out_g[n], out_d[n], out_m[n], out_v[n] = _adamw_pair(a, b, W[n], Mo[n], Vo[n], name="adamw_" + n)

    gW["w_ple_proj"] = _matmul(ps.astype(BF16), de, name="d_w_ple_proj", ta=True, out_dtype=BF16, out_shards=N_CHIPS)
    gW["w_ple_gate"] = _matmul(hn3, dgl, name="d_w_ple_gate", ta=True, out_dtype=BF16)
    dhn3 = _matmul(dgl, w_pg_f, name="d_hn3", tb=True)
    dh2, dff, dg_ple_pre, dg_mlp_post = _bwd_res_norm(
        dh3, dhn3, h2, row(W["ple_norm_pre"]), ff, row(W["mlp_norm_post"]), name="bwd_res_mlp")
    gW["w_down"] = _matmul(act, dff, name="d_w_down", ta=True, out_dtype=BF16)
    batch1, token1 = scatter_start(("w_ple_proj", "w_ple_gate", "w_down"), 1)
    dup = _matmul(dff, w_down_f, name="d_up", tb=True, after=token1, relu2_of=up, out_dtype=BF16)
    gW["w_up"] = _matmul(hn2, dup, name="d_w_up", ta=True, out_dtype=BF16, out_shards=N_CHIPS)
    dhn2 = _matmul(dup, w_up_f, name="d_hn2", tb=True, b_shards=N_CHIPS)
    dh1, dmo, dg_mlp_pre, dg_mix_post = _bwd_res_norm(
        dh2, dhn2, h1, row(W["mlp_norm_pre"]), mo, row(W["mix_norm_post"]), name="bwd_res_mix")
    gW["w_out"] = _matmul(mixed, dmo, name="d_w_out", ta=True, out_dtype=BF16)
    dmixed = _matmul(dmo, w_out_f, name="d_mixed", tb=True)
    dattn, dd, dz, dy2a, dg_attn, dg_ssm, db_glu = _mix_bwd(
        dmixed, attn, y1, z, row(W["b_glu"]), row(W["attn_out_norm"]), row(W["ssm_out_norm"]))
    gW["w_glu"] = _matmul(y2b, dz, name="d_w_glu", ta=True, out_dtype=BF16)
    batch2, token2 = scatter_start(("w_up", "w_out", "w_glu"), 2)
    dy2b = _matmul(dz, w_glu_f, name="d_y2", tb=True, after=token2)
    du, dar8, dai8, dcr_e, dci_e, dbr_e, dbi_e, dd8 = _ssm_bwd(
        u, y1, dy2a, dy2b, ends_r, ends_i, lr_e, li_e, ldt_e, bre_e, bim_e, cre_e, cim_e, d_row)
    scatter_finish(batch1, du, 1)
    dlr_e, dli_e, dldt_e, dbre_e, dbim_e = _ssm_param_bwd(dar8, dai8, dbr_e, dbi_e, lr_e, li_e, ldt_e, bre_e, bim_e)

    grads5 = [[], [], [], [], []]
    for d, qb in zip(DILATIONS, qkv_b):
        res = _attn_bwd(qb, _to_branch(dattn, d), _to_branch(lse, d), _to_branch(dd, d), d, heads)
        for lst, a in zip(grads5, res):
            lst.append(_from_branch(a, d, S))
    dproj = _dproj_join(*grads5, du)
    scatter_finish(batch2, dproj, 2)
    gW["w_in"] = _matmul(hn1, dproj, name="d_w_in", ta=True, out_dtype=BF16, out_shards=N_CHIPS)
    batch3, token3 = scatter_start(("w_in",), 3)
    dhn1 = _matmul(dproj, w_in_f, name="d_hn1", tb=True, b_shards=N_CHIPS, after=token3)
    grad_x, dg_mix_pre = _bwd_first(dh1, dhn1, xs, row(W["mix_norm_pre"]))
    scatter_finish(batch3, grad_x, 3)

    small_g = {
        "mix_norm_pre": dg_mix_pre, "lam_re": dlr_e.reshape(G, SSM_P), "lam_im": dli_e.reshape(G, SSM_P),
        "log_dt": dldt_e.reshape(G, SSM_P)[:, 0], "ssm_b_re": _collapse_b(dbre_e), "ssm_b_im": _collapse_b(dbim_e),
        "ssm_c_re": _collapse_c(dcr_e), "ssm_c_im": _collapse_c(dci_e), "ssm_d": dd8.sum(axis=1).reshape(-1),
        "b_glu": db_glu, "attn_out_norm": dg_attn, "ssm_out_norm": dg_ssm, "mix_norm_post": dg_mix_post,
        "mlp_norm_pre": dg_mlp_pre, "mlp_norm_post": dg_mlp_post, "ple_norm_pre": dg_ple_pre,
        "ple_norm_post": dg_ple_post,
    }
    g_pack, layout = _pack([small_g[n].reshape(W[n].shape) for n in SMALL])
    w_pack, _ = _pack([W[n] for n in SMALL])
    m_pack, _ = _pack([Mo[n] for n in SMALL])
    v_pack, _ = _pack([Vo[n] for n in SMALL])
    packed = _adamw_small(_all_gather_small(g_pack), w_pack, m_pack, v_pack)
    for dst, buf in zip((out_g, out_d, out_m, out_v), packed):
        dst.update(zip(SMALL, _unpack(buf, layout)))

    loss = lax.psum(loss_part[0, 0], ("x", "y", "c"))
    lead = lambda a: a[None]
    return (loss, grad_x[None],
            *[lead(out_g[n]) for n in WEIGHTS], *[lead(out_d[n]) for n in WEIGHTS],
            *[lead(out_m[n]) for n in WEIGHTS], *[lead(out_v[n]) for n in WEIGHTS])
```

```python
import functools
import math

import jax
import jax.numpy as jnp
from jax import lax
from jax.experimental import pallas as pl
from jax.experimental.pallas import tpu as pltpu

F32 = jnp.float32
BF16 = jnp.bfloat16
MESH = pl.DeviceIdType.MESH

RMS_EPS = 1e-6
NEG_INF = -1e30
HEAD_DIM = 128
BLK = 128
DILATIONS = (1, 4, 16)
SSM_C = 16
SSM_P = 64
LANES = 128
GROUPS_PER_BLOCK = LANES // SSM_C
STATE_LANES = GROUPS_PER_BLOCK * SSM_P
SSM_CHUNK = 128
TILE = 8
ADAM_LR, ADAM_B1, ADAM_B2, ADAM_EPS, ADAM_WD, ADAM_STEP = 1e-3, 0.9, 0.999, 1e-8, 0.01, 10
VMEM_LIMIT_BYTES = 56 * 1024 * 1024
N_CHIPS = 4
N_DEV = 8
PACK_ROWS = 256


def _cparams(*sem):
    return pltpu.CompilerParams(dimension_semantics=sem or None, vmem_limit_bytes=VMEM_LIMIT_BYTES)


def _rows(tr, w):
    return pl.BlockSpec((tr, w), lambda i: (i, 0))


def _vec(w):
    return pl.BlockSpec((1, w), lambda i: (0, 0))


def _sigmoid(x):
    return 1.0 / (1.0 + jnp.exp(-x))


def _gelu(x):
    c = math.sqrt(2.0 / math.pi)
    return 0.5 * x * (1.0 + jnp.tanh(c * (x + 0.044715 * x * x * x)))


def _gelu_grad(x):
    c = math.sqrt(2.0 / math.pi)
    th = jnp.tanh(c * (x + 0.044715 * x * x * x))
    return 0.5 * (1.0 + th) + 0.5 * x * (1.0 - th * th) * c * (1.0 + 3.0 * 0.044715 * x * x)


def _rms(x, g):
    r = lax.rsqrt(jnp.mean(x * x, axis=-1, keepdims=True) + RMS_EPS)
    return x * r * g


def _rms_bwd(dy, x, g):
    r = lax.rsqrt(jnp.mean(x * x, axis=-1, keepdims=True) + RMS_EPS)
    n = x * r
    dn = dy * g
    dx = r * (dn - n * jnp.mean(dn * n, axis=-1, keepdims=True))
    return dx, dy * n


def _colsum(a):
    return jnp.sum(a, axis=0, keepdims=True)


def _first(i):
    return i == 0


def _matmul(a, b, *, name, ta=False, tb=False, out_dtype=F32, b_shards=1, out_shards=1, b_cols=None,
            after=None, relu2=False, relu2_of=None, tm=1024, tn=1024, tk=2048):
    if ta:
        K, M = a.shape
    else:
        M, K = a.shape
    if b_shards > 1:
        rows, cols = b.shape[1], b.shape[2] * b_shards
    else:
        rows, cols = b.shape
    N, Kb = (rows, cols) if tb else (cols, rows)
    assert K == Kb, (a.shape, b.shape, ta, tb)
    col0 = 0
    if b_cols is not None:
        assert not tb
        col0, N = b_cols
    tm, tn, tk = min(tm, M), min(tn, N), min(tk, K)
    if b_shards > 1:
        shard_cols = cols // b_shards
        if tb:
            tk = min(tk, shard_cols)
        else:
            tn = min(tn, shard_cols)
    if out_shards > 1:
        tn = min(tn, N // out_shards)
    assert M % tm == 0 and N % tn == 0 and K % tk == 0 and col0 % tn == 0
    nk = K // tk
    j0 = col0 // tn

    a_spec = (pl.BlockSpec((tk, tm), lambda i, j, k: (k, i)) if ta
              else pl.BlockSpec((tm, tk), lambda i, j, k: (i, k)))
    if b_shards > 1:
        if tb:
            per = shard_cols // tk
            b_spec = pl.BlockSpec((None, tn, tk), lambda i, j, k: (k // per, j, k % per))
        else:
            per = shard_cols // tn
            b_spec = pl.BlockSpec((None, tk, tn), lambda i, j, k: ((j + j0) // per, k, (j + j0) % per))
    else:
        b_spec = (pl.BlockSpec((tn, tk), lambda i, j, k: (j, k)) if tb
                  else pl.BlockSpec((tk, tn), lambda i, j, k: (k, j + j0)))
    if out_shards > 1:
        per_o = (N // out_shards) // tn
        out_shape = jax.ShapeDtypeStruct((out_shards, M, N // out_shards), out_dtype)
        out_spec = pl.BlockSpec((None, tm, tn), lambda i, j, k: (j // per_o, i, j % per_o))
    else:
        out_shape = jax.ShapeDtypeStruct((M, N), out_dtype)
        out_spec = pl.BlockSpec((tm, tn), lambda i, j, k: (i, j))
    dims = (((0 if ta else 1,), (1 if tb else 0,)), ((), ()))

    extra, extra_specs = [], []
    if relu2_of is not None:
        assert out_shards == 1 and relu2_of.shape == (M, N)
        extra.append(relu2_of)
        extra_specs.append(pl.BlockSpec((tm, tn), lambda i, j, k: (i, j)))
    if after is not None:
        extra.append(after)
        extra_specs.append(pl.BlockSpec(after.shape, lambda i, j, k: (0, 0)))
    n_in = 2 + len(extra)
    if relu2:
        assert out_shards == 1
        out_shape = (out_shape, jax.ShapeDtypeStruct((M, N), BF16))
        out_spec = (out_spec, out_spec)

    def finish(acc, refs):
        o_ref = refs[n_in]
        if relu2_of is not None:
            acc = acc * (2.0 * jnp.maximum(refs[2][...], 0.0))
        o_ref[...] = acc.astype(o_ref.dtype)
        if relu2:
            r = jnp.maximum(acc, 0.0)
            refs[n_in + 1][...] = (r * r).astype(BF16)

    def body(*refs):
        prod = lax.dot_general(refs[0][...], refs[1][...], dims, preferred_element_type=F32)
        if nk == 1:
            finish(prod, refs)
            return
        acc_ref = refs[-1]
        k = pl.program_id(2)

        @pl.when(k == 0)
        def _():
            acc_ref[...] = prod

        @pl.when(k > 0)
        def _():
            acc_ref[...] += prod

        @pl.when(k == nk - 1)
        def _():
            finish(acc_ref[...], refs)

    return pl.pallas_call(
        body, name=name, out_shape=out_shape, grid=(M // tm, N // tn, nk),
        in_specs=[a_spec, b_spec] + extra_specs, out_specs=out_spec,
        scratch_shapes=[pltpu.VMEM((tm, tn), F32)] if nk > 1 else [],
        compiler_params=_cparams("parallel", "parallel", "arbitrary"),
    )(a, b, *extra)


def _norm_cast(x, g, *, name, tr=256):
    S, D = x.shape
    tr = min(tr, S)

    def body(x_ref, g_ref, o_ref):
        o_ref[...] = _rms(x_ref[...], g_ref[...]).astype(BF16)

    return pl.pallas_call(
        body, name=name, out_shape=jax.ShapeDtypeStruct((S, D), BF16), grid=(S // tr,),
        in_specs=[_rows(tr, D), _vec(D)], out_specs=_rows(tr, D),
        compiler_params=_cparams("parallel"))(x, g)


def _res_norm(res, y, g_post, g_next, *, name, tr=256):
    S, D = res.shape
    tr = min(tr, S)

    def body(res_ref, y_ref, gp_ref, gn_ref, h_ref, hn_ref):
        h = res_ref[...] + _rms(y_ref[...], gp_ref[...])
        h_ref[...] = h
        hn_ref[...] = _rms(h, gn_ref[...]).astype(BF16)

    return pl.pallas_call(
        body, name=name,
        out_shape=(jax.ShapeDtypeStruct((S, D), F32), jax.ShapeDtypeStruct((S, D), BF16)),
        grid=(S // tr,), in_specs=[_rows(tr, D), _rows(tr, D), _vec(D), _vec(D)],
        out_specs=(_rows(tr, D), _rows(tr, D)), compiler_params=_cparams("parallel"))(res, y, g_post, g_next)


def _gelu_cast(y1, *, tr=256):
    S, W = y1.shape
    tr = min(tr, S)

    def body(y_ref, o_ref):
        o_ref[...] = _gelu(y_ref[...]).astype(BF16)

    return pl.pallas_call(
        body, name="gelu_cast", out_shape=jax.ShapeDtypeStruct((S, W), BF16), grid=(S // tr,),
        in_specs=[_rows(tr, W)], out_specs=_rows(tr, W), compiler_params=_cparams("parallel"))(y1)


def _mix_fwd(os, ls, y1, z, b_glu, g_attn, g_ssm, *, tr=128):
    S, AW = os[0].shape
    SW = y1.shape[1]
    tr = min(tr, S)

    def body(o1, o2, o3, l1, l2, l3, y_ref, z_ref, b_ref, ga_ref, gs_ref, attn_ref, lse_ref, mixed_ref):
        la, lb, lc = l1[...], l2[...], l3[...]
        m = jnp.maximum(jnp.maximum(la, lb), lc)
        ea, eb, ec = jnp.exp(la - m), jnp.exp(lb - m), jnp.exp(lc - m)
        tot = ea + eb + ec
        attn = (ea * o1[...] + eb * o2[...] + ec * o3[...]) / tot
        attn_ref[...] = attn
        lse_ref[...] = m + jnp.log(tot)
        ssm = _gelu(y_ref[...]) * _sigmoid(z_ref[...] + b_ref[...])
        mixed_ref[:, :AW] = _rms(attn, ga_ref[...]).astype(BF16)
        mixed_ref[:, AW:] = _rms(ssm, gs_ref[...]).astype(BF16)

    return pl.pallas_call(
        body, name="mix_fwd",
        out_shape=(jax.ShapeDtypeStruct((S, AW), F32), jax.ShapeDtypeStruct((S, AW), F32),
                   jax.ShapeDtypeStruct((S, AW + SW), BF16)),
        grid=(S // tr,),
        in_specs=[_rows(tr, AW)] * 6 + [_rows(tr, SW), _rows(tr, SW), _vec(SW), _vec(AW), _vec(SW)],
        out_specs=(_rows(tr, AW), _rows(tr, AW), _rows(tr, AW + SW)),
        compiler_params=_cparams("parallel"))(*os, *ls, y1, z, b_glu, g_attn, g_ssm)


def _final(h2, gl, e, g_post, target, *, tr=128):
    S, D = h2.shape
    tr = min(tr, S)

    def body(h_ref, gl_ref, e_ref, g_ref, t_ref, dh_ref, dgl_ref, de_ref, loss_ref, dg_ref):
        i = pl.program_id(0)
        gate = _sigmoid(gl_ref[...])
        e_ = e_ref[...]
        ge = gate * e_
        g = g_ref[...]
        diff = h_ref[...] + _rms(ge, g) - t_ref[...]
        dh = diff * (1.0 / D)
        dh_ref[...] = dh
        dge, dgrow = _rms_bwd(dh, ge, g)
        dgl_ref[...] = (dge * e_ * gate * (1.0 - gate)).astype(BF16)
        de_ref[...] = (dge * gate).astype(BF16)
        part = _colsum(0.5 * jnp.mean(diff * diff, axis=-1, keepdims=True))

        @pl.when(_first(i))
        def _():
            loss_ref[...] = jnp.zeros_like(loss_ref)
            dg_ref[...] = jnp.zeros_like(dg_ref)

        loss_ref[...] += part + jnp.zeros((1, LANES), F32)
        dg_ref[...] += _colsum(dgrow)

    return pl.pallas_call(
        body, name="final_fwd_bwd",
        out_shape=(jax.ShapeDtypeStruct((S, D), F32), jax.ShapeDtypeStruct((S, D), BF16),
                   jax.ShapeDtypeStruct((S, D), BF16), jax.ShapeDtypeStruct((1, LANES), F32),
                   jax.ShapeDtypeStruct((1, D), F32)),
        grid=(S // tr,),
        in_specs=[_rows(tr, D), _rows(tr, D), _rows(tr, D), _vec(D), _rows(tr, D)],
        out_specs=(_rows(tr, D), _rows(tr, D), _rows(tr, D), _vec(LANES), _vec(D)),
        compiler_params=_cparams("arbitrary"))(h2, gl, e, g_post, target)


def _bwd_res_norm(dh_out, dhn, h, g_next, y, g_post, *, name, tr=128):
    S, D = h.shape
    tr = min(tr, S)

    def body(dho_ref, dhn_ref, h_ref, gn_ref, y_ref, gp_ref, dh_ref, dy_ref, dgn_ref, dgp_ref):
        i = pl.program_id(0)
        dx, dgn_rows = _rms_bwd(dhn_ref[...], h_ref[...], gn_ref[...])
        dh = dho_ref[...] + dx
        dh_ref[...] = dh
        dy, dgp_rows = _rms_bwd(dh, y_ref[...], gp_ref[...])
        dy_ref[...] = dy.astype(BF16)

        @pl.when(_first(i))
        def _():
            dgn_ref[...] = jnp.zeros_like(dgn_ref)
            dgp_ref[...] = jnp.zeros_like(dgp_ref)

        dgn_ref[...] += _colsum(dgn_rows)
        dgp_ref[...] += _colsum(dgp_rows)

    return pl.pallas_call(
        body, name=name,
        out_shape=(jax.ShapeDtypeStruct((S, D), F32), jax.ShapeDtypeStruct((S, D), BF16),
                   jax.ShapeDtypeStruct((1, D), F32), jax.ShapeDtypeStruct((1, D), F32)),
        grid=(S // tr,),
        in_specs=[_rows(tr, D), _rows(tr, D), _rows(tr, D), _vec(D), _rows(tr, D), _vec(D)],
        out_specs=(_rows(tr, D), _rows(tr, D), _vec(D), _vec(D)),
        compiler_params=_cparams("arbitrary"))(dh_out, dhn, h, g_next, y, g_post)


def _bwd_first(dh1, dhn1, x, g1, *, tr=256):
    S, D = x.shape
    tr = min(tr, S)

    def body(dh_ref, dhn_ref, x_ref, g_ref, dx_ref, dg_ref):
        i = pl.program_id(0)
        dx, dg_rows = _rms_bwd(dhn_ref[...], x_ref[...], g_ref[...])
        dx_ref[...] = dh_ref[...] + dx

        @pl.when(_first(i))
        def _():
            dg_ref[...] = jnp.zeros_like(dg_ref)

        dg_ref[...] += _colsum(dg_rows)

    return pl.pallas_call(
        body, name="bwd_first",
        out_shape=(jax.ShapeDtypeStruct((S, D), F32), jax.ShapeDtypeStruct((1, D), F32)),
        grid=(S // tr,), in_specs=[_rows(tr, D), _rows(tr, D), _rows(tr, D), _vec(D)],
        out_specs=(_rows(tr, D), _vec(D)), compiler_params=_cparams("arbitrary"))(dh1, dhn1, x, g1)


def _mix_bwd(dmixed, attn, y1, z, b_glu, g_attn, g_ssm, *, tr=128):
    S, AW = attn.shape
    SW = y1.shape[1]
    tr = min(tr, S)
    heads = AW // HEAD_DIM

    def body(dm_ref, a_ref, y_ref, z_ref, b_ref, ga_ref, gs_ref,
             da_ref, dd_ref, dz_ref, dy2_ref, dga_ref, dgs_ref, db_ref):
        i = pl.program_id(0)
        attn_ = a_ref[...]
        dattn, dga_rows = _rms_bwd(dm_ref[:, :AW], attn_, ga_ref[...])
        da_ref[...] = dattn.astype(BF16)
        prod = dattn * attn_
        for h in range(heads):
            sl = slice(h * HEAD_DIM, (h + 1) * HEAD_DIM)
            dd_ref[:, sl] = jnp.broadcast_to(jnp.sum(prod[:, sl], axis=-1, keepdims=True), (tr, HEAD_DIM))
        y2 = _gelu(y_ref[...])
        gate = _sigmoid(z_ref[...] + b_ref[...])
        dssm, dgs_rows = _rms_bwd(dm_ref[:, AW:], y2 * gate, gs_ref[...])
        dz = dssm * y2 * gate * (1.0 - gate)
        dz_ref[...] = dz.astype(BF16)
        dy2_ref[...] = dssm * gate

        @pl.when(_first(i))
        def _():
            dga_ref[...] = jnp.zeros_like(dga_ref)
            dgs_ref[...] = jnp.zeros_like(dgs_ref)
            db_ref[...] = jnp.zeros_like(db_ref)

        dga_ref[...] += _colsum(dga_rows)
        dgs_ref[...] += _colsum(dgs_rows)
        db_ref[...] += _colsum(dz)

    return pl.pallas_call(
        body, name="mix_bwd",
        out_shape=(jax.ShapeDtypeStruct((S, AW), BF16), jax.ShapeDtypeStruct((S, AW), F32),
                   jax.ShapeDtypeStruct((S, SW), BF16), jax.ShapeDtypeStruct((S, SW), F32),
                   jax.ShapeDtypeStruct((1, AW), F32), jax.ShapeDtypeStruct((1, SW), F32),
                   jax.ShapeDtypeStruct((1, SW), F32)),
        grid=(S // tr,),
        in_specs=[_rows(tr, AW + SW), _rows(tr, AW), _rows(tr, SW), _rows(tr, SW), _vec(SW), _vec(AW), _vec(SW)],
        out_specs=(_rows(tr, AW), _rows(tr, AW), _rows(tr, SW), _rows(tr, SW), _vec(AW), _vec(SW), _vec(SW)),
        compiler_params=_cparams("arbitrary"))(dmixed, attn, y1, z, b_glu, g_attn, g_ssm)


def _attn_masks(i):
    row = lax.broadcasted_iota(jnp.int32, (BLK, BLK), 0)
    col = lax.broadcasted_iota(jnp.int32, (BLK, BLK), 1)
    return col <= row, jnp.logical_and(col >= row, i > 0)


_NT = (((1,), (1,)), ((), ()))
_TN = (((0,), (0,)), ((), ()))


def _attn_in_specs(width):
    def at(part, prev):
        def index(r, i):
            return (jnp.maximum(i - 1, 0) if prev else i, r * 3 + part)
        return pl.BlockSpec((BLK, width), index)
    return [at(0, False), at(1, False), at(1, True), at(2, False), at(2, True)]


def _attn_fwd(qkv, d, heads):
    M = qkv.shape[0]
    nb = M // BLK
    width = heads * HEAD_DIM
    scale = 1.0 / math.sqrt(HEAD_DIM)

    def body(q_ref, kc_ref, kp_ref, vc_ref, vp_ref, o_ref, l_ref):
        mc, mp = _attn_masks(pl.program_id(1))
        for h in range(heads):
            sl = slice(h * HEAD_DIM, (h + 1) * HEAD_DIM)
            q = q_ref[:, sl]
            sc = jnp.where(mc, lax.dot_general(q, kc_ref[:, sl], _NT, preferred_element_type=F32) * scale, NEG_INF)
            sp = jnp.where(mp, lax.dot_general(q, kp_ref[:, sl], _NT, preferred_element_type=F32) * scale, NEG_INF)
            m = jnp.maximum(jnp.max(sc, axis=-1, keepdims=True), jnp.max(sp, axis=-1, keepdims=True))
            pc, pp = jnp.exp(sc - m), jnp.exp(sp - m)
            tot = jnp.sum(pc, axis=-1, keepdims=True) + jnp.sum(pp, axis=-1, keepdims=True)
            acc = (jnp.dot(pc.astype(BF16), vc_ref[:, sl], preferred_element_type=F32)
                   + jnp.dot(pp.astype(BF16), vp_ref[:, sl], preferred_element_type=F32))
            o_ref[:, sl] = acc / tot
            l_ref[:, sl] = jnp.broadcast_to(m + jnp.log(tot), (BLK, HEAD_DIM))

    out_spec = pl.BlockSpec((BLK, width), lambda r, i: (i, r))
    shape = jax.ShapeDtypeStruct((M, d * width), F32)
    return pl.pallas_call(
        body, name=f"attn_fwd_d{d}", out_shape=(shape, shape), grid=(d, nb),
        in_specs=_attn_in_specs(width), out_specs=(out_spec, out_spec),
        compiler_params=_cparams("parallel", "parallel"))(qkv, qkv, qkv, qkv, qkv)


def _attn_bwd(qkv, dattn, lse, dd, d, heads):
    M = qkv.shape[0]
    nb = M // BLK
    width = heads * HEAD_DIM
    scale = 1.0 / math.sqrt(HEAD_DIM)

    def body(q_ref, kc_ref, kp_ref, vc_ref, vp_ref, da_ref, l_ref, dd_ref,
             dq_ref, dkc_ref, dkp_ref, dvc_ref, dvp_ref):
        mc, mp = _attn_masks(pl.program_id(1))
        for h in range(heads):
            sl = slice(h * HEAD_DIM, (h + 1) * HEAD_DIM)
            q, kc, kp, vc, vp, da = q_ref[:, sl], kc_ref[:, sl], kp_ref[:, sl], vc_ref[:, sl], vp_ref[:, sl], da_ref[:, sl]
            lse_, dd_ = l_ref[:, sl], dd_ref[:, sl]
            sc = lax.dot_general(q, kc, _NT, preferred_element_type=F32) * scale
            sp = lax.dot_general(q, kp, _NT, preferred_element_type=F32) * scale
            pc = jnp.where(mc, jnp.exp(jnp.where(mc, sc, NEG_INF) - lse_), 0.0)
            pp = jnp.where(mp, jnp.exp(jnp.where(mp, sp, NEG_INF) - lse_), 0.0)
            dsc = (pc * (lax.dot_general(da, vc, _NT, preferred_element_type=F32) - dd_) * scale).astype(BF16)
            dsp = (pp * (lax.dot_general(da, vp, _NT, preferred_element_type=F32) - dd_) * scale).astype(BF16)
            dq_ref[:, sl] = (jnp.dot(dsc, kc, preferred_element_type=F32)
                             + jnp.dot(dsp, kp, preferred_element_type=F32)).astype(BF16)
            dkc_ref[:, sl] = lax.dot_general(dsc, q, _TN, preferred_element_type=F32).astype(BF16)
            dkp_ref[:, sl] = lax.dot_general(dsp, q, _TN, preferred_element_type=F32).astype(BF16)
            dvc_ref[:, sl] = lax.dot_general(pc.astype(BF16), da, _TN, preferred_element_type=F32).astype(BF16)
            dvp_ref[:, sl] = lax.dot_general(pp.astype(BF16), da, _TN, preferred_element_type=F32).astype(BF16)

    blk = pl.BlockSpec((BLK, width), lambda r, i: (i, r))
    shape = jax.ShapeDtypeStruct((M, d * width), BF16)
    return pl.pallas_call(
        body, name=f"attn_bwd_d{d}", out_shape=(shape,) * 5, grid=(d, nb),
        in_specs=_attn_in_specs(width) + [blk, blk, blk], out_specs=(blk,) * 5,
        compiler_params=_cparams("parallel", "parallel"))(qkv, qkv, qkv, qkv, qkv, dattn, lse, dd)


def _dproj_join(dqs, dkcs, dkps, dvcs, dvps, du):
    S, AW = dqs[0].shape
    SW = du.shape[1]
    nblk = S // BLK

    def cur():
        return pl.BlockSpec((BLK, AW), lambda i: (i, 0))

    def ahead(d):
        return pl.BlockSpec((BLK, AW), lambda i: (jnp.minimum(i + d, nblk - 1), 0))

    def body(*refs):
        dq_r, dkc_r, dkp_r, dvc_r, dvp_r = (refs[3 * n:3 * n + 3] for n in range(5))
        du_ref, out_ref = refs[15], refs[16]
        i = pl.program_id(0)
        f32 = lambda ref: ref[...].astype(F32)
        dq = f32(dq_r[0]) + f32(dq_r[1]) + f32(dq_r[2])
        dk = f32(dkc_r[0]) + f32(dkc_r[1]) + f32(dkc_r[2])
        dv = f32(dvc_r[0]) + f32(dvc_r[1]) + f32(dvc_r[2])
        for n, d in enumerate(DILATIONS):
            live = i + d < nblk
            dk = dk + jnp.where(live, f32(dkp_r[n]), 0.0)
            dv = dv + jnp.where(live, f32(dvp_r[n]), 0.0)
        out_ref[:, :AW] = dq.astype(BF16)
        out_ref[:, AW:2 * AW] = dk.astype(BF16)
        out_ref[:, 2 * AW:3 * AW] = dv.astype(BF16)
        out_ref[:, 3 * AW:] = du_ref[...].astype(BF16)

    in_specs = ([cur()] * 3 + [cur()] * 3 + [ahead(d) for d in DILATIONS]
                + [cur()] * 3 + [ahead(d) for d in DILATIONS] + [pl.BlockSpec((BLK, SW), lambda i: (i, 0))])
    return pl.pallas_call(
        body, name="dproj_join", out_shape=jax.ShapeDtypeStruct((S, 3 * AW + SW), BF16), grid=(nblk,),
        in_specs=in_specs, out_specs=pl.BlockSpec((BLK, 3 * AW + SW), lambda i: (i, 0)),
        compiler_params=_cparams("parallel"))(*dqs, *dkcs, *dkps, *dvcs, *dvps, du)


def _ssm_disc(lr, li, ldt):
    dt = jnp.exp(ldt)
    mag = jnp.exp(lr * dt)
    ar = mag * jnp.cos(li * dt)
    ai = mag * jnp.sin(li * dt)
    nr = ar - 1.0
    den = lr * lr + li * li
    return ar, ai, (nr * lr + ai * li) / den, (ai * lr - nr * li) / den


def _ssm_tile_powers(lr, li, ldt, reverse):
    t = lax.broadcasted_iota(jnp.int32, (TILE, 1), 0)
    n = (TILE - t if reverse else t + 1).astype(F32)
    dt = jnp.exp(ldt)
    mag = jnp.exp(n * (lr * dt))
    ang = n * (li * dt)
    return mag * jnp.cos(ang), mag * jnp.sin(ang) * (-1.0 if reverse else 1.0)


def _cmul(ar, ai, br, bi):
    return ar * br - ai * bi, ar * bi + ai * br


def _scan(xr, xi, ar, ai, pr, pi, cr, ci, reverse):
    T = xr.shape[0]
    sub = lax.broadcasted_iota(jnp.int32, xr.shape, 0) & (TILE - 1)
    sh = 1
    while sh < TILE:
        if reverse:
            keep = sub < TILE - sh
            sr, si = pltpu.roll(xr, T - sh, 0), pltpu.roll(xi, T - sh, 0)
        else:
            keep = sub >= sh
            sr, si = pltpu.roll(xr, sh, 0), pltpu.roll(xi, sh, 0)
        sr, si = jnp.where(keep, sr, 0.0), jnp.where(keep, si, 0.0)
        qr, qi = _cmul(ar, ai, sr, si)
        xr, xi = xr + qr, xi + qi
        ar, ai = _cmul(ar, ai, ar, ai)
        sh *= 2
    n = T // TILE
    out_r, out_i = [None] * n, [None] * n
    edge = 0 if reverse else TILE - 1
    for j in (reversed(range(n)) if reverse else range(n)):
        er, ei = _cmul(pr, pi, cr, ci)
        sr, si = xr[j * TILE:(j + 1) * TILE] + er, xi[j * TILE:(j + 1) * TILE] + ei
        out_r[j], out_i[j] = sr, si
        cr, ci = sr[edge:edge + 1], si[edge:edge + 1]
    return jnp.concatenate(out_r, axis=0), jnp.concatenate(out_i, axis=0), cr, ci


def _ssm_specs(T, nch, rev):
    def t_of(c):
        return nch - 1 - c if rev else c
    tok = pl.BlockSpec((T, LANES), lambda j, c: (t_of(c), j))
    par = pl.BlockSpec((None, 1, STATE_LANES), lambda j, c: (j, 0, 0))
    bmat = pl.BlockSpec((None, LANES, STATE_LANES), lambda j, c: (j, 0, 0))
    cmat = pl.BlockSpec((None, STATE_LANES, LANES), lambda j, c: (j, 0, 0))
    dvec = pl.BlockSpec((1, LANES), lambda j, c: (0, j))
    return tok, par, bmat, cmat, dvec


def _ssm_fwd(u, lr_e, li_e, ldt_e, bre_e, bim_e, cre_e, cim_e, d_skip):
    S, SW = u.shape
    T = min(SSM_CHUNK, S)
    nch, nbk = S // T, SW // LANES
    tok, par, bmat, cmat, dvec = _ssm_specs(T, nch, False)
    state_spec = pl.BlockSpec((T, STATE_LANES), lambda j, c: (c, j))
    carry_spec = pl.BlockSpec((None, 1, STATE_LANES), lambda j, c: (c, 0, j))

    def body(u_ref, lr_ref, li_ref, ldt_ref, bre_ref, bim_ref, cre_ref, cim_ref, d_ref,
             y_ref, sr_ref, si_ref, er_ref, ei_ref, bbr, bbi, a_scr, pw, carry):
        c = pl.program_id(1)

        @pl.when(c == 0)
        def _():
            lr, li, ldt = lr_ref[...], li_ref[...], ldt_ref[...]
            ar, ai, kr, ki = _ssm_disc(lr, li, ldt)
            a_scr[0], a_scr[1] = ar, ai
            bbr[...] = (kr * bre_ref[...] - ki * bim_ref[...]).astype(BF16)
            bbi[...] = (kr * bim_ref[...] + ki * bre_ref[...]).astype(BF16)
            pw[0], pw[1] = _ssm_tile_powers(lr, li, ldt, False)
            carry[...] = jnp.zeros_like(carry)

        u_ = u_ref[...]
        ub = u_.astype(BF16)
        sr, si, cr, ci = _scan(jnp.dot(ub, bbr[...], preferred_element_type=F32),
                               jnp.dot(ub, bbi[...], preferred_element_type=F32),
                               a_scr[0], a_scr[1], pw[0], pw[1], carry[0], carry[1], False)
        carry[0], carry[1] = cr, ci
        er_ref[...], ei_ref[...] = cr, ci
        sr_ref[...], si_ref[...] = sr, si
        y0 = (jnp.dot(sr.astype(BF16), cre_ref[...].astype(BF16), preferred_element_type=F32)
              - jnp.dot(si.astype(BF16), cim_ref[...].astype(BF16), preferred_element_type=F32))
        y_ref[...] = y0 + d_ref[...] * u_

    states = jax.ShapeDtypeStruct((S, nbk * STATE_LANES), F32)
    ends = jax.ShapeDtypeStruct((nch, 1, nbk * STATE_LANES), F32)
    return pl.pallas_call(
        body, name="ssm_fwd", out_shape=(jax.ShapeDtypeStruct((S, SW), F32), states, states, ends, ends),
        grid=(nbk, nch), in_specs=[tok, par, par, par, bmat, bmat, cmat, cmat, dvec],
        out_specs=(tok, state_spec, state_spec, carry_spec, carry_spec),
        scratch_shapes=[pltpu.VMEM((LANES, STATE_LANES), BF16), pltpu.VMEM((LANES, STATE_LANES), BF16),
                        pltpu.VMEM((2, 1, STATE_LANES), F32), pltpu.VMEM((2, TILE, STATE_LANES), F32),
                        pltpu.VMEM((2, 1, STATE_LANES), F32)],
        compiler_params=_cparams("arbitrary", "arbitrary"),
    )(u, lr_e, li_e, ldt_e, bre_e, bim_e, cre_e, cim_e, d_skip)


def _ssm_bwd(u, y1, dy2a, dy2b, st_r, st_i, ends_r, ends_i, lr_e, li_e, ldt_e, bre_e, bim_e, cre_e, cim_e, d_skip):
    S, SW = u.shape
    T = min(SSM_CHUNK, S)
    nch, nbk = S // T, SW // LANES
    tok, par, bmat, cmat, dvec = _ssm_specs(T, nch, True)
    state_spec = pl.BlockSpec((T, STATE_LANES), lambda j, c: (nch - 1 - c, j))
    prev_spec = pl.BlockSpec((None, 1, STATE_LANES), lambda j, c: (jnp.maximum(nch - 2 - c, 0), 0, j))
    acc8 = pl.BlockSpec((None, 8, STATE_LANES), lambda j, c: (j, 0, 0))
    dd8 = pl.BlockSpec((None, 8, LANES), lambda j, c: (j, 0, 0))

    def body(u_ref, y_ref, da_ref, db_ref, sr_ref, si_ref, pr_ref, pi_ref, lr_ref, li_ref, ldt_ref,
             bre_ref, bim_ref, cre_ref, cim_ref, d_ref,
             du_ref, dar_ref, dai_ref, dcr_ref, dci_ref, dbr_ref, dbi_ref, ddk_ref,
             bbr, bbi, a_scr, pw, carry):
        c = pl.program_id(1)

        @pl.when(c == 0)
        def _():
            lr, li, ldt = lr_ref[...], li_ref[...], ldt_ref[...]
            ar, ai, kr, ki = _ssm_disc(lr, li, ldt)
            a_scr[0], a_scr[1] = ar, -ai
            bbr[...] = (kr * bre_ref[...] - ki * bim_ref[...]).astype(BF16)
            bbi[...] = (kr * bim_ref[...] + ki * bre_ref[...]).astype(BF16)
            pw[0], pw[1] = _ssm_tile_powers(lr, li, ldt, True)
            carry[...] = jnp.zeros_like(carry)
            for ref in (dar_ref, dai_ref, dcr_ref, dci_ref, dbr_ref, dbi_ref, ddk_ref):
                ref[...] = jnp.zeros_like(ref)

        u_ = u_ref[...]
        ub = u_.astype(BF16)
        dy1 = (da_ref[...] + db_ref[...]) * _gelu_grad(y_ref[...])
        dyb = dy1.astype(BF16)

        sr, si = sr_ref[...], si_ref[...]
        has_prev = c < nch - 1
        s0r = jnp.where(has_prev, pr_ref[...], 0.0)
        s0i = jnp.where(has_prev, pi_ref[...], 0.0)

        cre_b, cim_b = cre_ref[...].astype(BF16), cim_ref[...].astype(BF16)
        gr, gi, cr, ci = _scan(lax.dot_general(dyb, cre_b, _NT, preferred_element_type=F32),
                               -lax.dot_general(dyb, cim_b, _NT, preferred_element_type=F32),
                               a_scr[0], a_scr[1], pw[0], pw[1], carry[0], carry[1], True)
        carry[0], carry[1] = cr, ci

        row = lax.broadcasted_iota(jnp.int32, (T, STATE_LANES), 0)
        spr = jnp.where(row == 0, s0r, pltpu.roll(sr, 1, 0))
        spi = jnp.where(row == 0, s0i, pltpu.roll(si, 1, 0))

        def fold(a):
            return jnp.sum(a.reshape(T // 8, 8, a.shape[-1]), axis=0)

        dar_ref[...] += fold(gr * spr + gi * spi)
        dai_ref[...] += fold(gi * spr - gr * spi)
        srb, sib, grb, gib = sr.astype(BF16), si.astype(BF16), gr.astype(BF16), gi.astype(BF16)
        dcr_ref[...] += lax.dot_general(srb, dyb, _TN, preferred_element_type=F32)
        dci_ref[...] -= lax.dot_general(sib, dyb, _TN, preferred_element_type=F32)
        dbr_ref[...] += lax.dot_general(ub, grb, _TN, preferred_element_type=F32)
        dbi_ref[...] += lax.dot_general(ub, gib, _TN, preferred_element_type=F32)
        du_ref[...] = (lax.dot_general(grb, bbr[...], _NT, preferred_element_type=F32)
                       + lax.dot_general(gib, bbi[...], _NT, preferred_element_type=F32)
                       + dy1 * d_ref[...])
        ddk_ref[...] += fold(dy1 * u_)

    return pl.pallas_call(
        body, name="ssm_bwd",
        out_shape=(jax.ShapeDtypeStruct((S, SW), F32),
                   jax.ShapeDtypeStruct((nbk, 8, STATE_LANES), F32), jax.ShapeDtypeStruct((nbk, 8, STATE_LANES), F32),
                   jax.ShapeDtypeStruct((nbk, STATE_LANES, LANES), F32), jax.ShapeDtypeStruct((nbk, STATE_LANES, LANES), F32),
                   jax.ShapeDtypeStruct((nbk, LANES, STATE_LANES), F32), jax.ShapeDtypeStruct((nbk, LANES, STATE_LANES), F32),
                   jax.ShapeDtypeStruct((nbk, 8, LANES), F32)),
        grid=(nbk, nch),
        in_specs=[tok, tok, tok, tok, state_spec, state_spec, prev_spec, prev_spec, par, par, par,
                  bmat, bmat, cmat, cmat, dvec],
        out_specs=(tok, acc8, acc8, cmat, cmat, bmat, bmat, dd8),
        scratch_shapes=[pltpu.VMEM((LANES, STATE_LANES), BF16), pltpu.VMEM((LANES, STATE_LANES), BF16),
                        pltpu.VMEM((2, 1, STATE_LANES), F32), pltpu.VMEM((2, TILE, STATE_LANES), F32),
                        pltpu.VMEM((2, 1, STATE_LANES), F32)],
        compiler_params=_cparams("arbitrary", "arbitrary"),
    )(u, y1, dy2a, dy2b, st_r, st_i, ends_r, ends_i, lr_e, li_e, ldt_e, bre_e, bim_e, cre_e, cim_e, d_skip)


def _ssm_param_bwd(dar8, dai8, dbr_e, dbi_e, lr_e, li_e, ldt_e, bre_e, bim_e):
    nbk = lr_e.shape[0]
    par = pl.BlockSpec((None, 1, STATE_LANES), lambda j: (j, 0, 0))
    acc8 = pl.BlockSpec((None, 8, STATE_LANES), lambda j: (j, 0, 0))
    bmat = pl.BlockSpec((None, LANES, STATE_LANES), lambda j: (j, 0, 0))

    def body(dar_ref, dai_ref, dbr_ref, dbi_ref, lr_ref, li_ref, ldt_ref, bre_ref, bim_ref,
             dlr_ref, dli_ref, dldt_ref, dbre_ref, dbim_ref):
        lr, li, ldt = lr_ref[...], li_ref[...], ldt_ref[...]
        (ar, ai, kr, ki), vjp = jax.vjp(_ssm_disc, lr, li, ldt)
        dbr, dbi, bre, bim = dbr_ref[...], dbi_ref[...], bre_ref[...], bim_ref[...]
        dbre_ref[...] = kr * dbr + ki * dbi
        dbim_ref[...] = kr * dbi - ki * dbr
        dkr = _colsum(dbr * bre + dbi * bim)
        dki = _colsum(dbi * bre - dbr * bim)
        dlr, dli, dldt = vjp((_colsum(dar_ref[...]), _colsum(dai_ref[...]), dkr, dki))
        dlr_ref[...] = dlr
        dli_ref[...] = dli
        tot = jnp.broadcast_to(dldt, (8, STATE_LANES))
        sh = 1
        while sh < SSM_P:
            tot = tot + pltpu.roll(tot, STATE_LANES - sh, 1)
            sh *= 2
        dldt_ref[...] = tot[:1]

    vec = jax.ShapeDtypeStruct((nbk, 1, STATE_LANES), F32)
    mat = jax.ShapeDtypeStruct((nbk, LANES, STATE_LANES), F32)
    return pl.pallas_call(
        body, name="ssm_param_bwd", out_shape=(vec, vec, vec, mat, mat), grid=(nbk,),
        in_specs=[acc8, acc8, bmat, bmat, par, par, par, bmat, bmat],
        out_specs=(par, par, par, bmat, bmat), compiler_params=_cparams("parallel"),
    )(dar8, dai8, dbr_e, dbi_e, lr_e, li_e, ldt_e, bre_e, bim_e)


def _expand_b(b):
    G = b.shape[0]
    bt = b.transpose(0, 2, 1).reshape(G // GROUPS_PER_BLOCK, GROUPS_PER_BLOCK, SSM_C, SSM_P)
    eye = jnp.eye(GROUPS_PER_BLOCK, dtype=b.dtype)
    return (bt[:, :, :, None, :] * eye[None, :, None, :, None]).reshape(G // GROUPS_PER_BLOCK, LANES, STATE_LANES)


def _collapse_b(be):
    nbk = be.shape[0]
    eye = jnp.eye(GROUPS_PER_BLOCK, dtype=be.dtype)
    d5 = be.reshape(nbk, GROUPS_PER_BLOCK, SSM_C, GROUPS_PER_BLOCK, SSM_P)
    d4 = (d5 * eye[None, :, None, :, None]).sum(axis=3)
    return d4.transpose(0, 1, 3, 2).reshape(nbk * GROUPS_PER_BLOCK, SSM_P, SSM_C)


def _expand_c(cm):
    G = cm.shape[0]
    ct = cm.transpose(0, 2, 1).reshape(G // GROUPS_PER_BLOCK, GROUPS_PER_BLOCK, SSM_P, SSM_C)
    eye = jnp.eye(GROUPS_PER_BLOCK, dtype=cm.dtype)
    return (ct[:, :, :, None, :] * eye[None, :, None, :, None]).reshape(G // GROUPS_PER_BLOCK, STATE_LANES, LANES)


def _collapse_c(ce):
    nbk = ce.shape[0]
    eye = jnp.eye(GROUPS_PER_BLOCK, dtype=ce.dtype)
    d5 = ce.reshape(nbk, GROUPS_PER_BLOCK, SSM_P, GROUPS_PER_BLOCK, SSM_C)
    d4 = (d5 * eye[None, :, None, :, None]).sum(axis=3)
    return d4.transpose(0, 1, 3, 2).reshape(nbk * GROUPS_PER_BLOCK, SSM_C, SSM_P)


def _place():
    x, y, c = lax.axis_index("x"), lax.axis_index("y"), lax.axis_index("c")
    return x, y, c


def _other_chips(x, y):
    return [(1 - x, y), (x, 1 - y), (1 - x, 1 - y)]


_ANY = pl.BlockSpec(memory_space=pl.ANY)


_HBM = pl.BlockSpec(memory_space=pltpu.HBM)
_SEM = pl.BlockSpec(memory_space=pltpu.SEMAPHORE)
_EFFECT = pltpu.SideEffectType.DATAFLOW_SIDE_EFFECTING
_TOKEN = jax.ShapeDtypeStruct((8, LANES), F32)


def _hbm(a):
    return pltpu.with_memory_space_constraint(a, pltpu.HBM)


def _place_own(src, *, gather, name, tr=512):
    R, C = src.shape[-2:]
    tr = min(tr, R)
    x, y, _ = _place()
    me = (2 * x + y).astype(jnp.int32).reshape(1)

    def body(me_ref, s_ref, o_ref):
        o_ref[...] = s_ref[...].astype(BF16)

    own = pl.BlockSpec((None, tr, C), lambda i, me_ref: (me_ref[0], i, 0))
    grid_spec = pltpu.PrefetchScalarGridSpec(
        num_scalar_prefetch=1, grid=(R // tr,),
        in_specs=[pl.BlockSpec((tr, C), lambda i, me_ref: (i, 0)) if gather else own], out_specs=own)
    return pl.pallas_call(
        body, name=name, grid_spec=grid_spec, out_shape=jax.ShapeDtypeStruct((N_CHIPS, R, C), BF16),
        compiler_params=_cparams("parallel"))(me, src)


def _exchange_copy(src_slot, land_slot, send, recv, k, j, peer, c):
    return pltpu.make_async_remote_copy(
        src_ref=src_slot, dst_ref=land_slot, send_sem=send.at[3 * k + j], recv_sem=recv.at[3 * k + j],
        device_id=(peer[0], peer[1], c), device_id_type=MESH)


def _exchange_start(lands, srcs, groups, *, name):
    n, ng = len(lands), len(groups)
    bufs = list(lands) + list(srcs)
    nb = len(bufs)

    def body(*refs):
        lnd, src, sems = refs[:n], refs[n:nb], refs[nb:nb + 2 * ng]
        token = refs[2 * nb + 2 * ng]
        x, y, c = _place()
        me = 2 * x + y
        for gi, group in enumerate(groups):
            for k, w in enumerate(group):
                for j, peer in enumerate(_other_chips(x, y)):
                    sent = src[w].at[2 * peer[0] + peer[1]] if src else lnd[w].at[me]
                    _exchange_copy(sent, lnd[w].at[me], sems[2 * gi], sems[2 * gi + 1], k, j, peer, c).start()
        token[...] = jnp.zeros_like(token)

    sem_shapes = [pltpu.SemaphoreType.DMA((3 * len(g),)) for g in groups for _ in range(2)]
    res = pl.pallas_call(
        body, name=name,
        out_shape=sem_shapes + [pltpu.HBM(a.shape, a.dtype) for a in bufs] + [_TOKEN],
        in_specs=[_HBM] * nb,
        out_specs=[_SEM] * (2 * ng) + [_HBM] * nb + [pl.BlockSpec(memory_space=pltpu.VMEM)],
        input_output_aliases={i: 2 * ng + i for i in range(nb)},
        compiler_params=pltpu.CompilerParams(has_side_effects=_EFFECT),
    )(*[_hbm(a) for a in bufs])
    sems = [(res[2 * gi], res[2 * gi + 1]) for gi in range(ng)]
    return sems, res[2 * ng:2 * ng + n], res[2 * ng + n:2 * ng + nb], res[-1]


def _exchange_wait(lands, srcs, sems, after, *, name):
    n = len(lands)
    bufs = list(lands) + list(srcs)
    nb = len(bufs)
    send_sems, recv_sems = sems

    def body(*refs):
        lnd, src, send, recv = refs[:n], refs[n:nb], refs[nb], refs[nb + 1]
        x, y, c = _place()
        for k in range(n):
            for j, peer in enumerate(_other_chips(x, y)):
                slot = 2 * peer[0] + peer[1]
                copy = _exchange_copy((src[k] if src else lnd[k]).at[slot], lnd[k].at[slot], send, recv, k, j, peer, c)
                copy.wait_send()
                copy.wait_recv()

    res = pl.pallas_call(
        body, name=name, out_shape=[pltpu.HBM(a.shape, a.dtype) for a in bufs],
        in_specs=[_HBM] * nb + [_SEM, _SEM, _ANY], out_specs=[_HBM] * nb,
        input_output_aliases={i: i for i in range(nb)},
        compiler_params=pltpu.CompilerParams(has_side_effects=_EFFECT),
    )(*bufs, send_sems, recv_sems, after)
    return res[:n]


def _sum_partials(land, *, name, tr=256):
    _, R, C = land.shape
    tr = min(tr, R)

    def body(l_ref, o_ref):
        acc = l_ref[0].astype(F32)
        for k in range(1, N_CHIPS):
            acc = acc + l_ref[k].astype(F32)
        o_ref[...] = acc

    return pl.pallas_call(
        body, name=name, out_shape=jax.ShapeDtypeStruct((R, C), F32), grid=(R // tr,),
        in_specs=[pl.BlockSpec((N_CHIPS, tr, C), lambda i: (0, i, 0))], out_specs=_rows(tr, C),
        compiler_params=_cparams("parallel"))(land)


def _swap_with_sibling(sums, *, name):
    n = len(sums)

    def body(*refs):
        ins, outs = refs[:n], refs[n:2 * n]
        send_sems, recv_sems = refs[2 * n:]
        x, y, c = _place()
        copies = [pltpu.make_async_remote_copy(
            src_ref=ins[w], dst_ref=outs[w], send_sem=send_sems.at[w], recv_sem=recv_sems.at[w],
            device_id=(x, y, 1 - c), device_id_type=MESH) for w in range(n)]
        for cp in copies:
            cp.start()
        for cp in copies:
            cp.wait_recv()
            cp.wait_send()

    return pl.pallas_call(
        body, name=name,
        out_shape=[jax.ShapeDtypeStruct(s.shape, s.dtype) for s in sums],
        in_specs=[_ANY] * n, out_specs=[_ANY] * n,
        scratch_shapes=[pltpu.SemaphoreType.DMA((n,)), pltpu.SemaphoreType.DMA((n,))],
    )(*sums)


def _adamw_math(w, g, m, v):
    m = ADAM_B1 * m + (1.0 - ADAM_B1) * g
    v = ADAM_B2 * v + (1.0 - ADAM_B2) * (g * g)
    m_hat = m / (1.0 - ADAM_B1 ** ADAM_STEP)
    v_hat = v / (1.0 - ADAM_B2 ** ADAM_STEP)
    delta = -ADAM_LR * (m_hat / (jnp.sqrt(v_hat) + ADAM_EPS) + ADAM_WD * w)
    return delta, m, v


def _adamw_pair(mine, theirs, w, m, v, *, name, tr=128):
    R, C = w.shape
    tr = min(tr, R)

    def body(a_ref, b_ref, w_ref, m_ref, v_ref, g_ref, d_ref, nm_ref, nv_ref):
        g = a_ref[...] + b_ref[...]
        g_ref[...] = g
        d_ref[...], nm_ref[...], nv_ref[...] = _adamw_math(w_ref[...], g, m_ref[...], v_ref[...])

    shape = jax.ShapeDtypeStruct((R, C), F32)
    return pl.pallas_call(
        body, name=name, out_shape=(shape,) * 4, grid=(R // tr,),
        in_specs=[_rows(tr, C)] * 5, out_specs=(_rows(tr, C),) * 4,
        compiler_params=_cparams("parallel"))(mine, theirs, w, m, v)


def _all_gather_small(packed):
    R = packed.shape[0]

    def body(x_ref, out_ref, send_sems, recv_sems, local_sem):
        x, y, c = _place()

        def slot(px, py, pc):
            return out_ref.at[4 * px + 2 * py + pc]

        local = pltpu.make_async_copy(x_ref, slot(x, y, c), local_sem)
        local.start()
        peers = [(x ^ (k >> 2), y ^ ((k >> 1) & 1), c ^ (k & 1)) for k in range(1, N_DEV)]
        for k, peer in enumerate(peers):
            pltpu.make_async_remote_copy(
                src_ref=x_ref, dst_ref=slot(x, y, c), send_sem=send_sems.at[k], recv_sem=recv_sems.at[k],
                device_id=peer, device_id_type=MESH).start()
        for k, peer in enumerate(peers):
            arrival = pltpu.make_async_remote_copy(
                src_ref=x_ref, dst_ref=slot(*peer), send_sem=send_sems.at[k], recv_sem=recv_sems.at[k],
                device_id=peer, device_id_type=MESH)
            arrival.wait_recv()
            arrival.wait_send()
        local.wait()

    vm = pl.BlockSpec(memory_space=pltpu.VMEM)
    return pl.pallas_call(
        body, name="all_gather_small", out_shape=jax.ShapeDtypeStruct((N_DEV, R, LANES), F32),
        in_specs=[vm], out_specs=vm,
        scratch_shapes=[pltpu.SemaphoreType.DMA((N_DEV - 1,)), pltpu.SemaphoreType.DMA((N_DEV - 1,)),
                        pltpu.SemaphoreType.DMA],
        compiler_params=pltpu.CompilerParams(vmem_limit_bytes=VMEM_LIMIT_BYTES),
    )(packed)


def _adamw_small(gathered, w, m, v):
    _, R, _ = gathered.shape
    tr = PACK_ROWS

    def body(gs_ref, w_ref, m_ref, v_ref, g_ref, d_ref, nm_ref, nv_ref):
        g = gs_ref[0]
        for k in range(1, N_DEV):
            g = g + gs_ref[k]
        g_ref[...] = g
        d_ref[...], nm_ref[...], nv_ref[...] = _adamw_math(w_ref[...], g, m_ref[...], v_ref[...])

    shape = jax.ShapeDtypeStruct((R, LANES), F32)
    return pl.pallas_call(
        body, name="adamw_small", out_shape=(shape,) * 4, grid=(R // tr,),
        in_specs=[pl.BlockSpec((N_DEV, tr, LANES), lambda i: (0, i, 0))] + [_rows(tr, LANES)] * 3,
        out_specs=(_rows(tr, LANES),) * 4, compiler_params=_cparams("parallel"))(gathered, w, m, v)


def _pack(arrays):
    parts, layout = [], []
    for a in arrays:
        n = a.size
        rows = -(-n // (8 * LANES)) * 8
        flat = jnp.pad(a.reshape(-1).astype(F32), (0, rows * LANES - n))
        parts.append(flat.reshape(rows, LANES))
        layout.append((rows, n, a.shape))
    total = sum(r for r, _, _ in layout)
    parts.append(jnp.zeros((-total % PACK_ROWS, LANES), F32))
    return jnp.concatenate(parts, axis=0), layout


def _unpack(buf, layout):
    out, r0 = [], 0
    for rows, n, shape in layout:
        out.append(buf[r0:r0 + rows].reshape(-1)[:n].reshape(shape))
        r0 += rows
    return out


SMALL = ("mix_norm_pre", "lam_re", "lam_im", "log_dt", "ssm_b_re", "ssm_b_im", "ssm_c_re", "ssm_c_im",
         "ssm_d", "b_glu", "attn_out_norm", "ssm_out_norm", "mix_norm_post", "mlp_norm_pre",
         "mlp_norm_post", "ple_norm_pre", "ple_norm_post")
BIG = ("w_in", "w_glu", "w_out", "w_up", "w_down", "w_ple_gate", "w_ple_proj")
WEIGHTS = ("mix_norm_pre", "w_in", "lam_re", "lam_im", "log_dt", "ssm_b_re", "ssm_b_im", "ssm_c_re",
           "ssm_c_im", "ssm_d", "w_glu", "b_glu", "attn_out_norm", "ssm_out_norm", "w_out",
           "mix_norm_post", "mlp_norm_pre", "w_up", "w_down", "mlp_norm_post", "ple_norm_pre",
           "w_ple_gate", "w_ple_proj", "ple_norm_post")


def _to_branch(a, d):
    return a if d == 1 else a.reshape(a.shape[0] // d, d * a.shape[1])


def _from_branch(a, d, S):
    return a if d == 1 else a.reshape(S, a.shape[1] // d)


def kernel(x, p, mix_norm_pre, w_in, lam_re, lam_im, log_dt, ssm_b_re, ssm_b_im, ssm_c_re, ssm_c_im, ssm_d, w_glu, b_glu, attn_out_norm, ssm_out_norm, w_out, mix_norm_post, mlp_norm_pre, w_up, w_down, mlp_norm_post, ple_norm_pre, w_ple_gate, w_ple_proj, ple_norm_post, loss_target, m_mix_norm_pre, m_w_in, m_lam_re, m_lam_im, m_log_dt, m_ssm_b_re, m_ssm_b_im, m_ssm_c_re, m_ssm_c_im, m_ssm_d, m_w_glu, m_b_glu, m_attn_out_norm, m_ssm_out_norm, m_w_out, m_mix_norm_post, m_mlp_norm_pre, m_w_up, m_w_down, m_mlp_norm_post, m_ple_norm_pre, m_w_ple_gate, m_w_ple_proj, m_ple_norm_post, v_mix_norm_pre, v_w_in, v_lam_re, v_lam_im, v_log_dt, v_ssm_b_re, v_ssm_b_im, v_ssm_c_re, v_ssm_c_im, v_ssm_d, v_w_glu, v_b_glu, v_attn_out_norm, v_ssm_out_norm, v_w_out, v_mix_norm_post, v_mlp_norm_pre, v_w_up, v_w_down, v_mlp_norm_post, v_ple_norm_pre, v_w_ple_gate, v_w_ple_proj, v_ple_norm_post):
    args = dict(locals())
    W = {n: args[n][0] for n in WEIGHTS}
    Mo = {n: args["m_" + n][0] for n in WEIGHTS}
    Vo = {n: args["v_" + n][0] for n in WEIGHTS}
    xs, ps, tgt = x[0], p[0, 0], loss_target[0]
    S, D = xs.shape
    SW = W["ssm_d"].shape[0]
    AW = W["attn_out_norm"].shape[0]
    heads = AW // HEAD_DIM
    G = SW // SSM_C
    nbk = SW // LANES
    assert W["w_in"].shape[1] * N_CHIPS == 3 * AW + SW and AW == SW

    row = lambda a: a.reshape(1, -1)

    ag_groups = (("w_in",), ("w_glu", "w_out"), ("w_up",), ("w_down", "w_ple_gate", "w_ple_proj"))
    ag_names = [n for g in ag_groups for n in g]
    ag_sems, ag_land, _, ag_token = _exchange_start(
        [_place_own(W[n], gather=True, name="ag_place_" + n) for n in ag_names], [],
        [[ag_names.index(n) for n in g] for g in ag_groups], name="ag_start")

    def gathered(gi, after):
        got = _exchange_wait([ag_land[ag_names.index(n)] for n in ag_groups[gi]], [], ag_sems[gi], after,
                             name=f"ag_wait_{gi}")
        return dict(zip(ag_groups[gi], got))

    lr_e = W["lam_re"].reshape(nbk, 1, STATE_LANES)
    li_e = W["lam_im"].reshape(nbk, 1, STATE_LANES)
    ldt_e = jnp.repeat(W["log_dt"], SSM_P).reshape(nbk, 1, STATE_LANES)
    bre_e, bim_e = _expand_b(W["ssm_b_re"]), _expand_b(W["ssm_b_im"])
    cre_e, cim_e = _expand_c(W["ssm_c_re"]), _expand_c(W["ssm_c_im"])
    d_row = row(W["ssm_d"])

    hn1 = _norm_cast(xs, row(W["mix_norm_pre"]) + ag_token[0, 0], name="norm_in")
    w_in_f = gathered(0, hn1)["w_in"]
    qkv = _matmul(hn1, w_in_f, name="proj_qkv", out_dtype=BF16, b_shards=N_CHIPS, b_cols=(0, 3 * AW))
    u = _matmul(hn1, w_in_f, name="proj_u", b_shards=N_CHIPS, b_cols=(3 * AW, SW))
    qkv_b = [_to_branch(qkv, d) for d in DILATIONS]
    outs, lses = [], []
    for d, qb in zip(DILATIONS, qkv_b):
        o, l = _attn_fwd(qb, d, heads)
        outs.append(_from_branch(o, d, S))
        lses.append(_from_branch(l, d, S))
    y1, st_r, st_i, ends_r, ends_i = _ssm_fwd(u, lr_e, li_e, ldt_e, bre_e, bim_e, cre_e, cim_e, d_row)
    y2b = _gelu_cast(y1)
    full = gathered(1, y2b)
    w_glu_f = full["w_glu"].reshape(SW, SW)
    w_out_f = full["w_out"].reshape(AW + SW, D)
    z = _matmul(y2b, w_glu_f, name="glu_z")
    attn, lse, mixed = _mix_fwd(outs, lses, y1, z, row(W["b_glu"]), row(W["attn_out_norm"]), row(W["ssm_out_norm"]))
    mo = _matmul(mixed, w_out_f, name="mix_out")
    h1, hn2 = _res_norm(xs, mo, row(W["mix_norm_post"]), row(W["mlp_norm_pre"]), name="res_mix")
    w_up_f = gathered(2, hn2)["w_up"]
    up, act = _matmul(hn2, w_up_f, name="mlp_up", b_shards=N_CHIPS, relu2=True)
    full = gathered(3, act)
    w_down_f = full["w_down"].reshape(-1, D)
    w_pg_f = full["w_ple_gate"].reshape(D, D)
    w_pp_f = full["w_ple_proj"]
    ff = _matmul(act, w_down_f, name="mlp_down")
    h2, hn3 = _res_norm(h1, ff, row(W["mlp_norm_post"]), row(W["ple_norm_pre"]), name="res_mlp")
    gl = _matmul(hn3, w_pg_f, name="ple_gate")
    e = _matmul(ps.astype(BF16), w_pp_f, name="ple_proj", b_shards=N_CHIPS)

    dh3, dgl, de, loss_part, dg_ple_post = _final(h2, gl, e, row(W["ple_norm_post"]), tgt)
    gW = {}
    out_g, out_d, out_m, out_v = {}, {}, {}, {}

    def scatter_start(names, tag):
        parts = [gW[n] if gW[n].ndim == 3 else gW[n].reshape((N_CHIPS, -1, gW[n].shape[1])) for n in names]
        sems, land, src, token = _exchange_start(
            [_place_own(part, gather=False, name="rs_place_" + n) for n, part in zip(names, parts)], parts,
            [list(range(len(names)))], name=f"rs_start_{tag}")
        return (names, sems[0], land, src), token

    def scatter_finish(batch, after, tag):
        names, sems, land, src = batch
        landed = _exchange_wait(land, src, sems, after, name=f"rs_wait_{tag}")
        sums = [_sum_partials(l, name="sum_" + n) for n, l in zip(names, landed)]
        theirs = _swap_with_sibling(sums, name=f"swap_{tag}")
        for n, a, b in zip(names, sums, theirs):
            out_g[n], out_d[n], out_m[n], out_v[n] = _adamw_pair(a, b, W[n], Mo[n], Vo[n], name="adamw_" + n)

    gW["w_ple_proj"] = _matmul(ps.astype(BF16), de, name="d_w_ple_proj", ta=True, out_dtype=BF16, out_shards=N_CHIPS)
    gW["w_ple_gate"] = _matmul(hn3, dgl, name="d_w_ple_gate", ta=True, out_dtype=BF16)
    dhn3 = _matmul(dgl, w_pg_f, name="d_hn3", tb=True)
    dh2, dff, dg_ple_pre, dg_mlp_post = _bwd_res_norm(
        dh3, dhn3, h2, row(W["ple_norm_pre"]), ff, row(W["mlp_norm_post"]), name="bwd_res_mlp")
    gW["w_down"] = _matmul(act, dff, name="d_w_down", ta=True, out_dtype=BF16)
    batch1, token1 = scatter_start(("w_ple_proj", "w_ple_gate", "w_down"), 1)
    dup = _matmul(dff, w_down_f, name="d_up", tb=True, after=token1, relu2_of=up, out_dtype=BF16)
    gW["w_up"] = _matmul(hn2, dup, name="d_w_up", ta=True, out_dtype=BF16, out_shards=N_CHIPS)
    dhn2 = _matmul(dup, w_up_f, name="d_hn2", tb=True, b_shards=N_CHIPS)
    dh1, dmo, dg_mlp_pre, dg_mix_post = _bwd_res_norm(
        dh2, dhn2, h1, row(W["mlp_norm_pre"]), mo, row(W["mix_norm_post"]), name="bwd_res_mix")
    gW["w_out"] = _matmul(mixed, dmo, name="d_w_out", ta=True, out_dtype=BF16)
    dmixed = _matmul(dmo, w_out_f, name="d_mixed", tb=True)
    dattn, dd, dz, dy2a, dg_attn, dg_ssm, db_glu = _mix_bwd(
        dmixed, attn, y1, z, row(W["b_glu"]), row(W["attn_out_norm"]), row(W["ssm_out_norm"]))
    gW["w_glu"] = _matmul(y2b, dz, name="d_w_glu", ta=True, out_dtype=BF16)
    batch2, token2 = scatter_start(("w_up", "w_out", "w_glu"), 2)
    dy2b = _matmul(dz, w_glu_f, name="d_y2", tb=True, after=token2)
    du, dar8, dai8, dcr_e, dci_e, dbr_e, dbi_e, dd8 = _ssm_bwd(
        u, y1, dy2a, dy2b, st_r, st_i, ends_r, ends_i, lr_e, li_e, ldt_e, bre_e, bim_e, cre_e, cim_e, d_row)
    scatter_finish(batch1, du, 1)
    dlr_e, dli_e, dldt_e, dbre_e, dbim_e = _ssm_param_bwd(dar8, dai8, dbr_e, dbi_e, lr_e, li_e, ldt_e, bre_e, bim_e)

    grads5 = [[], [], [], [], []]
    for d, qb in zip(DILATIONS, qkv_b):
        res = _attn_bwd(qb, _to_branch(dattn, d), _to_branch(lse, d), _to_branch(dd, d), d, heads)
        for lst, a in zip(grads5, res):
            lst.append(_from_branch(a, d, S))
    dproj = _dproj_join(*grads5, du)
    scatter_finish(batch2, dproj, 2)
    gW["w_in"] = _matmul(hn1, dproj, name="d_w_in", ta=True, out_dtype=BF16, out_shards=N_CHIPS)
    batch3, token3 = scatter_start(("w_in",), 3)
    dhn1 = _matmul(dproj, w_in_f, name="d_hn1", tb=True, b_shards=N_CHIPS, after=token3)
    grad_x, dg_mix_pre = _bwd_first(dh1, dhn1, xs, row(W["mix_norm_pre"]))
    scatter_finish(batch3, grad_x, 3)

    small_g = {
        "mix_norm_pre": dg_mix_pre, "lam_re": dlr_e.reshape(G, SSM_P), "lam_im": dli_e.reshape(G, SSM_P),
        "log_dt": dldt_e.reshape(G, SSM_P)[:, 0], "ssm_b_re": _collapse_b(dbre_e), "ssm_b_im": _collapse_b(dbim_e),
        "ssm_c_re": _collapse_c(dcr_e), "ssm_c_im": _collapse_c(dci_e), "ssm_d": dd8.sum(axis=1).reshape(-1),
        "b_glu": db_glu, "attn_out_norm": dg_attn, "ssm_out_norm": dg_ssm, "mix_norm_post": dg_mix_post,
        "mlp_norm_pre": dg_mlp_pre, "mlp_norm_post": dg_mlp_post, "ple_norm_pre": dg_ple_pre,
        "ple_norm_post": dg_ple_post,
    }
    g_pack, layout = _pack([small_g[n].reshape(W[n].shape) for n in SMALL])
    w_pack, _ = _pack([W[n] for n in SMALL])
    m_pack, _ = _pack([Mo[n] for n in SMALL])
    v_pack, _ = _pack([Vo[n] for n in SMALL])
    packed = _adamw_small(_all_gather_small(g_pack), w_pack, m_pack, v_pack)
    for dst, buf in zip((out_g, out_d, out_m, out_v), packed):
        dst.update(zip(SMALL, _unpack(buf, layout)))

    loss = lax.psum(loss_part[0, 0], ("x", "y", "c"))
    lead = lambda a: a[None]
    return (loss, grad_x[None],
            *[lead(out_g[n]) for n in WEIGHTS], *[lead(out_d[n]) for n in WEIGHTS],
            *[lead(out_m[n]) for n in WEIGHTS], *[lead(out_v[n]) for n in WEIGHTS])
```

```python
import functools
import math

import jax
import jax.numpy as jnp
from jax import lax
from jax.experimental import pallas as pl
from jax.experimental.pallas import tpu as pltpu

F32 = jnp.float32
BF16 = jnp.bfloat16
MESH = pl.DeviceIdType.MESH

RMS_EPS = 1e-6
NEG_INF = -1e30
HEAD_DIM = 128
BLK = 128
DILATIONS = (1, 4, 16)
SSM_C = 16
SSM_P = 64
LANES = 128
GROUPS_PER_BLOCK = LANES // SSM_C
STATE_LANES = GROUPS_PER_BLOCK * SSM_P
SSM_CHUNK = 128
TILE = 8
ADAM_LR, ADAM_B1, ADAM_B2, ADAM_EPS, ADAM_WD, ADAM_STEP = 1e-3, 0.9, 0.999, 1e-8, 0.01, 10
VMEM_LIMIT_BYTES = 56 * 1024 * 1024
N_CHIPS = 4
N_DEV = 8
PACK_ROWS = 256


def _cparams(*sem):
    return pltpu.CompilerParams(dimension_semantics=sem or None, vmem_limit_bytes=VMEM_LIMIT_BYTES)


def _rows(tr, w):
    return pl.BlockSpec((tr, w), lambda i: (i, 0))


def _vec(w):
    return pl.BlockSpec((1, w), lambda i: (0, 0))


def _sigmoid(x):
    return 1.0 / (1.0 + jnp.exp(-x))


def _gelu(x):
    c = math.sqrt(2.0 / math.pi)
    return 0.5 * x * (1.0 + jnp.tanh(c * (x + 0.044715 * x * x * x)))


def _gelu_grad(x):
    c = math.sqrt(2.0 / math.pi)
    th = jnp.tanh(c * (x + 0.044715 * x * x * x))
    return 0.5 * (1.0 + th) + 0.5 * x * (1.0 - th * th) * c * (1.0 + 3.0 * 0.044715 * x * x)


def _rms(x, g):
    r = lax.rsqrt(jnp.mean(x * x, axis=-1, keepdims=True) + RMS_EPS)
    return x * r * g


def _rms_bwd(dy, x, g):
    r = lax.rsqrt(jnp.mean(x * x, axis=-1, keepdims=True) + RMS_EPS)
    n = x * r
    dn = dy * g
    dx = r * (dn - n * jnp.mean(dn * n, axis=-1, keepdims=True))
    return dx, dy * n


def _colsum(a):
    return jnp.sum(a, axis=0, keepdims=True)


def _first(i):
    return i == 0


def _matmul(a, b, *, name, ta=False, tb=False, out_dtype=F32, b_shards=1, out_shards=1, b_cols=None,
            after=None, relu2=False, relu2_of=None, tm=1024, tn=1024, tk=2048):
    if ta:
        K, M = a.shape
    else:
        M, K = a.shape
    if b_shards > 1:
        rows, cols = b.shape[1], b.shape[2] * b_shards
    else:
        rows, cols = b.shape
    N, Kb = (rows, cols) if tb else (cols, rows)
    assert K == Kb, (a.shape, b.shape, ta, tb)
    col0 = 0
    if b_cols is not None:
        assert not tb
        col0, N = b_cols
    tm, tn, tk = min(tm, M), min(tn, N), min(tk, K)
    if b_shards > 1:
        shard_cols = cols // b_shards
        if tb:
            tk = min(tk, shard_cols)
        else:
            tn = min(tn, shard_cols)
    if out_shards > 1:
        tn = min(tn, N // out_shards)
    assert M % tm == 0 and N % tn == 0 and K % tk == 0 and col0 % tn == 0
    nk = K // tk
    j0 = col0 // tn

    a_spec = (pl.BlockSpec((tk, tm), lambda i, j, k: (k, i)) if ta
              else pl.BlockSpec((tm, tk), lambda i, j, k: (i, k)))
    if b_shards > 1:
        if tb:
            per = shard_cols // tk
            b_spec = pl.BlockSpec((None, tn, tk), lambda i, j, k: (k // per, j, k % per))
        else:
            per = shard_cols // tn
            b_spec = pl.BlockSpec((None, tk, tn), lambda i, j, k: ((j + j0) // per, k, (j + j0) % per))
    else:
        b_spec = (pl.BlockSpec((tn, tk), lambda i, j, k: (j, k)) if tb
                  else pl.BlockSpec((tk, tn), lambda i, j, k: (k, j + j0)))
    if out_shards > 1:
        per_o = (N // out_shards) // tn
        out_shape = jax.ShapeDtypeStruct((out_shards, M, N // out_shards), out_dtype)
        out_spec = pl.BlockSpec((None, tm, tn), lambda i, j, k: (j // per_o, i, j % per_o))
    else:
        out_shape = jax.ShapeDtypeStruct((M, N), out_dtype)
        out_spec = pl.BlockSpec((tm, tn), lambda i, j, k: (i, j))
    dims = (((0 if ta else 1,), (1 if tb else 0,)), ((), ()))

    extra, extra_specs = [], []
    if relu2_of is not None:
        assert out_shards == 1 and relu2_of.shape == (M, N)
        extra.append(relu2_of)
        extra_specs.append(pl.BlockSpec((tm, tn), lambda i, j, k: (i, j)))
    if after is not None:
        extra.append(after)
        extra_specs.append(pl.BlockSpec(after.shape, lambda i, j, k: (0, 0)))
    n_in = 2 + len(extra)
    if relu2:
        assert out_shards == 1
        out_shape = (out_shape, jax.ShapeDtypeStruct((M, N), BF16))
        out_spec = (out_spec, out_spec)

    def finish(acc, refs):
        o_ref = refs[n_in]
        if relu2_of is not None:
            acc = acc * (2.0 * jnp.maximum(refs[2][...], 0.0))
        o_ref[...] = acc.astype(o_ref.dtype)
        if relu2:
            r = jnp.maximum(acc, 0.0)
            refs[n_in + 1][...] = (r * r).astype(BF16)

    def body(*refs):
        prod = lax.dot_general(refs[0][...], refs[1][...], dims, preferred_element_type=F32)
        if nk == 1:
            finish(prod, refs)
            return
        acc_ref = refs[-1]
        k = pl.program_id(2)

        @pl.when(k == 0)
        def _():
            acc_ref[...] = prod

        @pl.when(k > 0)
        def _():
            acc_ref[...] += prod

        @pl.when(k == nk - 1)
        def _():
            finish(acc_ref[...], refs)

    return pl.pallas_call(
        body, name=name, out_shape=out_shape, grid=(M // tm, N // tn, nk),
        in_specs=[a_spec, b_spec] + extra_specs, out_specs=out_spec,
        scratch_shapes=[pltpu.VMEM((tm, tn), F32)] if nk > 1 else [],
        compiler_params=_cparams("parallel", "parallel", "arbitrary"),
    )(a, b, *extra)


def _norm_cast(x, g, *, name, tr=256):
    S, D = x.shape
    tr = min(tr, S)

    def body(x_ref, g_ref, o_ref):
        o_ref[...] = _rms(x_ref[...], g_ref[...]).astype(BF16)

    return pl.pallas_call(
        body, name=name, out_shape=jax.ShapeDtypeStruct((S, D), BF16), grid=(S // tr,),
        in_specs=[_rows(tr, D), _vec(D)], out_specs=_rows(tr, D),
        compiler_params=_cparams("parallel"))(x, g)


def _res_norm(res, y, g_post, g_next, *, name, tr=256):
    S, D = res.shape
    tr = min(tr, S)

    def body(res_ref, y_ref, gp_ref, gn_ref, h_ref, hn_ref):
        h = res_ref[...] + _rms(y_ref[...], gp_ref[...])
        h_ref[...] = h
        hn_ref[...] = _rms(h, gn_ref[...]).astype(BF16)

    return pl.pallas_call(
        body, name=name,
        out_shape=(jax.ShapeDtypeStruct((S, D), F32), jax.ShapeDtypeStruct((S, D), BF16)),
        grid=(S // tr,), in_specs=[_rows(tr, D), _rows(tr, D), _vec(D), _vec(D)],
        out_specs=(_rows(tr, D), _rows(tr, D)), compiler_params=_cparams("parallel"))(res, y, g_post, g_next)


def _gelu_cast(y1, *, tr=256):
    S, W = y1.shape
    tr = min(tr, S)

    def body(y_ref, o_ref):
        o_ref[...] = _gelu(y_ref[...]).astype(BF16)

    return pl.pallas_call(
        body, name="gelu_cast", out_shape=jax.ShapeDtypeStruct((S, W), BF16), grid=(S // tr,),
        in_specs=[_rows(tr, W)], out_specs=_rows(tr, W), compiler_params=_cparams("parallel"))(y1)


def _residue_spec(tr, d, w):
    return pl.BlockSpec((tr // d, d * w), lambda i: (i, 0))


def _residue_shape(S, d, w, dtype):
    return jax.ShapeDtypeStruct((S // d, d * w), dtype)


def _residue_scratch(rows, w):
    return pltpu.VMEM((w // LANES, rows, LANES), F32)


def _fill_strips(scr, val):
    for s in range(scr.shape[0]):
        scr[s] = val[:, s * LANES:(s + 1) * LANES]


def _strips_to_residues(scr, o_ref, d):
    strips, rows, _ = scr.shape
    for r in range(d):
        for s in range(strips):
            col = (r * strips + s) * LANES
            o_ref[:, col:col + LANES] = scr[s, pl.ds(r, rows // d, stride=d), :].astype(o_ref.dtype)


def _to_residues(scr, val, o_ref, d):
    if d == 1:
        o_ref[...] = val.astype(o_ref.dtype)
        return
    _fill_strips(scr, val)
    _strips_to_residues(scr, o_ref, d)


def _from_residues(scr, in_ref, d):
    if d == 1:
        return in_ref[...].astype(F32)
    strips, rows, _ = scr.shape
    for r in range(d):
        for s in range(strips):
            col = (r * strips + s) * LANES
            scr[s, pl.ds(r, rows // d, stride=d), :] = in_ref[:, col:col + LANES].astype(F32)
    return jnp.concatenate([scr[s] for s in range(strips)], axis=1)


def _mix_fwd(os, ls, y1, z, b_glu, g_attn, g_ssm, *, tr=128):
    S, SW = y1.shape
    AW = os[0].shape[1]
    tr = min(tr, S)
    nd = len(DILATIONS)

    def body(*refs):
        o_refs, l_refs = refs[:nd], refs[nd:2 * nd]
        y_ref, z_ref, b_ref, ga_ref, gs_ref, attn_ref = refs[2 * nd:2 * nd + 6]
        lse_refs = refs[2 * nd + 6:3 * nd + 6]
        mixed_ref, scr = refs[3 * nd + 6:]
        ls_ = [_from_residues(scr, l_refs[n], d) for n, d in enumerate(DILATIONS)]
        m = functools.reduce(jnp.maximum, ls_)
        es = [jnp.exp(l - m) for l in ls_]
        tot = functools.reduce(jnp.add, es)
        attn = functools.reduce(jnp.add, [e * _from_residues(scr, o_refs[n], d)
                                          for n, (e, d) in enumerate(zip(es, DILATIONS))]) / tot
        attn_ref[...] = attn
        lse = m + jnp.log(tot)
        for n, d in enumerate(DILATIONS):
            _to_residues(scr, lse, lse_refs[n], d)
        ssm = _gelu(y_ref[...]) * _sigmoid(z_ref[...] + b_ref[...])
        mixed_ref[:, :AW] = _rms(attn, ga_ref[...]).astype(BF16)
        mixed_ref[:, AW:] = _rms(ssm, gs_ref[...]).astype(BF16)

    res_in = [_residue_spec(tr, d, AW) for d in DILATIONS]
    res = pl.pallas_call(
        body, name="mix_fwd",
        out_shape=([jax.ShapeDtypeStruct((S, AW), F32)] + [_residue_shape(S, d, AW, F32) for d in DILATIONS]
                   + [jax.ShapeDtypeStruct((S, AW + SW), BF16)]),
        grid=(S // tr,),
        in_specs=res_in + res_in + [_rows(tr, SW), _rows(tr, SW), _vec(SW), _vec(AW), _vec(SW)],
        out_specs=[_rows(tr, AW)] + res_in + [_rows(tr, AW + SW)],
        scratch_shapes=[_residue_scratch(tr, AW)],
        compiler_params=_cparams("parallel"))(*os, *ls, y1, z, b_glu, g_attn, g_ssm)
    return res[0], res[1:1 + nd], res[1 + nd]


def _final(h2, gl, e, g_post, target, *, tr=128):
    S, D = h2.shape
    tr = min(tr, S)

    def body(h_ref, gl_ref, e_ref, g_ref, t_ref, dh_ref, dgl_ref, de_ref, loss_ref, dg_ref):
        i = pl.program_id(0)
        gate = _sigmoid(gl_ref[...])
        e_ = e_ref[...]
        ge = gate * e_
        g = g_ref[...]
        diff = h_ref[...] + _rms(ge, g) - t_ref[...]
        dh = diff * (1.0 / D)
        dh_ref[...] = dh
        dge, dgrow = _rms_bwd(dh, ge, g)
        dgl_ref[...] = (dge * e_ * gate * (1.0 - gate)).astype(BF16)
        de_ref[...] = (dge * gate).astype(BF16)
        part = _colsum(0.5 * jnp.mean(diff * diff, axis=-1, keepdims=True))

        @pl.when(_first(i))
        def _():
            loss_ref[...] = jnp.zeros_like(loss_ref)
            dg_ref[...] = jnp.zeros_like(dg_ref)

        loss_ref[...] += part + jnp.zeros((1, LANES), F32)
        dg_ref[...] += _colsum(dgrow)

    return pl.pallas_call(
        body, name="final_fwd_bwd",
        out_shape=(jax.ShapeDtypeStruct((S, D), F32), jax.ShapeDtypeStruct((S, D), BF16),
                   jax.ShapeDtypeStruct((S, D), BF16), jax.ShapeDtypeStruct((1, LANES), F32),
                   jax.ShapeDtypeStruct((1, D), F32)),
        grid=(S // tr,),
        in_specs=[_rows(tr, D), _rows(tr, D), _rows(tr, D), _vec(D), _rows(tr, D)],
        out_specs=(_rows(tr, D), _rows(tr, D), _rows(tr, D), _vec(LANES), _vec(D)),
        compiler_params=_cparams("arbitrary"))(h2, gl, e, g_post, target)


def _bwd_res_norm(dh_out, dhn, h, g_next, y, g_post, *, name, tr=128):
    S, D = h.shape
    tr = min(tr, S)

    def body(dho_ref, dhn_ref, h_ref, gn_ref, y_ref, gp_ref, dh_ref, dy_ref, dgn_ref, dgp_ref):
        i = pl.program_id(0)
        dx, dgn_rows = _rms_bwd(dhn_ref[...], h_ref[...], gn_ref[...])
        dh = dho_ref[...] + dx
        dh_ref[...] = dh
        dy, dgp_rows = _rms_bwd(dh, y_ref[...], gp_ref[...])
        dy_ref[...] = dy.astype(BF16)

        @pl.when(_first(i))
        def _():
            dgn_ref[...] = jnp.zeros_like(dgn_ref)
            dgp_ref[...] = jnp.zeros_like(dgp_ref)

        dgn_ref[...] += _colsum(dgn_rows)
        dgp_ref[...] += _colsum(dgp_rows)

    return pl.pallas_call(
        body, name=name,
        out_shape=(jax.ShapeDtypeStruct((S, D), F32), jax.ShapeDtypeStruct((S, D), BF16),
                   jax.ShapeDtypeStruct((1, D), F32), jax.ShapeDtypeStruct((1, D), F32)),
        grid=(S // tr,),
        in_specs=[_rows(tr, D), _rows(tr, D), _rows(tr, D), _vec(D), _rows(tr, D), _vec(D)],
        out_specs=(_rows(tr, D), _rows(tr, D), _vec(D), _vec(D)),
        compiler_params=_cparams("arbitrary"))(dh_out, dhn, h, g_next, y, g_post)


def _bwd_first(dh1, dhn1, x, g1, *, tr=256):
    S, D = x.shape
    tr = min(tr, S)

    def body(dh_ref, dhn_ref, x_ref, g_ref, dx_ref, dg_ref):
        i = pl.program_id(0)
        dx, dg_rows = _rms_bwd(dhn_ref[...], x_ref[...], g_ref[...])
        dx_ref[...] = dh_ref[...] + dx

        @pl.when(_first(i))
        def _():
            dg_ref[...] = jnp.zeros_like(dg_ref)

        dg_ref[...] += _colsum(dg_rows)

    return pl.pallas_call(
        body, name="bwd_first",
        out_shape=(jax.ShapeDtypeStruct((S, D), F32), jax.ShapeDtypeStruct((1, D), F32)),
        grid=(S // tr,), in_specs=[_rows(tr, D), _rows(tr, D), _rows(tr, D), _vec(D)],
        out_specs=(_rows(tr, D), _vec(D)), compiler_params=_cparams("arbitrary"))(dh1, dhn1, x, g1)


def _mix_bwd(dmixed, attn, y1, z, b_glu, g_attn, g_ssm, *, tr=256):
    S, AW = attn.shape
    SW = y1.shape[1]
    tr = min(tr, S)
    heads = AW // HEAD_DIM
    nd = len(DILATIONS)

    def body(*refs):
        dm_ref, a_ref, y_ref, z_ref, b_ref, ga_ref, gs_ref = refs[:7]
        da_refs, dd_refs = refs[7:7 + nd], refs[7 + nd:7 + 2 * nd]
        dz_ref, dy2_ref, dga_ref, dgs_ref, db_ref, scr, dd_scr = refs[7 + 2 * nd:]
        i = pl.program_id(0)
        attn_ = a_ref[...]
        dattn, dga_rows = _rms_bwd(dm_ref[:, :AW], attn_, ga_ref[...])
        prod = dattn * attn_
        for h in range(heads):
            sl = slice(h * HEAD_DIM, (h + 1) * HEAD_DIM)
            dd_scr[:, sl] = jnp.broadcast_to(jnp.sum(prod[:, sl], axis=-1, keepdims=True), (tr, HEAD_DIM))
        for n, d in enumerate(DILATIONS):
            _to_residues(scr, dattn, da_refs[n], d)
            _to_residues(scr, dd_scr[...], dd_refs[n], d)
        y2 = _gelu(y_ref[...])
        gate = _sigmoid(z_ref[...] + b_ref[...])
        dssm, dgs_rows = _rms_bwd(dm_ref[:, AW:], y2 * gate, gs_ref[...])
        dz = dssm * y2 * gate * (1.0 - gate)
        dz_ref[...] = dz.astype(BF16)
        dy2_ref[...] = dssm * gate

        @pl.when(_first(i))
        def _():
            dga_ref[...] = jnp.zeros_like(dga_ref)
            dgs_ref[...] = jnp.zeros_like(dgs_ref)
            db_ref[...] = jnp.zeros_like(db_ref)

        dga_ref[...] += _colsum(dga_rows)
        dgs_ref[...] += _colsum(dgs_rows)
        db_ref[...] += _colsum(dz)

    res_out = [_residue_spec(tr, d, AW) for d in DILATIONS]
    res = pl.pallas_call(
        body, name="mix_bwd",
        out_shape=([_residue_shape(S, d, AW, BF16) for d in DILATIONS]
                   + [_residue_shape(S, d, AW, F32) for d in DILATIONS]
                   + [jax.ShapeDtypeStruct((S, SW), BF16), jax.ShapeDtypeStruct((S, SW), F32),
                      jax.ShapeDtypeStruct((1, AW), F32), jax.ShapeDtypeStruct((1, SW), F32),
                      jax.ShapeDtypeStruct((1, SW), F32)]),
        grid=(S // tr,),
        in_specs=[_rows(tr, AW + SW), _rows(tr, AW), _rows(tr, SW), _rows(tr, SW), _vec(SW), _vec(AW), _vec(SW)],
        out_specs=res_out + res_out + [_rows(tr, SW), _rows(tr, SW), _vec(AW), _vec(SW), _vec(SW)],
        scratch_shapes=[_residue_scratch(tr, AW), pltpu.VMEM((tr, AW), F32)],
        compiler_params=_cparams("arbitrary"))(dmixed, attn, y1, z, b_glu, g_attn, g_ssm)
    return (res[:nd], res[nd:2 * nd]) + tuple(res[2 * nd:])


def _attn_masks(i):
    row = lax.broadcasted_iota(jnp.int32, (BLK, BLK), 0)
    col = lax.broadcasted_iota(jnp.int32, (BLK, BLK), 1)
    return col <= row, jnp.logical_and(col >= row, i > 0)


_NT = (((1,), (1,)), ((), ()))
_TN = (((0,), (0,)), ((), ()))


def _attn_in_specs(width, block_of):
    def at(part, prev):
        def index(r, i):
            blk = block_of(i)
            return (part, jnp.maximum(blk - 1, 0) if prev else blk, r)
        return pl.BlockSpec((None, BLK, width), index)
    return [at(0, False), at(1, False), at(1, True), at(2, False), at(2, True)]


def _proj_qkv(hn, w_in_f, *, tm=1024):
    S, D = hn.shape
    AW = w_in_f.shape[2]
    tm = min(tm, S)

    def body(a_ref, b_ref, *rest):
        o_refs, scr = rest[:-1], rest[-1]
        prod = jnp.dot(a_ref[...], b_ref[...], preferred_element_type=F32)
        _fill_strips(scr, prod)
        for o_ref, d in zip(o_refs, DILATIONS):
            if d == 1:
                o_ref[...] = prod.astype(BF16)
            else:
                _strips_to_residues(scr, o_ref, d)

    return pl.pallas_call(
        body, name="proj_qkv",
        out_shape=[jax.ShapeDtypeStruct((3, S // d, d * AW), BF16) for d in DILATIONS], grid=(S // tm, 3),
        in_specs=[pl.BlockSpec((tm, D), lambda i, j: (i, 0)), pl.BlockSpec((None, D, AW), lambda i, j: (j, 0, 0))],
        out_specs=[pl.BlockSpec((None, tm // d, d * AW), lambda i, j: (j, i, 0)) for d in DILATIONS],
        scratch_shapes=[_residue_scratch(tm, AW)],
        compiler_params=_cparams("parallel", "parallel"))(hn, w_in_f)


def _attn_fwd(qkv, d, heads):
    M = qkv.shape[1]
    nb = M // BLK
    width = heads * HEAD_DIM
    scale = 1.0 / math.sqrt(HEAD_DIM)

    def body(q_ref, kc_ref, kp_ref, vc_ref, vp_ref, o_ref, l_ref):
        mc, mp = _attn_masks(pl.program_id(1))
        for h in range(heads):
            sl = slice(h * HEAD_DIM, (h + 1) * HEAD_DIM)
            q = q_ref[:, sl]
            sc = jnp.where(mc, lax.dot_general(q, kc_ref[:, sl], _NT, preferred_element_type=F32) * scale, NEG_INF)
            sp = jnp.where(mp, lax.dot_general(q, kp_ref[:, sl], _NT, preferred_element_type=F32) * scale, NEG_INF)
            m = jnp.maximum(jnp.max(sc, axis=-1, keepdims=True), jnp.max(sp, axis=-1, keepdims=True))
            pc, pp = jnp.exp(sc - m), jnp.exp(sp - m)
            tot = jnp.sum(pc, axis=-1, keepdims=True) + jnp.sum(pp, axis=-1, keepdims=True)
            acc = (jnp.dot(pc.astype(BF16), vc_ref[:, sl], preferred_element_type=F32)
                   + jnp.dot(pp.astype(BF16), vp_ref[:, sl], preferred_element_type=F32))
            o_ref[:, sl] = acc / tot
            l_ref[:, sl] = jnp.broadcast_to(m + jnp.log(tot), (BLK, HEAD_DIM))

    out_spec = pl.BlockSpec((BLK, width), lambda r, i: (i, r))
    shape = jax.ShapeDtypeStruct((M, d * width), F32)
    return pl.pallas_call(
        body, name=f"attn_fwd_d{d}", out_shape=(shape, shape), grid=(d, nb),
        in_specs=_attn_in_specs(width, lambda i: i), out_specs=(out_spec, out_spec),
        compiler_params=_cparams("parallel", "parallel"))(qkv, qkv, qkv, qkv, qkv)


def _attn_bwd(qkv, dattn, lse, dd, d, heads):
    M = qkv.shape[1]
    nb = M // BLK
    width = heads * HEAD_DIM
    scale = 1.0 / math.sqrt(HEAD_DIM)

    def block_of(i):
        return nb - 1 - i

    def body(q_ref, kc_ref, kp_ref, vc_ref, vp_ref, da_ref, l_ref, dd_ref,
             dq_ref, dk_ref, dv_ref, dk_carry, dv_carry):
        @pl.when(pl.program_id(1) == 0)
        def _():
            dk_carry[...] = jnp.zeros_like(dk_carry)
            dv_carry[...] = jnp.zeros_like(dv_carry)

        mc, mp = _attn_masks(block_of(pl.program_id(1)))
        for h in range(heads):
            sl = slice(h * HEAD_DIM, (h + 1) * HEAD_DIM)
            q, kc, kp, vc, vp, da = q_ref[:, sl], kc_ref[:, sl], kp_ref[:, sl], vc_ref[:, sl], vp_ref[:, sl], da_ref[:, sl]
            lse_, dd_ = l_ref[:, sl], dd_ref[:, sl]
            sc = lax.dot_general(q, kc, _NT, preferred_element_type=F32) * scale
            sp = lax.dot_general(q, kp, _NT, preferred_element_type=F32) * scale
            pc = jnp.where(mc, jnp.exp(jnp.where(mc, sc, NEG_INF) - lse_), 0.0)
            pp = jnp.where(mp, jnp.exp(jnp.where(mp, sp, NEG_INF) - lse_), 0.0)
            dsc = (pc * (lax.dot_general(da, vc, _NT, preferred_element_type=F32) - dd_) * scale).astype(BF16)
            dsp = (pp * (lax.dot_general(da, vp, _NT, preferred_element_type=F32) - dd_) * scale).astype(BF16)
            dq_ref[:, sl] = (jnp.dot(dsc, kc, preferred_element_type=F32)
                             + jnp.dot(dsp, kp, preferred_element_type=F32)).astype(BF16)
            dk_ref[:, sl] = (lax.dot_general(dsc, q, _TN, preferred_element_type=F32) + dk_carry[:, sl]).astype(BF16)
            dv_ref[:, sl] = (lax.dot_general(pc.astype(BF16), da, _TN, preferred_element_type=F32)
                             + dv_carry[:, sl]).astype(BF16)
            dk_carry[:, sl] = lax.dot_general(dsp, q, _TN, preferred_element_type=F32)
            dv_carry[:, sl] = lax.dot_general(pp.astype(BF16), da, _TN, preferred_element_type=F32)

    blk = pl.BlockSpec((BLK, width), lambda r, i: (block_of(i), r))
    shape = jax.ShapeDtypeStruct((M, d * width), BF16)
    return pl.pallas_call(
        body, name=f"attn_bwd_d{d}", out_shape=(shape,) * 3, grid=(d, nb),
        in_specs=_attn_in_specs(width, block_of) + [blk, blk, blk], out_specs=(blk,) * 3,
        scratch_shapes=[pltpu.VMEM((BLK, width), F32), pltpu.VMEM((BLK, width), F32)],
        compiler_params=_cparams("arbitrary", "arbitrary"))(qkv, qkv, qkv, qkv, qkv, dattn, lse, dd)


def _dproj_join(dqs, dks, dvs, du, *, tr=256):
    S, SW = du.shape
    AW = dqs[0].shape[1]
    tr = min(tr, S)
    nd = len(DILATIONS)

    def body(*refs):
        du_ref, out_ref, scr = refs[3 * nd:]
        for part in range(3):
            total = functools.reduce(jnp.add, [_from_residues(scr, refs[part * nd + n], d)
                                               for n, d in enumerate(DILATIONS)])
            out_ref[:, part * AW:(part + 1) * AW] = total.astype(BF16)
        out_ref[:, 3 * AW:] = du_ref[...].astype(BF16)

    return pl.pallas_call(
        body, name="dproj_join", out_shape=jax.ShapeDtypeStruct((S, 3 * AW + SW), BF16), grid=(S // tr,),
        in_specs=[_residue_spec(tr, d, AW) for d in DILATIONS] * 3 + [_rows(tr, SW)],
        out_specs=_rows(tr, 3 * AW + SW), scratch_shapes=[_residue_scratch(tr, AW)],
        compiler_params=_cparams("parallel"))(*dqs, *dks, *dvs, du)


def _ssm_disc(lr, li, ldt):
    dt = jnp.exp(ldt)
    mag = jnp.exp(lr * dt)
    ar = mag * jnp.cos(li * dt)
    ai = mag * jnp.sin(li * dt)
    nr = ar - 1.0
    den = lr * lr + li * li
    return ar, ai, (nr * lr + ai * li) / den, (ai * lr - nr * li) / den


def _ssm_tile_powers(lr, li, ldt, reverse):
    t = lax.broadcasted_iota(jnp.int32, (TILE, 1), 0)
    n = (TILE - t if reverse else t + 1).astype(F32)
    dt = jnp.exp(ldt)
    mag = jnp.exp(n * (lr * dt))
    ang = n * (li * dt)
    return mag * jnp.cos(ang), mag * jnp.sin(ang) * (-1.0 if reverse else 1.0)


def _cmul(ar, ai, br, bi):
    return ar * br - ai * bi, ar * bi + ai * br


def _scan(xr, xi, ar, ai, pr, pi, cr, ci, reverse):
    T = xr.shape[0]
    sub = lax.broadcasted_iota(jnp.int32, xr.shape, 0) & (TILE - 1)
    sh = 1
    while sh < TILE:
        if reverse:
            keep = sub < TILE - sh
            sr, si = pltpu.roll(xr, T - sh, 0), pltpu.roll(xi, T - sh, 0)
        else:
            keep = sub >= sh
            sr, si = pltpu.roll(xr, sh, 0), pltpu.roll(xi, sh, 0)
        sr, si = jnp.where(keep, sr, 0.0), jnp.where(keep, si, 0.0)
        qr, qi = _cmul(ar, ai, sr, si)
        xr, xi = xr + qr, xi + qi
        ar, ai = _cmul(ar, ai, ar, ai)
        sh *= 2
    n = T // TILE
    out_r, out_i = [None] * n, [None] * n
    edge = 0 if reverse else TILE - 1
    for j in (reversed(range(n)) if reverse else range(n)):
        er, ei = _cmul(pr, pi, cr, ci)
        sr, si = xr[j * TILE:(j + 1) * TILE] + er, xi[j * TILE:(j + 1) * TILE] + ei
        out_r[j], out_i[j] = sr, si
        cr, ci = sr[edge:edge + 1], si[edge:edge + 1]
    return jnp.concatenate(out_r, axis=0), jnp.concatenate(out_i, axis=0), cr, ci


def _ssm_specs(T, nch, rev):
    def t_of(c):
        return nch - 1 - c if rev else c
    tok = pl.BlockSpec((T, LANES), lambda j, c: (t_of(c), j))
    par = pl.BlockSpec((None, 1, STATE_LANES), lambda j, c: (j, 0, 0))
    bmat = pl.BlockSpec((None, LANES, STATE_LANES), lambda j, c: (j, 0, 0))
    cmat = pl.BlockSpec((None, STATE_LANES, LANES), lambda j, c: (j, 0, 0))
    dvec = pl.BlockSpec((1, LANES), lambda j, c: (0, j))
    return tok, par, bmat, cmat, dvec


def _ssm_fwd(u, lr_e, li_e, ldt_e, bre_e, bim_e, cre_e, cim_e, d_skip):
    S, SW = u.shape
    T = min(SSM_CHUNK, S)
    nch, nbk = S // T, SW // LANES
    tok, par, bmat, cmat, dvec = _ssm_specs(T, nch, False)
    state_spec = pl.BlockSpec((T, STATE_LANES), lambda j, c: (c, j))
    carry_spec = pl.BlockSpec((None, 1, STATE_LANES), lambda j, c: (c, 0, j))

    def body(u_ref, lr_ref, li_ref, ldt_ref, bre_ref, bim_ref, cre_ref, cim_ref, d_ref,
             y_ref, sr_ref, si_ref, er_ref, ei_ref, bbr, bbi, a_scr, pw, carry):
        c = pl.program_id(1)

        @pl.when(c == 0)
        def _():
            lr, li, ldt = lr_ref[...], li_ref[...], ldt_ref[...]
            ar, ai, kr, ki = _ssm_disc(lr, li, ldt)
            a_scr[0], a_scr[1] = ar, ai
            bbr[...] = (kr * bre_ref[...] - ki * bim_ref[...]).astype(BF16)
            bbi[...] = (kr * bim_ref[...] + ki * bre_ref[...]).astype(BF16)
            pw[0], pw[1] = _ssm_tile_powers(lr, li, ldt, False)
            carry[...] = jnp.zeros_like(carry)

        u_ = u_ref[...]
        ub = u_.astype(BF16)
        sr, si, cr, ci = _scan(jnp.dot(ub, bbr[...], preferred_element_type=F32),
                               jnp.dot(ub, bbi[...], preferred_element_type=F32),
                               a_scr[0], a_scr[1], pw[0], pw[1], carry[0], carry[1], False)
        carry[0], carry[1] = cr, ci
        er_ref[...], ei_ref[...] = cr, ci
        sr_ref[...], si_ref[...] = sr, si
        y0 = (jnp.dot(sr.astype(BF16), cre_ref[...].astype(BF16), preferred_element_type=F32)
              - jnp.dot(si.astype(BF16), cim_ref[...].astype(BF16), preferred_element_type=F32))
        y_ref[...] = y0 + d_ref[...] * u_

    states = jax.ShapeDtypeStruct((S, nbk * STATE_LANES), F32)
    ends = jax.ShapeDtypeStruct((nch, 1, nbk * STATE_LANES), F32)
    return pl.pallas_call(
        body, name="ssm_fwd", out_shape=(jax.ShapeDtypeStruct((S, SW), F32), states, states, ends, ends),
        grid=(nbk, nch), in_specs=[tok, par, par, par, bmat, bmat, cmat, cmat, dvec],
        out_specs=(tok, state_spec, state_spec, carry_spec, carry_spec),
        scratch_shapes=[pltpu.VMEM((LANES, STATE_LANES), BF16), pltpu.VMEM((LANES, STATE_LANES), BF16),
                        pltpu.VMEM((2, 1, STATE_LANES), F32), pltpu.VMEM((2, TILE, STATE_LANES), F32),
                        pltpu.VMEM((2, 1, STATE_LANES), F32)],
        compiler_params=_cparams("arbitrary", "arbitrary"),
    )(u, lr_e, li_e, ldt_e, bre_e, bim_e, cre_e, cim_e, d_skip)


def _ssm_bwd(u, y1, dy2a, dy2b, st_r, st_i, ends_r, ends_i, lr_e, li_e, ldt_e, bre_e, bim_e, cre_e, cim_e, d_skip):
    S, SW = u.shape
    T = min(SSM_CHUNK, S)
    nch, nbk = S // T, SW // LANES
    tok, par, bmat, cmat, dvec = _ssm_specs(T, nch, True)
    state_spec = pl.BlockSpec((T, STATE_LANES), lambda j, c: (nch - 1 - c, j))
    prev_spec = pl.BlockSpec((None, 1, STATE_LANES), lambda j, c: (jnp.maximum(nch - 2 - c, 0), 0, j))
    acc8 = pl.BlockSpec((None, 8, STATE_LANES), lambda j, c: (j, 0, 0))
    dd8 = pl.BlockSpec((None, 8, LANES), lambda j, c: (j, 0, 0))

    def body(u_ref, y_ref, da_ref, db_ref, sr_ref, si_ref, pr_ref, pi_ref, lr_ref, li_ref, ldt_ref,
             bre_ref, bim_ref, cre_ref, cim_ref, d_ref,
             du_ref, dar_ref, dai_ref, dcr_ref, dci_ref, dbr_ref, dbi_ref, ddk_ref,
             bbr, bbi, a_scr, pw, carry):
        c = pl.program_id(1)

        @pl.when(c == 0)
        def _():
            lr, li, ldt = lr_ref[...], li_ref[...], ldt_ref[...]
            ar, ai, kr, ki = _ssm_disc(lr, li, ldt)
            a_scr[0], a_scr[1] = ar, -ai
            bbr[...] = (kr * bre_ref[...] - ki * bim_ref[...]).astype(BF16)
            bbi[...] = (kr * bim_ref[...] + ki * bre_ref[...]).astype(BF16)
            pw[0], pw[1] = _ssm_tile_powers(lr, li, ldt, True)
            carry[...] = jnp.zeros_like(carry)
            for ref in (dar_ref, dai_ref, dcr_ref, dci_ref, dbr_ref, dbi_ref, ddk_ref):
                ref[...] = jnp.zeros_like(ref)

        u_ = u_ref[...]
        ub = u_.astype(BF16)
        dy1 = (da_ref[...] + db_ref[...]) * _gelu_grad(y_ref[...])
        dyb = dy1.astype(BF16)

        sr, si = sr_ref[...], si_ref[...]
        has_prev = c < nch - 1
        s0r = jnp.where(has_prev, pr_ref[...], 0.0)
        s0i = jnp.where(has_prev, pi_ref[...], 0.0)

        cre_b, cim_b = cre_ref[...].astype(BF16), cim_ref[...].astype(BF16)
        gr, gi, cr, ci = _scan(lax.dot_general(dyb, cre_b, _NT, preferred_element_type=F32),
                               -lax.dot_general(dyb, cim_b, _NT, preferred_element_type=F32),
                               a_scr[0], a_scr[1], pw[0], pw[1], carry[0], carry[1], True)
        carry[0], carry[1] = cr, ci

        row = lax.broadcasted_iota(jnp.int32, (T, STATE_LANES), 0)
        spr = jnp.where(row == 0, s0r, pltpu.roll(sr, 1, 0))
        spi = jnp.where(row == 0, s0i, pltpu.roll(si, 1, 0))

        def fold(a):
            return jnp.sum(a.reshape(T // 8, 8, a.shape[-1]), axis=0)

        dar_ref[...] += fold(gr * spr + gi * spi)
        dai_ref[...] += fold(gi * spr - gr * spi)
        srb, sib, grb, gib = sr.astype(BF16), si.astype(BF16), gr.astype(BF16), gi.astype(BF16)
        dcr_ref[...] += lax.dot_general(srb, dyb, _TN, preferred_element_type=F32)
        dci_ref[...] -= lax.dot_general(sib, dyb, _TN, preferred_element_type=F32)
        dbr_ref[...] += lax.dot_general(ub, grb, _TN, preferred_element_type=F32)
        dbi_ref[...] += lax.dot_general(ub, gib, _TN, preferred_element_type=F32)
        du_ref[...] = (lax.dot_general(grb, bbr[...], _NT, preferred_element_type=F32)
                       + lax.dot_general(gib, bbi[...], _NT, preferred_element_type=F32)
                       + dy1 * d_ref[...])
        ddk_ref[...] += fold(dy1 * u_)

    return pl.pallas_call(
        body, name="ssm_bwd",
        out_shape=(jax.ShapeDtypeStruct((S, SW), F32),
                   jax.ShapeDtypeStruct((nbk, 8, STATE_LANES), F32), jax.ShapeDtypeStruct((nbk, 8, STATE_LANES), F32),
                   jax.ShapeDtypeStruct((nbk, STATE_LANES, LANES), F32), jax.ShapeDtypeStruct((nbk, STATE_LANES, LANES), F32),
                   jax.ShapeDtypeStruct((nbk, LANES, STATE_LANES), F32), jax.ShapeDtypeStruct((nbk, LANES, STATE_LANES), F32),
                   jax.ShapeDtypeStruct((nbk, 8, LANES), F32)),
        grid=(nbk, nch),
        in_specs=[tok, tok, tok, tok, state_spec, state_spec, prev_spec, prev_spec, par, par, par,
                  bmat, bmat, cmat, cmat, dvec],
        out_specs=(tok, acc8, acc8, cmat, cmat, bmat, bmat, dd8),
        scratch_shapes=[pltpu.VMEM((LANES, STATE_LANES), BF16), pltpu.VMEM((LANES, STATE_LANES), BF16),
                        pltpu.VMEM((2, 1, STATE_LANES), F32), pltpu.VMEM((2, TILE, STATE_LANES), F32),
                        pltpu.VMEM((2, 1, STATE_LANES), F32)],
        compiler_params=_cparams("arbitrary", "arbitrary"),
    )(u, y1, dy2a, dy2b, st_r, st_i, ends_r, ends_i, lr_e, li_e, ldt_e, bre_e, bim_e, cre_e, cim_e, d_skip)


def _ssm_param_bwd(dar8, dai8, dbr_e, dbi_e, lr_e, li_e, ldt_e, bre_e, bim_e):
    nbk = lr_e.shape[0]
    par = pl.BlockSpec((None, 1, STATE_LANES), lambda j: (j, 0, 0))
    acc8 = pl.BlockSpec((None, 8, STATE_LANES), lambda j: (j, 0, 0))
    bmat = pl.BlockSpec((None, LANES, STATE_LANES), lambda j: (j, 0, 0))

    def body(dar_ref, dai_ref, dbr_ref, dbi_ref, lr_ref, li_ref, ldt_ref, bre_ref, bim_ref,
             dlr_ref, dli_ref, dldt_ref, dbre_ref, dbim_ref):
        lr, li, ldt = lr_ref[...], li_ref[...], ldt_ref[...]
        (ar, ai, kr, ki), vjp = jax.vjp(_ssm_disc, lr, li, ldt)
        dbr, dbi, bre, bim = dbr_ref[...], dbi_ref[...], bre_ref[...], bim_ref[...]
        dbre_ref[...] = kr * dbr + ki * dbi
        dbim_ref[...] = kr * dbi - ki * dbr
        dkr = _colsum(dbr * bre + dbi * bim)
        dki = _colsum(dbi * bre - dbr * bim)
        dlr, dli, dldt = vjp((_colsum(dar_ref[...]), _colsum(dai_ref[...]), dkr, dki))
        dlr_ref[...] = dlr
        dli_ref[...] = dli
        tot = jnp.broadcast_to(dldt, (8, STATE_LANES))
        sh = 1
        while sh < SSM_P:
            tot = tot + pltpu.roll(tot, STATE_LANES - sh, 1)
            sh *= 2
        dldt_ref[...] = tot[:1]

    vec = jax.ShapeDtypeStruct((nbk, 1, STATE_LANES), F32)
    mat = jax.ShapeDtypeStruct((nbk, LANES, STATE_LANES), F32)
    return pl.pallas_call(
        body, name="ssm_param_bwd", out_shape=(vec, vec, vec, mat, mat), grid=(nbk,),
        in_specs=[acc8, acc8, bmat, bmat, par, par, par, bmat, bmat],
        out_specs=(par, par, par, bmat, bmat), compiler_params=_cparams("parallel"),
    )(dar8, dai8, dbr_e, dbi_e, lr_e, li_e, ldt_e, bre_e, bim_e)


def _expand_b(b):
    G = b.shape[0]
    bt = b.transpose(0, 2, 1).reshape(G // GROUPS_PER_BLOCK, GROUPS_PER_BLOCK, SSM_C, SSM_P)
    eye = jnp.eye(GROUPS_PER_BLOCK, dtype=b.dtype)
    return (bt[:, :, :, None, :] * eye[None, :, None, :, None]).reshape(G // GROUPS_PER_BLOCK, LANES, STATE_LANES)


def _collapse_b(be):
    nbk = be.shape[0]
    eye = jnp.eye(GROUPS_PER_BLOCK, dtype=be.dtype)
    d5 = be.reshape(nbk, GROUPS_PER_BLOCK, SSM_C, GROUPS_PER_BLOCK, SSM_P)
    d4 = (d5 * eye[None, :, None, :, None]).sum(axis=3)
    return d4.transpose(0, 1, 3, 2).reshape(nbk * GROUPS_PER_BLOCK, SSM_P, SSM_C)


def _expand_c(cm):
    G = cm.shape[0]
    ct = cm.transpose(0, 2, 1).reshape(G // GROUPS_PER_BLOCK, GROUPS_PER_BLOCK, SSM_P, SSM_C)
    eye = jnp.eye(GROUPS_PER_BLOCK, dtype=cm.dtype)
    return (ct[:, :, :, None, :] * eye[None, :, None, :, None]).reshape(G // GROUPS_PER_BLOCK, STATE_LANES, LANES)


def _collapse_c(ce):
    nbk = ce.shape[0]
    eye = jnp.eye(GROUPS_PER_BLOCK, dtype=ce.dtype)
    d5 = ce.reshape(nbk, GROUPS_PER_BLOCK, SSM_P, GROUPS_PER_BLOCK, SSM_C)
    d4 = (d5 * eye[None, :, None, :, None]).sum(axis=3)
    return d4.transpose(0, 1, 3, 2).reshape(nbk * GROUPS_PER_BLOCK, SSM_C, SSM_P)


def _place():
    x, y, c = lax.axis_index("x"), lax.axis_index("y"), lax.axis_index("c")
    return x, y, c


def _other_chips(x, y):
    return [(1 - x, y), (x, 1 - y), (1 - x, 1 - y)]


_ANY = pl.BlockSpec(memory_space=pl.ANY)


_HBM = pl.BlockSpec(memory_space=pltpu.HBM)
_SEM = pl.BlockSpec(memory_space=pltpu.SEMAPHORE)
_EFFECT = pltpu.SideEffectType.DATAFLOW_SIDE_EFFECTING
_TOKEN = jax.ShapeDtypeStruct((8, LANES), F32)


def _hbm(a):
    return pltpu.with_memory_space_constraint(a, pltpu.HBM)


def _place_own(src, *, gather, name, tr=512):
    R, C = src.shape[-2:]
    tr = min(tr, R)
    x, y, _ = _place()
    me = (2 * x + y).astype(jnp.int32).reshape(1)

    def body(me_ref, s_ref, o_ref):
        o_ref[...] = s_ref[...].astype(BF16)

    own = pl.BlockSpec((None, tr, C), lambda i, me_ref: (me_ref[0], i, 0))
    grid_spec = pltpu.PrefetchScalarGridSpec(
        num_scalar_prefetch=1, grid=(R // tr,),
        in_specs=[pl.BlockSpec((tr, C), lambda i, me_ref: (i, 0)) if gather else own], out_specs=own)
    return pl.pallas_call(
        body, name=name, grid_spec=grid_spec, out_shape=jax.ShapeDtypeStruct((N_CHIPS, R, C), BF16),
        compiler_params=_cparams("parallel"))(me, src)


def _exchange_copy(src_slot, land_slot, send, recv, k, j, peer, c):
    return pltpu.make_async_remote_copy(
        src_ref=src_slot, dst_ref=land_slot, send_sem=send.at[3 * k + j], recv_sem=recv.at[3 * k + j],
        device_id=(peer[0], peer[1], c), device_id_type=MESH)


def _exchange_start(lands, srcs, groups, *, name):
    n, ng = len(lands), len(groups)
    bufs = list(lands) + list(srcs)
    nb = len(bufs)

    def body(*refs):
        lnd, src, sems = refs[:n], refs[n:nb], refs[nb:nb + 2 * ng]
        token = refs[2 * nb + 2 * ng]
        x, y, c = _place()
        me = 2 * x + y
        for gi, group in enumerate(groups):
            for k, w in enumerate(group):
                for j, peer in enumerate(_other_chips(x, y)):
                    sent = src[w].at[2 * peer[0] + peer[1]] if src else lnd[w].at[me]
                    _exchange_copy(sent, lnd[w].at[me], sems[2 * gi], sems[2 * gi + 1], k, j, peer, c).start()
        token[...] = jnp.zeros_like(token)

    sem_shapes = [pltpu.SemaphoreType.DMA((3 * len(g),)) for g in groups for _ in range(2)]
    res = pl.pallas_call(
        body, name=name,
        out_shape=sem_shapes + [pltpu.HBM(a.shape, a.dtype) for a in bufs] + [_TOKEN],
        in_specs=[_HBM] * nb,
        out_specs=[_SEM] * (2 * ng) + [_HBM] * nb + [pl.BlockSpec(memory_space=pltpu.VMEM)],
        input_output_aliases={i: 2 * ng + i for i in range(nb)},
        compiler_params=pltpu.CompilerParams(has_side_effects=_EFFECT),
    )(*[_hbm(a) for a in bufs])
    sems = [(res[2 * gi], res[2 * gi + 1]) for gi in range(ng)]
    return sems, res[2 * ng:2 * ng + n], res[2 * ng + n:2 * ng + nb], res[-1]


def _exchange_wait(lands, srcs, sems, after, *, name):
    n = len(lands)
    bufs = list(lands) + list(srcs)
    nb = len(bufs)
    send_sems, recv_sems = sems

    def body(*refs):
        lnd, src, send, recv = refs[:n], refs[n:nb], refs[nb], refs[nb + 1]
        x, y, c = _place()
        for k in range(n):
            for j, peer in enumerate(_other_chips(x, y)):
                slot = 2 * peer[0] + peer[1]
                copy = _exchange_copy((src[k] if src else lnd[k]).at[slot], lnd[k].at[slot], send, recv, k, j, peer, c)
                copy.wait_send()
                copy.wait_recv()

    res = pl.pallas_call(
        body, name=name, out_shape=[pltpu.HBM(a.shape, a.dtype) for a in bufs],
        in_specs=[_HBM] * nb + [_SEM, _SEM, _ANY], out_specs=[_HBM] * nb,
        input_output_aliases={i: i for i in range(nb)},
        compiler_params=pltpu.CompilerParams(has_side_effects=_EFFECT),
    )(*bufs, send_sems, recv_sems, after)
    return res[:n]


def _sum_partials(land, *, name, tr=256):
    _, R, C = land.shape
    tr = min(tr, R)

    def body(l_ref, o_ref):
        acc = l_ref[0].astype(F32)
        for k in range(1, N_CHIPS):
            acc = acc + l_ref[k].astype(F32)
        o_ref[...] = acc

    return pl.pallas_call(
        body, name=name, out_shape=jax.ShapeDtypeStruct((R, C), F32), grid=(R // tr,),
        in_specs=[pl.BlockSpec((N_CHIPS, tr, C), lambda i: (0, i, 0))], out_specs=_rows(tr, C),
        compiler_params=_cparams("parallel"))(land)


def _swap_with_sibling(sums, *, name):
    n = len(sums)

    def body(*refs):
        ins, outs = refs[:n], refs[n:2 * n]
        send_sems, recv_sems = refs[2 * n:]
        x, y, c = _place()
        copies = [pltpu.make_async_remote_copy(
            src_ref=ins[w], dst_ref=outs[w], send_sem=send_sems.at[w], recv_sem=recv_sems.at[w],
            device_id=(x, y, 1 - c), device_id_type=MESH) for w in range(n)]
        for cp in copies:
            cp.start()
        for cp in copies:
            cp.wait_recv()
            cp.wait_send()

    return pl.pallas_call(
        body, name=name,
        out_shape=[jax.ShapeDtypeStruct(s.shape, s.dtype) for s in sums],
        in_specs=[_ANY] * n, out_specs=[_ANY] * n,
        scratch_shapes=[pltpu.SemaphoreType.DMA((n,)), pltpu.SemaphoreType.DMA((n,))],
    )(*sums)


def _adamw_math(w, g, m, v):
    m = ADAM_B1 * m + (1.0 - ADAM_B1) * g
    v = ADAM_B2 * v + (1.0 - ADAM_B2) * (g * g)
    m_hat = m / (1.0 - ADAM_B1 ** ADAM_STEP)
    v_hat = v / (1.0 - ADAM_B2 ** ADAM_STEP)
    delta = -ADAM_LR * (m_hat / (jnp.sqrt(v_hat) + ADAM_EPS) + ADAM_WD * w)
    return delta, m, v


def _adamw_pair(mine, theirs, w, m, v, *, name, tr=128):
    R, C = w.shape
    tr = min(tr, R)

    def body(a_ref, b_ref, w_ref, m_ref, v_ref, g_ref, d_ref, nm_ref, nv_ref):
        g = a_ref[...] + b_ref[...]
        g_ref[...] = g
        d_ref[...], nm_ref[...], nv_ref[...] = _adamw_math(w_ref[...], g, m_ref[...], v_ref[...])

    shape = jax.ShapeDtypeStruct((R, C), F32)
    return pl.pallas_call(
        body, name=name, out_shape=(shape,) * 4, grid=(R // tr,),
        in_specs=[_rows(tr, C)] * 5, out_specs=(_rows(tr, C),) * 4,
        compiler_params=_cparams("parallel"))(mine, theirs, w, m, v)


def _all_gather_small(packed):
    R = packed.shape[0]

    def body(x_ref, out_ref, send_sems, recv_sems, local_sem):
        x, y, c = _place()

        def slot(px, py, pc):
            return out_ref.at[4 * px + 2 * py + pc]

        local = pltpu.make_async_copy(x_ref, slot(x, y, c), local_sem)
        local.start()
        peers = [(x ^ (k >> 2), y ^ ((k >> 1) & 1), c ^ (k & 1)) for k in range(1, N_DEV)]
        for k, peer in enumerate(peers):
            pltpu.make_async_remote_copy(
                src_ref=x_ref, dst_ref=slot(x, y, c), send_sem=send_sems.at[k], recv_sem=recv_sems.at[k],
                device_id=peer, device_id_type=MESH).start()
        for k, peer in enumerate(peers):
            arrival = pltpu.make_async_remote_copy(
                src_ref=x_ref, dst_ref=slot(*peer), send_sem=send_sems.at[k], recv_sem=recv_sems.at[k],
                device_id=peer, device_id_type=MESH)
            arrival.wait_recv()
            arrival.wait_send()
        local.wait()

    vm = pl.BlockSpec(memory_space=pltpu.VMEM)
    return pl.pallas_call(
        body, name="all_gather_small", out_shape=jax.ShapeDtypeStruct((N_DEV, R, LANES), F32),
        in_specs=[vm], out_specs=vm,
        scratch_shapes=[pltpu.SemaphoreType.DMA((N_DEV - 1,)), pltpu.SemaphoreType.DMA((N_DEV - 1,)),
                        pltpu.SemaphoreType.DMA],
        compiler_params=pltpu.CompilerParams(vmem_limit_bytes=VMEM_LIMIT_BYTES),
    )(packed)


def _adamw_small(gathered, w, m, v):
    _, R, _ = gathered.shape
    tr = PACK_ROWS

    def body(gs_ref, w_ref, m_ref, v_ref, g_ref, d_ref, nm_ref, nv_ref):
        g = gs_ref[0]
        for k in range(1, N_DEV):
            g = g + gs_ref[k]
        g_ref[...] = g
        d_ref[...], nm_ref[...], nv_ref[...] = _adamw_math(w_ref[...], g, m_ref[...], v_ref[...])

    shape = jax.ShapeDtypeStruct((R, LANES), F32)
    return pl.pallas_call(
        body, name="adamw_small", out_shape=(shape,) * 4, grid=(R // tr,),
        in_specs=[pl.BlockSpec((N_DEV, tr, LANES), lambda i: (0, i, 0))] + [_rows(tr, LANES)] * 3,
        out_specs=(_rows(tr, LANES),) * 4, compiler_params=_cparams("parallel"))(gathered, w, m, v)


def _pack(arrays):
    parts, layout = [], []
    for a in arrays:
        n = a.size
        rows = -(-n // (8 * LANES)) * 8
        flat = jnp.pad(a.reshape(-1).astype(F32), (0, rows * LANES - n))
        parts.append(flat.reshape(rows, LANES))
        layout.append((rows, n, a.shape))
    total = sum(r for r, _, _ in layout)
    parts.append(jnp.zeros((-total % PACK_ROWS, LANES), F32))
    return jnp.concatenate(parts, axis=0), layout


def _unpack(buf, layout):
    out, r0 = [], 0
    for rows, n, shape in layout:
        out.append(buf[r0:r0 + rows].reshape(-1)[:n].reshape(shape))
        r0 += rows
    return out


SMALL = ("mix_norm_pre", "lam_re", "lam_im", "log_dt", "ssm_b_re", "ssm_b_im", "ssm_c_re", "ssm_c_im",
         "ssm_d", "b_glu", "attn_out_norm", "ssm_out_norm", "mix_norm_post", "mlp_norm_pre",
         "mlp_norm_post", "ple_norm_pre", "ple_norm_post")
BIG = ("w_in", "w_glu", "w_out", "w_up", "w_down", "w_ple_gate", "w_ple_proj")
WEIGHTS = ("mix_norm_pre", "w_in", "lam_re", "lam_im", "log_dt", "ssm_b_re", "ssm_b_im", "ssm_c_re",
           "ssm_c_im", "ssm_d", "w_glu", "b_glu", "attn_out_norm", "ssm_out_norm", "w_out",
           "mix_norm_post", "mlp_norm_pre", "w_up", "w_down", "mlp_norm_post", "ple_norm_pre",
           "w_ple_gate", "w_ple_proj", "ple_norm_post")


def kernel(x, p, mix_norm_pre, w_in, lam_re, lam_im, log_dt, ssm_b_re, ssm_b_im, ssm_c_re, ssm_c_im, ssm_d, w_glu, b_glu, attn_out_norm, ssm_out_norm, w_out, mix_norm_post, mlp_norm_pre, w_up, w_down, mlp_norm_post, ple_norm_pre, w_ple_gate, w_ple_proj, ple_norm_post, loss_target, m_mix_norm_pre, m_w_in, m_lam_re, m_lam_im, m_log_dt, m_ssm_b_re, m_ssm_b_im, m_ssm_c_re, m_ssm_c_im, m_ssm_d, m_w_glu, m_b_glu, m_attn_out_norm, m_ssm_out_norm, m_w_out, m_mix_norm_post, m_mlp_norm_pre, m_w_up, m_w_down, m_mlp_norm_post, m_ple_norm_pre, m_w_ple_gate, m_w_ple_proj, m_ple_norm_post, v_mix_norm_pre, v_w_in, v_lam_re, v_lam_im, v_log_dt, v_ssm_b_re, v_ssm_b_im, v_ssm_c_re, v_ssm_c_im, v_ssm_d, v_w_glu, v_b_glu, v_attn_out_norm, v_ssm_out_norm, v_w_out, v_mix_norm_post, v_mlp_norm_pre, v_w_up, v_w_down, v_mlp_norm_post, v_ple_norm_pre, v_w_ple_gate, v_w_ple_proj, v_ple_norm_post):
    args = dict(locals())
    W = {n: args[n][0] for n in WEIGHTS}
    Mo = {n: args["m_" + n][0] for n in WEIGHTS}
    Vo = {n: args["v_" + n][0] for n in WEIGHTS}
    xs, ps, tgt = x[0], p[0, 0], loss_target[0]
    S, D = xs.shape
    SW = W["ssm_d"].shape[0]
    AW = W["attn_out_norm"].shape[0]
    heads = AW // HEAD_DIM
    G = SW // SSM_C
    nbk = SW // LANES
    assert W["w_in"].shape[1] * N_CHIPS == 3 * AW + SW and AW == SW

    row = lambda a: a.reshape(1, -1)

    ag_groups = (("w_in",), ("w_glu", "w_out"), ("w_up",), ("w_down", "w_ple_gate", "w_ple_proj"))
    ag_names = [n for g in ag_groups for n in g]
    ag_sems, ag_land, _, ag_token = _exchange_start(
        [_place_own(W[n], gather=True, name="ag_place_" + n) for n in ag_names], [],
        [[ag_names.index(n) for n in g] for g in ag_groups], name="ag_start")

    def gathered(gi, after):
        got = _exchange_wait([ag_land[ag_names.index(n)] for n in ag_groups[gi]], [], ag_sems[gi], after,
                             name=f"ag_wait_{gi}")
        return dict(zip(ag_groups[gi], got))

    lr_e = W["lam_re"].reshape(nbk, 1, STATE_LANES)
    li_e = W["lam_im"].reshape(nbk, 1, STATE_LANES)
    ldt_e = jnp.repeat(W["log_dt"], SSM_P).reshape(nbk, 1, STATE_LANES)
    bre_e, bim_e = _expand_b(W["ssm_b_re"]), _expand_b(W["ssm_b_im"])
    cre_e, cim_e = _expand_c(W["ssm_c_re"]), _expand_c(W["ssm_c_im"])
    d_row = row(W["ssm_d"])

    hn1 = _norm_cast(xs, row(W["mix_norm_pre"]) + ag_token[0, 0], name="norm_in")
    w_in_f = gathered(0, hn1)["w_in"]
    qkv_b = _proj_qkv(hn1, w_in_f)
    u = _matmul(hn1, w_in_f, name="proj_u", b_shards=N_CHIPS, b_cols=(3 * AW, SW))
    outs, lses = zip(*[_attn_fwd(qb, d, heads) for d, qb in zip(DILATIONS, qkv_b)])
    y1, st_r, st_i, ends_r, ends_i = _ssm_fwd(u, lr_e, li_e, ldt_e, bre_e, bim_e, cre_e, cim_e, d_row)
    y2b = _gelu_cast(y1)
    full = gathered(1, y2b)
    w_glu_f = full["w_glu"].reshape(SW, SW)
    w_out_f = full["w_out"].reshape(AW + SW, D)
    z = _matmul(y2b, w_glu_f, name="glu_z")
    attn, lse_b, mixed = _mix_fwd(outs, lses, y1, z, row(W["b_glu"]), row(W["attn_out_norm"]), row(W["ssm_out_norm"]))
    mo = _matmul(mixed, w_out_f, name="mix_out")
    h1, hn2 = _res_norm(xs, mo, row(W["mix_norm_post"]), row(W["mlp_norm_pre"]), name="res_mix")
    w_up_f = gathered(2, hn2)["w_up"]
    up, act = _matmul(hn2, w_up_f, name="mlp_up", b_shards=N_CHIPS, relu2=True)
    full = gathered(3, act)
    w_down_f = full["w_down"].reshape(-1, D)
    w_pg_f = full["w_ple_gate"].reshape(D, D)
    w_pp_f = full["w_ple_proj"]
    ff = _matmul(act, w_down_f, name="mlp_down")
    h2, hn3 = _res_norm(h1, ff, row(W["mlp_norm_post"]), row(W["ple_norm_pre"]), name="res_mlp")
    gl = _matmul(hn3, w_pg_f, name="ple_gate")
    e = _matmul(ps.astype(BF16), w_pp_f, name="ple_proj", b_shards=N_CHIPS)

    dh3, dgl, de, loss_part, dg_ple_post = _final(h2, gl, e, row(W["ple_norm_post"]), tgt)
    gW = {}
    out_g, out_d, out_m, out_v = {}, {}, {}, {}

    def scatter_start(names, tag):
        parts = [gW[n] if gW[n].ndim == 3 else gW[n].reshape((N_CHIPS, -1, gW[n].shape[1])) for n in names]
        sems, land, src, token = _exchange_start(
            [_place_own(part, gather=False, name="rs_place_" + n) for n, part in zip(names, parts)], parts,
            [list(range(len(names)))], name=f"rs_start_{tag}")
        return (names, sems[0], land, src), token

    def scatter_finish(batch, after, tag):
        names, sems, land, src = batch
        landed = _exchange_wait(land, src, sems, after, name=f"rs_wait_{tag}")
        sums = [_sum_partials(l, name="sum_" + n) for n, l in zip(names, landed)]
        theirs = _swap_with_sibling(sums, name=f"swap_{tag}")
        for n, a, b in zip(names, sums, theirs):
            out_g[n], out_d[n], out_m[n], out_v[n] = _adamw_pair(a, b, W[n], Mo[n], Vo[n], name="adamw_" + n)

    gW["w_ple_proj"] = _matmul(ps.astype(BF16), de, name="d_w_ple_proj", ta=True, out_dtype=BF16, out_shards=N_CHIPS)
    gW["w_ple_gate"] = _matmul(hn3, dgl, name="d_w_ple_gate", ta=True, out_dtype=BF16)
    dhn3 = _matmul(dgl, w_pg_f, name="d_hn3", tb=True)
    dh2, dff, dg_ple_pre, dg_mlp_post = _bwd_res_norm(
        dh3, dhn3, h2, row(W["ple_norm_pre"]), ff, row(W["mlp_norm_post"]), name="bwd_res_mlp")
    gW["w_down"] = _matmul(act, dff, name="d_w_down", ta=True, out_dtype=BF16)
    batch1, token1 = scatter_start(("w_ple_proj", "w_ple_gate", "w_down"), 1)
    dup = _matmul(dff, w_down_f, name="d_up", tb=True, after=token1, relu2_of=up, out_dtype=BF16)
    gW["w_up"] = _matmul(hn2, dup, name="d_w_up", ta=True, out_dtype=BF16, out_shards=N_CHIPS)
    dhn2 = _matmul(dup, w_up_f, name="d_hn2", tb=True, b_shards=N_CHIPS)
    dh1, dmo, dg_mlp_pre, dg_mix_post = _bwd_res_norm(
        dh2, dhn2, h1, row(W["mlp_norm_pre"]), mo, row(W["mix_norm_post"]), name="bwd_res_mix")
    gW["w_out"] = _matmul(mixed, dmo, name="d_w_out", ta=True, out_dtype=BF16)
    dmixed = _matmul(dmo, w_out_f, name="d_mixed", tb=True)
    dattn_b, dd_b, dz, dy2a, dg_attn, dg_ssm, db_glu = _mix_bwd(
        dmixed, attn, y1, z, row(W["b_glu"]), row(W["attn_out_norm"]), row(W["ssm_out_norm"]))
    gW["w_glu"] = _matmul(y2b, dz, name="d_w_glu", ta=True, out_dtype=BF16)
    batch2, token2 = scatter_start(("w_up", "w_out", "w_glu"), 2)
    dy2b = _matmul(dz, w_glu_f, name="d_y2", tb=True, after=token2)
    du, dar8, dai8, dcr_e, dci_e, dbr_e, dbi_e, dd8 = _ssm_bwd(
        u, y1, dy2a, dy2b, st_r, st_i, ends_r, ends_i, lr_e, li_e, ldt_e, bre_e, bim_e, cre_e, cim_e, d_row)
    scatter_finish(batch1, du, 1)
    dlr_e, dli_e, dldt_e, dbre_e, dbim_e = _ssm_param_bwd(dar8, dai8, dbr_e, dbi_e, lr_e, li_e, ldt_e, bre_e, bim_e)

    dqs, dks, dvs = zip(*[_attn_bwd(qb, da, l, dd_, d, heads)
                          for d, qb, da, l, dd_ in zip(DILATIONS, qkv_b, dattn_b, lse_b, dd_b)])
    dproj = _dproj_join(dqs, dks, dvs, du)
    scatter_finish(batch2, dproj, 2)
    gW["w_in"] = _matmul(hn1, dproj, name="d_w_in", ta=True, out_dtype=BF16, out_shards=N_CHIPS)
    batch3, token3 = scatter_start(("w_in",), 3)
    dhn1 = _matmul(dproj, w_in_f, name="d_hn1", tb=True, b_shards=N_CHIPS, after=token3)
    grad_x, dg_mix_pre = _bwd_first(dh1, dhn1, xs, row(W["mix_norm_pre"]))
    scatter_finish(batch3, grad_x, 3)

    small_g = {
        "mix_norm_pre": dg_mix_pre, "lam_re": dlr_e.reshape(G, SSM_P), "lam_im": dli_e.reshape(G, SSM_P),
        "log_dt": dldt_e.reshape(G, SSM_P)[:, 0], "ssm_b_re": _collapse_b(dbre_e), "ssm_b_im": _collapse_b(dbim_e),
        "ssm_c_re": _collapse_c(dcr_e), "ssm_c_im": _collapse_c(dci_e), "ssm_d": dd8.sum(axis=1).reshape(-1),
        "b_glu": db_glu, "attn_out_norm": dg_attn, "ssm_out_norm": dg_ssm, "mix_norm_post": dg_mix_post,
        "mlp_norm_pre": dg_mlp_pre, "mlp_norm_post": dg_mlp_post, "ple_norm_pre": dg_ple_pre,
        "ple_norm_post": dg_ple_post,
    }
    g_pack, layout = _pack([small_g[n].reshape(W[n].shape) for n in SMALL])
    w_pack, _ = _pack([W[n] for n in SMALL])
    m_pack, _ = _pack([Mo[n] for n in SMALL])
    v_pack, _ = _pack([Vo[n] for n in SMALL])
    packed = _adamw_small(_all_gather_small(g_pack), w_pack, m_pack, v_pack)
    for dst, buf in zip((out_g, out_d, out_m, out_v), packed):
        dst.update(zip(SMALL, _unpack(buf, layout)))

    loss = lax.psum(loss_part[0, 0], ("x", "y", "c"))
    lead = lambda a: a[None]
    return (loss, grad_x[None],
            *[lead(out_g[n]) for n in WEIGHTS], *[lead(out_d[n]) for n in WEIGHTS],
            *[lead(out_m[n]) for n in WEIGHTS], *[lead(out_v[n]) for n in WEIGHTS])
```

```python
import functools
import math

import jax
import jax.numpy as jnp
from jax import lax
from jax.experimental import pallas as pl
from jax.experimental.pallas import tpu as pltpu

F32 = jnp.float32
BF16 = jnp.bfloat16
MESH = pl.DeviceIdType.MESH

RMS_EPS = 1e-6
NEG_INF = -1e30
HEAD_DIM = 128
BLK = 128
DILATIONS = (1, 4, 16)
SSM_C = 16
SSM_P = 64
LANES = 128
GROUPS_PER_BLOCK = LANES // SSM_C
STATE_LANES = GROUPS_PER_BLOCK * SSM_P
SSM_CHUNK = 128
TILE = 8
ADAM_LR, ADAM_B1, ADAM_B2, ADAM_EPS, ADAM_WD, ADAM_STEP = 1e-3, 0.9, 0.999, 1e-8, 0.01, 10
VMEM_LIMIT_BYTES = 56 * 1024 * 1024
N_CHIPS = 4
N_DEV = 8
PACK_ROWS = 256


def _cparams(*sem):
    return pltpu.CompilerParams(dimension_semantics=sem or None, vmem_limit_bytes=VMEM_LIMIT_BYTES)


def _rows(tr, w):
    return pl.BlockSpec((tr, w), lambda i: (i, 0))


def _vec(w):
    return pl.BlockSpec((1, w), lambda i: (0, 0))


def _sigmoid(x):
    return 1.0 / (1.0 + jnp.exp(-x))


def _gelu(x):
    c = math.sqrt(2.0 / math.pi)
    return 0.5 * x * (1.0 + jnp.tanh(c * (x + 0.044715 * x * x * x)))


def _gelu_grad(x):
    c = math.sqrt(2.0 / math.pi)
    th = jnp.tanh(c * (x + 0.044715 * x * x * x))
    return 0.5 * (1.0 + th) + 0.5 * x * (1.0 - th * th) * c * (1.0 + 3.0 * 0.044715 * x * x)


def _rms(x, g):
    r = lax.rsqrt(jnp.mean(x * x, axis=-1, keepdims=True) + RMS_EPS)
    return x * r * g


def _rms_bwd(dy, x, g):
    r = lax.rsqrt(jnp.mean(x * x, axis=-1, keepdims=True) + RMS_EPS)
    n = x * r
    dn = dy * g
    dx = r * (dn - n * jnp.mean(dn * n, axis=-1, keepdims=True))
    return dx, dy * n


def _colsum(a):
    return jnp.sum(a, axis=0, keepdims=True)


def _first(i):
    return i == 0


def _matmul(a, b, *, name, ta=False, tb=False, out_dtype=F32, b_shards=1, out_shards=1, b_cols=None,
            after=None, relu2=False, relu2_of=None, tm=1024, tn=1024, tk=2048):
    if ta:
        K, M = a.shape
    else:
        M, K = a.shape
    if b_shards > 1:
        rows, cols = b.shape[1], b.shape[2] * b_shards
    else:
        rows, cols = b.shape
    N, Kb = (rows, cols) if tb else (cols, rows)
    assert K == Kb, (a.shape, b.shape, ta, tb)
    col0 = 0
    if b_cols is not None:
        assert not tb
        col0, N = b_cols
    tm, tn, tk = min(tm, M), min(tn, N), min(tk, K)
    if b_shards > 1:
        shard_cols = cols // b_shards
        if tb:
            tk = min(tk, shard_cols)
        else:
            tn = min(tn, shard_cols)
    if out_shards > 1:
        tn = min(tn, N // out_shards)
    assert M % tm == 0 and N % tn == 0 and K % tk == 0 and col0 % tn == 0
    nk = K // tk
    j0 = col0 // tn

    a_spec = (pl.BlockSpec((tk, tm), lambda i, j, k: (k, i)) if ta
              else pl.BlockSpec((tm, tk), lambda i, j, k: (i, k)))
    if b_shards > 1:
        if tb:
            per = shard_cols // tk
            b_spec = pl.BlockSpec((None, tn, tk), lambda i, j, k: (k // per, j, k % per))
        else:
            per = shard_cols // tn
            b_spec = pl.BlockSpec((None, tk, tn), lambda i, j, k: ((j + j0) // per, k, (j + j0) % per))
    else:
        b_spec = (pl.BlockSpec((tn, tk), lambda i, j, k: (j, k)) if tb
                  else pl.BlockSpec((tk, tn), lambda i, j, k: (k, j + j0)))
    if out_shards > 1:
        per_o = (N // out_shards) // tn
        out_shape = jax.ShapeDtypeStruct((out_shards, M, N // out_shards), out_dtype)
        out_spec = pl.BlockSpec((None, tm, tn), lambda i, j, k: (j // per_o, i, j % per_o))
    else:
        out_shape = jax.ShapeDtypeStruct((M, N), out_dtype)
        out_spec = pl.BlockSpec((tm, tn), lambda i, j, k: (i, j))
    dims = (((0 if ta else 1,), (1 if tb else 0,)), ((), ()))

    extra, extra_specs = [], []
    if relu2_of is not None:
        assert out_shards == 1 and relu2_of.shape == (M, N)
        extra.append(relu2_of)
        extra_specs.append(pl.BlockSpec((tm, tn), lambda i, j, k: (i, j)))
    if after is not None:
        extra.append(after)
        extra_specs.append(pl.BlockSpec(after.shape, lambda i, j, k: (0, 0)))
    n_in = 2 + len(extra)
    if relu2:
        assert out_shards == 1
        out_shape = (out_shape, jax.ShapeDtypeStruct((M, N), BF16))
        out_spec = (out_spec, out_spec)

    def finish(acc, refs):
        o_ref = refs[n_in]
        if relu2_of is not None:
            acc = acc * (2.0 * jnp.maximum(refs[2][...], 0.0))
        o_ref[...] = acc.astype(o_ref.dtype)
        if relu2:
            r = jnp.maximum(acc, 0.0)
            refs[n_in + 1][...] = (r * r).astype(BF16)

    def body(*refs):
        prod = lax.dot_general(refs[0][...], refs[1][...], dims, preferred_element_type=F32)
        if nk == 1:
            finish(prod, refs)
            return
        acc_ref = refs[-1]
        k = pl.program_id(2)

        @pl.when(k == 0)
        def _():
            acc_ref[...] = prod

        @pl.when(k > 0)
        def _():
            acc_ref[...] += prod

        @pl.when(k == nk - 1)
        def _():
            finish(acc_ref[...], refs)

    return pl.pallas_call(
        body, name=name, out_shape=out_shape, grid=(M // tm, N // tn, nk),
        in_specs=[a_spec, b_spec] + extra_specs, out_specs=out_spec,
        scratch_shapes=[pltpu.VMEM((tm, tn), F32)] if nk > 1 else [],
        compiler_params=_cparams("parallel", "parallel", "arbitrary"),
    )(a, b, *extra)


def _norm_cast(x, g, *, name, tr=256):
    S, D = x.shape
    tr = min(tr, S)

    def body(x_ref, g_ref, o_ref):
        o_ref[...] = _rms(x_ref[...], g_ref[...]).astype(BF16)

    return pl.pallas_call(
        body, name=name, out_shape=jax.ShapeDtypeStruct((S, D), BF16), grid=(S // tr,),
        in_specs=[_rows(tr, D), _vec(D)], out_specs=_rows(tr, D),
        compiler_params=_cparams("parallel"))(x, g)


def _res_norm(res, y, g_post, g_next, *, name, tr=256):
    S, D = res.shape
    tr = min(tr, S)

    def body(res_ref, y_ref, gp_ref, gn_ref, h_ref, hn_ref):
        h = res_ref[...] + _rms(y_ref[...], gp_ref[...])
        h_ref[...] = h
        hn_ref[...] = _rms(h, gn_ref[...]).astype(BF16)

    return pl.pallas_call(
        body, name=name,
        out_shape=(jax.ShapeDtypeStruct((S, D), F32), jax.ShapeDtypeStruct((S, D), BF16)),
        grid=(S // tr,), in_specs=[_rows(tr, D), _rows(tr, D), _vec(D), _vec(D)],
        out_specs=(_rows(tr, D), _rows(tr, D)), compiler_params=_cparams("parallel"))(res, y, g_post, g_next)


def _residue_spec(tr, d, w):
    return pl.BlockSpec((tr // d, d * w), lambda i: (i, 0))


def _residue_shape(S, d, w, dtype):
    return jax.ShapeDtypeStruct((S // d, d * w), dtype)


def _residue_scratch(rows, w):
    return pltpu.VMEM((w // LANES, rows, LANES), F32)


def _fill_strips(scr, val):
    for s in range(scr.shape[0]):
        scr[s] = val[:, s * LANES:(s + 1) * LANES]


def _strips_to_residues(scr, o_ref, d):
    strips, rows, _ = scr.shape
    for r in range(d):
        for s in range(strips):
            col = (r * strips + s) * LANES
            o_ref[:, col:col + LANES] = scr[s, pl.ds(r, rows // d, stride=d), :].astype(o_ref.dtype)


def _to_residues(scr, val, o_ref, d):
    if d == 1:
        o_ref[...] = val.astype(o_ref.dtype)
        return
    _fill_strips(scr, val)
    _strips_to_residues(scr, o_ref, d)


def _from_residues(scr, in_ref, d):
    if d == 1:
        return in_ref[...].astype(F32)
    strips, rows, _ = scr.shape
    for r in range(d):
        for s in range(strips):
            col = (r * strips + s) * LANES
            scr[s, pl.ds(r, rows // d, stride=d), :] = in_ref[:, col:col + LANES].astype(F32)
    return jnp.concatenate([scr[s] for s in range(strips)], axis=1)


def _mix_fwd(os, ls, y1, z, b_glu, g_attn, g_ssm, *, tr=128):
    S, SW = y1.shape
    AW = os[0].shape[1]
    tr = min(tr, S)
    nd = len(DILATIONS)

    def body(*refs):
        o_refs, l_refs = refs[:nd], refs[nd:2 * nd]
        y_ref, z_ref, b_ref, ga_ref, gs_ref, attn_ref = refs[2 * nd:2 * nd + 6]
        lse_refs = refs[2 * nd + 6:3 * nd + 6]
        mixed_ref, scr = refs[3 * nd + 6:]
        ls_ = [_from_residues(scr, l_refs[n], d) for n, d in enumerate(DILATIONS)]
        m = functools.reduce(jnp.maximum, ls_)
        es = [jnp.exp(l - m) for l in ls_]
        tot = functools.reduce(jnp.add, es)
        attn = functools.reduce(jnp.add, [e * _from_residues(scr, o_refs[n], d)
                                          for n, (e, d) in enumerate(zip(es, DILATIONS))]) / tot
        attn_ref[...] = attn
        lse = m + jnp.log(tot)
        for n, d in enumerate(DILATIONS):
            _to_residues(scr, lse, lse_refs[n], d)
        ssm = _gelu(y_ref[...]) * _sigmoid(z_ref[...] + b_ref[...])
        mixed_ref[:, :AW] = _rms(attn, ga_ref[...]).astype(BF16)
        mixed_ref[:, AW:] = _rms(ssm, gs_ref[...]).astype(BF16)

    res_in = [_residue_spec(tr, d, AW) for d in DILATIONS]
    res = pl.pallas_call(
        body, name="mix_fwd",
        out_shape=([jax.ShapeDtypeStruct((S, AW), F32)] + [_residue_shape(S, d, AW, F32) for d in DILATIONS]
                   + [jax.ShapeDtypeStruct((S, AW + SW), BF16)]),
        grid=(S // tr,),
        in_specs=res_in + res_in + [_rows(tr, SW), _rows(tr, SW), _vec(SW), _vec(AW), _vec(SW)],
        out_specs=[_rows(tr, AW)] + res_in + [_rows(tr, AW + SW)],
        scratch_shapes=[_residue_scratch(tr, AW)],
        compiler_params=_cparams("parallel"))(*os, *ls, y1, z, b_glu, g_attn, g_ssm)
    return res[0], res[1:1 + nd], res[1 + nd]


def _final(h2, gl, e, g_post, target, *, tr=128):
    S, D = h2.shape
    tr = min(tr, S)

    def body(h_ref, gl_ref, e_ref, g_ref, t_ref, dh_ref, dgl_ref, de_ref, loss_ref, dg_ref):
        i = pl.program_id(0)
        gate = _sigmoid(gl_ref[...])
        e_ = e_ref[...]
        ge = gate * e_
        g = g_ref[...]
        diff = h_ref[...] + _rms(ge, g) - t_ref[...]
        dh = diff * (1.0 / D)
        dh_ref[...] = dh
        dge, dgrow = _rms_bwd(dh, ge, g)
        dgl_ref[...] = (dge * e_ * gate * (1.0 - gate)).astype(BF16)
        de_ref[...] = (dge * gate).astype(BF16)
        part = _colsum(0.5 * jnp.mean(diff * diff, axis=-1, keepdims=True))

        @pl.when(_first(i))
        def _():
            loss_ref[...] = jnp.zeros_like(loss_ref)
            dg_ref[...] = jnp.zeros_like(dg_ref)

        loss_ref[...] += part + jnp.zeros((1, LANES), F32)
        dg_ref[...] += _colsum(dgrow)

    return pl.pallas_call(
        body, name="final_fwd_bwd",
        out_shape=(jax.ShapeDtypeStruct((S, D), F32), jax.ShapeDtypeStruct((S, D), BF16),
                   jax.ShapeDtypeStruct((S, D), BF16), jax.ShapeDtypeStruct((1, LANES), F32),
                   jax.ShapeDtypeStruct((1, D), F32)),
        grid=(S // tr,),
        in_specs=[_rows(tr, D), _rows(tr, D), _rows(tr, D), _vec(D), _rows(tr, D)],
        out_specs=(_rows(tr, D), _rows(tr, D), _rows(tr, D), _vec(LANES), _vec(D)),
        compiler_params=_cparams("arbitrary"))(h2, gl, e, g_post, target)


def _bwd_res_norm(dh_out, dhn, h, g_next, y, g_post, *, name, tr=128):
    S, D = h.shape
    tr = min(tr, S)

    def body(dho_ref, dhn_ref, h_ref, gn_ref, y_ref, gp_ref, dh_ref, dy_ref, dgn_ref, dgp_ref):
        i = pl.program_id(0)
        dx, dgn_rows = _rms_bwd(dhn_ref[...], h_ref[...], gn_ref[...])
        dh = dho_ref[...] + dx
        dh_ref[...] = dh
        dy, dgp_rows = _rms_bwd(dh, y_ref[...], gp_ref[...])
        dy_ref[...] = dy.astype(BF16)

        @pl.when(_first(i))
        def _():
            dgn_ref[...] = jnp.zeros_like(dgn_ref)
            dgp_ref[...] = jnp.zeros_like(dgp_ref)

        dgn_ref[...] += _colsum(dgn_rows)
        dgp_ref[...] += _colsum(dgp_rows)

    return pl.pallas_call(
        body, name=name,
        out_shape=(jax.ShapeDtypeStruct((S, D), F32), jax.ShapeDtypeStruct((S, D), BF16),
                   jax.ShapeDtypeStruct((1, D), F32), jax.ShapeDtypeStruct((1, D), F32)),
        grid=(S // tr,),
        in_specs=[_rows(tr, D), _rows(tr, D), _rows(tr, D), _vec(D), _rows(tr, D), _vec(D)],
        out_specs=(_rows(tr, D), _rows(tr, D), _vec(D), _vec(D)),
        compiler_params=_cparams("arbitrary"))(dh_out, dhn, h, g_next, y, g_post)


def _bwd_first(dh1, dhn1, x, g1, *, tr=256):
    S, D = x.shape
    tr = min(tr, S)

    def body(dh_ref, dhn_ref, x_ref, g_ref, dx_ref, dg_ref):
        i = pl.program_id(0)
        dx, dg_rows = _rms_bwd(dhn_ref[...], x_ref[...], g_ref[...])
        dx_ref[...] = dh_ref[...] + dx

        @pl.when(_first(i))
        def _():
            dg_ref[...] = jnp.zeros_like(dg_ref)

        dg_ref[...] += _colsum(dg_rows)

    return pl.pallas_call(
        body, name="bwd_first",
        out_shape=(jax.ShapeDtypeStruct((S, D), F32), jax.ShapeDtypeStruct((1, D), F32)),
        grid=(S // tr,), in_specs=[_rows(tr, D), _rows(tr, D), _rows(tr, D), _vec(D)],
        out_specs=(_rows(tr, D), _vec(D)), compiler_params=_cparams("arbitrary"))(dh1, dhn1, x, g1)


def _mix_bwd(dmixed, attn, y1, z, b_glu, g_attn, g_ssm, *, tr=256):
    S, AW = attn.shape
    SW = y1.shape[1]
    tr = min(tr, S)
    heads = AW // HEAD_DIM
    nd = len(DILATIONS)

    def body(*refs):
        dm_ref, a_ref, y_ref, z_ref, b_ref, ga_ref, gs_ref = refs[:7]
        da_refs, dd_refs = refs[7:7 + nd], refs[7 + nd:7 + 2 * nd]
        dz_ref, dy2_ref, dga_ref, dgs_ref, db_ref, scr, dd_scr = refs[7 + 2 * nd:]
        i = pl.program_id(0)
        attn_ = a_ref[...]
        dattn, dga_rows = _rms_bwd(dm_ref[:, :AW], attn_, ga_ref[...])
        prod = dattn * attn_
        for h in range(heads):
            sl = slice(h * HEAD_DIM, (h + 1) * HEAD_DIM)
            dd_scr[:, sl] = jnp.broadcast_to(jnp.sum(prod[:, sl], axis=-1, keepdims=True), (tr, HEAD_DIM))
        for n, d in enumerate(DILATIONS):
            _to_residues(scr, dattn, da_refs[n], d)
            _to_residues(scr, dd_scr[...], dd_refs[n], d)
        y2 = _gelu(y_ref[...])
        gate = _sigmoid(z_ref[...] + b_ref[...])
        dssm, dgs_rows = _rms_bwd(dm_ref[:, AW:], y2 * gate, gs_ref[...])
        dz = dssm * y2 * gate * (1.0 - gate)
        dz_ref[...] = dz.astype(BF16)
        dy2_ref[...] = dssm * gate

        @pl.when(_first(i))
        def _():
            dga_ref[...] = jnp.zeros_like(dga_ref)
            dgs_ref[...] = jnp.zeros_like(dgs_ref)
            db_ref[...] = jnp.zeros_like(db_ref)

        dga_ref[...] += _colsum(dga_rows)
        dgs_ref[...] += _colsum(dgs_rows)
        db_ref[...] += _colsum(dz)

    res_out = [_residue_spec(tr, d, AW) for d in DILATIONS]
    res = pl.pallas_call(
        body, name="mix_bwd",
        out_shape=([_residue_shape(S, d, AW, BF16) for d in DILATIONS]
                   + [_residue_shape(S, d, AW, F32) for d in DILATIONS]
                   + [jax.ShapeDtypeStruct((S, SW), BF16), jax.ShapeDtypeStruct((S, SW), F32),
                      jax.ShapeDtypeStruct((1, AW), F32), jax.ShapeDtypeStruct((1, SW), F32),
                      jax.ShapeDtypeStruct((1, SW), F32)]),
        grid=(S // tr,),
        in_specs=[_rows(tr, AW + SW), _rows(tr, AW), _rows(tr, SW), _rows(tr, SW), _vec(SW), _vec(AW), _vec(SW)],
        out_specs=res_out + res_out + [_rows(tr, SW), _rows(tr, SW), _vec(AW), _vec(SW), _vec(SW)],
        scratch_shapes=[_residue_scratch(tr, AW), pltpu.VMEM((tr, AW), F32)],
        compiler_params=_cparams("arbitrary"))(dmixed, attn, y1, z, b_glu, g_attn, g_ssm)
    return (res[:nd], res[nd:2 * nd]) + tuple(res[2 * nd:])


def _attn_masks(i):
    row = lax.broadcasted_iota(jnp.int32, (BLK, BLK), 0)
    col = lax.broadcasted_iota(jnp.int32, (BLK, BLK), 1)
    return col <= row, jnp.logical_and(col >= row, i > 0)


_NT = (((1,), (1,)), ((), ()))
_TN = (((0,), (0,)), ((), ()))


def _attn_in_specs(width, block_of):
    def at(part, prev):
        def index(r, i):
            blk = block_of(i)
            return (part, jnp.maximum(blk - 1, 0) if prev else blk, r)
        return pl.BlockSpec((None, BLK, width), index)
    return [at(0, False), at(1, False), at(1, True), at(2, False), at(2, True)]


def _proj_qkv(hn, w_in_f, *, tm=1024):
    S, D = hn.shape
    AW = w_in_f.shape[2]
    tm = min(tm, S)

    def body(a_ref, b_ref, *rest):
        o_refs, scr = rest[:-1], rest[-1]
        prod = jnp.dot(a_ref[...], b_ref[...], preferred_element_type=F32)
        _fill_strips(scr, prod)
        for o_ref, d in zip(o_refs, DILATIONS):
            if d == 1:
                o_ref[...] = prod.astype(BF16)
            else:
                _strips_to_residues(scr, o_ref, d)

    return pl.pallas_call(
        body, name="proj_qkv",
        out_shape=[jax.ShapeDtypeStruct((3, S // d, d * AW), BF16) for d in DILATIONS], grid=(S // tm, 3),
        in_specs=[pl.BlockSpec((tm, D), lambda i, j: (i, 0)), pl.BlockSpec((None, D, AW), lambda i, j: (j, 0, 0))],
        out_specs=[pl.BlockSpec((None, tm // d, d * AW), lambda i, j: (j, i, 0)) for d in DILATIONS],
        scratch_shapes=[_residue_scratch(tm, AW)],
        compiler_params=_cparams("parallel", "parallel"))(hn, w_in_f)


def _attn_fwd(qkv, d, heads):
    M = qkv.shape[1]
    nb = M // BLK
    width = heads * HEAD_DIM
    scale = 1.0 / math.sqrt(HEAD_DIM)

    def body(q_ref, kc_ref, kp_ref, vc_ref, vp_ref, o_ref, l_ref):
        mc, mp = _attn_masks(pl.program_id(1))
        ones = jnp.ones((BLK, HEAD_DIM), BF16)
        for h in range(heads):
            sl = slice(h * HEAD_DIM, (h + 1) * HEAD_DIM)
            q = q_ref[:, sl]
            sc = jnp.where(mc, lax.dot_general(q, kc_ref[:, sl], _NT, preferred_element_type=F32) * scale, NEG_INF)
            sp = jnp.where(mp, lax.dot_general(q, kp_ref[:, sl], _NT, preferred_element_type=F32) * scale, NEG_INF)
            m = jnp.max(jnp.maximum(sc, sp), axis=-1, keepdims=True)
            pc, pp = jnp.exp(sc - m).astype(BF16), jnp.exp(sp - m).astype(BF16)
            tot = (jnp.dot(pc, ones, preferred_element_type=F32) + jnp.dot(pp, ones, preferred_element_type=F32))
            acc = (jnp.dot(pc, vc_ref[:, sl], preferred_element_type=F32)
                   + jnp.dot(pp, vp_ref[:, sl], preferred_element_type=F32))
            o_ref[:, sl] = acc / tot
            l_ref[:, sl] = m + jnp.log(tot)

    out_spec = pl.BlockSpec((BLK, width), lambda r, i: (i, r))
    shape = jax.ShapeDtypeStruct((M, d * width), F32)
    return pl.pallas_call(
        body, name=f"attn_fwd_d{d}", out_shape=(shape, shape), grid=(d, nb),
        in_specs=_attn_in_specs(width, lambda i: i), out_specs=(out_spec, out_spec),
        compiler_params=_cparams("parallel", "parallel"))(qkv, qkv, qkv, qkv, qkv)


def _attn_bwd(qkv, dattn, lse, dd, d, heads):
    M = qkv.shape[1]
    nb = M // BLK
    width = heads * HEAD_DIM
    scale = 1.0 / math.sqrt(HEAD_DIM)

    def block_of(i):
        return nb - 1 - i

    def body(q_ref, kc_ref, kp_ref, vc_ref, vp_ref, da_ref, l_ref, dd_ref,
             dq_ref, dk_ref, dv_ref, dk_carry, dv_carry):
        @pl.when(pl.program_id(1) == 0)
        def _():
            dk_carry[...] = jnp.zeros_like(dk_carry)
            dv_carry[...] = jnp.zeros_like(dv_carry)

        mc, mp = _attn_masks(block_of(pl.program_id(1)))
        for h in range(heads):
            sl = slice(h * HEAD_DIM, (h + 1) * HEAD_DIM)
            q, kc, kp, vc, vp, da = q_ref[:, sl], kc_ref[:, sl], kp_ref[:, sl], vc_ref[:, sl], vp_ref[:, sl], da_ref[:, sl]
            lse_, dd_ = l_ref[:, sl], dd_ref[:, sl]
            sc = lax.dot_general(q, kc, _NT, preferred_element_type=F32) * scale
            sp = lax.dot_general(q, kp, _NT, preferred_element_type=F32) * scale
            pc = jnp.where(mc, jnp.exp(jnp.where(mc, sc, NEG_INF) - lse_), 0.0)
            pp = jnp.where(mp, jnp.exp(jnp.where(mp, sp, NEG_INF) - lse_), 0.0)
            dsc = (pc * (lax.dot_general(da, vc, _NT, preferred_element_type=F32) - dd_) * scale).astype(BF16)
            dsp = (pp * (lax.dot_general(da, vp, _NT, preferred_element_type=F32) - dd_) * scale).astype(BF16)
            dq_ref[:, sl] = (jnp.dot(dsc, kc, preferred_element_type=F32)
                             + jnp.dot(dsp, kp, preferred_element_type=F32)).astype(BF16)
            dk_ref[:, sl] = (lax.dot_general(dsc, q, _TN, preferred_element_type=F32) + dk_carry[:, sl]).astype(BF16)
            dv_ref[:, sl] = (lax.dot_general(pc.astype(BF16), da, _TN, preferred_element_type=F32)
                             + dv_carry[:, sl]).astype(BF16)
            dk_carry[:, sl] = lax.dot_general(dsp, q, _TN, preferred_element_type=F32)
            dv_carry[:, sl] = lax.dot_general(pp.astype(BF16), da, _TN, preferred_element_type=F32)

    blk = pl.BlockSpec((BLK, width), lambda r, i: (block_of(i), r))
    shape = jax.ShapeDtypeStruct((M, d * width), BF16)
    return pl.pallas_call(
        body, name=f"attn_bwd_d{d}", out_shape=(shape,) * 3, grid=(d, nb),
        in_specs=_attn_in_specs(width, block_of) + [blk, blk, blk], out_specs=(blk,) * 3,
        scratch_shapes=[pltpu.VMEM((BLK, width), F32), pltpu.VMEM((BLK, width), F32)],
        compiler_params=_cparams("arbitrary", "arbitrary"))(qkv, qkv, qkv, qkv, qkv, dattn, lse, dd)


def _dproj_join(dqs, dks, dvs, du, *, tr=256):
    S, SW = du.shape
    AW = dqs[0].shape[1]
    tr = min(tr, S)
    nd = len(DILATIONS)

    def body(*refs):
        du_ref, out_ref, scr = refs[3 * nd:]
        for part in range(3):
            total = functools.reduce(jnp.add, [_from_residues(scr, refs[part * nd + n], d)
                                               for n, d in enumerate(DILATIONS)])
            out_ref[:, part * AW:(part + 1) * AW] = total.astype(BF16)
        out_ref[:, 3 * AW:] = du_ref[...].astype(BF16)

    return pl.pallas_call(
        body, name="dproj_join", out_shape=jax.ShapeDtypeStruct((S, 3 * AW + SW), BF16), grid=(S // tr,),
        in_specs=[_residue_spec(tr, d, AW) for d in DILATIONS] * 3 + [_rows(tr, SW)],
        out_specs=_rows(tr, 3 * AW + SW), scratch_shapes=[_residue_scratch(tr, AW)],
        compiler_params=_cparams("parallel"))(*dqs, *dks, *dvs, du)


def _ssm_disc(lr, li, ldt):
    dt = jnp.exp(ldt)
    mag = jnp.exp(lr * dt)
    ar = mag * jnp.cos(li * dt)
    ai = mag * jnp.sin(li * dt)
    nr = ar - 1.0
    den = lr * lr + li * li
    return ar, ai, (nr * lr + ai * li) / den, (ai * lr - nr * li) / den


def _ssm_tile_powers(lr, li, ldt, reverse):
    t = lax.broadcasted_iota(jnp.int32, (TILE, 1), 0)
    n = (TILE - t if reverse else t + 1).astype(F32)
    dt = jnp.exp(ldt)
    mag = jnp.exp(n * (lr * dt))
    ang = n * (li * dt)
    return mag * jnp.cos(ang), mag * jnp.sin(ang) * (-1.0 if reverse else 1.0)


def _cmul(ar, ai, br, bi):
    return ar * br - ai * bi, ar * bi + ai * br


def _scan(xr, xi, ar, ai, pr, pi, cr, ci, reverse):
    T = xr.shape[0]
    sub = lax.broadcasted_iota(jnp.int32, xr.shape, 0) & (TILE - 1)
    sh = 1
    while sh < TILE:
        if reverse:
            keep = sub < TILE - sh
            sr, si = pltpu.roll(xr, T - sh, 0), pltpu.roll(xi, T - sh, 0)
        else:
            keep = sub >= sh
            sr, si = pltpu.roll(xr, sh, 0), pltpu.roll(xi, sh, 0)
        sr, si = jnp.where(keep, sr, 0.0), jnp.where(keep, si, 0.0)
        qr, qi = _cmul(ar, ai, sr, si)
        xr, xi = xr + qr, xi + qi
        ar, ai = _cmul(ar, ai, ar, ai)
        sh *= 2
    n = T // TILE
    out_r, out_i = [None] * n, [None] * n
    edge = 0 if reverse else TILE - 1
    for j in (reversed(range(n)) if reverse else range(n)):
        er, ei = _cmul(pr, pi, cr, ci)
        sr, si = xr[j * TILE:(j + 1) * TILE] + er, xi[j * TILE:(j + 1) * TILE] + ei
        out_r[j], out_i[j] = sr, si
        cr, ci = sr[edge:edge + 1], si[edge:edge + 1]
    return jnp.concatenate(out_r, axis=0), jnp.concatenate(out_i, axis=0), cr, ci


def _ssm_specs(T, nch, rev):
    def t_of(c):
        return nch - 1 - c if rev else c
    tok = pl.BlockSpec((T, LANES), lambda j, c: (t_of(c), j))
    par = pl.BlockSpec((None, 1, STATE_LANES), lambda j, c: (j, 0, 0))
    bmat = pl.BlockSpec((None, LANES, STATE_LANES), lambda j, c: (j, 0, 0))
    cmat = pl.BlockSpec((None, STATE_LANES, LANES), lambda j, c: (j, 0, 0))
    dvec = pl.BlockSpec((1, LANES), lambda j, c: (0, j))
    return tok, par, bmat, cmat, dvec


def _ssm_fwd(u, lr_e, li_e, ldt_e, bre_e, bim_e, cre_e, cim_e, d_skip):
    S, SW = u.shape
    T = min(SSM_CHUNK, S)
    nch, nbk = S // T, SW // LANES
    tok, par, bmat, cmat, dvec = _ssm_specs(T, nch, False)
    state_spec = pl.BlockSpec((T, STATE_LANES), lambda j, c: (c, j))
    carry_spec = pl.BlockSpec((None, 1, STATE_LANES), lambda j, c: (c, 0, j))

    def body(u_ref, lr_ref, li_ref, ldt_ref, bre_ref, bim_ref, cre_ref, cim_ref, d_ref,
             y_ref, y2_ref, sr_ref, si_ref, er_ref, ei_ref, bbr, bbi, a_scr, pw, carry):
        c = pl.program_id(1)

        @pl.when(c == 0)
        def _():
            lr, li, ldt = lr_ref[...], li_ref[...], ldt_ref[...]
            ar, ai, kr, ki = _ssm_disc(lr, li, ldt)
            a_scr[0], a_scr[1] = ar, ai
            bbr[...] = (kr * bre_ref[...] - ki * bim_ref[...]).astype(BF16)
            bbi[...] = (kr * bim_ref[...] + ki * bre_ref[...]).astype(BF16)
            pw[0], pw[1] = _ssm_tile_powers(lr, li, ldt, False)
            carry[...] = jnp.zeros_like(carry)

        u_ = u_ref[...]
        ub = u_.astype(BF16)
        sr, si, cr, ci = _scan(jnp.dot(ub, bbr[...], preferred_element_type=F32),
                               jnp.dot(ub, bbi[...], preferred_element_type=F32),
                               a_scr[0], a_scr[1], pw[0], pw[1], carry[0], carry[1], False)
        carry[0], carry[1] = cr, ci
        er_ref[...], ei_ref[...] = cr, ci
        sr_ref[...], si_ref[...] = sr, si
        y0 = (jnp.dot(sr.astype(BF16), cre_ref[...].astype(BF16), preferred_element_type=F32)
              - jnp.dot(si.astype(BF16), cim_ref[...].astype(BF16), preferred_element_type=F32))
        y1 = y0 + d_ref[...] * u_
        y_ref[...] = y1
        y2_ref[...] = _gelu(y1).astype(BF16)

    states = jax.ShapeDtypeStruct((S, nbk * STATE_LANES), F32)
    ends = jax.ShapeDtypeStruct((nch, 1, nbk * STATE_LANES), F32)
    return pl.pallas_call(
        body, name="ssm_fwd",
        out_shape=(jax.ShapeDtypeStruct((S, SW), F32), jax.ShapeDtypeStruct((S, SW), BF16), states, states, ends, ends),
        grid=(nbk, nch), in_specs=[tok, par, par, par, bmat, bmat, cmat, cmat, dvec],
        out_specs=(tok, tok, state_spec, state_spec, carry_spec, carry_spec),
        scratch_shapes=[pltpu.VMEM((LANES, STATE_LANES), BF16), pltpu.VMEM((LANES, STATE_LANES), BF16),
                        pltpu.VMEM((2, 1, STATE_LANES), F32), pltpu.VMEM((2, TILE, STATE_LANES), F32),
                        pltpu.VMEM((2, 1, STATE_LANES), F32)],
        compiler_params=_cparams("arbitrary", "arbitrary"),
    )(u, lr_e, li_e, ldt_e, bre_e, bim_e, cre_e, cim_e, d_skip)


def _ssm_bwd(u, y1, dy2a, dy2b, st_r, st_i, ends_r, ends_i, lr_e, li_e, ldt_e, bre_e, bim_e, cre_e, cim_e, d_skip):
    S, SW = u.shape
    T = min(SSM_CHUNK, S)
    nch, nbk = S // T, SW // LANES
    tok, par, bmat, cmat, dvec = _ssm_specs(T, nch, True)
    state_spec = pl.BlockSpec((T, STATE_LANES), lambda j, c: (nch - 1 - c, j))
    prev_spec = pl.BlockSpec((None, 1, STATE_LANES), lambda j, c: (jnp.maximum(nch - 2 - c, 0), 0, j))
    acc8 = pl.BlockSpec((None, 8, STATE_LANES), lambda j, c: (j, 0, 0))
    dd8 = pl.BlockSpec((None, 8, LANES), lambda j, c: (j, 0, 0))

    def body(u_ref, y_ref, da_ref, db_ref, sr_ref, si_ref, pr_ref, pi_ref, lr_ref, li_ref, ldt_ref,
             bre_ref, bim_ref, cre_ref, cim_ref, d_ref,
             du_ref, dar_ref, dai_ref, dcr_ref, dci_ref, dbr_ref, dbi_ref, ddk_ref,
             bbr, bbi, a_scr, pw, carry):
        c = pl.program_id(1)

        @pl.when(c == 0)
        def _():
            lr, li, ldt = lr_ref[...], li_ref[...], ldt_ref[...]
            ar, ai, kr, ki = _ssm_disc(lr, li, ldt)
            a_scr[0], a_scr[1] = ar, -ai
            bbr[...] = (kr * bre_ref[...] - ki * bim_ref[...]).astype(BF16)
            bbi[...] = (kr * bim_ref[...] + ki * bre_ref[...]).astype(BF16)
            pw[0], pw[1] = _ssm_tile_powers(lr, li, ldt, True)
            carry[...] = jnp.zeros_like(carry)
            for ref in (dar_ref, dai_ref, dcr_ref, dci_ref, dbr_ref, dbi_ref, ddk_ref):
                ref[...] = jnp.zeros_like(ref)

        u_ = u_ref[...]
        ub = u_.astype(BF16)
        dy1 = (da_ref[...] + db_ref[...]) * _gelu_grad(y_ref[...])
        dyb = dy1.astype(BF16)

        sr, si = sr_ref[...], si_ref[...]
        has_prev = c < nch - 1
        s0r = jnp.where(has_prev, pr_ref[...], 0.0)
        s0i = jnp.where(has_prev, pi_ref[...], 0.0)

        cre_b, cim_b = cre_ref[...].astype(BF16), cim_ref[...].astype(BF16)
        gr, gi, cr, ci = _scan(lax.dot_general(dyb, cre_b, _NT, preferred_element_type=F32),
                               -lax.dot_general(dyb, cim_b, _NT, preferred_element_type=F32),
                               a_scr[0], a_scr[1], pw[0], pw[1], carry[0], carry[1], True)
        carry[0], carry[1] = cr, ci

        row = lax.broadcasted_iota(jnp.int32, (T, STATE_LANES), 0)
        spr = jnp.where(row == 0, s0r, pltpu.roll(sr, 1, 0))
        spi = jnp.where(row == 0, s0i, pltpu.roll(si, 1, 0))

        def fold(a):
            return jnp.sum(a.reshape(T // 8, 8, a.shape[-1]), axis=0)

        dar_ref[...] += fold(gr * spr + gi * spi)
        dai_ref[...] += fold(gi * spr - gr * spi)
        srb, sib, grb, gib = sr.astype(BF16), si.astype(BF16), gr.astype(BF16), gi.astype(BF16)
        dcr_ref[...] += lax.dot_general(srb, dyb, _TN, preferred_element_type=F32)
        dci_ref[...] -= lax.dot_general(sib, dyb, _TN, preferred_element_type=F32)
        dbr_ref[...] += lax.dot_general(ub, grb, _TN, preferred_element_type=F32)
        dbi_ref[...] += lax.dot_general(ub, gib, _TN, preferred_element_type=F32)
        du_ref[...] = (lax.dot_general(grb, bbr[...], _NT, preferred_element_type=F32)
                       + lax.dot_general(gib, bbi[...], _NT, preferred_element_type=F32)
                       + dy1 * d_ref[...])
        ddk_ref[...] += fold(dy1 * u_)

    return pl.pallas_call(
        body, name="ssm_bwd",
        out_shape=(jax.ShapeDtypeStruct((S, SW), F32),
                   jax.ShapeDtypeStruct((nbk, 8, STATE_LANES), F32), jax.ShapeDtypeStruct((nbk, 8, STATE_LANES), F32),
                   jax.ShapeDtypeStruct((nbk, STATE_LANES, LANES), F32), jax.ShapeDtypeStruct((nbk, STATE_LANES, LANES), F32),
                   jax.ShapeDtypeStruct((nbk, LANES, STATE_LANES), F32), jax.ShapeDtypeStruct((nbk, LANES, STATE_LANES), F32),
                   jax.ShapeDtypeStruct((nbk, 8, LANES), F32)),
        grid=(nbk, nch),
        in_specs=[tok, tok, tok, tok, state_spec, state_spec, prev_spec, prev_spec, par, par, par,
                  bmat, bmat, cmat, cmat, dvec],
        out_specs=(tok, acc8, acc8, cmat, cmat, bmat, bmat, dd8),
        scratch_shapes=[pltpu.VMEM((LANES, STATE_LANES), BF16), pltpu.VMEM((LANES, STATE_LANES), BF16),
                        pltpu.VMEM((2, 1, STATE_LANES), F32), pltpu.VMEM((2, TILE, STATE_LANES), F32),
                        pltpu.VMEM((2, 1, STATE_LANES), F32)],
        compiler_params=_cparams("arbitrary", "arbitrary"),
    )(u, y1, dy2a, dy2b, st_r, st_i, ends_r, ends_i, lr_e, li_e, ldt_e, bre_e, bim_e, cre_e, cim_e, d_skip)


def _ssm_param_bwd(dar8, dai8, dbr_e, dbi_e, lr_e, li_e, ldt_e, bre_e, bim_e):
    nbk = lr_e.shape[0]
    par = pl.BlockSpec((None, 1, STATE_LANES), lambda j: (j, 0, 0))
    acc8 = pl.BlockSpec((None, 8, STATE_LANES), lambda j: (j, 0, 0))
    bmat = pl.BlockSpec((None, LANES, STATE_LANES), lambda j: (j, 0, 0))

    def body(dar_ref, dai_ref, dbr_ref, dbi_ref, lr_ref, li_ref, ldt_ref, bre_ref, bim_ref,
             dlr_ref, dli_ref, dldt_ref, dbre_ref, dbim_ref):
        lr, li, ldt = lr_ref[...], li_ref[...], ldt_ref[...]
        (ar, ai, kr, ki), vjp = jax.vjp(_ssm_disc, lr, li, ldt)
        dbr, dbi, bre, bim = dbr_ref[...], dbi_ref[...], bre_ref[...], bim_ref[...]
        dbre_ref[...] = kr * dbr + ki * dbi
        dbim_ref[...] = kr * dbi - ki * dbr
        dkr = _colsum(dbr * bre + dbi * bim)
        dki = _colsum(dbi * bre - dbr * bim)
        dlr, dli, dldt = vjp((_colsum(dar_ref[...]), _colsum(dai_ref[...]), dkr, dki))
        dlr_ref[...] = dlr
        dli_ref[...] = dli
        tot = jnp.broadcast_to(dldt, (8, STATE_LANES))
        sh = 1
        while sh < SSM_P:
            tot = tot + pltpu.roll(tot, STATE_LANES - sh, 1)
            sh *= 2
        dldt_ref[...] = tot[:1]

    vec = jax.ShapeDtypeStruct((nbk, 1, STATE_LANES), F32)
    mat = jax.ShapeDtypeStruct((nbk, LANES, STATE_LANES), F32)
    return pl.pallas_call(
        body, name="ssm_param_bwd", out_shape=(vec, vec, vec, mat, mat), grid=(nbk,),
        in_specs=[acc8, acc8, bmat, bmat, par, par, par, bmat, bmat],
        out_specs=(par, par, par, bmat, bmat), compiler_params=_cparams("parallel"),
    )(dar8, dai8, dbr_e, dbi_e, lr_e, li_e, ldt_e, bre_e, bim_e)


def _expand_b(b):
    G = b.shape[0]
    bt = b.transpose(0, 2, 1).reshape(G // GROUPS_PER_BLOCK, GROUPS_PER_BLOCK, SSM_C, SSM_P)
    eye = jnp.eye(GROUPS_PER_BLOCK, dtype=b.dtype)
    return (bt[:, :, :, None, :] * eye[None, :, None, :, None]).reshape(G // GROUPS_PER_BLOCK, LANES, STATE_LANES)


def _collapse_b(be):
    nbk = be.shape[0]
    eye = jnp.eye(GROUPS_PER_BLOCK, dtype=be.dtype)
    d5 = be.reshape(nbk, GROUPS_PER_BLOCK, SSM_C, GROUPS_PER_BLOCK, SSM_P)
    d4 = (d5 * eye[None, :, None, :, None]).sum(axis=3)
    return d4.transpose(0, 1, 3, 2).reshape(nbk * GROUPS_PER_BLOCK, SSM_P, SSM_C)


def _expand_c(cm):
    G = cm.shape[0]
    ct = cm.transpose(0, 2, 1).reshape(G // GROUPS_PER_BLOCK, GROUPS_PER_BLOCK, SSM_P, SSM_C)
    eye = jnp.eye(GROUPS_PER_BLOCK, dtype=cm.dtype)
    return (ct[:, :, :, None, :] * eye[None, :, None, :, None]).reshape(G // GROUPS_PER_BLOCK, STATE_LANES, LANES)


def _collapse_c(ce):
    nbk = ce.shape[0]
    eye = jnp.eye(GROUPS_PER_BLOCK, dtype=ce.dtype)
    d5 = ce.reshape(nbk, GROUPS_PER_BLOCK, SSM_P, GROUPS_PER_BLOCK, SSM_C)
    d4 = (d5 * eye[None, :, None, :, None]).sum(axis=3)
    return d4.transpose(0, 1, 3, 2).reshape(nbk * GROUPS_PER_BLOCK, SSM_C, SSM_P)


def _place():
    x, y, c = lax.axis_index("x"), lax.axis_index("y"), lax.axis_index("c")
    return x, y, c


def _other_chips(x, y):
    return [(1 - x, y), (x, 1 - y), (1 - x, 1 - y)]


_ANY = pl.BlockSpec(memory_space=pl.ANY)


_HBM = pl.BlockSpec(memory_space=pltpu.HBM)
_SEM = pl.BlockSpec(memory_space=pltpu.SEMAPHORE)
_EFFECT = pltpu.SideEffectType.DATAFLOW_SIDE_EFFECTING
_TOKEN = jax.ShapeDtypeStruct((8, LANES), F32)


def _hbm(a):
    return pltpu.with_memory_space_constraint(a, pltpu.HBM)


def _place_own(src, *, gather, name, tr=512):
    R, C = src.shape[-2:]
    tr = min(tr, R)
    x, y, _ = _place()
    me = (2 * x + y).astype(jnp.int32).reshape(1)

    def body(me_ref, s_ref, o_ref):
        o_ref[...] = s_ref[...].astype(BF16)

    own = pl.BlockSpec((None, tr, C), lambda i, me_ref: (me_ref[0], i, 0))
    grid_spec = pltpu.PrefetchScalarGridSpec(
        num_scalar_prefetch=1, grid=(R // tr,),
        in_specs=[pl.BlockSpec((tr, C), lambda i, me_ref: (i, 0)) if gather else own], out_specs=own)
    return pl.pallas_call(
        body, name=name, grid_spec=grid_spec, out_shape=jax.ShapeDtypeStruct((N_CHIPS, R, C), BF16),
        compiler_params=_cparams("parallel"))(me, src)


def _exchange_copy(src_slot, land_slot, send, recv, k, j, peer, c):
    return pltpu.make_async_remote_copy(
        src_ref=src_slot, dst_ref=land_slot, send_sem=send.at[3 * k + j], recv_sem=recv.at[3 * k + j],
        device_id=(peer[0], peer[1], c), device_id_type=MESH)


def _exchange_start(lands, srcs, groups, *, name):
    n, ng = len(lands), len(groups)
    bufs = list(lands) + list(srcs)
    nb = len(bufs)

    def body(*refs):
        lnd, src, sems = refs[:n], refs[n:nb], refs[nb:nb + 2 * ng]
        token = refs[2 * nb + 2 * ng]
        x, y, c = _place()
        me = 2 * x + y
        for gi, group in enumerate(groups):
            for k, w in enumerate(group):
                for j, peer in enumerate(_other_chips(x, y)):
                    sent = src[w].at[2 * peer[0] + peer[1]] if src else lnd[w].at[me]
                    _exchange_copy(sent, lnd[w].at[me], sems[2 * gi], sems[2 * gi + 1], k, j, peer, c).start()
        token[...] = jnp.zeros_like(token)

    sem_shapes = [pltpu.SemaphoreType.DMA((3 * len(g),)) for g in groups for _ in range(2)]
    res = pl.pallas_call(
        body, name=name,
        out_shape=sem_shapes + [pltpu.HBM(a.shape, a.dtype) for a in bufs] + [_TOKEN],
        in_specs=[_HBM] * nb,
        out_specs=[_SEM] * (2 * ng) + [_HBM] * nb + [pl.BlockSpec(memory_space=pltpu.VMEM)],
        input_output_aliases={i: 2 * ng + i for i in range(nb)},
        compiler_params=pltpu.CompilerParams(has_side_effects=_EFFECT),
    )(*[_hbm(a) for a in bufs])
    sems = [(res[2 * gi], res[2 * gi + 1]) for gi in range(ng)]
    return sems, res[2 * ng:2 * ng + n], res[2 * ng + n:2 * ng + nb], res[-1]


def _exchange_wait(lands, srcs, sems, after, *, name):
    n = len(lands)
    bufs = list(lands) + list(srcs)
    nb = len(bufs)
    send_sems, recv_sems = sems

    def body(*refs):
        lnd, src, send, recv = refs[:n], refs[n:nb], refs[nb], refs[nb + 1]
        x, y, c = _place()
        for k in range(n):
            for j, peer in enumerate(_other_chips(x, y)):
                slot = 2 * peer[0] + peer[1]
                copy = _exchange_copy((src[k] if src else lnd[k]).at[slot], lnd[k].at[slot], send, recv, k, j, peer, c)
                copy.wait_send()
                copy.wait_recv()

    res = pl.pallas_call(
        body, name=name, out_shape=[pltpu.HBM(a.shape, a.dtype) for a in bufs],
        in_specs=[_HBM] * nb + [_SEM, _SEM, _ANY], out_specs=[_HBM] * nb,
        input_output_aliases={i: i for i in range(nb)},
        compiler_params=pltpu.CompilerParams(has_side_effects=_EFFECT),
    )(*bufs, send_sems, recv_sems, after)
    return res[:n]


def _sum_partials(land, *, name, tr=256):
    _, R, C = land.shape
    tr = min(tr, R)

    def body(l_ref, o_ref):
        acc = l_ref[0].astype(F32)
        for k in range(1, N_CHIPS):
            acc = acc + l_ref[k].astype(F32)
        o_ref[...] = acc

    return pl.pallas_call(
        body, name=name, out_shape=jax.ShapeDtypeStruct((R, C), F32), grid=(R // tr,),
        in_specs=[pl.BlockSpec((N_CHIPS, tr, C), lambda i: (0, i, 0))], out_specs=_rows(tr, C),
        compiler_params=_cparams("parallel"))(land)


def _swap_with_sibling(sums, *, name):
    n = len(sums)

    def body(*refs):
        ins, outs = refs[:n], refs[n:2 * n]
        send_sems, recv_sems = refs[2 * n:]
        x, y, c = _place()
        copies = [pltpu.make_async_remote_copy(
            src_ref=ins[w], dst_ref=outs[w], send_sem=send_sems.at[w], recv_sem=recv_sems.at[w],
            device_id=(x, y, 1 - c), device_id_type=MESH) for w in range(n)]
        for cp in copies:
            cp.start()
        for cp in copies:
            cp.wait_recv()
            cp.wait_send()

    return pl.pallas_call(
        body, name=name,
        out_shape=[jax.ShapeDtypeStruct(s.shape, s.dtype) for s in sums],
        in_specs=[_ANY] * n, out_specs=[_ANY] * n,
        scratch_shapes=[pltpu.SemaphoreType.DMA((n,)), pltpu.SemaphoreType.DMA((n,))],
    )(*sums)


def _adamw_math(w, g, m, v):
    m = ADAM_B1 * m + (1.0 - ADAM_B1) * g
    v = ADAM_B2 * v + (1.0 - ADAM_B2) * (g * g)
    m_hat = m / (1.0 - ADAM_B1 ** ADAM_STEP)
    v_hat = v / (1.0 - ADAM_B2 ** ADAM_STEP)
    delta = -ADAM_LR * (m_hat / (jnp.sqrt(v_hat) + ADAM_EPS) + ADAM_WD * w)
    return delta, m, v


def _adamw_pair(mine, theirs, w, m, v, *, name, tr=128):
    R, C = w.shape
    tr = min(tr, R)

    def body(a_ref, b_ref, w_ref, m_ref, v_ref, g_ref, d_ref, nm_ref, nv_ref):
        g = a_ref[...] + b_ref[...]
        g_ref[...] = g
        d_ref[...], nm_ref[...], nv_ref[...] = _adamw_math(w_ref[...], g, m_ref[...], v_ref[...])

    shape = jax.ShapeDtypeStruct((R, C), F32)
    return pl.pallas_call(
        body, name=name, out_shape=(shape,) * 4, grid=(R // tr,),
        in_specs=[_rows(tr, C)] * 5, out_specs=(_rows(tr, C),) * 4,
        compiler_params=_cparams("parallel"))(mine, theirs, w, m, v)


def _all_reduce_small(packed):
    R = packed.shape[0]
    half = R // 2

    def body(x_ref, g_ref, sib_ref, pair_ref, land_ref, send_sems, recv_sems):
        x, y, c = _place()
        me = 2 * x + y
        sibling = (x, y, 1 - c)

        swap = pltpu.make_async_remote_copy(
            src_ref=x_ref, dst_ref=sib_ref, send_sem=send_sems.at[0], recv_sem=recv_sems.at[0],
            device_id=sibling, device_id_type=MESH)
        swap.start()
        swap.wait()
        mine, theirs = x_ref[...], sib_ref[...]
        south = c == 0
        pair_ref[...] = jnp.where(south, mine, theirs) + jnp.where(south, theirs, mine)

        land_ref[me] = pair_ref[c]
        for j, (px, py) in enumerate(_other_chips(x, y)):
            pltpu.make_async_remote_copy(
                src_ref=pair_ref.at[c], dst_ref=land_ref.at[me], send_sem=send_sems.at[1 + j],
                recv_sem=recv_sems.at[1 + j], device_id=(px, py, c), device_id_type=MESH).start()
        for j, (px, py) in enumerate(_other_chips(x, y)):
            arrival = pltpu.make_async_remote_copy(
                src_ref=pair_ref.at[c], dst_ref=land_ref.at[2 * px + py], send_sem=send_sems.at[1 + j],
                recv_sem=recv_sems.at[1 + j], device_id=(px, py, c), device_id_type=MESH)
            arrival.wait_recv()
            arrival.wait_send()
        total = land_ref[0]
        for k in range(1, N_CHIPS):
            total = total + land_ref[k]
        g_ref[c] = total

        give = pltpu.make_async_remote_copy(
            src_ref=g_ref.at[c], dst_ref=g_ref.at[c], send_sem=send_sems.at[4], recv_sem=recv_sems.at[4],
            device_id=sibling, device_id_type=MESH)
        give.start()
        take = pltpu.make_async_remote_copy(
            src_ref=g_ref.at[c], dst_ref=g_ref.at[1 - c], send_sem=send_sems.at[4], recv_sem=recv_sems.at[4],
            device_id=sibling, device_id_type=MESH)
        take.wait_recv()
        give.wait_send()

    vm = pl.BlockSpec(memory_space=pltpu.VMEM)
    return pl.pallas_call(
        body, name="all_reduce_small", out_shape=jax.ShapeDtypeStruct((2, half, LANES), F32),
        in_specs=[vm], out_specs=vm,
        scratch_shapes=[pltpu.VMEM((2, half, LANES), F32), pltpu.VMEM((2, half, LANES), F32),
                        pltpu.VMEM((N_CHIPS, half, LANES), F32),
                        pltpu.SemaphoreType.DMA((5,)), pltpu.SemaphoreType.DMA((5,))],
        compiler_params=pltpu.CompilerParams(vmem_limit_bytes=VMEM_LIMIT_BYTES),
    )(packed.reshape(2, half, LANES)).reshape(R, LANES)


def _adamw_small(g, w, m, v):
    R = g.shape[0]
    tr = PACK_ROWS

    def body(g_ref, w_ref, m_ref, v_ref, d_ref, nm_ref, nv_ref):
        d_ref[...], nm_ref[...], nv_ref[...] = _adamw_math(w_ref[...], g_ref[...], m_ref[...], v_ref[...])

    shape = jax.ShapeDtypeStruct((R, LANES), F32)
    return pl.pallas_call(
        body, name="adamw_small", out_shape=(shape,) * 3, grid=(R // tr,),
        in_specs=[_rows(tr, LANES)] * 4, out_specs=(_rows(tr, LANES),) * 3,
        compiler_params=_cparams("parallel"))(g, w, m, v)


def _pack(arrays):
    parts, layout = [], []
    for a in arrays:
        n = a.size
        rows = -(-n // (8 * LANES)) * 8
        flat = jnp.pad(a.reshape(-1).astype(F32), (0, rows * LANES - n))
        parts.append(flat.reshape(rows, LANES))
        layout.append((rows, n, a.shape))
    total = sum(r for r, _, _ in layout)
    parts.append(jnp.zeros((-total % PACK_ROWS, LANES), F32))
    return jnp.concatenate(parts, axis=0), layout


def _unpack(buf, layout):
    out, r0 = [], 0
    for rows, n, shape in layout:
        out.append(buf[r0:r0 + rows].reshape(-1)[:n].reshape(shape))
        r0 += rows
    return out


SMALL = ("mix_norm_pre", "lam_re", "lam_im", "log_dt", "ssm_b_re", "ssm_b_im", "ssm_c_re", "ssm_c_im",
         "ssm_d", "b_glu", "attn_out_norm", "ssm_out_norm", "mix_norm_post", "mlp_norm_pre",
         "mlp_norm_post", "ple_norm_pre", "ple_norm_post")
BIG = ("w_in", "w_glu", "w_out", "w_up", "w_down", "w_ple_gate", "w_ple_proj")
WEIGHTS = ("mix_norm_pre", "w_in", "lam_re", "lam_im", "log_dt", "ssm_b_re", "ssm_b_im", "ssm_c_re",
           "ssm_c_im", "ssm_d", "w_glu", "b_glu", "attn_out_norm", "ssm_out_norm", "w_out",
           "mix_norm_post", "mlp_norm_pre", "w_up", "w_down", "mlp_norm_post", "ple_norm_pre",
           "w_ple_gate", "w_ple_proj", "ple_norm_post")


def kernel(x, p, mix_norm_pre, w_in, lam_re, lam_im, log_dt, ssm_b_re, ssm_b_im, ssm_c_re, ssm_c_im, ssm_d, w_glu, b_glu, attn_out_norm, ssm_out_norm, w_out, mix_norm_post, mlp_norm_pre, w_up, w_down, mlp_norm_post, ple_norm_pre, w_ple_gate, w_ple_proj, ple_norm_post, loss_target, m_mix_norm_pre, m_w_in, m_lam_re, m_lam_im, m_log_dt, m_ssm_b_re, m_ssm_b_im, m_ssm_c_re, m_ssm_c_im, m_ssm_d, m_w_glu, m_b_glu, m_attn_out_norm, m_ssm_out_norm, m_w_out, m_mix_norm_post, m_mlp_norm_pre, m_w_up, m_w_down, m_mlp_norm_post, m_ple_norm_pre, m_w_ple_gate, m_w_ple_proj, m_ple_norm_post, v_mix_norm_pre, v_w_in, v_lam_re, v_lam_im, v_log_dt, v_ssm_b_re, v_ssm_b_im, v_ssm_c_re, v_ssm_c_im, v_ssm_d, v_w_glu, v_b_glu, v_attn_out_norm, v_ssm_out_norm, v_w_out, v_mix_norm_post, v_mlp_norm_pre, v_w_up, v_w_down, v_mlp_norm_post, v_ple_norm_pre, v_w_ple_gate, v_w_ple_proj, v_ple_norm_post):
    args = dict(locals())
    W = {n: args[n][0] for n in WEIGHTS}
    Mo = {n: args["m_" + n][0] for n in WEIGHTS}
    Vo = {n: args["v_" + n][0] for n in WEIGHTS}
    xs, ps, tgt = x[0], p[0, 0], loss_target[0]
    S, D = xs.shape
    SW = W["ssm_d"].shape[0]
    AW = W["attn_out_norm"].shape[0]
    heads = AW // HEAD_DIM
    G = SW // SSM_C
    nbk = SW // LANES
    assert W["w_in"].shape[1] * N_CHIPS == 3 * AW + SW and AW == SW

    row = lambda a: a.reshape(1, -1)

    ag_groups = (("w_in",), ("w_glu", "w_out"), ("w_up",), ("w_down", "w_ple_gate", "w_ple_proj"))
    ag_names = [n for g in ag_groups for n in g]
    ag_sems, ag_land, _, ag_token = _exchange_start(
        [_place_own(W[n], gather=True, name="ag_place_" + n) for n in ag_names], [],
        [[ag_names.index(n) for n in g] for g in ag_groups], name="ag_start")

    def gathered(gi, after):
        got = _exchange_wait([ag_land[ag_names.index(n)] for n in ag_groups[gi]], [], ag_sems[gi], after,
                             name=f"ag_wait_{gi}")
        return dict(zip(ag_groups[gi], got))

    lr_e = W["lam_re"].reshape(nbk, 1, STATE_LANES)
    li_e = W["lam_im"].reshape(nbk, 1, STATE_LANES)
    ldt_e = jnp.repeat(W["log_dt"], SSM_P).reshape(nbk, 1, STATE_LANES)
    bre_e, bim_e = _expand_b(W["ssm_b_re"]), _expand_b(W["ssm_b_im"])
    cre_e, cim_e = _expand_c(W["ssm_c_re"]), _expand_c(W["ssm_c_im"])
    d_row = row(W["ssm_d"])

    hn1 = _norm_cast(xs, row(W["mix_norm_pre"]) + ag_token[0, 0], name="norm_in")
    w_in_f = gathered(0, hn1)["w_in"]
    qkv_b = _proj_qkv(hn1, w_in_f)
    u = _matmul(hn1, w_in_f, name="proj_u", b_shards=N_CHIPS, b_cols=(3 * AW, SW))
    outs, lses = zip(*[_attn_fwd(qb, d, heads) for d, qb in zip(DILATIONS, qkv_b)])
    y1, y2b, st_r, st_i, ends_r, ends_i = _ssm_fwd(u, lr_e, li_e, ldt_e, bre_e, bim_e, cre_e, cim_e, d_row)
    full = gathered(1, y2b)
    w_glu_f = full["w_glu"].reshape(SW, SW)
    w_out_f = full["w_out"].reshape(AW + SW, D)
    z = _matmul(y2b, w_glu_f, name="glu_z")
    attn, lse_b, mixed = _mix_fwd(outs, lses, y1, z, row(W["b_glu"]), row(W["attn_out_norm"]), row(W["ssm_out_norm"]))
    mo = _matmul(mixed, w_out_f, name="mix_out")
    h1, hn2 = _res_norm(xs, mo, row(W["mix_norm_post"]), row(W["mlp_norm_pre"]), name="res_mix")
    w_up_f = gathered(2, hn2)["w_up"]
    up, act = _matmul(hn2, w_up_f, name="mlp_up", b_shards=N_CHIPS, relu2=True)
    full = gathered(3, act)
    w_down_f = full["w_down"].reshape(-1, D)
    w_pg_f = full["w_ple_gate"].reshape(D, D)
    w_pp_f = full["w_ple_proj"]
    ff = _matmul(act, w_down_f, name="mlp_down")
    h2, hn3 = _res_norm(h1, ff, row(W["mlp_norm_post"]), row(W["ple_norm_pre"]), name="res_mlp")
    gl = _matmul(hn3, w_pg_f, name="ple_gate")
    e = _matmul(ps.astype(BF16), w_pp_f, name="ple_proj", b_shards=N_CHIPS)

    dh3, dgl, de, loss_part, dg_ple_post = _final(h2, gl, e, row(W["ple_norm_post"]), tgt)
    gW = {}
    out_g, out_d, out_m, out_v = {}, {}, {}, {}

    def scatter_start(names, tag):
        parts = [gW[n] if gW[n].ndim == 3 else gW[n].reshape((N_CHIPS, -1, gW[n].shape[1])) for n in names]
        sems, land, src, token = _exchange_start(
            [_place_own(part, gather=False, name="rs_place_" + n) for n, part in zip(names, parts)], parts,
            [list(range(len(names)))], name=f"rs_start_{tag}")
        return (names, sems[0], land, src), token

    def scatter_finish(batch, after, tag):
        names, sems, land, src = batch
        landed = _exchange_wait(land, src, sems, after, name=f"rs_wait_{tag}")
        sums = [_sum_partials(l, name="sum_" + n) for n, l in zip(names, landed)]
        theirs = _swap_with_sibling(sums, name=f"swap_{tag}")
        for n, a, b in zip(names, sums, theirs):
            out_g[n], out_d[n], out_m[n], out_v[n] = _adamw_pair(a, b, W[n], Mo[n], Vo[n], name="adamw_" + n)

    gW["w_ple_proj"] = _matmul(ps.astype(BF16), de, name="d_w_ple_proj", ta=True, out_dtype=BF16, out_shards=N_CHIPS)
    gW["w_ple_gate"] = _matmul(hn3, dgl, name="d_w_ple_gate", ta=True, out_dtype=BF16)
    dhn3 = _matmul(dgl, w_pg_f, name="d_hn3", tb=True)
    dh2, dff, dg_ple_pre, dg_mlp_post = _bwd_res_norm(
        dh3, dhn3, h2, row(W["ple_norm_pre"]), ff, row(W["mlp_norm_post"]), name="bwd_res_mlp")
    gW["w_down"] = _matmul(act, dff, name="d_w_down", ta=True, out_dtype=BF16)
    batch1, token1 = scatter_start(("w_ple_proj", "w_ple_gate", "w_down"), 1)
    dup = _matmul(dff, w_down_f, name="d_up", tb=True, after=token1, relu2_of=up, out_dtype=BF16)
    gW["w_up"] = _matmul(hn2, dup, name="d_w_up", ta=True, out_dtype=BF16, out_shards=N_CHIPS)
    batch2, token2 = scatter_start(("w_up",), 2)
    dhn2 = _matmul(dup, w_up_f, name="d_hn2", tb=True, b_shards=N_CHIPS, after=token2)
    dh1, dmo, dg_mlp_pre, dg_mix_post = _bwd_res_norm(
        dh2, dhn2, h1, row(W["mlp_norm_pre"]), mo, row(W["mix_norm_post"]), name="bwd_res_mix")
    gW["w_out"] = _matmul(mixed, dmo, name="d_w_out", ta=True, out_dtype=BF16)
    dmixed = _matmul(dmo, w_out_f, name="d_mixed", tb=True)
    dattn_b, dd_b, dz, dy2a, dg_attn, dg_ssm, db_glu = _mix_bwd(
        dmixed, attn, y1, z, row(W["b_glu"]), row(W["attn_out_norm"]), row(W["ssm_out_norm"]))
    gW["w_glu"] = _matmul(y2b, dz, name="d_w_glu", ta=True, out_dtype=BF16)
    batch3, token3 = scatter_start(("w_out", "w_glu"), 3)
    dy2b = _matmul(dz, w_glu_f, name="d_y2", tb=True, after=token3)
    du, dar8, dai8, dcr_e, dci_e, dbr_e, dbi_e, dd8 = _ssm_bwd(
        u, y1, dy2a, dy2b, st_r, st_i, ends_r, ends_i, lr_e, li_e, ldt_e, bre_e, bim_e, cre_e, cim_e, d_row)
    scatter_finish(batch1, du, 1)
    dlr_e, dli_e, dldt_e, dbre_e, dbim_e = _ssm_param_bwd(dar8, dai8, dbr_e, dbi_e, lr_e, li_e, ldt_e, bre_e, bim_e)

    dqs, dks, dvs = zip(*[_attn_bwd(qb, da, l, dd_, d, heads)
                          for d, qb, da, l, dd_ in zip(DILATIONS, qkv_b, dattn_b, lse_b, dd_b)])
    dproj = _dproj_join(dqs, dks, dvs, du)
    scatter_finish(batch2, dproj, 2)
    scatter_finish(batch3, dproj, 3)
    gW["w_in"] = _matmul(hn1, dproj, name="d_w_in", ta=True, out_dtype=BF16, out_shards=N_CHIPS)
    batch4, token4 = scatter_start(("w_in",), 4)
    dhn1 = _matmul(dproj, w_in_f, name="d_hn1", tb=True, b_shards=N_CHIPS, after=token4)
    grad_x, dg_mix_pre = _bwd_first(dh1, dhn1, xs, row(W["mix_norm_pre"]))
    scatter_finish(batch4, grad_x, 4)

    small_g = {
        "mix_norm_pre": dg_mix_pre, "lam_re": dlr_e.reshape(G, SSM_P), "lam_im": dli_e.reshape(G, SSM_P),
        "log_dt": dldt_e.reshape(G, SSM_P)[:, 0], "ssm_b_re": _collapse_b(dbre_e), "ssm_b_im": _collapse_b(dbim_e),
        "ssm_c_re": _collapse_c(dcr_e), "ssm_c_im": _collapse_c(dci_e), "ssm_d": dd8.sum(axis=1).reshape(-1),
        "b_glu": db_glu, "attn_out_norm": dg_attn, "ssm_out_norm": dg_ssm, "mix_norm_post": dg_mix_post,
        "mlp_norm_pre": dg_mlp_pre, "mlp_norm_post": dg_mlp_post, "ple_norm_pre": dg_ple_pre,
        "ple_norm_post": dg_ple_post,
    }
    g_pack, layout = _pack([small_g[n].reshape(W[n].shape) for n in SMALL])
    w_pack, _ = _pack([W[n] for n in SMALL])
    m_pack, _ = _pack([Mo[n] for n in SMALL])
    v_pack, _ = _pack([Vo[n] for n in SMALL])
    g_sum = _all_reduce_small(g_pack)
    packed = (g_sum,) + tuple(_adamw_small(g_sum, w_pack, m_pack, v_pack))
    for dst, buf in zip((out_g, out_d, out_m, out_v), packed):
        dst.update(zip(SMALL, _unpack(buf, layout)))

    loss = lax.psum(loss_part[0, 0], ("x", "y", "c"))
    lead = lambda a: a[None]
    return (loss, grad_x[None],
            *[lead(out_g[n]) for n in WEIGHTS], *[lead(out_d[n]) for n in WEIGHTS],
            *[lead(out_m[n]) for n in WEIGHTS], *[lead(out_v[n]) for n in WEIGHTS])
```

```python
import functools
import math

import jax
import jax.numpy as jnp
from jax import lax
from jax.experimental import pallas as pl
from jax.experimental.pallas import tpu as pltpu

F32 = jnp.float32
BF16 = jnp.bfloat16
MESH = pl.DeviceIdType.MESH

RMS_EPS = 1e-6
NEG_INF = -1e30
HEAD_DIM = 128
BLK = 128
DILATIONS = (1, 4, 16)
ATTN_LOOKAHEAD = 3
SSM_C = 16
SSM_P = 64
LANES = 128
GROUPS_PER_BLOCK = LANES // SSM_C
STATE_LANES = GROUPS_PER_BLOCK * SSM_P
SSM_CHUNK = 512
TILE = 8
ADAM_LR, ADAM_B1, ADAM_B2, ADAM_EPS, ADAM_WD, ADAM_STEP = 1e-3, 0.9, 0.999, 1e-8, 0.01, 10
VMEM_LIMIT_BYTES = 56 * 1024 * 1024
N_CHIPS = 4
N_DEV = 8
PACK_ROWS = 256


def _cparams(*sem):
    return pltpu.CompilerParams(dimension_semantics=sem or None, vmem_limit_bytes=VMEM_LIMIT_BYTES)


def _rows(tr, w):
    return pl.BlockSpec((tr, w), lambda i: (i, 0))


def _vec(w):
    return pl.BlockSpec((1, w), lambda i: (0, 0))


def _sigmoid(x):
    return 1.0 / (1.0 + jnp.exp(-x))


def _gelu(x):
    c = math.sqrt(2.0 / math.pi)
    return 0.5 * x * (1.0 + jnp.tanh(c * (x + 0.044715 * x * x * x)))


def _gelu_grad(x):
    c = math.sqrt(2.0 / math.pi)
    th = jnp.tanh(c * (x + 0.044715 * x * x * x))
    return 0.5 * (1.0 + th) + 0.5 * x * (1.0 - th * th) * c * (1.0 + 3.0 * 0.044715 * x * x)


def _rms(x, g):
    r = lax.rsqrt(jnp.mean(x * x, axis=-1, keepdims=True) + RMS_EPS)
    return x * r * g


def _rms_bwd(dy, x, g):
    r = lax.rsqrt(jnp.mean(x * x, axis=-1, keepdims=True) + RMS_EPS)
    n = x * r
    dn = dy * g
    dx = r * (dn - n * jnp.mean(dn * n, axis=-1, keepdims=True))
    return dx, dy * n


def _colsum(a):
    return jnp.sum(a, axis=0, keepdims=True)


def _first(i):
    return i == 0


def _matmul(a, b, *, name, ta=False, tb=False, out_dtype=F32, b_shards=1, out_shards=1, b_cols=None,
            after=None, relu2=False, relu2_of=None, tm=1024, tn=1024, tk=2048):
    if ta:
        K, M = a.shape
    else:
        M, K = a.shape
    if b_shards > 1:
        rows, cols = b.shape[1], b.shape[2] * b_shards
    else:
        rows, cols = b.shape
    N, Kb = (rows, cols) if tb else (cols, rows)
    assert K == Kb, (a.shape, b.shape, ta, tb)
    col0 = 0
    if b_cols is not None:
        assert not tb
        col0, N = b_cols
    tm, tn, tk = min(tm, M), min(tn, N), min(tk, K)
    if b_shards > 1:
        shard_cols = cols // b_shards
        if tb:
            tk = min(tk, shard_cols)
        else:
            tn = min(tn, shard_cols)
    if out_shards > 1:
        tn = min(tn, N // out_shards)
    assert M % tm == 0 and N % tn == 0 and K % tk == 0 and col0 % tn == 0
    nk = K // tk
    j0 = col0 // tn

    a_spec = (pl.BlockSpec((tk, tm), lambda i, j, k: (k, i)) if ta
              else pl.BlockSpec((tm, tk), lambda i, j, k: (i, k)))
    if b_shards > 1:
        if tb:
            per = shard_cols // tk
            b_spec = pl.BlockSpec((None, tn, tk), lambda i, j, k: (k // per, j, k % per))
        else:
            per = shard_cols // tn
            b_spec = pl.BlockSpec((None, tk, tn), lambda i, j, k: ((j + j0) // per, k, (j + j0) % per))
    else:
        b_spec = (pl.BlockSpec((tn, tk), lambda i, j, k: (j, k)) if tb
                  else pl.BlockSpec((tk, tn), lambda i, j, k: (k, j + j0)))
    if out_shards > 1:
        per_o = (N // out_shards) // tn
        out_shape = jax.ShapeDtypeStruct((out_shards, M, N // out_shards), out_dtype)
        out_spec = pl.BlockSpec((None, tm, tn), lambda i, j, k: (j // per_o, i, j % per_o))
    else:
        out_shape = jax.ShapeDtypeStruct((M, N), out_dtype)
        out_spec = pl.BlockSpec((tm, tn), lambda i, j, k: (i, j))
    dims = (((0 if ta else 1,), (1 if tb else 0,)), ((), ()))

    extra, extra_specs = [], []
    if relu2_of is not None:
        assert out_shards == 1 and relu2_of.shape == (M, N)
        extra.append(relu2_of)
        extra_specs.append(pl.BlockSpec((tm, tn), lambda i, j, k: (i, j)))
    if after is not None:
        extra.append(after)
        extra_specs.append(pl.BlockSpec(after.shape, lambda i, j, k: (0, 0)))
    n_in = 2 + len(extra)
    if relu2:
        assert out_shards == 1
        out_shape = (out_shape, jax.ShapeDtypeStruct((M, N), BF16))
        out_spec = (out_spec, out_spec)

    def finish(acc, refs):
        o_ref = refs[n_in]
        if relu2_of is not None:
            acc = acc * (2.0 * jnp.maximum(refs[2][...], 0.0))
        o_ref[...] = acc.astype(o_ref.dtype)
        if relu2:
            r = jnp.maximum(acc, 0.0)
            refs[n_in + 1][...] = (r * r).astype(BF16)

    def body(*refs):
        prod = lax.dot_general(refs[0][...], refs[1][...], dims, preferred_element_type=F32)
        if nk == 1:
            finish(prod, refs)
            return
        acc_ref = refs[-1]
        k = pl.program_id(2)

        @pl.when(k == 0)
        def _():
            acc_ref[...] = prod

        @pl.when(k > 0)
        def _():
            acc_ref[...] += prod

        @pl.when(k == nk - 1)
        def _():
            finish(acc_ref[...], refs)

    return pl.pallas_call(
        body, name=name, out_shape=out_shape, grid=(M // tm, N // tn, nk),
        in_specs=[a_spec, b_spec] + extra_specs, out_specs=out_spec,
        scratch_shapes=[pltpu.VMEM((tm, tn), F32)] if nk > 1 else [],
        compiler_params=_cparams("parallel", "parallel", "arbitrary"),
    )(a, b, *extra)


def _norm_cast(x, g, *, name, tr=256):
    S, D = x.shape
    tr = min(tr, S)

    def body(x_ref, g_ref, o_ref):
        o_ref[...] = _rms(x_ref[...], g_ref[...]).astype(BF16)

    return pl.pallas_call(
        body, name=name, out_shape=jax.ShapeDtypeStruct((S, D), BF16), grid=(S // tr,),
        in_specs=[_rows(tr, D), _vec(D)], out_specs=_rows(tr, D),
        compiler_params=_cparams("parallel"))(x, g)


def _res_norm(res, y, g_post, g_next, *, name, tr=256):
    S, D = res.shape
    tr = min(tr, S)

    def body(res_ref, y_ref, gp_ref, gn_ref, h_ref, hn_ref):
        h = res_ref[...] + _rms(y_ref[...], gp_ref[...])
        h_ref[...] = h
        hn_ref[...] = _rms(h, gn_ref[...]).astype(BF16)

    return pl.pallas_call(
        body, name=name,
        out_shape=(jax.ShapeDtypeStruct((S, D), F32), jax.ShapeDtypeStruct((S, D), BF16)),
        grid=(S // tr,), in_specs=[_rows(tr, D), _rows(tr, D), _vec(D), _vec(D)],
        out_specs=(_rows(tr, D), _rows(tr, D)), compiler_params=_cparams("parallel"))(res, y, g_post, g_next)


def _residue_spec(tr, d, w):
    return pl.BlockSpec((tr // d, d * w), lambda i: (i, 0))


def _residue_shape(S, d, w, dtype):
    return jax.ShapeDtypeStruct((S // d, d * w), dtype)


def _residue_scratch(rows, w):
    return pltpu.VMEM((w // LANES, rows, LANES), F32)


def _fill_strips(scr, val):
    for s in range(scr.shape[0]):
        scr[s] = val[:, s * LANES:(s + 1) * LANES]


def _strips_to_residues(scr, o_ref, d):
    strips, rows, _ = scr.shape
    for r in range(d):
        for s in range(strips):
            col = (r * strips + s) * LANES
            o_ref[:, col:col + LANES] = scr[s, pl.ds(r, rows // d, stride=d), :].astype(o_ref.dtype)


def _to_residues(scr, val, o_ref, d):
    if d == 1:
        o_ref[...] = val.astype(o_ref.dtype)
        return
    _fill_strips(scr, val)
    _strips_to_residues(scr, o_ref, d)


def _from_residues(scr, in_ref, d):
    if d == 1:
        return in_ref[...].astype(F32)
    strips, rows, _ = scr.shape
    for r in range(d):
        for s in range(strips):
            col = (r * strips + s) * LANES
            scr[s, pl.ds(r, rows // d, stride=d), :] = in_ref[:, col:col + LANES].astype(F32)
    return jnp.concatenate([scr[s] for s in range(strips)], axis=1)


def _mix_fwd(os, ls, y1, z, b_glu, g_attn, g_ssm, *, tr=128):
    S, SW = y1.shape
    AW = os[0].shape[1]
    tr = min(tr, S)
    nd = len(DILATIONS)

    def body(*refs):
        o_refs, l_refs = refs[:nd], refs[nd:2 * nd]
        y_ref, z_ref, b_ref, ga_ref, gs_ref, attn_ref = refs[2 * nd:2 * nd + 6]
        lse_refs = refs[2 * nd + 6:3 * nd + 6]
        mixed_ref, scr = refs[3 * nd + 6:]
        ls_ = [_from_residues(scr, l_refs[n], d) for n, d in enumerate(DILATIONS)]
        m = functools.reduce(jnp.maximum, ls_)
        es = [jnp.exp(l - m) for l in ls_]
        tot = functools.reduce(jnp.add, es)
        attn = functools.reduce(jnp.add, [e * _from_residues(scr, o_refs[n], d)
                                          for n, (e, d) in enumerate(zip(es, DILATIONS))]) / tot
        attn_ref[...] = attn
        lse = m + jnp.log(tot)
        for n, d in enumerate(DILATIONS):
            _to_residues(scr, lse, lse_refs[n], d)
        ssm = _gelu(y_ref[...]) * _sigmoid(z_ref[...] + b_ref[...])
        mixed_ref[:, :AW] = _rms(attn, ga_ref[...]).astype(BF16)
        mixed_ref[:, AW:] = _rms(ssm, gs_ref[...]).astype(BF16)

    res_in = [_residue_spec(tr, d, AW) for d in DILATIONS]
    res = pl.pallas_call(
        body, name="mix_fwd",
        out_shape=([jax.ShapeDtypeStruct((S, AW), F32)] + [_residue_shape(S, d, AW, F32) for d in DILATIONS]
                   + [jax.ShapeDtypeStruct((S, AW + SW), BF16)]),
        grid=(S // tr,),
        in_specs=res_in + res_in + [_rows(tr, SW), _rows(tr, SW), _vec(SW), _vec(AW), _vec(SW)],
        out_specs=[_rows(tr, AW)] + res_in + [_rows(tr, AW + SW)],
        scratch_shapes=[_residue_scratch(tr, AW)],
        compiler_params=_cparams("parallel"))(*os, *ls, y1, z, b_glu, g_attn, g_ssm)
    return res[0], res[1:1 + nd], res[1 + nd]


def _final(h2, gl, e, g_post, target, *, tr=128):
    S, D = h2.shape
    tr = min(tr, S)

    def body(h_ref, gl_ref, e_ref, g_ref, t_ref, dh_ref, dgl_ref, de_ref, loss_ref, dg_ref):
        i = pl.program_id(0)
        gate = _sigmoid(gl_ref[...])
        e_ = e_ref[...]
        ge = gate * e_
        g = g_ref[...]
        diff = h_ref[...] + _rms(ge, g) - t_ref[...]
        dh = diff * (1.0 / D)
        dh_ref[...] = dh
        dge, dgrow = _rms_bwd(dh, ge, g)
        dgl_ref[...] = (dge * e_ * gate * (1.0 - gate)).astype(BF16)
        de_ref[...] = (dge * gate).astype(BF16)
        part = _colsum(0.5 * jnp.mean(diff * diff, axis=-1, keepdims=True))

        @pl.when(_first(i))
        def _():
            loss_ref[...] = jnp.zeros_like(loss_ref)
            dg_ref[...] = jnp.zeros_like(dg_ref)

        loss_ref[...] += part + jnp.zeros((1, LANES), F32)
        dg_ref[...] += _colsum(dgrow)

    return pl.pallas_call(
        body, name="final_fwd_bwd",
        out_shape=(jax.ShapeDtypeStruct((S, D), F32), jax.ShapeDtypeStruct((S, D), BF16),
                   jax.ShapeDtypeStruct((S, D), BF16), jax.ShapeDtypeStruct((1, LANES), F32),
                   jax.ShapeDtypeStruct((1, D), F32)),
        grid=(S // tr,),
        in_specs=[_rows(tr, D), _rows(tr, D), _rows(tr, D), _vec(D), _rows(tr, D)],
        out_specs=(_rows(tr, D), _rows(tr, D), _rows(tr, D), _vec(LANES), _vec(D)),
        compiler_params=_cparams("arbitrary"))(h2, gl, e, g_post, target)


def _bwd_res_norm(dh_out, dhn, h, g_next, y, g_post, *, name, tr=128):
    S, D = h.shape
    tr = min(tr, S)

    def body(dho_ref, dhn_ref, h_ref, gn_ref, y_ref, gp_ref, dh_ref, dy_ref, dgn_ref, dgp_ref):
        i = pl.program_id(0)
        dx, dgn_rows = _rms_bwd(dhn_ref[...], h_ref[...], gn_ref[...])
        dh = dho_ref[...] + dx
        dh_ref[...] = dh
        dy, dgp_rows = _rms_bwd(dh, y_ref[...], gp_ref[...])
        dy_ref[...] = dy.astype(BF16)

        @pl.when(_first(i))
        def _():
            dgn_ref[...] = jnp.zeros_like(dgn_ref)
            dgp_ref[...] = jnp.zeros_like(dgp_ref)

        dgn_ref[...] += _colsum(dgn_rows)
        dgp_ref[...] += _colsum(dgp_rows)

    return pl.pallas_call(
        body, name=name,
        out_shape=(jax.ShapeDtypeStruct((S, D), F32), jax.ShapeDtypeStruct((S, D), BF16),
                   jax.ShapeDtypeStruct((1, D), F32), jax.ShapeDtypeStruct((1, D), F32)),
        grid=(S // tr,),
        in_specs=[_rows(tr, D), _rows(tr, D), _rows(tr, D), _vec(D), _rows(tr, D), _vec(D)],
        out_specs=(_rows(tr, D), _rows(tr, D), _vec(D), _vec(D)),
        compiler_params=_cparams("arbitrary"))(dh_out, dhn, h, g_next, y, g_post)


def _bwd_first(dh1, dhn1, x, g1, *, tr=256):
    S, D = x.shape
    tr = min(tr, S)

    def body(dh_ref, dhn_ref, x_ref, g_ref, dx_ref, dg_ref):
        i = pl.program_id(0)
        dx, dg_rows = _rms_bwd(dhn_ref[...], x_ref[...], g_ref[...])
        dx_ref[...] = dh_ref[...] + dx

        @pl.when(_first(i))
        def _():
            dg_ref[...] = jnp.zeros_like(dg_ref)

        dg_ref[...] += _colsum(dg_rows)

    return pl.pallas_call(
        body, name="bwd_first",
        out_shape=(jax.ShapeDtypeStruct((S, D), F32), jax.ShapeDtypeStruct((1, D), F32)),
        grid=(S // tr,), in_specs=[_rows(tr, D), _rows(tr, D), _rows(tr, D), _vec(D)],
        out_specs=(_rows(tr, D), _vec(D)), compiler_params=_cparams("arbitrary"))(dh1, dhn1, x, g1)


def _mix_bwd(dmixed, attn, y1, z, b_glu, g_attn, g_ssm, *, tr=256):
    S, AW = attn.shape
    SW = y1.shape[1]
    tr = min(tr, S)
    heads = AW // HEAD_DIM
    nd = len(DILATIONS)

    def body(*refs):
        dm_ref, a_ref, y_ref, z_ref, b_ref, ga_ref, gs_ref = refs[:7]
        da_refs, dd_refs = refs[7:7 + nd], refs[7 + nd:7 + 2 * nd]
        dz_ref, dy2_ref, dga_ref, dgs_ref, db_ref, scr, dd_scr = refs[7 + 2 * nd:]
        i = pl.program_id(0)
        attn_ = a_ref[...]
        dattn, dga_rows = _rms_bwd(dm_ref[:, :AW], attn_, ga_ref[...])
        prod = dattn * attn_
        for h in range(heads):
            sl = slice(h * HEAD_DIM, (h + 1) * HEAD_DIM)
            dd_scr[:, sl] = jnp.broadcast_to(jnp.sum(prod[:, sl], axis=-1, keepdims=True), (tr, HEAD_DIM))
        for n, d in enumerate(DILATIONS):
            _to_residues(scr, dattn, da_refs[n], d)
            _to_residues(scr, dd_scr[...], dd_refs[n], d)
        y2 = _gelu(y_ref[...])
        gate = _sigmoid(z_ref[...] + b_ref[...])
        dssm, dgs_rows = _rms_bwd(dm_ref[:, AW:], y2 * gate, gs_ref[...])
        dz = dssm * y2 * gate * (1.0 - gate)
        dz_ref[...] = dz.astype(BF16)
        dy2_ref[...] = dssm * gate

        @pl.when(_first(i))
        def _():
            dga_ref[...] = jnp.zeros_like(dga_ref)
            dgs_ref[...] = jnp.zeros_like(dgs_ref)
            db_ref[...] = jnp.zeros_like(db_ref)

        dga_ref[...] += _colsum(dga_rows)
        dgs_ref[...] += _colsum(dgs_rows)
        db_ref[...] += _colsum(dz)

    res_out = [_residue_spec(tr, d, AW) for d in DILATIONS]
    res = pl.pallas_call(
        body, name="mix_bwd",
        out_shape=([_residue_shape(S, d, AW, BF16) for d in DILATIONS]
                   + [_residue_shape(S, d, AW, F32) for d in DILATIONS]
                   + [jax.ShapeDtypeStruct((S, SW), BF16), jax.ShapeDtypeStruct((S, SW), F32),
                      jax.ShapeDtypeStruct((1, AW), F32), jax.ShapeDtypeStruct((1, SW), F32),
                      jax.ShapeDtypeStruct((1, SW), F32)]),
        grid=(S // tr,),
        in_specs=[_rows(tr, AW + SW), _rows(tr, AW), _rows(tr, SW), _rows(tr, SW), _vec(SW), _vec(AW), _vec(SW)],
        out_specs=res_out + res_out + [_rows(tr, SW), _rows(tr, SW), _vec(AW), _vec(SW), _vec(SW)],
        scratch_shapes=[_residue_scratch(tr, AW), pltpu.VMEM((tr, AW), F32)],
        compiler_params=_cparams("arbitrary"))(dmixed, attn, y1, z, b_glu, g_attn, g_ssm)
    return (res[:nd], res[nd:2 * nd]) + tuple(res[2 * nd:])


def _attn_mask2(i):
    row = lax.broadcasted_iota(jnp.int32, (BLK, 2 * BLK), 0)
    col = lax.broadcasted_iota(jnp.int32, (BLK, 2 * BLK), 1)
    return jnp.logical_and(col >= row, jnp.logical_and(col <= row + BLK, jnp.logical_or(col >= BLK, i > 0)))


_NT = (((1,), (1,)), ((), ()))
_TN = (((0,), (0,)), ((), ()))


def _attn_in_specs(width, block_of):
    def at(part, prev):
        def index(r, i):
            blk = block_of(i)
            return (part, jnp.maximum(blk - 1, 0) if prev else blk, r)
        return pl.BlockSpec((None, BLK, width), index)
    return [at(0, False), at(1, False), at(1, True), at(2, False), at(2, True)]


def _proj_qkv(hn, w_in_f, *, tm=1024):
    S, D = hn.shape
    AW = w_in_f.shape[2]
    tm = min(tm, S)

    def body(a_ref, b_ref, *rest):
        o_refs, scr = rest[:-1], rest[-1]
        prod = jnp.dot(a_ref[...], b_ref[...], preferred_element_type=F32)
        _fill_strips(scr, prod)
        for o_ref, d in zip(o_refs, DILATIONS):
            if d == 1:
                o_ref[...] = prod.astype(BF16)
            else:
                _strips_to_residues(scr, o_ref, d)

    return pl.pallas_call(
        body, name="proj_qkv",
        out_shape=[jax.ShapeDtypeStruct((3, S // d, d * AW), BF16) for d in DILATIONS], grid=(S // tm, 3),
        in_specs=[pl.BlockSpec((tm, D), lambda i, j: (i, 0)), pl.BlockSpec((None, D, AW), lambda i, j: (j, 0, 0))],
        out_specs=[pl.BlockSpec((None, tm // d, d * AW), lambda i, j: (j, i, 0)) for d in DILATIONS],
        scratch_shapes=[_residue_scratch(tm, AW)],
        compiler_params=_cparams("parallel", "parallel"))(hn, w_in_f)


def _attn_fwd(qkv, d, heads):
    M = qkv.shape[1]
    nb = M // BLK
    width = heads * HEAD_DIM
    scale = 1.0 / math.sqrt(HEAD_DIM)

    def body(q_ref, kc_ref, kp_ref, vc_ref, vp_ref, o_ref, l_ref):
        mask = _attn_mask2(pl.program_id(1))
        ones = jnp.ones((2 * BLK, HEAD_DIM), BF16)

        def scores(h):
            sl = slice(h * HEAD_DIM, (h + 1) * HEAD_DIM)
            k2 = jnp.concatenate([kp_ref[:, sl], kc_ref[:, sl]], axis=0)
            return lax.dot_general(q_ref[:, sl], k2, _NT, preferred_element_type=F32)

        ahead = [scores(h) for h in range(min(ATTN_LOOKAHEAD, heads))]
        for h in range(heads):
            sl = slice(h * HEAD_DIM, (h + 1) * HEAD_DIM)
            s = jnp.where(mask, ahead.pop(0) * scale, NEG_INF)
            if h + ATTN_LOOKAHEAD < heads:
                ahead.append(scores(h + ATTN_LOOKAHEAD))
            v2 = jnp.concatenate([vp_ref[:, sl], vc_ref[:, sl]], axis=0)
            m = jnp.max(jnp.maximum(s[:, :BLK], s[:, BLK:]), axis=-1, keepdims=True)
            p = jnp.exp(s - m).astype(BF16)
            tot = jnp.dot(p, ones, preferred_element_type=F32)
            o_ref[:, sl] = jnp.dot(p, v2, preferred_element_type=F32) / tot
            l_ref[:, sl] = m + jnp.log(tot)

    out_spec = pl.BlockSpec((BLK, width), lambda r, i: (i, r))
    shape = jax.ShapeDtypeStruct((M, d * width), F32)
    return pl.pallas_call(
        body, name=f"attn_fwd_d{d}", out_shape=(shape, shape), grid=(d, nb),
        in_specs=_attn_in_specs(width, lambda i: i), out_specs=(out_spec, out_spec),
        compiler_params=_cparams("parallel", "parallel"))(qkv, qkv, qkv, qkv, qkv)


def _attn_bwd(qkv, dattn, lse, dd, d, heads):
    M = qkv.shape[1]
    nb = M // BLK
    width = heads * HEAD_DIM
    scale = 1.0 / math.sqrt(HEAD_DIM)

    def block_of(i):
        return nb - 1 - i

    def body(q_ref, kc_ref, kp_ref, vc_ref, vp_ref, da_ref, l_ref, dd_ref,
             dq_ref, dk_ref, dv_ref, dk_carry, dv_carry):
        @pl.when(pl.program_id(1) == 0)
        def _():
            dk_carry[...] = jnp.zeros_like(dk_carry)
            dv_carry[...] = jnp.zeros_like(dv_carry)

        mask = _attn_mask2(block_of(pl.program_id(1)))

        def products(h):
            sl = slice(h * HEAD_DIM, (h + 1) * HEAD_DIM)
            k2 = jnp.concatenate([kp_ref[:, sl], kc_ref[:, sl]], axis=0)
            v2 = jnp.concatenate([vp_ref[:, sl], vc_ref[:, sl]], axis=0)
            return (lax.dot_general(q_ref[:, sl], k2, _NT, preferred_element_type=F32),
                    lax.dot_general(da_ref[:, sl], v2, _NT, preferred_element_type=F32), k2)

        ahead = [products(h) for h in range(min(ATTN_LOOKAHEAD, heads))]
        for h in range(heads):
            sl = slice(h * HEAD_DIM, (h + 1) * HEAD_DIM)
            qk, dp, k2 = ahead.pop(0)
            if h + ATTN_LOOKAHEAD < heads:
                ahead.append(products(h + ATTN_LOOKAHEAD))
            q, da = q_ref[:, sl], da_ref[:, sl]
            lse_ = jnp.concatenate([l_ref[:, sl], l_ref[:, sl]], axis=1)
            dd_ = jnp.concatenate([dd_ref[:, sl], dd_ref[:, sl]], axis=1)
            p = jnp.where(mask, jnp.exp(jnp.where(mask, qk * scale, NEG_INF) - lse_), 0.0)
            ds = (p * (dp - dd_) * scale).astype(BF16)
            dq_ref[:, sl] = jnp.dot(ds, k2, preferred_element_type=F32).astype(BF16)
            dk2 = lax.dot_general(ds, q, _TN, preferred_element_type=F32)
            dv2 = lax.dot_general(p.astype(BF16), da, _TN, preferred_element_type=F32)
            dk_ref[:, sl] = (dk2[BLK:] + dk_carry[:, sl]).astype(BF16)
            dv_ref[:, sl] = (dv2[BLK:] + dv_carry[:, sl]).astype(BF16)
            dk_carry[:, sl] = dk2[:BLK]
            dv_carry[:, sl] = dv2[:BLK]

    blk = pl.BlockSpec((BLK, width), lambda r, i: (block_of(i), r))
    shape = jax.ShapeDtypeStruct((M, d * width), BF16)
    return pl.pallas_call(
        body, name=f"attn_bwd_d{d}", out_shape=(shape,) * 3, grid=(d, nb),
        in_specs=_attn_in_specs(width, block_of) + [blk, blk, blk], out_specs=(blk,) * 3,
        scratch_shapes=[pltpu.VMEM((BLK, width), F32), pltpu.VMEM((BLK, width), F32)],
        compiler_params=_cparams("arbitrary", "arbitrary"))(qkv, qkv, qkv, qkv, qkv, dattn, lse, dd)


def _dproj_join(dqs, dks, dvs, du, *, tr=256):
    S, SW = du.shape
    AW = dqs[0].shape[1]
    tr = min(tr, S)
    nd = len(DILATIONS)

    def body(*refs):
        du_ref, out_ref, scr = refs[3 * nd:]
        for part in range(3):
            total = functools.reduce(jnp.add, [_from_residues(scr, refs[part * nd + n], d)
                                               for n, d in enumerate(DILATIONS)])
            out_ref[:, part * AW:(part + 1) * AW] = total.astype(BF16)
        out_ref[:, 3 * AW:] = du_ref[...].astype(BF16)

    return pl.pallas_call(
        body, name="dproj_join", out_shape=jax.ShapeDtypeStruct((S, 3 * AW + SW), BF16), grid=(S // tr,),
        in_specs=[_residue_spec(tr, d, AW) for d in DILATIONS] * 3 + [_rows(tr, SW)],
        out_specs=_rows(tr, 3 * AW + SW), scratch_shapes=[_residue_scratch(tr, AW)],
        compiler_params=_cparams("parallel"))(*dqs, *dks, *dvs, du)


def _ssm_disc(lr, li, ldt):
    dt = jnp.exp(ldt)
    mag = jnp.exp(lr * dt)
    ar = mag * jnp.cos(li * dt)
    ai = mag * jnp.sin(li * dt)
    nr = ar - 1.0
    den = lr * lr + li * li
    return ar, ai, (nr * lr + ai * li) / den, (ai * lr - nr * li) / den


def _ssm_tile_powers(lr, li, ldt, reverse):
    t = lax.broadcasted_iota(jnp.int32, (TILE, 1), 0)
    n = (TILE - t if reverse else t + 1).astype(F32)
    dt = jnp.exp(ldt)
    mag = jnp.exp(n * (lr * dt))
    ang = n * (li * dt)
    return mag * jnp.cos(ang), mag * jnp.sin(ang) * (-1.0 if reverse else 1.0)


def _cmul(ar, ai, br, bi):
    return ar * br - ai * bi, ar * bi + ai * br


def _scan(xr, xi, ar, ai, pr, pi, cr, ci, reverse):
    T = xr.shape[0]
    sub = lax.broadcasted_iota(jnp.int32, xr.shape, 0) & (TILE - 1)
    sh = 1
    while sh < TILE:
        if reverse:
            keep = sub < TILE - sh
            sr, si = pltpu.roll(xr, T - sh, 0), pltpu.roll(xi, T - sh, 0)
        else:
            keep = sub >= sh
            sr, si = pltpu.roll(xr, sh, 0), pltpu.roll(xi, sh, 0)
        sr, si = jnp.where(keep, sr, 0.0), jnp.where(keep, si, 0.0)
        qr, qi = _cmul(ar, ai, sr, si)
        xr, xi = xr + qr, xi + qi
        ar, ai = _cmul(ar, ai, ar, ai)
        sh *= 2
    n = T // TILE
    out_r, out_i = [None] * n, [None] * n
    edge = 0 if reverse else TILE - 1
    for j in (reversed(range(n)) if reverse else range(n)):
        er, ei = _cmul(pr, pi, cr, ci)
        sr, si = xr[j * TILE:(j + 1) * TILE] + er, xi[j * TILE:(j + 1) * TILE] + ei
        out_r[j], out_i[j] = sr, si
        cr, ci = sr[edge:edge + 1], si[edge:edge + 1]
    return jnp.concatenate(out_r, axis=0), jnp.concatenate(out_i, axis=0), cr, ci


def _ssm_specs(T, nch, rev):
    def t_of(c):
        return nch - 1 - c if rev else c
    tok = pl.BlockSpec((T, LANES), lambda j, c: (t_of(c), j))
    par = pl.BlockSpec((None, 1, STATE_LANES), lambda j, c: (j, 0, 0))
    bmat = pl.BlockSpec((None, LANES, STATE_LANES), lambda j, c: (j, 0, 0))
    cmat = pl.BlockSpec((None, STATE_LANES, LANES), lambda j, c: (j, 0, 0))
    dvec = pl.BlockSpec((1, LANES), lambda j, c: (0, j))
    return tok, par, bmat, cmat, dvec


def _ssm_fwd(u, lr_e, li_e, ldt_e, bre_e, bim_e, cre_e, cim_e, d_skip):
    S, SW = u.shape
    T = min(SSM_CHUNK, S)
    nch, nbk = S // T, SW // LANES
    tok, par, bmat, cmat, dvec = _ssm_specs(T, nch, False)
    state_spec = pl.BlockSpec((T, STATE_LANES), lambda j, c: (c, j))
    carry_spec = pl.BlockSpec((None, 1, STATE_LANES), lambda j, c: (c, 0, j))

    def body(u_ref, lr_ref, li_ref, ldt_ref, bre_ref, bim_ref, cre_ref, cim_ref, d_ref,
             y_ref, y2_ref, sr_ref, si_ref, er_ref, ei_ref, bbr, bbi, a_scr, pw, carry):
        c = pl.program_id(1)

        @pl.when(c == 0)
        def _():
            lr, li, ldt = lr_ref[...], li_ref[...], ldt_ref[...]
            ar, ai, kr, ki = _ssm_disc(lr, li, ldt)
            a_scr[0], a_scr[1] = ar, ai
            bbr[...] = (kr * bre_ref[...] - ki * bim_ref[...]).astype(BF16)
            bbi[...] = (kr * bim_ref[...] + ki * bre_ref[...]).astype(BF16)
            pw[0], pw[1] = _ssm_tile_powers(lr, li, ldt, False)
            carry[...] = jnp.zeros_like(carry)

        u_ = u_ref[...]
        ub = u_.astype(BF16)
        sr, si, cr, ci = _scan(jnp.dot(ub, bbr[...], preferred_element_type=F32),
                               jnp.dot(ub, bbi[...], preferred_element_type=F32),
                               a_scr[0], a_scr[1], pw[0], pw[1], carry[0], carry[1], False)
        carry[0], carry[1] = cr, ci
        er_ref[...], ei_ref[...] = cr, ci
        sr_ref[...], si_ref[...] = sr, si
        y0 = (jnp.dot(sr.astype(BF16), cre_ref[...].astype(BF16), preferred_element_type=F32)
              - jnp.dot(si.astype(BF16), cim_ref[...].astype(BF16), preferred_element_type=F32))
        y1 = y0 + d_ref[...] * u_
        y_ref[...] = y1
        y2_ref[...] = _gelu(y1).astype(BF16)

    states = jax.ShapeDtypeStruct((S, nbk * STATE_LANES), F32)
    ends = jax.ShapeDtypeStruct((nch, 1, nbk * STATE_LANES), F32)
    return pl.pallas_call(
        body, name="ssm_fwd",
        out_shape=(jax.ShapeDtypeStruct((S, SW), F32), jax.ShapeDtypeStruct((S, SW), BF16), states, states, ends, ends),
        grid=(nbk, nch), in_specs=[tok, par, par, par, bmat, bmat, cmat, cmat, dvec],
        out_specs=(tok, tok, state_spec, state_spec, carry_spec, carry_spec),
        scratch_shapes=[pltpu.VMEM((LANES, STATE_LANES), BF16), pltpu.VMEM((LANES, STATE_LANES), BF16),
                        pltpu.VMEM((2, 1, STATE_LANES), F32), pltpu.VMEM((2, TILE, STATE_LANES), F32),
                        pltpu.VMEM((2, 1, STATE_LANES), F32)],
        compiler_params=_cparams("arbitrary", "arbitrary"),
    )(u, lr_e, li_e, ldt_e, bre_e, bim_e, cre_e, cim_e, d_skip)


def _ssm_bwd(u, y1, dy2a, dy2b, st_r, st_i, ends_r, ends_i, lr_e, li_e, ldt_e, bre_e, bim_e, cre_e, cim_e, d_skip):
    S, SW = u.shape
    T = min(SSM_CHUNK, S)
    nch, nbk = S // T, SW // LANES
    tok, par, bmat, cmat, dvec = _ssm_specs(T, nch, True)
    state_spec = pl.BlockSpec((T, STATE_LANES), lambda j, c: (nch - 1 - c, j))
    prev_spec = pl.BlockSpec((None, 1, STATE_LANES), lambda j, c: (jnp.maximum(nch - 2 - c, 0), 0, j))
    acc8 = pl.BlockSpec((None, 8, STATE_LANES), lambda j, c: (j, 0, 0))
    dd8 = pl.BlockSpec((None, 8, LANES), lambda j, c: (j, 0, 0))

    def body(u_ref, y_ref, da_ref, db_ref, sr_ref, si_ref, pr_ref, pi_ref, lr_ref, li_ref, ldt_ref,
             bre_ref, bim_ref, cre_ref, cim_ref, d_ref,
             du_ref, dar_ref, dai_ref, dcr_ref, dci_ref, dbr_ref, dbi_ref, ddk_ref,
             bbr, bbi, a_scr, pw, carry):
        c = pl.program_id(1)

        @pl.when(c == 0)
        def _():
            lr, li, ldt = lr_ref[...], li_ref[...], ldt_ref[...]
            ar, ai, kr, ki = _ssm_disc(lr, li, ldt)
            a_scr[0], a_scr[1] = ar, -ai
            bbr[...] = (kr * bre_ref[...] - ki * bim_ref[...]).astype(BF16)
            bbi[...] = (kr * bim_ref[...] + ki * bre_ref[...]).astype(BF16)
            pw[0], pw[1] = _ssm_tile_powers(lr, li, ldt, True)
            carry[...] = jnp.zeros_like(carry)
            for ref in (dar_ref, dai_ref, dcr_ref, dci_ref, dbr_ref, dbi_ref, ddk_ref):
                ref[...] = jnp.zeros_like(ref)

        u_ = u_ref[...]
        ub = u_.astype(BF16)
        dy1 = (da_ref[...] + db_ref[...]) * _gelu_grad(y_ref[...])
        dyb = dy1.astype(BF16)

        sr, si = sr_ref[...], si_ref[...]
        has_prev = c < nch - 1
        s0r = jnp.where(has_prev, pr_ref[...], 0.0)
        s0i = jnp.where(has_prev, pi_ref[...], 0.0)

        cre_b, cim_b = cre_ref[...].astype(BF16), cim_ref[...].astype(BF16)
        gr, gi, cr, ci = _scan(lax.dot_general(dyb, cre_b, _NT, preferred_element_type=F32),
                               -lax.dot_general(dyb, cim_b, _NT, preferred_element_type=F32),
                               a_scr[0], a_scr[1], pw[0], pw[1], carry[0], carry[1], True)
        carry[0], carry[1] = cr, ci

        row = lax.broadcasted_iota(jnp.int32, (T, STATE_LANES), 0)
        spr = jnp.where(row == 0, s0r, pltpu.roll(sr, 1, 0))
        spi = jnp.where(row == 0, s0i, pltpu.roll(si, 1, 0))

        def fold(a):
            return jnp.sum(a.reshape(T // 8, 8, a.shape[-1]), axis=0)

        dar_ref[...] += fold(gr * spr + gi * spi)
        dai_ref[...] += fold(gi * spr - gr * spi)
        srb, sib, grb, gib = sr.astype(BF16), si.astype(BF16), gr.astype(BF16), gi.astype(BF16)
        dcr_ref[...] += lax.dot_general(srb, dyb, _TN, preferred_element_type=F32)
        dci_ref[...] -= lax.dot_general(sib, dyb, _TN, preferred_element_type=F32)
        dbr_ref[...] += lax.dot_general(ub, grb, _TN, preferred_element_type=F32)
        dbi_ref[...] += lax.dot_general(ub, gib, _TN, preferred_element_type=F32)
        du_ref[...] = (lax.dot_general(grb, bbr[...], _NT, preferred_element_type=F32)
                       + lax.dot_general(gib, bbi[...], _NT, preferred_element_type=F32)
                       + dy1 * d_ref[...])
        ddk_ref[...] += fold(dy1 * u_)

    return pl.pallas_call(
        body, name="ssm_bwd",
        out_shape=(jax.ShapeDtypeStruct((S, SW), F32),
                   jax.ShapeDtypeStruct((nbk, 8, STATE_LANES), F32), jax.ShapeDtypeStruct((nbk, 8, STATE_LANES), F32),
                   jax.ShapeDtypeStruct((nbk, STATE_LANES, LANES), F32), jax.ShapeDtypeStruct((nbk, STATE_LANES, LANES), F32),
                   jax.ShapeDtypeStruct((nbk, LANES, STATE_LANES), F32), jax.ShapeDtypeStruct((nbk, LANES, STATE_LANES), F32),
                   jax.ShapeDtypeStruct((nbk, 8, LANES), F32)),
        grid=(nbk, nch),
        in_specs=[tok, tok, tok, tok, state_spec, state_spec, prev_spec, prev_spec, par, par, par,
                  bmat, bmat, cmat, cmat, dvec],
        out_specs=(tok, acc8, acc8, cmat, cmat, bmat, bmat, dd8),
        scratch_shapes=[pltpu.VMEM((LANES, STATE_LANES), BF16), pltpu.VMEM((LANES, STATE_LANES), BF16),
                        pltpu.VMEM((2, 1, STATE_LANES), F32), pltpu.VMEM((2, TILE, STATE_LANES), F32),
                        pltpu.VMEM((2, 1, STATE_LANES), F32)],
        compiler_params=_cparams("arbitrary", "arbitrary"),
    )(u, y1, dy2a, dy2b, st_r, st_i, ends_r, ends_i, lr_e, li_e, ldt_e, bre_e, bim_e, cre_e, cim_e, d_skip)


def _ssm_param_bwd(dar8, dai8, dbr_e, dbi_e, lr_e, li_e, ldt_e, bre_e, bim_e):
    nbk = lr_e.shape[0]
    par = pl.BlockSpec((None, 1, STATE_LANES), lambda j: (j, 0, 0))
    acc8 = pl.BlockSpec((None, 8, STATE_LANES), lambda j: (j, 0, 0))
    bmat = pl.BlockSpec((None, LANES, STATE_LANES), lambda j: (j, 0, 0))

    def body(dar_ref, dai_ref, dbr_ref, dbi_ref, lr_ref, li_ref, ldt_ref, bre_ref, bim_ref,
             dlr_ref, dli_ref, dldt_ref, dbre_ref, dbim_ref):
        lr, li, ldt = lr_ref[...], li_ref[...], ldt_ref[...]
        (ar, ai, kr, ki), vjp = jax.vjp(_ssm_disc, lr, li, ldt)
        dbr, dbi, bre, bim = dbr_ref[...], dbi_ref[...], bre_ref[...], bim_ref[...]
        dbre_ref[...] = kr * dbr + ki * dbi
        dbim_ref[...] = kr * dbi - ki * dbr
        dkr = _colsum(dbr * bre + dbi * bim)
        dki = _colsum(dbi * bre - dbr * bim)
        dlr, dli, dldt = vjp((_colsum(dar_ref[...]), _colsum(dai_ref[...]), dkr, dki))
        dlr_ref[...] = dlr
        dli_ref[...] = dli
        tot = jnp.broadcast_to(dldt, (8, STATE_LANES))
        sh = 1
        while sh < SSM_P:
            tot = tot + pltpu.roll(tot, STATE_LANES - sh, 1)
            sh *= 2
        dldt_ref[...] = tot[:1]

    vec = jax.ShapeDtypeStruct((nbk, 1, STATE_LANES), F32)
    mat = jax.ShapeDtypeStruct((nbk, LANES, STATE_LANES), F32)
    return pl.pallas_call(
        body, name="ssm_param_bwd", out_shape=(vec, vec, vec, mat, mat), grid=(nbk,),
        in_specs=[acc8, acc8, bmat, bmat, par, par, par, bmat, bmat],
        out_specs=(par, par, par, bmat, bmat), compiler_params=_cparams("parallel"),
    )(dar8, dai8, dbr_e, dbi_e, lr_e, li_e, ldt_e, bre_e, bim_e)


def _expand_b(b):
    G = b.shape[0]
    bt = b.transpose(0, 2, 1).reshape(G // GROUPS_PER_BLOCK, GROUPS_PER_BLOCK, SSM_C, SSM_P)
    eye = jnp.eye(GROUPS_PER_BLOCK, dtype=b.dtype)
    return (bt[:, :, :, None, :] * eye[None, :, None, :, None]).reshape(G // GROUPS_PER_BLOCK, LANES, STATE_LANES)


def _collapse_b(be):
    nbk = be.shape[0]
    eye = jnp.eye(GROUPS_PER_BLOCK, dtype=be.dtype)
    d5 = be.reshape(nbk, GROUPS_PER_BLOCK, SSM_C, GROUPS_PER_BLOCK, SSM_P)
    d4 = (d5 * eye[None, :, None, :, None]).sum(axis=3)
    return d4.transpose(0, 1, 3, 2).reshape(nbk * GROUPS_PER_BLOCK, SSM_P, SSM_C)


def _expand_c(cm):
    G = cm.shape[0]
    ct = cm.transpose(0, 2, 1).reshape(G // GROUPS_PER_BLOCK, GROUPS_PER_BLOCK, SSM_P, SSM_C)
    eye = jnp.eye(GROUPS_PER_BLOCK, dtype=cm.dtype)
    return (ct[:, :, :, None, :] * eye[None, :, None, :, None]).reshape(G // GROUPS_PER_BLOCK, STATE_LANES, LANES)


def _collapse_c(ce):
    nbk = ce.shape[0]
    eye = jnp.eye(GROUPS_PER_BLOCK, dtype=ce.dtype)
    d5 = ce.reshape(nbk, GROUPS_PER_BLOCK, SSM_P, GROUPS_PER_BLOCK, SSM_C)
    d4 = (d5 * eye[None, :, None, :, None]).sum(axis=3)
    return d4.transpose(0, 1, 3, 2).reshape(nbk * GROUPS_PER_BLOCK, SSM_C, SSM_P)


def _place():
    x, y, c = lax.axis_index("x"), lax.axis_index("y"), lax.axis_index("c")
    return x, y, c


def _other_chips(x, y):
    return [(1 - x, y), (x, 1 - y), (1 - x, 1 - y)]


_ANY = pl.BlockSpec(memory_space=pl.ANY)


_HBM = pl.BlockSpec(memory_space=pltpu.HBM)
_SEM = pl.BlockSpec(memory_space=pltpu.SEMAPHORE)
_EFFECT = pltpu.SideEffectType.DATAFLOW_SIDE_EFFECTING
_TOKEN = jax.ShapeDtypeStruct((8, LANES), F32)


def _hbm(a):
    return pltpu.with_memory_space_constraint(a, pltpu.HBM)


def _place_own(src, *, gather, name, tr=512):
    R, C = src.shape[-2:]
    tr = min(tr, R)
    x, y, _ = _place()
    me = (2 * x + y).astype(jnp.int32).reshape(1)

    def body(me_ref, s_ref, o_ref):
        o_ref[...] = s_ref[...].astype(BF16)

    own = pl.BlockSpec((None, tr, C), lambda i, me_ref: (me_ref[0], i, 0))
    grid_spec = pltpu.PrefetchScalarGridSpec(
        num_scalar_prefetch=1, grid=(R // tr,),
        in_specs=[pl.BlockSpec((tr, C), lambda i, me_ref: (i, 0)) if gather else own], out_specs=own)
    return pl.pallas_call(
        body, name=name, grid_spec=grid_spec, out_shape=jax.ShapeDtypeStruct((N_CHIPS, R, C), BF16),
        compiler_params=_cparams("parallel"))(me, src)


def _exchange_copy(src_slot, land_slot, send, recv, k, j, peer, c):
    return pltpu.make_async_remote_copy(
        src_ref=src_slot, dst_ref=land_slot, send_sem=send.at[3 * k + j], recv_sem=recv.at[3 * k + j],
        device_id=(peer[0], peer[1], c), device_id_type=MESH)


def _exchange_start(lands, srcs, groups, *, name):
    n, ng = len(lands), len(groups)
    bufs = list(lands) + list(srcs)
    nb = len(bufs)

    def body(*refs):
        lnd, src, sems = refs[:n], refs[n:nb], refs[nb:nb + 2 * ng]
        token = refs[2 * nb + 2 * ng]
        x, y, c = _place()
        me = 2 * x + y
        for gi, group in enumerate(groups):
            for k, w in enumerate(group):
                for j, peer in enumerate(_other_chips(x, y)):
                    sent = src[w].at[2 * peer[0] + peer[1]] if src else lnd[w].at[me]
                    _exchange_copy(sent, lnd[w].at[me], sems[2 * gi], sems[2 * gi + 1], k, j, peer, c).start()
        token[...] = jnp.zeros_like(token)

    sem_shapes = [pltpu.SemaphoreType.DMA((3 * len(g),)) for g in groups for _ in range(2)]
    res = pl.pallas_call(
        body, name=name,
        out_shape=sem_shapes + [pltpu.HBM(a.shape, a.dtype) for a in bufs] + [_TOKEN],
        in_specs=[_HBM] * nb,
        out_specs=[_SEM] * (2 * ng) + [_HBM] * nb + [pl.BlockSpec(memory_space=pltpu.VMEM)],
        input_output_aliases={i: 2 * ng + i for i in range(nb)},
        compiler_params=pltpu.CompilerParams(has_side_effects=_EFFECT),
    )(*[_hbm(a) for a in bufs])
    sems = [(res[2 * gi], res[2 * gi + 1]) for gi in range(ng)]
    return sems, res[2 * ng:2 * ng + n], res[2 * ng + n:2 * ng + nb], res[-1]


def _exchange_wait(lands, srcs, sems, after, *, name):
    n = len(lands)
    bufs = list(lands) + list(srcs)
    nb = len(bufs)
    send_sems, recv_sems = sems

    def body(*refs):
        lnd, src, send, recv = refs[:n], refs[n:nb], refs[nb], refs[nb + 1]
        x, y, c = _place()
        for k in range(n):
            for j, peer in enumerate(_other_chips(x, y)):
                slot = 2 * peer[0] + peer[1]
                copy = _exchange_copy((src[k] if src else lnd[k]).at[slot], lnd[k].at[slot], send, recv, k, j, peer, c)
                copy.wait_send()
                copy.wait_recv()

    res = pl.pallas_call(
        body, name=name, out_shape=[pltpu.HBM(a.shape, a.dtype) for a in bufs],
        in_specs=[_HBM] * nb + [_SEM, _SEM, _ANY], out_specs=[_HBM] * nb,
        input_output_aliases={i: i for i in range(nb)},
        compiler_params=pltpu.CompilerParams(has_side_effects=_EFFECT),
    )(*bufs, send_sems, recv_sems, after)
    return res[:n]


def _sum_partials(land, *, name, tr=256):
    _, R, C = land.shape
    tr = min(tr, R)

    def body(l_ref, o_ref):
        acc = l_ref[0].astype(F32)
        for k in range(1, N_CHIPS):
            acc = acc + l_ref[k].astype(F32)
        o_ref[...] = acc

    return pl.pallas_call(
        body, name=name, out_shape=jax.ShapeDtypeStruct((R, C), F32), grid=(R // tr,),
        in_specs=[pl.BlockSpec((N_CHIPS, tr, C), lambda i: (0, i, 0))], out_specs=_rows(tr, C),
        compiler_params=_cparams("parallel"))(land)


def _swap_with_sibling(sums, *, name):
    n = len(sums)

    def body(*refs):
        ins, outs = refs[:n], refs[n:2 * n]
        send_sems, recv_sems = refs[2 * n:]
        x, y, c = _place()
        copies = [pltpu.make_async_remote_copy(
            src_ref=ins[w], dst_ref=outs[w], send_sem=send_sems.at[w], recv_sem=recv_sems.at[w],
            device_id=(x, y, 1 - c), device_id_type=MESH) for w in range(n)]
        for cp in copies:
            cp.start()
        for cp in copies:
            cp.wait_recv()
            cp.wait_send()

    return pl.pallas_call(
        body, name=name,
        out_shape=[jax.ShapeDtypeStruct(s.shape, s.dtype) for s in sums],
        in_specs=[_ANY] * n, out_specs=[_ANY] * n,
        scratch_shapes=[pltpu.SemaphoreType.DMA((n,)), pltpu.SemaphoreType.DMA((n,))],
    )(*sums)


def _adamw_math(w, g, m, v):
    m = ADAM_B1 * m + (1.0 - ADAM_B1) * g
    v = ADAM_B2 * v + (1.0 - ADAM_B2) * (g * g)
    m_hat = m / (1.0 - ADAM_B1 ** ADAM_STEP)
    v_hat = v / (1.0 - ADAM_B2 ** ADAM_STEP)
    delta = -ADAM_LR * (m_hat / (jnp.sqrt(v_hat) + ADAM_EPS) + ADAM_WD * w)
    return delta, m, v


def _adamw_pair(mine, theirs, w, m, v, *, name, tr=128):
    R, C = w.shape
    tr = min(tr, R)

    def body(a_ref, b_ref, w_ref, m_ref, v_ref, g_ref, d_ref, nm_ref, nv_ref):
        g = a_ref[...] + b_ref[...]
        g_ref[...] = g
        d_ref[...], nm_ref[...], nv_ref[...] = _adamw_math(w_ref[...], g, m_ref[...], v_ref[...])

    shape = jax.ShapeDtypeStruct((R, C), F32)
    return pl.pallas_call(
        body, name=name, out_shape=(shape,) * 4, grid=(R // tr,),
        in_specs=[_rows(tr, C)] * 5, out_specs=(_rows(tr, C),) * 4,
        compiler_params=_cparams("parallel"))(mine, theirs, w, m, v)


def _all_reduce_small(packed):
    R = packed.shape[0]
    half = R // 2

    def body(x_ref, g_ref, sib_ref, pair_ref, land_ref, send_sems, recv_sems):
        x, y, c = _place()
        me = 2 * x + y
        sibling = (x, y, 1 - c)

        swap = pltpu.make_async_remote_copy(
            src_ref=x_ref, dst_ref=sib_ref, send_sem=send_sems.at[0], recv_sem=recv_sems.at[0],
            device_id=sibling, device_id_type=MESH)
        swap.start()
        swap.wait()
        mine, theirs = x_ref[...], sib_ref[...]
        south = c == 0
        pair_ref[...] = jnp.where(south, mine, theirs) + jnp.where(south, theirs, mine)

        land_ref[me] = pair_ref[c]
        for j, (px, py) in enumerate(_other_chips(x, y)):
            pltpu.make_async_remote_copy(
                src_ref=pair_ref.at[c], dst_ref=land_ref.at[me], send_sem=send_sems.at[1 + j],
                recv_sem=recv_sems.at[1 + j], device_id=(px, py, c), device_id_type=MESH).start()
        for j, (px, py) in enumerate(_other_chips(x, y)):
            arrival = pltpu.make_async_remote_copy(
                src_ref=pair_ref.at[c], dst_ref=land_ref.at[2 * px + py], send_sem=send_sems.at[1 + j],
                recv_sem=recv_sems.at[1 + j], device_id=(px, py, c), device_id_type=MESH)
            arrival.wait_recv()
            arrival.wait_send()
        total = land_ref[0]
        for k in range(1, N_CHIPS):
            total = total + land_ref[k]
        g_ref[c] = total

        give = pltpu.make_async_remote_copy(
            src_ref=g_ref.at[c], dst_ref=g_ref.at[c], send_sem=send_sems.at[4], recv_sem=recv_sems.at[4],
            device_id=sibling, device_id_type=MESH)
        give.start()
        take = pltpu.make_async_remote_copy(
            src_ref=g_ref.at[c], dst_ref=g_ref.at[1 - c], send_sem=send_sems.at[4], recv_sem=recv_sems.at[4],
            device_id=sibling, device_id_type=MESH)
        take.wait_recv()
        give.wait_send()

    vm = pl.BlockSpec(memory_space=pltpu.VMEM)
    return pl.pallas_call(
        body, name="all_reduce_small", out_shape=jax.ShapeDtypeStruct((2, half, LANES), F32),
        in_specs=[vm], out_specs=vm,
        scratch_shapes=[pltpu.VMEM((2, half, LANES), F32), pltpu.VMEM((2, half, LANES), F32),
                        pltpu.VMEM((N_CHIPS, half, LANES), F32),
                        pltpu.SemaphoreType.DMA((5,)), pltpu.SemaphoreType.DMA((5,))],
        compiler_params=pltpu.CompilerParams(vmem_limit_bytes=VMEM_LIMIT_BYTES),
    )(packed.reshape(2, half, LANES)).reshape(R, LANES)


def _adamw_small(g, w, m, v):
    R = g.shape[0]
    tr = PACK_ROWS

    def body(g_ref, w_ref, m_ref, v_ref, d_ref, nm_ref, nv_ref):
        d_ref[...], nm_ref[...], nv_ref[...] = _adamw_math(w_ref[...], g_ref[...], m_ref[...], v_ref[...])

    shape = jax.ShapeDtypeStruct((R, LANES), F32)
    return pl.pallas_call(
        body, name="adamw_small", out_shape=(shape,) * 3, grid=(R // tr,),
        in_specs=[_rows(tr, LANES)] * 4, out_specs=(_rows(tr, LANES),) * 3,
        compiler_params=_cparams("parallel"))(g, w, m, v)


def _pack(arrays):
    parts, layout = [], []
    for a in arrays:
        n = a.size
        rows = -(-n // (8 * LANES)) * 8
        flat = jnp.pad(a.reshape(-1).astype(F32), (0, rows * LANES - n))
        parts.append(flat.reshape(rows, LANES))
        layout.append((rows, n, a.shape))
    total = sum(r for r, _, _ in layout)
    parts.append(jnp.zeros((-total % PACK_ROWS, LANES), F32))
    return jnp.concatenate(parts, axis=0), layout


def _unpack(buf, layout):
    out, r0 = [], 0
    for rows, n, shape in layout:
        out.append(buf[r0:r0 + rows].reshape(-1)[:n].reshape(shape))
        r0 += rows
    return out


SMALL = ("mix_norm_pre", "lam_re", "lam_im", "log_dt", "ssm_b_re", "ssm_b_im", "ssm_c_re", "ssm_c_im",
         "ssm_d", "b_glu", "attn_out_norm", "ssm_out_norm", "mix_norm_post", "mlp_norm_pre",
         "mlp_norm_post", "ple_norm_pre", "ple_norm_post")
BIG = ("w_in", "w_glu", "w_out", "w_up", "w_down", "w_ple_gate", "w_ple_proj")
WEIGHTS = ("mix_norm_pre", "w_in", "lam_re", "lam_im", "log_dt", "ssm_b_re", "ssm_b_im", "ssm_c_re",
           "ssm_c_im", "ssm_d", "w_glu", "b_glu", "attn_out_norm", "ssm_out_norm", "w_out",
           "mix_norm_post", "mlp_norm_pre", "w_up", "w_down", "mlp_norm_post", "ple_norm_pre",
           "w_ple_gate", "w_ple_proj", "ple_norm_post")


def kernel(x, p, mix_norm_pre, w_in, lam_re, lam_im, log_dt, ssm_b_re, ssm_b_im, ssm_c_re, ssm_c_im, ssm_d, w_glu, b_glu, attn_out_norm, ssm_out_norm, w_out, mix_norm_post, mlp_norm_pre, w_up, w_down, mlp_norm_post, ple_norm_pre, w_ple_gate, w_ple_proj, ple_norm_post, loss_target, m_mix_norm_pre, m_w_in, m_lam_re, m_lam_im, m_log_dt, m_ssm_b_re, m_ssm_b_im, m_ssm_c_re, m_ssm_c_im, m_ssm_d, m_w_glu, m_b_glu, m_attn_out_norm, m_ssm_out_norm, m_w_out, m_mix_norm_post, m_mlp_norm_pre, m_w_up, m_w_down, m_mlp_norm_post, m_ple_norm_pre, m_w_ple_gate, m_w_ple_proj, m_ple_norm_post, v_mix_norm_pre, v_w_in, v_lam_re, v_lam_im, v_log_dt, v_ssm_b_re, v_ssm_b_im, v_ssm_c_re, v_ssm_c_im, v_ssm_d, v_w_glu, v_b_glu, v_attn_out_norm, v_ssm_out_norm, v_w_out, v_mix_norm_post, v_mlp_norm_pre, v_w_up, v_w_down, v_mlp_norm_post, v_ple_norm_pre, v_w_ple_gate, v_w_ple_proj, v_ple_norm_post):
    args = dict(locals())
    W = {n: args[n][0] for n in WEIGHTS}
    Mo = {n: args["m_" + n][0] for n in WEIGHTS}
    Vo = {n: args["v_" + n][0] for n in WEIGHTS}
    xs, ps, tgt = x[0], p[0, 0], loss_target[0]
    S, D = xs.shape
    SW = W["ssm_d"].shape[0]
    AW = W["attn_out_norm"].shape[0]
    heads = AW // HEAD_DIM
    G = SW // SSM_C
    nbk = SW // LANES
    assert W["w_in"].shape[1] * N_CHIPS == 3 * AW + SW and AW == SW

    row = lambda a: a.reshape(1, -1)

    ag_groups = (("w_in",), ("w_glu", "w_out"), ("w_up",), ("w_down", "w_ple_gate", "w_ple_proj"))
    ag_names = [n for g in ag_groups for n in g]
    ag_sems, ag_land, _, ag_token = _exchange_start(
        [_place_own(W[n], gather=True, name="ag_place_" + n) for n in ag_names], [],
        [[ag_names.index(n) for n in g] for g in ag_groups], name="ag_start")

    def gathered(gi, after):
        got = _exchange_wait([ag_land[ag_names.index(n)] for n in ag_groups[gi]], [], ag_sems[gi], after,
                             name=f"ag_wait_{gi}")
        return dict(zip(ag_groups[gi], got))

    lr_e = W["lam_re"].reshape(nbk, 1, STATE_LANES)
    li_e = W["lam_im"].reshape(nbk, 1, STATE_LANES)
    ldt_e = jnp.repeat(W["log_dt"], SSM_P).reshape(nbk, 1, STATE_LANES)
    bre_e, bim_e = _expand_b(W["ssm_b_re"]), _expand_b(W["ssm_b_im"])
    cre_e, cim_e = _expand_c(W["ssm_c_re"]), _expand_c(W["ssm_c_im"])
    d_row = row(W["ssm_d"])

    hn1 = _norm_cast(xs, row(W["mix_norm_pre"]) + ag_token[0, 0], name="norm_in")
    w_in_f = gathered(0, hn1)["w_in"]
    qkv_b = _proj_qkv(hn1, w_in_f)
    u = _matmul(hn1, w_in_f, name="proj_u", b_shards=N_CHIPS, b_cols=(3 * AW, SW))
    outs, lses = zip(*[_attn_fwd(qb, d, heads) for d, qb in zip(DILATIONS, qkv_b)])
    y1, y2b, st_r, st_i, ends_r, ends_i = _ssm_fwd(u, lr_e, li_e, ldt_e, bre_e, bim_e, cre_e, cim_e, d_row)
    full = gathered(1, y2b)
    w_glu_f = full["w_glu"].reshape(SW, SW)
    w_out_f = full["w_out"].reshape(AW + SW, D)
    z = _matmul(y2b, w_glu_f, name="glu_z")
    attn, lse_b, mixed = _mix_fwd(outs, lses, y1, z, row(W["b_glu"]), row(W["attn_out_norm"]), row(W["ssm_out_norm"]))
    mo = _matmul(mixed, w_out_f, name="mix_out")
    h1, hn2 = _res_norm(xs, mo, row(W["mix_norm_post"]), row(W["mlp_norm_pre"]), name="res_mix")
    w_up_f = gathered(2, hn2)["w_up"]
    up, act = _matmul(hn2, w_up_f, name="mlp_up", b_shards=N_CHIPS, relu2=True)
    full = gathered(3, act)
    w_down_f = full["w_down"].reshape(-1, D)
    w_pg_f = full["w_ple_gate"].reshape(D, D)
    w_pp_f = full["w_ple_proj"]
    ff = _matmul(act, w_down_f, name="mlp_down")
    h2, hn3 = _res_norm(h1, ff, row(W["mlp_norm_post"]), row(W["ple_norm_pre"]), name="res_mlp")
    gl = _matmul(hn3, w_pg_f, name="ple_gate")
    e = _matmul(ps.astype(BF16), w_pp_f, name="ple_proj", b_shards=N_CHIPS)

    dh3, dgl, de, loss_part, dg_ple_post = _final(h2, gl, e, row(W["ple_norm_post"]), tgt)
    gW = {}
    out_g, out_d, out_m, out_v = {}, {}, {}, {}

    def scatter_start(names, tag):
        parts = [gW[n] if gW[n].ndim == 3 else gW[n].reshape((N_CHIPS, -1, gW[n].shape[1])) for n in names]
        sems, land, src, token = _exchange_start(
            [_place_own(part, gather=False, name="rs_place_" + n) for n, part in zip(names, parts)], parts,
            [list(range(len(names)))], name=f"rs_start_{tag}")
        return (names, sems[0], land, src), token

    def scatter_finish(batch, after, tag):
        names, sems, land, src = batch
        landed = _exchange_wait(land, src, sems, after, name=f"rs_wait_{tag}")
        sums = [_sum_partials(l, name="sum_" + n) for n, l in zip(names, landed)]
        theirs = _swap_with_sibling(sums, name=f"swap_{tag}")
        for n, a, b in zip(names, sums, theirs):
            out_g[n], out_d[n], out_m[n], out_v[n] = _adamw_pair(a, b, W[n], Mo[n], Vo[n], name="adamw_" + n)

    gW["w_ple_proj"] = _matmul(ps.astype(BF16), de, name="d_w_ple_proj", ta=True, out_dtype=BF16, out_shards=N_CHIPS)
    gW["w_ple_gate"] = _matmul(hn3, dgl, name="d_w_ple_gate", ta=True, out_dtype=BF16)
    dhn3 = _matmul(dgl, w_pg_f, name="d_hn3", tb=True)
    dh2, dff, dg_ple_pre, dg_mlp_post = _bwd_res_norm(
        dh3, dhn3, h2, row(W["ple_norm_pre"]), ff, row(W["mlp_norm_post"]), name="bwd_res_mlp")
    gW["w_down"] = _matmul(act, dff, name="d_w_down", ta=True, out_dtype=BF16)
    batch1, token1 = scatter_start(("w_ple_proj", "w_ple_gate", "w_down"), 1)
    dup = _matmul(dff, w_down_f, name="d_up", tb=True, after=token1, relu2_of=up, out_dtype=BF16)
    gW["w_up"] = _matmul(hn2, dup, name="d_w_up", ta=True, out_dtype=BF16, out_shards=N_CHIPS)
    batch2, token2 = scatter_start(("w_up",), 2)
    dhn2 = _matmul(dup, w_up_f, name="d_hn2", tb=True, b_shards=N_CHIPS, after=token2)
    dh1, dmo, dg_mlp_pre, dg_mix_post = _bwd_res_norm(
        dh2, dhn2, h1, row(W["mlp_norm_pre"]), mo, row(W["mix_norm_post"]), name="bwd_res_mix")
    gW["w_out"] = _matmul(mixed, dmo, name="d_w_out", ta=True, out_dtype=BF16)
    dmixed = _matmul(dmo, w_out_f, name="d_mixed", tb=True)
    dattn_b, dd_b, dz, dy2a, dg_attn, dg_ssm, db_glu = _mix_bwd(
        dmixed, attn, y1, z, row(W["b_glu"]), row(W["attn_out_norm"]), row(W["ssm_out_norm"]))
    gW["w_glu"] = _matmul(y2b, dz, name="d_w_glu", ta=True, out_dtype=BF16)
    batch3, token3 = scatter_start(("w_out", "w_glu"), 3)
    dy2b = _matmul(dz, w_glu_f, name="d_y2", tb=True, after=token3)
    du, dar8, dai8, dcr_e, dci_e, dbr_e, dbi_e, dd8 = _ssm_bwd(
        u, y1, dy2a, dy2b, st_r, st_i, ends_r, ends_i, lr_e, li_e, ldt_e, bre_e, bim_e, cre_e, cim_e, d_row)
    scatter_finish(batch1, du, 1)
    dlr_e, dli_e, dldt_e, dbre_e, dbim_e = _ssm_param_bwd(dar8, dai8, dbr_e, dbi_e, lr_e, li_e, ldt_e, bre_e, bim_e)

    dqs, dks, dvs = zip(*[_attn_bwd(qb, da, l, dd_, d, heads)
                          for d, qb, da, l, dd_ in zip(DILATIONS, qkv_b, dattn_b, lse_b, dd_b)])
    dproj = _dproj_join(dqs, dks, dvs, du)
    scatter_finish(batch2, dproj, 2)
    scatter_finish(batch3, dproj, 3)
    gW["w_in"] = _matmul(hn1, dproj, name="d_w_in", ta=True, out_dtype=BF16, out_shards=N_CHIPS)
    batch4, token4 = scatter_start(("w_in",), 4)
    dhn1 = _matmul(dproj, w_in_f, name="d_hn1", tb=True, b_shards=N_CHIPS, after=token4)
    grad_x, dg_mix_pre = _bwd_first(dh1, dhn1, xs, row(W["mix_norm_pre"]))
    scatter_finish(batch4, grad_x, 4)

    small_g = {
        "mix_norm_pre": dg_mix_pre, "lam_re": dlr_e.reshape(G, SSM_P), "lam_im": dli_e.reshape(G, SSM_P),
        "log_dt": dldt_e.reshape(G, SSM_P)[:, 0], "ssm_b_re": _collapse_b(dbre_e), "ssm_b_im": _collapse_b(dbim_e),
        "ssm_c_re": _collapse_c(dcr_e), "ssm_c_im": _collapse_c(dci_e), "ssm_d": dd8.sum(axis=1).reshape(-1),
        "b_glu": db_glu, "attn_out_norm": dg_attn, "ssm_out_norm": dg_ssm, "mix_norm_post": dg_mix_post,
        "mlp_norm_pre": dg_mlp_pre, "mlp_norm_post": dg_mlp_post, "ple_norm_pre": dg_ple_pre,
        "ple_norm_post": dg_ple_post,
    }
    g_pack, layout = _pack([small_g[n].reshape(W[n].shape) for n in SMALL])
    w_pack, _ = _pack([W[n] for n in SMALL])
    m_pack, _ = _pack([Mo[n] for n in SMALL])
    v_pack, _ = _pack([Vo[n] for n in SMALL])
    g_sum = _all_reduce_small(g_pack)
    packed = (g_sum,) + tuple(_adamw_small(g_sum, w_pack, m_pack, v_pack))
    for dst, buf in zip((out_g, out_d, out_m, out_v), packed):
        dst.update(zip(SMALL, _unpack(buf, layout)))

    loss = lax.psum(loss_part[0, 0], ("x", "y", "c"))
    lead = lambda a: a[None]
    return (loss, grad_x[None],
            *[lead(out_g[n]) for n in WEIGHTS], *[lead(out_d[n]) for n in WEIGHTS],
            *[lead(out_m[n]) for n in WEIGHTS], *[lead(out_v[n]) for n in WEIGHTS])
```

```python
import functools
import math

import jax
import jax.numpy as jnp
from jax import lax
from jax.experimental import pallas as pl
from jax.experimental.pallas import tpu as pltpu

F32 = jnp.float32
BF16 = jnp.bfloat16
MESH = pl.DeviceIdType.MESH

RMS_EPS = 1e-6
NEG_INF = -1e30
HEAD_DIM = 128
BLK = 128
DILATIONS = (1, 4, 16)
ATTN_LOOKAHEAD = 3
SSM_C = 16
SSM_P = 64
LANES = 128
GROUPS_PER_BLOCK = LANES // SSM_C
STATE_LANES = GROUPS_PER_BLOCK * SSM_P
SSM_CHUNK = 512
TILE = 8
ADAM_LR, ADAM_B1, ADAM_B2, ADAM_EPS, ADAM_WD, ADAM_STEP = 1e-3, 0.9, 0.999, 1e-8, 0.01, 10
VMEM_LIMIT_BYTES = 56 * 1024 * 1024
N_CHIPS = 4
N_DEV = 8
PACK_ROWS = 256


def _cparams(*sem):
    return pltpu.CompilerParams(dimension_semantics=sem or None, vmem_limit_bytes=VMEM_LIMIT_BYTES)


def _rows(tr, w):
    return pl.BlockSpec((tr, w), lambda i: (i, 0))


def _vec(w):
    return pl.BlockSpec((1, w), lambda i: (0, 0))


def _sigmoid(x):
    return 1.0 / (1.0 + jnp.exp(-x))


def _gelu(x):
    c = math.sqrt(2.0 / math.pi)
    return 0.5 * x * (1.0 + jnp.tanh(c * (x + 0.044715 * x * x * x)))


def _gelu_grad(x):
    c = math.sqrt(2.0 / math.pi)
    th = jnp.tanh(c * (x + 0.044715 * x * x * x))
    return 0.5 * (1.0 + th) + 0.5 * x * (1.0 - th * th) * c * (1.0 + 3.0 * 0.044715 * x * x)


def _rms(x, g):
    r = lax.rsqrt(jnp.mean(x * x, axis=-1, keepdims=True) + RMS_EPS)
    return x * r * g


def _rms_bwd(dy, x, g):
    r = lax.rsqrt(jnp.mean(x * x, axis=-1, keepdims=True) + RMS_EPS)
    n = x * r
    dn = dy * g
    dx = r * (dn - n * jnp.mean(dn * n, axis=-1, keepdims=True))
    return dx, dy * n


def _colsum(a):
    return jnp.sum(a, axis=0, keepdims=True)


def _first(i):
    return i == 0


def _matmul(a, b, *, name, ta=False, tb=False, out_dtype=F32, b_shards=1, out_shards=1, b_cols=None,
            after=None, relu2=False, relu2_of=None, tm=1024, tn=1024, tk=2048):
    if ta:
        K, M = a.shape
    else:
        M, K = a.shape
    if b_shards > 1:
        rows, cols = b.shape[1], b.shape[2] * b_shards
    else:
        rows, cols = b.shape
    N, Kb = (rows, cols) if tb else (cols, rows)
    assert K == Kb, (a.shape, b.shape, ta, tb)
    col0 = 0
    if b_cols is not None:
        assert not tb
        col0, N = b_cols
    tm, tn, tk = min(tm, M), min(tn, N), min(tk, K)
    if b_shards > 1:
        shard_cols = cols // b_shards
        if tb:
            tk = min(tk, shard_cols)
        else:
            tn = min(tn, shard_cols)
    if out_shards > 1:
        tn = min(tn, N // out_shards)
    assert M % tm == 0 and N % tn == 0 and K % tk == 0 and col0 % tn == 0
    nk = K // tk
    j0 = col0 // tn

    a_spec = (pl.BlockSpec((tk, tm), lambda i, j, k: (k, i)) if ta
              else pl.BlockSpec((tm, tk), lambda i, j, k: (i, k)))
    if b_shards > 1:
        if tb:
            per = shard_cols // tk
            b_spec = pl.BlockSpec((None, tn, tk), lambda i, j, k: (k // per, j, k % per))
        else:
            per = shard_cols // tn
            b_spec = pl.BlockSpec((None, tk, tn), lambda i, j, k: ((j + j0) // per, k, (j + j0) % per))
    else:
        b_spec = (pl.BlockSpec((tn, tk), lambda i, j, k: (j, k)) if tb
                  else pl.BlockSpec((tk, tn), lambda i, j, k: (k, j + j0)))
    if out_shards > 1:
        per_o = (N // out_shards) // tn
        out_shape = jax.ShapeDtypeStruct((out_shards, M, N // out_shards), out_dtype)
        out_spec = pl.BlockSpec((None, tm, tn), lambda i, j, k: (j // per_o, i, j % per_o))
    else:
        out_shape = jax.ShapeDtypeStruct((M, N), out_dtype)
        out_spec = pl.BlockSpec((tm, tn), lambda i, j, k: (i, j))
    dims = (((0 if ta else 1,), (1 if tb else 0,)), ((), ()))

    extra, extra_specs = [], []
    if relu2_of is not None:
        assert out_shards == 1 and relu2_of.shape == (M, N)
        extra.append(relu2_of)
        extra_specs.append(pl.BlockSpec((tm, tn), lambda i, j, k: (i, j)))
    if after is not None:
        extra.append(after)
        extra_specs.append(pl.BlockSpec(after.shape, lambda i, j, k: (0, 0)))
    n_in = 2 + len(extra)
    if relu2:
        assert out_shards == 1
        out_shape = (out_shape, jax.ShapeDtypeStruct((M, N), BF16))
        out_spec = (out_spec, out_spec)

    def finish(acc, refs):
        o_ref = refs[n_in]
        if relu2_of is not None:
            acc = acc * (2.0 * jnp.maximum(refs[2][...], 0.0))
        o_ref[...] = acc.astype(o_ref.dtype)
        if relu2:
            r = jnp.maximum(acc, 0.0)
            refs[n_in + 1][...] = (r * r).astype(BF16)

    def body(*refs):
        prod = lax.dot_general(refs[0][...], refs[1][...], dims, preferred_element_type=F32)
        if nk == 1:
            finish(prod, refs)
            return
        acc_ref = refs[-1]
        k = pl.program_id(2)

        @pl.when(k == 0)
        def _():
            acc_ref[...] = prod

        @pl.when(k > 0)
        def _():
            acc_ref[...] += prod

        @pl.when(k == nk - 1)
        def _():
            finish(acc_ref[...], refs)

    return pl.pallas_call(
        body, name=name, out_shape=out_shape, grid=(M // tm, N // tn, nk),
        in_specs=[a_spec, b_spec] + extra_specs, out_specs=out_spec,
        scratch_shapes=[pltpu.VMEM((tm, tn), F32)] if nk > 1 else [],
        compiler_params=_cparams("parallel", "parallel", "arbitrary"),
    )(a, b, *extra)


def _norm_cast(x, g, *, name, tr=256):
    S, D = x.shape
    tr = min(tr, S)

    def body(x_ref, g_ref, o_ref):
        o_ref[...] = _rms(x_ref[...], g_ref[...]).astype(BF16)

    return pl.pallas_call(
        body, name=name, out_shape=jax.ShapeDtypeStruct((S, D), BF16), grid=(S // tr,),
        in_specs=[_rows(tr, D), _vec(D)], out_specs=_rows(tr, D),
        compiler_params=_cparams("parallel"))(x, g)


def _res_norm(res, y, g_post, g_next, *, name, tr=256):
    S, D = res.shape
    tr = min(tr, S)

    def body(res_ref, y_ref, gp_ref, gn_ref, h_ref, hn_ref):
        h = res_ref[...] + _rms(y_ref[...], gp_ref[...])
        h_ref[...] = h
        hn_ref[...] = _rms(h, gn_ref[...]).astype(BF16)

    return pl.pallas_call(
        body, name=name,
        out_shape=(jax.ShapeDtypeStruct((S, D), F32), jax.ShapeDtypeStruct((S, D), BF16)),
        grid=(S // tr,), in_specs=[_rows(tr, D), _rows(tr, D), _vec(D), _vec(D)],
        out_specs=(_rows(tr, D), _rows(tr, D)), compiler_params=_cparams("parallel"))(res, y, g_post, g_next)


def _residue_spec(tr, d, w):
    return pl.BlockSpec((tr // d, d * w), lambda i: (i, 0))


def _residue_shape(S, d, w, dtype):
    return jax.ShapeDtypeStruct((S // d, d * w), dtype)


def _residue_scratch(rows, w):
    return pltpu.VMEM((w // LANES, rows, LANES), F32)


def _fill_strips(scr, val):
    for s in range(scr.shape[0]):
        scr[s] = val[:, s * LANES:(s + 1) * LANES]


def _strips_to_residues(scr, o_ref, d):
    strips, rows, _ = scr.shape
    for r in range(d):
        for s in range(strips):
            col = (r * strips + s) * LANES
            o_ref[:, col:col + LANES] = scr[s, pl.ds(r, rows // d, stride=d), :].astype(o_ref.dtype)


def _to_residues(scr, val, o_ref, d):
    if d == 1:
        o_ref[...] = val.astype(o_ref.dtype)
        return
    _fill_strips(scr, val)
    _strips_to_residues(scr, o_ref, d)


def _from_residues(scr, in_ref, d):
    if d == 1:
        return in_ref[...].astype(F32)
    strips, rows, _ = scr.shape
    for r in range(d):
        for s in range(strips):
            col = (r * strips + s) * LANES
            scr[s, pl.ds(r, rows // d, stride=d), :] = in_ref[:, col:col + LANES].astype(F32)
    return jnp.concatenate([scr[s] for s in range(strips)], axis=1)


def _mix_fwd(os, ls, y1, z, b_glu, g_attn, g_ssm, *, tr=128):
    S, SW = y1.shape
    AW = os[0].shape[1]
    tr = min(tr, S)
    nd = len(DILATIONS)

    def body(*refs):
        o_refs, l_refs = refs[:nd], refs[nd:2 * nd]
        y_ref, z_ref, b_ref, ga_ref, gs_ref, attn_ref = refs[2 * nd:2 * nd + 6]
        lse_refs = refs[2 * nd + 6:3 * nd + 6]
        mixed_ref, scr = refs[3 * nd + 6:]
        ls_ = [_from_residues(scr, l_refs[n], d) for n, d in enumerate(DILATIONS)]
        m = functools.reduce(jnp.maximum, ls_)
        es = [jnp.exp(l - m) for l in ls_]
        tot = functools.reduce(jnp.add, es)
        attn = functools.reduce(jnp.add, [e * _from_residues(scr, o_refs[n], d)
                                          for n, (e, d) in enumerate(zip(es, DILATIONS))]) / tot
        attn_ref[...] = attn
        lse = m + jnp.log(tot)
        for n, d in enumerate(DILATIONS):
            _to_residues(scr, lse, lse_refs[n], d)
        ssm = _gelu(y_ref[...]) * _sigmoid(z_ref[...] + b_ref[...])
        mixed_ref[:, :AW] = _rms(attn, ga_ref[...]).astype(BF16)
        mixed_ref[:, AW:] = _rms(ssm, gs_ref[...]).astype(BF16)

    res_in = [_residue_spec(tr, d, AW) for d in DILATIONS]
    res = pl.pallas_call(
        body, name="mix_fwd",
        out_shape=([jax.ShapeDtypeStruct((S, AW), F32)] + [_residue_shape(S, d, AW, F32) for d in DILATIONS]
                   + [jax.ShapeDtypeStruct((S, AW + SW), BF16)]),
        grid=(S // tr,),
        in_specs=res_in + res_in + [_rows(tr, SW), _rows(tr, SW), _vec(SW), _vec(AW), _vec(SW)],
        out_specs=[_rows(tr, AW)] + res_in + [_rows(tr, AW + SW)],
        scratch_shapes=[_residue_scratch(tr, AW)],
        compiler_params=_cparams("parallel"))(*os, *ls, y1, z, b_glu, g_attn, g_ssm)
    return res[0], res[1:1 + nd], res[1 + nd]


def _final(h2, gl, e, g_post, target, *, tr=128):
    S, D = h2.shape
    tr = min(tr, S)

    def body(h_ref, gl_ref, e_ref, g_ref, t_ref, dh_ref, dgl_ref, de_ref, loss_ref, dg_ref):
        i = pl.program_id(0)
        gate = _sigmoid(gl_ref[...])
        e_ = e_ref[...]
        ge = gate * e_
        g = g_ref[...]
        diff = h_ref[...] + _rms(ge, g) - t_ref[...]
        dh = diff * (1.0 / D)
        dh_ref[...] = dh
        dge, dgrow = _rms_bwd(dh, ge, g)
        dgl_ref[...] = (dge * e_ * gate * (1.0 - gate)).astype(BF16)
        de_ref[...] = (dge * gate).astype(BF16)
        part = _colsum(0.5 * jnp.mean(diff * diff, axis=-1, keepdims=True))

        @pl.when(_first(i))
        def _():
            loss_ref[...] = jnp.zeros_like(loss_ref)
            dg_ref[...] = jnp.zeros_like(dg_ref)

        loss_ref[...] += part + jnp.zeros((1, LANES), F32)
        dg_ref[...] += _colsum(dgrow)

    return pl.pallas_call(
        body, name="final_fwd_bwd",
        out_shape=(jax.ShapeDtypeStruct((S, D), F32), jax.ShapeDtypeStruct((S, D), BF16),
                   jax.ShapeDtypeStruct((S, D), BF16), jax.ShapeDtypeStruct((1, LANES), F32),
                   jax.ShapeDtypeStruct((1, D), F32)),
        grid=(S // tr,),
        in_specs=[_rows(tr, D), _rows(tr, D), _rows(tr, D), _vec(D), _rows(tr, D)],
        out_specs=(_rows(tr, D), _rows(tr, D), _rows(tr, D), _vec(LANES), _vec(D)),
        compiler_params=_cparams("arbitrary"))(h2, gl, e, g_post, target)


def _bwd_res_norm(dh_out, dhn, h, g_next, y, g_post, *, name, tr=128):
    S, D = h.shape
    tr = min(tr, S)

    def body(dho_ref, dhn_ref, h_ref, gn_ref, y_ref, gp_ref, dh_ref, dy_ref, dgn_ref, dgp_ref):
        i = pl.program_id(0)
        dx, dgn_rows = _rms_bwd(dhn_ref[...], h_ref[...], gn_ref[...])
        dh = dho_ref[...] + dx
        dh_ref[...] = dh
        dy, dgp_rows = _rms_bwd(dh, y_ref[...], gp_ref[...])
        dy_ref[...] = dy.astype(BF16)

        @pl.when(_first(i))
        def _():
            dgn_ref[...] = jnp.zeros_like(dgn_ref)
            dgp_ref[...] = jnp.zeros_like(dgp_ref)

        dgn_ref[...] += _colsum(dgn_rows)
        dgp_ref[...] += _colsum(dgp_rows)

    return pl.pallas_call(
        body, name=name,
        out_shape=(jax.ShapeDtypeStruct((S, D), F32), jax.ShapeDtypeStruct((S, D), BF16),
                   jax.ShapeDtypeStruct((1, D), F32), jax.ShapeDtypeStruct((1, D), F32)),
        grid=(S // tr,),
        in_specs=[_rows(tr, D), _rows(tr, D), _rows(tr, D), _vec(D), _rows(tr, D), _vec(D)],
        out_specs=(_rows(tr, D), _rows(tr, D), _vec(D), _vec(D)),
        compiler_params=_cparams("arbitrary"))(dh_out, dhn, h, g_next, y, g_post)


def _bwd_first(dh1, dhn1, x, g1, *, tr=256):
    S, D = x.shape
    tr = min(tr, S)

    def body(dh_ref, dhn_ref, x_ref, g_ref, dx_ref, dg_ref):
        i = pl.program_id(0)
        dx, dg_rows = _rms_bwd(dhn_ref[...], x_ref[...], g_ref[...])
        dx_ref[...] = dh_ref[...] + dx

        @pl.when(_first(i))
        def _():
            dg_ref[...] = jnp.zeros_like(dg_ref)

        dg_ref[...] += _colsum(dg_rows)

    return pl.pallas_call(
        body, name="bwd_first",
        out_shape=(jax.ShapeDtypeStruct((S, D), F32), jax.ShapeDtypeStruct((1, D), F32)),
        grid=(S // tr,), in_specs=[_rows(tr, D), _rows(tr, D), _rows(tr, D), _vec(D)],
        out_specs=(_rows(tr, D), _vec(D)), compiler_params=_cparams("arbitrary"))(dh1, dhn1, x, g1)


def _mix_bwd(dmixed, attn, y1, z, b_glu, g_attn, g_ssm, *, tr=256):
    S, AW = attn.shape
    SW = y1.shape[1]
    tr = min(tr, S)
    heads = AW // HEAD_DIM
    nd = len(DILATIONS)

    def body(*refs):
        dm_ref, a_ref, y_ref, z_ref, b_ref, ga_ref, gs_ref = refs[:7]
        da_refs, dd_refs = refs[7:7 + nd], refs[7 + nd:7 + 2 * nd]
        dz_ref, dy2_ref, dga_ref, dgs_ref, db_ref, scr, dd_scr = refs[7 + 2 * nd:]
        i = pl.program_id(0)
        attn_ = a_ref[...]
        dattn, dga_rows = _rms_bwd(dm_ref[:, :AW], attn_, ga_ref[...])
        prod = dattn * attn_
        for h in range(heads):
            sl = slice(h * HEAD_DIM, (h + 1) * HEAD_DIM)
            dd_scr[:, sl] = jnp.broadcast_to(jnp.sum(prod[:, sl], axis=-1, keepdims=True), (tr, HEAD_DIM))
        for n, d in enumerate(DILATIONS):
            _to_residues(scr, dattn, da_refs[n], d)
            _to_residues(scr, dd_scr[...], dd_refs[n], d)
        y2 = _gelu(y_ref[...])
        gate = _sigmoid(z_ref[...] + b_ref[...])
        dssm, dgs_rows = _rms_bwd(dm_ref[:, AW:], y2 * gate, gs_ref[...])
        dz = dssm * y2 * gate * (1.0 - gate)
        dz_ref[...] = dz.astype(BF16)
        dy2_ref[...] = dssm * gate

        @pl.when(_first(i))
        def _():
            dga_ref[...] = jnp.zeros_like(dga_ref)
            dgs_ref[...] = jnp.zeros_like(dgs_ref)
            db_ref[...] = jnp.zeros_like(db_ref)

        dga_ref[...] += _colsum(dga_rows)
        dgs_ref[...] += _colsum(dgs_rows)
        db_ref[...] += _colsum(dz)

    res_out = [_residue_spec(tr, d, AW) for d in DILATIONS]
    res = pl.pallas_call(
        body, name="mix_bwd",
        out_shape=([_residue_shape(S, d, AW, BF16) for d in DILATIONS]
                   + [_residue_shape(S, d, AW, F32) for d in DILATIONS]
                   + [jax.ShapeDtypeStruct((S, SW), BF16), jax.ShapeDtypeStruct((S, SW), F32),
                      jax.ShapeDtypeStruct((1, AW), F32), jax.ShapeDtypeStruct((1, SW), F32),
                      jax.ShapeDtypeStruct((1, SW), F32)]),
        grid=(S // tr,),
        in_specs=[_rows(tr, AW + SW), _rows(tr, AW), _rows(tr, SW), _rows(tr, SW), _vec(SW), _vec(AW), _vec(SW)],
        out_specs=res_out + res_out + [_rows(tr, SW), _rows(tr, SW), _vec(AW), _vec(SW), _vec(SW)],
        scratch_shapes=[_residue_scratch(tr, AW), pltpu.VMEM((tr, AW), F32)],
        compiler_params=_cparams("arbitrary"))(dmixed, attn, y1, z, b_glu, g_attn, g_ssm)
    return (res[:nd], res[nd:2 * nd]) + tuple(res[2 * nd:])


def _attn_mask2(i):
    row = lax.broadcasted_iota(jnp.int32, (BLK, 2 * BLK), 0)
    col = lax.broadcasted_iota(jnp.int32, (BLK, 2 * BLK), 1)
    return jnp.logical_and(col >= row, jnp.logical_and(col <= row + BLK, jnp.logical_or(col >= BLK, i > 0)))


_NT = (((1,), (1,)), ((), ()))
_TN = (((0,), (0,)), ((), ()))


def _attn_in_specs(width, block_of):
    def at(part, prev):
        def index(r, i):
            blk = block_of(i)
            return (part, jnp.maximum(blk - 1, 0) if prev else blk, r)
        return pl.BlockSpec((None, BLK, width), index)
    return [at(0, False), at(1, False), at(1, True), at(2, False), at(2, True)]


def _proj_qkv(hn, w_in_f, *, tm=1024):
    S, D = hn.shape
    AW = w_in_f.shape[2]
    tm = min(tm, S)

    def body(a_ref, b_ref, *rest):
        o_refs, scr = rest[:-1], rest[-1]
        prod = jnp.dot(a_ref[...], b_ref[...], preferred_element_type=F32)
        _fill_strips(scr, prod)
        for o_ref, d in zip(o_refs, DILATIONS):
            if d == 1:
                o_ref[...] = prod.astype(BF16)
            else:
                _strips_to_residues(scr, o_ref, d)

    return pl.pallas_call(
        body, name="proj_qkv",
        out_shape=[jax.ShapeDtypeStruct((3, S // d, d * AW), BF16) for d in DILATIONS], grid=(S // tm, 3),
        in_specs=[pl.BlockSpec((tm, D), lambda i, j: (i, 0)), pl.BlockSpec((None, D, AW), lambda i, j: (j, 0, 0))],
        out_specs=[pl.BlockSpec((None, tm // d, d * AW), lambda i, j: (j, i, 0)) for d in DILATIONS],
        scratch_shapes=[_residue_scratch(tm, AW)],
        compiler_params=_cparams("parallel", "parallel"))(hn, w_in_f)


def _attn_fwd(qkv, d, heads):
    M = qkv.shape[1]
    nb = M // BLK
    width = heads * HEAD_DIM
    scale = 1.0 / math.sqrt(HEAD_DIM)

    def body(q_ref, kc_ref, kp_ref, vc_ref, vp_ref, o_ref, l_ref):
        mask = _attn_mask2(pl.program_id(1))
        ones = jnp.ones((2 * BLK, HEAD_DIM), BF16)

        def scores(h):
            sl = slice(h * HEAD_DIM, (h + 1) * HEAD_DIM)
            k2 = jnp.concatenate([kp_ref[:, sl], kc_ref[:, sl]], axis=0)
            return lax.dot_general(q_ref[:, sl], k2, _NT, preferred_element_type=F32)

        ahead = [scores(h) for h in range(min(ATTN_LOOKAHEAD, heads))]
        for h in range(heads):
            sl = slice(h * HEAD_DIM, (h + 1) * HEAD_DIM)
            s = jnp.where(mask, ahead.pop(0) * scale, NEG_INF)
            if h + ATTN_LOOKAHEAD < heads:
                ahead.append(scores(h + ATTN_LOOKAHEAD))
            v2 = jnp.concatenate([vp_ref[:, sl], vc_ref[:, sl]], axis=0)
            m = jnp.max(jnp.maximum(s[:, :BLK], s[:, BLK:]), axis=-1, keepdims=True)
            p = jnp.exp(s - m).astype(BF16)
            tot = jnp.dot(p, ones, preferred_element_type=F32)
            o_ref[:, sl] = jnp.dot(p, v2, preferred_element_type=F32) / tot
            l_ref[:, sl] = m + jnp.log(tot)

    out_spec = pl.BlockSpec((BLK, width), lambda r, i: (i, r))
    shape = jax.ShapeDtypeStruct((M, d * width), F32)
    return pl.pallas_call(
        body, name=f"attn_fwd_d{d}", out_shape=(shape, shape), grid=(d, nb),
        in_specs=_attn_in_specs(width, lambda i: i), out_specs=(out_spec, out_spec),
        compiler_params=_cparams("parallel", "parallel"))(qkv, qkv, qkv, qkv, qkv)


def _attn_bwd(qkv, dattn, lse, dd, d, heads):
    M = qkv.shape[1]
    nb = M // BLK
    width = heads * HEAD_DIM
    scale = 1.0 / math.sqrt(HEAD_DIM)

    def block_of(i):
        return nb - 1 - i

    def body(q_ref, kc_ref, kp_ref, vc_ref, vp_ref, da_ref, l_ref, dd_ref,
             dq_ref, dk_ref, dv_ref, dk_carry, dv_carry):
        @pl.when(pl.program_id(1) == 0)
        def _():
            dk_carry[...] = jnp.zeros_like(dk_carry)
            dv_carry[...] = jnp.zeros_like(dv_carry)

        mask = _attn_mask2(block_of(pl.program_id(1)))

        def products(h):
            sl = slice(h * HEAD_DIM, (h + 1) * HEAD_DIM)
            k2 = jnp.concatenate([kp_ref[:, sl], kc_ref[:, sl]], axis=0)
            v2 = jnp.concatenate([vp_ref[:, sl], vc_ref[:, sl]], axis=0)
            return (lax.dot_general(q_ref[:, sl], k2, _NT, preferred_element_type=F32),
                    lax.dot_general(da_ref[:, sl], v2, _NT, preferred_element_type=F32), k2)

        ahead = [products(h) for h in range(min(ATTN_LOOKAHEAD, heads))]
        for h in range(heads):
            sl = slice(h * HEAD_DIM, (h + 1) * HEAD_DIM)
            qk, dp, k2 = ahead.pop(0)
            if h + ATTN_LOOKAHEAD < heads:
                ahead.append(products(h + ATTN_LOOKAHEAD))
            q, da = q_ref[:, sl], da_ref[:, sl]
            lse_ = jnp.concatenate([l_ref[:, sl], l_ref[:, sl]], axis=1)
            dd_ = jnp.concatenate([dd_ref[:, sl], dd_ref[:, sl]], axis=1)
            p = jnp.where(mask, jnp.exp(jnp.where(mask, qk * scale, NEG_INF) - lse_), 0.0)
            ds = (p * (dp - dd_) * scale).astype(BF16)
            dq_ref[:, sl] = jnp.dot(ds, k2, preferred_element_type=F32).astype(BF16)
            dk2 = lax.dot_general(ds, q, _TN, preferred_element_type=F32)
            dv2 = lax.dot_general(p.astype(BF16), da, _TN, preferred_element_type=F32)
            dk_ref[:, sl] = (dk2[BLK:] + dk_carry[:, sl]).astype(BF16)
            dv_ref[:, sl] = (dv2[BLK:] + dv_carry[:, sl]).astype(BF16)
            dk_carry[:, sl] = dk2[:BLK]
            dv_carry[:, sl] = dv2[:BLK]

    blk = pl.BlockSpec((BLK, width), lambda r, i: (block_of(i), r))
    shape = jax.ShapeDtypeStruct((M, d * width), BF16)
    return pl.pallas_call(
        body, name=f"attn_bwd_d{d}", out_shape=(shape,) * 3, grid=(d, nb),
        in_specs=_attn_in_specs(width, block_of) + [blk, blk, blk], out_specs=(blk,) * 3,
        scratch_shapes=[pltpu.VMEM((BLK, width), F32), pltpu.VMEM((BLK, width), F32)],
        compiler_params=_cparams("arbitrary", "arbitrary"))(qkv, qkv, qkv, qkv, qkv, dattn, lse, dd)


def _dproj_join(dqs, dks, dvs, du, *, tr=256):
    S, SW = du.shape
    AW = dqs[0].shape[1]
    tr = min(tr, S)
    nd = len(DILATIONS)

    def body(*refs):
        du_ref, out_ref, scr = refs[3 * nd:]
        for part in range(3):
            total = functools.reduce(jnp.add, [_from_residues(scr, refs[part * nd + n], d)
                                               for n, d in enumerate(DILATIONS)])
            out_ref[:, part * AW:(part + 1) * AW] = total.astype(BF16)
        out_ref[:, 3 * AW:] = du_ref[...].astype(BF16)

    return pl.pallas_call(
        body, name="dproj_join", out_shape=jax.ShapeDtypeStruct((S, 3 * AW + SW), BF16), grid=(S // tr,),
        in_specs=[_residue_spec(tr, d, AW) for d in DILATIONS] * 3 + [_rows(tr, SW)],
        out_specs=_rows(tr, 3 * AW + SW), scratch_shapes=[_residue_scratch(tr, AW)],
        compiler_params=_cparams("parallel"))(*dqs, *dks, *dvs, du)


def _ssm_disc(lr, li, ldt):
    dt = jnp.exp(ldt)
    mag = jnp.exp(lr * dt)
    ar = mag * jnp.cos(li * dt)
    ai = mag * jnp.sin(li * dt)
    nr = ar - 1.0
    den = lr * lr + li * li
    return ar, ai, (nr * lr + ai * li) / den, (ai * lr - nr * li) / den


def _ssm_tile_powers(lr, li, ldt, reverse):
    t = lax.broadcasted_iota(jnp.int32, (TILE, 1), 0)
    n = (TILE - t if reverse else t + 1).astype(F32)
    dt = jnp.exp(ldt)
    mag = jnp.exp(n * (lr * dt))
    ang = n * (li * dt)
    return mag * jnp.cos(ang), mag * jnp.sin(ang) * (-1.0 if reverse else 1.0)


def _cmul(ar, ai, br, bi):
    return ar * br - ai * bi, ar * bi + ai * br


def _scan(xr, xi, ar, ai, pr, pi, cr, ci, reverse):
    T = xr.shape[0]
    sub = lax.broadcasted_iota(jnp.int32, xr.shape, 0) & (TILE - 1)
    sh = 1
    while sh < TILE:
        if reverse:
            keep = sub < TILE - sh
            sr, si = pltpu.roll(xr, T - sh, 0), pltpu.roll(xi, T - sh, 0)
        else:
            keep = sub >= sh
            sr, si = pltpu.roll(xr, sh, 0), pltpu.roll(xi, sh, 0)
        sr, si = jnp.where(keep, sr, 0.0), jnp.where(keep, si, 0.0)
        qr, qi = _cmul(ar, ai, sr, si)
        xr, xi = xr + qr, xi + qi
        ar, ai = _cmul(ar, ai, ar, ai)
        sh *= 2
    n = T // TILE
    out_r, out_i = [None] * n, [None] * n
    edge = 0 if reverse else TILE - 1
    for j in (reversed(range(n)) if reverse else range(n)):
        er, ei = _cmul(pr, pi, cr, ci)
        sr, si = xr[j * TILE:(j + 1) * TILE] + er, xi[j * TILE:(j + 1) * TILE] + ei
        out_r[j], out_i[j] = sr, si
        cr, ci = sr[edge:edge + 1], si[edge:edge + 1]
    return jnp.concatenate(out_r, axis=0), jnp.concatenate(out_i, axis=0), cr, ci


def _ssm_specs(T, nch, rev):
    def t_of(c):
        return nch - 1 - c if rev else c
    tok = pl.BlockSpec((T, LANES), lambda j, c: (t_of(c), j))
    par = pl.BlockSpec((None, 1, STATE_LANES), lambda j, c: (j, 0, 0))
    bmat = pl.BlockSpec((None, LANES, STATE_LANES), lambda j, c: (j, 0, 0))
    cmat = pl.BlockSpec((None, STATE_LANES, LANES), lambda j, c: (j, 0, 0))
    dvec = pl.BlockSpec((1, LANES), lambda j, c: (0, j))
    return tok, par, bmat, cmat, dvec


def _ssm_fwd(u, lr_e, li_e, ldt_e, bre_e, bim_e, cre_e, cim_e, d_skip):
    S, SW = u.shape
    T = min(SSM_CHUNK, S)
    nch, nbk = S // T, SW // LANES
    tok, par, bmat, cmat, dvec = _ssm_specs(T, nch, False)
    state_spec = pl.BlockSpec((T, STATE_LANES), lambda j, c: (c, j))
    carry_spec = pl.BlockSpec((None, 1, STATE_LANES), lambda j, c: (c, 0, j))

    def body(u_ref, lr_ref, li_ref, ldt_ref, bre_ref, bim_ref, cre_ref, cim_ref, d_ref,
             y_ref, y2_ref, sr_ref, si_ref, er_ref, ei_ref, bbr, bbi, a_scr, pw, carry):
        c = pl.program_id(1)

        @pl.when(c == 0)
        def _():
            lr, li, ldt = lr_ref[...], li_ref[...], ldt_ref[...]
            ar, ai, kr, ki = _ssm_disc(lr, li, ldt)
            a_scr[0], a_scr[1] = ar, ai
            bbr[...] = (kr * bre_ref[...] - ki * bim_ref[...]).astype(BF16)
            bbi[...] = (kr * bim_ref[...] + ki * bre_ref[...]).astype(BF16)
            pw[0], pw[1] = _ssm_tile_powers(lr, li, ldt, False)
            carry[...] = jnp.zeros_like(carry)

        u_ = u_ref[...]
        ub = u_.astype(BF16)
        sr, si, cr, ci = _scan(jnp.dot(ub, bbr[...], preferred_element_type=F32),
                               jnp.dot(ub, bbi[...], preferred_element_type=F32),
                               a_scr[0], a_scr[1], pw[0], pw[1], carry[0], carry[1], False)
        carry[0], carry[1] = cr, ci
        er_ref[...], ei_ref[...] = cr, ci
        sr_ref[...], si_ref[...] = sr, si
        y0 = (jnp.dot(sr.astype(BF16), cre_ref[...].astype(BF16), preferred_element_type=F32)
              - jnp.dot(si.astype(BF16), cim_ref[...].astype(BF16), preferred_element_type=F32))
        y1 = y0 + d_ref[...] * u_
        y_ref[...] = y1
        y2_ref[...] = _gelu(y1).astype(BF16)

    states = jax.ShapeDtypeStruct((S, nbk * STATE_LANES), F32)
    ends = jax.ShapeDtypeStruct((nch, 1, nbk * STATE_LANES), F32)
    return pl.pallas_call(
        body, name="ssm_fwd",
        out_shape=(jax.ShapeDtypeStruct((S, SW), F32), jax.ShapeDtypeStruct((S, SW), BF16), states, states, ends, ends),
        grid=(nbk, nch), in_specs=[tok, par, par, par, bmat, bmat, cmat, cmat, dvec],
        out_specs=(tok, tok, state_spec, state_spec, carry_spec, carry_spec),
        scratch_shapes=[pltpu.VMEM((LANES, STATE_LANES), BF16), pltpu.VMEM((LANES, STATE_LANES), BF16),
                        pltpu.VMEM((2, 1, STATE_LANES), F32), pltpu.VMEM((2, TILE, STATE_LANES), F32),
                        pltpu.VMEM((2, 1, STATE_LANES), F32)],
        compiler_params=_cparams("arbitrary", "arbitrary"),
    )(u, lr_e, li_e, ldt_e, bre_e, bim_e, cre_e, cim_e, d_skip)


def _ssm_bwd(u, y1, dy2a, dy2b, st_r, st_i, ends_r, ends_i, lr_e, li_e, ldt_e, bre_e, bim_e, cre_e, cim_e, d_skip):
    S, SW = u.shape
    T = min(SSM_CHUNK, S)
    nch, nbk = S // T, SW // LANES
    tok, par, bmat, cmat, dvec = _ssm_specs(T, nch, True)
    state_spec = pl.BlockSpec((T, STATE_LANES), lambda j, c: (nch - 1 - c, j))
    prev_spec = pl.BlockSpec((None, 1, STATE_LANES), lambda j, c: (jnp.maximum(nch - 2 - c, 0), 0, j))
    acc8 = pl.BlockSpec((None, 8, STATE_LANES), lambda j, c: (j, 0, 0))
    dd8 = pl.BlockSpec((None, 8, LANES), lambda j, c: (j, 0, 0))

    def body(u_ref, y_ref, da_ref, db_ref, sr_ref, si_ref, pr_ref, pi_ref, lr_ref, li_ref, ldt_ref,
             bre_ref, bim_ref, cre_ref, cim_ref, d_ref,
             du_ref, dar_ref, dai_ref, dcr_ref, dci_ref, dbr_ref, dbi_ref, ddk_ref,
             bbr, bbi, a_scr, pw, carry):
        c = pl.program_id(1)

        @pl.when(c == 0)
        def _():
            lr, li, ldt = lr_ref[...], li_ref[...], ldt_ref[...]
            ar, ai, kr, ki = _ssm_disc(lr, li, ldt)
            a_scr[0], a_scr[1] = ar, -ai
            bbr[...] = (kr * bre_ref[...] - ki * bim_ref[...]).astype(BF16)
            bbi[...] = (kr * bim_ref[...] + ki * bre_ref[...]).astype(BF16)
            pw[0], pw[1] = _ssm_tile_powers(lr, li, ldt, True)
            carry[...] = jnp.zeros_like(carry)
            for ref in (dar_ref, dai_ref, dcr_ref, dci_ref, dbr_ref, dbi_ref, ddk_ref):
                ref[...] = jnp.zeros_like(ref)

        u_ = u_ref[...]
        ub = u_.astype(BF16)
        dy1 = (da_ref[...] + db_ref[...]) * _gelu_grad(y_ref[...])
        dyb = dy1.astype(BF16)

        sr, si = sr_ref[...], si_ref[...]
        has_prev = c < nch - 1
        s0r = jnp.where(has_prev, pr_ref[...], 0.0)
        s0i = jnp.where(has_prev, pi_ref[...], 0.0)

        cre_b, cim_b = cre_ref[...].astype(BF16), cim_ref[...].astype(BF16)
        gr, gi, cr, ci = _scan(lax.dot_general(dyb, cre_b, _NT, preferred_element_type=F32),
                               -lax.dot_general(dyb, cim_b, _NT, preferred_element_type=F32),
                               a_scr[0], a_scr[1], pw[0], pw[1], carry[0], carry[1], True)
        carry[0], carry[1] = cr, ci

        row = lax.broadcasted_iota(jnp.int32, (T, STATE_LANES), 0)
        spr = jnp.where(row == 0, s0r, pltpu.roll(sr, 1, 0))
        spi = jnp.where(row == 0, s0i, pltpu.roll(si, 1, 0))

        def fold(a):
            return jnp.sum(a.reshape(T // 8, 8, a.shape[-1]), axis=0)

        dar_ref[...] += fold(gr * spr + gi * spi)
        dai_ref[...] += fold(gi * spr - gr * spi)
        srb, sib, grb, gib = sr.astype(BF16), si.astype(BF16), gr.astype(BF16), gi.astype(BF16)
        dcr_ref[...] += lax.dot_general(srb, dyb, _TN, preferred_element_type=F32)
        dci_ref[...] -= lax.dot_general(sib, dyb, _TN, preferred_element_type=F32)
        dbr_ref[...] += lax.dot_general(ub, grb, _TN, preferred_element_type=F32)
        dbi_ref[...] += lax.dot_general(ub, gib, _TN, preferred_element_type=F32)
        du_ref[...] = (lax.dot_general(grb, bbr[...], _NT, preferred_element_type=F32)
                       + lax.dot_general(gib, bbi[...], _NT, preferred_element_type=F32)
                       + dy1 * d_ref[...])
        ddk_ref[...] += fold(dy1 * u_)

    return pl.pallas_call(
        body, name="ssm_bwd",
        out_shape=(jax.ShapeDtypeStruct((S, SW), F32),
                   jax.ShapeDtypeStruct((nbk, 8, STATE_LANES), F32), jax.ShapeDtypeStruct((nbk, 8, STATE_LANES), F32),
                   jax.ShapeDtypeStruct((nbk, STATE_LANES, LANES), F32), jax.ShapeDtypeStruct((nbk, STATE_LANES, LANES), F32),
                   jax.ShapeDtypeStruct((nbk, LANES, STATE_LANES), F32), jax.ShapeDtypeStruct((nbk, LANES, STATE_LANES), F32),
                   jax.ShapeDtypeStruct((nbk, 8, LANES), F32)),
        grid=(nbk, nch),
        in_specs=[tok, tok, tok, tok, state_spec, state_spec, prev_spec, prev_spec, par, par, par,
                  bmat, bmat, cmat, cmat, dvec],
        out_specs=(tok, acc8, acc8, cmat, cmat, bmat, bmat, dd8),
        scratch_shapes=[pltpu.VMEM((LANES, STATE_LANES), BF16), pltpu.VMEM((LANES, STATE_LANES), BF16),
                        pltpu.VMEM((2, 1, STATE_LANES), F32), pltpu.VMEM((2, TILE, STATE_LANES), F32),
                        pltpu.VMEM((2, 1, STATE_LANES), F32)],
        compiler_params=_cparams("arbitrary", "arbitrary"),
    )(u, y1, dy2a, dy2b, st_r, st_i, ends_r, ends_i, lr_e, li_e, ldt_e, bre_e, bim_e, cre_e, cim_e, d_skip)


def _ssm_param_bwd(dar8, dai8, dbr_e, dbi_e, lr_e, li_e, ldt_e, bre_e, bim_e):
    nbk = lr_e.shape[0]
    par = pl.BlockSpec((None, 1, STATE_LANES), lambda j: (j, 0, 0))
    acc8 = pl.BlockSpec((None, 8, STATE_LANES), lambda j: (j, 0, 0))
    bmat = pl.BlockSpec((None, LANES, STATE_LANES), lambda j: (j, 0, 0))

    def body(dar_ref, dai_ref, dbr_ref, dbi_ref, lr_ref, li_ref, ldt_ref, bre_ref, bim_ref,
             dlr_ref, dli_ref, dldt_ref, dbre_ref, dbim_ref):
        lr, li, ldt = lr_ref[...], li_ref[...], ldt_ref[...]
        (ar, ai, kr, ki), vjp = jax.vjp(_ssm_disc, lr, li, ldt)
        dbr, dbi, bre, bim = dbr_ref[...], dbi_ref[...], bre_ref[...], bim_ref[...]
        dbre_ref[...] = kr * dbr + ki * dbi
        dbim_ref[...] = kr * dbi - ki * dbr
        dkr = _colsum(dbr * bre + dbi * bim)
        dki = _colsum(dbi * bre - dbr * bim)
        dlr, dli, dldt = vjp((_colsum(dar_ref[...]), _colsum(dai_ref[...]), dkr, dki))
        dlr_ref[...] = dlr
        dli_ref[...] = dli
        tot = jnp.broadcast_to(dldt, (8, STATE_LANES))
        sh = 1
        while sh < SSM_P:
            tot = tot + pltpu.roll(tot, STATE_LANES - sh, 1)
            sh *= 2
        dldt_ref[...] = tot[:1]

    vec = jax.ShapeDtypeStruct((nbk, 1, STATE_LANES), F32)
    mat = jax.ShapeDtypeStruct((nbk, LANES, STATE_LANES), F32)
    return pl.pallas_call(
        body, name="ssm_param_bwd", out_shape=(vec, vec, vec, mat, mat), grid=(nbk,),
        in_specs=[acc8, acc8, bmat, bmat, par, par, par, bmat, bmat],
        out_specs=(par, par, par, bmat, bmat), compiler_params=_cparams("parallel"),
    )(dar8, dai8, dbr_e, dbi_e, lr_e, li_e, ldt_e, bre_e, bim_e)


def _expand_b(b):
    G = b.shape[0]
    bt = b.transpose(0, 2, 1).reshape(G // GROUPS_PER_BLOCK, GROUPS_PER_BLOCK, SSM_C, SSM_P)
    eye = jnp.eye(GROUPS_PER_BLOCK, dtype=b.dtype)
    return (bt[:, :, :, None, :] * eye[None, :, None, :, None]).reshape(G // GROUPS_PER_BLOCK, LANES, STATE_LANES)


def _collapse_b(be):
    nbk = be.shape[0]
    eye = jnp.eye(GROUPS_PER_BLOCK, dtype=be.dtype)
    d5 = be.reshape(nbk, GROUPS_PER_BLOCK, SSM_C, GROUPS_PER_BLOCK, SSM_P)
    d4 = (d5 * eye[None, :, None, :, None]).sum(axis=3)
    return d4.transpose(0, 1, 3, 2).reshape(nbk * GROUPS_PER_BLOCK, SSM_P, SSM_C)


def _expand_c(cm):
    G = cm.shape[0]
    ct = cm.transpose(0, 2, 1).reshape(G // GROUPS_PER_BLOCK, GROUPS_PER_BLOCK, SSM_P, SSM_C)
    eye = jnp.eye(GROUPS_PER_BLOCK, dtype=cm.dtype)
    return (ct[:, :, :, None, :] * eye[None, :, None, :, None]).reshape(G // GROUPS_PER_BLOCK, STATE_LANES, LANES)


def _collapse_c(ce):
    nbk = ce.shape[0]
    eye = jnp.eye(GROUPS_PER_BLOCK, dtype=ce.dtype)
    d5 = ce.reshape(nbk, GROUPS_PER_BLOCK, SSM_P, GROUPS_PER_BLOCK, SSM_C)
    d4 = (d5 * eye[None, :, None, :, None]).sum(axis=3)
    return d4.transpose(0, 1, 3, 2).reshape(nbk * GROUPS_PER_BLOCK, SSM_C, SSM_P)


def _place():
    x, y, c = lax.axis_index("x"), lax.axis_index("y"), lax.axis_index("c")
    return x, y, c


def _other_chips(x, y):
    return [(1 - x, y), (x, 1 - y), (1 - x, 1 - y)]


_ANY = pl.BlockSpec(memory_space=pl.ANY)


_HBM = pl.BlockSpec(memory_space=pltpu.HBM)
_SEM = pl.BlockSpec(memory_space=pltpu.SEMAPHORE)
_EFFECT = pltpu.SideEffectType.DATAFLOW_SIDE_EFFECTING
_TOKEN = jax.ShapeDtypeStruct((8, LANES), F32)


def _hbm(a):
    return pltpu.with_memory_space_constraint(a, pltpu.HBM)


def _place_own(src, *, gather, name, tr=512):
    R, C = src.shape[-2:]
    tr = min(tr, R)
    x, y, _ = _place()
    me = (2 * x + y).astype(jnp.int32).reshape(1)

    def body(me_ref, s_ref, o_ref):
        o_ref[...] = s_ref[...].astype(BF16)

    own = pl.BlockSpec((None, tr, C), lambda i, me_ref: (me_ref[0], i, 0))
    grid_spec = pltpu.PrefetchScalarGridSpec(
        num_scalar_prefetch=1, grid=(R // tr,),
        in_specs=[pl.BlockSpec((tr, C), lambda i, me_ref: (i, 0)) if gather else own], out_specs=own)
    return pl.pallas_call(
        body, name=name, grid_spec=grid_spec, out_shape=jax.ShapeDtypeStruct((N_CHIPS, R, C), BF16),
        compiler_params=_cparams("parallel"))(me, src)


def _exchange_copy(src_slot, land_slot, send, recv, k, j, peer, c):
    return pltpu.make_async_remote_copy(
        src_ref=src_slot, dst_ref=land_slot, send_sem=send.at[3 * k + j], recv_sem=recv.at[3 * k + j],
        device_id=(peer[0], peer[1], c), device_id_type=MESH)


def _exchange_start(lands, srcs, groups, *, name):
    n, ng = len(lands), len(groups)
    bufs = list(lands) + list(srcs)
    nb = len(bufs)

    def body(*refs):
        lnd, src, sems = refs[:n], refs[n:nb], refs[nb:nb + 2 * ng]
        token = refs[2 * nb + 2 * ng]
        x, y, c = _place()
        me = 2 * x + y
        for gi, group in enumerate(groups):
            for k, w in enumerate(group):
                for j, peer in enumerate(_other_chips(x, y)):
                    if src:
                        sent, dst = src[w].at[2 * peer[0] + peer[1]], lnd[w].at[me]
                    else:
                        sent = dst = lnd[w].at[me, c]
                    _exchange_copy(sent, dst, sems[2 * gi], sems[2 * gi + 1], k, j, peer, c).start()
        token[...] = jnp.zeros_like(token)

    sem_shapes = [pltpu.SemaphoreType.DMA((3 * len(g),)) for g in groups for _ in range(2)]
    res = pl.pallas_call(
        body, name=name,
        out_shape=sem_shapes + [pltpu.HBM(a.shape, a.dtype) for a in bufs] + [_TOKEN],
        in_specs=[_HBM] * nb,
        out_specs=[_SEM] * (2 * ng) + [_HBM] * nb + [pl.BlockSpec(memory_space=pltpu.VMEM)],
        input_output_aliases={i: 2 * ng + i for i in range(nb)},
        compiler_params=pltpu.CompilerParams(has_side_effects=_EFFECT),
    )(*[_hbm(a) for a in bufs])
    sems = [(res[2 * gi], res[2 * gi + 1]) for gi in range(ng)]
    return sems, res[2 * ng:2 * ng + n], res[2 * ng + n:2 * ng + nb], res[-1]


def _exchange_wait(lands, srcs, sems, after, *, name):
    n = len(lands)
    bufs = list(lands) + list(srcs)
    nb = len(bufs)
    send_sems, recv_sems = sems

    def body(*refs):
        lnd, src, send, recv = refs[:n], refs[n:nb], refs[nb], refs[nb + 1]
        x, y, c = _place()
        for k in range(n):
            for j, peer in enumerate(_other_chips(x, y)):
                slot = 2 * peer[0] + peer[1]
                if src:
                    copy = _exchange_copy(src[k].at[slot], lnd[k].at[slot], send, recv, k, j, peer, c)
                else:
                    copy = _exchange_copy(lnd[k].at[slot, c], lnd[k].at[slot, c], send, recv, k, j, peer, c)
                copy.wait_send()
                copy.wait_recv()

    res = pl.pallas_call(
        body, name=name, out_shape=[pltpu.HBM(a.shape, a.dtype) for a in bufs],
        in_specs=[_HBM] * nb + [_SEM, _SEM, _ANY], out_specs=[_HBM] * nb,
        input_output_aliases={i: i for i in range(nb)},
        compiler_params=pltpu.CompilerParams(has_side_effects=_EFFECT),
    )(*bufs, send_sems, recv_sems, after)
    return res[:n]


def _pair_fill(lands, *, name):
    n = len(lands)

    def body(*refs):
        ins, outs, send, recv = refs[:n], refs[n:2 * n], refs[2 * n], refs[2 * n + 1]
        x, y, c = _place()
        for w in range(n):
            for j, (px, py) in enumerate(_other_chips(x, y)):
                slot = 2 * px + py
                pltpu.make_async_remote_copy(
                    src_ref=ins[w].at[slot, c], dst_ref=outs[w].at[slot, c], send_sem=send.at[3 * w + j],
                    recv_sem=recv.at[3 * w + j], device_id=(x, y, 1 - c), device_id_type=MESH).start()
        for w in range(n):
            for j, (px, py) in enumerate(_other_chips(x, y)):
                slot = 2 * px + py
                arrival = pltpu.make_async_remote_copy(
                    src_ref=ins[w].at[slot, c], dst_ref=outs[w].at[slot, 1 - c], send_sem=send.at[3 * w + j],
                    recv_sem=recv.at[3 * w + j], device_id=(x, y, 1 - c), device_id_type=MESH)
                arrival.wait_recv()
                arrival.wait_send()

    return pl.pallas_call(
        body, name=name, out_shape=[jax.ShapeDtypeStruct(a.shape, a.dtype) for a in lands],
        in_specs=[_ANY] * n, out_specs=[_ANY] * n, input_output_aliases={i: i for i in range(n)},
        scratch_shapes=[pltpu.SemaphoreType.DMA((3 * n,)), pltpu.SemaphoreType.DMA((3 * n,))],
    )(*lands)


def _sum_partials(land, *, name, tr=256):
    _, R, C = land.shape
    tr = min(tr, R)

    def body(l_ref, o_ref):
        acc = l_ref[0].astype(F32)
        for k in range(1, N_CHIPS):
            acc = acc + l_ref[k].astype(F32)
        o_ref[...] = acc

    return pl.pallas_call(
        body, name=name, out_shape=jax.ShapeDtypeStruct((R, C), F32), grid=(R // tr,),
        in_specs=[pl.BlockSpec((N_CHIPS, tr, C), lambda i: (0, i, 0))], out_specs=_rows(tr, C),
        compiler_params=_cparams("parallel"))(land)


def _swap_with_sibling(sums, *, name):
    n = len(sums)

    def body(*refs):
        ins, outs = refs[:n], refs[n:2 * n]
        send_sems, recv_sems = refs[2 * n:]
        x, y, c = _place()
        copies = [pltpu.make_async_remote_copy(
            src_ref=ins[w], dst_ref=outs[w], send_sem=send_sems.at[w], recv_sem=recv_sems.at[w],
            device_id=(x, y, 1 - c), device_id_type=MESH) for w in range(n)]
        for cp in copies:
            cp.start()
        for cp in copies:
            cp.wait_recv()
            cp.wait_send()

    return pl.pallas_call(
        body, name=name,
        out_shape=[jax.ShapeDtypeStruct(s.shape, s.dtype) for s in sums],
        in_specs=[_ANY] * n, out_specs=[_ANY] * n,
        scratch_shapes=[pltpu.SemaphoreType.DMA((n,)), pltpu.SemaphoreType.DMA((n,))],
    )(*sums)


def _adamw_math(w, g, m, v):
    m = ADAM_B1 * m + (1.0 - ADAM_B1) * g
    v = ADAM_B2 * v + (1.0 - ADAM_B2) * (g * g)
    m_hat = m / (1.0 - ADAM_B1 ** ADAM_STEP)
    v_hat = v / (1.0 - ADAM_B2 ** ADAM_STEP)
    delta = -ADAM_LR * (m_hat / (jnp.sqrt(v_hat) + ADAM_EPS) + ADAM_WD * w)
    return delta, m, v


def _adamw_pair(mine, theirs, w, m, v, *, name, tr=128):
    R, C = w.shape
    tr = min(tr, R)

    def body(a_ref, b_ref, w_ref, m_ref, v_ref, g_ref, d_ref, nm_ref, nv_ref):
        g = a_ref[...] + b_ref[...]
        g_ref[...] = g
        d_ref[...], nm_ref[...], nv_ref[...] = _adamw_math(w_ref[...], g, m_ref[...], v_ref[...])

    shape = jax.ShapeDtypeStruct((R, C), F32)
    return pl.pallas_call(
        body, name=name, out_shape=(shape,) * 4, grid=(R // tr,),
        in_specs=[_rows(tr, C)] * 5, out_specs=(_rows(tr, C),) * 4,
        compiler_params=_cparams("parallel"))(mine, theirs, w, m, v)


def _all_reduce_small(packed):
    R = packed.shape[0]
    half = R // 2

    def body(x_ref, g_ref, sib_ref, pair_ref, land_ref, send_sems, recv_sems):
        x, y, c = _place()
        me = 2 * x + y
        sibling = (x, y, 1 - c)

        swap = pltpu.make_async_remote_copy(
            src_ref=x_ref, dst_ref=sib_ref, send_sem=send_sems.at[0], recv_sem=recv_sems.at[0],
            device_id=sibling, device_id_type=MESH)
        swap.start()
        swap.wait()
        mine, theirs = x_ref[...], sib_ref[...]
        south = c == 0
        pair_ref[...] = jnp.where(south, mine, theirs) + jnp.where(south, theirs, mine)

        land_ref[me] = pair_ref[c]
        for j, (px, py) in enumerate(_other_chips(x, y)):
            pltpu.make_async_remote_copy(
                src_ref=pair_ref.at[c], dst_ref=land_ref.at[me], send_sem=send_sems.at[1 + j],
                recv_sem=recv_sems.at[1 + j], device_id=(px, py, c), device_id_type=MESH).start()
        for j, (px, py) in enumerate(_other_chips(x, y)):
            arrival = pltpu.make_async_remote_copy(
                src_ref=pair_ref.at[c], dst_ref=land_ref.at[2 * px + py], send_sem=send_sems.at[1 + j],
                recv_sem=recv_sems.at[1 + j], device_id=(px, py, c), device_id_type=MESH)
            arrival.wait_recv()
            arrival.wait_send()
        total = land_ref[0]
        for k in range(1, N_CHIPS):
            total = total + land_ref[k]
        g_ref[c] = total

        give = pltpu.make_async_remote_copy(
            src_ref=g_ref.at[c], dst_ref=g_ref.at[c], send_sem=send_sems.at[4], recv_sem=recv_sems.at[4],
            device_id=sibling, device_id_type=MESH)
        give.start()
        take = pltpu.make_async_remote_copy(
            src_ref=g_ref.at[c], dst_ref=g_ref.at[1 - c], send_sem=send_sems.at[4], recv_sem=recv_sems.at[4],
            device_id=sibling, device_id_type=MESH)
        take.wait_recv()
        give.wait_send()

    vm = pl.BlockSpec(memory_space=pltpu.VMEM)
    return pl.pallas_call(
        body, name="all_reduce_small", out_shape=jax.ShapeDtypeStruct((2, half, LANES), F32),
        in_specs=[vm], out_specs=vm,
        scratch_shapes=[pltpu.VMEM((2, half, LANES), F32), pltpu.VMEM((2, half, LANES), F32),
                        pltpu.VMEM((N_CHIPS, half, LANES), F32),
                        pltpu.SemaphoreType.DMA((5,)), pltpu.SemaphoreType.DMA((5,))],
        compiler_params=pltpu.CompilerParams(vmem_limit_bytes=VMEM_LIMIT_BYTES),
    )(packed.reshape(2, half, LANES)).reshape(R, LANES)


def _adamw_small(g, w, m, v):
    R = g.shape[0]
    tr = PACK_ROWS

    def body(g_ref, w_ref, m_ref, v_ref, d_ref, nm_ref, nv_ref):
        d_ref[...], nm_ref[...], nv_ref[...] = _adamw_math(w_ref[...], g_ref[...], m_ref[...], v_ref[...])

    shape = jax.ShapeDtypeStruct((R, LANES), F32)
    return pl.pallas_call(
        body, name="adamw_small", out_shape=(shape,) * 3, grid=(R // tr,),
        in_specs=[_rows(tr, LANES)] * 4, out_specs=(_rows(tr, LANES),) * 3,
        compiler_params=_cparams("parallel"))(g, w, m, v)


def _pack(arrays):
    parts, layout = [], []
    for a in arrays:
        n = a.size
        rows = -(-n // (8 * LANES)) * 8
        flat = jnp.pad(a.reshape(-1).astype(F32), (0, rows * LANES - n))
        parts.append(flat.reshape(rows, LANES))
        layout.append((rows, n, a.shape))
    total = sum(r for r, _, _ in layout)
    parts.append(jnp.zeros((-total % PACK_ROWS, LANES), F32))
    return jnp.concatenate(parts, axis=0), layout


def _unpack(buf, layout):
    out, r0 = [], 0
    for rows, n, shape in layout:
        out.append(buf[r0:r0 + rows].reshape(-1)[:n].reshape(shape))
        r0 += rows
    return out


SMALL = ("mix_norm_pre", "lam_re", "lam_im", "log_dt", "ssm_b_re", "ssm_b_im", "ssm_c_re", "ssm_c_im",
         "ssm_d", "b_glu", "attn_out_norm", "ssm_out_norm", "mix_norm_post", "mlp_norm_pre",
         "mlp_norm_post", "ple_norm_pre", "ple_norm_post")
BIG = ("w_in", "w_glu", "w_out", "w_up", "w_down", "w_ple_gate", "w_ple_proj")
WEIGHTS = ("mix_norm_pre", "w_in", "lam_re", "lam_im", "log_dt", "ssm_b_re", "ssm_b_im", "ssm_c_re",
           "ssm_c_im", "ssm_d", "w_glu", "b_glu", "attn_out_norm", "ssm_out_norm", "w_out",
           "mix_norm_post", "mlp_norm_pre", "w_up", "w_down", "mlp_norm_post", "ple_norm_pre",
           "w_ple_gate", "w_ple_proj", "ple_norm_post")


def kernel(x, p, mix_norm_pre, w_in, lam_re, lam_im, log_dt, ssm_b_re, ssm_b_im, ssm_c_re, ssm_c_im, ssm_d, w_glu, b_glu, attn_out_norm, ssm_out_norm, w_out, mix_norm_post, mlp_norm_pre, w_up, w_down, mlp_norm_post, ple_norm_pre, w_ple_gate, w_ple_proj, ple_norm_post, loss_target, m_mix_norm_pre, m_w_in, m_lam_re, m_lam_im, m_log_dt, m_ssm_b_re, m_ssm_b_im, m_ssm_c_re, m_ssm_c_im, m_ssm_d, m_w_glu, m_b_glu, m_attn_out_norm, m_ssm_out_norm, m_w_out, m_mix_norm_post, m_mlp_norm_pre, m_w_up, m_w_down, m_mlp_norm_post, m_ple_norm_pre, m_w_ple_gate, m_w_ple_proj, m_ple_norm_post, v_mix_norm_pre, v_w_in, v_lam_re, v_lam_im, v_log_dt, v_ssm_b_re, v_ssm_b_im, v_ssm_c_re, v_ssm_c_im, v_ssm_d, v_w_glu, v_b_glu, v_attn_out_norm, v_ssm_out_norm, v_w_out, v_mix_norm_post, v_mlp_norm_pre, v_w_up, v_w_down, v_mlp_norm_post, v_ple_norm_pre, v_w_ple_gate, v_w_ple_proj, v_ple_norm_post):
    args = dict(locals())
    W = {n: args[n][0] for n in WEIGHTS}
    Mo = {n: args["m_" + n][0] for n in WEIGHTS}
    Vo = {n: args["v_" + n][0] for n in WEIGHTS}
    xs, ps, tgt = x[0], p[0, 0], loss_target[0]
    S, D = xs.shape
    SW = W["ssm_d"].shape[0]
    AW = W["attn_out_norm"].shape[0]
    heads = AW // HEAD_DIM
    G = SW // SSM_C
    nbk = SW // LANES
    assert W["w_in"].shape[1] * N_CHIPS == 3 * AW + SW and AW == SW

    row = lambda a: a.reshape(1, -1)

    ag_groups = (("w_in",), ("w_glu", "w_out"), ("w_up",), ("w_down", "w_ple_gate", "w_ple_proj"))
    ag_names = [n for g in ag_groups for n in g]
    def in_halves(a):
        return a.reshape(N_CHIPS, 2, a.shape[1] // 2, a.shape[2])

    ag_sems, ag_land, _, ag_token = _exchange_start(
        [in_halves(_place_own(W[n], gather=True, name="ag_place_" + n)) for n in ag_names], [],
        [[ag_names.index(n) for n in g] for g in ag_groups], name="ag_start")

    def gathered(gi, after):
        got = _exchange_wait([ag_land[ag_names.index(n)] for n in ag_groups[gi]], [], ag_sems[gi], after,
                             name=f"ag_wait_{gi}")
        got = _pair_fill(got, name=f"ag_pair_{gi}")
        return {n: a.reshape(N_CHIPS, -1, a.shape[-1]) for n, a in zip(ag_groups[gi], got)}

    lr_e = W["lam_re"].reshape(nbk, 1, STATE_LANES)
    li_e = W["lam_im"].reshape(nbk, 1, STATE_LANES)
    ldt_e = jnp.repeat(W["log_dt"], SSM_P).reshape(nbk, 1, STATE_LANES)
    bre_e, bim_e = _expand_b(W["ssm_b_re"]), _expand_b(W["ssm_b_im"])
    cre_e, cim_e = _expand_c(W["ssm_c_re"]), _expand_c(W["ssm_c_im"])
    d_row = row(W["ssm_d"])

    hn1 = _norm_cast(xs, row(W["mix_norm_pre"]) + ag_token[0, 0], name="norm_in")
    w_in_f = gathered(0, hn1)["w_in"]
    qkv_b = _proj_qkv(hn1, w_in_f)
    u = _matmul(hn1, w_in_f, name="proj_u", b_shards=N_CHIPS, b_cols=(3 * AW, SW))
    outs, lses = zip(*[_attn_fwd(qb, d, heads) for d, qb in zip(DILATIONS, qkv_b)])
    y1, y2b, st_r, st_i, ends_r, ends_i = _ssm_fwd(u, lr_e, li_e, ldt_e, bre_e, bim_e, cre_e, cim_e, d_row)
    full = gathered(1, y2b)
    w_glu_f = full["w_glu"].reshape(SW, SW)
    w_out_f = full["w_out"].reshape(AW + SW, D)
    z = _matmul(y2b, w_glu_f, name="glu_z")
    attn, lse_b, mixed = _mix_fwd(outs, lses, y1, z, row(W["b_glu"]), row(W["attn_out_norm"]), row(W["ssm_out_norm"]))
    mo = _matmul(mixed, w_out_f, name="mix_out")
    h1, hn2 = _res_norm(xs, mo, row(W["mix_norm_post"]), row(W["mlp_norm_pre"]), name="res_mix")
    w_up_f = gathered(2, hn2)["w_up"]
    up, act = _matmul(hn2, w_up_f, name="mlp_up", b_shards=N_CHIPS, relu2=True)
    full = gathered(3, act)
    w_down_f = full["w_down"].reshape(-1, D)
    w_pg_f = full["w_ple_gate"].reshape(D, D)
    w_pp_f = full["w_ple_proj"]
    ff = _matmul(act, w_down_f, name="mlp_down")
    h2, hn3 = _res_norm(h1, ff, row(W["mlp_norm_post"]), row(W["ple_norm_pre"]), name="res_mlp")
    gl = _matmul(hn3, w_pg_f, name="ple_gate")
    e = _matmul(ps.astype(BF16), w_pp_f, name="ple_proj", b_shards=N_CHIPS)

    dh3, dgl, de, loss_part, dg_ple_post = _final(h2, gl, e, row(W["ple_norm_post"]), tgt)
    gW = {}
    out_g, out_d, out_m, out_v = {}, {}, {}, {}

    def scatter_start(names, tag):
        parts = [gW[n] if gW[n].ndim == 3 else gW[n].reshape((N_CHIPS, -1, gW[n].shape[1])) for n in names]
        sems, land, src, token = _exchange_start(
            [_place_own(part, gather=False, name="rs_place_" + n) for n, part in zip(names, parts)], parts,
            [list(range(len(names)))], name=f"rs_start_{tag}")
        return (names, sems[0], land, src), token

    def scatter_finish(batch, after, tag):
        names, sems, land, src = batch
        landed = _exchange_wait(land, src, sems, after, name=f"rs_wait_{tag}")
        sums = [_sum_partials(l, name="sum_" + n) for n, l in zip(names, landed)]
        theirs = _swap_with_sibling(sums, name=f"swap_{tag}")
        for n, a, b in zip(names, sums, theirs):
            out_g[n], out_d[n], out_m[n], out_v[n] = _adamw_pair(a, b, W[n], Mo[n], Vo[n], name="adamw_" + n)

    gW["w_ple_proj"] = _matmul(ps.astype(BF16), de, name="d_w_ple_proj", ta=True, out_dtype=BF16, out_shards=N_CHIPS)
    gW["w_ple_gate"] = _matmul(hn3, dgl, name="d_w_ple_gate", ta=True, out_dtype=BF16)
    dhn3 = _matmul(dgl, w_pg_f, name="d_hn3", tb=True)
    dh2, dff, dg_ple_pre, dg_mlp_post = _bwd_res_norm(
        dh3, dhn3, h2, row(W["ple_norm_pre"]), ff, row(W["mlp_norm_post"]), name="bwd_res_mlp")
    gW["w_down"] = _matmul(act, dff, name="d_w_down", ta=True, out_dtype=BF16)
    batch1, token1 = scatter_start(("w_ple_proj", "w_ple_gate", "w_down"), 1)
    dup = _matmul(dff, w_down_f, name="d_up", tb=True, after=token1, relu2_of=up, out_dtype=BF16)
    gW["w_up"] = _matmul(hn2, dup, name="d_w_up", ta=True, out_dtype=BF16, out_shards=N_CHIPS)
    batch2, token2 = scatter_start(("w_up",), 2)
    dhn2 = _matmul(dup, w_up_f, name="d_hn2", tb=True, b_shards=N_CHIPS, after=token2)
    dh1, dmo, dg_mlp_pre, dg_mix_post = _bwd_res_norm(
        dh2, dhn2, h1, row(W["mlp_norm_pre"]), mo, row(W["mix_norm_post"]), name="bwd_res_mix")
    gW["w_out"] = _matmul(mixed, dmo, name="d_w_out", ta=True, out_dtype=BF16)
    dmixed = _matmul(dmo, w_out_f, name="d_mixed", tb=True)
    dattn_b, dd_b, dz, dy2a, dg_attn, dg_ssm, db_glu = _mix_bwd(
        dmixed, attn, y1, z, row(W["b_glu"]), row(W["attn_out_norm"]), row(W["ssm_out_norm"]))
    gW["w_glu"] = _matmul(y2b, dz, name="d_w_glu", ta=True, out_dtype=BF16)
    batch3, token3 = scatter_start(("w_out", "w_glu"), 3)
    dy2b = _matmul(dz, w_glu_f, name="d_y2", tb=True, after=token3)
    du, dar8, dai8, dcr_e, dci_e, dbr_e, dbi_e, dd8 = _ssm_bwd(
        u, y1, dy2a, dy2b, st_r, st_i, ends_r, ends_i, lr_e, li_e, ldt_e, bre_e, bim_e, cre_e, cim_e, d_row)
    scatter_finish(batch1, du, 1)
    dlr_e, dli_e, dldt_e, dbre_e, dbim_e = _ssm_param_bwd(dar8, dai8, dbr_e, dbi_e, lr_e, li_e, ldt_e, bre_e, bim_e)

    dqs, dks, dvs = zip(*[_attn_bwd(qb, da, l, dd_, d, heads)
                          for d, qb, da, l, dd_ in zip(DILATIONS, qkv_b, dattn_b, lse_b, dd_b)])
    dproj = _dproj_join(dqs, dks, dvs, du)
    scatter_finish(batch2, dproj, 2)
    scatter_finish(batch3, dproj, 3)
    gW["w_in"] = _matmul(hn1, dproj, name="d_w_in", ta=True, out_dtype=BF16, out_shards=N_CHIPS)
    batch4, token4 = scatter_start(("w_in",), 4)
    dhn1 = _matmul(dproj, w_in_f, name="d_hn1", tb=True, b_shards=N_CHIPS, after=token4)
    grad_x, dg_mix_pre = _bwd_first(dh1, dhn1, xs, row(W["mix_norm_pre"]))
    scatter_finish(batch4, grad_x, 4)

    small_g = {
        "mix_norm_pre": dg_mix_pre, "lam_re": dlr_e.reshape(G, SSM_P), "lam_im": dli_e.reshape(G, SSM_P),
        "log_dt": dldt_e.reshape(G, SSM_P)[:, 0], "ssm_b_re": _collapse_b(dbre_e), "ssm_b_im": _collapse_b(dbim_e),
        "ssm_c_re": _collapse_c(dcr_e), "ssm_c_im": _collapse_c(dci_e), "ssm_d": dd8.sum(axis=1).reshape(-1),
        "b_glu": db_glu, "attn_out_norm": dg_attn, "ssm_out_norm": dg_ssm, "mix_norm_post": dg_mix_post,
        "mlp_norm_pre": dg_mlp_pre, "mlp_norm_post": dg_mlp_post, "ple_norm_pre": dg_ple_pre,
        "ple_norm_post": dg_ple_post,
    }
    g_pack, layout = _pack([small_g[n].reshape(W[n].shape) for n in SMALL])
    w_pack, _ = _pack([W[n] for n in SMALL])
    m_pack, _ = _pack([Mo[n] for n in SMALL])
    v_pack, _ = _pack([Vo[n] for n in SMALL])
    g_sum = _all_reduce_small(g_pack)
    packed = (g_sum,) + tuple(_adamw_small(g_sum, w_pack, m_pack, v_pack))
    for dst, buf in zip((out_g, out_d, out_m, out_v), packed):
        dst.update(zip(SMALL, _unpack(buf, layout)))

    loss = lax.psum(loss_part[0, 0], ("x", "y", "c"))
    lead = lambda a: a[None]
    return (loss, grad_x[None],
            *[lead(out_g[n]) for n in WEIGHTS], *[lead(out_d[n]) for n in WEIGHTS],
            *[lead(out_m[n]) for n in WEIGHTS], *[lead(out_v[n]) for n in WEIGHTS])
```

```python
import functools
import math

import jax
import jax.numpy as jnp
from jax import lax
from jax.experimental import pallas as pl
from jax.experimental.pallas import tpu as pltpu

F32 = jnp.float32
BF16 = jnp.bfloat16
MESH = pl.DeviceIdType.MESH

RMS_EPS = 1e-6
NEG_INF = -1e30
HEAD_DIM = 128
BLK = 128
DILATIONS = (1, 4, 16)
ATTN_LOOKAHEAD = 3
SSM_C = 16
SSM_P = 64
LANES = 128
GROUPS_PER_BLOCK = LANES // SSM_C
STATE_LANES = GROUPS_PER_BLOCK * SSM_P
SSM_CHUNK = 512
TILE = 8
ADAM_LR, ADAM_B1, ADAM_B2, ADAM_EPS, ADAM_WD, ADAM_STEP = 1e-3, 0.9, 0.999, 1e-8, 0.01, 10
VMEM_LIMIT_BYTES = 56 * 1024 * 1024
N_CHIPS = 4
N_DEV = 8
PACK_ROWS = 256


def _cparams(*sem):
    return pltpu.CompilerParams(dimension_semantics=sem or None, vmem_limit_bytes=VMEM_LIMIT_BYTES)


def _rows(tr, w):
    return pl.BlockSpec((tr, w), lambda i: (i, 0))


def _vec(w):
    return pl.BlockSpec((1, w), lambda i: (0, 0))


def _sigmoid(x):
    return 1.0 / (1.0 + jnp.exp(-x))


def _gelu(x):
    c = math.sqrt(2.0 / math.pi)
    return 0.5 * x * (1.0 + jnp.tanh(c * (x + 0.044715 * x * x * x)))


def _gelu_grad(x):
    c = math.sqrt(2.0 / math.pi)
    th = jnp.tanh(c * (x + 0.044715 * x * x * x))
    return 0.5 * (1.0 + th) + 0.5 * x * (1.0 - th * th) * c * (1.0 + 3.0 * 0.044715 * x * x)


def _rms(x, g):
    r = lax.rsqrt(jnp.mean(x * x, axis=-1, keepdims=True) + RMS_EPS)
    return x * r * g


def _rms_bwd(dy, x, g):
    r = lax.rsqrt(jnp.mean(x * x, axis=-1, keepdims=True) + RMS_EPS)
    n = x * r
    dn = dy * g
    dx = r * (dn - n * jnp.mean(dn * n, axis=-1, keepdims=True))
    return dx, dy * n


def _colsum(a):
    return jnp.sum(a, axis=0, keepdims=True)


def _first(i):
    return i == 0


def _matmul(a, b, *, name, ta=False, tb=False, out_dtype=F32, b_shards=1, out_shards=1, b_cols=None,
            after=None, relu2=False, relu2_of=None, tm=1024, tn=1024, tk=2048):
    if ta:
        K, M = a.shape
    else:
        M, K = a.shape
    if b_shards > 1:
        rows, cols = b.shape[1], b.shape[2] * b_shards
    else:
        rows, cols = b.shape
    N, Kb = (rows, cols) if tb else (cols, rows)
    assert K == Kb, (a.shape, b.shape, ta, tb)
    col0 = 0
    if b_cols is not None:
        assert not tb
        col0, N = b_cols
    tm, tn, tk = min(tm, M), min(tn, N), min(tk, K)
    if b_shards > 1:
        shard_cols = cols // b_shards
        if tb:
            tk = min(tk, shard_cols)
        else:
            tn = min(tn, shard_cols)
    if out_shards > 1:
        tn = min(tn, N // out_shards)
    assert M % tm == 0 and N % tn == 0 and K % tk == 0 and col0 % tn == 0
    nk = K // tk
    j0 = col0 // tn

    a_spec = (pl.BlockSpec((tk, tm), lambda i, j, k: (k, i)) if ta
              else pl.BlockSpec((tm, tk), lambda i, j, k: (i, k)))
    if b_shards > 1:
        if tb:
            per = shard_cols // tk
            b_spec = pl.BlockSpec((None, tn, tk), lambda i, j, k: (k // per, j, k % per))
        else:
            per = shard_cols // tn
            b_spec = pl.BlockSpec((None, tk, tn), lambda i, j, k: ((j + j0) // per, k, (j + j0) % per))
    else:
        b_spec = (pl.BlockSpec((tn, tk), lambda i, j, k: (j, k)) if tb
                  else pl.BlockSpec((tk, tn), lambda i, j, k: (k, j + j0)))
    if out_shards > 1:
        per_o = (N // out_shards) // tn
        out_shape = jax.ShapeDtypeStruct((out_shards, M, N // out_shards), out_dtype)
        out_spec = pl.BlockSpec((None, tm, tn), lambda i, j, k: (j // per_o, i, j % per_o))
    else:
        out_shape = jax.ShapeDtypeStruct((M, N), out_dtype)
        out_spec = pl.BlockSpec((tm, tn), lambda i, j, k: (i, j))
    dims = (((0 if ta else 1,), (1 if tb else 0,)), ((), ()))

    extra, extra_specs = [], []
    if relu2_of is not None:
        assert out_shards == 1 and relu2_of.shape == (M, N)
        extra.append(relu2_of)
        extra_specs.append(pl.BlockSpec((tm, tn), lambda i, j, k: (i, j)))
    if after is not None:
        extra.append(after)
        extra_specs.append(pl.BlockSpec(after.shape, lambda i, j, k: (0, 0)))
    n_in = 2 + len(extra)
    if relu2:
        assert out_shards == 1
        out_shape = (out_shape, jax.ShapeDtypeStruct((M, N), BF16))
        out_spec = (out_spec, out_spec)

    def finish(acc, refs):
        o_ref = refs[n_in]
        if relu2_of is not None:
            acc = acc * (2.0 * jnp.maximum(refs[2][...], 0.0))
        o_ref[...] = acc.astype(o_ref.dtype)
        if relu2:
            r = jnp.maximum(acc, 0.0)
            refs[n_in + 1][...] = (r * r).astype(BF16)

    def body(*refs):
        prod = lax.dot_general(refs[0][...], refs[1][...], dims, preferred_element_type=F32)
        if nk == 1:
            finish(prod, refs)
            return
        acc_ref = refs[-1]
        k = pl.program_id(2)

        @pl.when(k == 0)
        def _():
            acc_ref[...] = prod

        @pl.when(k > 0)
        def _():
            acc_ref[...] += prod

        @pl.when(k == nk - 1)
        def _():
            finish(acc_ref[...], refs)

    return pl.pallas_call(
        body, name=name, out_shape=out_shape, grid=(M // tm, N // tn, nk),
        in_specs=[a_spec, b_spec] + extra_specs, out_specs=out_spec,
        scratch_shapes=[pltpu.VMEM((tm, tn), F32)] if nk > 1 else [],
        compiler_params=_cparams("parallel", "parallel", "arbitrary"),
    )(a, b, *extra)


def _norm_cast(x, g, *, name, tr=256):
    S, D = x.shape
    tr = min(tr, S)

    def body(x_ref, g_ref, o_ref):
        o_ref[...] = _rms(x_ref[...], g_ref[...]).astype(BF16)

    return pl.pallas_call(
        body, name=name, out_shape=jax.ShapeDtypeStruct((S, D), BF16), grid=(S // tr,),
        in_specs=[_rows(tr, D), _vec(D)], out_specs=_rows(tr, D),
        compiler_params=_cparams("parallel"))(x, g)


def _res_norm(res, y, g_post, g_next, *, name, tr=256):
    S, D = res.shape
    tr = min(tr, S)

    def body(res_ref, y_ref, gp_ref, gn_ref, h_ref, hn_ref):
        h = res_ref[...] + _rms(y_ref[...], gp_ref[...])
        h_ref[...] = h
        hn_ref[...] = _rms(h, gn_ref[...]).astype(BF16)

    return pl.pallas_call(
        body, name=name,
        out_shape=(jax.ShapeDtypeStruct((S, D), F32), jax.ShapeDtypeStruct((S, D), BF16)),
        grid=(S // tr,), in_specs=[_rows(tr, D), _rows(tr, D), _vec(D), _vec(D)],
        out_specs=(_rows(tr, D), _rows(tr, D)), compiler_params=_cparams("parallel"))(res, y, g_post, g_next)


def _residue_spec(tr, d, w):
    return pl.BlockSpec((tr // d, d * w), lambda i: (i, 0))


def _residue_shape(S, d, w, dtype):
    return jax.ShapeDtypeStruct((S // d, d * w), dtype)


def _residue_scratch(rows, w):
    return pltpu.VMEM((w // LANES, rows, LANES), F32)


def _fill_strips(scr, val):
    for s in range(scr.shape[0]):
        scr[s] = val[:, s * LANES:(s + 1) * LANES]


def _strips_to_residues(scr, o_ref, d):
    strips, rows, _ = scr.shape
    for r in range(d):
        for s in range(strips):
            col = (r * strips + s) * LANES
            o_ref[:, col:col + LANES] = scr[s, pl.ds(r, rows // d, stride=d), :].astype(o_ref.dtype)


def _to_residues(scr, val, o_ref, d):
    if d == 1:
        o_ref[...] = val.astype(o_ref.dtype)
        return
    _fill_strips(scr, val)
    _strips_to_residues(scr, o_ref, d)


def _from_residues(scr, in_ref, d):
    if d == 1:
        return in_ref[...].astype(F32)
    strips, rows, _ = scr.shape
    for r in range(d):
        for s in range(strips):
            col = (r * strips + s) * LANES
            scr[s, pl.ds(r, rows // d, stride=d), :] = in_ref[:, col:col + LANES].astype(F32)
    return jnp.concatenate([scr[s] for s in range(strips)], axis=1)


def _mix_fwd(os, ls, y1, z, b_glu, g_attn, g_ssm, *, tr=128):
    S, SW = y1.shape
    AW = os[0].shape[1]
    tr = min(tr, S)
    nd = len(DILATIONS)

    def body(*refs):
        o_refs, l_refs = refs[:nd], refs[nd:2 * nd]
        y_ref, z_ref, b_ref, ga_ref, gs_ref, attn_ref = refs[2 * nd:2 * nd + 6]
        lse_refs = refs[2 * nd + 6:3 * nd + 6]
        mixed_ref, scr = refs[3 * nd + 6:]
        ls_ = [_from_residues(scr, l_refs[n], d) for n, d in enumerate(DILATIONS)]
        m = functools.reduce(jnp.maximum, ls_)
        es = [jnp.exp(l - m) for l in ls_]
        tot = functools.reduce(jnp.add, es)
        attn = functools.reduce(jnp.add, [e * _from_residues(scr, o_refs[n], d)
                                          for n, (e, d) in enumerate(zip(es, DILATIONS))]) / tot
        attn_ref[...] = attn
        lse = m + jnp.log(tot)
        for n, d in enumerate(DILATIONS):
            _to_residues(scr, lse, lse_refs[n], d)
        ssm = _gelu(y_ref[...]) * _sigmoid(z_ref[...] + b_ref[...])
        mixed_ref[:, :AW] = _rms(attn, ga_ref[...]).astype(BF16)
        mixed_ref[:, AW:] = _rms(ssm, gs_ref[...]).astype(BF16)

    res_in = [_residue_spec(tr, d, AW) for d in DILATIONS]
    res = pl.pallas_call(
        body, name="mix_fwd",
        out_shape=([jax.ShapeDtypeStruct((S, AW), F32)] + [_residue_shape(S, d, AW, F32) for d in DILATIONS]
                   + [jax.ShapeDtypeStruct((S, AW + SW), BF16)]),
        grid=(S // tr,),
        in_specs=res_in + res_in + [_rows(tr, SW), _rows(tr, SW), _vec(SW), _vec(AW), _vec(SW)],
        out_specs=[_rows(tr, AW)] + res_in + [_rows(tr, AW + SW)],
        scratch_shapes=[_residue_scratch(tr, AW)],
        compiler_params=_cparams("parallel"))(*os, *ls, y1, z, b_glu, g_attn, g_ssm)
    return res[0], res[1:1 + nd], res[1 + nd]


def _final(h2, gl, e, g_post, target, *, tr=128):
    S, D = h2.shape
    tr = min(tr, S)

    def body(h_ref, gl_ref, e_ref, g_ref, t_ref, dh_ref, dgl_ref, de_ref, loss_ref, dg_ref):
        i = pl.program_id(0)
        gate = _sigmoid(gl_ref[...])
        e_ = e_ref[...]
        ge = gate * e_
        g = g_ref[...]
        diff = h_ref[...] + _rms(ge, g) - t_ref[...]
        dh = diff * (1.0 / D)
        dh_ref[...] = dh
        dge, dgrow = _rms_bwd(dh, ge, g)
        dgl_ref[...] = (dge * e_ * gate * (1.0 - gate)).astype(BF16)
        de_ref[...] = (dge * gate).astype(BF16)
        part = _colsum(0.5 * jnp.mean(diff * diff, axis=-1, keepdims=True))

        @pl.when(_first(i))
        def _():
            loss_ref[...] = jnp.zeros_like(loss_ref)
            dg_ref[...] = jnp.zeros_like(dg_ref)

        loss_ref[...] += part + jnp.zeros((1, LANES), F32)
        dg_ref[...] += _colsum(dgrow)

    return pl.pallas_call(
        body, name="final_fwd_bwd",
        out_shape=(jax.ShapeDtypeStruct((S, D), F32), jax.ShapeDtypeStruct((S, D), BF16),
                   jax.ShapeDtypeStruct((S, D), BF16), jax.ShapeDtypeStruct((1, LANES), F32),
                   jax.ShapeDtypeStruct((1, D), F32)),
        grid=(S // tr,),
        in_specs=[_rows(tr, D), _rows(tr, D), _rows(tr, D), _vec(D), _rows(tr, D)],
        out_specs=(_rows(tr, D), _rows(tr, D), _rows(tr, D), _vec(LANES), _vec(D)),
        compiler_params=_cparams("arbitrary"))(h2, gl, e, g_post, target)


def _bwd_res_norm(dh_out, dhn, h, g_next, y, g_post, *, name, tr=128):
    S, D = h.shape
    tr = min(tr, S)

    def body(dho_ref, dhn_ref, h_ref, gn_ref, y_ref, gp_ref, dh_ref, dy_ref, dgn_ref, dgp_ref):
        i = pl.program_id(0)
        dx, dgn_rows = _rms_bwd(dhn_ref[...], h_ref[...], gn_ref[...])
        dh = dho_ref[...] + dx
        dh_ref[...] = dh
        dy, dgp_rows = _rms_bwd(dh, y_ref[...], gp_ref[...])
        dy_ref[...] = dy.astype(BF16)

        @pl.when(_first(i))
        def _():
            dgn_ref[...] = jnp.zeros_like(dgn_ref)
            dgp_ref[...] = jnp.zeros_like(dgp_ref)

        dgn_ref[...] += _colsum(dgn_rows)
        dgp_ref[...] += _colsum(dgp_rows)

    return pl.pallas_call(
        body, name=name,
        out_shape=(jax.ShapeDtypeStruct((S, D), F32), jax.ShapeDtypeStruct((S, D), BF16),
                   jax.ShapeDtypeStruct((1, D), F32), jax.ShapeDtypeStruct((1, D), F32)),
        grid=(S // tr,),
        in_specs=[_rows(tr, D), _rows(tr, D), _rows(tr, D), _vec(D), _rows(tr, D), _vec(D)],
        out_specs=(_rows(tr, D), _rows(tr, D), _vec(D), _vec(D)),
        compiler_params=_cparams("arbitrary"))(dh_out, dhn, h, g_next, y, g_post)


def _bwd_first(dh1, dhn1, x, g1, *, tr=256):
    S, D = x.shape
    tr = min(tr, S)

    def body(dh_ref, dhn_ref, x_ref, g_ref, dx_ref, dg_ref):
        i = pl.program_id(0)
        dx, dg_rows = _rms_bwd(dhn_ref[...], x_ref[...], g_ref[...])
        dx_ref[...] = dh_ref[...] + dx

        @pl.when(_first(i))
        def _():
            dg_ref[...] = jnp.zeros_like(dg_ref)

        dg_ref[...] += _colsum(dg_rows)

    return pl.pallas_call(
        body, name="bwd_first",
        out_shape=(jax.ShapeDtypeStruct((S, D), F32), jax.ShapeDtypeStruct((1, D), F32)),
        grid=(S // tr,), in_specs=[_rows(tr, D), _rows(tr, D), _rows(tr, D), _vec(D)],
        out_specs=(_rows(tr, D), _vec(D)), compiler_params=_cparams("arbitrary"))(dh1, dhn1, x, g1)


def _mix_bwd(dmixed, attn, y1, z, b_glu, g_attn, g_ssm, *, tr=256):
    S, AW = attn.shape
    SW = y1.shape[1]
    tr = min(tr, S)
    heads = AW // HEAD_DIM
    nd = len(DILATIONS)

    def body(*refs):
        dm_ref, a_ref, y_ref, z_ref, b_ref, ga_ref, gs_ref = refs[:7]
        da_refs, dd_refs = refs[7:7 + nd], refs[7 + nd:7 + 2 * nd]
        dz_ref, dy2_ref, dga_ref, dgs_ref, db_ref, scr, dd_scr = refs[7 + 2 * nd:]
        i = pl.program_id(0)
        attn_ = a_ref[...]
        dattn, dga_rows = _rms_bwd(dm_ref[:, :AW], attn_, ga_ref[...])
        prod = dattn * attn_
        for h in range(heads):
            sl = slice(h * HEAD_DIM, (h + 1) * HEAD_DIM)
            dd_scr[:, sl] = jnp.broadcast_to(jnp.sum(prod[:, sl], axis=-1, keepdims=True), (tr, HEAD_DIM))
        for n, d in enumerate(DILATIONS):
            _to_residues(scr, dattn, da_refs[n], d)
            _to_residues(scr, dd_scr[...], dd_refs[n], d)
        y2 = _gelu(y_ref[...])
        gate = _sigmoid(z_ref[...] + b_ref[...])
        dssm, dgs_rows = _rms_bwd(dm_ref[:, AW:], y2 * gate, gs_ref[...])
        dz = dssm * y2 * gate * (1.0 - gate)
        dz_ref[...] = dz.astype(BF16)
        dy2_ref[...] = dssm * gate

        @pl.when(_first(i))
        def _():
            dga_ref[...] = jnp.zeros_like(dga_ref)
            dgs_ref[...] = jnp.zeros_like(dgs_ref)
            db_ref[...] = jnp.zeros_like(db_ref)

        dga_ref[...] += _colsum(dga_rows)
        dgs_ref[...] += _colsum(dgs_rows)
        db_ref[...] += _colsum(dz)

    res_out = [_residue_spec(tr, d, AW) for d in DILATIONS]
    res = pl.pallas_call(
        body, name="mix_bwd",
        out_shape=([_residue_shape(S, d, AW, BF16) for d in DILATIONS]
                   + [_residue_shape(S, d, AW, F32) for d in DILATIONS]
                   + [jax.ShapeDtypeStruct((S, SW), BF16), jax.ShapeDtypeStruct((S, SW), F32),
                      jax.ShapeDtypeStruct((1, AW), F32), jax.ShapeDtypeStruct((1, SW), F32),
                      jax.ShapeDtypeStruct((1, SW), F32)]),
        grid=(S // tr,),
        in_specs=[_rows(tr, AW + SW), _rows(tr, AW), _rows(tr, SW), _rows(tr, SW), _vec(SW), _vec(AW), _vec(SW)],
        out_specs=res_out + res_out + [_rows(tr, SW), _rows(tr, SW), _vec(AW), _vec(SW), _vec(SW)],
        scratch_shapes=[_residue_scratch(tr, AW), pltpu.VMEM((tr, AW), F32)],
        compiler_params=_cparams("arbitrary"))(dmixed, attn, y1, z, b_glu, g_attn, g_ssm)
    return (res[:nd], res[nd:2 * nd]) + tuple(res[2 * nd:])


def _attn_mask2(i):
    row = lax.broadcasted_iota(jnp.int32, (BLK, 2 * BLK), 0)
    col = lax.broadcasted_iota(jnp.int32, (BLK, 2 * BLK), 1)
    return jnp.logical_and(col >= row, jnp.logical_and(col <= row + BLK, jnp.logical_or(col >= BLK, i > 0)))


_NT = (((1,), (1,)), ((), ()))
_TN = (((0,), (0,)), ((), ()))


def _attn_in_specs(width, block_of):
    def at(part, prev):
        def index(r, i):
            blk = block_of(i)
            return (part, jnp.maximum(blk - 1, 0) if prev else blk, r)
        return pl.BlockSpec((None, BLK, width), index)
    return [at(0, False), at(1, False), at(1, True), at(2, False), at(2, True)]


def _proj_qkv(hn, w_in_f, *, tm=1024):
    S, D = hn.shape
    AW = w_in_f.shape[2]
    tm = min(tm, S)

    def body(a_ref, b_ref, *rest):
        o_refs, scr = rest[:-1], rest[-1]
        prod = jnp.dot(a_ref[...], b_ref[...], preferred_element_type=F32)
        _fill_strips(scr, prod)
        for o_ref, d in zip(o_refs, DILATIONS):
            if d == 1:
                o_ref[...] = prod.astype(BF16)
            else:
                _strips_to_residues(scr, o_ref, d)

    return pl.pallas_call(
        body, name="proj_qkv",
        out_shape=[jax.ShapeDtypeStruct((3, S // d, d * AW), BF16) for d in DILATIONS], grid=(S // tm, 3),
        in_specs=[pl.BlockSpec((tm, D), lambda i, j: (i, 0)), pl.BlockSpec((None, D, AW), lambda i, j: (j, 0, 0))],
        out_specs=[pl.BlockSpec((None, tm // d, d * AW), lambda i, j: (j, i, 0)) for d in DILATIONS],
        scratch_shapes=[_residue_scratch(tm, AW)],
        compiler_params=_cparams("parallel", "parallel"))(hn, w_in_f)


def _attn_fwd(qkv, d, heads):
    M = qkv.shape[1]
    nb = M // BLK
    width = heads * HEAD_DIM
    scale = 1.0 / math.sqrt(HEAD_DIM)

    def body(q_ref, kc_ref, kp_ref, vc_ref, vp_ref, o_ref, l_ref):
        mask = _attn_mask2(pl.program_id(1))
        ones = jnp.ones((2 * BLK, HEAD_DIM), BF16)

        def scores(h):
            sl = slice(h * HEAD_DIM, (h + 1) * HEAD_DIM)
            k2 = jnp.concatenate([kp_ref[:, sl], kc_ref[:, sl]], axis=0)
            return lax.dot_general(q_ref[:, sl], k2, _NT, preferred_element_type=F32)

        ahead = [scores(h) for h in range(min(ATTN_LOOKAHEAD, heads))]
        for h in range(heads):
            sl = slice(h * HEAD_DIM, (h + 1) * HEAD_DIM)
            s = jnp.where(mask, ahead.pop(0) * scale, NEG_INF)
            if h + ATTN_LOOKAHEAD < heads:
                ahead.append(scores(h + ATTN_LOOKAHEAD))
            v2 = jnp.concatenate([vp_ref[:, sl], vc_ref[:, sl]], axis=0)
            m = jnp.max(jnp.maximum(s[:, :BLK], s[:, BLK:]), axis=-1, keepdims=True)
            p = jnp.exp(s - m).astype(BF16)
            tot = jnp.dot(p, ones, preferred_element_type=F32)
            o_ref[:, sl] = jnp.dot(p, v2, preferred_element_type=F32) / tot
            l_ref[:, sl] = m + jnp.log(tot)

    out_spec = pl.BlockSpec((BLK, width), lambda r, i: (i, r))
    shape = jax.ShapeDtypeStruct((M, d * width), F32)
    return pl.pallas_call(
        body, name=f"attn_fwd_d{d}", out_shape=(shape, shape), grid=(d, nb),
        in_specs=_attn_in_specs(width, lambda i: i), out_specs=(out_spec, out_spec),
        compiler_params=_cparams("parallel", "parallel"))(qkv, qkv, qkv, qkv, qkv)


def _attn_bwd(qkv, dattn, lse, dd, d, heads, after):
    M = qkv.shape[1]
    nb = M // BLK
    width = heads * HEAD_DIM
    scale = 1.0 / math.sqrt(HEAD_DIM)

    def block_of(i):
        return nb - 1 - i

    def body(q_ref, kc_ref, kp_ref, vc_ref, vp_ref, da_ref, l_ref, dd_ref, after_ref,
             dq_ref, dk_ref, dv_ref, dk_carry, dv_carry):
        @pl.when(pl.program_id(1) == 0)
        def _():
            dk_carry[...] = jnp.zeros_like(dk_carry)
            dv_carry[...] = jnp.zeros_like(dv_carry)

        mask = _attn_mask2(block_of(pl.program_id(1)))

        def products(h):
            sl = slice(h * HEAD_DIM, (h + 1) * HEAD_DIM)
            k2 = jnp.concatenate([kp_ref[:, sl], kc_ref[:, sl]], axis=0)
            v2 = jnp.concatenate([vp_ref[:, sl], vc_ref[:, sl]], axis=0)
            return (lax.dot_general(q_ref[:, sl], k2, _NT, preferred_element_type=F32),
                    lax.dot_general(da_ref[:, sl], v2, _NT, preferred_element_type=F32), k2)

        ahead = [products(h) for h in range(min(ATTN_LOOKAHEAD, heads))]
        for h in range(heads):
            sl = slice(h * HEAD_DIM, (h + 1) * HEAD_DIM)
            qk, dp, k2 = ahead.pop(0)
            if h + ATTN_LOOKAHEAD < heads:
                ahead.append(products(h + ATTN_LOOKAHEAD))
            q, da = q_ref[:, sl], da_ref[:, sl]
            lse_ = jnp.concatenate([l_ref[:, sl], l_ref[:, sl]], axis=1)
            dd_ = jnp.concatenate([dd_ref[:, sl], dd_ref[:, sl]], axis=1)
            p = jnp.where(mask, jnp.exp(jnp.where(mask, qk * scale, NEG_INF) - lse_), 0.0)
            ds = (p * (dp - dd_) * scale).astype(BF16)
            dq_ref[:, sl] = jnp.dot(ds, k2, preferred_element_type=F32).astype(BF16)
            dk2 = lax.dot_general(ds, q, _TN, preferred_element_type=F32)
            dv2 = lax.dot_general(p.astype(BF16), da, _TN, preferred_element_type=F32)
            dk_ref[:, sl] = (dk2[BLK:] + dk_carry[:, sl]).astype(BF16)
            dv_ref[:, sl] = (dv2[BLK:] + dv_carry[:, sl]).astype(BF16)
            dk_carry[:, sl] = dk2[:BLK]
            dv_carry[:, sl] = dv2[:BLK]

    blk = pl.BlockSpec((BLK, width), lambda r, i: (block_of(i), r))
    shape = jax.ShapeDtypeStruct((M, d * width), BF16)
    return pl.pallas_call(
        body, name=f"attn_bwd_d{d}", out_shape=(shape,) * 3, grid=(d, nb),
        in_specs=(_attn_in_specs(width, block_of) + [blk, blk, blk]
                  + [pl.BlockSpec(after.shape, lambda r, i: (0, 0))]), out_specs=(blk,) * 3,
        scratch_shapes=[pltpu.VMEM((BLK, width), F32), pltpu.VMEM((BLK, width), F32)],
        compiler_params=_cparams("arbitrary", "arbitrary"))(qkv, qkv, qkv, qkv, qkv, dattn, lse, dd, after)


def _dproj_join(dqs, dks, dvs, du, *, tr=256):
    S, SW = du.shape
    AW = dqs[0].shape[1]
    tr = min(tr, S)
    nd = len(DILATIONS)

    def body(*refs):
        du_ref, out_ref, scr = refs[3 * nd:]
        for part in range(3):
            total = functools.reduce(jnp.add, [_from_residues(scr, refs[part * nd + n], d)
                                               for n, d in enumerate(DILATIONS)])
            out_ref[:, part * AW:(part + 1) * AW] = total.astype(BF16)
        out_ref[:, 3 * AW:] = du_ref[...].astype(BF16)

    return pl.pallas_call(
        body, name="dproj_join", out_shape=jax.ShapeDtypeStruct((S, 3 * AW + SW), BF16), grid=(S // tr,),
        in_specs=[_residue_spec(tr, d, AW) for d in DILATIONS] * 3 + [_rows(tr, SW)],
        out_specs=_rows(tr, 3 * AW + SW), scratch_shapes=[_residue_scratch(tr, AW)],
        compiler_params=_cparams("parallel"))(*dqs, *dks, *dvs, du)


def _ssm_disc(lr, li, ldt):
    dt = jnp.exp(ldt)
    mag = jnp.exp(lr * dt)
    ar = mag * jnp.cos(li * dt)
    ai = mag * jnp.sin(li * dt)
    nr = ar - 1.0
    den = lr * lr + li * li
    return ar, ai, (nr * lr + ai * li) / den, (ai * lr - nr * li) / den


def _ssm_tile_powers(lr, li, ldt, reverse):
    t = lax.broadcasted_iota(jnp.int32, (TILE, 1), 0)
    n = (TILE - t if reverse else t + 1).astype(F32)
    dt = jnp.exp(ldt)
    mag = jnp.exp(n * (lr * dt))
    ang = n * (li * dt)
    return mag * jnp.cos(ang), mag * jnp.sin(ang) * (-1.0 if reverse else 1.0)


def _cmul(ar, ai, br, bi):
    return ar * br - ai * bi, ar * bi + ai * br


def _scan(xr, xi, ar, ai, pr, pi, cr, ci, reverse):
    T = xr.shape[0]
    sub = lax.broadcasted_iota(jnp.int32, xr.shape, 0) & (TILE - 1)
    sh = 1
    while sh < TILE:
        if reverse:
            keep = sub < TILE - sh
            sr, si = pltpu.roll(xr, T - sh, 0), pltpu.roll(xi, T - sh, 0)
        else:
            keep = sub >= sh
            sr, si = pltpu.roll(xr, sh, 0), pltpu.roll(xi, sh, 0)
        sr, si = jnp.where(keep, sr, 0.0), jnp.where(keep, si, 0.0)
        qr, qi = _cmul(ar, ai, sr, si)
        xr, xi = xr + qr, xi + qi
        ar, ai = _cmul(ar, ai, ar, ai)
        sh *= 2
    n = T // TILE
    out_r, out_i = [None] * n, [None] * n
    edge = 0 if reverse else TILE - 1
    for j in (reversed(range(n)) if reverse else range(n)):
        er, ei = _cmul(pr, pi, cr, ci)
        sr, si = xr[j * TILE:(j + 1) * TILE] + er, xi[j * TILE:(j + 1) * TILE] + ei
        out_r[j], out_i[j] = sr, si
        cr, ci = sr[edge:edge + 1], si[edge:edge + 1]
    return jnp.concatenate(out_r, axis=0), jnp.concatenate(out_i, axis=0), cr, ci


def _ssm_specs(T, nch, rev):
    def t_of(c):
        return nch - 1 - c if rev else c
    tok = pl.BlockSpec((T, LANES), lambda j, c: (t_of(c), j))
    par = pl.BlockSpec((None, 1, STATE_LANES), lambda j, c: (j, 0, 0))
    bmat = pl.BlockSpec((None, LANES, STATE_LANES), lambda j, c: (j, 0, 0))
    cmat = pl.BlockSpec((None, STATE_LANES, LANES), lambda j, c: (j, 0, 0))
    dvec = pl.BlockSpec((1, LANES), lambda j, c: (0, j))
    return tok, par, bmat, cmat, dvec


def _ssm_fwd(u, lr_e, li_e, ldt_e, bre_e, bim_e, cre_e, cim_e, d_skip):
    S, SW = u.shape
    T = min(SSM_CHUNK, S)
    nch, nbk = S // T, SW // LANES
    tok, par, bmat, cmat, dvec = _ssm_specs(T, nch, False)
    state_spec = pl.BlockSpec((T, STATE_LANES), lambda j, c: (c, j))
    carry_spec = pl.BlockSpec((None, 1, STATE_LANES), lambda j, c: (c, 0, j))

    def body(u_ref, lr_ref, li_ref, ldt_ref, bre_ref, bim_ref, cre_ref, cim_ref, d_ref,
             y_ref, y2_ref, sr_ref, si_ref, er_ref, ei_ref, bbr, bbi, a_scr, pw, carry):
        c = pl.program_id(1)

        @pl.when(c == 0)
        def _():
            lr, li, ldt = lr_ref[...], li_ref[...], ldt_ref[...]
            ar, ai, kr, ki = _ssm_disc(lr, li, ldt)
            a_scr[0], a_scr[1] = ar, ai
            bbr[...] = (kr * bre_ref[...] - ki * bim_ref[...]).astype(BF16)
            bbi[...] = (kr * bim_ref[...] + ki * bre_ref[...]).astype(BF16)
            pw[0], pw[1] = _ssm_tile_powers(lr, li, ldt, False)
            carry[...] = jnp.zeros_like(carry)

        u_ = u_ref[...]
        ub = u_.astype(BF16)
        sr, si, cr, ci = _scan(jnp.dot(ub, bbr[...], preferred_element_type=F32),
                               jnp.dot(ub, bbi[...], preferred_element_type=F32),
                               a_scr[0], a_scr[1], pw[0], pw[1], carry[0], carry[1], False)
        carry[0], carry[1] = cr, ci
        er_ref[...], ei_ref[...] = cr, ci
        sr_ref[...], si_ref[...] = sr, si
        y0 = (jnp.dot(sr.astype(BF16), cre_ref[...].astype(BF16), preferred_element_type=F32)
              - jnp.dot(si.astype(BF16), cim_ref[...].astype(BF16), preferred_element_type=F32))
        y1 = y0 + d_ref[...] * u_
        y_ref[...] = y1
        y2_ref[...] = _gelu(y1).astype(BF16)

    states = jax.ShapeDtypeStruct((S, nbk * STATE_LANES), F32)
    ends = jax.ShapeDtypeStruct((nch, 1, nbk * STATE_LANES), F32)
    return pl.pallas_call(
        body, name="ssm_fwd",
        out_shape=(jax.ShapeDtypeStruct((S, SW), F32), jax.ShapeDtypeStruct((S, SW), BF16), states, states, ends, ends),
        grid=(nbk, nch), in_specs=[tok, par, par, par, bmat, bmat, cmat, cmat, dvec],
        out_specs=(tok, tok, state_spec, state_spec, carry_spec, carry_spec),
        scratch_shapes=[pltpu.VMEM((LANES, STATE_LANES), BF16), pltpu.VMEM((LANES, STATE_LANES), BF16),
                        pltpu.VMEM((2, 1, STATE_LANES), F32), pltpu.VMEM((2, TILE, STATE_LANES), F32),
                        pltpu.VMEM((2, 1, STATE_LANES), F32)],
        compiler_params=_cparams("arbitrary", "arbitrary"),
    )(u, lr_e, li_e, ldt_e, bre_e, bim_e, cre_e, cim_e, d_skip)


def _ssm_bwd(u, y1, dy2a, dy2b, st_r, st_i, ends_r, ends_i, lr_e, li_e, ldt_e, bre_e, bim_e, cre_e, cim_e, d_skip):
    S, SW = u.shape
    T = min(SSM_CHUNK, S)
    nch, nbk = S // T, SW // LANES
    tok, par, bmat, cmat, dvec = _ssm_specs(T, nch, True)
    state_spec = pl.BlockSpec((T, STATE_LANES), lambda j, c: (nch - 1 - c, j))
    prev_spec = pl.BlockSpec((None, 1, STATE_LANES), lambda j, c: (jnp.maximum(nch - 2 - c, 0), 0, j))
    acc8 = pl.BlockSpec((None, 8, STATE_LANES), lambda j, c: (j, 0, 0))
    dd8 = pl.BlockSpec((None, 8, LANES), lambda j, c: (j, 0, 0))

    def body(u_ref, y_ref, da_ref, db_ref, sr_ref, si_ref, pr_ref, pi_ref, lr_ref, li_ref, ldt_ref,
             bre_ref, bim_ref, cre_ref, cim_ref, d_ref,
             du_ref, dar_ref, dai_ref, dcr_ref, dci_ref, dbr_ref, dbi_ref, ddk_ref,
             bbr, bbi, a_scr, pw, carry):
        c = pl.program_id(1)

        @pl.when(c == 0)
        def _():
            lr, li, ldt = lr_ref[...], li_ref[...], ldt_ref[...]
            ar, ai, kr, ki = _ssm_disc(lr, li, ldt)
            a_scr[0], a_scr[1] = ar, -ai
            bbr[...] = (kr * bre_ref[...] - ki * bim_ref[...]).astype(BF16)
            bbi[...] = (kr * bim_ref[...] + ki * bre_ref[...]).astype(BF16)
            pw[0], pw[1] = _ssm_tile_powers(lr, li, ldt, True)
            carry[...] = jnp.zeros_like(carry)
            for ref in (dar_ref, dai_ref, dcr_ref, dci_ref, dbr_ref, dbi_ref, ddk_ref):
                ref[...] = jnp.zeros_like(ref)

        u_ = u_ref[...]
        ub = u_.astype(BF16)
        dy1 = (da_ref[...] + db_ref[...]) * _gelu_grad(y_ref[...])
        dyb = dy1.astype(BF16)

        sr, si = sr_ref[...], si_ref[...]
        has_prev = c < nch - 1
        s0r = jnp.where(has_prev, pr_ref[...], 0.0)
        s0i = jnp.where(has_prev, pi_ref[...], 0.0)

        cre_b, cim_b = cre_ref[...].astype(BF16), cim_ref[...].astype(BF16)
        gr, gi, cr, ci = _scan(lax.dot_general(dyb, cre_b, _NT, preferred_element_type=F32),
                               -lax.dot_general(dyb, cim_b, _NT, preferred_element_type=F32),
                               a_scr[0], a_scr[1], pw[0], pw[1], carry[0], carry[1], True)
        carry[0], carry[1] = cr, ci

        row = lax.broadcasted_iota(jnp.int32, (T, STATE_LANES), 0)
        spr = jnp.where(row == 0, s0r, pltpu.roll(sr, 1, 0))
        spi = jnp.where(row == 0, s0i, pltpu.roll(si, 1, 0))

        def fold(a):
            return jnp.sum(a.reshape(T // 8, 8, a.shape[-1]), axis=0)

        dar_ref[...] += fold(gr * spr + gi * spi)
        dai_ref[...] += fold(gi * spr - gr * spi)
        srb, sib, grb, gib = sr.astype(BF16), si.astype(BF16), gr.astype(BF16), gi.astype(BF16)
        dcr_ref[...] += lax.dot_general(srb, dyb, _TN, preferred_element_type=F32)
        dci_ref[...] -= lax.dot_general(sib, dyb, _TN, preferred_element_type=F32)
        dbr_ref[...] += lax.dot_general(ub, grb, _TN, preferred_element_type=F32)
        dbi_ref[...] += lax.dot_general(ub, gib, _TN, preferred_element_type=F32)
        du_ref[...] = (lax.dot_general(grb, bbr[...], _NT, preferred_element_type=F32)
                       + lax.dot_general(gib, bbi[...], _NT, preferred_element_type=F32)
                       + dy1 * d_ref[...])
        ddk_ref[...] += fold(dy1 * u_)

    return pl.pallas_call(
        body, name="ssm_bwd",
        out_shape=(jax.ShapeDtypeStruct((S, SW), F32),
                   jax.ShapeDtypeStruct((nbk, 8, STATE_LANES), F32), jax.ShapeDtypeStruct((nbk, 8, STATE_LANES), F32),
                   jax.ShapeDtypeStruct((nbk, STATE_LANES, LANES), F32), jax.ShapeDtypeStruct((nbk, STATE_LANES, LANES), F32),
                   jax.ShapeDtypeStruct((nbk, LANES, STATE_LANES), F32), jax.ShapeDtypeStruct((nbk, LANES, STATE_LANES), F32),
                   jax.ShapeDtypeStruct((nbk, 8, LANES), F32)),
        grid=(nbk, nch),
        in_specs=[tok, tok, tok, tok, state_spec, state_spec, prev_spec, prev_spec, par, par, par,
                  bmat, bmat, cmat, cmat, dvec],
        out_specs=(tok, acc8, acc8, cmat, cmat, bmat, bmat, dd8),
        scratch_shapes=[pltpu.VMEM((LANES, STATE_LANES), BF16), pltpu.VMEM((LANES, STATE_LANES), BF16),
                        pltpu.VMEM((2, 1, STATE_LANES), F32), pltpu.VMEM((2, TILE, STATE_LANES), F32),
                        pltpu.VMEM((2, 1, STATE_LANES), F32)],
        compiler_params=_cparams("arbitrary", "arbitrary"),
    )(u, y1, dy2a, dy2b, st_r, st_i, ends_r, ends_i, lr_e, li_e, ldt_e, bre_e, bim_e, cre_e, cim_e, d_skip)


def _ssm_param_bwd(dar8, dai8, dbr_e, dbi_e, lr_e, li_e, ldt_e, bre_e, bim_e):
    nbk = lr_e.shape[0]
    par = pl.BlockSpec((None, 1, STATE_LANES), lambda j: (j, 0, 0))
    acc8 = pl.BlockSpec((None, 8, STATE_LANES), lambda j: (j, 0, 0))
    bmat = pl.BlockSpec((None, LANES, STATE_LANES), lambda j: (j, 0, 0))

    def body(dar_ref, dai_ref, dbr_ref, dbi_ref, lr_ref, li_ref, ldt_ref, bre_ref, bim_ref,
             dlr_ref, dli_ref, dldt_ref, dbre_ref, dbim_ref):
        lr, li, ldt = lr_ref[...], li_ref[...], ldt_ref[...]
        (ar, ai, kr, ki), vjp = jax.vjp(_ssm_disc, lr, li, ldt)
        dbr, dbi, bre, bim = dbr_ref[...], dbi_ref[...], bre_ref[...], bim_ref[...]
        dbre_ref[...] = kr * dbr + ki * dbi
        dbim_ref[...] = kr * dbi - ki * dbr
        dkr = _colsum(dbr * bre + dbi * bim)
        dki = _colsum(dbi * bre - dbr * bim)
        dlr, dli, dldt = vjp((_colsum(dar_ref[...]), _colsum(dai_ref[...]), dkr, dki))
        dlr_ref[...] = dlr
        dli_ref[...] = dli
        tot = jnp.broadcast_to(dldt, (8, STATE_LANES))
        sh = 1
        while sh < SSM_P:
            tot = tot + pltpu.roll(tot, STATE_LANES - sh, 1)
            sh *= 2
        dldt_ref[...] = tot[:1]

    vec = jax.ShapeDtypeStruct((nbk, 1, STATE_LANES), F32)
    mat = jax.ShapeDtypeStruct((nbk, LANES, STATE_LANES), F32)
    return pl.pallas_call(
        body, name="ssm_param_bwd", out_shape=(vec, vec, vec, mat, mat), grid=(nbk,),
        in_specs=[acc8, acc8, bmat, bmat, par, par, par, bmat, bmat],
        out_specs=(par, par, par, bmat, bmat), compiler_params=_cparams("parallel"),
    )(dar8, dai8, dbr_e, dbi_e, lr_e, li_e, ldt_e, bre_e, bim_e)


def _expand_b(b):
    G = b.shape[0]
    bt = b.transpose(0, 2, 1).reshape(G // GROUPS_PER_BLOCK, GROUPS_PER_BLOCK, SSM_C, SSM_P)
    eye = jnp.eye(GROUPS_PER_BLOCK, dtype=b.dtype)
    return (bt[:, :, :, None, :] * eye[None, :, None, :, None]).reshape(G // GROUPS_PER_BLOCK, LANES, STATE_LANES)


def _collapse_b(be):
    nbk = be.shape[0]
    eye = jnp.eye(GROUPS_PER_BLOCK, dtype=be.dtype)
    d5 = be.reshape(nbk, GROUPS_PER_BLOCK, SSM_C, GROUPS_PER_BLOCK, SSM_P)
    d4 = (d5 * eye[None, :, None, :, None]).sum(axis=3)
    return d4.transpose(0, 1, 3, 2).reshape(nbk * GROUPS_PER_BLOCK, SSM_P, SSM_C)


def _expand_c(cm):
    G = cm.shape[0]
    ct = cm.transpose(0, 2, 1).reshape(G // GROUPS_PER_BLOCK, GROUPS_PER_BLOCK, SSM_P, SSM_C)
    eye = jnp.eye(GROUPS_PER_BLOCK, dtype=cm.dtype)
    return (ct[:, :, :, None, :] * eye[None, :, None, :, None]).reshape(G // GROUPS_PER_BLOCK, STATE_LANES, LANES)


def _collapse_c(ce):
    nbk = ce.shape[0]
    eye = jnp.eye(GROUPS_PER_BLOCK, dtype=ce.dtype)
    d5 = ce.reshape(nbk, GROUPS_PER_BLOCK, SSM_P, GROUPS_PER_BLOCK, SSM_C)
    d4 = (d5 * eye[None, :, None, :, None]).sum(axis=3)
    return d4.transpose(0, 1, 3, 2).reshape(nbk * GROUPS_PER_BLOCK, SSM_C, SSM_P)


def _place():
    x, y, c = lax.axis_index("x"), lax.axis_index("y"), lax.axis_index("c")
    return x, y, c


def _other_chips(x, y):
    return [(1 - x, y), (x, 1 - y), (1 - x, 1 - y)]


_ANY = pl.BlockSpec(memory_space=pl.ANY)


_HBM = pl.BlockSpec(memory_space=pltpu.HBM)
_SEM = pl.BlockSpec(memory_space=pltpu.SEMAPHORE)
_EFFECT = pltpu.SideEffectType.DATAFLOW_SIDE_EFFECTING
_TOKEN = jax.ShapeDtypeStruct((8, LANES), F32)


def _hbm(a):
    return pltpu.with_memory_space_constraint(a, pltpu.HBM)


def _place_own(src, *, gather, name, tr=512):
    R, C = src.shape[-2:]
    tr = min(tr, R)
    x, y, _ = _place()
    me = (2 * x + y).astype(jnp.int32).reshape(1)

    def body(me_ref, s_ref, o_ref):
        o_ref[...] = s_ref[...].astype(BF16)

    own = pl.BlockSpec((None, tr, C), lambda i, me_ref: (me_ref[0], i, 0))
    grid_spec = pltpu.PrefetchScalarGridSpec(
        num_scalar_prefetch=1, grid=(R // tr,),
        in_specs=[pl.BlockSpec((tr, C), lambda i, me_ref: (i, 0)) if gather else own], out_specs=own)
    return pl.pallas_call(
        body, name=name, grid_spec=grid_spec, out_shape=jax.ShapeDtypeStruct((N_CHIPS, R, C), BF16),
        compiler_params=_cparams("parallel"))(me, src)


def _exchange_copy(src_slot, land_slot, send, recv, k, j, peer, c):
    return pltpu.make_async_remote_copy(
        src_ref=src_slot, dst_ref=land_slot, send_sem=send.at[3 * k + j], recv_sem=recv.at[3 * k + j],
        device_id=(peer[0], peer[1], c), device_id_type=MESH)


def _exchange_start(lands, srcs, groups, *, name):
    n, ng = len(lands), len(groups)
    bufs = list(lands) + list(srcs)
    nb = len(bufs)

    def body(*refs):
        lnd, src, sems = refs[:n], refs[n:nb], refs[nb:nb + 2 * ng]
        token = refs[2 * nb + 2 * ng]
        x, y, c = _place()
        me = 2 * x + y
        for gi, group in enumerate(groups):
            for k, w in enumerate(group):
                for j, peer in enumerate(_other_chips(x, y)):
                    if src:
                        sent, dst = src[w].at[2 * peer[0] + peer[1]], lnd[w].at[me]
                    else:
                        sent = dst = lnd[w].at[me, c]
                    _exchange_copy(sent, dst, sems[2 * gi], sems[2 * gi + 1], k, j, peer, c).start()
        token[...] = jnp.zeros_like(token)

    sem_shapes = [pltpu.SemaphoreType.DMA((3 * len(g),)) for g in groups for _ in range(2)]
    res = pl.pallas_call(
        body, name=name,
        out_shape=sem_shapes + [pltpu.HBM(a.shape, a.dtype) for a in bufs] + [_TOKEN],
        in_specs=[_HBM] * nb,
        out_specs=[_SEM] * (2 * ng) + [_HBM] * nb + [pl.BlockSpec(memory_space=pltpu.VMEM)],
        input_output_aliases={i: 2 * ng + i for i in range(nb)},
        compiler_params=pltpu.CompilerParams(has_side_effects=_EFFECT),
    )(*[_hbm(a) for a in bufs])
    sems = [(res[2 * gi], res[2 * gi + 1]) for gi in range(ng)]
    return sems, res[2 * ng:2 * ng + n], res[2 * ng + n:2 * ng + nb], res[-1]


def _exchange_wait(lands, srcs, sems, after, *, name):
    n = len(lands)
    bufs = list(lands) + list(srcs)
    nb = len(bufs)
    send_sems, recv_sems = sems

    def body(*refs):
        lnd, src, send, recv = refs[:n], refs[n:nb], refs[nb], refs[nb + 1]
        x, y, c = _place()
        for k in range(n):
            for j, peer in enumerate(_other_chips(x, y)):
                slot = 2 * peer[0] + peer[1]
                if src:
                    copy = _exchange_copy(src[k].at[slot], lnd[k].at[slot], send, recv, k, j, peer, c)
                else:
                    copy = _exchange_copy(lnd[k].at[slot, c], lnd[k].at[slot, c], send, recv, k, j, peer, c)
                copy.wait_send()
                copy.wait_recv()

    res = pl.pallas_call(
        body, name=name, out_shape=[pltpu.HBM(a.shape, a.dtype) for a in bufs],
        in_specs=[_HBM] * nb + [_SEM, _SEM, _ANY], out_specs=[_HBM] * nb,
        input_output_aliases={i: i for i in range(nb)},
        compiler_params=pltpu.CompilerParams(has_side_effects=_EFFECT),
    )(*bufs, send_sems, recv_sems, after)
    return res[:n]


def _pair_fill(lands, *, name):
    n = len(lands)

    def body(*refs):
        ins, outs, send, recv = refs[:n], refs[n:2 * n], refs[2 * n], refs[2 * n + 1]
        x, y, c = _place()
        for w in range(n):
            for j, (px, py) in enumerate(_other_chips(x, y)):
                slot = 2 * px + py
                pltpu.make_async_remote_copy(
                    src_ref=ins[w].at[slot, c], dst_ref=outs[w].at[slot, c], send_sem=send.at[3 * w + j],
                    recv_sem=recv.at[3 * w + j], device_id=(x, y, 1 - c), device_id_type=MESH).start()
        for w in range(n):
            for j, (px, py) in enumerate(_other_chips(x, y)):
                slot = 2 * px + py
                arrival = pltpu.make_async_remote_copy(
                    src_ref=ins[w].at[slot, c], dst_ref=outs[w].at[slot, 1 - c], send_sem=send.at[3 * w + j],
                    recv_sem=recv.at[3 * w + j], device_id=(x, y, 1 - c), device_id_type=MESH)
                arrival.wait_recv()
                arrival.wait_send()

    return pl.pallas_call(
        body, name=name, out_shape=[jax.ShapeDtypeStruct(a.shape, a.dtype) for a in lands],
        in_specs=[_ANY] * n, out_specs=[_ANY] * n, input_output_aliases={i: i for i in range(n)},
        scratch_shapes=[pltpu.SemaphoreType.DMA((3 * n,)), pltpu.SemaphoreType.DMA((3 * n,))],
    )(*lands)


def _sum_partials(land, *, name, tr=256):
    _, R, C = land.shape
    tr = min(tr, R)

    def body(l_ref, o_ref):
        acc = l_ref[0].astype(F32)
        for k in range(1, N_CHIPS):
            acc = acc + l_ref[k].astype(F32)
        o_ref[...] = acc

    return pl.pallas_call(
        body, name=name, out_shape=jax.ShapeDtypeStruct((R, C), F32), grid=(R // tr,),
        in_specs=[pl.BlockSpec((N_CHIPS, tr, C), lambda i: (0, i, 0))], out_specs=_rows(tr, C),
        compiler_params=_cparams("parallel"))(land)


def _swap_with_sibling(sums, *, name):
    n = len(sums)

    def body(*refs):
        ins, outs = refs[:n], refs[n:2 * n]
        send_sems, recv_sems = refs[2 * n:]
        x, y, c = _place()
        copies = [pltpu.make_async_remote_copy(
            src_ref=ins[w], dst_ref=outs[w], send_sem=send_sems.at[w], recv_sem=recv_sems.at[w],
            device_id=(x, y, 1 - c), device_id_type=MESH) for w in range(n)]
        for cp in copies:
            cp.start()
        for cp in copies:
            cp.wait_recv()
            cp.wait_send()

    return pl.pallas_call(
        body, name=name,
        out_shape=[jax.ShapeDtypeStruct(s.shape, s.dtype) for s in sums],
        in_specs=[_ANY] * n, out_specs=[_ANY] * n,
        scratch_shapes=[pltpu.SemaphoreType.DMA((n,)), pltpu.SemaphoreType.DMA((n,))],
    )(*sums)


def _swap_start(sums, *, name):
    n = len(sums)
    bufs = list(sums) + [lax.empty(s.shape, s.dtype) for s in sums]

    def body(*refs):
        src, lnd, send, recv, token = refs[:n], refs[n:2 * n], refs[2 * n], refs[2 * n + 1], refs[4 * n + 2]
        x, y, c = _place()
        for w in range(n):
            pltpu.make_async_remote_copy(
                src_ref=src[w], dst_ref=lnd[w], send_sem=send.at[w], recv_sem=recv.at[w],
                device_id=(x, y, 1 - c), device_id_type=MESH).start()
        token[...] = jnp.zeros_like(token)

    res = pl.pallas_call(
        body, name=name,
        out_shape=[pltpu.SemaphoreType.DMA((n,))] * 2 + [pltpu.HBM(a.shape, a.dtype) for a in bufs] + [_TOKEN],
        in_specs=[_HBM] * (2 * n),
        out_specs=[_SEM, _SEM] + [_HBM] * (2 * n) + [pl.BlockSpec(memory_space=pltpu.VMEM)],
        input_output_aliases={i: 2 + i for i in range(2 * n)},
        compiler_params=pltpu.CompilerParams(has_side_effects=_EFFECT),
    )(*[_hbm(a) for a in bufs])
    return (res[0], res[1]), res[2:2 + n], res[2 + n:2 + 2 * n], res[-1]


def _swap_wait(sums, lands, sems, after, *, name):
    n = len(sums)

    def body(*refs):
        src, lnd, send, recv = refs[:n], refs[n:2 * n], refs[2 * n], refs[2 * n + 1]
        x, y, c = _place()
        for w in range(n):
            copy = pltpu.make_async_remote_copy(
                src_ref=src[w], dst_ref=lnd[w], send_sem=send.at[w], recv_sem=recv.at[w],
                device_id=(x, y, 1 - c), device_id_type=MESH)
            copy.wait_send()
            copy.wait_recv()

    bufs = list(sums) + list(lands)
    res = pl.pallas_call(
        body, name=name, out_shape=[pltpu.HBM(a.shape, a.dtype) for a in bufs],
        in_specs=[_HBM] * (2 * n) + [_SEM, _SEM, _ANY], out_specs=[_HBM] * (2 * n),
        input_output_aliases={i: i for i in range(2 * n)},
        compiler_params=pltpu.CompilerParams(has_side_effects=_EFFECT),
    )(*bufs, *sems, after)
    return res[:n], res[n:]


def _adamw_math(w, g, m, v):
    m = ADAM_B1 * m + (1.0 - ADAM_B1) * g
    v = ADAM_B2 * v + (1.0 - ADAM_B2) * (g * g)
    m_hat = m / (1.0 - ADAM_B1 ** ADAM_STEP)
    v_hat = v / (1.0 - ADAM_B2 ** ADAM_STEP)
    delta = -ADAM_LR * (m_hat / (jnp.sqrt(v_hat) + ADAM_EPS) + ADAM_WD * w)
    return delta, m, v


def _adamw_pair(mine, theirs, w, m, v, *, name, tr=128):
    R, C = w.shape
    tr = min(tr, R)

    def body(a_ref, b_ref, w_ref, m_ref, v_ref, g_ref, d_ref, nm_ref, nv_ref):
        g = a_ref[...] + b_ref[...]
        g_ref[...] = g
        d_ref[...], nm_ref[...], nv_ref[...] = _adamw_math(w_ref[...], g, m_ref[...], v_ref[...])

    shape = jax.ShapeDtypeStruct((R, C), F32)
    return pl.pallas_call(
        body, name=name, out_shape=(shape,) * 4, grid=(R // tr,),
        in_specs=[_rows(tr, C)] * 5, out_specs=(_rows(tr, C),) * 4,
        compiler_params=_cparams("parallel"))(mine, theirs, w, m, v)


def _all_reduce_small(packed):
    R = packed.shape[0]
    half = R // 2

    def body(x_ref, g_ref, sib_ref, pair_ref, land_ref, send_sems, recv_sems):
        x, y, c = _place()
        me = 2 * x + y
        sibling = (x, y, 1 - c)

        swap = pltpu.make_async_remote_copy(
            src_ref=x_ref, dst_ref=sib_ref, send_sem=send_sems.at[0], recv_sem=recv_sems.at[0],
            device_id=sibling, device_id_type=MESH)
        swap.start()
        swap.wait()
        mine, theirs = x_ref[...], sib_ref[...]
        south = c == 0
        pair_ref[...] = jnp.where(south, mine, theirs) + jnp.where(south, theirs, mine)

        land_ref[me] = pair_ref[c]
        for j, (px, py) in enumerate(_other_chips(x, y)):
            pltpu.make_async_remote_copy(
                src_ref=pair_ref.at[c], dst_ref=land_ref.at[me], send_sem=send_sems.at[1 + j],
                recv_sem=recv_sems.at[1 + j], device_id=(px, py, c), device_id_type=MESH).start()
        for j, (px, py) in enumerate(_other_chips(x, y)):
            arrival = pltpu.make_async_remote_copy(
                src_ref=pair_ref.at[c], dst_ref=land_ref.at[2 * px + py], send_sem=send_sems.at[1 + j],
                recv_sem=recv_sems.at[1 + j], device_id=(px, py, c), device_id_type=MESH)
            arrival.wait_recv()
            arrival.wait_send()
        total = land_ref[0]
        for k in range(1, N_CHIPS):
            total = total + land_ref[k]
        g_ref[c] = total

        give = pltpu.make_async_remote_copy(
            src_ref=g_ref.at[c], dst_ref=g_ref.at[c], send_sem=send_sems.at[4], recv_sem=recv_sems.at[4],
            device_id=sibling, device_id_type=MESH)
        give.start()
        take = pltpu.make_async_remote_copy(
            src_ref=g_ref.at[c], dst_ref=g_ref.at[1 - c], send_sem=send_sems.at[4], recv_sem=recv_sems.at[4],
            device_id=sibling, device_id_type=MESH)
        take.wait_recv()
        give.wait_send()

    vm = pl.BlockSpec(memory_space=pltpu.VMEM)
    return pl.pallas_call(
        body, name="all_reduce_small", out_shape=jax.ShapeDtypeStruct((2, half, LANES), F32),
        in_specs=[vm], out_specs=vm,
        scratch_shapes=[pltpu.VMEM((2, half, LANES), F32), pltpu.VMEM((2, half, LANES), F32),
                        pltpu.VMEM((N_CHIPS, half, LANES), F32),
                        pltpu.SemaphoreType.DMA((5,)), pltpu.SemaphoreType.DMA((5,))],
        compiler_params=pltpu.CompilerParams(vmem_limit_bytes=VMEM_LIMIT_BYTES),
    )(packed.reshape(2, half, LANES)).reshape(R, LANES)


def _adamw_small(g, w, m, v):
    R = g.shape[0]
    tr = PACK_ROWS

    def body(g_ref, w_ref, m_ref, v_ref, d_ref, nm_ref, nv_ref):
        d_ref[...], nm_ref[...], nv_ref[...] = _adamw_math(w_ref[...], g_ref[...], m_ref[...], v_ref[...])

    shape = jax.ShapeDtypeStruct((R, LANES), F32)
    return pl.pallas_call(
        body, name="adamw_small", out_shape=(shape,) * 3, grid=(R // tr,),
        in_specs=[_rows(tr, LANES)] * 4, out_specs=(_rows(tr, LANES),) * 3,
        compiler_params=_cparams("parallel"))(g, w, m, v)


def _pack(arrays):
    parts, layout = [], []
    for a in arrays:
        n = a.size
        rows = -(-n // (8 * LANES)) * 8
        flat = jnp.pad(a.reshape(-1).astype(F32), (0, rows * LANES - n))
        parts.append(flat.reshape(rows, LANES))
        layout.append((rows, n, a.shape))
    total = sum(r for r, _, _ in layout)
    parts.append(jnp.zeros((-total % PACK_ROWS, LANES), F32))
    return jnp.concatenate(parts, axis=0), layout


def _unpack(buf, layout):
    out, r0 = [], 0
    for rows, n, shape in layout:
        out.append(buf[r0:r0 + rows].reshape(-1)[:n].reshape(shape))
        r0 += rows
    return out


SMALL = ("mix_norm_pre", "lam_re", "lam_im", "log_dt", "ssm_b_re", "ssm_b_im", "ssm_c_re", "ssm_c_im",
         "ssm_d", "b_glu", "attn_out_norm", "ssm_out_norm", "mix_norm_post", "mlp_norm_pre",
         "mlp_norm_post", "ple_norm_pre", "ple_norm_post")
BIG = ("w_in", "w_glu", "w_out", "w_up", "w_down", "w_ple_gate", "w_ple_proj")
WEIGHTS = ("mix_norm_pre", "w_in", "lam_re", "lam_im", "log_dt", "ssm_b_re", "ssm_b_im", "ssm_c_re",
           "ssm_c_im", "ssm_d", "w_glu", "b_glu", "attn_out_norm", "ssm_out_norm", "w_out",
           "mix_norm_post", "mlp_norm_pre", "w_up", "w_down", "mlp_norm_post", "ple_norm_pre",
           "w_ple_gate", "w_ple_proj", "ple_norm_post")


def kernel(x, p, mix_norm_pre, w_in, lam_re, lam_im, log_dt, ssm_b_re, ssm_b_im, ssm_c_re, ssm_c_im, ssm_d, w_glu, b_glu, attn_out_norm, ssm_out_norm, w_out, mix_norm_post, mlp_norm_pre, w_up, w_down, mlp_norm_post, ple_norm_pre, w_ple_gate, w_ple_proj, ple_norm_post, loss_target, m_mix_norm_pre, m_w_in, m_lam_re, m_lam_im, m_log_dt, m_ssm_b_re, m_ssm_b_im, m_ssm_c_re, m_ssm_c_im, m_ssm_d, m_w_glu, m_b_glu, m_attn_out_norm, m_ssm_out_norm, m_w_out, m_mix_norm_post, m_mlp_norm_pre, m_w_up, m_w_down, m_mlp_norm_post, m_ple_norm_pre, m_w_ple_gate, m_w_ple_proj, m_ple_norm_post, v_mix_norm_pre, v_w_in, v_lam_re, v_lam_im, v_log_dt, v_ssm_b_re, v_ssm_b_im, v_ssm_c_re, v_ssm_c_im, v_ssm_d, v_w_glu, v_b_glu, v_attn_out_norm, v_ssm_out_norm, v_w_out, v_mix_norm_post, v_mlp_norm_pre, v_w_up, v_w_down, v_mlp_norm_post, v_ple_norm_pre, v_w_ple_gate, v_w_ple_proj, v_ple_norm_post):
    args = dict(locals())
    W = {n: args[n][0] for n in WEIGHTS}
    Mo = {n: args["m_" + n][0] for n in WEIGHTS}
    Vo = {n: args["v_" + n][0] for n in WEIGHTS}
    xs, ps, tgt = x[0], p[0, 0], loss_target[0]
    S, D = xs.shape
    SW = W["ssm_d"].shape[0]
    AW = W["attn_out_norm"].shape[0]
    heads = AW // HEAD_DIM
    G = SW // SSM_C
    nbk = SW // LANES
    assert W["w_in"].shape[1] * N_CHIPS == 3 * AW + SW and AW == SW

    row = lambda a: a.reshape(1, -1)

    ag_groups = (("w_in",), ("w_glu", "w_out"), ("w_up",), ("w_down", "w_ple_gate", "w_ple_proj"))
    ag_names = [n for g in ag_groups for n in g]
    def in_halves(a):
        return a.reshape(N_CHIPS, 2, a.shape[1] // 2, a.shape[2])

    ag_sems, ag_land, _, ag_token = _exchange_start(
        [in_halves(_place_own(W[n], gather=True, name="ag_place_" + n)) for n in ag_names], [],
        [[ag_names.index(n) for n in g] for g in ag_groups], name="ag_start")

    def gathered(gi, after):
        got = _exchange_wait([ag_land[ag_names.index(n)] for n in ag_groups[gi]], [], ag_sems[gi], after,
                             name=f"ag_wait_{gi}")
        got = _pair_fill(got, name=f"ag_pair_{gi}")
        return {n: a.reshape(N_CHIPS, -1, a.shape[-1]) for n, a in zip(ag_groups[gi], got)}

    lr_e = W["lam_re"].reshape(nbk, 1, STATE_LANES)
    li_e = W["lam_im"].reshape(nbk, 1, STATE_LANES)
    ldt_e = jnp.repeat(W["log_dt"], SSM_P).reshape(nbk, 1, STATE_LANES)
    bre_e, bim_e = _expand_b(W["ssm_b_re"]), _expand_b(W["ssm_b_im"])
    cre_e, cim_e = _expand_c(W["ssm_c_re"]), _expand_c(W["ssm_c_im"])
    d_row = row(W["ssm_d"])

    hn1 = _norm_cast(xs, row(W["mix_norm_pre"]) + ag_token[0, 0], name="norm_in")
    w_in_f = gathered(0, hn1)["w_in"]
    qkv_b = _proj_qkv(hn1, w_in_f)
    u = _matmul(hn1, w_in_f, name="proj_u", b_shards=N_CHIPS, b_cols=(3 * AW, SW))
    outs, lses = zip(*[_attn_fwd(qb, d, heads) for d, qb in zip(DILATIONS, qkv_b)])
    y1, y2b, st_r, st_i, ends_r, ends_i = _ssm_fwd(u, lr_e, li_e, ldt_e, bre_e, bim_e, cre_e, cim_e, d_row)
    full = gathered(1, y2b)
    w_glu_f = full["w_glu"].reshape(SW, SW)
    w_out_f = full["w_out"].reshape(AW + SW, D)
    z = _matmul(y2b, w_glu_f, name="glu_z")
    attn, lse_b, mixed = _mix_fwd(outs, lses, y1, z, row(W["b_glu"]), row(W["attn_out_norm"]), row(W["ssm_out_norm"]))
    mo = _matmul(mixed, w_out_f, name="mix_out")
    h1, hn2 = _res_norm(xs, mo, row(W["mix_norm_post"]), row(W["mlp_norm_pre"]), name="res_mix")
    w_up_f = gathered(2, hn2)["w_up"]
    up, act = _matmul(hn2, w_up_f, name="mlp_up", b_shards=N_CHIPS, relu2=True)
    full = gathered(3, act)
    w_down_f = full["w_down"].reshape(-1, D)
    w_pg_f = full["w_ple_gate"].reshape(D, D)
    w_pp_f = full["w_ple_proj"]
    ff = _matmul(act, w_down_f, name="mlp_down")
    h2, hn3 = _res_norm(h1, ff, row(W["mlp_norm_post"]), row(W["ple_norm_pre"]), name="res_mlp")
    gl = _matmul(hn3, w_pg_f, name="ple_gate")
    e = _matmul(ps.astype(BF16), w_pp_f, name="ple_proj", b_shards=N_CHIPS)

    dh3, dgl, de, loss_part, dg_ple_post = _final(h2, gl, e, row(W["ple_norm_post"]), tgt)
    gW = {}
    out_g, out_d, out_m, out_v = {}, {}, {}, {}

    def scatter_start(names, tag):
        parts = [gW[n] if gW[n].ndim == 3 else gW[n].reshape((N_CHIPS, -1, gW[n].shape[1])) for n in names]
        sems, land, src, token = _exchange_start(
            [_place_own(part, gather=False, name="rs_place_" + n) for n, part in zip(names, parts)], parts,
            [list(range(len(names)))], name=f"rs_start_{tag}")
        return (names, sems[0], land, src), token

    def scatter_sums(batches, after):
        names, sums = [], []
        for tag, (batch_names, sems, land, src) in batches:
            landed = _exchange_wait(land, src, sems, after, name=f"rs_wait_{tag}")
            names += batch_names
            sums += [_sum_partials(l, name="sum_" + n) for n, l in zip(batch_names, landed)]
        return names, sums

    def apply(names, sums, theirs):
        for n, a, b in zip(names, sums, theirs):
            out_g[n], out_d[n], out_m[n], out_v[n] = _adamw_pair(a, b, W[n], Mo[n], Vo[n], name="adamw_" + n)

    def swap_begin(batches, after, tag):
        names, sums = scatter_sums(batches, after)
        sems, sums, lands, token = _swap_start(sums, name=f"swap_start_{tag}")
        return (names, sems, sums, lands), token

    def swap_end(swap, after, tag):
        names, sems, sums, lands = swap
        sums, theirs = _swap_wait(sums, lands, sems, after, name=f"swap_wait_{tag}")
        apply(names, sums, theirs)

    def scatter_finish(batch, after, tag):
        names, sums = scatter_sums([(tag, batch)], after)
        apply(names, sums, _swap_with_sibling(sums, name=f"swap_{tag}"))

    gW["w_ple_proj"] = _matmul(ps.astype(BF16), de, name="d_w_ple_proj", ta=True, out_dtype=BF16, out_shards=N_CHIPS)
    gW["w_ple_gate"] = _matmul(hn3, dgl, name="d_w_ple_gate", ta=True, out_dtype=BF16)
    dhn3 = _matmul(dgl, w_pg_f, name="d_hn3", tb=True)
    dh2, dff, dg_ple_pre, dg_mlp_post = _bwd_res_norm(
        dh3, dhn3, h2, row(W["ple_norm_pre"]), ff, row(W["mlp_norm_post"]), name="bwd_res_mlp")
    gW["w_down"] = _matmul(act, dff, name="d_w_down", ta=True, out_dtype=BF16)
    batch1, token1 = scatter_start(("w_ple_proj", "w_ple_gate", "w_down"), 1)
    dup = _matmul(dff, w_down_f, name="d_up", tb=True, after=token1, relu2_of=up, out_dtype=BF16)
    gW["w_up"] = _matmul(hn2, dup, name="d_w_up", ta=True, out_dtype=BF16, out_shards=N_CHIPS)
    batch2, token2 = scatter_start(("w_up",), 2)
    dhn2 = _matmul(dup, w_up_f, name="d_hn2", tb=True, b_shards=N_CHIPS, after=token2)
    dh1, dmo, dg_mlp_pre, dg_mix_post = _bwd_res_norm(
        dh2, dhn2, h1, row(W["mlp_norm_pre"]), mo, row(W["mix_norm_post"]), name="bwd_res_mix")
    gW["w_out"] = _matmul(mixed, dmo, name="d_w_out", ta=True, out_dtype=BF16)
    dmixed = _matmul(dmo, w_out_f, name="d_mixed", tb=True)
    dattn_b, dd_b, dz, dy2a, dg_attn, dg_ssm, db_glu = _mix_bwd(
        dmixed, attn, y1, z, row(W["b_glu"]), row(W["attn_out_norm"]), row(W["ssm_out_norm"]))
    gW["w_glu"] = _matmul(y2b, dz, name="d_w_glu", ta=True, out_dtype=BF16)
    batch3, token3 = scatter_start(("w_out", "w_glu"), 3)
    dy2b = _matmul(dz, w_glu_f, name="d_y2", tb=True, after=token3)
    du, dar8, dai8, dcr_e, dci_e, dbr_e, dbi_e, dd8 = _ssm_bwd(
        u, y1, dy2a, dy2b, st_r, st_i, ends_r, ends_i, lr_e, li_e, ldt_e, bre_e, bim_e, cre_e, cim_e, d_row)
    swap_a, token_a = swap_begin([(1, batch1)], du, "a")
    dlr_e, dli_e, dldt_e, dbre_e, dbim_e = _ssm_param_bwd(dar8, dai8, dbr_e, dbi_e, lr_e, li_e, ldt_e, bre_e, bim_e)

    dqs, dks, dvs = zip(*[_attn_bwd(qb, da, l, dd_, d, heads, token_a)
                          for d, qb, da, l, dd_ in zip(DILATIONS, qkv_b, dattn_b, lse_b, dd_b)])
    dproj = _dproj_join(dqs, dks, dvs, du)
    swap_end(swap_a, dproj, "a")
    swap_b, token_b = swap_begin([(2, batch2), (3, batch3)], dproj, "b")
    gW["w_in"] = _matmul(hn1, dproj, name="d_w_in", ta=True, out_dtype=BF16, out_shards=N_CHIPS, after=token_b)
    batch4, token4 = scatter_start(("w_in",), 4)
    dhn1 = _matmul(dproj, w_in_f, name="d_hn1", tb=True, b_shards=N_CHIPS, after=token4)
    grad_x, dg_mix_pre = _bwd_first(dh1, dhn1, xs, row(W["mix_norm_pre"]))
    swap_end(swap_b, grad_x, "b")
    scatter_finish(batch4, grad_x, 4)

    small_g = {
        "mix_norm_pre": dg_mix_pre, "lam_re": dlr_e.reshape(G, SSM_P), "lam_im": dli_e.reshape(G, SSM_P),
        "log_dt": dldt_e.reshape(G, SSM_P)[:, 0], "ssm_b_re": _collapse_b(dbre_e), "ssm_b_im": _collapse_b(dbim_e),
        "ssm_c_re": _collapse_c(dcr_e), "ssm_c_im": _collapse_c(dci_e), "ssm_d": dd8.sum(axis=1).reshape(-1),
        "b_glu": db_glu, "attn_out_norm": dg_attn, "ssm_out_norm": dg_ssm, "mix_norm_post": dg_mix_post,
        "mlp_norm_pre": dg_mlp_pre, "mlp_norm_post": dg_mlp_post, "ple_norm_pre": dg_ple_pre,
        "ple_norm_post": dg_ple_post,
    }
    g_pack, layout = _pack([small_g[n].reshape(W[n].shape) for n in SMALL])
    w_pack, _ = _pack([W[n] for n in SMALL])
    m_pack, _ = _pack([Mo[n] for n in SMALL])
    v_pack, _ = _pack([Vo[n] for n in SMALL])
    g_sum = _all_reduce_small(g_pack)
    packed = (g_sum,) + tuple(_adamw_small(g_sum, w_pack, m_pack, v_pack))
    for dst, buf in zip((out_g, out_d, out_m, out_v), packed):
        dst.update(zip(SMALL, _unpack(buf, layout)))

    loss = lax.psum(loss_part[0, 0], ("x", "y", "c"))
    lead = lambda a: a[None]
    return (loss, grad_x[None],
            *[lead(out_g[n]) for n in WEIGHTS], *[lead(out_d[n]) for n in WEIGHTS],
            *[lead(out_m[n]) for n in WEIGHTS], *[lead(out_v[n]) for n in WEIGHTS])
```

```python
import functools
import math

import jax
import jax.numpy as jnp
from jax import lax
from jax.experimental import pallas as pl
from jax.experimental.pallas import tpu as pltpu

F32 = jnp.float32
BF16 = jnp.bfloat16
MESH = pl.DeviceIdType.MESH

RMS_EPS = 1e-6
NEG_INF = -1e30
HEAD_DIM = 128
BLK = 128
DILATIONS = (1, 4, 16)
ATTN_LOOKAHEAD = 3
SSM_C = 16
SSM_P = 64
LANES = 128
GROUPS_PER_BLOCK = LANES // SSM_C
STATE_LANES = GROUPS_PER_BLOCK * SSM_P
SSM_CHUNK = 512
TILE = 8
ADAM_LR, ADAM_B1, ADAM_B2, ADAM_EPS, ADAM_WD, ADAM_STEP = 1e-3, 0.9, 0.999, 1e-8, 0.01, 10
VMEM_LIMIT_BYTES = 56 * 1024 * 1024
N_CHIPS = 4
N_DEV = 8
PACK_ROWS = 256


def _cparams(*sem):
    return pltpu.CompilerParams(dimension_semantics=sem or None, vmem_limit_bytes=VMEM_LIMIT_BYTES)


def _rows(tr, w):
    return pl.BlockSpec((tr, w), lambda i: (i, 0))


def _vec(w):
    return pl.BlockSpec((1, w), lambda i: (0, 0))


def _sigmoid(x):
    return 1.0 / (1.0 + jnp.exp(-x))


def _gelu(x):
    c = math.sqrt(2.0 / math.pi)
    return 0.5 * x * (1.0 + jnp.tanh(c * (x + 0.044715 * x * x * x)))


def _gelu_grad(x):
    c = math.sqrt(2.0 / math.pi)
    th = jnp.tanh(c * (x + 0.044715 * x * x * x))
    return 0.5 * (1.0 + th) + 0.5 * x * (1.0 - th * th) * c * (1.0 + 3.0 * 0.044715 * x * x)


def _rms(x, g):
    r = lax.rsqrt(jnp.mean(x * x, axis=-1, keepdims=True) + RMS_EPS)
    return x * r * g


def _rms_bwd(dy, x, g):
    r = lax.rsqrt(jnp.mean(x * x, axis=-1, keepdims=True) + RMS_EPS)
    n = x * r
    dn = dy * g
    dx = r * (dn - n * jnp.mean(dn * n, axis=-1, keepdims=True))
    return dx, dy * n


def _colsum(a):
    return jnp.sum(a, axis=0, keepdims=True)


def _first(i):
    return i == 0


def _matmul(a, b, *, name, ta=False, tb=False, out_dtype=F32, b_shards=1, out_shards=1, b_cols=None,
            after=None, relu2=False, relu2_of=None, tm=1024, tn=1024, tk=2048):
    if ta:
        K, M = a.shape
    else:
        M, K = a.shape
    if b_shards > 1:
        rows, cols = b.shape[1], b.shape[2] * b_shards
    else:
        rows, cols = b.shape
    N, Kb = (rows, cols) if tb else (cols, rows)
    assert K == Kb, (a.shape, b.shape, ta, tb)
    col0 = 0
    if b_cols is not None:
        assert not tb
        col0, N = b_cols
    tm, tn, tk = min(tm, M), min(tn, N), min(tk, K)
    if b_shards > 1:
        shard_cols = cols // b_shards
        if tb:
            tk = min(tk, shard_cols)
        else:
            tn = min(tn, shard_cols)
    if out_shards > 1:
        tn = min(tn, N // out_shards)
    assert M % tm == 0 and N % tn == 0 and K % tk == 0 and col0 % tn == 0
    nk = K // tk
    j0 = col0 // tn

    a_spec = (pl.BlockSpec((tk, tm), lambda i, j, k: (k, i)) if ta
              else pl.BlockSpec((tm, tk), lambda i, j, k: (i, k)))
    if b_shards > 1:
        if tb:
            per = shard_cols // tk
            b_spec = pl.BlockSpec((None, tn, tk), lambda i, j, k: (k // per, j, k % per))
        else:
            per = shard_cols // tn
            b_spec = pl.BlockSpec((None, tk, tn), lambda i, j, k: ((j + j0) // per, k, (j + j0) % per))
    else:
        b_spec = (pl.BlockSpec((tn, tk), lambda i, j, k: (j, k)) if tb
                  else pl.BlockSpec((tk, tn), lambda i, j, k: (k, j + j0)))
    if out_shards > 1:
        per_o = (N // out_shards) // tn
        out_shape = jax.ShapeDtypeStruct((out_shards, M, N // out_shards), out_dtype)
        out_spec = pl.BlockSpec((None, tm, tn), lambda i, j, k: (j // per_o, i, j % per_o))
    else:
        out_shape = jax.ShapeDtypeStruct((M, N), out_dtype)
        out_spec = pl.BlockSpec((tm, tn), lambda i, j, k: (i, j))
    dims = (((0 if ta else 1,), (1 if tb else 0,)), ((), ()))

    extra, extra_specs = [], []
    if relu2_of is not None:
        assert out_shards == 1 and relu2_of.shape == (M, N)
        extra.append(relu2_of)
        extra_specs.append(pl.BlockSpec((tm, tn), lambda i, j, k: (i, j)))
    if after is not None:
        extra.append(after)
        extra_specs.append(pl.BlockSpec(after.shape, lambda i, j, k: (0, 0)))
    n_in = 2 + len(extra)
    if relu2:
        assert out_shards == 1
        out_shape = (out_shape, jax.ShapeDtypeStruct((M, N), BF16))
        out_spec = (out_spec, out_spec)

    def finish(acc, refs):
        o_ref = refs[n_in]
        if relu2_of is not None:
            acc = acc * (2.0 * jnp.maximum(refs[2][...], 0.0))
        o_ref[...] = acc.astype(o_ref.dtype)
        if relu2:
            r = jnp.maximum(acc, 0.0)
            refs[n_in + 1][...] = (r * r).astype(BF16)

    def body(*refs):
        prod = lax.dot_general(refs[0][...], refs[1][...], dims, preferred_element_type=F32)
        if nk == 1:
            finish(prod, refs)
            return
        acc_ref = refs[-1]
        k = pl.program_id(2)

        @pl.when(k == 0)
        def _():
            acc_ref[...] = prod

        @pl.when(k > 0)
        def _():
            acc_ref[...] += prod

        @pl.when(k == nk - 1)
        def _():
            finish(acc_ref[...], refs)

    return pl.pallas_call(
        body, name=name, out_shape=out_shape, grid=(M // tm, N // tn, nk),
        in_specs=[a_spec, b_spec] + extra_specs, out_specs=out_spec,
        scratch_shapes=[pltpu.VMEM((tm, tn), F32)] if nk > 1 else [],
        compiler_params=_cparams("parallel", "parallel", "arbitrary"),
    )(a, b, *extra)


def _norm_cast(x, g, *, name, tr=256):
    S, D = x.shape
    tr = min(tr, S)

    def body(x_ref, g_ref, o_ref):
        o_ref[...] = _rms(x_ref[...], g_ref[...]).astype(BF16)

    return pl.pallas_call(
        body, name=name, out_shape=jax.ShapeDtypeStruct((S, D), BF16), grid=(S // tr,),
        in_specs=[_rows(tr, D), _vec(D)], out_specs=_rows(tr, D),
        compiler_params=_cparams("parallel"))(x, g)


def _res_norm(res, y, g_post, g_next, *, name, tr=256):
    S, D = res.shape
    tr = min(tr, S)

    def body(res_ref, y_ref, gp_ref, gn_ref, h_ref, hn_ref):
        h = res_ref[...] + _rms(y_ref[...], gp_ref[...])
        h_ref[...] = h
        hn_ref[...] = _rms(h, gn_ref[...]).astype(BF16)

    return pl.pallas_call(
        body, name=name,
        out_shape=(jax.ShapeDtypeStruct((S, D), F32), jax.ShapeDtypeStruct((S, D), BF16)),
        grid=(S // tr,), in_specs=[_rows(tr, D), _rows(tr, D), _vec(D), _vec(D)],
        out_specs=(_rows(tr, D), _rows(tr, D)), compiler_params=_cparams("parallel"))(res, y, g_post, g_next)


def _residue_spec(tr, d, w):
    return pl.BlockSpec((tr // d, d * w), lambda i: (i, 0))


def _residue_shape(S, d, w, dtype):
    return jax.ShapeDtypeStruct((S // d, d * w), dtype)


def _residue_scratch(rows, w):
    return pltpu.VMEM((w // LANES, rows, LANES), F32)


def _fill_strips(scr, val):
    for s in range(scr.shape[0]):
        scr[s] = val[:, s * LANES:(s + 1) * LANES]


def _strips_to_residues(scr, o_ref, d):
    strips, rows, _ = scr.shape
    for r in range(d):
        for s in range(strips):
            col = (r * strips + s) * LANES
            o_ref[:, col:col + LANES] = scr[s, pl.ds(r, rows // d, stride=d), :].astype(o_ref.dtype)


def _to_residues(scr, val, o_ref, d):
    if d == 1:
        o_ref[...] = val.astype(o_ref.dtype)
        return
    _fill_strips(scr, val)
    _strips_to_residues(scr, o_ref, d)


def _from_residues(scr, in_ref, d):
    if d == 1:
        return in_ref[...].astype(F32)
    strips, rows, _ = scr.shape
    for r in range(d):
        for s in range(strips):
            col = (r * strips + s) * LANES
            scr[s, pl.ds(r, rows // d, stride=d), :] = in_ref[:, col:col + LANES].astype(F32)
    return jnp.concatenate([scr[s] for s in range(strips)], axis=1)


def _mix_fwd(os, ls, y1, z, b_glu, g_attn, g_ssm, *, tr=128):
    S, SW = y1.shape
    AW = os[0].shape[1]
    tr = min(tr, S)
    nd = len(DILATIONS)

    def body(*refs):
        o_refs, l_refs = refs[:nd], refs[nd:2 * nd]
        y_ref, z_ref, b_ref, ga_ref, gs_ref, attn_ref = refs[2 * nd:2 * nd + 6]
        lse_refs = refs[2 * nd + 6:3 * nd + 6]
        mixed_ref, scr = refs[3 * nd + 6:]
        ls_ = [_from_residues(scr, l_refs[n], d) for n, d in enumerate(DILATIONS)]
        m = functools.reduce(jnp.maximum, ls_)
        es = [jnp.exp(l - m) for l in ls_]
        tot = functools.reduce(jnp.add, es)
        attn = functools.reduce(jnp.add, [e * _from_residues(scr, o_refs[n], d)
                                          for n, (e, d) in enumerate(zip(es, DILATIONS))]) / tot
        attn_ref[...] = attn
        lse = m + jnp.log(tot)
        for n, d in enumerate(DILATIONS):
            _to_residues(scr, lse, lse_refs[n], d)
        ssm = _gelu(y_ref[...]) * _sigmoid(z_ref[...] + b_ref[...])
        mixed_ref[:, :AW] = _rms(attn, ga_ref[...]).astype(BF16)
        mixed_ref[:, AW:] = _rms(ssm, gs_ref[...]).astype(BF16)

    res_in = [_residue_spec(tr, d, AW) for d in DILATIONS]
    res = pl.pallas_call(
        body, name="mix_fwd",
        out_shape=([jax.ShapeDtypeStruct((S, AW), F32)] + [_residue_shape(S, d, AW, F32) for d in DILATIONS]
                   + [jax.ShapeDtypeStruct((S, AW + SW), BF16)]),
        grid=(S // tr,),
        in_specs=res_in + res_in + [_rows(tr, SW), _rows(tr, SW), _vec(SW), _vec(AW), _vec(SW)],
        out_specs=[_rows(tr, AW)] + res_in + [_rows(tr, AW + SW)],
        scratch_shapes=[_residue_scratch(tr, AW)],
        compiler_params=_cparams("parallel"))(*os, *ls, y1, z, b_glu, g_attn, g_ssm)
    return res[0], res[1:1 + nd], res[1 + nd]


def _final(h2, gl, e, g_post, target, *, tr=128):
    S, D = h2.shape
    tr = min(tr, S)

    def body(h_ref, gl_ref, e_ref, g_ref, t_ref, dh_ref, dgl_ref, de_ref, loss_ref, dg_ref):
        i = pl.program_id(0)
        gate = _sigmoid(gl_ref[...])
        e_ = e_ref[...]
        ge = gate * e_
        g = g_ref[...]
        diff = h_ref[...] + _rms(ge, g) - t_ref[...]
        dh = diff * (1.0 / D)
        dh_ref[...] = dh
        dge, dgrow = _rms_bwd(dh, ge, g)
        dgl_ref[...] = (dge * e_ * gate * (1.0 - gate)).astype(BF16)
        de_ref[...] = (dge * gate).astype(BF16)
        part = _colsum(0.5 * jnp.mean(diff * diff, axis=-1, keepdims=True))

        @pl.when(_first(i))
        def _():
            loss_ref[...] = jnp.zeros_like(loss_ref)
            dg_ref[...] = jnp.zeros_like(dg_ref)

        loss_ref[...] += part + jnp.zeros((1, LANES), F32)
        dg_ref[...] += _colsum(dgrow)

    return pl.pallas_call(
        body, name="final_fwd_bwd",
        out_shape=(jax.ShapeDtypeStruct((S, D), F32), jax.ShapeDtypeStruct((S, D), BF16),
                   jax.ShapeDtypeStruct((S, D), BF16), jax.ShapeDtypeStruct((1, LANES), F32),
                   jax.ShapeDtypeStruct((1, D), F32)),
        grid=(S // tr,),
        in_specs=[_rows(tr, D), _rows(tr, D), _rows(tr, D), _vec(D), _rows(tr, D)],
        out_specs=(_rows(tr, D), _rows(tr, D), _rows(tr, D), _vec(LANES), _vec(D)),
        compiler_params=_cparams("arbitrary"))(h2, gl, e, g_post, target)


def _bwd_res_norm(dh_out, dhn, h, g_next, y, g_post, *, name, tr=128):
    S, D = h.shape
    tr = min(tr, S)

    def body(dho_ref, dhn_ref, h_ref, gn_ref, y_ref, gp_ref, dh_ref, dy_ref, dgn_ref, dgp_ref):
        i = pl.program_id(0)
        dx, dgn_rows = _rms_bwd(dhn_ref[...], h_ref[...], gn_ref[...])
        dh = dho_ref[...] + dx
        dh_ref[...] = dh
        dy, dgp_rows = _rms_bwd(dh, y_ref[...], gp_ref[...])
        dy_ref[...] = dy.astype(BF16)

        @pl.when(_first(i))
        def _():
            dgn_ref[...] = jnp.zeros_like(dgn_ref)
            dgp_ref[...] = jnp.zeros_like(dgp_ref)

        dgn_ref[...] += _colsum(dgn_rows)
        dgp_ref[...] += _colsum(dgp_rows)

    return pl.pallas_call(
        body, name=name,
        out_shape=(jax.ShapeDtypeStruct((S, D), F32), jax.ShapeDtypeStruct((S, D), BF16),
                   jax.ShapeDtypeStruct((1, D), F32), jax.ShapeDtypeStruct((1, D), F32)),
        grid=(S // tr,),
        in_specs=[_rows(tr, D), _rows(tr, D), _rows(tr, D), _vec(D), _rows(tr, D), _vec(D)],
        out_specs=(_rows(tr, D), _rows(tr, D), _vec(D), _vec(D)),
        compiler_params=_cparams("arbitrary"))(dh_out, dhn, h, g_next, y, g_post)


def _bwd_first(dh1, dhn1, x, g1, *, tr=256):
    S, D = x.shape
    tr = min(tr, S)

    def body(dh_ref, dhn_ref, x_ref, g_ref, dx_ref, dg_ref):
        i = pl.program_id(0)
        dx, dg_rows = _rms_bwd(dhn_ref[...], x_ref[...], g_ref[...])
        dx_ref[...] = dh_ref[...] + dx

        @pl.when(_first(i))
        def _():
            dg_ref[...] = jnp.zeros_like(dg_ref)

        dg_ref[...] += _colsum(dg_rows)

    return pl.pallas_call(
        body, name="bwd_first",
        out_shape=(jax.ShapeDtypeStruct((S, D), F32), jax.ShapeDtypeStruct((1, D), F32)),
        grid=(S // tr,), in_specs=[_rows(tr, D), _rows(tr, D), _rows(tr, D), _vec(D)],
        out_specs=(_rows(tr, D), _vec(D)), compiler_params=_cparams("arbitrary"))(dh1, dhn1, x, g1)


def _mix_bwd(dmixed, attn, y1, z, b_glu, g_attn, g_ssm, *, tr=256):
    S, AW = attn.shape
    SW = y1.shape[1]
    tr = min(tr, S)
    heads = AW // HEAD_DIM
    nd = len(DILATIONS)

    def body(*refs):
        dm_ref, a_ref, y_ref, z_ref, b_ref, ga_ref, gs_ref = refs[:7]
        da_refs, dd_refs = refs[7:7 + nd], refs[7 + nd:7 + 2 * nd]
        dz_ref, dy2_ref, dga_ref, dgs_ref, db_ref, scr, dd_scr = refs[7 + 2 * nd:]
        i = pl.program_id(0)
        attn_ = a_ref[...]
        dattn, dga_rows = _rms_bwd(dm_ref[:, :AW], attn_, ga_ref[...])
        prod = dattn * attn_
        for h in range(heads):
            sl = slice(h * HEAD_DIM, (h + 1) * HEAD_DIM)
            dd_scr[:, sl] = jnp.broadcast_to(jnp.sum(prod[:, sl], axis=-1, keepdims=True), (tr, HEAD_DIM))
        for n, d in enumerate(DILATIONS):
            _to_residues(scr, dattn, da_refs[n], d)
            _to_residues(scr, dd_scr[...], dd_refs[n], d)
        y2 = _gelu(y_ref[...])
        gate = _sigmoid(z_ref[...] + b_ref[...])
        dssm, dgs_rows = _rms_bwd(dm_ref[:, AW:], y2 * gate, gs_ref[...])
        dz = dssm * y2 * gate * (1.0 - gate)
        dz_ref[...] = dz.astype(BF16)
        dy2_ref[...] = dssm * gate

        @pl.when(_first(i))
        def _():
            dga_ref[...] = jnp.zeros_like(dga_ref)
            dgs_ref[...] = jnp.zeros_like(dgs_ref)
            db_ref[...] = jnp.zeros_like(db_ref)

        dga_ref[...] += _colsum(dga_rows)
        dgs_ref[...] += _colsum(dgs_rows)
        db_ref[...] += _colsum(dz)

    res_out = [_residue_spec(tr, d, AW) for d in DILATIONS]
    res = pl.pallas_call(
        body, name="mix_bwd",
        out_shape=([_residue_shape(S, d, AW, BF16) for d in DILATIONS]
                   + [_residue_shape(S, d, AW, F32) for d in DILATIONS]
                   + [jax.ShapeDtypeStruct((S, SW), BF16), jax.ShapeDtypeStruct((S, SW), F32),
                      jax.ShapeDtypeStruct((1, AW), F32), jax.ShapeDtypeStruct((1, SW), F32),
                      jax.ShapeDtypeStruct((1, SW), F32)]),
        grid=(S // tr,),
        in_specs=[_rows(tr, AW + SW), _rows(tr, AW), _rows(tr, SW), _rows(tr, SW), _vec(SW), _vec(AW), _vec(SW)],
        out_specs=res_out + res_out + [_rows(tr, SW), _rows(tr, SW), _vec(AW), _vec(SW), _vec(SW)],
        scratch_shapes=[_residue_scratch(tr, AW), pltpu.VMEM((tr, AW), F32)],
        compiler_params=_cparams("arbitrary"))(dmixed, attn, y1, z, b_glu, g_attn, g_ssm)
    return (res[:nd], res[nd:2 * nd]) + tuple(res[2 * nd:])


def _attn_mask2(i):
    row = lax.broadcasted_iota(jnp.int32, (BLK, 2 * BLK), 0)
    col = lax.broadcasted_iota(jnp.int32, (BLK, 2 * BLK), 1)
    return jnp.logical_and(col >= row, jnp.logical_and(col <= row + BLK, jnp.logical_or(col >= BLK, i > 0)))


_NT = (((1,), (1,)), ((), ()))
_TN = (((0,), (0,)), ((), ()))


def _attn_in_specs(width, block_of):
    def at(part, prev):
        def index(r, i):
            blk = block_of(i)
            return (part, jnp.maximum(blk - 1, 0) if prev else blk, r)
        return pl.BlockSpec((None, BLK, width), index)
    return [at(0, False), at(1, False), at(1, True), at(2, False), at(2, True)]


def _proj_qkv(hn, w_in_f, *, tm=1024):
    S, D = hn.shape
    AW = w_in_f.shape[2]
    tm = min(tm, S)

    def body(a_ref, b_ref, *rest):
        o_refs, scr = rest[:-1], rest[-1]
        prod = jnp.dot(a_ref[...], b_ref[...], preferred_element_type=F32)
        _fill_strips(scr, prod)
        for o_ref, d in zip(o_refs, DILATIONS):
            if d == 1:
                o_ref[...] = prod.astype(BF16)
            else:
                _strips_to_residues(scr, o_ref, d)

    return pl.pallas_call(
        body, name="proj_qkv",
        out_shape=[jax.ShapeDtypeStruct((3, S // d, d * AW), BF16) for d in DILATIONS], grid=(S // tm, 3),
        in_specs=[pl.BlockSpec((tm, D), lambda i, j: (i, 0)), pl.BlockSpec((None, D, AW), lambda i, j: (j, 0, 0))],
        out_specs=[pl.BlockSpec((None, tm // d, d * AW), lambda i, j: (j, i, 0)) for d in DILATIONS],
        scratch_shapes=[_residue_scratch(tm, AW)],
        compiler_params=_cparams("parallel", "parallel"))(hn, w_in_f)


def _attn_fwd(qkv, d, heads):
    M = qkv.shape[1]
    nb = M // BLK
    width = heads * HEAD_DIM
    scale = 1.0 / math.sqrt(HEAD_DIM)

    def body(q_ref, kc_ref, kp_ref, vc_ref, vp_ref, o_ref, l_ref):
        mask = _attn_mask2(pl.program_id(1))
        ones = jnp.ones((2 * BLK, HEAD_DIM), BF16)

        def scores(h):
            sl = slice(h * HEAD_DIM, (h + 1) * HEAD_DIM)
            k2 = jnp.concatenate([kp_ref[:, sl], kc_ref[:, sl]], axis=0)
            return lax.dot_general(q_ref[:, sl], k2, _NT, preferred_element_type=F32)

        ahead = [scores(h) for h in range(min(ATTN_LOOKAHEAD, heads))]
        for h in range(heads):
            sl = slice(h * HEAD_DIM, (h + 1) * HEAD_DIM)
            s = jnp.where(mask, ahead.pop(0) * scale, NEG_INF)
            if h + ATTN_LOOKAHEAD < heads:
                ahead.append(scores(h + ATTN_LOOKAHEAD))
            v2 = jnp.concatenate([vp_ref[:, sl], vc_ref[:, sl]], axis=0)
            m = jnp.max(jnp.maximum(s[:, :BLK], s[:, BLK:]), axis=-1, keepdims=True)
            p = jnp.exp(s - m).astype(BF16)
            tot = jnp.dot(p, ones, preferred_element_type=F32)
            o_ref[:, sl] = jnp.dot(p, v2, preferred_element_type=F32) / tot
            l_ref[:, sl] = m + jnp.log(tot)

    out_spec = pl.BlockSpec((BLK, width), lambda r, i: (i, r))
    shape = jax.ShapeDtypeStruct((M, d * width), F32)
    return pl.pallas_call(
        body, name=f"attn_fwd_d{d}", out_shape=(shape, shape), grid=(d, nb),
        in_specs=_attn_in_specs(width, lambda i: i), out_specs=(out_spec, out_spec),
        compiler_params=_cparams("parallel", "parallel"))(qkv, qkv, qkv, qkv, qkv)


def _attn_bwd(qkv, dattn, lse, dd, d, heads, after):
    M = qkv.shape[1]
    nb = M // BLK
    width = heads * HEAD_DIM
    scale = 1.0 / math.sqrt(HEAD_DIM)

    def block_of(i):
        return nb - 1 - i

    def body(q_ref, kc_ref, kp_ref, vc_ref, vp_ref, da_ref, l_ref, dd_ref, after_ref,
             dq_ref, dk_ref, dv_ref, dk_carry, dv_carry):
        @pl.when(pl.program_id(1) == 0)
        def _():
            dk_carry[...] = jnp.zeros_like(dk_carry)
            dv_carry[...] = jnp.zeros_like(dv_carry)

        mask = _attn_mask2(block_of(pl.program_id(1)))

        def products(h):
            sl = slice(h * HEAD_DIM, (h + 1) * HEAD_DIM)
            k2 = jnp.concatenate([kp_ref[:, sl], kc_ref[:, sl]], axis=0)
            v2 = jnp.concatenate([vp_ref[:, sl], vc_ref[:, sl]], axis=0)
            return (lax.dot_general(q_ref[:, sl], k2, _NT, preferred_element_type=F32),
                    lax.dot_general(da_ref[:, sl], v2, _NT, preferred_element_type=F32), k2)

        ahead = [products(h) for h in range(min(ATTN_LOOKAHEAD, heads))]
        for h in range(heads):
            sl = slice(h * HEAD_DIM, (h + 1) * HEAD_DIM)
            qk, dp, k2 = ahead.pop(0)
            if h + ATTN_LOOKAHEAD < heads:
                ahead.append(products(h + ATTN_LOOKAHEAD))
            q, da = q_ref[:, sl], da_ref[:, sl]
            lse_ = jnp.concatenate([l_ref[:, sl], l_ref[:, sl]], axis=1)
            dd_ = jnp.concatenate([dd_ref[:, sl], dd_ref[:, sl]], axis=1)
            p = jnp.where(mask, jnp.exp(jnp.where(mask, qk * scale, NEG_INF) - lse_), 0.0)
            ds = (p * (dp - dd_) * scale).astype(BF16)
            dq_ref[:, sl] = jnp.dot(ds, k2, preferred_element_type=F32).astype(BF16)
            dk2 = lax.dot_general(ds, q, _TN, preferred_element_type=F32)
            dv2 = lax.dot_general(p.astype(BF16), da, _TN, preferred_element_type=F32)
            dk_ref[:, sl] = (dk2[BLK:] + dk_carry[:, sl]).astype(BF16)
            dv_ref[:, sl] = (dv2[BLK:] + dv_carry[:, sl]).astype(BF16)
            dk_carry[:, sl] = dk2[:BLK]
            dv_carry[:, sl] = dv2[:BLK]

    blk = pl.BlockSpec((BLK, width), lambda r, i: (block_of(i), r))
    shape = jax.ShapeDtypeStruct((M, d * width), BF16)
    return pl.pallas_call(
        body, name=f"attn_bwd_d{d}", out_shape=(shape,) * 3, grid=(d, nb),
        in_specs=(_attn_in_specs(width, block_of) + [blk, blk, blk]
                  + [pl.BlockSpec(after.shape, lambda r, i: (0, 0))]), out_specs=(blk,) * 3,
        scratch_shapes=[pltpu.VMEM((BLK, width), F32), pltpu.VMEM((BLK, width), F32)],
        compiler_params=_cparams("arbitrary", "arbitrary"))(qkv, qkv, qkv, qkv, qkv, dattn, lse, dd, after)


def _dproj_join(dqs, dks, dvs, du, *, tr=256):
    S, SW = du.shape
    AW = dqs[0].shape[1]
    tr = min(tr, S)
    nd = len(DILATIONS)

    def body(*refs):
        du_ref, out_ref, scr = refs[3 * nd:]
        for part in range(3):
            total = functools.reduce(jnp.add, [_from_residues(scr, refs[part * nd + n], d)
                                               for n, d in enumerate(DILATIONS)])
            out_ref[:, part * AW:(part + 1) * AW] = total.astype(BF16)
        out_ref[:, 3 * AW:] = du_ref[...].astype(BF16)

    return pl.pallas_call(
        body, name="dproj_join", out_shape=jax.ShapeDtypeStruct((S, 3 * AW + SW), BF16), grid=(S // tr,),
        in_specs=[_residue_spec(tr, d, AW) for d in DILATIONS] * 3 + [_rows(tr, SW)],
        out_specs=_rows(tr, 3 * AW + SW), scratch_shapes=[_residue_scratch(tr, AW)],
        compiler_params=_cparams("parallel"))(*dqs, *dks, *dvs, du)


def _ssm_disc(lr, li, ldt):
    dt = jnp.exp(ldt)
    mag = jnp.exp(lr * dt)
    ar = mag * jnp.cos(li * dt)
    ai = mag * jnp.sin(li * dt)
    nr = ar - 1.0
    den = lr * lr + li * li
    return ar, ai, (nr * lr + ai * li) / den, (ai * lr - nr * li) / den


def _ssm_tile_powers(lr, li, ldt, reverse):
    t = lax.broadcasted_iota(jnp.int32, (TILE, 1), 0)
    n = (TILE - t if reverse else t + 1).astype(F32)
    dt = jnp.exp(ldt)
    mag = jnp.exp(n * (lr * dt))
    ang = n * (li * dt)
    return mag * jnp.cos(ang), mag * jnp.sin(ang) * (-1.0 if reverse else 1.0)


def _cmul(ar, ai, br, bi):
    return ar * br - ai * bi, ar * bi + ai * br


def _scan(xr, xi, ar, ai, pr, pi, cr, ci, reverse):
    T = xr.shape[0]
    sub = lax.broadcasted_iota(jnp.int32, xr.shape, 0) & (TILE - 1)
    sh = 1
    while sh < TILE:
        if reverse:
            keep = sub < TILE - sh
            sr, si = pltpu.roll(xr, T - sh, 0), pltpu.roll(xi, T - sh, 0)
        else:
            keep = sub >= sh
            sr, si = pltpu.roll(xr, sh, 0), pltpu.roll(xi, sh, 0)
        sr, si = jnp.where(keep, sr, 0.0), jnp.where(keep, si, 0.0)
        qr, qi = _cmul(ar, ai, sr, si)
        xr, xi = xr + qr, xi + qi
        ar, ai = _cmul(ar, ai, ar, ai)
        sh *= 2
    n = T // TILE
    out_r, out_i = [None] * n, [None] * n
    edge = 0 if reverse else TILE - 1
    for j in (reversed(range(n)) if reverse else range(n)):
        er, ei = _cmul(pr, pi, cr, ci)
        sr, si = xr[j * TILE:(j + 1) * TILE] + er, xi[j * TILE:(j + 1) * TILE] + ei
        out_r[j], out_i[j] = sr, si
        cr, ci = sr[edge:edge + 1], si[edge:edge + 1]
    return jnp.concatenate(out_r, axis=0), jnp.concatenate(out_i, axis=0), cr, ci


def _ssm_specs(T, nch, rev):
    def t_of(c):
        return nch - 1 - c if rev else c
    tok = pl.BlockSpec((T, LANES), lambda j, c: (t_of(c), j))
    par = pl.BlockSpec((None, 1, STATE_LANES), lambda j, c: (j, 0, 0))
    bmat = pl.BlockSpec((None, LANES, STATE_LANES), lambda j, c: (j, 0, 0))
    cmat = pl.BlockSpec((None, STATE_LANES, LANES), lambda j, c: (j, 0, 0))
    dvec = pl.BlockSpec((1, LANES), lambda j, c: (0, j))
    return tok, par, bmat, cmat, dvec


def _ssm_fwd(u, lr_e, li_e, ldt_e, bre_e, bim_e, cre_e, cim_e, d_skip):
    S, SW = u.shape
    T = min(SSM_CHUNK, S)
    nch, nbk = S // T, SW // LANES
    tok, par, bmat, cmat, dvec = _ssm_specs(T, nch, False)
    state_spec = pl.BlockSpec((T, STATE_LANES), lambda j, c: (c, j))
    carry_spec = pl.BlockSpec((None, 1, STATE_LANES), lambda j, c: (c, 0, j))

    def body(u_ref, lr_ref, li_ref, ldt_ref, bre_ref, bim_ref, cre_ref, cim_ref, d_ref,
             y_ref, y2_ref, sr_ref, si_ref, er_ref, ei_ref, bbr, bbi, a_scr, pw, carry):
        c = pl.program_id(1)

        @pl.when(c == 0)
        def _():
            lr, li, ldt = lr_ref[...], li_ref[...], ldt_ref[...]
            ar, ai, kr, ki = _ssm_disc(lr, li, ldt)
            a_scr[0], a_scr[1] = ar, ai
            bbr[...] = (kr * bre_ref[...] - ki * bim_ref[...]).astype(BF16)
            bbi[...] = (kr * bim_ref[...] + ki * bre_ref[...]).astype(BF16)
            pw[0], pw[1] = _ssm_tile_powers(lr, li, ldt, False)
            carry[...] = jnp.zeros_like(carry)

        u_ = u_ref[...]
        ub = u_.astype(BF16)
        sr, si, cr, ci = _scan(jnp.dot(ub, bbr[...], preferred_element_type=F32),
                               jnp.dot(ub, bbi[...], preferred_element_type=F32),
                               a_scr[0], a_scr[1], pw[0], pw[1], carry[0], carry[1], False)
        carry[0], carry[1] = cr, ci
        er_ref[...], ei_ref[...] = cr, ci
        sr_ref[...], si_ref[...] = sr, si
        y0 = (jnp.dot(sr.astype(BF16), cre_ref[...].astype(BF16), preferred_element_type=F32)
              - jnp.dot(si.astype(BF16), cim_ref[...].astype(BF16), preferred_element_type=F32))
        y1 = y0 + d_ref[...] * u_
        y_ref[...] = y1
        y2_ref[...] = _gelu(y1).astype(BF16)

    states = jax.ShapeDtypeStruct((S, nbk * STATE_LANES), F32)
    ends = jax.ShapeDtypeStruct((nch, 1, nbk * STATE_LANES), F32)
    return pl.pallas_call(
        body, name="ssm_fwd",
        out_shape=(jax.ShapeDtypeStruct((S, SW), F32), jax.ShapeDtypeStruct((S, SW), BF16), states, states, ends, ends),
        grid=(nbk, nch), in_specs=[tok, par, par, par, bmat, bmat, cmat, cmat, dvec],
        out_specs=(tok, tok, state_spec, state_spec, carry_spec, carry_spec),
        scratch_shapes=[pltpu.VMEM((LANES, STATE_LANES), BF16), pltpu.VMEM((LANES, STATE_LANES), BF16),
                        pltpu.VMEM((2, 1, STATE_LANES), F32), pltpu.VMEM((2, TILE, STATE_LANES), F32),
                        pltpu.VMEM((2, 1, STATE_LANES), F32)],
        compiler_params=_cparams("arbitrary", "arbitrary"),
    )(u, lr_e, li_e, ldt_e, bre_e, bim_e, cre_e, cim_e, d_skip)


def _ssm_bwd(u, y1, dy2a, dy2b, st_r, st_i, ends_r, ends_i, lr_e, li_e, ldt_e, bre_e, bim_e, cre_e, cim_e, d_skip):
    S, SW = u.shape
    T = min(SSM_CHUNK, S)
    nch, nbk = S // T, SW // LANES
    tok, par, bmat, cmat, dvec = _ssm_specs(T, nch, True)
    state_spec = pl.BlockSpec((T, STATE_LANES), lambda j, c: (nch - 1 - c, j))
    prev_spec = pl.BlockSpec((None, 1, STATE_LANES), lambda j, c: (jnp.maximum(nch - 2 - c, 0), 0, j))
    acc8 = pl.BlockSpec((None, 8, STATE_LANES), lambda j, c: (j, 0, 0))
    dd8 = pl.BlockSpec((None, 8, LANES), lambda j, c: (j, 0, 0))

    def body(u_ref, y_ref, da_ref, db_ref, sr_ref, si_ref, pr_ref, pi_ref, lr_ref, li_ref, ldt_ref,
             bre_ref, bim_ref, cre_ref, cim_ref, d_ref,
             du_ref, dar_ref, dai_ref, dcr_ref, dci_ref, dbr_ref, dbi_ref, ddk_ref,
             bbr, bbi, a_scr, pw, carry):
        c = pl.program_id(1)

        @pl.when(c == 0)
        def _():
            lr, li, ldt = lr_ref[...], li_ref[...], ldt_ref[...]
            ar, ai, kr, ki = _ssm_disc(lr, li, ldt)
            a_scr[0], a_scr[1] = ar, -ai
            bbr[...] = (kr * bre_ref[...] - ki * bim_ref[...]).astype(BF16)
            bbi[...] = (kr * bim_ref[...] + ki * bre_ref[...]).astype(BF16)
            pw[0], pw[1] = _ssm_tile_powers(lr, li, ldt, True)
            carry[...] = jnp.zeros_like(carry)
            for ref in (dar_ref, dai_ref, dcr_ref, dci_ref, dbr_ref, dbi_ref, ddk_ref):
                ref[...] = jnp.zeros_like(ref)

        u_ = u_ref[...]
        ub = u_.astype(BF16)
        dy1 = (da_ref[...] + db_ref[...]) * _gelu_grad(y_ref[...])
        dyb = dy1.astype(BF16)

        sr, si = sr_ref[...], si_ref[...]
        has_prev = c < nch - 1
        s0r = jnp.where(has_prev, pr_ref[...], 0.0)
        s0i = jnp.where(has_prev, pi_ref[...], 0.0)

        cre_b, cim_b = cre_ref[...].astype(BF16), cim_ref[...].astype(BF16)
        gr, gi, cr, ci = _scan(lax.dot_general(dyb, cre_b, _NT, preferred_element_type=F32),
                               -lax.dot_general(dyb, cim_b, _NT, preferred_element_type=F32),
                               a_scr[0], a_scr[1], pw[0], pw[1], carry[0], carry[1], True)
        carry[0], carry[1] = cr, ci

        row = lax.broadcasted_iota(jnp.int32, (T, STATE_LANES), 0)
        spr = jnp.where(row == 0, s0r, pltpu.roll(sr, 1, 0))
        spi = jnp.where(row == 0, s0i, pltpu.roll(si, 1, 0))

        def fold(a):
            return jnp.sum(a.reshape(T // 8, 8, a.shape[-1]), axis=0)

        dar_ref[...] += fold(gr * spr + gi * spi)
        dai_ref[...] += fold(gi * spr - gr * spi)
        srb, sib, grb, gib = sr.astype(BF16), si.astype(BF16), gr.astype(BF16), gi.astype(BF16)
        dcr_ref[...] += lax.dot_general(srb, dyb, _TN, preferred_element_type=F32)
        dci_ref[...] -= lax.dot_general(sib, dyb, _TN, preferred_element_type=F32)
        dbr_ref[...] += lax.dot_general(ub, grb, _TN, preferred_element_type=F32)
        dbi_ref[...] += lax.dot_general(ub, gib, _TN, preferred_element_type=F32)
        du_ref[...] = (lax.dot_general(grb, bbr[...], _NT, preferred_element_type=F32)
                       + lax.dot_general(gib, bbi[...], _NT, preferred_element_type=F32)
                       + dy1 * d_ref[...])
        ddk_ref[...] += fold(dy1 * u_)

    return pl.pallas_call(
        body, name="ssm_bwd",
        out_shape=(jax.ShapeDtypeStruct((S, SW), F32),
                   jax.ShapeDtypeStruct((nbk, 8, STATE_LANES), F32), jax.ShapeDtypeStruct((nbk, 8, STATE_LANES), F32),
                   jax.ShapeDtypeStruct((nbk, STATE_LANES, LANES), F32), jax.ShapeDtypeStruct((nbk, STATE_LANES, LANES), F32),
                   jax.ShapeDtypeStruct((nbk, LANES, STATE_LANES), F32), jax.ShapeDtypeStruct((nbk, LANES, STATE_LANES), F32),
                   jax.ShapeDtypeStruct((nbk, 8, LANES), F32)),
        grid=(nbk, nch),
        in_specs=[tok, tok, tok, tok, state_spec, state_spec, prev_spec, prev_spec, par, par, par,
                  bmat, bmat, cmat, cmat, dvec],
        out_specs=(tok, acc8, acc8, cmat, cmat, bmat, bmat, dd8),
        scratch_shapes=[pltpu.VMEM((LANES, STATE_LANES), BF16), pltpu.VMEM((LANES, STATE_LANES), BF16),
                        pltpu.VMEM((2, 1, STATE_LANES), F32), pltpu.VMEM((2, TILE, STATE_LANES), F32),
                        pltpu.VMEM((2, 1, STATE_LANES), F32)],
        compiler_params=_cparams("arbitrary", "arbitrary"),
    )(u, y1, dy2a, dy2b, st_r, st_i, ends_r, ends_i, lr_e, li_e, ldt_e, bre_e, bim_e, cre_e, cim_e, d_skip)


def _ssm_param_bwd(dar8, dai8, dbr_e, dbi_e, lr_e, li_e, ldt_e, bre_e, bim_e):
    nbk = lr_e.shape[0]
    par = pl.BlockSpec((None, 1, STATE_LANES), lambda j: (j, 0, 0))
    acc8 = pl.BlockSpec((None, 8, STATE_LANES), lambda j: (j, 0, 0))
    bmat = pl.BlockSpec((None, LANES, STATE_LANES), lambda j: (j, 0, 0))

    def body(dar_ref, dai_ref, dbr_ref, dbi_ref, lr_ref, li_ref, ldt_ref, bre_ref, bim_ref,
             dlr_ref, dli_ref, dldt_ref, dbre_ref, dbim_ref):
        lr, li, ldt = lr_ref[...], li_ref[...], ldt_ref[...]
        (ar, ai, kr, ki), vjp = jax.vjp(_ssm_disc, lr, li, ldt)
        dbr, dbi, bre, bim = dbr_ref[...], dbi_ref[...], bre_ref[...], bim_ref[...]
        dbre_ref[...] = kr * dbr + ki * dbi
        dbim_ref[...] = kr * dbi - ki * dbr
        dkr = _colsum(dbr * bre + dbi * bim)
        dki = _colsum(dbi * bre - dbr * bim)
        dlr, dli, dldt = vjp((_colsum(dar_ref[...]), _colsum(dai_ref[...]), dkr, dki))
        dlr_ref[...] = dlr
        dli_ref[...] = dli
        tot = jnp.broadcast_to(dldt, (8, STATE_LANES))
        sh = 1
        while sh < SSM_P:
            tot = tot + pltpu.roll(tot, STATE_LANES - sh, 1)
            sh *= 2
        dldt_ref[...] = tot[:1]

    vec = jax.ShapeDtypeStruct((nbk, 1, STATE_LANES), F32)
    mat = jax.ShapeDtypeStruct((nbk, LANES, STATE_LANES), F32)
    return pl.pallas_call(
        body, name="ssm_param_bwd", out_shape=(vec, vec, vec, mat, mat), grid=(nbk,),
        in_specs=[acc8, acc8, bmat, bmat, par, par, par, bmat, bmat],
        out_specs=(par, par, par, bmat, bmat), compiler_params=_cparams("parallel"),
    )(dar8, dai8, dbr_e, dbi_e, lr_e, li_e, ldt_e, bre_e, bim_e)


def _expand_b(b):
    G = b.shape[0]
    bt = b.transpose(0, 2, 1).reshape(G // GROUPS_PER_BLOCK, GROUPS_PER_BLOCK, SSM_C, SSM_P)
    eye = jnp.eye(GROUPS_PER_BLOCK, dtype=b.dtype)
    return (bt[:, :, :, None, :] * eye[None, :, None, :, None]).reshape(G // GROUPS_PER_BLOCK, LANES, STATE_LANES)


def _collapse_b(be):
    nbk = be.shape[0]
    eye = jnp.eye(GROUPS_PER_BLOCK, dtype=be.dtype)
    d5 = be.reshape(nbk, GROUPS_PER_BLOCK, SSM_C, GROUPS_PER_BLOCK, SSM_P)
    d4 = (d5 * eye[None, :, None, :, None]).sum(axis=3)
    return d4.transpose(0, 1, 3, 2).reshape(nbk * GROUPS_PER_BLOCK, SSM_P, SSM_C)


def _expand_c(cm):
    G = cm.shape[0]
    ct = cm.transpose(0, 2, 1).reshape(G // GROUPS_PER_BLOCK, GROUPS_PER_BLOCK, SSM_P, SSM_C)
    eye = jnp.eye(GROUPS_PER_BLOCK, dtype=cm.dtype)
    return (ct[:, :, :, None, :] * eye[None, :, None, :, None]).reshape(G // GROUPS_PER_BLOCK, STATE_LANES, LANES)


def _collapse_c(ce):
    nbk = ce.shape[0]
    eye = jnp.eye(GROUPS_PER_BLOCK, dtype=ce.dtype)
    d5 = ce.reshape(nbk, GROUPS_PER_BLOCK, SSM_P, GROUPS_PER_BLOCK, SSM_C)
    d4 = (d5 * eye[None, :, None, :, None]).sum(axis=3)
    return d4.transpose(0, 1, 3, 2).reshape(nbk * GROUPS_PER_BLOCK, SSM_C, SSM_P)


def _place():
    x, y, c = lax.axis_index("x"), lax.axis_index("y"), lax.axis_index("c")
    return x, y, c


def _other_chips(x, y):
    return [(1 - x, y), (x, 1 - y), (1 - x, 1 - y)]


_ANY = pl.BlockSpec(memory_space=pl.ANY)


_HBM = pl.BlockSpec(memory_space=pltpu.HBM)
_SEM = pl.BlockSpec(memory_space=pltpu.SEMAPHORE)
_EFFECT = pltpu.SideEffectType.DATAFLOW_SIDE_EFFECTING
_TOKEN = jax.ShapeDtypeStruct((8, LANES), F32)


def _hbm(a):
    return pltpu.with_memory_space_constraint(a, pltpu.HBM)


def _place_own(src, *, gather, name, tr=512):
    R, C = src.shape[-2:]
    tr = min(tr, R)
    x, y, _ = _place()
    me = (2 * x + y).astype(jnp.int32).reshape(1)

    def body(me_ref, s_ref, o_ref):
        o_ref[...] = s_ref[...].astype(BF16)

    own = pl.BlockSpec((None, tr, C), lambda i, me_ref: (me_ref[0], i, 0))
    grid_spec = pltpu.PrefetchScalarGridSpec(
        num_scalar_prefetch=1, grid=(R // tr,),
        in_specs=[pl.BlockSpec((tr, C), lambda i, me_ref: (i, 0)) if gather else own], out_specs=own)
    return pl.pallas_call(
        body, name=name, grid_spec=grid_spec, out_shape=jax.ShapeDtypeStruct((N_CHIPS, R, C), BF16),
        compiler_params=_cparams("parallel"))(me, src)


def _exchange_copy(src_slot, land_slot, send, recv, k, j, peer, c):
    return pltpu.make_async_remote_copy(
        src_ref=src_slot, dst_ref=land_slot, send_sem=send.at[3 * k + j], recv_sem=recv.at[3 * k + j],
        device_id=(peer[0], peer[1], c), device_id_type=MESH)


def _exchange_start(lands, srcs, groups, *, name):
    n, ng = len(lands), len(groups)
    bufs = list(lands) + list(srcs)
    nb = len(bufs)

    def body(*refs):
        lnd, src, sems = refs[:n], refs[n:nb], refs[nb:nb + 2 * ng]
        token = refs[2 * nb + 2 * ng]
        x, y, c = _place()
        me = 2 * x + y
        for gi, group in enumerate(groups):
            for k, w in enumerate(group):
                for j, peer in enumerate(_other_chips(x, y)):
                    if src:
                        sent, dst = src[w].at[2 * peer[0] + peer[1]], lnd[w].at[me]
                    else:
                        sent = dst = lnd[w].at[me, c]
                    _exchange_copy(sent, dst, sems[2 * gi], sems[2 * gi + 1], k, j, peer, c).start()
        token[...] = jnp.zeros_like(token)

    sem_shapes = [pltpu.SemaphoreType.DMA((3 * len(g),)) for g in groups for _ in range(2)]
    res = pl.pallas_call(
        body, name=name,
        out_shape=sem_shapes + [pltpu.HBM(a.shape, a.dtype) for a in bufs] + [_TOKEN],
        in_specs=[_HBM] * nb,
        out_specs=[_SEM] * (2 * ng) + [_HBM] * nb + [pl.BlockSpec(memory_space=pltpu.VMEM)],
        input_output_aliases={i: 2 * ng + i for i in range(nb)},
        compiler_params=pltpu.CompilerParams(has_side_effects=_EFFECT),
    )(*[_hbm(a) for a in bufs])
    sems = [(res[2 * gi], res[2 * gi + 1]) for gi in range(ng)]
    return sems, res[2 * ng:2 * ng + n], res[2 * ng + n:2 * ng + nb], res[-1]


def _exchange_wait(lands, srcs, sems, after, *, name):
    n = len(lands)
    bufs = list(lands) + list(srcs)
    nb = len(bufs)
    send_sems, recv_sems = sems

    def body(*refs):
        lnd, src, send, recv = refs[:n], refs[n:nb], refs[nb], refs[nb + 1]
        x, y, c = _place()
        for k in range(n):
            for j, peer in enumerate(_other_chips(x, y)):
                slot = 2 * peer[0] + peer[1]
                if src:
                    copy = _exchange_copy(src[k].at[slot], lnd[k].at[slot], send, recv, k, j, peer, c)
                else:
                    copy = _exchange_copy(lnd[k].at[slot, c], lnd[k].at[slot, c], send, recv, k, j, peer, c)
                copy.wait_send()
                copy.wait_recv()

    res = pl.pallas_call(
        body, name=name, out_shape=[pltpu.HBM(a.shape, a.dtype) for a in bufs],
        in_specs=[_HBM] * nb + [_SEM, _SEM, _ANY], out_specs=[_HBM] * nb,
        input_output_aliases={i: i for i in range(nb)},
        compiler_params=pltpu.CompilerParams(has_side_effects=_EFFECT),
    )(*bufs, send_sems, recv_sems, after)
    return res[:n]


def _pair_fill(lands, *, name):
    n = len(lands)

    def body(*refs):
        ins, outs, send, recv = refs[:n], refs[n:2 * n], refs[2 * n], refs[2 * n + 1]
        x, y, c = _place()
        for w in range(n):
            for j, (px, py) in enumerate(_other_chips(x, y)):
                slot = 2 * px + py
                pltpu.make_async_remote_copy(
                    src_ref=ins[w].at[slot, c], dst_ref=outs[w].at[slot, c], send_sem=send.at[3 * w + j],
                    recv_sem=recv.at[3 * w + j], device_id=(x, y, 1 - c), device_id_type=MESH).start()
        for w in range(n):
            for j, (px, py) in enumerate(_other_chips(x, y)):
                slot = 2 * px + py
                arrival = pltpu.make_async_remote_copy(
                    src_ref=ins[w].at[slot, c], dst_ref=outs[w].at[slot, 1 - c], send_sem=send.at[3 * w + j],
                    recv_sem=recv.at[3 * w + j], device_id=(x, y, 1 - c), device_id_type=MESH)
                arrival.wait_recv()
                arrival.wait_send()

    return pl.pallas_call(
        body, name=name, out_shape=[jax.ShapeDtypeStruct(a.shape, a.dtype) for a in lands],
        in_specs=[_ANY] * n, out_specs=[_ANY] * n, input_output_aliases={i: i for i in range(n)},
        scratch_shapes=[pltpu.SemaphoreType.DMA((3 * n,)), pltpu.SemaphoreType.DMA((3 * n,))],
    )(*lands)


def _pair_copy(src, dst, send, recv, w, j, sibling):
    return pltpu.make_async_remote_copy(
        src_ref=src, dst_ref=dst, send_sem=send.at[3 * w + j], recv_sem=recv.at[3 * w + j],
        device_id=sibling, device_id_type=MESH)


def _pair_start(lands, *, name):
    n = len(lands)

    def body(*refs):
        bufs, send, recv, token = refs[:n], refs[n], refs[n + 1], refs[2 * n + 2]
        x, y, c = _place()
        for w in range(n):
            for j, (px, py) in enumerate(_other_chips(x, y)):
                half = bufs[w].at[2 * px + py, c]
                _pair_copy(half, half, send, recv, w, j, (x, y, 1 - c)).start()
        token[...] = jnp.zeros_like(token)

    res = pl.pallas_call(
        body, name=name,
        out_shape=[pltpu.SemaphoreType.DMA((3 * n,))] * 2 + [pltpu.HBM(a.shape, a.dtype) for a in lands] + [_TOKEN],
        in_specs=[_HBM] * n, out_specs=[_SEM, _SEM] + [_HBM] * n + [pl.BlockSpec(memory_space=pltpu.VMEM)],
        input_output_aliases={i: 2 + i for i in range(n)},
        compiler_params=pltpu.CompilerParams(has_side_effects=_EFFECT),
    )(*[_hbm(a) for a in lands])
    return (res[0], res[1]), res[2:2 + n], res[-1]


def _pair_wait(lands, sems, after, *, name):
    n = len(lands)

    def body(*refs):
        bufs, send, recv = refs[:n], refs[n], refs[n + 1]
        x, y, c = _place()
        for w in range(n):
            for j, (px, py) in enumerate(_other_chips(x, y)):
                slot = 2 * px + py
                copy = _pair_copy(bufs[w].at[slot, c], bufs[w].at[slot, 1 - c], send, recv, w, j, (x, y, 1 - c))
                copy.wait_send()
                copy.wait_recv()

    return pl.pallas_call(
        body, name=name, out_shape=[pltpu.HBM(a.shape, a.dtype) for a in lands],
        in_specs=[_HBM] * n + [_SEM, _SEM, _ANY], out_specs=[_HBM] * n,
        input_output_aliases={i: i for i in range(n)},
        compiler_params=pltpu.CompilerParams(has_side_effects=_EFFECT),
    )(*lands, *sems, after)


def _sum_partials(land, *, name, tr=256):
    _, R, C = land.shape
    tr = min(tr, R)

    def body(l_ref, o_ref):
        acc = l_ref[0].astype(F32)
        for k in range(1, N_CHIPS):
            acc = acc + l_ref[k].astype(F32)
        o_ref[...] = acc

    return pl.pallas_call(
        body, name=name, out_shape=jax.ShapeDtypeStruct((R, C), F32), grid=(R // tr,),
        in_specs=[pl.BlockSpec((N_CHIPS, tr, C), lambda i: (0, i, 0))], out_specs=_rows(tr, C),
        compiler_params=_cparams("parallel"))(land)


def _swap_with_sibling(sums, *, name):
    n = len(sums)

    def body(*refs):
        ins, outs = refs[:n], refs[n:2 * n]
        send_sems, recv_sems = refs[2 * n:]
        x, y, c = _place()
        copies = [pltpu.make_async_remote_copy(
            src_ref=ins[w], dst_ref=outs[w], send_sem=send_sems.at[w], recv_sem=recv_sems.at[w],
            device_id=(x, y, 1 - c), device_id_type=MESH) for w in range(n)]
        for cp in copies:
            cp.start()
        for cp in copies:
            cp.wait_recv()
            cp.wait_send()

    return pl.pallas_call(
        body, name=name,
        out_shape=[jax.ShapeDtypeStruct(s.shape, s.dtype) for s in sums],
        in_specs=[_ANY] * n, out_specs=[_ANY] * n,
        scratch_shapes=[pltpu.SemaphoreType.DMA((n,)), pltpu.SemaphoreType.DMA((n,))],
    )(*sums)


def _swap_start(sums, *, name):
    n = len(sums)
    bufs = list(sums) + [lax.empty(s.shape, s.dtype) for s in sums]

    def body(*refs):
        src, lnd, send, recv, token = refs[:n], refs[n:2 * n], refs[2 * n], refs[2 * n + 1], refs[4 * n + 2]
        x, y, c = _place()
        for w in range(n):
            pltpu.make_async_remote_copy(
                src_ref=src[w], dst_ref=lnd[w], send_sem=send.at[w], recv_sem=recv.at[w],
                device_id=(x, y, 1 - c), device_id_type=MESH).start()
        token[...] = jnp.zeros_like(token)

    res = pl.pallas_call(
        body, name=name,
        out_shape=[pltpu.SemaphoreType.DMA((n,))] * 2 + [pltpu.HBM(a.shape, a.dtype) for a in bufs] + [_TOKEN],
        in_specs=[_HBM] * (2 * n),
        out_specs=[_SEM, _SEM] + [_HBM] * (2 * n) + [pl.BlockSpec(memory_space=pltpu.VMEM)],
        input_output_aliases={i: 2 + i for i in range(2 * n)},
        compiler_params=pltpu.CompilerParams(has_side_effects=_EFFECT),
    )(*[_hbm(a) for a in bufs])
    return (res[0], res[1]), res[2:2 + n], res[2 + n:2 + 2 * n], res[-1]


def _swap_wait(sums, lands, sems, after, *, name):
    n = len(sums)

    def body(*refs):
        src, lnd, send, recv = refs[:n], refs[n:2 * n], refs[2 * n], refs[2 * n + 1]
        x, y, c = _place()
        for w in range(n):
            copy = pltpu.make_async_remote_copy(
                src_ref=src[w], dst_ref=lnd[w], send_sem=send.at[w], recv_sem=recv.at[w],
                device_id=(x, y, 1 - c), device_id_type=MESH)
            copy.wait_send()
            copy.wait_recv()

    bufs = list(sums) + list(lands)
    res = pl.pallas_call(
        body, name=name, out_shape=[pltpu.HBM(a.shape, a.dtype) for a in bufs],
        in_specs=[_HBM] * (2 * n) + [_SEM, _SEM, _ANY], out_specs=[_HBM] * (2 * n),
        input_output_aliases={i: i for i in range(2 * n)},
        compiler_params=pltpu.CompilerParams(has_side_effects=_EFFECT),
    )(*bufs, *sems, after)
    return res[:n], res[n:]


def _adamw_math(w, g, m, v):
    m = ADAM_B1 * m + (1.0 - ADAM_B1) * g
    v = ADAM_B2 * v + (1.0 - ADAM_B2) * (g * g)
    m_hat = m / (1.0 - ADAM_B1 ** ADAM_STEP)
    v_hat = v / (1.0 - ADAM_B2 ** ADAM_STEP)
    delta = -ADAM_LR * (m_hat / (jnp.sqrt(v_hat) + ADAM_EPS) + ADAM_WD * w)
    return delta, m, v


def _adamw_pair(mine, theirs, w, m, v, *, name, tr=128):
    R, C = w.shape
    tr = min(tr, R)

    def body(a_ref, b_ref, w_ref, m_ref, v_ref, g_ref, d_ref, nm_ref, nv_ref):
        g = a_ref[...] + b_ref[...]
        g_ref[...] = g
        d_ref[...], nm_ref[...], nv_ref[...] = _adamw_math(w_ref[...], g, m_ref[...], v_ref[...])

    shape = jax.ShapeDtypeStruct((R, C), F32)
    return pl.pallas_call(
        body, name=name, out_shape=(shape,) * 4, grid=(R // tr,),
        in_specs=[_rows(tr, C)] * 5, out_specs=(_rows(tr, C),) * 4,
        compiler_params=_cparams("parallel"))(mine, theirs, w, m, v)


def _all_reduce_small(packed):
    R = packed.shape[0]
    half = R // 2

    def body(x_ref, g_ref, sib_ref, pair_ref, land_ref, send_sems, recv_sems):
        x, y, c = _place()
        me = 2 * x + y
        sibling = (x, y, 1 - c)

        swap = pltpu.make_async_remote_copy(
            src_ref=x_ref, dst_ref=sib_ref, send_sem=send_sems.at[0], recv_sem=recv_sems.at[0],
            device_id=sibling, device_id_type=MESH)
        swap.start()
        swap.wait()
        mine, theirs = x_ref[...], sib_ref[...]
        south = c == 0
        pair_ref[...] = jnp.where(south, mine, theirs) + jnp.where(south, theirs, mine)

        land_ref[me] = pair_ref[c]
        for j, (px, py) in enumerate(_other_chips(x, y)):
            pltpu.make_async_remote_copy(
                src_ref=pair_ref.at[c], dst_ref=land_ref.at[me], send_sem=send_sems.at[1 + j],
                recv_sem=recv_sems.at[1 + j], device_id=(px, py, c), device_id_type=MESH).start()
        for j, (px, py) in enumerate(_other_chips(x, y)):
            arrival = pltpu.make_async_remote_copy(
                src_ref=pair_ref.at[c], dst_ref=land_ref.at[2 * px + py], send_sem=send_sems.at[1 + j],
                recv_sem=recv_sems.at[1 + j], device_id=(px, py, c), device_id_type=MESH)
            arrival.wait_recv()
            arrival.wait_send()
        total = land_ref[0]
        for k in range(1, N_CHIPS):
            total = total + land_ref[k]
        g_ref[c] = total

        give = pltpu.make_async_remote_copy(
            src_ref=g_ref.at[c], dst_ref=g_ref.at[c], send_sem=send_sems.at[4], recv_sem=recv_sems.at[4],
            device_id=sibling, device_id_type=MESH)
        give.start()
        take = pltpu.make_async_remote_copy(
            src_ref=g_ref.at[c], dst_ref=g_ref.at[1 - c], send_sem=send_sems.at[4], recv_sem=recv_sems.at[4],
            device_id=sibling, device_id_type=MESH)
        take.wait_recv()
        give.wait_send()

    vm = pl.BlockSpec(memory_space=pltpu.VMEM)
    return pl.pallas_call(
        body, name="all_reduce_small", out_shape=jax.ShapeDtypeStruct((2, half, LANES), F32),
        in_specs=[vm], out_specs=vm,
        scratch_shapes=[pltpu.VMEM((2, half, LANES), F32), pltpu.VMEM((2, half, LANES), F32),
                        pltpu.VMEM((N_CHIPS, half, LANES), F32),
                        pltpu.SemaphoreType.DMA((5,)), pltpu.SemaphoreType.DMA((5,))],
        compiler_params=pltpu.CompilerParams(vmem_limit_bytes=VMEM_LIMIT_BYTES),
    )(packed.reshape(2, half, LANES)).reshape(R, LANES)


def _adamw_small(g, w, m, v):
    R = g.shape[0]
    tr = PACK_ROWS

    def body(g_ref, w_ref, m_ref, v_ref, d_ref, nm_ref, nv_ref):
        d_ref[...], nm_ref[...], nv_ref[...] = _adamw_math(w_ref[...], g_ref[...], m_ref[...], v_ref[...])

    shape = jax.ShapeDtypeStruct((R, LANES), F32)
    return pl.pallas_call(
        body, name="adamw_small", out_shape=(shape,) * 3, grid=(R // tr,),
        in_specs=[_rows(tr, LANES)] * 4, out_specs=(_rows(tr, LANES),) * 3,
        compiler_params=_cparams("parallel"))(g, w, m, v)


def _pack(arrays):
    parts, layout = [], []
    for a in arrays:
        n = a.size
        rows = -(-n // (8 * LANES)) * 8
        flat = jnp.pad(a.reshape(-1).astype(F32), (0, rows * LANES - n))
        parts.append(flat.reshape(rows, LANES))
        layout.append((rows, n, a.shape))
    total = sum(r for r, _, _ in layout)
    parts.append(jnp.zeros((-total % PACK_ROWS, LANES), F32))
    return jnp.concatenate(parts, axis=0), layout


def _unpack(buf, layout):
    out, r0 = [], 0
    for rows, n, shape in layout:
        out.append(buf[r0:r0 + rows].reshape(-1)[:n].reshape(shape))
        r0 += rows
    return out


SMALL = ("mix_norm_pre", "lam_re", "lam_im", "log_dt", "ssm_b_re", "ssm_b_im", "ssm_c_re", "ssm_c_im",
         "ssm_d", "b_glu", "attn_out_norm", "ssm_out_norm", "mix_norm_post", "mlp_norm_pre",
         "mlp_norm_post", "ple_norm_pre", "ple_norm_post")
BIG = ("w_in", "w_glu", "w_out", "w_up", "w_down", "w_ple_gate", "w_ple_proj")
WEIGHTS = ("mix_norm_pre", "w_in", "lam_re", "lam_im", "log_dt", "ssm_b_re", "ssm_b_im", "ssm_c_re",
           "ssm_c_im", "ssm_d", "w_glu", "b_glu", "attn_out_norm", "ssm_out_norm", "w_out",
           "mix_norm_post", "mlp_norm_pre", "w_up", "w_down", "mlp_norm_post", "ple_norm_pre",
           "w_ple_gate", "w_ple_proj", "ple_norm_post")


def kernel(x, p, mix_norm_pre, w_in, lam_re, lam_im, log_dt, ssm_b_re, ssm_b_im, ssm_c_re, ssm_c_im, ssm_d, w_glu, b_glu, attn_out_norm, ssm_out_norm, w_out, mix_norm_post, mlp_norm_pre, w_up, w_down, mlp_norm_post, ple_norm_pre, w_ple_gate, w_ple_proj, ple_norm_post, loss_target, m_mix_norm_pre, m_w_in, m_lam_re, m_lam_im, m_log_dt, m_ssm_b_re, m_ssm_b_im, m_ssm_c_re, m_ssm_c_im, m_ssm_d, m_w_glu, m_b_glu, m_attn_out_norm, m_ssm_out_norm, m_w_out, m_mix_norm_post, m_mlp_norm_pre, m_w_up, m_w_down, m_mlp_norm_post, m_ple_norm_pre, m_w_ple_gate, m_w_ple_proj, m_ple_norm_post, v_mix_norm_pre, v_w_in, v_lam_re, v_lam_im, v_log_dt, v_ssm_b_re, v_ssm_b_im, v_ssm_c_re, v_ssm_c_im, v_ssm_d, v_w_glu, v_b_glu, v_attn_out_norm, v_ssm_out_norm, v_w_out, v_mix_norm_post, v_mlp_norm_pre, v_w_up, v_w_down, v_mlp_norm_post, v_ple_norm_pre, v_w_ple_gate, v_w_ple_proj, v_ple_norm_post):
    args = dict(locals())
    W = {n: args[n][0] for n in WEIGHTS}
    Mo = {n: args["m_" + n][0] for n in WEIGHTS}
    Vo = {n: args["v_" + n][0] for n in WEIGHTS}
    xs, ps, tgt = x[0], p[0, 0], loss_target[0]
    S, D = xs.shape
    SW = W["ssm_d"].shape[0]
    AW = W["attn_out_norm"].shape[0]
    heads = AW // HEAD_DIM
    G = SW // SSM_C
    nbk = SW // LANES
    assert W["w_in"].shape[1] * N_CHIPS == 3 * AW + SW and AW == SW

    row = lambda a: a.reshape(1, -1)

    ag_groups = (("w_in",), ("w_glu", "w_out"), ("w_up",), ("w_down", "w_ple_gate", "w_ple_proj"))
    ag_names = [n for g in ag_groups for n in g]
    def in_halves(a):
        return a.reshape(N_CHIPS, 2, a.shape[1] // 2, a.shape[2])

    ag_sems, ag_land, _, ag_token = _exchange_start(
        [in_halves(_place_own(W[n], gather=True, name="ag_place_" + n)) for n in ag_names], [],
        [[ag_names.index(n) for n in g] for g in ag_groups], name="ag_start")

    def fetched(gi, after):
        return _exchange_wait([ag_land[ag_names.index(n)] for n in ag_groups[gi]], [], ag_sems[gi], after,
                              name=f"ag_wait_{gi}")

    def whole(gis, bufs):
        names = [n for gi in gis for n in ag_groups[gi]]
        return {n: a.reshape(N_CHIPS, -1, a.shape[-1]) for n, a in zip(names, bufs)}

    lr_e = W["lam_re"].reshape(nbk, 1, STATE_LANES)
    li_e = W["lam_im"].reshape(nbk, 1, STATE_LANES)
    ldt_e = jnp.repeat(W["log_dt"], SSM_P).reshape(nbk, 1, STATE_LANES)
    bre_e, bim_e = _expand_b(W["ssm_b_re"]), _expand_b(W["ssm_b_im"])
    cre_e, cim_e = _expand_c(W["ssm_c_re"]), _expand_c(W["ssm_c_im"])
    d_row = row(W["ssm_d"])

    hn1 = _norm_cast(xs, row(W["mix_norm_pre"]) + ag_token[0, 0], name="norm_in")
    w_in_f = whole([0], _pair_fill(fetched(0, hn1), name="ag_pair_0"))["w_in"]
    qkv_b = _proj_qkv(hn1, w_in_f)
    outs, lses = zip(*[_attn_fwd(qb, d, heads) for d, qb in zip(DILATIONS, qkv_b)])
    pair_a_sems, pair_a, pair_a_token = _pair_start(fetched(1, outs[-1]) + fetched(2, outs[-1]), name="ag_pair_start_a")
    u = _matmul(hn1, w_in_f, name="proj_u", b_shards=N_CHIPS, b_cols=(3 * AW, SW), after=pair_a_token)
    y1, y2b, st_r, st_i, ends_r, ends_i = _ssm_fwd(u, lr_e, li_e, ldt_e, bre_e, bim_e, cre_e, cim_e, d_row)
    pair_b_sems, pair_b, pair_b_token = _pair_start(fetched(3, y2b), name="ag_pair_start_b")
    full = whole([1, 2], _pair_wait(pair_a, pair_a_sems, y2b, name="ag_pair_wait_a"))
    w_glu_f = full["w_glu"].reshape(SW, SW)
    w_out_f = full["w_out"].reshape(AW + SW, D)
    w_up_f = full["w_up"]
    z = _matmul(y2b, w_glu_f, name="glu_z", after=pair_b_token)
    attn, lse_b, mixed = _mix_fwd(outs, lses, y1, z, row(W["b_glu"]), row(W["attn_out_norm"]), row(W["ssm_out_norm"]))
    mo = _matmul(mixed, w_out_f, name="mix_out")
    h1, hn2 = _res_norm(xs, mo, row(W["mix_norm_post"]), row(W["mlp_norm_pre"]), name="res_mix")
    up, act = _matmul(hn2, w_up_f, name="mlp_up", b_shards=N_CHIPS, relu2=True)
    full = whole([3], _pair_wait(pair_b, pair_b_sems, act, name="ag_pair_wait_b"))
    w_down_f = full["w_down"].reshape(-1, D)
    w_pg_f = full["w_ple_gate"].reshape(D, D)
    w_pp_f = full["w_ple_proj"]
    ff = _matmul(act, w_down_f, name="mlp_down")
    h2, hn3 = _res_norm(h1, ff, row(W["mlp_norm_post"]), row(W["ple_norm_pre"]), name="res_mlp")
    gl = _matmul(hn3, w_pg_f, name="ple_gate")
    e = _matmul(ps.astype(BF16), w_pp_f, name="ple_proj", b_shards=N_CHIPS)

    dh3, dgl, de, loss_part, dg_ple_post = _final(h2, gl, e, row(W["ple_norm_post"]), tgt)
    gW = {}
    out_g, out_d, out_m, out_v = {}, {}, {}, {}

    def scatter_start(names, tag):
        parts = [gW[n] if gW[n].ndim == 3 else gW[n].reshape((N_CHIPS, -1, gW[n].shape[1])) for n in names]
        sems, land, src, token = _exchange_start(
            [_place_own(part, gather=False, name="rs_place_" + n) for n, part in zip(names, parts)], parts,
            [list(range(len(names)))], name=f"rs_start_{tag}")
        return (names, sems[0], land, src), token

    def scatter_sums(batches, after):
        names, sums = [], []
        for tag, (batch_names, sems, land, src) in batches:
            landed = _exchange_wait(land, src, sems, after, name=f"rs_wait_{tag}")
            names += batch_names
            sums += [_sum_partials(l, name="sum_" + n) for n, l in zip(batch_names, landed)]
        return names, sums

    def apply(names, sums, theirs):
        for n, a, b in zip(names, sums, theirs):
            out_g[n], out_d[n], out_m[n], out_v[n] = _adamw_pair(a, b, W[n], Mo[n], Vo[n], name="adamw_" + n)

    def swap_begin(batches, after, tag):
        names, sums = scatter_sums(batches, after)
        sems, sums, lands, token = _swap_start(sums, name=f"swap_start_{tag}")
        return (names, sems, sums, lands), token

    def swap_end(swap, after, tag):
        names, sems, sums, lands = swap
        sums, theirs = _swap_wait(sums, lands, sems, after, name=f"swap_wait_{tag}")
        apply(names, sums, theirs)

    def scatter_finish(batch, after, tag):
        names, sums = scatter_sums([(tag, batch)], after)
        apply(names, sums, _swap_with_sibling(sums, name=f"swap_{tag}"))

    gW["w_ple_proj"] = _matmul(ps.astype(BF16), de, name="d_w_ple_proj", ta=True, out_dtype=BF16, out_shards=N_CHIPS)
    gW["w_ple_gate"] = _matmul(hn3, dgl, name="d_w_ple_gate", ta=True, out_dtype=BF16)
    dhn3 = _matmul(dgl, w_pg_f, name="d_hn3", tb=True)
    dh2, dff, dg_ple_pre, dg_mlp_post = _bwd_res_norm(
        dh3, dhn3, h2, row(W["ple_norm_pre"]), ff, row(W["mlp_norm_post"]), name="bwd_res_mlp")
    gW["w_down"] = _matmul(act, dff, name="d_w_down", ta=True, out_dtype=BF16)
    batch1, token1 = scatter_start(("w_ple_proj", "w_ple_gate", "w_down"), 1)
    dup = _matmul(dff, w_down_f, name="d_up", tb=True, after=token1, relu2_of=up, out_dtype=BF16)
    gW["w_up"] = _matmul(hn2, dup, name="d_w_up", ta=True, out_dtype=BF16, out_shards=N_CHIPS)
    batch2, token2 = scatter_start(("w_up",), 2)
    dhn2 = _matmul(dup, w_up_f, name="d_hn2", tb=True, b_shards=N_CHIPS, after=token2)
    dh1, dmo, dg_mlp_pre, dg_mix_post = _bwd_res_norm(
        dh2, dhn2, h1, row(W["mlp_norm_pre"]), mo, row(W["mix_norm_post"]), name="bwd_res_mix")
    gW["w_out"] = _matmul(mixed, dmo, name="d_w_out", ta=True, out_dtype=BF16)
    dmixed = _matmul(dmo, w_out_f, name="d_mixed", tb=True)
    dattn_b, dd_b, dz, dy2a, dg_attn, dg_ssm, db_glu = _mix_bwd(
        dmixed, attn, y1, z, row(W["b_glu"]), row(W["attn_out_norm"]), row(W["ssm_out_norm"]))
    gW["w_glu"] = _matmul(y2b, dz, name="d_w_glu", ta=True, out_dtype=BF16)
    batch3, token3 = scatter_start(("w_out", "w_glu"), 3)
    dy2b = _matmul(dz, w_glu_f, name="d_y2", tb=True, after=token3)
    du, dar8, dai8, dcr_e, dci_e, dbr_e, dbi_e, dd8 = _ssm_bwd(
        u, y1, dy2a, dy2b, st_r, st_i, ends_r, ends_i, lr_e, li_e, ldt_e, bre_e, bim_e, cre_e, cim_e, d_row)
    swap_a, token_a = swap_begin([(1, batch1)], du, "a")
    dlr_e, dli_e, dldt_e, dbre_e, dbim_e = _ssm_param_bwd(dar8, dai8, dbr_e, dbi_e, lr_e, li_e, ldt_e, bre_e, bim_e)

    dqs, dks, dvs = zip(*[_attn_bwd(qb, da, l, dd_, d, heads, token_a)
                          for d, qb, da, l, dd_ in zip(DILATIONS, qkv_b, dattn_b, lse_b, dd_b)])
    dproj = _dproj_join(dqs, dks, dvs, du)
    swap_end(swap_a, dproj, "a")
    swap_b, token_b = swap_begin([(2, batch2), (3, batch3)], dproj, "b")
    gW["w_in"] = _matmul(hn1, dproj, name="d_w_in", ta=True, out_dtype=BF16, out_shards=N_CHIPS, after=token_b)
    batch4, token4 = scatter_start(("w_in",), 4)
    dhn1 = _matmul(dproj, w_in_f, name="d_hn1", tb=True, b_shards=N_CHIPS, after=token4)
    grad_x, dg_mix_pre = _bwd_first(dh1, dhn1, xs, row(W["mix_norm_pre"]))
    swap_end(swap_b, grad_x, "b")
    scatter_finish(batch4, grad_x, 4)

    small_g = {
        "mix_norm_pre": dg_mix_pre, "lam_re": dlr_e.reshape(G, SSM_P), "lam_im": dli_e.reshape(G, SSM_P),
        "log_dt": dldt_e.reshape(G, SSM_P)[:, 0], "ssm_b_re": _collapse_b(dbre_e), "ssm_b_im": _collapse_b(dbim_e),
        "ssm_c_re": _collapse_c(dcr_e), "ssm_c_im": _collapse_c(dci_e), "ssm_d": dd8.sum(axis=1).reshape(-1),
        "b_glu": db_glu, "attn_out_norm": dg_attn, "ssm_out_norm": dg_ssm, "mix_norm_post": dg_mix_post,
        "mlp_norm_pre": dg_mlp_pre, "mlp_norm_post": dg_mlp_post, "ple_norm_pre": dg_ple_pre,
        "ple_norm_post": dg_ple_post,
    }
    g_pack, layout = _pack([small_g[n].reshape(W[n].shape) for n in SMALL])
    w_pack, _ = _pack([W[n] for n in SMALL])
    m_pack, _ = _pack([Mo[n] for n in SMALL])
    v_pack, _ = _pack([Vo[n] for n in SMALL])
    g_sum = _all_reduce_small(g_pack)
    packed = (g_sum,) + tuple(_adamw_small(g_sum, w_pack, m_pack, v_pack))
    for dst, buf in zip((out_g, out_d, out_m, out_v), packed):
        dst.update(zip(SMALL, _unpack(buf, layout)))

    loss = lax.psum(loss_part[0, 0], ("x", "y", "c"))
    lead = lambda a: a[None]
    return (loss, grad_x[None],
            *[lead(out_g[n]) for n in WEIGHTS], *[lead(out_d[n]) for n in WEIGHTS],
            *[lead(out_m[n]) for n in WEIGHTS], *[lead(out_v[n]) for n in WEIGHTS])
```

```python
import functools
import math

import jax
import jax.numpy as jnp
from jax import lax
from jax.experimental import pallas as pl
from jax.experimental.pallas import tpu as pltpu

F32 = jnp.float32
BF16 = jnp.bfloat16
MESH = pl.DeviceIdType.MESH

RMS_EPS = 1e-6
NEG_INF = -1e30
HEAD_DIM = 128
BLK = 128
DILATIONS = (1, 4, 16)
ATTN_LOOKAHEAD = 3
SSM_C = 16
SSM_P = 64
LANES = 128
GROUPS_PER_BLOCK = LANES // SSM_C
STATE_LANES = GROUPS_PER_BLOCK * SSM_P
SSM_CHUNK = 512
TILE = 8
ADAM_LR, ADAM_B1, ADAM_B2, ADAM_EPS, ADAM_WD, ADAM_STEP = 1e-3, 0.9, 0.999, 1e-8, 0.01, 10
VMEM_LIMIT_BYTES = 56 * 1024 * 1024
N_CHIPS = 4
N_DEV = 8
PACK_ROWS = 256


def _cparams(*sem):
    return pltpu.CompilerParams(dimension_semantics=sem or None, vmem_limit_bytes=VMEM_LIMIT_BYTES)


def _rows(tr, w):
    return pl.BlockSpec((tr, w), lambda i: (i, 0))


def _vec(w):
    return pl.BlockSpec((1, w), lambda i: (0, 0))


def _sigmoid(x):
    return 1.0 / (1.0 + jnp.exp(-x))


def _gelu(x):
    c = math.sqrt(2.0 / math.pi)
    return 0.5 * x * (1.0 + jnp.tanh(c * (x + 0.044715 * x * x * x)))


def _gelu_grad(x):
    c = math.sqrt(2.0 / math.pi)
    th = jnp.tanh(c * (x + 0.044715 * x * x * x))
    return 0.5 * (1.0 + th) + 0.5 * x * (1.0 - th * th) * c * (1.0 + 3.0 * 0.044715 * x * x)


def _rms(x, g):
    r = lax.rsqrt(jnp.mean(x * x, axis=-1, keepdims=True) + RMS_EPS)
    return x * r * g


def _rms_bwd(dy, x, g):
    r = lax.rsqrt(jnp.mean(x * x, axis=-1, keepdims=True) + RMS_EPS)
    n = x * r
    dn = dy * g
    dx = r * (dn - n * jnp.mean(dn * n, axis=-1, keepdims=True))
    return dx, dy * n


def _colsum(a):
    return jnp.sum(a, axis=0, keepdims=True)


def _first(i):
    return i == 0


def _matmul(a, b, *, name, ta=False, tb=False, out_dtype=F32, b_shards=1, out_shards=1, b_cols=None,
            after=None, relu2=False, relu2_of=None, tm=1024, tn=1024, tk=2048):
    if ta:
        K, M = a.shape
    else:
        M, K = a.shape
    if b_shards > 1:
        rows, cols = b.shape[1], b.shape[2] * b_shards
    else:
        rows, cols = b.shape
    N, Kb = (rows, cols) if tb else (cols, rows)
    assert K == Kb, (a.shape, b.shape, ta, tb)
    col0 = 0
    if b_cols is not None:
        assert not tb
        col0, N = b_cols
    tm, tn, tk = min(tm, M), min(tn, N), min(tk, K)
    if b_shards > 1:
        shard_cols = cols // b_shards
        if tb:
            tk = min(tk, shard_cols)
        else:
            tn = min(tn, shard_cols)
    if out_shards > 1:
        tn = min(tn, N // out_shards)
    assert M % tm == 0 and N % tn == 0 and K % tk == 0 and col0 % tn == 0
    nk = K // tk
    j0 = col0 // tn

    a_spec = (pl.BlockSpec((tk, tm), lambda i, j, k: (k, i)) if ta
              else pl.BlockSpec((tm, tk), lambda i, j, k: (i, k)))
    if b_shards > 1:
        if tb:
            per = shard_cols // tk
            b_spec = pl.BlockSpec((None, tn, tk), lambda i, j, k: (k // per, j, k % per))
        else:
            per = shard_cols // tn
            b_spec = pl.BlockSpec((None, tk, tn), lambda i, j, k: ((j + j0) // per, k, (j + j0) % per))
    else:
        b_spec = (pl.BlockSpec((tn, tk), lambda i, j, k: (j, k)) if tb
                  else pl.BlockSpec((tk, tn), lambda i, j, k: (k, j + j0)))
    if out_shards > 1:
        per_o = (N // out_shards) // tn
        out_shape = jax.ShapeDtypeStruct((out_shards, M, N // out_shards), out_dtype)
        out_spec = pl.BlockSpec((None, tm, tn), lambda i, j, k: (j // per_o, i, j % per_o))
    else:
        out_shape = jax.ShapeDtypeStruct((M, N), out_dtype)
        out_spec = pl.BlockSpec((tm, tn), lambda i, j, k: (i, j))
    dims = (((0 if ta else 1,), (1 if tb else 0,)), ((), ()))

    extra, extra_specs = [], []
    if relu2_of is not None:
        assert out_shards == 1 and relu2_of.shape == (M, N)
        extra.append(relu2_of)
        extra_specs.append(pl.BlockSpec((tm, tn), lambda i, j, k: (i, j)))
    if after is not None:
        extra.append(after)
        extra_specs.append(pl.BlockSpec(after.shape, lambda i, j, k: (0, 0)))
    n_in = 2 + len(extra)
    if relu2:
        assert out_shards == 1
        out_shape = (out_shape, jax.ShapeDtypeStruct((M, N), BF16))
        out_spec = (out_spec, out_spec)

    def finish(acc, refs):
        o_ref = refs[n_in]
        if relu2_of is not None:
            acc = acc * (2.0 * jnp.maximum(refs[2][...], 0.0))
        o_ref[...] = acc.astype(o_ref.dtype)
        if relu2:
            r = jnp.maximum(acc, 0.0)
            refs[n_in + 1][...] = (r * r).astype(BF16)

    def body(*refs):
        prod = lax.dot_general(refs[0][...], refs[1][...], dims, preferred_element_type=F32)
        if nk == 1:
            finish(prod, refs)
            return
        acc_ref = refs[-1]
        k = pl.program_id(2)

        @pl.when(k == 0)
        def _():
            acc_ref[...] = prod

        @pl.when(k > 0)
        def _():
            acc_ref[...] += prod

        @pl.when(k == nk - 1)
        def _():
            finish(acc_ref[...], refs)

    return pl.pallas_call(
        body, name=name, out_shape=out_shape, grid=(M // tm, N // tn, nk),
        in_specs=[a_spec, b_spec] + extra_specs, out_specs=out_spec,
        scratch_shapes=[pltpu.VMEM((tm, tn), F32)] if nk > 1 else [],
        compiler_params=_cparams("parallel", "parallel", "arbitrary"),
    )(a, b, *extra)


def _norm_cast(x, g, *, name, tr=256):
    S, D = x.shape
    tr = min(tr, S)

    def body(x_ref, g_ref, o_ref):
        o_ref[...] = _rms(x_ref[...], g_ref[...]).astype(BF16)

    return pl.pallas_call(
        body, name=name, out_shape=jax.ShapeDtypeStruct((S, D), BF16), grid=(S // tr,),
        in_specs=[_rows(tr, D), _vec(D)], out_specs=_rows(tr, D),
        compiler_params=_cparams("parallel"))(x, g)


def _res_norm(res, y, g_post, g_next, *, name, tr=256):
    S, D = res.shape
    tr = min(tr, S)

    def body(res_ref, y_ref, gp_ref, gn_ref, h_ref, hn_ref):
        h = res_ref[...] + _rms(y_ref[...], gp_ref[...])
        h_ref[...] = h
        hn_ref[...] = _rms(h, gn_ref[...]).astype(BF16)

    return pl.pallas_call(
        body, name=name,
        out_shape=(jax.ShapeDtypeStruct((S, D), F32), jax.ShapeDtypeStruct((S, D), BF16)),
        grid=(S // tr,), in_specs=[_rows(tr, D), _rows(tr, D), _vec(D), _vec(D)],
        out_specs=(_rows(tr, D), _rows(tr, D)), compiler_params=_cparams("parallel"))(res, y, g_post, g_next)


def _residue_spec(tr, d, w):
    return pl.BlockSpec((tr // d, d * w), lambda i: (i, 0))


def _residue_shape(S, d, w, dtype):
    return jax.ShapeDtypeStruct((S // d, d * w), dtype)


def _residue_scratch(rows, w):
    return pltpu.VMEM((w // LANES, rows, LANES), F32)


def _fill_strips(scr, val):
    for s in range(scr.shape[0]):
        scr[s] = val[:, s * LANES:(s + 1) * LANES]


def _strips_to_residues(scr, o_ref, d):
    strips, rows, _ = scr.shape
    for r in range(d):
        for s in range(strips):
            col = (r * strips + s) * LANES
            o_ref[:, col:col + LANES] = scr[s, pl.ds(r, rows // d, stride=d), :].astype(o_ref.dtype)


def _to_residues(scr, val, o_ref, d):
    if d == 1:
        o_ref[...] = val.astype(o_ref.dtype)
        return
    _fill_strips(scr, val)
    _strips_to_residues(scr, o_ref, d)


def _from_residues(scr, in_ref, d):
    if d == 1:
        return in_ref[...].astype(F32)
    strips, rows, _ = scr.shape
    for r in range(d):
        for s in range(strips):
            col = (r * strips + s) * LANES
            scr[s, pl.ds(r, rows // d, stride=d), :] = in_ref[:, col:col + LANES].astype(F32)
    return jnp.concatenate([scr[s] for s in range(strips)], axis=1)


def _mix_fwd(os, ls, y1, z, b_glu, g_attn, g_ssm, *, tr=128):
    S, SW = y1.shape
    AW = os[0].shape[1]
    tr = min(tr, S)
    nd = len(DILATIONS)

    def body(*refs):
        o_refs, l_refs = refs[:nd], refs[nd:2 * nd]
        y_ref, z_ref, b_ref, ga_ref, gs_ref, attn_ref = refs[2 * nd:2 * nd + 6]
        lse_refs = refs[2 * nd + 6:3 * nd + 6]
        mixed_ref, scr = refs[3 * nd + 6:]
        ls_ = [_from_residues(scr, l_refs[n], d) for n, d in enumerate(DILATIONS)]
        m = functools.reduce(jnp.maximum, ls_)
        es = [jnp.exp(l - m) for l in ls_]
        tot = functools.reduce(jnp.add, es)
        attn = functools.reduce(jnp.add, [e * _from_residues(scr, o_refs[n], d)
                                          for n, (e, d) in enumerate(zip(es, DILATIONS))]) / tot
        attn_ref[...] = attn
        lse = m + jnp.log(tot)
        for n, d in enumerate(DILATIONS):
            _to_residues(scr, lse, lse_refs[n], d)
        ssm = _gelu(y_ref[...]) * _sigmoid(z_ref[...] + b_ref[...])
        mixed_ref[:, :AW] = _rms(attn, ga_ref[...]).astype(BF16)
        mixed_ref[:, AW:] = _rms(ssm, gs_ref[...]).astype(BF16)

    res_in = [_residue_spec(tr, d, AW) for d in DILATIONS]
    res = pl.pallas_call(
        body, name="mix_fwd",
        out_shape=([jax.ShapeDtypeStruct((S, AW), F32)] + [_residue_shape(S, d, AW, F32) for d in DILATIONS]
                   + [jax.ShapeDtypeStruct((S, AW + SW), BF16)]),
        grid=(S // tr,),
        in_specs=res_in + res_in + [_rows(tr, SW), _rows(tr, SW), _vec(SW), _vec(AW), _vec(SW)],
        out_specs=[_rows(tr, AW)] + res_in + [_rows(tr, AW + SW)],
        scratch_shapes=[_residue_scratch(tr, AW)],
        compiler_params=_cparams("parallel"))(*os, *ls, y1, z, b_glu, g_attn, g_ssm)
    return res[0], res[1:1 + nd], res[1 + nd]


def _final(h2, gl, e, g_post, target, *, tr=128):
    S, D = h2.shape
    tr = min(tr, S)

    def body(h_ref, gl_ref, e_ref, g_ref, t_ref, dh_ref, dgl_ref, de_ref, loss_ref, dg_ref):
        i = pl.program_id(0)
        gate = _sigmoid(gl_ref[...])
        e_ = e_ref[...]
        ge = gate * e_
        g = g_ref[...]
        diff = h_ref[...] + _rms(ge, g) - t_ref[...]
        dh = diff * (1.0 / D)
        dh_ref[...] = dh
        dge, dgrow = _rms_bwd(dh, ge, g)
        dgl_ref[...] = (dge * e_ * gate * (1.0 - gate)).astype(BF16)
        de_ref[...] = (dge * gate).astype(BF16)
        part = _colsum(0.5 * jnp.mean(diff * diff, axis=-1, keepdims=True))

        @pl.when(_first(i))
        def _():
            loss_ref[...] = jnp.zeros_like(loss_ref)
            dg_ref[...] = jnp.zeros_like(dg_ref)

        loss_ref[...] += part + jnp.zeros((1, LANES), F32)
        dg_ref[...] += _colsum(dgrow)

    return pl.pallas_call(
        body, name="final_fwd_bwd",
        out_shape=(jax.ShapeDtypeStruct((S, D), F32), jax.ShapeDtypeStruct((S, D), BF16),
                   jax.ShapeDtypeStruct((S, D), BF16), jax.ShapeDtypeStruct((1, LANES), F32),
                   jax.ShapeDtypeStruct((1, D), F32)),
        grid=(S // tr,),
        in_specs=[_rows(tr, D), _rows(tr, D), _rows(tr, D), _vec(D), _rows(tr, D)],
        out_specs=(_rows(tr, D), _rows(tr, D), _rows(tr, D), _vec(LANES), _vec(D)),
        compiler_params=_cparams("arbitrary"))(h2, gl, e, g_post, target)


def _bwd_res_norm(dh_out, dhn, h, g_next, y, g_post, *, name, tr=128):
    S, D = h.shape
    tr = min(tr, S)

    def body(dho_ref, dhn_ref, h_ref, gn_ref, y_ref, gp_ref, dh_ref, dy_ref, dgn_ref, dgp_ref):
        i = pl.program_id(0)
        dx, dgn_rows = _rms_bwd(dhn_ref[...], h_ref[...], gn_ref[...])
        dh = dho_ref[...] + dx
        dh_ref[...] = dh
        dy, dgp_rows = _rms_bwd(dh, y_ref[...], gp_ref[...])
        dy_ref[...] = dy.astype(BF16)

        @pl.when(_first(i))
        def _():
            dgn_ref[...] = jnp.zeros_like(dgn_ref)
            dgp_ref[...] = jnp.zeros_like(dgp_ref)

        dgn_ref[...] += _colsum(dgn_rows)
        dgp_ref[...] += _colsum(dgp_rows)

    return pl.pallas_call(
        body, name=name,
        out_shape=(jax.ShapeDtypeStruct((S, D), F32), jax.ShapeDtypeStruct((S, D), BF16),
                   jax.ShapeDtypeStruct((1, D), F32), jax.ShapeDtypeStruct((1, D), F32)),
        grid=(S // tr,),
        in_specs=[_rows(tr, D), _rows(tr, D), _rows(tr, D), _vec(D), _rows(tr, D), _vec(D)],
        out_specs=(_rows(tr, D), _rows(tr, D), _vec(D), _vec(D)),
        compiler_params=_cparams("arbitrary"))(dh_out, dhn, h, g_next, y, g_post)


def _bwd_first(dh1, dhn1, x, g1, *, tr=256):
    S, D = x.shape
    tr = min(tr, S)

    def body(dh_ref, dhn_ref, x_ref, g_ref, dx_ref, dg_ref):
        i = pl.program_id(0)
        dx, dg_rows = _rms_bwd(dhn_ref[...], x_ref[...], g_ref[...])
        dx_ref[...] = dh_ref[...] + dx

        @pl.when(_first(i))
        def _():
            dg_ref[...] = jnp.zeros_like(dg_ref)

        dg_ref[...] += _colsum(dg_rows)

    return pl.pallas_call(
        body, name="bwd_first",
        out_shape=(jax.ShapeDtypeStruct((S, D), F32), jax.ShapeDtypeStruct((1, D), F32)),
        grid=(S // tr,), in_specs=[_rows(tr, D), _rows(tr, D), _rows(tr, D), _vec(D)],
        out_specs=(_rows(tr, D), _vec(D)), compiler_params=_cparams("arbitrary"))(dh1, dhn1, x, g1)


def _mix_bwd(dmixed, attn, y1, z, b_glu, g_attn, g_ssm, *, tr=256):
    S, AW = attn.shape
    SW = y1.shape[1]
    tr = min(tr, S)
    heads = AW // HEAD_DIM
    nd = len(DILATIONS)

    def body(*refs):
        dm_ref, a_ref, y_ref, z_ref, b_ref, ga_ref, gs_ref = refs[:7]
        da_refs, dd_refs = refs[7:7 + nd], refs[7 + nd:7 + 2 * nd]
        dz_ref, dy2_ref, dga_ref, dgs_ref, db_ref, scr, dd_scr = refs[7 + 2 * nd:]
        i = pl.program_id(0)
        attn_ = a_ref[...]
        dattn, dga_rows = _rms_bwd(dm_ref[:, :AW], attn_, ga_ref[...])
        prod = dattn * attn_
        for h in range(heads):
            sl = slice(h * HEAD_DIM, (h + 1) * HEAD_DIM)
            dd_scr[:, sl] = jnp.broadcast_to(jnp.sum(prod[:, sl], axis=-1, keepdims=True), (tr, HEAD_DIM))
        for n, d in enumerate(DILATIONS):
            _to_residues(scr, dattn, da_refs[n], d)
            _to_residues(scr, dd_scr[...], dd_refs[n], d)
        y2 = _gelu(y_ref[...])
        gate = _sigmoid(z_ref[...] + b_ref[...])
        dssm, dgs_rows = _rms_bwd(dm_ref[:, AW:], y2 * gate, gs_ref[...])
        dz = dssm * y2 * gate * (1.0 - gate)
        dz_ref[...] = dz.astype(BF16)
        dy2_ref[...] = dssm * gate

        @pl.when(_first(i))
        def _():
            dga_ref[...] = jnp.zeros_like(dga_ref)
            dgs_ref[...] = jnp.zeros_like(dgs_ref)
            db_ref[...] = jnp.zeros_like(db_ref)

        dga_ref[...] += _colsum(dga_rows)
        dgs_ref[...] += _colsum(dgs_rows)
        db_ref[...] += _colsum(dz)

    res_out = [_residue_spec(tr, d, AW) for d in DILATIONS]
    res = pl.pallas_call(
        body, name="mix_bwd",
        out_shape=([_residue_shape(S, d, AW, BF16) for d in DILATIONS]
                   + [_residue_shape(S, d, AW, F32) for d in DILATIONS]
                   + [jax.ShapeDtypeStruct((S, SW), BF16), jax.ShapeDtypeStruct((S, SW), F32),
                      jax.ShapeDtypeStruct((1, AW), F32), jax.ShapeDtypeStruct((1, SW), F32),
                      jax.ShapeDtypeStruct((1, SW), F32)]),
        grid=(S // tr,),
        in_specs=[_rows(tr, AW + SW), _rows(tr, AW), _rows(tr, SW), _rows(tr, SW), _vec(SW), _vec(AW), _vec(SW)],
        out_specs=res_out + res_out + [_rows(tr, SW), _rows(tr, SW), _vec(AW), _vec(SW), _vec(SW)],
        scratch_shapes=[_residue_scratch(tr, AW), pltpu.VMEM((tr, AW), F32)],
        compiler_params=_cparams("arbitrary"))(dmixed, attn, y1, z, b_glu, g_attn, g_ssm)
    return (res[:nd], res[nd:2 * nd]) + tuple(res[2 * nd:])


def _attn_mask2(i):
    row = lax.broadcasted_iota(jnp.int32, (BLK, 2 * BLK), 0)
    col = lax.broadcasted_iota(jnp.int32, (BLK, 2 * BLK), 1)
    return jnp.logical_and(col >= row, jnp.logical_and(col <= row + BLK, jnp.logical_or(col >= BLK, i > 0)))


_NT = (((1,), (1,)), ((), ()))
_TN = (((0,), (0,)), ((), ()))


def _attn_in_specs(width, block_of):
    def at(part, prev):
        def index(r, i):
            blk = block_of(i)
            return (part, jnp.maximum(blk - 1, 0) if prev else blk, r)
        return pl.BlockSpec((None, BLK, width), index)
    return [at(0, False), at(1, False), at(1, True), at(2, False), at(2, True)]


def _proj_qkv(hn, w_in_f, *, tm=1024):
    S, D = hn.shape
    AW = w_in_f.shape[2]
    tm = min(tm, S)

    def body(a_ref, b_ref, *rest):
        o_refs, scr = rest[:-1], rest[-1]
        prod = jnp.dot(a_ref[...], b_ref[...], preferred_element_type=F32)
        _fill_strips(scr, prod)
        for o_ref, d in zip(o_refs, DILATIONS):
            if d == 1:
                o_ref[...] = prod.astype(BF16)
            else:
                _strips_to_residues(scr, o_ref, d)

    return pl.pallas_call(
        body, name="proj_qkv",
        out_shape=[jax.ShapeDtypeStruct((3, S // d, d * AW), BF16) for d in DILATIONS], grid=(S // tm, 3),
        in_specs=[pl.BlockSpec((tm, D), lambda i, j: (i, 0)), pl.BlockSpec((None, D, AW), lambda i, j: (j, 0, 0))],
        out_specs=[pl.BlockSpec((None, tm // d, d * AW), lambda i, j: (j, i, 0)) for d in DILATIONS],
        scratch_shapes=[_residue_scratch(tm, AW)],
        compiler_params=_cparams("parallel", "parallel"))(hn, w_in_f)


def _attn_fwd(qkv, d, heads):
    M = qkv.shape[1]
    nb = M // BLK
    width = heads * HEAD_DIM
    scale = 1.0 / math.sqrt(HEAD_DIM)

    def body(q_ref, kc_ref, kp_ref, vc_ref, vp_ref, o_ref, l_ref):
        mask = _attn_mask2(pl.program_id(1))
        ones = jnp.ones((2 * BLK, HEAD_DIM), BF16)

        def scores(h):
            sl = slice(h * HEAD_DIM, (h + 1) * HEAD_DIM)
            k2 = jnp.concatenate([kp_ref[:, sl], kc_ref[:, sl]], axis=0)
            return lax.dot_general(q_ref[:, sl], k2, _NT, preferred_element_type=F32)

        ahead = [scores(h) for h in range(min(ATTN_LOOKAHEAD, heads))]
        for h in range(heads):
            sl = slice(h * HEAD_DIM, (h + 1) * HEAD_DIM)
            s = jnp.where(mask, ahead.pop(0) * scale, NEG_INF)
            if h + ATTN_LOOKAHEAD < heads:
                ahead.append(scores(h + ATTN_LOOKAHEAD))
            v2 = jnp.concatenate([vp_ref[:, sl], vc_ref[:, sl]], axis=0)
            m = jnp.max(jnp.maximum(s[:, :BLK], s[:, BLK:]), axis=-1, keepdims=True)
            p = jnp.exp(s - m).astype(BF16)
            tot = jnp.dot(p, ones, preferred_element_type=F32)
            o_ref[:, sl] = jnp.dot(p, v2, preferred_element_type=F32) / tot
            l_ref[:, sl] = m + jnp.log(tot)

    out_spec = pl.BlockSpec((BLK, width), lambda r, i: (i, r))
    shape = jax.ShapeDtypeStruct((M, d * width), F32)
    return pl.pallas_call(
        body, name=f"attn_fwd_d{d}", out_shape=(shape, shape), grid=(d, nb),
        in_specs=_attn_in_specs(width, lambda i: i), out_specs=(out_spec, out_spec),
        compiler_params=_cparams("parallel", "parallel"))(qkv, qkv, qkv, qkv, qkv)


def _attn_bwd(qkv, dattn, lse, dd, d, heads, after):
    M = qkv.shape[1]
    nb = M // BLK
    width = heads * HEAD_DIM
    scale = 1.0 / math.sqrt(HEAD_DIM)

    def block_of(i):
        return nb - 1 - i

    def body(q_ref, kc_ref, kp_ref, vc_ref, vp_ref, da_ref, l_ref, dd_ref, after_ref,
             dq_ref, dk_ref, dv_ref, dk_carry, dv_carry):
        @pl.when(pl.program_id(1) == 0)
        def _():
            dk_carry[...] = jnp.zeros_like(dk_carry)
            dv_carry[...] = jnp.zeros_like(dv_carry)

        mask = _attn_mask2(block_of(pl.program_id(1)))

        def products(h):
            sl = slice(h * HEAD_DIM, (h + 1) * HEAD_DIM)
            k2 = jnp.concatenate([kp_ref[:, sl], kc_ref[:, sl]], axis=0)
            v2 = jnp.concatenate([vp_ref[:, sl], vc_ref[:, sl]], axis=0)
            return (lax.dot_general(q_ref[:, sl], k2, _NT, preferred_element_type=F32),
                    lax.dot_general(da_ref[:, sl], v2, _NT, preferred_element_type=F32), k2)

        ahead = [products(h) for h in range(min(ATTN_LOOKAHEAD, heads))]
        for h in range(heads):
            sl = slice(h * HEAD_DIM, (h + 1) * HEAD_DIM)
            qk, dp, k2 = ahead.pop(0)
            if h + ATTN_LOOKAHEAD < heads:
                ahead.append(products(h + ATTN_LOOKAHEAD))
            q, da = q_ref[:, sl], da_ref[:, sl]
            lse_ = jnp.concatenate([l_ref[:, sl], l_ref[:, sl]], axis=1)
            dd_ = jnp.concatenate([dd_ref[:, sl], dd_ref[:, sl]], axis=1)
            p = jnp.where(mask, jnp.exp(jnp.where(mask, qk * scale, NEG_INF) - lse_), 0.0)
            ds = (p * (dp - dd_) * scale).astype(BF16)
            dq_ref[:, sl] = jnp.dot(ds, k2, preferred_element_type=F32).astype(BF16)
            dk2 = lax.dot_general(ds, q, _TN, preferred_element_type=F32)
            dv2 = lax.dot_general(p.astype(BF16), da, _TN, preferred_element_type=F32)
            dk_ref[:, sl] = (dk2[BLK:] + dk_carry[:, sl]).astype(BF16)
            dv_ref[:, sl] = (dv2[BLK:] + dv_carry[:, sl]).astype(BF16)
            dk_carry[:, sl] = dk2[:BLK]
            dv_carry[:, sl] = dv2[:BLK]

    blk = pl.BlockSpec((BLK, width), lambda r, i: (block_of(i), r))
    shape = jax.ShapeDtypeStruct((M, d * width), BF16)
    return pl.pallas_call(
        body, name=f"attn_bwd_d{d}", out_shape=(shape,) * 3, grid=(d, nb),
        in_specs=(_attn_in_specs(width, block_of) + [blk, blk, blk]
                  + [pl.BlockSpec(after.shape, lambda r, i: (0, 0))]), out_specs=(blk,) * 3,
        scratch_shapes=[pltpu.VMEM((BLK, width), F32), pltpu.VMEM((BLK, width), F32)],
        compiler_params=_cparams("arbitrary", "arbitrary"))(qkv, qkv, qkv, qkv, qkv, dattn, lse, dd, after)


def _dproj_join(dqs, dks, dvs, du, *, tr=256):
    S, SW = du.shape
    AW = dqs[0].shape[1]
    tr = min(tr, S)
    nd = len(DILATIONS)

    def body(*refs):
        du_ref, out_ref, scr = refs[3 * nd:]
        for part in range(3):
            total = functools.reduce(jnp.add, [_from_residues(scr, refs[part * nd + n], d)
                                               for n, d in enumerate(DILATIONS)])
            out_ref[:, part * AW:(part + 1) * AW] = total.astype(BF16)
        out_ref[:, 3 * AW:] = du_ref[...].astype(BF16)

    return pl.pallas_call(
        body, name="dproj_join", out_shape=jax.ShapeDtypeStruct((S, 3 * AW + SW), BF16), grid=(S // tr,),
        in_specs=[_residue_spec(tr, d, AW) for d in DILATIONS] * 3 + [_rows(tr, SW)],
        out_specs=_rows(tr, 3 * AW + SW), scratch_shapes=[_residue_scratch(tr, AW)],
        compiler_params=_cparams("parallel"))(*dqs, *dks, *dvs, du)


def _ssm_disc(lr, li, ldt):
    dt = jnp.exp(ldt)
    mag = jnp.exp(lr * dt)
    ar = mag * jnp.cos(li * dt)
    ai = mag * jnp.sin(li * dt)
    nr = ar - 1.0
    den = lr * lr + li * li
    return ar, ai, (nr * lr + ai * li) / den, (ai * lr - nr * li) / den


def _ssm_tile_powers(lr, li, ldt, reverse):
    t = lax.broadcasted_iota(jnp.int32, (TILE, 1), 0)
    n = (TILE - t if reverse else t + 1).astype(F32)
    dt = jnp.exp(ldt)
    mag = jnp.exp(n * (lr * dt))
    ang = n * (li * dt)
    return mag * jnp.cos(ang), mag * jnp.sin(ang) * (-1.0 if reverse else 1.0)


def _cmul(ar, ai, br, bi):
    return ar * br - ai * bi, ar * bi + ai * br


LOG_STEPS = 3


def _ssm_step_tables(ar, ai, reverse):
    sub = lax.broadcasted_iota(jnp.int32, (TILE, ar.shape[-1]), 0)
    tables = []
    for k in range(LOG_STEPS):
        keep = sub < TILE - (1 << k) if reverse else sub >= (1 << k)
        tables.append((jnp.where(keep, ar, 0.0), jnp.where(keep, ai, 0.0)))
        ar, ai = _cmul(ar, ai, ar, ai)
    return tables


def _scan(xr, xi, steps, pr, pi, cr, ci, reverse):
    T, lanes = xr.shape
    n = T // TILE
    xr, xi = xr.reshape(n, TILE, lanes), xi.reshape(n, TILE, lanes)
    for k, (mr, mi) in enumerate(steps):
        shift = TILE - (1 << k) if reverse else 1 << k
        qr, qi = _cmul(mr, mi, pltpu.roll(xr, shift, 1), pltpu.roll(xi, shift, 1))
        xr, xi = xr + qr, xi + qi
    out_r, out_i = [None] * n, [None] * n
    edge = 0 if reverse else TILE - 1
    for j in (reversed(range(n)) if reverse else range(n)):
        er, ei = _cmul(pr, pi, cr, ci)
        sr, si = xr[j] + er, xi[j] + ei
        out_r[j], out_i[j] = sr, si
        cr, ci = sr[edge:edge + 1], si[edge:edge + 1]
    return jnp.concatenate(out_r, axis=0), jnp.concatenate(out_i, axis=0), cr, ci


def _ssm_specs(T, nch, rev):
    def t_of(c):
        return nch - 1 - c if rev else c
    tok = pl.BlockSpec((T, LANES), lambda j, c: (t_of(c), j))
    par = pl.BlockSpec((None, 1, STATE_LANES), lambda j, c: (j, 0, 0))
    bmat = pl.BlockSpec((None, LANES, STATE_LANES), lambda j, c: (j, 0, 0))
    cmat = pl.BlockSpec((None, STATE_LANES, LANES), lambda j, c: (j, 0, 0))
    dvec = pl.BlockSpec((1, LANES), lambda j, c: (0, j))
    return tok, par, bmat, cmat, dvec


def _ssm_fwd(u, lr_e, li_e, ldt_e, bre_e, bim_e, cre_e, cim_e, d_skip):
    S, SW = u.shape
    T = min(SSM_CHUNK, S)
    nch, nbk = S // T, SW // LANES
    tok, par, bmat, cmat, dvec = _ssm_specs(T, nch, False)
    state_spec = pl.BlockSpec((T, STATE_LANES), lambda j, c: (c, j))
    carry_spec = pl.BlockSpec((None, 1, STATE_LANES), lambda j, c: (c, 0, j))

    def body(u_ref, lr_ref, li_ref, ldt_ref, bre_ref, bim_ref, cre_ref, cim_ref, d_ref,
             y_ref, y2_ref, sr_ref, si_ref, er_ref, ei_ref, bbr, bbi, steps, pw, carry):
        c = pl.program_id(1)

        @pl.when(c == 0)
        def _():
            lr, li, ldt = lr_ref[...], li_ref[...], ldt_ref[...]
            ar, ai, kr, ki = _ssm_disc(lr, li, ldt)
            for k, (mr, mi) in enumerate(_ssm_step_tables(ar, ai, False)):
                steps[0, k], steps[1, k] = mr, mi
            bbr[...] = (kr * bre_ref[...] - ki * bim_ref[...]).astype(BF16)
            bbi[...] = (kr * bim_ref[...] + ki * bre_ref[...]).astype(BF16)
            pw[0], pw[1] = _ssm_tile_powers(lr, li, ldt, False)
            carry[...] = jnp.zeros_like(carry)

        u_ = u_ref[...]
        ub = u_.astype(BF16)
        sr, si, cr, ci = _scan(jnp.dot(ub, bbr[...], preferred_element_type=F32),
                               jnp.dot(ub, bbi[...], preferred_element_type=F32),
                               [(steps[0, k], steps[1, k]) for k in range(LOG_STEPS)],
                               pw[0], pw[1], carry[0], carry[1], False)
        carry[0], carry[1] = cr, ci
        er_ref[...], ei_ref[...] = cr, ci
        sr_ref[...], si_ref[...] = sr, si
        y0 = (jnp.dot(sr.astype(BF16), cre_ref[...].astype(BF16), preferred_element_type=F32)
              - jnp.dot(si.astype(BF16), cim_ref[...].astype(BF16), preferred_element_type=F32))
        y1 = y0 + d_ref[...] * u_
        y_ref[...] = y1
        y2_ref[...] = _gelu(y1).astype(BF16)

    states = jax.ShapeDtypeStruct((S, nbk * STATE_LANES), F32)
    ends = jax.ShapeDtypeStruct((nch, 1, nbk * STATE_LANES), F32)
    return pl.pallas_call(
        body, name="ssm_fwd",
        out_shape=(jax.ShapeDtypeStruct((S, SW), F32), jax.ShapeDtypeStruct((S, SW), BF16), states, states, ends, ends),
        grid=(nbk, nch), in_specs=[tok, par, par, par, bmat, bmat, cmat, cmat, dvec],
        out_specs=(tok, tok, state_spec, state_spec, carry_spec, carry_spec),
        scratch_shapes=[pltpu.VMEM((LANES, STATE_LANES), BF16), pltpu.VMEM((LANES, STATE_LANES), BF16),
                        pltpu.VMEM((2, LOG_STEPS, TILE, STATE_LANES), F32), pltpu.VMEM((2, TILE, STATE_LANES), F32),
                        pltpu.VMEM((2, 1, STATE_LANES), F32)],
        compiler_params=_cparams("arbitrary", "arbitrary"),
    )(u, lr_e, li_e, ldt_e, bre_e, bim_e, cre_e, cim_e, d_skip)


def _ssm_bwd(u, y1, dy2a, dy2b, st_r, st_i, ends_r, ends_i, lr_e, li_e, ldt_e, bre_e, bim_e, cre_e, cim_e, d_skip):
    S, SW = u.shape
    T = min(SSM_CHUNK, S)
    nch, nbk = S // T, SW // LANES
    tok, par, bmat, cmat, dvec = _ssm_specs(T, nch, True)
    state_spec = pl.BlockSpec((T, STATE_LANES), lambda j, c: (nch - 1 - c, j))
    prev_spec = pl.BlockSpec((None, 1, STATE_LANES), lambda j, c: (jnp.maximum(nch - 2 - c, 0), 0, j))
    acc8 = pl.BlockSpec((None, 8, STATE_LANES), lambda j, c: (j, 0, 0))
    dd8 = pl.BlockSpec((None, 8, LANES), lambda j, c: (j, 0, 0))

    def body(u_ref, y_ref, da_ref, db_ref, sr_ref, si_ref, pr_ref, pi_ref, lr_ref, li_ref, ldt_ref,
             bre_ref, bim_ref, cre_ref, cim_ref, d_ref,
             du_ref, dar_ref, dai_ref, dcr_ref, dci_ref, dbr_ref, dbi_ref, ddk_ref,
             bbr, bbi, steps, pw, carry):
        c = pl.program_id(1)

        @pl.when(c == 0)
        def _():
            lr, li, ldt = lr_ref[...], li_ref[...], ldt_ref[...]
            ar, ai, kr, ki = _ssm_disc(lr, li, ldt)
            for k, (mr, mi) in enumerate(_ssm_step_tables(ar, -ai, True)):
                steps[0, k], steps[1, k] = mr, mi
            bbr[...] = (kr * bre_ref[...] - ki * bim_ref[...]).astype(BF16)
            bbi[...] = (kr * bim_ref[...] + ki * bre_ref[...]).astype(BF16)
            pw[0], pw[1] = _ssm_tile_powers(lr, li, ldt, True)
            carry[...] = jnp.zeros_like(carry)
            for ref in (dar_ref, dai_ref, dcr_ref, dci_ref, dbr_ref, dbi_ref, ddk_ref):
                ref[...] = jnp.zeros_like(ref)

        u_ = u_ref[...]
        ub = u_.astype(BF16)
        dy1 = (da_ref[...] + db_ref[...]) * _gelu_grad(y_ref[...])
        dyb = dy1.astype(BF16)

        sr, si = sr_ref[...], si_ref[...]
        has_prev = c < nch - 1
        s0r = jnp.where(has_prev, pr_ref[...], 0.0)
        s0i = jnp.where(has_prev, pi_ref[...], 0.0)

        cre_b, cim_b = cre_ref[...].astype(BF16), cim_ref[...].astype(BF16)
        gr, gi, cr, ci = _scan(lax.dot_general(dyb, cre_b, _NT, preferred_element_type=F32),
                               -lax.dot_general(dyb, cim_b, _NT, preferred_element_type=F32),
                               [(steps[0, k], steps[1, k]) for k in range(LOG_STEPS)],
                               pw[0], pw[1], carry[0], carry[1], True)
        carry[0], carry[1] = cr, ci

        row = lax.broadcasted_iota(jnp.int32, (T, STATE_LANES), 0)
        spr = jnp.where(row == 0, s0r, pltpu.roll(sr, 1, 0))
        spi = jnp.where(row == 0, s0i, pltpu.roll(si, 1, 0))

        def fold(a):
            return jnp.sum(a.reshape(T // 8, 8, a.shape[-1]), axis=0)

        dar_ref[...] += fold(gr * spr + gi * spi)
        dai_ref[...] += fold(gi * spr - gr * spi)
        srb, sib, grb, gib = sr.astype(BF16), si.astype(BF16), gr.astype(BF16), gi.astype(BF16)
        dcr_ref[...] += lax.dot_general(srb, dyb, _TN, preferred_element_type=F32)
        dci_ref[...] -= lax.dot_general(sib, dyb, _TN, preferred_element_type=F32)
        dbr_ref[...] += lax.dot_general(ub, grb, _TN, preferred_element_type=F32)
        dbi_ref[...] += lax.dot_general(ub, gib, _TN, preferred_element_type=F32)
        du_ref[...] = (lax.dot_general(grb, bbr[...], _NT, preferred_element_type=F32)
                       + lax.dot_general(gib, bbi[...], _NT, preferred_element_type=F32)
                       + dy1 * d_ref[...])
        ddk_ref[...] += fold(dy1 * u_)

    return pl.pallas_call(
        body, name="ssm_bwd",
        out_shape=(jax.ShapeDtypeStruct((S, SW), F32),
                   jax.ShapeDtypeStruct((nbk, 8, STATE_LANES), F32), jax.ShapeDtypeStruct((nbk, 8, STATE_LANES), F32),
                   jax.ShapeDtypeStruct((nbk, STATE_LANES, LANES), F32), jax.ShapeDtypeStruct((nbk, STATE_LANES, LANES), F32),
                   jax.ShapeDtypeStruct((nbk, LANES, STATE_LANES), F32), jax.ShapeDtypeStruct((nbk, LANES, STATE_LANES), F32),
                   jax.ShapeDtypeStruct((nbk, 8, LANES), F32)),
        grid=(nbk, nch),
        in_specs=[tok, tok, tok, tok, state_spec, state_spec, prev_spec, prev_spec, par, par, par,
                  bmat, bmat, cmat, cmat, dvec],
        out_specs=(tok, acc8, acc8, cmat, cmat, bmat, bmat, dd8),
        scratch_shapes=[pltpu.VMEM((LANES, STATE_LANES), BF16), pltpu.VMEM((LANES, STATE_LANES), BF16),
                        pltpu.VMEM((2, LOG_STEPS, TILE, STATE_LANES), F32), pltpu.VMEM((2, TILE, STATE_LANES), F32),
                        pltpu.VMEM((2, 1, STATE_LANES), F32)],
        compiler_params=_cparams("arbitrary", "arbitrary"),
    )(u, y1, dy2a, dy2b, st_r, st_i, ends_r, ends_i, lr_e, li_e, ldt_e, bre_e, bim_e, cre_e, cim_e, d_skip)


def _ssm_param_bwd(dar8, dai8, dbr_e, dbi_e, lr_e, li_e, ldt_e, bre_e, bim_e):
    nbk = lr_e.shape[0]
    par = pl.BlockSpec((None, 1, STATE_LANES), lambda j: (j, 0, 0))
    acc8 = pl.BlockSpec((None, 8, STATE_LANES), lambda j: (j, 0, 0))
    bmat = pl.BlockSpec((None, LANES, STATE_LANES), lambda j: (j, 0, 0))

    def body(dar_ref, dai_ref, dbr_ref, dbi_ref, lr_ref, li_ref, ldt_ref, bre_ref, bim_ref,
             dlr_ref, dli_ref, dldt_ref, dbre_ref, dbim_ref):
        lr, li, ldt = lr_ref[...], li_ref[...], ldt_ref[...]
        (ar, ai, kr, ki), vjp = jax.vjp(_ssm_disc, lr, li, ldt)
        dbr, dbi, bre, bim = dbr_ref[...], dbi_ref[...], bre_ref[...], bim_ref[...]
        dbre_ref[...] = kr * dbr + ki * dbi
        dbim_ref[...] = kr * dbi - ki * dbr
        dkr = _colsum(dbr * bre + dbi * bim)
        dki = _colsum(dbi * bre - dbr * bim)
        dlr, dli, dldt = vjp((_colsum(dar_ref[...]), _colsum(dai_ref[...]), dkr, dki))
        dlr_ref[...] = dlr
        dli_ref[...] = dli
        tot = jnp.broadcast_to(dldt, (8, STATE_LANES))
        sh = 1
        while sh < SSM_P:
            tot = tot + pltpu.roll(tot, STATE_LANES - sh, 1)
            sh *= 2
        dldt_ref[...] = tot[:1]

    vec = jax.ShapeDtypeStruct((nbk, 1, STATE_LANES), F32)
    mat = jax.ShapeDtypeStruct((nbk, LANES, STATE_LANES), F32)
    return pl.pallas_call(
        body, name="ssm_param_bwd", out_shape=(vec, vec, vec, mat, mat), grid=(nbk,),
        in_specs=[acc8, acc8, bmat, bmat, par, par, par, bmat, bmat],
        out_specs=(par, par, par, bmat, bmat), compiler_params=_cparams("parallel"),
    )(dar8, dai8, dbr_e, dbi_e, lr_e, li_e, ldt_e, bre_e, bim_e)


def _expand_b(b):
    G = b.shape[0]
    bt = b.transpose(0, 2, 1).reshape(G // GROUPS_PER_BLOCK, GROUPS_PER_BLOCK, SSM_C, SSM_P)
    eye = jnp.eye(GROUPS_PER_BLOCK, dtype=b.dtype)
    return (bt[:, :, :, None, :] * eye[None, :, None, :, None]).reshape(G // GROUPS_PER_BLOCK, LANES, STATE_LANES)


def _collapse_b(be):
    nbk = be.shape[0]
    eye = jnp.eye(GROUPS_PER_BLOCK, dtype=be.dtype)
    d5 = be.reshape(nbk, GROUPS_PER_BLOCK, SSM_C, GROUPS_PER_BLOCK, SSM_P)
    d4 = (d5 * eye[None, :, None, :, None]).sum(axis=3)
    return d4.transpose(0, 1, 3, 2).reshape(nbk * GROUPS_PER_BLOCK, SSM_P, SSM_C)


def _expand_c(cm):
    G = cm.shape[0]
    ct = cm.transpose(0, 2, 1).reshape(G // GROUPS_PER_BLOCK, GROUPS_PER_BLOCK, SSM_P, SSM_C)
    eye = jnp.eye(GROUPS_PER_BLOCK, dtype=cm.dtype)
    return (ct[:, :, :, None, :] * eye[None, :, None, :, None]).reshape(G // GROUPS_PER_BLOCK, STATE_LANES, LANES)


def _collapse_c(ce):
    nbk = ce.shape[0]
    eye = jnp.eye(GROUPS_PER_BLOCK, dtype=ce.dtype)
    d5 = ce.reshape(nbk, GROUPS_PER_BLOCK, SSM_P, GROUPS_PER_BLOCK, SSM_C)
    d4 = (d5 * eye[None, :, None, :, None]).sum(axis=3)
    return d4.transpose(0, 1, 3, 2).reshape(nbk * GROUPS_PER_BLOCK, SSM_C, SSM_P)


def _place():
    x, y, c = lax.axis_index("x"), lax.axis_index("y"), lax.axis_index("c")
    return x, y, c


def _other_chips(x, y):
    return [(1 - x, y), (x, 1 - y), (1 - x, 1 - y)]


_ANY = pl.BlockSpec(memory_space=pl.ANY)


_HBM = pl.BlockSpec(memory_space=pltpu.HBM)
_SEM = pl.BlockSpec(memory_space=pltpu.SEMAPHORE)
_EFFECT = pltpu.SideEffectType.DATAFLOW_SIDE_EFFECTING
_TOKEN = jax.ShapeDtypeStruct((8, LANES), F32)


def _hbm(a):
    return pltpu.with_memory_space_constraint(a, pltpu.HBM)


def _place_own(src, *, gather, name, tr=512):
    R, C = src.shape[-2:]
    tr = min(tr, R)
    x, y, _ = _place()
    me = (2 * x + y).astype(jnp.int32).reshape(1)

    def body(me_ref, s_ref, o_ref):
        o_ref[...] = s_ref[...].astype(BF16)

    own = pl.BlockSpec((None, tr, C), lambda i, me_ref: (me_ref[0], i, 0))
    grid_spec = pltpu.PrefetchScalarGridSpec(
        num_scalar_prefetch=1, grid=(R // tr,),
        in_specs=[pl.BlockSpec((tr, C), lambda i, me_ref: (i, 0)) if gather else own], out_specs=own)
    return pl.pallas_call(
        body, name=name, grid_spec=grid_spec, out_shape=jax.ShapeDtypeStruct((N_CHIPS, R, C), BF16),
        compiler_params=_cparams("parallel"))(me, src)


def _exchange_copy(src_slot, land_slot, send, recv, k, j, peer, c):
    return pltpu.make_async_remote_copy(
        src_ref=src_slot, dst_ref=land_slot, send_sem=send.at[3 * k + j], recv_sem=recv.at[3 * k + j],
        device_id=(peer[0], peer[1], c), device_id_type=MESH)


def _exchange_start(lands, srcs, groups, *, name):
    n, ng = len(lands), len(groups)
    bufs = list(lands) + list(srcs)
    nb = len(bufs)

    def body(*refs):
        lnd, src, sems = refs[:n], refs[n:nb], refs[nb:nb + 2 * ng]
        token = refs[2 * nb + 2 * ng]
        x, y, c = _place()
        me = 2 * x + y
        for gi, group in enumerate(groups):
            for k, w in enumerate(group):
                for j, peer in enumerate(_other_chips(x, y)):
                    if src:
                        sent, dst = src[w].at[2 * peer[0] + peer[1]], lnd[w].at[me]
                    else:
                        sent = dst = lnd[w].at[me, c]
                    _exchange_copy(sent, dst, sems[2 * gi], sems[2 * gi + 1], k, j, peer, c).start()
        token[...] = jnp.zeros_like(token)

    sem_shapes = [pltpu.SemaphoreType.DMA((3 * len(g),)) for g in groups for _ in range(2)]
    res = pl.pallas_call(
        body, name=name,
        out_shape=sem_shapes + [pltpu.HBM(a.shape, a.dtype) for a in bufs] + [_TOKEN],
        in_specs=[_HBM] * nb,
        out_specs=[_SEM] * (2 * ng) + [_HBM] * nb + [pl.BlockSpec(memory_space=pltpu.VMEM)],
        input_output_aliases={i: 2 * ng + i for i in range(nb)},
        compiler_params=pltpu.CompilerParams(has_side_effects=_EFFECT),
    )(*[_hbm(a) for a in bufs])
    sems = [(res[2 * gi], res[2 * gi + 1]) for gi in range(ng)]
    return sems, res[2 * ng:2 * ng + n], res[2 * ng + n:2 * ng + nb], res[-1]


def _exchange_wait(lands, srcs, sems, after, *, name):
    n = len(lands)
    bufs = list(lands) + list(srcs)
    nb = len(bufs)
    send_sems, recv_sems = sems

    def body(*refs):
        lnd, src, send, recv = refs[:n], refs[n:nb], refs[nb], refs[nb + 1]
        x, y, c = _place()
        for k in range(n):
            for j, peer in enumerate(_other_chips(x, y)):
                slot = 2 * peer[0] + peer[1]
                if src:
                    copy = _exchange_copy(src[k].at[slot], lnd[k].at[slot], send, recv, k, j, peer, c)
                else:
                    copy = _exchange_copy(lnd[k].at[slot, c], lnd[k].at[slot, c], send, recv, k, j, peer, c)
                copy.wait_send()
                copy.wait_recv()

    res = pl.pallas_call(
        body, name=name, out_shape=[pltpu.HBM(a.shape, a.dtype) for a in bufs],
        in_specs=[_HBM] * nb + [_SEM, _SEM, _ANY], out_specs=[_HBM] * nb,
        input_output_aliases={i: i for i in range(nb)},
        compiler_params=pltpu.CompilerParams(has_side_effects=_EFFECT),
    )(*bufs, send_sems, recv_sems, after)
    return res[:n]


def _pair_fill(lands, *, name):
    n = len(lands)

    def body(*refs):
        ins, outs, send, recv = refs[:n], refs[n:2 * n], refs[2 * n], refs[2 * n + 1]
        x, y, c = _place()
        for w in range(n):
            for j, (px, py) in enumerate(_other_chips(x, y)):
                slot = 2 * px + py
                pltpu.make_async_remote_copy(
                    src_ref=ins[w].at[slot, c], dst_ref=outs[w].at[slot, c], send_sem=send.at[3 * w + j],
                    recv_sem=recv.at[3 * w + j], device_id=(x, y, 1 - c), device_id_type=MESH).start()
        for w in range(n):
            for j, (px, py) in enumerate(_other_chips(x, y)):
                slot = 2 * px + py
                arrival = pltpu.make_async_remote_copy(
                    src_ref=ins[w].at[slot, c], dst_ref=outs[w].at[slot, 1 - c], send_sem=send.at[3 * w + j],
                    recv_sem=recv.at[3 * w + j], device_id=(x, y, 1 - c), device_id_type=MESH)
                arrival.wait_recv()
                arrival.wait_send()

    return pl.pallas_call(
        body, name=name, out_shape=[jax.ShapeDtypeStruct(a.shape, a.dtype) for a in lands],
        in_specs=[_ANY] * n, out_specs=[_ANY] * n, input_output_aliases={i: i for i in range(n)},
        scratch_shapes=[pltpu.SemaphoreType.DMA((3 * n,)), pltpu.SemaphoreType.DMA((3 * n,))],
    )(*lands)


def _pair_copy(src, dst, send, recv, w, j, sibling):
    return pltpu.make_async_remote_copy(
        src_ref=src, dst_ref=dst, send_sem=send.at[3 * w + j], recv_sem=recv.at[3 * w + j],
        device_id=sibling, device_id_type=MESH)


def _pair_start(lands, *, name):
    n = len(lands)

    def body(*refs):
        bufs, send, recv, token = refs[:n], refs[n], refs[n + 1], refs[2 * n + 2]
        x, y, c = _place()
        for w in range(n):
            for j, (px, py) in enumerate(_other_chips(x, y)):
                half = bufs[w].at[2 * px + py, c]
                _pair_copy(half, half, send, recv, w, j, (x, y, 1 - c)).start()
        token[...] = jnp.zeros_like(token)

    res = pl.pallas_call(
        body, name=name,
        out_shape=[pltpu.SemaphoreType.DMA((3 * n,))] * 2 + [pltpu.HBM(a.shape, a.dtype) for a in lands] + [_TOKEN],
        in_specs=[_HBM] * n, out_specs=[_SEM, _SEM] + [_HBM] * n + [pl.BlockSpec(memory_space=pltpu.VMEM)],
        input_output_aliases={i: 2 + i for i in range(n)},
        compiler_params=pltpu.CompilerParams(has_side_effects=_EFFECT),
    )(*[_hbm(a) for a in lands])
    return (res[0], res[1]), res[2:2 + n], res[-1]


def _pair_wait(lands, sems, after, *, name):
    n = len(lands)

    def body(*refs):
        bufs, send, recv = refs[:n], refs[n], refs[n + 1]
        x, y, c = _place()
        for w in range(n):
            for j, (px, py) in enumerate(_other_chips(x, y)):
                slot = 2 * px + py
                copy = _pair_copy(bufs[w].at[slot, c], bufs[w].at[slot, 1 - c], send, recv, w, j, (x, y, 1 - c))
                copy.wait_send()
                copy.wait_recv()

    return pl.pallas_call(
        body, name=name, out_shape=[pltpu.HBM(a.shape, a.dtype) for a in lands],
        in_specs=[_HBM] * n + [_SEM, _SEM, _ANY], out_specs=[_HBM] * n,
        input_output_aliases={i: i for i in range(n)},
        compiler_params=pltpu.CompilerParams(has_side_effects=_EFFECT),
    )(*lands, *sems, after)


def _sum_partials(land, *, name, tr=256):
    _, R, C = land.shape
    tr = min(tr, R)

    def body(l_ref, o_ref):
        acc = l_ref[0].astype(F32)
        for k in range(1, N_CHIPS):
            acc = acc + l_ref[k].astype(F32)
        o_ref[...] = acc

    return pl.pallas_call(
        body, name=name, out_shape=jax.ShapeDtypeStruct((R, C), F32), grid=(R // tr,),
        in_specs=[pl.BlockSpec((N_CHIPS, tr, C), lambda i: (0, i, 0))], out_specs=_rows(tr, C),
        compiler_params=_cparams("parallel"))(land)


def _swap_with_sibling(sums, *, name):
    n = len(sums)

    def body(*refs):
        ins, outs = refs[:n], refs[n:2 * n]
        send_sems, recv_sems = refs[2 * n:]
        x, y, c = _place()
        copies = [pltpu.make_async_remote_copy(
            src_ref=ins[w], dst_ref=outs[w], send_sem=send_sems.at[w], recv_sem=recv_sems.at[w],
            device_id=(x, y, 1 - c), device_id_type=MESH) for w in range(n)]
        for cp in copies:
            cp.start()
        for cp in copies:
            cp.wait_recv()
            cp.wait_send()

    return pl.pallas_call(
        body, name=name,
        out_shape=[jax.ShapeDtypeStruct(s.shape, s.dtype) for s in sums],
        in_specs=[_ANY] * n, out_specs=[_ANY] * n,
        scratch_shapes=[pltpu.SemaphoreType.DMA((n,)), pltpu.SemaphoreType.DMA((n,))],
    )(*sums)


def _swap_start(sums, *, name):
    n = len(sums)
    bufs = list(sums) + [lax.empty(s.shape, s.dtype) for s in sums]

    def body(*refs):
        src, lnd, send, recv, token = refs[:n], refs[n:2 * n], refs[2 * n], refs[2 * n + 1], refs[4 * n + 2]
        x, y, c = _place()
        for w in range(n):
            pltpu.make_async_remote_copy(
                src_ref=src[w], dst_ref=lnd[w], send_sem=send.at[w], recv_sem=recv.at[w],
                device_id=(x, y, 1 - c), device_id_type=MESH).start()
        token[...] = jnp.zeros_like(token)

    res = pl.pallas_call(
        body, name=name,
        out_shape=[pltpu.SemaphoreType.DMA((n,))] * 2 + [pltpu.HBM(a.shape, a.dtype) for a in bufs] + [_TOKEN],
        in_specs=[_HBM] * (2 * n),
        out_specs=[_SEM, _SEM] + [_HBM] * (2 * n) + [pl.BlockSpec(memory_space=pltpu.VMEM)],
        input_output_aliases={i: 2 + i for i in range(2 * n)},
        compiler_params=pltpu.CompilerParams(has_side_effects=_EFFECT),
    )(*[_hbm(a) for a in bufs])
    return (res[0], res[1]), res[2:2 + n], res[2 + n:2 + 2 * n], res[-1]


def _swap_wait(sums, lands, sems, after, *, name):
    n = len(sums)

    def body(*refs):
        src, lnd, send, recv = refs[:n], refs[n:2 * n], refs[2 * n], refs[2 * n + 1]
        x, y, c = _place()
        for w in range(n):
            copy = pltpu.make_async_remote_copy(
                src_ref=src[w], dst_ref=lnd[w], send_sem=send.at[w], recv_sem=recv.at[w],
                device_id=(x, y, 1 - c), device_id_type=MESH)
            copy.wait_send()
            copy.wait_recv()

    bufs = list(sums) + list(lands)
    res = pl.pallas_call(
        body, name=name, out_shape=[pltpu.HBM(a.shape, a.dtype) for a in bufs],
        in_specs=[_HBM] * (2 * n) + [_SEM, _SEM, _ANY], out_specs=[_HBM] * (2 * n),
        input_output_aliases={i: i for i in range(2 * n)},
        compiler_params=pltpu.CompilerParams(has_side_effects=_EFFECT),
    )(*bufs, *sems, after)
    return res[:n], res[n:]


def _adamw_math(w, g, m, v):
    m = ADAM_B1 * m + (1.0 - ADAM_B1) * g
    v = ADAM_B2 * v + (1.0 - ADAM_B2) * (g * g)
    m_hat = m / (1.0 - ADAM_B1 ** ADAM_STEP)
    v_hat = v / (1.0 - ADAM_B2 ** ADAM_STEP)
    delta = -ADAM_LR * (m_hat / (jnp.sqrt(v_hat) + ADAM_EPS) + ADAM_WD * w)
    return delta, m, v


def _adamw_pair(mine, theirs, w, m, v, *, name, tr=128):
    R, C = w.shape
    tr = min(tr, R)

    def body(a_ref, b_ref, w_ref, m_ref, v_ref, g_ref, d_ref, nm_ref, nv_ref):
        g = a_ref[...] + b_ref[...]
        g_ref[...] = g
        d_ref[...], nm_ref[...], nv_ref[...] = _adamw_math(w_ref[...], g, m_ref[...], v_ref[...])

    shape = jax.ShapeDtypeStruct((R, C), F32)
    return pl.pallas_call(
        body, name=name, out_shape=(shape,) * 4, grid=(R // tr,),
        in_specs=[_rows(tr, C)] * 5, out_specs=(_rows(tr, C),) * 4,
        compiler_params=_cparams("parallel"))(mine, theirs, w, m, v)


def _all_reduce_small(packed):
    R = packed.shape[0]
    half = R // 2

    def body(x_ref, g_ref, sib_ref, pair_ref, land_ref, send_sems, recv_sems):
        x, y, c = _place()
        me = 2 * x + y
        sibling = (x, y, 1 - c)

        swap = pltpu.make_async_remote_copy(
            src_ref=x_ref, dst_ref=sib_ref, send_sem=send_sems.at[0], recv_sem=recv_sems.at[0],
            device_id=sibling, device_id_type=MESH)
        swap.start()
        swap.wait()
        mine, theirs = x_ref[...], sib_ref[...]
        south = c == 0
        pair_ref[...] = jnp.where(south, mine, theirs) + jnp.where(south, theirs, mine)

        land_ref[me] = pair_ref[c]
        for j, (px, py) in enumerate(_other_chips(x, y)):
            pltpu.make_async_remote_copy(
                src_ref=pair_ref.at[c], dst_ref=land_ref.at[me], send_sem=send_sems.at[1 + j],
                recv_sem=recv_sems.at[1 + j], device_id=(px, py, c), device_id_type=MESH).start()
        for j, (px, py) in enumerate(_other_chips(x, y)):
            arrival = pltpu.make_async_remote_copy(
                src_ref=pair_ref.at[c], dst_ref=land_ref.at[2 * px + py], send_sem=send_sems.at[1 + j],
                recv_sem=recv_sems.at[1 + j], device_id=(px, py, c), device_id_type=MESH)
            arrival.wait_recv()
            arrival.wait_send()
        total = land_ref[0]
        for k in range(1, N_CHIPS):
            total = total + land_ref[k]
        g_ref[c] = total

        give = pltpu.make_async_remote_copy(
            src_ref=g_ref.at[c], dst_ref=g_ref.at[c], send_sem=send_sems.at[4], recv_sem=recv_sems.at[4],
            device_id=sibling, device_id_type=MESH)
        give.start()
        take = pltpu.make_async_remote_copy(
            src_ref=g_ref.at[c], dst_ref=g_ref.at[1 - c], send_sem=send_sems.at[4], recv_sem=recv_sems.at[4],
            device_id=sibling, device_id_type=MESH)
        take.wait_recv()
        give.wait_send()

    vm = pl.BlockSpec(memory_space=pltpu.VMEM)
    return pl.pallas_call(
        body, name="all_reduce_small", out_shape=jax.ShapeDtypeStruct((2, half, LANES), F32),
        in_specs=[vm], out_specs=vm,
        scratch_shapes=[pltpu.VMEM((2, half, LANES), F32), pltpu.VMEM((2, half, LANES), F32),
                        pltpu.VMEM((N_CHIPS, half, LANES), F32),
                        pltpu.SemaphoreType.DMA((5,)), pltpu.SemaphoreType.DMA((5,))],
        compiler_params=pltpu.CompilerParams(vmem_limit_bytes=VMEM_LIMIT_BYTES),
    )(packed.reshape(2, half, LANES)).reshape(R, LANES)


def _adamw_small(g, w, m, v):
    R = g.shape[0]
    tr = PACK_ROWS

    def body(g_ref, w_ref, m_ref, v_ref, d_ref, nm_ref, nv_ref):
        d_ref[...], nm_ref[...], nv_ref[...] = _adamw_math(w_ref[...], g_ref[...], m_ref[...], v_ref[...])

    shape = jax.ShapeDtypeStruct((R, LANES), F32)
    return pl.pallas_call(
        body, name="adamw_small", out_shape=(shape,) * 3, grid=(R // tr,),
        in_specs=[_rows(tr, LANES)] * 4, out_specs=(_rows(tr, LANES),) * 3,
        compiler_params=_cparams("parallel"))(g, w, m, v)


def _pack(arrays):
    parts, layout = [], []
    for a in arrays:
        n = a.size
        rows = -(-n // (8 * LANES)) * 8
        flat = jnp.pad(a.reshape(-1).astype(F32), (0, rows * LANES - n))
        parts.append(flat.reshape(rows, LANES))
        layout.append((rows, n, a.shape))
    total = sum(r for r, _, _ in layout)
    parts.append(jnp.zeros((-total % PACK_ROWS, LANES), F32))
    return jnp.concatenate(parts, axis=0), layout


def _unpack(buf, layout):
    out, r0 = [], 0
    for rows, n, shape in layout:
        out.append(buf[r0:r0 + rows].reshape(-1)[:n].reshape(shape))
        r0 += rows
    return out


SMALL = ("mix_norm_pre", "lam_re", "lam_im", "log_dt", "ssm_b_re", "ssm_b_im", "ssm_c_re", "ssm_c_im",
         "ssm_d", "b_glu", "attn_out_norm", "ssm_out_norm", "mix_norm_post", "mlp_norm_pre",
         "mlp_norm_post", "ple_norm_pre", "ple_norm_post")
BIG = ("w_in", "w_glu", "w_out", "w_up", "w_down", "w_ple_gate", "w_ple_proj")
WEIGHTS = ("mix_norm_pre", "w_in", "lam_re", "lam_im", "log_dt", "ssm_b_re", "ssm_b_im", "ssm_c_re",
           "ssm_c_im", "ssm_d", "w_glu", "b_glu", "attn_out_norm", "ssm_out_norm", "w_out",
           "mix_norm_post", "mlp_norm_pre", "w_up", "w_down", "mlp_norm_post", "ple_norm_pre",
           "w_ple_gate", "w_ple_proj", "ple_norm_post")


def kernel(x, p, mix_norm_pre, w_in, lam_re, lam_im, log_dt, ssm_b_re, ssm_b_im, ssm_c_re, ssm_c_im, ssm_d, w_glu, b_glu, attn_out_norm, ssm_out_norm, w_out, mix_norm_post, mlp_norm_pre, w_up, w_down, mlp_norm_post, ple_norm_pre, w_ple_gate, w_ple_proj, ple_norm_post, loss_target, m_mix_norm_pre, m_w_in, m_lam_re, m_lam_im, m_log_dt, m_ssm_b_re, m_ssm_b_im, m_ssm_c_re, m_ssm_c_im, m_ssm_d, m_w_glu, m_b_glu, m_attn_out_norm, m_ssm_out_norm, m_w_out, m_mix_norm_post, m_mlp_norm_pre, m_w_up, m_w_down, m_mlp_norm_post, m_ple_norm_pre, m_w_ple_gate, m_w_ple_proj, m_ple_norm_post, v_mix_norm_pre, v_w_in, v_lam_re, v_lam_im, v_log_dt, v_ssm_b_re, v_ssm_b_im, v_ssm_c_re, v_ssm_c_im, v_ssm_d, v_w_glu, v_b_glu, v_attn_out_norm, v_ssm_out_norm, v_w_out, v_mix_norm_post, v_mlp_norm_pre, v_w_up, v_w_down, v_mlp_norm_post, v_ple_norm_pre, v_w_ple_gate, v_w_ple_proj, v_ple_norm_post):
    args = dict(locals())
    W = {n: args[n][0] for n in WEIGHTS}
    Mo = {n: args["m_" + n][0] for n in WEIGHTS}
    Vo = {n: args["v_" + n][0] for n in WEIGHTS}
    xs, ps, tgt = x[0], p[0, 0], loss_target[0]
    S, D = xs.shape
    SW = W["ssm_d"].shape[0]
    AW = W["attn_out_norm"].shape[0]
    heads = AW // HEAD_DIM
    G = SW // SSM_C
    nbk = SW // LANES
    assert W["w_in"].shape[1] * N_CHIPS == 3 * AW + SW and AW == SW

    row = lambda a: a.reshape(1, -1)

    ag_groups = (("w_in",), ("w_glu", "w_out"), ("w_up",), ("w_down", "w_ple_gate", "w_ple_proj"))
    ag_names = [n for g in ag_groups for n in g]
    def in_halves(a):
        return a.reshape(N_CHIPS, 2, a.shape[1] // 2, a.shape[2])

    ag_sems, ag_land, _, ag_token = _exchange_start(
        [in_halves(_place_own(W[n], gather=True, name="ag_place_" + n)) for n in ag_names], [],
        [[ag_names.index(n) for n in g] for g in ag_groups], name="ag_start")

    def fetched(gi, after):
        return _exchange_wait([ag_land[ag_names.index(n)] for n in ag_groups[gi]], [], ag_sems[gi], after,
                              name=f"ag_wait_{gi}")

    def whole(gis, bufs):
        names = [n for gi in gis for n in ag_groups[gi]]
        return {n: a.reshape(N_CHIPS, -1, a.shape[-1]) for n, a in zip(names, bufs)}

    lr_e = W["lam_re"].reshape(nbk, 1, STATE_LANES)
    li_e = W["lam_im"].reshape(nbk, 1, STATE_LANES)
    ldt_e = jnp.repeat(W["log_dt"], SSM_P).reshape(nbk, 1, STATE_LANES)
    bre_e, bim_e = _expand_b(W["ssm_b_re"]), _expand_b(W["ssm_b_im"])
    cre_e, cim_e = _expand_c(W["ssm_c_re"]), _expand_c(W["ssm_c_im"])
    d_row = row(W["ssm_d"])

    hn1 = _norm_cast(xs, row(W["mix_norm_pre"]) + ag_token[0, 0], name="norm_in")
    w_in_f = whole([0], _pair_fill(fetched(0, hn1), name="ag_pair_0"))["w_in"]
    qkv_b = _proj_qkv(hn1, w_in_f)
    outs, lses = zip(*[_attn_fwd(qb, d, heads) for d, qb in zip(DILATIONS, qkv_b)])
    pair_a_sems, pair_a, pair_a_token = _pair_start(fetched(1, outs[-1]), name="ag_pair_start_a")
    u = _matmul(hn1, w_in_f, name="proj_u", b_shards=N_CHIPS, b_cols=(3 * AW, SW), after=pair_a_token)
    y1, y2b, st_r, st_i, ends_r, ends_i = _ssm_fwd(u, lr_e, li_e, ldt_e, bre_e, bim_e, cre_e, cim_e, d_row)
    pair_b_sems, pair_b, pair_b_token = _pair_start(fetched(2, y2b), name="ag_pair_start_b")
    full = whole([1], _pair_wait(pair_a, pair_a_sems, y2b, name="ag_pair_wait_a"))
    w_glu_f = full["w_glu"].reshape(SW, SW)
    w_out_f = full["w_out"].reshape(AW + SW, D)
    z = _matmul(y2b, w_glu_f, name="glu_z", after=pair_b_token)
    attn, lse_b, mixed = _mix_fwd(outs, lses, y1, z, row(W["b_glu"]), row(W["attn_out_norm"]), row(W["ssm_out_norm"]))
    mo = _matmul(mixed, w_out_f, name="mix_out")
    h1, hn2 = _res_norm(xs, mo, row(W["mix_norm_post"]), row(W["mlp_norm_pre"]), name="res_mix")
    w_up_f = whole([2], _pair_wait(pair_b, pair_b_sems, hn2, name="ag_pair_wait_b"))["w_up"]
    up, act = _matmul(hn2, w_up_f, name="mlp_up", b_shards=N_CHIPS, relu2=True)
    full = whole([3], _pair_fill(fetched(3, act), name="ag_pair_3"))
    w_down_f = full["w_down"].reshape(-1, D)
    w_pg_f = full["w_ple_gate"].reshape(D, D)
    w_pp_f = full["w_ple_proj"]
    ff = _matmul(act, w_down_f, name="mlp_down")
    h2, hn3 = _res_norm(h1, ff, row(W["mlp_norm_post"]), row(W["ple_norm_pre"]), name="res_mlp")
    gl = _matmul(hn3, w_pg_f, name="ple_gate")
    e = _matmul(ps.astype(BF16), w_pp_f, name="ple_proj", b_shards=N_CHIPS)

    dh3, dgl, de, loss_part, dg_ple_post = _final(h2, gl, e, row(W["ple_norm_post"]), tgt)
    gW = {}
    out_g, out_d, out_m, out_v = {}, {}, {}, {}

    def scatter_start(names, tag):
        parts = [gW[n] if gW[n].ndim == 3 else gW[n].reshape((N_CHIPS, -1, gW[n].shape[1])) for n in names]
        sems, land, src, token = _exchange_start(
            [_place_own(part, gather=False, name="rs_place_" + n) for n, part in zip(names, parts)], parts,
            [list(range(len(names)))], name=f"rs_start_{tag}")
        return (names, sems[0], land, src), token

    def scatter_sums(batches, after):
        names, sums = [], []
        for tag, (batch_names, sems, land, src) in batches:
            landed = _exchange_wait(land, src, sems, after, name=f"rs_wait_{tag}")
            names += batch_names
            sums += [_sum_partials(l, name="sum_" + n) for n, l in zip(batch_names, landed)]
        return names, sums

    def apply(names, sums, theirs):
        for n, a, b in zip(names, sums, theirs):
            out_g[n], out_d[n], out_m[n], out_v[n] = _adamw_pair(a, b, W[n], Mo[n], Vo[n], name="adamw_" + n)

    def swap_begin(batches, after, tag):
        names, sums = scatter_sums(batches, after)
        sems, sums, lands, token = _swap_start(sums, name=f"swap_start_{tag}")
        return (names, sems, sums, lands), token

    def swap_end(swap, after, tag):
        names, sems, sums, lands = swap
        sums, theirs = _swap_wait(sums, lands, sems, after, name=f"swap_wait_{tag}")
        apply(names, sums, theirs)

    def scatter_finish(batch, after, tag):
        names, sums = scatter_sums([(tag, batch)], after)
        apply(names, sums, _swap_with_sibling(sums, name=f"swap_{tag}"))

    gW["w_ple_proj"] = _matmul(ps.astype(BF16), de, name="d_w_ple_proj", ta=True, out_dtype=BF16, out_shards=N_CHIPS)
    gW["w_ple_gate"] = _matmul(hn3, dgl, name="d_w_ple_gate", ta=True, out_dtype=BF16)
    dhn3 = _matmul(dgl, w_pg_f, name="d_hn3", tb=True)
    dh2, dff, dg_ple_pre, dg_mlp_post = _bwd_res_norm(
        dh3, dhn3, h2, row(W["ple_norm_pre"]), ff, row(W["mlp_norm_post"]), name="bwd_res_mlp")
    gW["w_down"] = _matmul(act, dff, name="d_w_down", ta=True, out_dtype=BF16)
    batch1, token1 = scatter_start(("w_ple_proj", "w_ple_gate", "w_down"), 1)
    dup = _matmul(dff, w_down_f, name="d_up", tb=True, after=token1, relu2_of=up, out_dtype=BF16)
    gW["w_up"] = _matmul(hn2, dup, name="d_w_up", ta=True, out_dtype=BF16, out_shards=N_CHIPS)
    batch2, token2 = scatter_start(("w_up",), 2)
    dhn2 = _matmul(dup, w_up_f, name="d_hn2", tb=True, b_shards=N_CHIPS, after=token2)
    dh1, dmo, dg_mlp_pre, dg_mix_post = _bwd_res_norm(
        dh2, dhn2, h1, row(W["mlp_norm_pre"]), mo, row(W["mix_norm_post"]), name="bwd_res_mix")
    gW["w_out"] = _matmul(mixed, dmo, name="d_w_out", ta=True, out_dtype=BF16)
    dmixed = _matmul(dmo, w_out_f, name="d_mixed", tb=True)
    dattn_b, dd_b, dz, dy2a, dg_attn, dg_ssm, db_glu = _mix_bwd(
        dmixed, attn, y1, z, row(W["b_glu"]), row(W["attn_out_norm"]), row(W["ssm_out_norm"]))
    gW["w_glu"] = _matmul(y2b, dz, name="d_w_glu", ta=True, out_dtype=BF16)
    batch3, token3 = scatter_start(("w_out", "w_glu"), 3)
    dy2b = _matmul(dz, w_glu_f, name="d_y2", tb=True, after=token3)
    du, dar8, dai8, dcr_e, dci_e, dbr_e, dbi_e, dd8 = _ssm_bwd(
        u, y1, dy2a, dy2b, st_r, st_i, ends_r, ends_i, lr_e, li_e, ldt_e, bre_e, bim_e, cre_e, cim_e, d_row)
    swap_a, token_a = swap_begin([(1, batch1)], du, "a")
    dlr_e, dli_e, dldt_e, dbre_e, dbim_e = _ssm_param_bwd(dar8, dai8, dbr_e, dbi_e, lr_e, li_e, ldt_e, bre_e, bim_e)

    dqs, dks, dvs = zip(*[_attn_bwd(qb, da, l, dd_, d, heads, token_a)
                          for d, qb, da, l, dd_ in zip(DILATIONS, qkv_b, dattn_b, lse_b, dd_b)])
    dproj = _dproj_join(dqs, dks, dvs, du)
    swap_end(swap_a, dproj, "a")
    swap_b, token_b = swap_begin([(2, batch2), (3, batch3)], dproj, "b")
    gW["w_in"] = _matmul(hn1, dproj, name="d_w_in", ta=True, out_dtype=BF16, out_shards=N_CHIPS, after=token_b)
    batch4, token4 = scatter_start(("w_in",), 4)
    dhn1 = _matmul(dproj, w_in_f, name="d_hn1", tb=True, b_shards=N_CHIPS, after=token4)
    grad_x, dg_mix_pre = _bwd_first(dh1, dhn1, xs, row(W["mix_norm_pre"]))
    swap_end(swap_b, grad_x, "b")
    scatter_finish(batch4, grad_x, 4)

    small_g = {
        "mix_norm_pre": dg_mix_pre, "lam_re": dlr_e.reshape(G, SSM_P), "lam_im": dli_e.reshape(G, SSM_P),
        "log_dt": dldt_e.reshape(G, SSM_P)[:, 0], "ssm_b_re": _collapse_b(dbre_e), "ssm_b_im": _collapse_b(dbim_e),
        "ssm_c_re": _collapse_c(dcr_e), "ssm_c_im": _collapse_c(dci_e), "ssm_d": dd8.sum(axis=1).reshape(-1),
        "b_glu": db_glu, "attn_out_norm": dg_attn, "ssm_out_norm": dg_ssm, "mix_norm_post": dg_mix_post,
        "mlp_norm_pre": dg_mlp_pre, "mlp_norm_post": dg_mlp_post, "ple_norm_pre": dg_ple_pre,
        "ple_norm_post": dg_ple_post,
    }
    g_pack, layout = _pack([small_g[n].reshape(W[n].shape) for n in SMALL])
    w_pack, _ = _pack([W[n] for n in SMALL])
    m_pack, _ = _pack([Mo[n] for n in SMALL])
    v_pack, _ = _pack([Vo[n] for n in SMALL])
    g_sum = _all_reduce_small(g_pack)
    packed = (g_sum,) + tuple(_adamw_small(g_sum, w_pack, m_pack, v_pack))
    for dst, buf in zip((out_g, out_d, out_m, out_v), packed):
        dst.update(zip(SMALL, _unpack(buf, layout)))

    loss = lax.psum(loss_part[0, 0], ("x", "y", "c"))
    lead = lambda a: a[None]
    return (loss, grad_x[None],
            *[lead(out_g[n]) for n in WEIGHTS], *[lead(out_d[n]) for n in WEIGHTS],
            *[lead(out_m[n]) for n in WEIGHTS], *[lead(out_v[n]) for n in WEIGHTS])
```

```python
import functools
import math

import jax
import jax.numpy as jnp
from jax import lax
from jax.experimental import pallas as pl
from jax.experimental.pallas import tpu as pltpu

F32 = jnp.float32
BF16 = jnp.bfloat16
MESH = pl.DeviceIdType.MESH

RMS_EPS = 1e-6
NEG_INF = -1e30
HEAD_DIM = 128
BLK = 128
DILATIONS = (1, 4, 16)
ATTN_LOOKAHEAD = 3
SSM_C = 16
SSM_P = 64
LANES = 128
GROUPS_PER_BLOCK = LANES // SSM_C
STATE_LANES = GROUPS_PER_BLOCK * SSM_P
SSM_CHUNK = 512
TILE = 8
ADAM_LR, ADAM_B1, ADAM_B2, ADAM_EPS, ADAM_WD, ADAM_STEP = 1e-3, 0.9, 0.999, 1e-8, 0.01, 10
VMEM_LIMIT_BYTES = 56 * 1024 * 1024
N_CHIPS = 4
N_DEV = 8
PACK_ROWS = 256


def _cparams(*sem):
    return pltpu.CompilerParams(dimension_semantics=sem or None, vmem_limit_bytes=VMEM_LIMIT_BYTES)


def _rows(tr, w):
    return pl.BlockSpec((tr, w), lambda i: (i, 0))


def _vec(w):
    return pl.BlockSpec((1, w), lambda i: (0, 0))


def _sigmoid(x):
    return 1.0 / (1.0 + jnp.exp(-x))


def _gelu(x):
    c = math.sqrt(2.0 / math.pi)
    return 0.5 * x * (1.0 + jnp.tanh(c * (x + 0.044715 * x * x * x)))


def _gelu_grad(x):
    c = math.sqrt(2.0 / math.pi)
    th = jnp.tanh(c * (x + 0.044715 * x * x * x))
    return 0.5 * (1.0 + th) + 0.5 * x * (1.0 - th * th) * c * (1.0 + 3.0 * 0.044715 * x * x)


def _rms(x, g):
    r = lax.rsqrt(jnp.mean(x * x, axis=-1, keepdims=True) + RMS_EPS)
    return x * r * g


def _rms_bwd(dy, x, g):
    r = lax.rsqrt(jnp.mean(x * x, axis=-1, keepdims=True) + RMS_EPS)
    n = x * r
    dn = dy * g
    dx = r * (dn - n * jnp.mean(dn * n, axis=-1, keepdims=True))
    return dx, dy * n


def _colsum(a):
    return jnp.sum(a, axis=0, keepdims=True)


def _first(i):
    return i == 0


def _matmul(a, b, *, name, ta=False, tb=False, out_dtype=F32, b_shards=1, out_shards=1, b_cols=None,
            after=None, relu2=False, relu2_of=None, tm=1024, tn=1024, tk=2048):
    if ta:
        K, M = a.shape
    else:
        M, K = a.shape
    if b_shards > 1:
        rows, cols = b.shape[1], b.shape[2] * b_shards
    else:
        rows, cols = b.shape
    N, Kb = (rows, cols) if tb else (cols, rows)
    assert K == Kb, (a.shape, b.shape, ta, tb)
    col0 = 0
    if b_cols is not None:
        assert not tb
        col0, N = b_cols
    tm, tn, tk = min(tm, M), min(tn, N), min(tk, K)
    if b_shards > 1:
        shard_cols = cols // b_shards
        if tb:
            tk = min(tk, shard_cols)
        else:
            tn = min(tn, shard_cols)
    if out_shards > 1:
        tn = min(tn, N // out_shards)
    assert M % tm == 0 and N % tn == 0 and K % tk == 0 and col0 % tn == 0
    nk = K // tk
    j0 = col0 // tn

    a_spec = (pl.BlockSpec((tk, tm), lambda i, j, k: (k, i)) if ta
              else pl.BlockSpec((tm, tk), lambda i, j, k: (i, k)))
    if b_shards > 1:
        if tb:
            per = shard_cols // tk
            b_spec = pl.BlockSpec((None, tn, tk), lambda i, j, k: (k // per, j, k % per))
        else:
            per = shard_cols // tn
            b_spec = pl.BlockSpec((None, tk, tn), lambda i, j, k: ((j + j0) // per, k, (j + j0) % per))
    else:
        b_spec = (pl.BlockSpec((tn, tk), lambda i, j, k: (j, k)) if tb
                  else pl.BlockSpec((tk, tn), lambda i, j, k: (k, j + j0)))
    if out_shards > 1:
        per_o = (N // out_shards) // tn
        out_shape = jax.ShapeDtypeStruct((out_shards, M, N // out_shards), out_dtype)
        out_spec = pl.BlockSpec((None, tm, tn), lambda i, j, k: (j // per_o, i, j % per_o))
    else:
        out_shape = jax.ShapeDtypeStruct((M, N), out_dtype)
        out_spec = pl.BlockSpec((tm, tn), lambda i, j, k: (i, j))
    dims = (((0 if ta else 1,), (1 if tb else 0,)), ((), ()))

    extra, extra_specs = [], []
    if relu2_of is not None:
        assert out_shards == 1 and relu2_of.shape == (M, N)
        extra.append(relu2_of)
        extra_specs.append(pl.BlockSpec((tm, tn), lambda i, j, k: (i, j)))
    if after is not None:
        extra.append(after)
        extra_specs.append(pl.BlockSpec(after.shape, lambda i, j, k: (0, 0)))
    n_in = 2 + len(extra)
    if relu2:
        assert out_shards == 1
        out_shape = (out_shape, jax.ShapeDtypeStruct((M, N), BF16))
        out_spec = (out_spec, out_spec)

    def finish(acc, refs):
        o_ref = refs[n_in]
        if relu2_of is not None:
            acc = acc * (2.0 * jnp.maximum(refs[2][...], 0.0))
        o_ref[...] = acc.astype(o_ref.dtype)
        if relu2:
            r = jnp.maximum(acc, 0.0)
            refs[n_in + 1][...] = (r * r).astype(BF16)

    def body(*refs):
        prod = lax.dot_general(refs[0][...], refs[1][...], dims, preferred_element_type=F32)
        if nk == 1:
            finish(prod, refs)
            return
        acc_ref = refs[-1]
        k = pl.program_id(2)

        @pl.when(k == 0)
        def _():
            acc_ref[...] = prod

        @pl.when(k > 0)
        def _():
            acc_ref[...] += prod

        @pl.when(k == nk - 1)
        def _():
            finish(acc_ref[...], refs)

    return pl.pallas_call(
        body, name=name, out_shape=out_shape, grid=(M // tm, N // tn, nk),
        in_specs=[a_spec, b_spec] + extra_specs, out_specs=out_spec,
        scratch_shapes=[pltpu.VMEM((tm, tn), F32)] if nk > 1 else [],
        compiler_params=_cparams("parallel", "parallel", "arbitrary"),
    )(a, b, *extra)


def _norm_cast(x, g, *, name, tr=256):
    S, D = x.shape
    tr = min(tr, S)

    def body(x_ref, g_ref, o_ref):
        o_ref[...] = _rms(x_ref[...], g_ref[...]).astype(BF16)

    return pl.pallas_call(
        body, name=name, out_shape=jax.ShapeDtypeStruct((S, D), BF16), grid=(S // tr,),
        in_specs=[_rows(tr, D), _vec(D)], out_specs=_rows(tr, D),
        compiler_params=_cparams("parallel"))(x, g)


def _res_norm(res, y, g_post, g_next, *, name, tr=256):
    S, D = res.shape
    tr = min(tr, S)

    def body(res_ref, y_ref, gp_ref, gn_ref, h_ref, hn_ref):
        h = res_ref[...] + _rms(y_ref[...], gp_ref[...])
        h_ref[...] = h
        hn_ref[...] = _rms(h, gn_ref[...]).astype(BF16)

    return pl.pallas_call(
        body, name=name,
        out_shape=(jax.ShapeDtypeStruct((S, D), F32), jax.ShapeDtypeStruct((S, D), BF16)),
        grid=(S // tr,), in_specs=[_rows(tr, D), _rows(tr, D), _vec(D), _vec(D)],
        out_specs=(_rows(tr, D), _rows(tr, D)), compiler_params=_cparams("parallel"))(res, y, g_post, g_next)


def _residue_spec(tr, d, w):
    return pl.BlockSpec((tr // d, d * w), lambda i: (i, 0))


def _residue_shape(S, d, w, dtype):
    return jax.ShapeDtypeStruct((S // d, d * w), dtype)


def _residue_scratch(rows, w):
    return pltpu.VMEM((w // LANES, rows, LANES), F32)


def _fill_strips(scr, val):
    for s in range(scr.shape[0]):
        scr[s] = val[:, s * LANES:(s + 1) * LANES]


def _strips_to_residues(scr, o_ref, d):
    strips, rows, _ = scr.shape
    for r in range(d):
        for s in range(strips):
            col = (r * strips + s) * LANES
            o_ref[:, col:col + LANES] = scr[s, pl.ds(r, rows // d, stride=d), :].astype(o_ref.dtype)


def _to_residues(scr, val, o_ref, d):
    if d == 1:
        o_ref[...] = val.astype(o_ref.dtype)
        return
    _fill_strips(scr, val)
    _strips_to_residues(scr, o_ref, d)


def _from_residues(scr, in_ref, d):
    if d == 1:
        return in_ref[...].astype(F32)
    strips, rows, _ = scr.shape
    for r in range(d):
        for s in range(strips):
            col = (r * strips + s) * LANES
            scr[s, pl.ds(r, rows // d, stride=d), :] = in_ref[:, col:col + LANES].astype(F32)
    return jnp.concatenate([scr[s] for s in range(strips)], axis=1)


def _spread_heads(packed, heads, width=HEAD_DIM):
    per = LANES // heads
    return jnp.concatenate([jnp.broadcast_to(packed[:, h * per:h * per + 1], (packed.shape[0], width))
                            for h in range(heads)], axis=1)


def _mix_fwd(os, ls, y1, z, b_glu, g_attn, g_ssm, *, tr=128):
    S, SW = y1.shape
    AW = os[0].shape[1] // DILATIONS[0]
    heads = AW // HEAD_DIM
    tr = min(tr, S)
    nd = len(DILATIONS)

    def body(*refs):
        o_refs, l_refs = refs[:nd], refs[nd:2 * nd]
        y_ref, z_ref, b_ref, ga_ref, gs_ref, attn_ref = refs[2 * nd:2 * nd + 6]
        lse_refs = refs[2 * nd + 6:3 * nd + 6]
        mixed_ref, scr, scr_p = refs[3 * nd + 6:]
        ls_ = [_from_residues(scr_p, l_refs[n], d) for n, d in enumerate(DILATIONS)]
        m = functools.reduce(jnp.maximum, ls_)
        es = [jnp.exp(l - m) for l in ls_]
        tot = functools.reduce(jnp.add, es)
        attn = functools.reduce(jnp.add, [_spread_heads(e / tot, heads) * _from_residues(scr, o_refs[n], d)
                                          for n, (e, d) in enumerate(zip(es, DILATIONS))])
        attn_ref[...] = attn
        lse = m + jnp.log(tot)
        for n, d in enumerate(DILATIONS):
            _to_residues(scr_p, lse, lse_refs[n], d)
        ssm = _gelu(y_ref[...]) * _sigmoid(z_ref[...] + b_ref[...])
        mixed_ref[:, :AW] = _rms(attn, ga_ref[...]).astype(BF16)
        mixed_ref[:, AW:] = _rms(ssm, gs_ref[...]).astype(BF16)

    res_o = [_residue_spec(tr, d, AW) for d in DILATIONS]
    res_l = [_residue_spec(tr, d, LANES) for d in DILATIONS]
    res = pl.pallas_call(
        body, name="mix_fwd",
        out_shape=([jax.ShapeDtypeStruct((S, AW), F32)] + [_residue_shape(S, d, LANES, F32) for d in DILATIONS]
                   + [jax.ShapeDtypeStruct((S, AW + SW), BF16)]),
        grid=(S // tr,),
        in_specs=res_o + res_l + [_rows(tr, SW), _rows(tr, SW), _vec(SW), _vec(AW), _vec(SW)],
        out_specs=[_rows(tr, AW)] + res_l + [_rows(tr, AW + SW)],
        scratch_shapes=[_residue_scratch(tr, AW), _residue_scratch(tr, LANES)],
        compiler_params=_cparams("parallel"))(*os, *ls, y1, z, b_glu, g_attn, g_ssm)
    return res[0], res[1:1 + nd], res[1 + nd]


def _final(h2, gl, e, g_post, target, *, tr=128):
    S, D = h2.shape
    tr = min(tr, S)

    def body(h_ref, gl_ref, e_ref, g_ref, t_ref, dh_ref, dgl_ref, de_ref, loss_ref, dg_ref):
        i = pl.program_id(0)
        gate = _sigmoid(gl_ref[...])
        e_ = e_ref[...]
        ge = gate * e_
        g = g_ref[...]
        diff = h_ref[...] + _rms(ge, g) - t_ref[...]
        dh = diff * (1.0 / D)
        dh_ref[...] = dh
        dge, dgrow = _rms_bwd(dh, ge, g)
        dgl_ref[...] = (dge * e_ * gate * (1.0 - gate)).astype(BF16)
        de_ref[...] = (dge * gate).astype(BF16)
        part = _colsum(0.5 * jnp.mean(diff * diff, axis=-1, keepdims=True))

        @pl.when(_first(i))
        def _():
            loss_ref[...] = jnp.zeros_like(loss_ref)
            dg_ref[...] = jnp.zeros_like(dg_ref)

        loss_ref[...] += part + jnp.zeros((1, LANES), F32)
        dg_ref[...] += _colsum(dgrow)

    return pl.pallas_call(
        body, name="final_fwd_bwd",
        out_shape=(jax.ShapeDtypeStruct((S, D), F32), jax.ShapeDtypeStruct((S, D), BF16),
                   jax.ShapeDtypeStruct((S, D), BF16), jax.ShapeDtypeStruct((1, LANES), F32),
                   jax.ShapeDtypeStruct((1, D), F32)),
        grid=(S // tr,),
        in_specs=[_rows(tr, D), _rows(tr, D), _rows(tr, D), _vec(D), _rows(tr, D)],
        out_specs=(_rows(tr, D), _rows(tr, D), _rows(tr, D), _vec(LANES), _vec(D)),
        compiler_params=_cparams("arbitrary"))(h2, gl, e, g_post, target)


def _bwd_res_norm(dh_out, dhn, h, g_next, y, g_post, *, name, tr=128):
    S, D = h.shape
    tr = min(tr, S)

    def body(dho_ref, dhn_ref, h_ref, gn_ref, y_ref, gp_ref, dh_ref, dy_ref, dgn_ref, dgp_ref):
        i = pl.program_id(0)
        dx, dgn_rows = _rms_bwd(dhn_ref[...], h_ref[...], gn_ref[...])
        dh = dho_ref[...] + dx
        dh_ref[...] = dh
        dy, dgp_rows = _rms_bwd(dh, y_ref[...], gp_ref[...])
        dy_ref[...] = dy.astype(BF16)

        @pl.when(_first(i))
        def _():
            dgn_ref[...] = jnp.zeros_like(dgn_ref)
            dgp_ref[...] = jnp.zeros_like(dgp_ref)

        dgn_ref[...] += _colsum(dgn_rows)
        dgp_ref[...] += _colsum(dgp_rows)

    return pl.pallas_call(
        body, name=name,
        out_shape=(jax.ShapeDtypeStruct((S, D), F32), jax.ShapeDtypeStruct((S, D), BF16),
                   jax.ShapeDtypeStruct((1, D), F32), jax.ShapeDtypeStruct((1, D), F32)),
        grid=(S // tr,),
        in_specs=[_rows(tr, D), _rows(tr, D), _rows(tr, D), _vec(D), _rows(tr, D), _vec(D)],
        out_specs=(_rows(tr, D), _rows(tr, D), _vec(D), _vec(D)),
        compiler_params=_cparams("arbitrary"))(dh_out, dhn, h, g_next, y, g_post)


def _bwd_first(dh1, dhn1, x, g1, *, tr=256):
    S, D = x.shape
    tr = min(tr, S)

    def body(dh_ref, dhn_ref, x_ref, g_ref, dx_ref, dg_ref):
        i = pl.program_id(0)
        dx, dg_rows = _rms_bwd(dhn_ref[...], x_ref[...], g_ref[...])
        dx_ref[...] = dh_ref[...] + dx

        @pl.when(_first(i))
        def _():
            dg_ref[...] = jnp.zeros_like(dg_ref)

        dg_ref[...] += _colsum(dg_rows)

    return pl.pallas_call(
        body, name="bwd_first",
        out_shape=(jax.ShapeDtypeStruct((S, D), F32), jax.ShapeDtypeStruct((1, D), F32)),
        grid=(S // tr,), in_specs=[_rows(tr, D), _rows(tr, D), _rows(tr, D), _vec(D)],
        out_specs=(_rows(tr, D), _vec(D)), compiler_params=_cparams("arbitrary"))(dh1, dhn1, x, g1)


def _mix_bwd(dmixed, attn, y1, z, b_glu, g_attn, g_ssm, *, tr=256):
    S, AW = attn.shape
    SW = y1.shape[1]
    tr = min(tr, S)
    heads = AW // HEAD_DIM
    nd = len(DILATIONS)

    def body(*refs):
        dm_ref, a_ref, y_ref, z_ref, b_ref, ga_ref, gs_ref = refs[:7]
        da_refs, dd_refs = refs[7:7 + nd], refs[7 + nd:7 + 2 * nd]
        dz_ref, dy2_ref, dga_ref, dgs_ref, db_ref, scr, scr_p, dd_scr = refs[7 + 2 * nd:]
        i = pl.program_id(0)
        attn_ = a_ref[...]
        dattn, dga_rows = _rms_bwd(dm_ref[:, :AW], attn_, ga_ref[...])
        prod = dattn * attn_
        per = LANES // heads
        for h in range(heads):
            total = jnp.sum(prod[:, h * HEAD_DIM:(h + 1) * HEAD_DIM], axis=-1, keepdims=True)
            dd_scr[:, h * per:(h + 1) * per] = jnp.broadcast_to(total, (tr, per))
        for n, d in enumerate(DILATIONS):
            _to_residues(scr, dattn, da_refs[n], d)
            _to_residues(scr_p, dd_scr[...], dd_refs[n], d)
        y2 = _gelu(y_ref[...])
        gate = _sigmoid(z_ref[...] + b_ref[...])
        dssm, dgs_rows = _rms_bwd(dm_ref[:, AW:], y2 * gate, gs_ref[...])
        dz = dssm * y2 * gate * (1.0 - gate)
        dz_ref[...] = dz.astype(BF16)
        dy2_ref[...] = dssm * gate

        @pl.when(_first(i))
        def _():
            dga_ref[...] = jnp.zeros_like(dga_ref)
            dgs_ref[...] = jnp.zeros_like(dgs_ref)
            db_ref[...] = jnp.zeros_like(db_ref)

        dga_ref[...] += _colsum(dga_rows)
        dgs_ref[...] += _colsum(dgs_rows)
        db_ref[...] += _colsum(dz)

    res_a = [_residue_spec(tr, d, AW) for d in DILATIONS]
    res_d = [_residue_spec(tr, d, LANES) for d in DILATIONS]
    res = pl.pallas_call(
        body, name="mix_bwd",
        out_shape=([_residue_shape(S, d, AW, BF16) for d in DILATIONS]
                   + [_residue_shape(S, d, LANES, F32) for d in DILATIONS]
                   + [jax.ShapeDtypeStruct((S, SW), BF16), jax.ShapeDtypeStruct((S, SW), F32),
                      jax.ShapeDtypeStruct((1, AW), F32), jax.ShapeDtypeStruct((1, SW), F32),
                      jax.ShapeDtypeStruct((1, SW), F32)]),
        grid=(S // tr,),
        in_specs=[_rows(tr, AW + SW), _rows(tr, AW), _rows(tr, SW), _rows(tr, SW), _vec(SW), _vec(AW), _vec(SW)],
        out_specs=res_a + res_d + [_rows(tr, SW), _rows(tr, SW), _vec(AW), _vec(SW), _vec(SW)],
        scratch_shapes=[_residue_scratch(tr, AW), _residue_scratch(tr, LANES), pltpu.VMEM((tr, LANES), F32)],
        compiler_params=_cparams("arbitrary"))(dmixed, attn, y1, z, b_glu, g_attn, g_ssm)
    return (res[:nd], res[nd:2 * nd]) + tuple(res[2 * nd:])


def _attn_mask2(i):
    row = lax.broadcasted_iota(jnp.int32, (BLK, 2 * BLK), 0)
    col = lax.broadcasted_iota(jnp.int32, (BLK, 2 * BLK), 1)
    return jnp.logical_and(col >= row, jnp.logical_and(col <= row + BLK, jnp.logical_or(col >= BLK, i > 0)))


_NT = (((1,), (1,)), ((), ()))
_TN = (((0,), (0,)), ((), ()))


def _attn_in_specs(width, block_of):
    def at(part, prev):
        def index(r, i):
            blk = block_of(i)
            return (part, jnp.maximum(blk - 1, 0) if prev else blk, r)
        return pl.BlockSpec((None, BLK, width), index)
    return [at(0, False), at(1, False), at(1, True), at(2, False), at(2, True)]


def _proj_qkv(hn, w_in_f, *, tm=1024):
    S, D = hn.shape
    AW = w_in_f.shape[2]
    tm = min(tm, S)

    def body(a_ref, b_ref, *rest):
        o_refs, scr = rest[:-1], rest[-1]
        prod = jnp.dot(a_ref[...], b_ref[...], preferred_element_type=F32)
        _fill_strips(scr, prod)
        for o_ref, d in zip(o_refs, DILATIONS):
            if d == 1:
                o_ref[...] = prod.astype(BF16)
            else:
                _strips_to_residues(scr, o_ref, d)

    return pl.pallas_call(
        body, name="proj_qkv",
        out_shape=[jax.ShapeDtypeStruct((3, S // d, d * AW), BF16) for d in DILATIONS], grid=(S // tm, 3),
        in_specs=[pl.BlockSpec((tm, D), lambda i, j: (i, 0)), pl.BlockSpec((None, D, AW), lambda i, j: (j, 0, 0))],
        out_specs=[pl.BlockSpec((None, tm // d, d * AW), lambda i, j: (j, i, 0)) for d in DILATIONS],
        scratch_shapes=[_residue_scratch(tm, AW)],
        compiler_params=_cparams("parallel", "parallel"))(hn, w_in_f)


def _attn_fwd(qkv, d, heads):
    M = qkv.shape[1]
    nb = M // BLK
    width = heads * HEAD_DIM
    per = LANES // heads
    scale = 1.0 / math.sqrt(HEAD_DIM)

    def body(q_ref, kc_ref, kp_ref, vc_ref, vp_ref, o_ref, l_ref):
        mask = _attn_mask2(pl.program_id(1))
        ones = jnp.ones((2 * BLK, HEAD_DIM), BF16)

        def scores(h):
            sl = slice(h * HEAD_DIM, (h + 1) * HEAD_DIM)
            k2 = jnp.concatenate([kp_ref[:, sl], kc_ref[:, sl]], axis=0)
            return lax.dot_general(q_ref[:, sl], k2, _NT, preferred_element_type=F32)

        ahead = [scores(h) for h in range(min(ATTN_LOOKAHEAD, heads))]
        for h in range(heads):
            sl = slice(h * HEAD_DIM, (h + 1) * HEAD_DIM)
            s = jnp.where(mask, ahead.pop(0) * scale, NEG_INF)
            if h + ATTN_LOOKAHEAD < heads:
                ahead.append(scores(h + ATTN_LOOKAHEAD))
            v2 = jnp.concatenate([vp_ref[:, sl], vc_ref[:, sl]], axis=0)
            m = jnp.max(jnp.maximum(s[:, :BLK], s[:, BLK:]), axis=-1, keepdims=True)
            p = jnp.exp(s - m).astype(BF16)
            tot = jnp.dot(p, ones, preferred_element_type=F32)
            o_ref[:, sl] = jnp.dot(p, v2, preferred_element_type=F32) / tot
            l_ref[:, h * per:(h + 1) * per] = m + jnp.log(tot[:, :per])

    return pl.pallas_call(
        body, name=f"attn_fwd_d{d}",
        out_shape=(jax.ShapeDtypeStruct((M, d * width), F32), jax.ShapeDtypeStruct((M, d * LANES), F32)),
        grid=(d, nb), in_specs=_attn_in_specs(width, lambda i: i),
        out_specs=(pl.BlockSpec((BLK, width), lambda r, i: (i, r)), pl.BlockSpec((BLK, LANES), lambda r, i: (i, r))),
        compiler_params=_cparams("parallel", "parallel"))(qkv, qkv, qkv, qkv, qkv)


def _attn_bwd(qkv, dattn, lse, dd, d, heads, after):
    M = qkv.shape[1]
    nb = M // BLK
    width = heads * HEAD_DIM
    per = LANES // heads
    scale = 1.0 / math.sqrt(HEAD_DIM)

    def block_of(i):
        return nb - 1 - i

    def body(q_ref, kc_ref, kp_ref, vc_ref, vp_ref, da_ref, l_ref, dd_ref, after_ref,
             dq_ref, dk_ref, dv_ref, dk_carry, dv_carry):
        @pl.when(pl.program_id(1) == 0)
        def _():
            dk_carry[...] = jnp.zeros_like(dk_carry)
            dv_carry[...] = jnp.zeros_like(dv_carry)

        mask = _attn_mask2(block_of(pl.program_id(1)))

        def products(h):
            sl = slice(h * HEAD_DIM, (h + 1) * HEAD_DIM)
            k2 = jnp.concatenate([kp_ref[:, sl], kc_ref[:, sl]], axis=0)
            v2 = jnp.concatenate([vp_ref[:, sl], vc_ref[:, sl]], axis=0)
            return (lax.dot_general(q_ref[:, sl], k2, _NT, preferred_element_type=F32),
                    lax.dot_general(da_ref[:, sl], v2, _NT, preferred_element_type=F32), k2)

        ahead = [products(h) for h in range(min(ATTN_LOOKAHEAD, heads))]
        for h in range(heads):
            sl = slice(h * HEAD_DIM, (h + 1) * HEAD_DIM)
            qk, dp, k2 = ahead.pop(0)
            if h + ATTN_LOOKAHEAD < heads:
                ahead.append(products(h + ATTN_LOOKAHEAD))
            q, da = q_ref[:, sl], da_ref[:, sl]
            lse_ = jnp.broadcast_to(l_ref[:, h * per:h * per + 1], (BLK, 2 * BLK))
            dd_ = jnp.broadcast_to(dd_ref[:, h * per:h * per + 1], (BLK, 2 * BLK))
            p = jnp.where(mask, jnp.exp(jnp.where(mask, qk * scale, NEG_INF) - lse_), 0.0)
            ds = (p * (dp - dd_) * scale).astype(BF16)
            dq_ref[:, sl] = jnp.dot(ds, k2, preferred_element_type=F32).astype(BF16)
            dk2 = lax.dot_general(ds, q, _TN, preferred_element_type=F32)
            dv2 = lax.dot_general(p.astype(BF16), da, _TN, preferred_element_type=F32)
            dk_ref[:, sl] = (dk2[BLK:] + dk_carry[:, sl]).astype(BF16)
            dv_ref[:, sl] = (dv2[BLK:] + dv_carry[:, sl]).astype(BF16)
            dk_carry[:, sl] = dk2[:BLK]
            dv_carry[:, sl] = dv2[:BLK]

    blk = pl.BlockSpec((BLK, width), lambda r, i: (block_of(i), r))
    packed = pl.BlockSpec((BLK, LANES), lambda r, i: (block_of(i), r))
    shape = jax.ShapeDtypeStruct((M, d * width), BF16)
    return pl.pallas_call(
        body, name=f"attn_bwd_d{d}", out_shape=(shape,) * 3, grid=(d, nb),
        in_specs=(_attn_in_specs(width, block_of) + [blk, packed, packed]
                  + [pl.BlockSpec(after.shape, lambda r, i: (0, 0))]), out_specs=(blk,) * 3,
        scratch_shapes=[pltpu.VMEM((BLK, width), F32), pltpu.VMEM((BLK, width), F32)],
        compiler_params=_cparams("arbitrary", "arbitrary"))(qkv, qkv, qkv, qkv, qkv, dattn, lse, dd, after)


def _dproj_join(dqs, dks, dvs, du, *, tr=256):
    S, SW = du.shape
    AW = dqs[0].shape[1]
    tr = min(tr, S)
    nd = len(DILATIONS)

    def body(*refs):
        du_ref, out_ref, scr = refs[3 * nd:]
        for part in range(3):
            total = functools.reduce(jnp.add, [_from_residues(scr, refs[part * nd + n], d)
                                               for n, d in enumerate(DILATIONS)])
            out_ref[:, part * AW:(part + 1) * AW] = total.astype(BF16)
        out_ref[:, 3 * AW:] = du_ref[...].astype(BF16)

    return pl.pallas_call(
        body, name="dproj_join", out_shape=jax.ShapeDtypeStruct((S, 3 * AW + SW), BF16), grid=(S // tr,),
        in_specs=[_residue_spec(tr, d, AW) for d in DILATIONS] * 3 + [_rows(tr, SW)],
        out_specs=_rows(tr, 3 * AW + SW), scratch_shapes=[_residue_scratch(tr, AW)],
        compiler_params=_cparams("parallel"))(*dqs, *dks, *dvs, du)


def _ssm_disc(lr, li, ldt):
    dt = jnp.exp(ldt)
    mag = jnp.exp(lr * dt)
    ar = mag * jnp.cos(li * dt)
    ai = mag * jnp.sin(li * dt)
    nr = ar - 1.0
    den = lr * lr + li * li
    return ar, ai, (nr * lr + ai * li) / den, (ai * lr - nr * li) / den


def _ssm_tile_powers(lr, li, ldt, reverse):
    t = lax.broadcasted_iota(jnp.int32, (TILE, 1), 0)
    n = (TILE - t if reverse else t + 1).astype(F32)
    dt = jnp.exp(ldt)
    mag = jnp.exp(n * (lr * dt))
    ang = n * (li * dt)
    return mag * jnp.cos(ang), mag * jnp.sin(ang) * (-1.0 if reverse else 1.0)


def _cmul(ar, ai, br, bi):
    return ar * br - ai * bi, ar * bi + ai * br


LOG_STEPS = 3


def _ssm_step_tables(ar, ai, reverse):
    sub = lax.broadcasted_iota(jnp.int32, (TILE, ar.shape[-1]), 0)
    tables = []
    for k in range(LOG_STEPS):
        keep = sub < TILE - (1 << k) if reverse else sub >= (1 << k)
        tables.append((jnp.where(keep, ar, 0.0), jnp.where(keep, ai, 0.0)))
        ar, ai = _cmul(ar, ai, ar, ai)
    return tables


def _scan(xr, xi, steps, pr, pi, cr, ci, reverse):
    T, lanes = xr.shape
    n = T // TILE
    xr, xi = xr.reshape(n, TILE, lanes), xi.reshape(n, TILE, lanes)
    for k, (mr, mi) in enumerate(steps):
        shift = TILE - (1 << k) if reverse else 1 << k
        qr, qi = _cmul(mr, mi, pltpu.roll(xr, shift, 1), pltpu.roll(xi, shift, 1))
        xr, xi = xr + qr, xi + qi
    out_r, out_i = [None] * n, [None] * n
    edge = 0 if reverse else TILE - 1
    for j in (reversed(range(n)) if reverse else range(n)):
        er, ei = _cmul(pr, pi, cr, ci)
        sr, si = xr[j] + er, xi[j] + ei
        out_r[j], out_i[j] = sr, si
        cr, ci = sr[edge:edge + 1], si[edge:edge + 1]
    return jnp.concatenate(out_r, axis=0), jnp.concatenate(out_i, axis=0), cr, ci


def _ssm_specs(T, nch, rev):
    def t_of(c):
        return nch - 1 - c if rev else c
    tok = pl.BlockSpec((T, LANES), lambda j, c: (t_of(c), j))
    par = pl.BlockSpec((None, 1, STATE_LANES), lambda j, c: (j, 0, 0))
    bmat = pl.BlockSpec((None, LANES, STATE_LANES), lambda j, c: (j, 0, 0))
    cmat = pl.BlockSpec((None, STATE_LANES, LANES), lambda j, c: (j, 0, 0))
    dvec = pl.BlockSpec((1, LANES), lambda j, c: (0, j))
    return tok, par, bmat, cmat, dvec


def _ssm_fwd(u, lr_e, li_e, ldt_e, bre_e, bim_e, cre_e, cim_e, d_skip):
    S, SW = u.shape
    T = min(SSM_CHUNK, S)
    nch, nbk = S // T, SW // LANES
    tok, par, bmat, cmat, dvec = _ssm_specs(T, nch, False)
    state_spec = pl.BlockSpec((T, STATE_LANES), lambda j, c: (c, j))
    carry_spec = pl.BlockSpec((None, 1, STATE_LANES), lambda j, c: (c, 0, j))

    def body(u_ref, lr_ref, li_ref, ldt_ref, bre_ref, bim_ref, cre_ref, cim_ref, d_ref,
             y_ref, y2_ref, sr_ref, si_ref, er_ref, ei_ref, bbr, bbi, steps, pw, carry):
        c = pl.program_id(1)

        @pl.when(c == 0)
        def _():
            lr, li, ldt = lr_ref[...], li_ref[...], ldt_ref[...]
            ar, ai, kr, ki = _ssm_disc(lr, li, ldt)
            for k, (mr, mi) in enumerate(_ssm_step_tables(ar, ai, False)):
                steps[0, k], steps[1, k] = mr, mi
            bbr[...] = (kr * bre_ref[...] - ki * bim_ref[...]).astype(BF16)
            bbi[...] = (kr * bim_ref[...] + ki * bre_ref[...]).astype(BF16)
            pw[0], pw[1] = _ssm_tile_powers(lr, li, ldt, False)
            carry[...] = jnp.zeros_like(carry)

        u_ = u_ref[...]
        ub = u_.astype(BF16)
        sr, si, cr, ci = _scan(jnp.dot(ub, bbr[...], preferred_element_type=F32),
                               jnp.dot(ub, bbi[...], preferred_element_type=F32),
                               [(steps[0, k], steps[1, k]) for k in range(LOG_STEPS)],
                               pw[0], pw[1], carry[0], carry[1], False)
        carry[0], carry[1] = cr, ci
        er_ref[...], ei_ref[...] = cr, ci
        sr_ref[...], si_ref[...] = sr, si
        y0 = (jnp.dot(sr.astype(BF16), cre_ref[...].astype(BF16), preferred_element_type=F32)
              - jnp.dot(si.astype(BF16), cim_ref[...].astype(BF16), preferred_element_type=F32))
        y1 = y0 + d_ref[...] * u_
        y_ref[...] = y1
        y2_ref[...] = _gelu(y1).astype(BF16)

    states = jax.ShapeDtypeStruct((S, nbk * STATE_LANES), F32)
    ends = jax.ShapeDtypeStruct((nch, 1, nbk * STATE_LANES), F32)
    return pl.pallas_call(
        body, name="ssm_fwd",
        out_shape=(jax.ShapeDtypeStruct((S, SW), F32), jax.ShapeDtypeStruct((S, SW), BF16), states, states, ends, ends),
        grid=(nbk, nch), in_specs=[tok, par, par, par, bmat, bmat, cmat, cmat, dvec],
        out_specs=(tok, tok, state_spec, state_spec, carry_spec, carry_spec),
        scratch_shapes=[pltpu.VMEM((LANES, STATE_LANES), BF16), pltpu.VMEM((LANES, STATE_LANES), BF16),
                        pltpu.VMEM((2, LOG_STEPS, TILE, STATE_LANES), F32), pltpu.VMEM((2, TILE, STATE_LANES), F32),
                        pltpu.VMEM((2, 1, STATE_LANES), F32)],
        compiler_params=_cparams("arbitrary", "arbitrary"),
    )(u, lr_e, li_e, ldt_e, bre_e, bim_e, cre_e, cim_e, d_skip)


def _ssm_bwd(u, y1, dy2a, dy2b, st_r, st_i, ends_r, ends_i, lr_e, li_e, ldt_e, bre_e, bim_e, cre_e, cim_e, d_skip):
    S, SW = u.shape
    T = min(SSM_CHUNK, S)
    nch, nbk = S // T, SW // LANES
    tok, par, bmat, cmat, dvec = _ssm_specs(T, nch, True)
    state_spec = pl.BlockSpec((T, STATE_LANES), lambda j, c: (nch - 1 - c, j))
    prev_spec = pl.BlockSpec((None, 1, STATE_LANES), lambda j, c: (jnp.maximum(nch - 2 - c, 0), 0, j))
    acc8 = pl.BlockSpec((None, 8, STATE_LANES), lambda j, c: (j, 0, 0))
    dd8 = pl.BlockSpec((None, 8, LANES), lambda j, c: (j, 0, 0))

    def body(u_ref, y_ref, da_ref, db_ref, sr_ref, si_ref, pr_ref, pi_ref, lr_ref, li_ref, ldt_ref,
             bre_ref, bim_ref, cre_ref, cim_ref, d_ref,
             du_ref, dar_ref, dai_ref, dcr_ref, dci_ref, dbr_ref, dbi_ref, ddk_ref,
             bbr, bbi, steps, pw, carry):
        c = pl.program_id(1)

        @pl.when(c == 0)
        def _():
            lr, li, ldt = lr_ref[...], li_ref[...], ldt_ref[...]
            ar, ai, kr, ki = _ssm_disc(lr, li, ldt)
            for k, (mr, mi) in enumerate(_ssm_step_tables(ar, -ai, True)):
                steps[0, k], steps[1, k] = mr, mi
            bbr[...] = (kr * bre_ref[...] - ki * bim_ref[...]).astype(BF16)
            bbi[...] = (kr * bim_ref[...] + ki * bre_ref[...]).astype(BF16)
            pw[0], pw[1] = _ssm_tile_powers(lr, li, ldt, True)
            carry[...] = jnp.zeros_like(carry)
            for ref in (dar_ref, dai_ref, dcr_ref, dci_ref, dbr_ref, dbi_ref, ddk_ref):
                ref[...] = jnp.zeros_like(ref)

        u_ = u_ref[...]
        ub = u_.astype(BF16)
        dy1 = (da_ref[...] + db_ref[...]) * _gelu_grad(y_ref[...])
        dyb = dy1.astype(BF16)

        sr, si = sr_ref[...], si_ref[...]
        has_prev = c < nch - 1
        s0r = jnp.where(has_prev, pr_ref[...], 0.0)
        s0i = jnp.where(has_prev, pi_ref[...], 0.0)

        cre_b, cim_b = cre_ref[...].astype(BF16), cim_ref[...].astype(BF16)
        gr, gi, cr, ci = _scan(lax.dot_general(dyb, cre_b, _NT, preferred_element_type=F32),
                               -lax.dot_general(dyb, cim_b, _NT, preferred_element_type=F32),
                               [(steps[0, k], steps[1, k]) for k in range(LOG_STEPS)],
                               pw[0], pw[1], carry[0], carry[1], True)
        carry[0], carry[1] = cr, ci

        row = lax.broadcasted_iota(jnp.int32, (T, STATE_LANES), 0)
        spr = jnp.where(row == 0, s0r, pltpu.roll(sr, 1, 0))
        spi = jnp.where(row == 0, s0i, pltpu.roll(si, 1, 0))

        def fold(a):
            return jnp.sum(a.reshape(T // 8, 8, a.shape[-1]), axis=0)

        dar_ref[...] += fold(gr * spr + gi * spi)
        dai_ref[...] += fold(gi * spr - gr * spi)
        srb, sib, grb, gib = sr.astype(BF16), si.astype(BF16), gr.astype(BF16), gi.astype(BF16)
        dcr_ref[...] += lax.dot_general(srb, dyb, _TN, preferred_element_type=F32)
        dci_ref[...] -= lax.dot_general(sib, dyb, _TN, preferred_element_type=F32)
        dbr_ref[...] += lax.dot_general(ub, grb, _TN, preferred_element_type=F32)
        dbi_ref[...] += lax.dot_general(ub, gib, _TN, preferred_element_type=F32)
        du_ref[...] = (lax.dot_general(grb, bbr[...], _NT, preferred_element_type=F32)
                       + lax.dot_general(gib, bbi[...], _NT, preferred_element_type=F32)
                       + dy1 * d_ref[...])
        ddk_ref[...] += fold(dy1 * u_)

    return pl.pallas_call(
        body, name="ssm_bwd",
        out_shape=(jax.ShapeDtypeStruct((S, SW), F32),
                   jax.ShapeDtypeStruct((nbk, 8, STATE_LANES), F32), jax.ShapeDtypeStruct((nbk, 8, STATE_LANES), F32),
                   jax.ShapeDtypeStruct((nbk, STATE_LANES, LANES), F32), jax.ShapeDtypeStruct((nbk, STATE_LANES, LANES), F32),
                   jax.ShapeDtypeStruct((nbk, LANES, STATE_LANES), F32), jax.ShapeDtypeStruct((nbk, LANES, STATE_LANES), F32),
                   jax.ShapeDtypeStruct((nbk, 8, LANES), F32)),
        grid=(nbk, nch),
        in_specs=[tok, tok, tok, tok, state_spec, state_spec, prev_spec, prev_spec, par, par, par,
                  bmat, bmat, cmat, cmat, dvec],
        out_specs=(tok, acc8, acc8, cmat, cmat, bmat, bmat, dd8),
        scratch_shapes=[pltpu.VMEM((LANES, STATE_LANES), BF16), pltpu.VMEM((LANES, STATE_LANES), BF16),
                        pltpu.VMEM((2, LOG_STEPS, TILE, STATE_LANES), F32), pltpu.VMEM((2, TILE, STATE_LANES), F32),
                        pltpu.VMEM((2, 1, STATE_LANES), F32)],
        compiler_params=_cparams("arbitrary", "arbitrary"),
    )(u, y1, dy2a, dy2b, st_r, st_i, ends_r, ends_i, lr_e, li_e, ldt_e, bre_e, bim_e, cre_e, cim_e, d_skip)


def _ssm_param_bwd(dar8, dai8, dbr_e, dbi_e, lr_e, li_e, ldt_e, bre_e, bim_e):
    nbk = lr_e.shape[0]
    par = pl.BlockSpec((None, 1, STATE_LANES), lambda j: (j, 0, 0))
    acc8 = pl.BlockSpec((None, 8, STATE_LANES), lambda j: (j, 0, 0))
    bmat = pl.BlockSpec((None, LANES, STATE_LANES), lambda j: (j, 0, 0))

    def body(dar_ref, dai_ref, dbr_ref, dbi_ref, lr_ref, li_ref, ldt_ref, bre_ref, bim_ref,
             dlr_ref, dli_ref, dldt_ref, dbre_ref, dbim_ref):
        lr, li, ldt = lr_ref[...], li_ref[...], ldt_ref[...]
        (ar, ai, kr, ki), vjp = jax.vjp(_ssm_disc, lr, li, ldt)
        dbr, dbi, bre, bim = dbr_ref[...], dbi_ref[...], bre_ref[...], bim_ref[...]
        dbre_ref[...] = kr * dbr + ki * dbi
        dbim_ref[...] = kr * dbi - ki * dbr
        dkr = _colsum(dbr * bre + dbi * bim)
        dki = _colsum(dbi * bre - dbr * bim)
        dlr, dli, dldt = vjp((_colsum(dar_ref[...]), _colsum(dai_ref[...]), dkr, dki))
        dlr_ref[...] = dlr
        dli_ref[...] = dli
        tot = jnp.broadcast_to(dldt, (8, STATE_LANES))
        sh = 1
        while sh < SSM_P:
            tot = tot + pltpu.roll(tot, STATE_LANES - sh, 1)
            sh *= 2
        dldt_ref[...] = tot[:1]

    vec = jax.ShapeDtypeStruct((nbk, 1, STATE_LANES), F32)
    mat = jax.ShapeDtypeStruct((nbk, LANES, STATE_LANES), F32)
    return pl.pallas_call(
        body, name="ssm_param_bwd", out_shape=(vec, vec, vec, mat, mat), grid=(nbk,),
        in_specs=[acc8, acc8, bmat, bmat, par, par, par, bmat, bmat],
        out_specs=(par, par, par, bmat, bmat), compiler_params=_cparams("parallel"),
    )(dar8, dai8, dbr_e, dbi_e, lr_e, li_e, ldt_e, bre_e, bim_e)


def _expand_b(b):
    G = b.shape[0]
    bt = b.transpose(0, 2, 1).reshape(G // GROUPS_PER_BLOCK, GROUPS_PER_BLOCK, SSM_C, SSM_P)
    eye = jnp.eye(GROUPS_PER_BLOCK, dtype=b.dtype)
    return (bt[:, :, :, None, :] * eye[None, :, None, :, None]).reshape(G // GROUPS_PER_BLOCK, LANES, STATE_LANES)


def _collapse_b(be):
    nbk = be.shape[0]
    eye = jnp.eye(GROUPS_PER_BLOCK, dtype=be.dtype)
    d5 = be.reshape(nbk, GROUPS_PER_BLOCK, SSM_C, GROUPS_PER_BLOCK, SSM_P)
    d4 = (d5 * eye[None, :, None, :, None]).sum(axis=3)
    return d4.transpose(0, 1, 3, 2).reshape(nbk * GROUPS_PER_BLOCK, SSM_P, SSM_C)


def _expand_c(cm):
    G = cm.shape[0]
    ct = cm.transpose(0, 2, 1).reshape(G // GROUPS_PER_BLOCK, GROUPS_PER_BLOCK, SSM_P, SSM_C)
    eye = jnp.eye(GROUPS_PER_BLOCK, dtype=cm.dtype)
    return (ct[:, :, :, None, :] * eye[None, :, None, :, None]).reshape(G // GROUPS_PER_BLOCK, STATE_LANES, LANES)


def _collapse_c(ce):
    nbk = ce.shape[0]
    eye = jnp.eye(GROUPS_PER_BLOCK, dtype=ce.dtype)
    d5 = ce.reshape(nbk, GROUPS_PER_BLOCK, SSM_P, GROUPS_PER_BLOCK, SSM_C)
    d4 = (d5 * eye[None, :, None, :, None]).sum(axis=3)
    return d4.transpose(0, 1, 3, 2).reshape(nbk * GROUPS_PER_BLOCK, SSM_C, SSM_P)


def _place():
    x, y, c = lax.axis_index("x"), lax.axis_index("y"), lax.axis_index("c")
    return x, y, c


def _other_chips(x, y):
    return [(1 - x, y), (x, 1 - y), (1 - x, 1 - y)]


_ANY = pl.BlockSpec(memory_space=pl.ANY)


_HBM = pl.BlockSpec(memory_space=pltpu.HBM)
_SEM = pl.BlockSpec(memory_space=pltpu.SEMAPHORE)
_EFFECT = pltpu.SideEffectType.DATAFLOW_SIDE_EFFECTING
_TOKEN = jax.ShapeDtypeStruct((8, LANES), F32)


def _hbm(a):
    return pltpu.with_memory_space_constraint(a, pltpu.HBM)


def _place_own(src, *, gather, name, after=None, tr=512):
    R, C = src.shape[-2:]
    tr = min(tr, R)
    x, y, _ = _place()
    me = (2 * x + y).astype(jnp.int32).reshape(1)
    extra = [] if after is None else [after]

    def body(me_ref, s_ref, *rest):
        rest[-1][...] = s_ref[...].astype(BF16)

    own = pl.BlockSpec((None, tr, C), lambda i, me_ref: (me_ref[0], i, 0))
    grid_spec = pltpu.PrefetchScalarGridSpec(
        num_scalar_prefetch=1, grid=(R // tr,),
        in_specs=([pl.BlockSpec((tr, C), lambda i, me_ref: (i, 0)) if gather else own]
                  + [pl.BlockSpec(a.shape, lambda i, me_ref: (0, 0)) for a in extra]), out_specs=own)
    return pl.pallas_call(
        body, name=name, grid_spec=grid_spec, out_shape=jax.ShapeDtypeStruct((N_CHIPS, R, C), BF16),
        compiler_params=_cparams("parallel"))(me, src, *extra)


def _exchange_copy(src_slot, land_slot, send, recv, k, j, peer, c):
    return pltpu.make_async_remote_copy(
        src_ref=src_slot, dst_ref=land_slot, send_sem=send.at[3 * k + j], recv_sem=recv.at[3 * k + j],
        device_id=(peer[0], peer[1], c), device_id_type=MESH)


def _exchange_start(lands, srcs, groups, *, name):
    n, ng = len(lands), len(groups)
    bufs = list(lands) + list(srcs)
    nb = len(bufs)

    def body(*refs):
        lnd, src, sems = refs[:n], refs[n:nb], refs[nb:nb + 2 * ng]
        token = refs[2 * nb + 2 * ng]
        x, y, c = _place()
        me = 2 * x + y
        for gi, group in enumerate(groups):
            for k, w in enumerate(group):
                for j, peer in enumerate(_other_chips(x, y)):
                    if src:
                        sent, dst = src[w].at[2 * peer[0] + peer[1]], lnd[w].at[me]
                    else:
                        sent = dst = lnd[w].at[me, c]
                    _exchange_copy(sent, dst, sems[2 * gi], sems[2 * gi + 1], k, j, peer, c).start()
        token[...] = jnp.zeros_like(token)

    sem_shapes = [pltpu.SemaphoreType.DMA((3 * len(g),)) for g in groups for _ in range(2)]
    res = pl.pallas_call(
        body, name=name,
        out_shape=sem_shapes + [pltpu.HBM(a.shape, a.dtype) for a in bufs] + [_TOKEN],
        in_specs=[_HBM] * nb,
        out_specs=[_SEM] * (2 * ng) + [_HBM] * nb + [pl.BlockSpec(memory_space=pltpu.VMEM)],
        input_output_aliases={i: 2 * ng + i for i in range(nb)},
        compiler_params=pltpu.CompilerParams(has_side_effects=_EFFECT),
    )(*[_hbm(a) for a in bufs])
    sems = [(res[2 * gi], res[2 * gi + 1]) for gi in range(ng)]
    return sems, res[2 * ng:2 * ng + n], res[2 * ng + n:2 * ng + nb], res[-1]


def _exchange_wait(lands, srcs, sems, after, *, name):
    n = len(lands)
    bufs = list(lands) + list(srcs)
    nb = len(bufs)
    send_sems, recv_sems = sems

    def body(*refs):
        lnd, src, send, recv = refs[:n], refs[n:nb], refs[nb], refs[nb + 1]
        x, y, c = _place()
        for k in range(n):
            for j, peer in enumerate(_other_chips(x, y)):
                slot = 2 * peer[0] + peer[1]
                if src:
                    copy = _exchange_copy(src[k].at[slot], lnd[k].at[slot], send, recv, k, j, peer, c)
                else:
                    copy = _exchange_copy(lnd[k].at[slot, c], lnd[k].at[slot, c], send, recv, k, j, peer, c)
                copy.wait_send()
                copy.wait_recv()

    res = pl.pallas_call(
        body, name=name, out_shape=[pltpu.HBM(a.shape, a.dtype) for a in bufs],
        in_specs=[_HBM] * nb + [_SEM, _SEM, _ANY], out_specs=[_HBM] * nb,
        input_output_aliases={i: i for i in range(nb)},
        compiler_params=pltpu.CompilerParams(has_side_effects=_EFFECT),
    )(*bufs, send_sems, recv_sems, after)
    return res[:n]


def _pair_fill(lands, *, name):
    n = len(lands)

    def body(*refs):
        ins, outs, send, recv = refs[:n], refs[n:2 * n], refs[2 * n], refs[2 * n + 1]
        x, y, c = _place()
        for w in range(n):
            for j, (px, py) in enumerate(_other_chips(x, y)):
                slot = 2 * px + py
                pltpu.make_async_remote_copy(
                    src_ref=ins[w].at[slot, c], dst_ref=outs[w].at[slot, c], send_sem=send.at[3 * w + j],
                    recv_sem=recv.at[3 * w + j], device_id=(x, y, 1 - c), device_id_type=MESH).start()
        for w in range(n):
            for j, (px, py) in enumerate(_other_chips(x, y)):
                slot = 2 * px + py
                arrival = pltpu.make_async_remote_copy(
                    src_ref=ins[w].at[slot, c], dst_ref=outs[w].at[slot, 1 - c], send_sem=send.at[3 * w + j],
                    recv_sem=recv.at[3 * w + j], device_id=(x, y, 1 - c), device_id_type=MESH)
                arrival.wait_recv()
                arrival.wait_send()

    return pl.pallas_call(
        body, name=name, out_shape=[jax.ShapeDtypeStruct(a.shape, a.dtype) for a in lands],
        in_specs=[_ANY] * n, out_specs=[_ANY] * n, input_output_aliases={i: i for i in range(n)},
        scratch_shapes=[pltpu.SemaphoreType.DMA((3 * n,)), pltpu.SemaphoreType.DMA((3 * n,))],
    )(*lands)


def _pair_copy(src, dst, send, recv, w, j, sibling):
    return pltpu.make_async_remote_copy(
        src_ref=src, dst_ref=dst, send_sem=send.at[3 * w + j], recv_sem=recv.at[3 * w + j],
        device_id=sibling, device_id_type=MESH)


def _pair_start(lands, *, name):
    n = len(lands)

    def body(*refs):
        bufs, send, recv, token = refs[:n], refs[n], refs[n + 1], refs[2 * n + 2]
        x, y, c = _place()
        for w in range(n):
            for j, (px, py) in enumerate(_other_chips(x, y)):
                half = bufs[w].at[2 * px + py, c]
                _pair_copy(half, half, send, recv, w, j, (x, y, 1 - c)).start()
        token[...] = jnp.zeros_like(token)

    res = pl.pallas_call(
        body, name=name,
        out_shape=[pltpu.SemaphoreType.DMA((3 * n,))] * 2 + [pltpu.HBM(a.shape, a.dtype) for a in lands] + [_TOKEN],
        in_specs=[_HBM] * n, out_specs=[_SEM, _SEM] + [_HBM] * n + [pl.BlockSpec(memory_space=pltpu.VMEM)],
        input_output_aliases={i: 2 + i for i in range(n)},
        compiler_params=pltpu.CompilerParams(has_side_effects=_EFFECT),
    )(*[_hbm(a) for a in lands])
    return (res[0], res[1]), res[2:2 + n], res[-1]


def _pair_wait(lands, sems, after, *, name):
    n = len(lands)

    def body(*refs):
        bufs, send, recv = refs[:n], refs[n], refs[n + 1]
        x, y, c = _place()
        for w in range(n):
            for j, (px, py) in enumerate(_other_chips(x, y)):
                slot = 2 * px + py
                copy = _pair_copy(bufs[w].at[slot, c], bufs[w].at[slot, 1 - c], send, recv, w, j, (x, y, 1 - c))
                copy.wait_send()
                copy.wait_recv()

    return pl.pallas_call(
        body, name=name, out_shape=[pltpu.HBM(a.shape, a.dtype) for a in lands],
        in_specs=[_HBM] * n + [_SEM, _SEM, _ANY], out_specs=[_HBM] * n,
        input_output_aliases={i: i for i in range(n)},
        compiler_params=pltpu.CompilerParams(has_side_effects=_EFFECT),
    )(*lands, *sems, after)


def _sum_partials(land, *, name, tr=256):
    _, R, C = land.shape
    tr = min(tr, R)

    def body(l_ref, o_ref):
        acc = l_ref[0].astype(F32)
        for k in range(1, N_CHIPS):
            acc = acc + l_ref[k].astype(F32)
        o_ref[...] = acc

    return pl.pallas_call(
        body, name=name, out_shape=jax.ShapeDtypeStruct((R, C), F32), grid=(R // tr,),
        in_specs=[pl.BlockSpec((N_CHIPS, tr, C), lambda i: (0, i, 0))], out_specs=_rows(tr, C),
        compiler_params=_cparams("parallel"))(land)


def _swap_with_sibling(sums, *, name):
    n = len(sums)

    def body(*refs):
        ins, outs = refs[:n], refs[n:2 * n]
        send_sems, recv_sems = refs[2 * n:]
        x, y, c = _place()
        copies = [pltpu.make_async_remote_copy(
            src_ref=ins[w], dst_ref=outs[w], send_sem=send_sems.at[w], recv_sem=recv_sems.at[w],
            device_id=(x, y, 1 - c), device_id_type=MESH) for w in range(n)]
        for cp in copies:
            cp.start()
        for cp in copies:
            cp.wait_recv()
            cp.wait_send()

    return pl.pallas_call(
        body, name=name,
        out_shape=[jax.ShapeDtypeStruct(s.shape, s.dtype) for s in sums],
        in_specs=[_ANY] * n, out_specs=[_ANY] * n,
        scratch_shapes=[pltpu.SemaphoreType.DMA((n,)), pltpu.SemaphoreType.DMA((n,))],
    )(*sums)


def _swap_start(sums, *, name):
    n = len(sums)
    bufs = list(sums) + [lax.empty(s.shape, s.dtype) for s in sums]

    def body(*refs):
        src, lnd, send, recv, token = refs[:n], refs[n:2 * n], refs[2 * n], refs[2 * n + 1], refs[4 * n + 2]
        x, y, c = _place()
        for w in range(n):
            pltpu.make_async_remote_copy(
                src_ref=src[w], dst_ref=lnd[w], send_sem=send.at[w], recv_sem=recv.at[w],
                device_id=(x, y, 1 - c), device_id_type=MESH).start()
        token[...] = jnp.zeros_like(token)

    res = pl.pallas_call(
        body, name=name,
        out_shape=[pltpu.SemaphoreType.DMA((n,))] * 2 + [pltpu.HBM(a.shape, a.dtype) for a in bufs] + [_TOKEN],
        in_specs=[_HBM] * (2 * n),
        out_specs=[_SEM, _SEM] + [_HBM] * (2 * n) + [pl.BlockSpec(memory_space=pltpu.VMEM)],
        input_output_aliases={i: 2 + i for i in range(2 * n)},
        compiler_params=pltpu.CompilerParams(has_side_effects=_EFFECT),
    )(*[_hbm(a) for a in bufs])
    return (res[0], res[1]), res[2:2 + n], res[2 + n:2 + 2 * n], res[-1]


def _swap_wait(sums, lands, sems, after, *, name):
    n = len(sums)

    def body(*refs):
        src, lnd, send, recv = refs[:n], refs[n:2 * n], refs[2 * n], refs[2 * n + 1]
        x, y, c = _place()
        for w in range(n):
            copy = pltpu.make_async_remote_copy(
                src_ref=src[w], dst_ref=lnd[w], send_sem=send.at[w], recv_sem=recv.at[w],
                device_id=(x, y, 1 - c), device_id_type=MESH)
            copy.wait_send()
            copy.wait_recv()

    bufs = list(sums) + list(lands)
    res = pl.pallas_call(
        body, name=name, out_shape=[pltpu.HBM(a.shape, a.dtype) for a in bufs],
        in_specs=[_HBM] * (2 * n) + [_SEM, _SEM, _ANY], out_specs=[_HBM] * (2 * n),
        input_output_aliases={i: i for i in range(2 * n)},
        compiler_params=pltpu.CompilerParams(has_side_effects=_EFFECT),
    )(*bufs, *sems, after)
    return res[:n], res[n:]


def _adamw_math(w, g, m, v):
    m = ADAM_B1 * m + (1.0 - ADAM_B1) * g
    v = ADAM_B2 * v + (1.0 - ADAM_B2) * (g * g)
    m_hat = m / (1.0 - ADAM_B1 ** ADAM_STEP)
    v_hat = v / (1.0 - ADAM_B2 ** ADAM_STEP)
    delta = -ADAM_LR * (m_hat / (jnp.sqrt(v_hat) + ADAM_EPS) + ADAM_WD * w)
    return delta, m, v


def _adamw_pair(mine, theirs, w, m, v, *, name, tr=128):
    R, C = w.shape
    tr = min(tr, R)

    def body(a_ref, b_ref, w_ref, m_ref, v_ref, g_ref, d_ref, nm_ref, nv_ref):
        g = a_ref[...] + b_ref[...]
        g_ref[...] = g
        d_ref[...], nm_ref[...], nv_ref[...] = _adamw_math(w_ref[...], g, m_ref[...], v_ref[...])

    shape = jax.ShapeDtypeStruct((R, C), F32)
    return pl.pallas_call(
        body, name=name, out_shape=(shape,) * 4, grid=(R // tr,),
        in_specs=[_rows(tr, C)] * 5, out_specs=(_rows(tr, C),) * 4,
        compiler_params=_cparams("parallel"))(mine, theirs, w, m, v)


def _all_reduce_small(packed):
    R = packed.shape[0]
    half = R // 2

    def body(x_ref, g_ref, sib_ref, pair_ref, land_ref, send_sems, recv_sems):
        x, y, c = _place()
        me = 2 * x + y
        sibling = (x, y, 1 - c)

        swap = pltpu.make_async_remote_copy(
            src_ref=x_ref, dst_ref=sib_ref, send_sem=send_sems.at[0], recv_sem=recv_sems.at[0],
            device_id=sibling, device_id_type=MESH)
        swap.start()
        swap.wait()
        mine, theirs = x_ref[...], sib_ref[...]
        south = c == 0
        pair_ref[...] = jnp.where(south, mine, theirs) + jnp.where(south, theirs, mine)

        land_ref[me] = pair_ref[c]
        for j, (px, py) in enumerate(_other_chips(x, y)):
            pltpu.make_async_remote_copy(
                src_ref=pair_ref.at[c], dst_ref=land_ref.at[me], send_sem=send_sems.at[1 + j],
                recv_sem=recv_sems.at[1 + j], device_id=(px, py, c), device_id_type=MESH).start()
        for j, (px, py) in enumerate(_other_chips(x, y)):
            arrival = pltpu.make_async_remote_copy(
                src_ref=pair_ref.at[c], dst_ref=land_ref.at[2 * px + py], send_sem=send_sems.at[1 + j],
                recv_sem=recv_sems.at[1 + j], device_id=(px, py, c), device_id_type=MESH)
            arrival.wait_recv()
            arrival.wait_send()
        total = land_ref[0]
        for k in range(1, N_CHIPS):
            total = total + land_ref[k]
        g_ref[c] = total

        give = pltpu.make_async_remote_copy(
            src_ref=g_ref.at[c], dst_ref=g_ref.at[c], send_sem=send_sems.at[4], recv_sem=recv_sems.at[4],
            device_id=sibling, device_id_type=MESH)
        give.start()
        take = pltpu.make_async_remote_copy(
            src_ref=g_ref.at[c], dst_ref=g_ref.at[1 - c], send_sem=send_sems.at[4], recv_sem=recv_sems.at[4],
            device_id=sibling, device_id_type=MESH)
        take.wait_recv()
        give.wait_send()

    vm = pl.BlockSpec(memory_space=pltpu.VMEM)
    return pl.pallas_call(
        body, name="all_reduce_small", out_shape=jax.ShapeDtypeStruct((2, half, LANES), F32),
        in_specs=[vm], out_specs=vm,
        scratch_shapes=[pltpu.VMEM((2, half, LANES), F32), pltpu.VMEM((2, half, LANES), F32),
                        pltpu.VMEM((N_CHIPS, half, LANES), F32),
                        pltpu.SemaphoreType.DMA((5,)), pltpu.SemaphoreType.DMA((5,))],
        compiler_params=pltpu.CompilerParams(vmem_limit_bytes=VMEM_LIMIT_BYTES),
    )(packed.reshape(2, half, LANES)).reshape(R, LANES)


def _adamw_small(g, w, m, v):
    R = g.shape[0]
    tr = PACK_ROWS

    def body(g_ref, w_ref, m_ref, v_ref, d_ref, nm_ref, nv_ref):
        d_ref[...], nm_ref[...], nv_ref[...] = _adamw_math(w_ref[...], g_ref[...], m_ref[...], v_ref[...])

    shape = jax.ShapeDtypeStruct((R, LANES), F32)
    return pl.pallas_call(
        body, name="adamw_small", out_shape=(shape,) * 3, grid=(R // tr,),
        in_specs=[_rows(tr, LANES)] * 4, out_specs=(_rows(tr, LANES),) * 3,
        compiler_params=_cparams("parallel"))(g, w, m, v)


def _pack(arrays):
    parts, layout = [], []
    for a in arrays:
        n = a.size
        rows = -(-n // (8 * LANES)) * 8
        flat = jnp.pad(a.reshape(-1).astype(F32), (0, rows * LANES - n))
        parts.append(flat.reshape(rows, LANES))
        layout.append((rows, n, a.shape))
    total = sum(r for r, _, _ in layout)
    parts.append(jnp.zeros((-total % PACK_ROWS, LANES), F32))
    return jnp.concatenate(parts, axis=0), layout


def _unpack(buf, layout):
    out, r0 = [], 0
    for rows, n, shape in layout:
        out.append(buf[r0:r0 + rows].reshape(-1)[:n].reshape(shape))
        r0 += rows
    return out


SMALL = ("mix_norm_pre", "lam_re", "lam_im", "log_dt", "ssm_b_re", "ssm_b_im", "ssm_c_re", "ssm_c_im",
         "ssm_d", "b_glu", "attn_out_norm", "ssm_out_norm", "mix_norm_post", "mlp_norm_pre",
         "mlp_norm_post", "ple_norm_pre", "ple_norm_post")
BIG = ("w_in", "w_glu", "w_out", "w_up", "w_down", "w_ple_gate", "w_ple_proj")
WEIGHTS = ("mix_norm_pre", "w_in", "lam_re", "lam_im", "log_dt", "ssm_b_re", "ssm_b_im", "ssm_c_re",
           "ssm_c_im", "ssm_d", "w_glu", "b_glu", "attn_out_norm", "ssm_out_norm", "w_out",
           "mix_norm_post", "mlp_norm_pre", "w_up", "w_down", "mlp_norm_post", "ple_norm_pre",
           "w_ple_gate", "w_ple_proj", "ple_norm_post")


def kernel(x, p, mix_norm_pre, w_in, lam_re, lam_im, log_dt, ssm_b_re, ssm_b_im, ssm_c_re, ssm_c_im, ssm_d, w_glu, b_glu, attn_out_norm, ssm_out_norm, w_out, mix_norm_post, mlp_norm_pre, w_up, w_down, mlp_norm_post, ple_norm_pre, w_ple_gate, w_ple_proj, ple_norm_post, loss_target, m_mix_norm_pre, m_w_in, m_lam_re, m_lam_im, m_log_dt, m_ssm_b_re, m_ssm_b_im, m_ssm_c_re, m_ssm_c_im, m_ssm_d, m_w_glu, m_b_glu, m_attn_out_norm, m_ssm_out_norm, m_w_out, m_mix_norm_post, m_mlp_norm_pre, m_w_up, m_w_down, m_mlp_norm_post, m_ple_norm_pre, m_w_ple_gate, m_w_ple_proj, m_ple_norm_post, v_mix_norm_pre, v_w_in, v_lam_re, v_lam_im, v_log_dt, v_ssm_b_re, v_ssm_b_im, v_ssm_c_re, v_ssm_c_im, v_ssm_d, v_w_glu, v_b_glu, v_attn_out_norm, v_ssm_out_norm, v_w_out, v_mix_norm_post, v_mlp_norm_pre, v_w_up, v_w_down, v_mlp_norm_post, v_ple_norm_pre, v_w_ple_gate, v_w_ple_proj, v_ple_norm_post):
    args = dict(locals())
    W = {n: args[n][0] for n in WEIGHTS}
    Mo = {n: args["m_" + n][0] for n in WEIGHTS}
    Vo = {n: args["v_" + n][0] for n in WEIGHTS}
    xs, ps, tgt = x[0], p[0, 0], loss_target[0]
    S, D = xs.shape
    SW = W["ssm_d"].shape[0]
    AW = W["attn_out_norm"].shape[0]
    heads = AW // HEAD_DIM
    G = SW // SSM_C
    nbk = SW // LANES
    assert W["w_in"].shape[1] * N_CHIPS == 3 * AW + SW and AW == SW

    row = lambda a: a.reshape(1, -1)

    ag_groups = (("w_in",), ("w_glu", "w_out"), ("w_up",), ("w_down", "w_ple_gate", "w_ple_proj"))
    ag_names = [n for g in ag_groups for n in g]
    def in_halves(a):
        return a.reshape(N_CHIPS, 2, a.shape[1] // 2, a.shape[2])

    def placed(n, after=None):
        return in_halves(_place_own(W[n], gather=True, name="ag_place_" + n, after=after))

    first_sems, first_land, _, first_token = _exchange_start([placed("w_in")], [], [[0]], name="ag_start_first")
    rest_sems, rest_land, _, ag_token = _exchange_start(
        [placed(n, first_token) for n in ag_names[1:]], [],
        [[ag_names.index(n) - 1 for n in g] for g in ag_groups[1:]], name="ag_start")
    ag_sems, ag_land = first_sems + rest_sems, list(first_land) + list(rest_land)

    def fetched(gi, after):
        return _exchange_wait([ag_land[ag_names.index(n)] for n in ag_groups[gi]], [], ag_sems[gi], after,
                              name=f"ag_wait_{gi}")

    def whole(gis, bufs):
        names = [n for gi in gis for n in ag_groups[gi]]
        return {n: a.reshape(N_CHIPS, -1, a.shape[-1]) for n, a in zip(names, bufs)}

    lr_e = W["lam_re"].reshape(nbk, 1, STATE_LANES)
    li_e = W["lam_im"].reshape(nbk, 1, STATE_LANES)
    ldt_e = jnp.repeat(W["log_dt"], SSM_P).reshape(nbk, 1, STATE_LANES)
    bre_e, bim_e = _expand_b(W["ssm_b_re"]), _expand_b(W["ssm_b_im"])
    cre_e, cim_e = _expand_c(W["ssm_c_re"]), _expand_c(W["ssm_c_im"])
    d_row = row(W["ssm_d"])

    hn1 = _norm_cast(xs, row(W["mix_norm_pre"]) + ag_token[0, 0], name="norm_in")
    w_in_f = whole([0], _pair_fill(fetched(0, hn1), name="ag_pair_0"))["w_in"]
    qkv_b = _proj_qkv(hn1, w_in_f)
    outs, lses = zip(*[_attn_fwd(qb, d, heads) for d, qb in zip(DILATIONS, qkv_b)])
    pair_a_sems, pair_a, pair_a_token = _pair_start(fetched(1, outs[-1]), name="ag_pair_start_a")
    u = _matmul(hn1, w_in_f, name="proj_u", b_shards=N_CHIPS, b_cols=(3 * AW, SW), after=pair_a_token)
    y1, y2b, st_r, st_i, ends_r, ends_i = _ssm_fwd(u, lr_e, li_e, ldt_e, bre_e, bim_e, cre_e, cim_e, d_row)
    pair_b_sems, pair_b, pair_b_token = _pair_start(fetched(2, y2b), name="ag_pair_start_b")
    full = whole([1], _pair_wait(pair_a, pair_a_sems, y2b, name="ag_pair_wait_a"))
    w_glu_f = full["w_glu"].reshape(SW, SW)
    w_out_f = full["w_out"].reshape(AW + SW, D)
    z = _matmul(y2b, w_glu_f, name="glu_z", after=pair_b_token)
    attn, lse_b, mixed = _mix_fwd(outs, lses, y1, z, row(W["b_glu"]), row(W["attn_out_norm"]), row(W["ssm_out_norm"]))
    mo = _matmul(mixed, w_out_f, name="mix_out")
    h1, hn2 = _res_norm(xs, mo, row(W["mix_norm_post"]), row(W["mlp_norm_pre"]), name="res_mix")
    w_up_f = whole([2], _pair_wait(pair_b, pair_b_sems, hn2, name="ag_pair_wait_b"))["w_up"]
    up, act = _matmul(hn2, w_up_f, name="mlp_up", b_shards=N_CHIPS, relu2=True)
    full = whole([3], _pair_fill(fetched(3, act), name="ag_pair_3"))
    w_down_f = full["w_down"].reshape(-1, D)
    w_pg_f = full["w_ple_gate"].reshape(D, D)
    w_pp_f = full["w_ple_proj"]
    ff = _matmul(act, w_down_f, name="mlp_down")
    h2, hn3 = _res_norm(h1, ff, row(W["mlp_norm_post"]), row(W["ple_norm_pre"]), name="res_mlp")
    gl = _matmul(hn3, w_pg_f, name="ple_gate")
    e = _matmul(ps.astype(BF16), w_pp_f, name="ple_proj", b_shards=N_CHIPS)

    dh3, dgl, de, loss_part, dg_ple_post = _final(h2, gl, e, row(W["ple_norm_post"]), tgt)
    gW = {}
    out_g, out_d, out_m, out_v = {}, {}, {}, {}

    def scatter_start(names, tag):
        parts = [gW[n] if gW[n].ndim == 3 else gW[n].reshape((N_CHIPS, -1, gW[n].shape[1])) for n in names]
        sems, land, src, token = _exchange_start(
            [_place_own(part, gather=False, name="rs_place_" + n) for n, part in zip(names, parts)], parts,
            [list(range(len(names)))], name=f"rs_start_{tag}")
        return (names, sems[0], land, src), token

    def scatter_sums(batches, after):
        names, sums = [], []
        for tag, (batch_names, sems, land, src) in batches:
            landed = _exchange_wait(land, src, sems, after, name=f"rs_wait_{tag}")
            names += batch_names
            sums += [_sum_partials(l, name="sum_" + n) for n, l in zip(batch_names, landed)]
        return names, sums

    def apply(names, sums, theirs):
        for n, a, b in zip(names, sums, theirs):
            out_g[n], out_d[n], out_m[n], out_v[n] = _adamw_pair(a, b, W[n], Mo[n], Vo[n], name="adamw_" + n)

    def swap_begin(batches, after, tag):
        names, sums = scatter_sums(batches, after)
        sems, sums, lands, token = _swap_start(sums, name=f"swap_start_{tag}")
        return (names, sems, sums, lands), token

    def swap_end(swap, after, tag):
        names, sems, sums, lands = swap
        sums, theirs = _swap_wait(sums, lands, sems, after, name=f"swap_wait_{tag}")
        apply(names, sums, theirs)

    def scatter_finish(batch, after, tag):
        names, sums = scatter_sums([(tag, batch)], after)
        apply(names, sums, _swap_with_sibling(sums, name=f"swap_{tag}"))

    gW["w_ple_proj"] = _matmul(ps.astype(BF16), de, name="d_w_ple_proj", ta=True, out_dtype=BF16, out_shards=N_CHIPS)
    gW["w_ple_gate"] = _matmul(hn3, dgl, name="d_w_ple_gate", ta=True, out_dtype=BF16)
    dhn3 = _matmul(dgl, w_pg_f, name="d_hn3", tb=True)
    dh2, dff, dg_ple_pre, dg_mlp_post = _bwd_res_norm(
        dh3, dhn3, h2, row(W["ple_norm_pre"]), ff, row(W["mlp_norm_post"]), name="bwd_res_mlp")
    gW["w_down"] = _matmul(act, dff, name="d_w_down", ta=True, out_dtype=BF16)
    batch1, token1 = scatter_start(("w_ple_proj", "w_ple_gate", "w_down"), 1)
    dup = _matmul(dff, w_down_f, name="d_up", tb=True, after=token1, relu2_of=up, out_dtype=BF16)
    gW["w_up"] = _matmul(hn2, dup, name="d_w_up", ta=True, out_dtype=BF16, out_shards=N_CHIPS)
    batch2, token2 = scatter_start(("w_up",), 2)
    dhn2 = _matmul(dup, w_up_f, name="d_hn2", tb=True, b_shards=N_CHIPS, after=token2)
    dh1, dmo, dg_mlp_pre, dg_mix_post = _bwd_res_norm(
        dh2, dhn2, h1, row(W["mlp_norm_pre"]), mo, row(W["mix_norm_post"]), name="bwd_res_mix")
    gW["w_out"] = _matmul(mixed, dmo, name="d_w_out", ta=True, out_dtype=BF16)
    dmixed = _matmul(dmo, w_out_f, name="d_mixed", tb=True)
    dattn_b, dd_b, dz, dy2a, dg_attn, dg_ssm, db_glu = _mix_bwd(
        dmixed, attn, y1, z, row(W["b_glu"]), row(W["attn_out_norm"]), row(W["ssm_out_norm"]))
    gW["w_glu"] = _matmul(y2b, dz, name="d_w_glu", ta=True, out_dtype=BF16)
    batch3, token3 = scatter_start(("w_out", "w_glu"), 3)
    dy2b = _matmul(dz, w_glu_f, name="d_y2", tb=True, after=token3)
    du, dar8, dai8, dcr_e, dci_e, dbr_e, dbi_e, dd8 = _ssm_bwd(
        u, y1, dy2a, dy2b, st_r, st_i, ends_r, ends_i, lr_e, li_e, ldt_e, bre_e, bim_e, cre_e, cim_e, d_row)
    swap_a, token_a = swap_begin([(1, batch1)], du, "a")
    dlr_e, dli_e, dldt_e, dbre_e, dbim_e = _ssm_param_bwd(dar8, dai8, dbr_e, dbi_e, lr_e, li_e, ldt_e, bre_e, bim_e)

    dqs, dks, dvs = zip(*[_attn_bwd(qb, da, l, dd_, d, heads, token_a)
                          for d, qb, da, l, dd_ in zip(DILATIONS, qkv_b, dattn_b, lse_b, dd_b)])
    dproj = _dproj_join(dqs, dks, dvs, du)
    swap_end(swap_a, dproj, "a")
    swap_b, token_b = swap_begin([(2, batch2), (3, batch3)], dproj, "b")
    gW["w_in"] = _matmul(hn1, dproj, name="d_w_in", ta=True, out_dtype=BF16, out_shards=N_CHIPS, after=token_b)
    batch4, token4 = scatter_start(("w_in",), 4)
    dhn1 = _matmul(dproj, w_in_f, name="d_hn1", tb=True, b_shards=N_CHIPS, after=token4)
    grad_x, dg_mix_pre = _bwd_first(dh1, dhn1, xs, row(W["mix_norm_pre"]))
    swap_end(swap_b, grad_x, "b")
    scatter_finish(batch4, grad_x, 4)

    small_g = {
        "mix_norm_pre": dg_mix_pre, "lam_re": dlr_e.reshape(G, SSM_P), "lam_im": dli_e.reshape(G, SSM_P),
        "log_dt": dldt_e.reshape(G, SSM_P)[:, 0], "ssm_b_re": _collapse_b(dbre_e), "ssm_b_im": _collapse_b(dbim_e),
        "ssm_c_re": _collapse_c(dcr_e), "ssm_c_im": _collapse_c(dci_e), "ssm_d": dd8.sum(axis=1).reshape(-1),
        "b_glu": db_glu, "attn_out_norm": dg_attn, "ssm_out_norm": dg_ssm, "mix_norm_post": dg_mix_post,
        "mlp_norm_pre": dg_mlp_pre, "mlp_norm_post": dg_mlp_post, "ple_norm_pre": dg_ple_pre,
        "ple_norm_post": dg_ple_post,
    }
    g_pack, layout = _pack([small_g[n].reshape(W[n].shape) for n in SMALL])
    w_pack, _ = _pack([W[n] for n in SMALL])
    m_pack, _ = _pack([Mo[n] for n in SMALL])
    v_pack, _ = _pack([Vo[n] for n in SMALL])
    g_sum = _all_reduce_small(g_pack)
    packed = (g_sum,) + tuple(_adamw_small(g_sum, w_pack, m_pack, v_pack))
    for dst, buf in zip((out_g, out_d, out_m, out_v), packed):
        dst.update(zip(SMALL, _unpack(buf, layout)))

    loss = lax.psum(loss_part[0, 0], ("x", "y", "c"))
    lead = lambda a: a[None]
    return (loss, grad_x[None],
            *[lead(out_g[n]) for n in WEIGHTS], *[lead(out_d[n]) for n in WEIGHTS],
            *[lead(out_m[n]) for n in WEIGHTS], *[lead(out_v[n]) for n in WEIGHTS])
```

```python
import functools
import math

import jax
import jax.numpy as jnp
from jax import lax
from jax.experimental import pallas as pl
from jax.experimental.pallas import tpu as pltpu

F32 = jnp.float32
BF16 = jnp.bfloat16
MESH = pl.DeviceIdType.MESH

RMS_EPS = 1e-6
NEG_INF = -1e30
HEAD_DIM = 128
BLK = 128
DILATIONS = (1, 4, 16)
ATTN_LOOKAHEAD = 3
SSM_C = 16
SSM_P = 64
LANES = 128
GROUPS_PER_BLOCK = LANES // SSM_C
STATE_LANES = GROUPS_PER_BLOCK * SSM_P
SSM_CHUNK = 512
TILE = 8
ADAM_LR, ADAM_B1, ADAM_B2, ADAM_EPS, ADAM_WD, ADAM_STEP = 1e-3, 0.9, 0.999, 1e-8, 0.01, 10
VMEM_LIMIT_BYTES = 56 * 1024 * 1024
MATMUL_VMEM_BYTES = 44 * 1024 * 1024
N_CHIPS = 4
N_DEV = 8
PACK_ROWS = 256


def _cparams(*sem):
    return pltpu.CompilerParams(dimension_semantics=sem or None, vmem_limit_bytes=VMEM_LIMIT_BYTES)


def _rows(tr, w):
    return pl.BlockSpec((tr, w), lambda i: (i, 0))


def _vec(w):
    return pl.BlockSpec((1, w), lambda i: (0, 0))


def _sigmoid(x):
    return 1.0 / (1.0 + jnp.exp(-x))


def _gelu(x):
    c = math.sqrt(2.0 / math.pi)
    return 0.5 * x * (1.0 + jnp.tanh(c * (x + 0.044715 * x * x * x)))


def _gelu_grad(x):
    c = math.sqrt(2.0 / math.pi)
    th = jnp.tanh(c * (x + 0.044715 * x * x * x))
    return 0.5 * (1.0 + th) + 0.5 * x * (1.0 - th * th) * c * (1.0 + 3.0 * 0.044715 * x * x)


def _rms(x, g):
    r = lax.rsqrt(jnp.mean(x * x, axis=-1, keepdims=True) + RMS_EPS)
    return x * r * g


def _rms_bwd(dy, x, g):
    r = lax.rsqrt(jnp.mean(x * x, axis=-1, keepdims=True) + RMS_EPS)
    n = x * r
    dn = dy * g
    dx = r * (dn - n * jnp.mean(dn * n, axis=-1, keepdims=True))
    return dx, dy * n


def _colsum(a):
    return jnp.sum(a, axis=0, keepdims=True)


def _first(i):
    return i == 0


def _matmul(a, b, *, name, ta=False, tb=False, out_dtype=F32, b_shards=1, out_shards=1, b_cols=None,
            after=None, relu2=False, relu2_of=None, tm=1024, tn=2048, tk=2048):
    if ta:
        K, M = a.shape
    else:
        M, K = a.shape
    if b_shards > 1:
        rows, cols = b.shape[1], b.shape[2] * b_shards
    else:
        rows, cols = b.shape
    N, Kb = (rows, cols) if tb else (cols, rows)
    assert K == Kb, (a.shape, b.shape, ta, tb)
    col0 = 0
    if b_cols is not None:
        assert not tb
        col0, N = b_cols
    tm, tn, tk = min(tm, M), min(tn, N), min(tk, K)
    if b_shards > 1:
        shard_cols = cols // b_shards
        if tb:
            tk = min(tk, shard_cols)
        else:
            tn = min(tn, shard_cols)
    if out_shards > 1:
        tn = min(tn, N // out_shards)

    def vmem_bytes(tn_):
        out_bytes = jnp.dtype(out_dtype).itemsize + (2 if relu2 else 0)
        return (4 * (tm * tk + tk * tn_) + 2 * tm * tn_ * out_bytes
                + (8 * tm * tn_ if relu2_of is not None else 0) + (4 * tm * tn_ if K > tk else 0))

    while vmem_bytes(tn) > MATMUL_VMEM_BYTES and tn > LANES and col0 % (tn // 2) == 0:
        tn //= 2
    assert M % tm == 0 and N % tn == 0 and K % tk == 0 and col0 % tn == 0
    nk = K // tk
    j0 = col0 // tn

    a_spec = (pl.BlockSpec((tk, tm), lambda i, j, k: (k, i)) if ta
              else pl.BlockSpec((tm, tk), lambda i, j, k: (i, k)))
    if b_shards > 1:
        if tb:
            per = shard_cols // tk
            b_spec = pl.BlockSpec((None, tn, tk), lambda i, j, k: (k // per, j, k % per))
        else:
            per = shard_cols // tn
            b_spec = pl.BlockSpec((None, tk, tn), lambda i, j, k: ((j + j0) // per, k, (j + j0) % per))
    else:
        b_spec = (pl.BlockSpec((tn, tk), lambda i, j, k: (j, k)) if tb
                  else pl.BlockSpec((tk, tn), lambda i, j, k: (k, j + j0)))
    if out_shards > 1:
        per_o = (N // out_shards) // tn
        out_shape = jax.ShapeDtypeStruct((out_shards, M, N // out_shards), out_dtype)
        out_spec = pl.BlockSpec((None, tm, tn), lambda i, j, k: (j // per_o, i, j % per_o))
    else:
        out_shape = jax.ShapeDtypeStruct((M, N), out_dtype)
        out_spec = pl.BlockSpec((tm, tn), lambda i, j, k: (i, j))
    dims = (((0 if ta else 1,), (1 if tb else 0,)), ((), ()))

    extra, extra_specs = [], []
    if relu2_of is not None:
        assert out_shards == 1 and relu2_of.shape == (M, N)
        extra.append(relu2_of)
        extra_specs.append(pl.BlockSpec((tm, tn), lambda i, j, k: (i, j)))
    if after is not None:
        extra.append(after)
        extra_specs.append(pl.BlockSpec(after.shape, lambda i, j, k: (0, 0)))
    n_in = 2 + len(extra)
    if relu2:
        assert out_shards == 1
        out_shape = (out_shape, jax.ShapeDtypeStruct((M, N), BF16))
        out_spec = (out_spec, out_spec)

    def finish(acc, refs):
        o_ref = refs[n_in]
        if relu2_of is not None:
            acc = acc * (2.0 * jnp.maximum(refs[2][...], 0.0))
        o_ref[...] = acc.astype(o_ref.dtype)
        if relu2:
            r = jnp.maximum(acc, 0.0)
            refs[n_in + 1][...] = (r * r).astype(BF16)

    def body(*refs):
        prod = lax.dot_general(refs[0][...], refs[1][...], dims, preferred_element_type=F32)
        if nk == 1:
            finish(prod, refs)
            return
        acc_ref = refs[-1]
        k = pl.program_id(2)

        @pl.when(k == 0)
        def _():
            acc_ref[...] = prod

        @pl.when(k > 0)
        def _():
            acc_ref[...] += prod

        @pl.when(k == nk - 1)
        def _():
            finish(acc_ref[...], refs)

    return pl.pallas_call(
        body, name=name, out_shape=out_shape, grid=(M // tm, N // tn, nk),
        in_specs=[a_spec, b_spec] + extra_specs, out_specs=out_spec,
        scratch_shapes=[pltpu.VMEM((tm, tn), F32)] if nk > 1 else [],
        compiler_params=_cparams("parallel", "parallel", "arbitrary"),
    )(a, b, *extra)


def _norm_cast(x, g, *, name, tr=256):
    S, D = x.shape
    tr = min(tr, S)

    def body(x_ref, g_ref, o_ref):
        o_ref[...] = _rms(x_ref[...], g_ref[...]).astype(BF16)

    return pl.pallas_call(
        body, name=name, out_shape=jax.ShapeDtypeStruct((S, D), BF16), grid=(S // tr,),
        in_specs=[_rows(tr, D), _vec(D)], out_specs=_rows(tr, D),
        compiler_params=_cparams("parallel"))(x, g)


def _res_norm(res, y, g_post, g_next, *, name, tr=256):
    S, D = res.shape
    tr = min(tr, S)

    def body(res_ref, y_ref, gp_ref, gn_ref, h_ref, hn_ref):
        h = res_ref[...] + _rms(y_ref[...], gp_ref[...])
        h_ref[...] = h
        hn_ref[...] = _rms(h, gn_ref[...]).astype(BF16)

    return pl.pallas_call(
        body, name=name,
        out_shape=(jax.ShapeDtypeStruct((S, D), F32), jax.ShapeDtypeStruct((S, D), BF16)),
        grid=(S // tr,), in_specs=[_rows(tr, D), _rows(tr, D), _vec(D), _vec(D)],
        out_specs=(_rows(tr, D), _rows(tr, D)), compiler_params=_cparams("parallel"))(res, y, g_post, g_next)


def _residue_spec(tr, d, w):
    return pl.BlockSpec((tr // d, d * w), lambda i: (i, 0))


def _residue_shape(S, d, w, dtype):
    return jax.ShapeDtypeStruct((S // d, d * w), dtype)


def _residue_scratch(rows, w):
    return pltpu.VMEM((w // LANES, rows, LANES), F32)


def _fill_strips(scr, val):
    for s in range(scr.shape[0]):
        scr[s] = val[:, s * LANES:(s + 1) * LANES]


def _strips_to_residues(scr, o_ref, d):
    strips, rows, _ = scr.shape
    for r in range(d):
        for s in range(strips):
            col = (r * strips + s) * LANES
            o_ref[:, col:col + LANES] = scr[s, pl.ds(r, rows // d, stride=d), :].astype(o_ref.dtype)


def _to_residues(scr, val, o_ref, d):
    if d == 1:
        o_ref[...] = val.astype(o_ref.dtype)
        return
    _fill_strips(scr, val)
    _strips_to_residues(scr, o_ref, d)


def _from_residues(scr, in_ref, d):
    if d == 1:
        return in_ref[...].astype(F32)
    strips, rows, _ = scr.shape
    for r in range(d):
        for s in range(strips):
            col = (r * strips + s) * LANES
            scr[s, pl.ds(r, rows // d, stride=d), :] = in_ref[:, col:col + LANES].astype(F32)
    return jnp.concatenate([scr[s] for s in range(strips)], axis=1)


def _spread_heads(packed, heads, width=HEAD_DIM):
    per = LANES // heads
    return jnp.concatenate([jnp.broadcast_to(packed[:, h * per:h * per + 1], (packed.shape[0], width))
                            for h in range(heads)], axis=1)


def _mix_fwd(os, ls, y1, z, b_glu, g_attn, g_ssm, *, tr=128):
    S, SW = y1.shape
    AW = os[0].shape[1] // DILATIONS[0]
    heads = AW // HEAD_DIM
    tr = min(tr, S)
    nd = len(DILATIONS)

    def body(*refs):
        o_refs, l_refs = refs[:nd], refs[nd:2 * nd]
        y_ref, z_ref, b_ref, ga_ref, gs_ref, attn_ref = refs[2 * nd:2 * nd + 6]
        lse_refs = refs[2 * nd + 6:3 * nd + 6]
        mixed_ref, scr, scr_p = refs[3 * nd + 6:]
        ls_ = [_from_residues(scr_p, l_refs[n], d) for n, d in enumerate(DILATIONS)]
        m = functools.reduce(jnp.maximum, ls_)
        es = [jnp.exp(l - m) for l in ls_]
        tot = functools.reduce(jnp.add, es)
        attn = functools.reduce(jnp.add, [_spread_heads(e / tot, heads) * _from_residues(scr, o_refs[n], d)
                                          for n, (e, d) in enumerate(zip(es, DILATIONS))])
        attn_ref[...] = attn
        lse = m + jnp.log(tot)
        for n, d in enumerate(DILATIONS):
            _to_residues(scr_p, lse, lse_refs[n], d)
        ssm = _gelu(y_ref[...]) * _sigmoid(z_ref[...] + b_ref[...])
        mixed_ref[:, :AW] = _rms(attn, ga_ref[...]).astype(BF16)
        mixed_ref[:, AW:] = _rms(ssm, gs_ref[...]).astype(BF16)

    res_o = [_residue_spec(tr, d, AW) for d in DILATIONS]
    res_l = [_residue_spec(tr, d, LANES) for d in DILATIONS]
    res = pl.pallas_call(
        body, name="mix_fwd",
        out_shape=([jax.ShapeDtypeStruct((S, AW), F32)] + [_residue_shape(S, d, LANES, F32) for d in DILATIONS]
                   + [jax.ShapeDtypeStruct((S, AW + SW), BF16)]),
        grid=(S // tr,),
        in_specs=res_o + res_l + [_rows(tr, SW), _rows(tr, SW), _vec(SW), _vec(AW), _vec(SW)],
        out_specs=[_rows(tr, AW)] + res_l + [_rows(tr, AW + SW)],
        scratch_shapes=[_residue_scratch(tr, AW), _residue_scratch(tr, LANES)],
        compiler_params=_cparams("parallel"))(*os, *ls, y1, z, b_glu, g_attn, g_ssm)
    return res[0], res[1:1 + nd], res[1 + nd]


def _final(h2, gl, e, g_post, target, *, tr=128):
    S, D = h2.shape
    tr = min(tr, S)

    def body(h_ref, gl_ref, e_ref, g_ref, t_ref, dh_ref, dgl_ref, de_ref, loss_ref, dg_ref):
        i = pl.program_id(0)
        gate = _sigmoid(gl_ref[...])
        e_ = e_ref[...]
        ge = gate * e_
        g = g_ref[...]
        diff = h_ref[...] + _rms(ge, g) - t_ref[...]
        dh = diff * (1.0 / D)
        dh_ref[...] = dh
        dge, dgrow = _rms_bwd(dh, ge, g)
        dgl_ref[...] = (dge * e_ * gate * (1.0 - gate)).astype(BF16)
        de_ref[...] = (dge * gate).astype(BF16)
        part = _colsum(0.5 * jnp.mean(diff * diff, axis=-1, keepdims=True))

        @pl.when(_first(i))
        def _():
            loss_ref[...] = jnp.zeros_like(loss_ref)
            dg_ref[...] = jnp.zeros_like(dg_ref)

        loss_ref[...] += part + jnp.zeros((1, LANES), F32)
        dg_ref[...] += _colsum(dgrow)

    return pl.pallas_call(
        body, name="final_fwd_bwd",
        out_shape=(jax.ShapeDtypeStruct((S, D), F32), jax.ShapeDtypeStruct((S, D), BF16),
                   jax.ShapeDtypeStruct((S, D), BF16), jax.ShapeDtypeStruct((1, LANES), F32),
                   jax.ShapeDtypeStruct((1, D), F32)),
        grid=(S // tr,),
        in_specs=[_rows(tr, D), _rows(tr, D), _rows(tr, D), _vec(D), _rows(tr, D)],
        out_specs=(_rows(tr, D), _rows(tr, D), _rows(tr, D), _vec(LANES), _vec(D)),
        compiler_params=_cparams("arbitrary"))(h2, gl, e, g_post, target)


def _bwd_res_norm(dh_out, dhn, h, g_next, y, g_post, *, name, tr=128):
    S, D = h.shape
    tr = min(tr, S)

    def body(dho_ref, dhn_ref, h_ref, gn_ref, y_ref, gp_ref, dh_ref, dy_ref, dgn_ref, dgp_ref):
        i = pl.program_id(0)
        dx, dgn_rows = _rms_bwd(dhn_ref[...], h_ref[...], gn_ref[...])
        dh = dho_ref[...] + dx
        dh_ref[...] = dh
        dy, dgp_rows = _rms_bwd(dh, y_ref[...], gp_ref[...])
        dy_ref[...] = dy.astype(BF16)

        @pl.when(_first(i))
        def _():
            dgn_ref[...] = jnp.zeros_like(dgn_ref)
            dgp_ref[...] = jnp.zeros_like(dgp_ref)

        dgn_ref[...] += _colsum(dgn_rows)
        dgp_ref[...] += _colsum(dgp_rows)

    return pl.pallas_call(
        body, name=name,
        out_shape=(jax.ShapeDtypeStruct((S, D), F32), jax.ShapeDtypeStruct((S, D), BF16),
                   jax.ShapeDtypeStruct((1, D), F32), jax.ShapeDtypeStruct((1, D), F32)),
        grid=(S // tr,),
        in_specs=[_rows(tr, D), _rows(tr, D), _rows(tr, D), _vec(D), _rows(tr, D), _vec(D)],
        out_specs=(_rows(tr, D), _rows(tr, D), _vec(D), _vec(D)),
        compiler_params=_cparams("arbitrary"))(dh_out, dhn, h, g_next, y, g_post)


def _bwd_first(dh1, dhn1, x, g1, *, tr=256):
    S, D = x.shape
    tr = min(tr, S)

    def body(dh_ref, dhn_ref, x_ref, g_ref, dx_ref, dg_ref):
        i = pl.program_id(0)
        dx, dg_rows = _rms_bwd(dhn_ref[...], x_ref[...], g_ref[...])
        dx_ref[...] = dh_ref[...] + dx

        @pl.when(_first(i))
        def _():
            dg_ref[...] = jnp.zeros_like(dg_ref)

        dg_ref[...] += _colsum(dg_rows)

    return pl.pallas_call(
        body, name="bwd_first",
        out_shape=(jax.ShapeDtypeStruct((S, D), F32), jax.ShapeDtypeStruct((1, D), F32)),
        grid=(S // tr,), in_specs=[_rows(tr, D), _rows(tr, D), _rows(tr, D), _vec(D)],
        out_specs=(_rows(tr, D), _vec(D)), compiler_params=_cparams("arbitrary"))(dh1, dhn1, x, g1)


def _mix_bwd(dmixed, attn, y1, z, b_glu, g_attn, g_ssm, *, tr=256):
    S, AW = attn.shape
    SW = y1.shape[1]
    tr = min(tr, S)
    heads = AW // HEAD_DIM
    nd = len(DILATIONS)

    def body(*refs):
        dm_ref, a_ref, y_ref, z_ref, b_ref, ga_ref, gs_ref = refs[:7]
        da_refs, dd_refs = refs[7:7 + nd], refs[7 + nd:7 + 2 * nd]
        dz_ref, dy2_ref, dga_ref, dgs_ref, db_ref, scr, scr_p, dd_scr = refs[7 + 2 * nd:]
        i = pl.program_id(0)
        attn_ = a_ref[...]
        dattn, dga_rows = _rms_bwd(dm_ref[:, :AW], attn_, ga_ref[...])
        prod = dattn * attn_
        per = LANES // heads
        for h in range(heads):
            total = jnp.sum(prod[:, h * HEAD_DIM:(h + 1) * HEAD_DIM], axis=-1, keepdims=True)
            dd_scr[:, h * per:(h + 1) * per] = jnp.broadcast_to(total, (tr, per))
        for n, d in enumerate(DILATIONS):
            _to_residues(scr, dattn, da_refs[n], d)
            _to_residues(scr_p, dd_scr[...], dd_refs[n], d)
        y2 = _gelu(y_ref[...])
        gate = _sigmoid(z_ref[...] + b_ref[...])
        dssm, dgs_rows = _rms_bwd(dm_ref[:, AW:], y2 * gate, gs_ref[...])
        dz = dssm * y2 * gate * (1.0 - gate)
        dz_ref[...] = dz.astype(BF16)
        dy2_ref[...] = dssm * gate

        @pl.when(_first(i))
        def _():
            dga_ref[...] = jnp.zeros_like(dga_ref)
            dgs_ref[...] = jnp.zeros_like(dgs_ref)
            db_ref[...] = jnp.zeros_like(db_ref)

        dga_ref[...] += _colsum(dga_rows)
        dgs_ref[...] += _colsum(dgs_rows)
        db_ref[...] += _colsum(dz)

    res_a = [_residue_spec(tr, d, AW) for d in DILATIONS]
    res_d = [_residue_spec(tr, d, LANES) for d in DILATIONS]
    res = pl.pallas_call(
        body, name="mix_bwd",
        out_shape=([_residue_shape(S, d, AW, BF16) for d in DILATIONS]
                   + [_residue_shape(S, d, LANES, F32) for d in DILATIONS]
                   + [jax.ShapeDtypeStruct((S, SW), BF16), jax.ShapeDtypeStruct((S, SW), F32),
                      jax.ShapeDtypeStruct((1, AW), F32), jax.ShapeDtypeStruct((1, SW), F32),
                      jax.ShapeDtypeStruct((1, SW), F32)]),
        grid=(S // tr,),
        in_specs=[_rows(tr, AW + SW), _rows(tr, AW), _rows(tr, SW), _rows(tr, SW), _vec(SW), _vec(AW), _vec(SW)],
        out_specs=res_a + res_d + [_rows(tr, SW), _rows(tr, SW), _vec(AW), _vec(SW), _vec(SW)],
        scratch_shapes=[_residue_scratch(tr, AW), _residue_scratch(tr, LANES), pltpu.VMEM((tr, LANES), F32)],
        compiler_params=_cparams("arbitrary"))(dmixed, attn, y1, z, b_glu, g_attn, g_ssm)
    return (res[:nd], res[nd:2 * nd]) + tuple(res[2 * nd:])


def _attn_mask2(i):
    row = lax.broadcasted_iota(jnp.int32, (BLK, 2 * BLK), 0)
    col = lax.broadcasted_iota(jnp.int32, (BLK, 2 * BLK), 1)
    return jnp.logical_and(col >= row, jnp.logical_and(col <= row + BLK, jnp.logical_or(col >= BLK, i > 0)))


_NT = (((1,), (1,)), ((), ()))
_TN = (((0,), (0,)), ((), ()))


def _attn_in_specs(width, block_of):
    def at(part, prev):
        def index(r, i):
            blk = block_of(i)
            return (part, jnp.maximum(blk - 1, 0) if prev else blk, r)
        return pl.BlockSpec((None, BLK, width), index)
    return [at(0, False), at(1, False), at(1, True), at(2, False), at(2, True)]


def _proj_qkv(hn, w_in_f, *, tm=1024):
    S, D = hn.shape
    AW = w_in_f.shape[2]
    tm = min(tm, S)

    def body(a_ref, b_ref, *rest):
        o_refs, scr = rest[:-1], rest[-1]
        prod = jnp.dot(a_ref[...], b_ref[...], preferred_element_type=F32)
        _fill_strips(scr, prod)
        for o_ref, d in zip(o_refs, DILATIONS):
            if d == 1:
                o_ref[...] = prod.astype(BF16)
            else:
                _strips_to_residues(scr, o_ref, d)

    return pl.pallas_call(
        body, name="proj_qkv",
        out_shape=[jax.ShapeDtypeStruct((3, S // d, d * AW), BF16) for d in DILATIONS], grid=(S // tm, 3),
        in_specs=[pl.BlockSpec((tm, D), lambda i, j: (i, 0)), pl.BlockSpec((None, D, AW), lambda i, j: (j, 0, 0))],
        out_specs=[pl.BlockSpec((None, tm // d, d * AW), lambda i, j: (j, i, 0)) for d in DILATIONS],
        scratch_shapes=[_residue_scratch(tm, AW)],
        compiler_params=_cparams("parallel", "parallel"))(hn, w_in_f)


def _attn_fwd(qkv, d, heads):
    M = qkv.shape[1]
    nb = M // BLK
    width = heads * HEAD_DIM
    per = LANES // heads
    scale = 1.0 / math.sqrt(HEAD_DIM)

    def body(q_ref, kc_ref, kp_ref, vc_ref, vp_ref, o_ref, l_ref):
        mask = _attn_mask2(pl.program_id(1))
        ones = jnp.ones((2 * BLK, HEAD_DIM), BF16)

        def scores(h):
            sl = slice(h * HEAD_DIM, (h + 1) * HEAD_DIM)
            k2 = jnp.concatenate([kp_ref[:, sl], kc_ref[:, sl]], axis=0)
            return lax.dot_general(q_ref[:, sl], k2, _NT, preferred_element_type=F32)

        ahead = [scores(h) for h in range(min(ATTN_LOOKAHEAD, heads))]
        for h in range(heads):
            sl = slice(h * HEAD_DIM, (h + 1) * HEAD_DIM)
            s = jnp.where(mask, ahead.pop(0) * scale, NEG_INF)
            if h + ATTN_LOOKAHEAD < heads:
                ahead.append(scores(h + ATTN_LOOKAHEAD))
            v2 = jnp.concatenate([vp_ref[:, sl], vc_ref[:, sl]], axis=0)
            m = jnp.max(jnp.maximum(s[:, :BLK], s[:, BLK:]), axis=-1, keepdims=True)
            p = jnp.exp(s - m).astype(BF16)
            tot = jnp.dot(p, ones, preferred_element_type=F32)
            o_ref[:, sl] = jnp.dot(p, v2, preferred_element_type=F32) / tot
            l_ref[:, h * per:(h + 1) * per] = m + jnp.log(tot[:, :per])

    return pl.pallas_call(
        body, name=f"attn_fwd_d{d}",
        out_shape=(jax.ShapeDtypeStruct((M, d * width), F32), jax.ShapeDtypeStruct((M, d * LANES), F32)),
        grid=(d, nb), in_specs=_attn_in_specs(width, lambda i: i),
        out_specs=(pl.BlockSpec((BLK, width), lambda r, i: (i, r)), pl.BlockSpec((BLK, LANES), lambda r, i: (i, r))),
        compiler_params=_cparams("parallel", "parallel"))(qkv, qkv, qkv, qkv, qkv)


def _attn_bwd(qkv, dattn, lse, dd, d, heads, after):
    M = qkv.shape[1]
    nb = M // BLK
    width = heads * HEAD_DIM
    per = LANES // heads
    scale = 1.0 / math.sqrt(HEAD_DIM)

    def block_of(i):
        return nb - 1 - i

    def body(q_ref, kc_ref, kp_ref, vc_ref, vp_ref, da_ref, l_ref, dd_ref, after_ref,
             dq_ref, dk_ref, dv_ref, dk_carry, dv_carry):
        @pl.when(pl.program_id(1) == 0)
        def _():
            dk_carry[...] = jnp.zeros_like(dk_carry)
            dv_carry[...] = jnp.zeros_like(dv_carry)

        mask = _attn_mask2(block_of(pl.program_id(1)))

        def products(h):
            sl = slice(h * HEAD_DIM, (h + 1) * HEAD_DIM)
            k2 = jnp.concatenate([kp_ref[:, sl], kc_ref[:, sl]], axis=0)
            v2 = jnp.concatenate([vp_ref[:, sl], vc_ref[:, sl]], axis=0)
            return (lax.dot_general(q_ref[:, sl], k2, _NT, preferred_element_type=F32),
                    lax.dot_general(da_ref[:, sl], v2, _NT, preferred_element_type=F32), k2)

        ahead = [products(h) for h in range(min(ATTN_LOOKAHEAD, heads))]
        for h in range(heads):
            sl = slice(h * HEAD_DIM, (h + 1) * HEAD_DIM)
            qk, dp, k2 = ahead.pop(0)
            if h + ATTN_LOOKAHEAD < heads:
                ahead.append(products(h + ATTN_LOOKAHEAD))
            q, da = q_ref[:, sl], da_ref[:, sl]
            lse_ = jnp.broadcast_to(l_ref[:, h * per:h * per + 1], (BLK, 2 * BLK))
            dd_ = jnp.broadcast_to(dd_ref[:, h * per:h * per + 1], (BLK, 2 * BLK))
            p = jnp.where(mask, jnp.exp(jnp.where(mask, qk * scale, NEG_INF) - lse_), 0.0)
            ds = (p * (dp - dd_) * scale).astype(BF16)
            dq_ref[:, sl] = jnp.dot(ds, k2, preferred_element_type=F32).astype(BF16)
            dk2 = lax.dot_general(ds, q, _TN, preferred_element_type=F32)
            dv2 = lax.dot_general(p.astype(BF16), da, _TN, preferred_element_type=F32)
            dk_ref[:, sl] = (dk2[BLK:] + dk_carry[:, sl]).astype(BF16)
            dv_ref[:, sl] = (dv2[BLK:] + dv_carry[:, sl]).astype(BF16)
            dk_carry[:, sl] = dk2[:BLK]
            dv_carry[:, sl] = dv2[:BLK]

    blk = pl.BlockSpec((BLK, width), lambda r, i: (block_of(i), r))
    packed = pl.BlockSpec((BLK, LANES), lambda r, i: (block_of(i), r))
    shape = jax.ShapeDtypeStruct((M, d * width), BF16)
    return pl.pallas_call(
        body, name=f"attn_bwd_d{d}", out_shape=(shape,) * 3, grid=(d, nb),
        in_specs=(_attn_in_specs(width, block_of) + [blk, packed, packed]
                  + [pl.BlockSpec(after.shape, lambda r, i: (0, 0))]), out_specs=(blk,) * 3,
        scratch_shapes=[pltpu.VMEM((BLK, width), F32), pltpu.VMEM((BLK, width), F32)],
        compiler_params=_cparams("arbitrary", "arbitrary"))(qkv, qkv, qkv, qkv, qkv, dattn, lse, dd, after)


def _dproj_join(dqs, dks, dvs, du, *, tr=256):
    S, SW = du.shape
    AW = dqs[0].shape[1]
    tr = min(tr, S)
    nd = len(DILATIONS)

    def body(*refs):
        du_ref, out_ref, scr = refs[3 * nd:]
        for part in range(3):
            total = functools.reduce(jnp.add, [_from_residues(scr, refs[part * nd + n], d)
                                               for n, d in enumerate(DILATIONS)])
            out_ref[:, part * AW:(part + 1) * AW] = total.astype(BF16)
        out_ref[:, 3 * AW:] = du_ref[...].astype(BF16)

    return pl.pallas_call(
        body, name="dproj_join", out_shape=jax.ShapeDtypeStruct((S, 3 * AW + SW), BF16), grid=(S // tr,),
        in_specs=[_residue_spec(tr, d, AW) for d in DILATIONS] * 3 + [_rows(tr, SW)],
        out_specs=_rows(tr, 3 * AW + SW), scratch_shapes=[_residue_scratch(tr, AW)],
        compiler_params=_cparams("parallel"))(*dqs, *dks, *dvs, du)


def _ssm_disc(lr, li, ldt):
    dt = jnp.exp(ldt)
    mag = jnp.exp(lr * dt)
    ar = mag * jnp.cos(li * dt)
    ai = mag * jnp.sin(li * dt)
    nr = ar - 1.0
    den = lr * lr + li * li
    return ar, ai, (nr * lr + ai * li) / den, (ai * lr - nr * li) / den


def _ssm_tile_powers(lr, li, ldt, reverse):
    t = lax.broadcasted_iota(jnp.int32, (TILE, 1), 0)
    n = (TILE - t if reverse else t + 1).astype(F32)
    dt = jnp.exp(ldt)
    mag = jnp.exp(n * (lr * dt))
    ang = n * (li * dt)
    return mag * jnp.cos(ang), mag * jnp.sin(ang) * (-1.0 if reverse else 1.0)


def _cmul(ar, ai, br, bi):
    return ar * br - ai * bi, ar * bi + ai * br


LOG_STEPS = 3


def _ssm_step_tables(ar, ai, reverse):
    sub = lax.broadcasted_iota(jnp.int32, (TILE, ar.shape[-1]), 0)
    tables = []
    for k in range(LOG_STEPS):
        keep = sub < TILE - (1 << k) if reverse else sub >= (1 << k)
        tables.append((jnp.where(keep, ar, 0.0), jnp.where(keep, ai, 0.0)))
        ar, ai = _cmul(ar, ai, ar, ai)
    return tables


def _scan(xr, xi, steps, pr, pi, cr, ci, reverse):
    T, lanes = xr.shape
    n = T // TILE
    xr, xi = xr.reshape(n, TILE, lanes), xi.reshape(n, TILE, lanes)
    for k, (mr, mi) in enumerate(steps):
        shift = TILE - (1 << k) if reverse else 1 << k
        qr, qi = _cmul(mr, mi, pltpu.roll(xr, shift, 1), pltpu.roll(xi, shift, 1))
        xr, xi = xr + qr, xi + qi
    out_r, out_i = [None] * n, [None] * n
    edge = 0 if reverse else TILE - 1
    for j in (reversed(range(n)) if reverse else range(n)):
        er, ei = _cmul(pr, pi, cr, ci)
        sr, si = xr[j] + er, xi[j] + ei
        out_r[j], out_i[j] = sr, si
        cr, ci = sr[edge:edge + 1], si[edge:edge + 1]
    return jnp.concatenate(out_r, axis=0), jnp.concatenate(out_i, axis=0), cr, ci


def _ssm_specs(T, nch, rev):
    def t_of(c):
        return nch - 1 - c if rev else c
    tok = pl.BlockSpec((T, LANES), lambda j, c: (t_of(c), j))
    par = pl.BlockSpec((None, 1, STATE_LANES), lambda j, c: (j, 0, 0))
    bmat = pl.BlockSpec((None, LANES, STATE_LANES), lambda j, c: (j, 0, 0))
    cmat = pl.BlockSpec((None, STATE_LANES, LANES), lambda j, c: (j, 0, 0))
    dvec = pl.BlockSpec((1, LANES), lambda j, c: (0, j))
    return tok, par, bmat, cmat, dvec


def _ssm_fwd(u, lr_e, li_e, ldt_e, bre_e, bim_e, cre_e, cim_e, d_skip):
    S, SW = u.shape
    T = min(SSM_CHUNK, S)
    nch, nbk = S // T, SW // LANES
    tok, par, bmat, cmat, dvec = _ssm_specs(T, nch, False)
    state_spec = pl.BlockSpec((T, STATE_LANES), lambda j, c: (c, j))
    carry_spec = pl.BlockSpec((None, 1, STATE_LANES), lambda j, c: (c, 0, j))

    def body(u_ref, lr_ref, li_ref, ldt_ref, bre_ref, bim_ref, cre_ref, cim_ref, d_ref,
             y_ref, y2_ref, sr_ref, si_ref, er_ref, ei_ref, bbr, bbi, steps, pw, carry):
        c = pl.program_id(1)

        @pl.when(c == 0)
        def _():
            lr, li, ldt = lr_ref[...], li_ref[...], ldt_ref[...]
            ar, ai, kr, ki = _ssm_disc(lr, li, ldt)
            for k, (mr, mi) in enumerate(_ssm_step_tables(ar, ai, False)):
                steps[0, k], steps[1, k] = mr, mi
            bbr[...] = (kr * bre_ref[...] - ki * bim_ref[...]).astype(BF16)
            bbi[...] = (kr * bim_ref[...] + ki * bre_ref[...]).astype(BF16)
            pw[0], pw[1] = _ssm_tile_powers(lr, li, ldt, False)
            carry[...] = jnp.zeros_like(carry)

        u_ = u_ref[...]
        ub = u_.astype(BF16)
        sr, si, cr, ci = _scan(jnp.dot(ub, bbr[...], preferred_element_type=F32),
                               jnp.dot(ub, bbi[...], preferred_element_type=F32),
                               [(steps[0, k], steps[1, k]) for k in range(LOG_STEPS)],
                               pw[0], pw[1], carry[0], carry[1], False)
        carry[0], carry[1] = cr, ci
        er_ref[...], ei_ref[...] = cr, ci
        sr_ref[...], si_ref[...] = sr, si
        y0 = (jnp.dot(sr.astype(BF16), cre_ref[...].astype(BF16), preferred_element_type=F32)
              - jnp.dot(si.astype(BF16), cim_ref[...].astype(BF16), preferred_element_type=F32))
        y1 = y0 + d_ref[...] * u_
        y_ref[...] = y1
        y2_ref[...] = _gelu(y1).astype(BF16)

    states = jax.ShapeDtypeStruct((S, nbk * STATE_LANES), F32)
    ends = jax.ShapeDtypeStruct((nch, 1, nbk * STATE_LANES), F32)
    return pl.pallas_call(
        body, name="ssm_fwd",
        out_shape=(jax.ShapeDtypeStruct((S, SW), F32), jax.ShapeDtypeStruct((S, SW), BF16), states, states, ends, ends),
        grid=(nbk, nch), in_specs=[tok, par, par, par, bmat, bmat, cmat, cmat, dvec],
        out_specs=(tok, tok, state_spec, state_spec, carry_spec, carry_spec),
        scratch_shapes=[pltpu.VMEM((LANES, STATE_LANES), BF16), pltpu.VMEM((LANES, STATE_LANES), BF16),
                        pltpu.VMEM((2, LOG_STEPS, TILE, STATE_LANES), F32), pltpu.VMEM((2, TILE, STATE_LANES), F32),
                        pltpu.VMEM((2, 1, STATE_LANES), F32)],
        compiler_params=_cparams("arbitrary", "arbitrary"),
    )(u, lr_e, li_e, ldt_e, bre_e, bim_e, cre_e, cim_e, d_skip)


def _ssm_bwd(u, y1, dy2a, dy2b, st_r, st_i, ends_r, ends_i, lr_e, li_e, ldt_e, bre_e, bim_e, cre_e, cim_e, d_skip):
    S, SW = u.shape
    T = min(SSM_CHUNK, S)
    nch, nbk = S // T, SW // LANES
    tok, par, bmat, cmat, dvec = _ssm_specs(T, nch, True)
    state_spec = pl.BlockSpec((T, STATE_LANES), lambda j, c: (nch - 1 - c, j))
    prev_spec = pl.BlockSpec((None, 1, STATE_LANES), lambda j, c: (jnp.maximum(nch - 2 - c, 0), 0, j))
    acc8 = pl.BlockSpec((None, 8, STATE_LANES), lambda j, c: (j, 0, 0))
    dd8 = pl.BlockSpec((None, 8, LANES), lambda j, c: (j, 0, 0))

    def body(u_ref, y_ref, da_ref, db_ref, sr_ref, si_ref, pr_ref, pi_ref, lr_ref, li_ref, ldt_ref,
             bre_ref, bim_ref, cre_ref, cim_ref, d_ref,
             du_ref, dar_ref, dai_ref, dcr_ref, dci_ref, dbr_ref, dbi_ref, ddk_ref,
             bbr, bbi, steps, pw, carry):
        c = pl.program_id(1)

        @pl.when(c == 0)
        def _():
            lr, li, ldt = lr_ref[...], li_ref[...], ldt_ref[...]
            ar, ai, kr, ki = _ssm_disc(lr, li, ldt)
            for k, (mr, mi) in enumerate(_ssm_step_tables(ar, -ai, True)):
                steps[0, k], steps[1, k] = mr, mi
            bbr[...] = (kr * bre_ref[...] - ki * bim_ref[...]).astype(BF16)
            bbi[...] = (kr * bim_ref[...] + ki * bre_ref[...]).astype(BF16)
            pw[0], pw[1] = _ssm_tile_powers(lr, li, ldt, True)
            carry[...] = jnp.zeros_like(carry)
            for ref in (dar_ref, dai_ref, dcr_ref, dci_ref, dbr_ref, dbi_ref, ddk_ref):
                ref[...] = jnp.zeros_like(ref)

        u_ = u_ref[...]
        ub = u_.astype(BF16)
        dy1 = (da_ref[...] + db_ref[...]) * _gelu_grad(y_ref[...])
        dyb = dy1.astype(BF16)

        sr, si = sr_ref[...], si_ref[...]
        has_prev = c < nch - 1
        s0r = jnp.where(has_prev, pr_ref[...], 0.0)
        s0i = jnp.where(has_prev, pi_ref[...], 0.0)

        cre_b, cim_b = cre_ref[...].astype(BF16), cim_ref[...].astype(BF16)
        gr, gi, cr, ci = _scan(lax.dot_general(dyb, cre_b, _NT, preferred_element_type=F32),
                               -lax.dot_general(dyb, cim_b, _NT, preferred_element_type=F32),
                               [(steps[0, k], steps[1, k]) for k in range(LOG_STEPS)],
                               pw[0], pw[1], carry[0], carry[1], True)
        carry[0], carry[1] = cr, ci

        row = lax.broadcasted_iota(jnp.int32, (T, STATE_LANES), 0)
        spr = jnp.where(row == 0, s0r, pltpu.roll(sr, 1, 0))
        spi = jnp.where(row == 0, s0i, pltpu.roll(si, 1, 0))

        def fold(a):
            return jnp.sum(a.reshape(T // 8, 8, a.shape[-1]), axis=0)

        dar_ref[...] += fold(gr * spr + gi * spi)
        dai_ref[...] += fold(gi * spr - gr * spi)
        srb, sib, grb, gib = sr.astype(BF16), si.astype(BF16), gr.astype(BF16), gi.astype(BF16)
        dcr_ref[...] += lax.dot_general(srb, dyb, _TN, preferred_element_type=F32)
        dci_ref[...] -= lax.dot_general(sib, dyb, _TN, preferred_element_type=F32)
        dbr_ref[...] += lax.dot_general(ub, grb, _TN, preferred_element_type=F32)
        dbi_ref[...] += lax.dot_general(ub, gib, _TN, preferred_element_type=F32)
        du_ref[...] = (lax.dot_general(grb, bbr[...], _NT, preferred_element_type=F32)
                       + lax.dot_general(gib, bbi[...], _NT, preferred_element_type=F32)
                       + dy1 * d_ref[...])
        ddk_ref[...] += fold(dy1 * u_)

    return pl.pallas_call(
        body, name="ssm_bwd",
        out_shape=(jax.ShapeDtypeStruct((S, SW), F32),
                   jax.ShapeDtypeStruct((nbk, 8, STATE_LANES), F32), jax.ShapeDtypeStruct((nbk, 8, STATE_LANES), F32),
                   jax.ShapeDtypeStruct((nbk, STATE_LANES, LANES), F32), jax.ShapeDtypeStruct((nbk, STATE_LANES, LANES), F32),
                   jax.ShapeDtypeStruct((nbk, LANES, STATE_LANES), F32), jax.ShapeDtypeStruct((nbk, LANES, STATE_LANES), F32),
                   jax.ShapeDtypeStruct((nbk, 8, LANES), F32)),
        grid=(nbk, nch),
        in_specs=[tok, tok, tok, tok, state_spec, state_spec, prev_spec, prev_spec, par, par, par,
                  bmat, bmat, cmat, cmat, dvec],
        out_specs=(tok, acc8, acc8, cmat, cmat, bmat, bmat, dd8),
        scratch_shapes=[pltpu.VMEM((LANES, STATE_LANES), BF16), pltpu.VMEM((LANES, STATE_LANES), BF16),
                        pltpu.VMEM((2, LOG_STEPS, TILE, STATE_LANES), F32), pltpu.VMEM((2, TILE, STATE_LANES), F32),
                        pltpu.VMEM((2, 1, STATE_LANES), F32)],
        compiler_params=_cparams("arbitrary", "arbitrary"),
    )(u, y1, dy2a, dy2b, st_r, st_i, ends_r, ends_i, lr_e, li_e, ldt_e, bre_e, bim_e, cre_e, cim_e, d_skip)


def _ssm_param_bwd(dar8, dai8, dbr_e, dbi_e, lr_e, li_e, ldt_e, bre_e, bim_e):
    nbk = lr_e.shape[0]
    par = pl.BlockSpec((None, 1, STATE_LANES), lambda j: (j, 0, 0))
    acc8 = pl.BlockSpec((None, 8, STATE_LANES), lambda j: (j, 0, 0))
    bmat = pl.BlockSpec((None, LANES, STATE_LANES), lambda j: (j, 0, 0))

    def body(dar_ref, dai_ref, dbr_ref, dbi_ref, lr_ref, li_ref, ldt_ref, bre_ref, bim_ref,
             dlr_ref, dli_ref, dldt_ref, dbre_ref, dbim_ref):
        lr, li, ldt = lr_ref[...], li_ref[...], ldt_ref[...]
        (ar, ai, kr, ki), vjp = jax.vjp(_ssm_disc, lr, li, ldt)
        dbr, dbi, bre, bim = dbr_ref[...], dbi_ref[...], bre_ref[...], bim_ref[...]
        dbre_ref[...] = kr * dbr + ki * dbi
        dbim_ref[...] = kr * dbi - ki * dbr
        dkr = _colsum(dbr * bre + dbi * bim)
        dki = _colsum(dbi * bre - dbr * bim)
        dlr, dli, dldt = vjp((_colsum(dar_ref[...]), _colsum(dai_ref[...]), dkr, dki))
        dlr_ref[...] = dlr
        dli_ref[...] = dli
        tot = jnp.broadcast_to(dldt, (8, STATE_LANES))
        sh = 1
        while sh < SSM_P:
            tot = tot + pltpu.roll(tot, STATE_LANES - sh, 1)
            sh *= 2
        dldt_ref[...] = tot[:1]

    vec = jax.ShapeDtypeStruct((nbk, 1, STATE_LANES), F32)
    mat = jax.ShapeDtypeStruct((nbk, LANES, STATE_LANES), F32)
    return pl.pallas_call(
        body, name="ssm_param_bwd", out_shape=(vec, vec, vec, mat, mat), grid=(nbk,),
        in_specs=[acc8, acc8, bmat, bmat, par, par, par, bmat, bmat],
        out_specs=(par, par, par, bmat, bmat), compiler_params=_cparams("parallel"),
    )(dar8, dai8, dbr_e, dbi_e, lr_e, li_e, ldt_e, bre_e, bim_e)


def _expand_b(b):
    G = b.shape[0]
    bt = b.transpose(0, 2, 1).reshape(G // GROUPS_PER_BLOCK, GROUPS_PER_BLOCK, SSM_C, SSM_P)
    eye = jnp.eye(GROUPS_PER_BLOCK, dtype=b.dtype)
    return (bt[:, :, :, None, :] * eye[None, :, None, :, None]).reshape(G // GROUPS_PER_BLOCK, LANES, STATE_LANES)


def _collapse_b(be):
    nbk = be.shape[0]
    eye = jnp.eye(GROUPS_PER_BLOCK, dtype=be.dtype)
    d5 = be.reshape(nbk, GROUPS_PER_BLOCK, SSM_C, GROUPS_PER_BLOCK, SSM_P)
    d4 = (d5 * eye[None, :, None, :, None]).sum(axis=3)
    return d4.transpose(0, 1, 3, 2).reshape(nbk * GROUPS_PER_BLOCK, SSM_P, SSM_C)


def _expand_c(cm):
    G = cm.shape[0]
    ct = cm.transpose(0, 2, 1).reshape(G // GROUPS_PER_BLOCK, GROUPS_PER_BLOCK, SSM_P, SSM_C)
    eye = jnp.eye(GROUPS_PER_BLOCK, dtype=cm.dtype)
    return (ct[:, :, :, None, :] * eye[None, :, None, :, None]).reshape(G // GROUPS_PER_BLOCK, STATE_LANES, LANES)


def _collapse_c(ce):
    nbk = ce.shape[0]
    eye = jnp.eye(GROUPS_PER_BLOCK, dtype=ce.dtype)
    d5 = ce.reshape(nbk, GROUPS_PER_BLOCK, SSM_P, GROUPS_PER_BLOCK, SSM_C)
    d4 = (d5 * eye[None, :, None, :, None]).sum(axis=3)
    return d4.transpose(0, 1, 3, 2).reshape(nbk * GROUPS_PER_BLOCK, SSM_C, SSM_P)


def _place():
    x, y, c = lax.axis_index("x"), lax.axis_index("y"), lax.axis_index("c")
    return x, y, c


def _other_chips(x, y):
    return [(1 - x, y), (x, 1 - y), (1 - x, 1 - y)]


_ANY = pl.BlockSpec(memory_space=pl.ANY)


_HBM = pl.BlockSpec(memory_space=pltpu.HBM)
_SEM = pl.BlockSpec(memory_space=pltpu.SEMAPHORE)
_EFFECT = pltpu.SideEffectType.DATAFLOW_SIDE_EFFECTING
_TOKEN = jax.ShapeDtypeStruct((8, LANES), F32)


def _hbm(a):
    return pltpu.with_memory_space_constraint(a, pltpu.HBM)


def _place_own(src, *, gather, name, after=None, tr=512):
    R, C = src.shape[-2:]
    tr = min(tr, R)
    x, y, _ = _place()
    me = (2 * x + y).astype(jnp.int32).reshape(1)
    extra = [] if after is None else [after]

    def body(me_ref, s_ref, *rest):
        rest[-1][...] = s_ref[...].astype(BF16)

    own = pl.BlockSpec((None, tr, C), lambda i, me_ref: (me_ref[0], i, 0))
    grid_spec = pltpu.PrefetchScalarGridSpec(
        num_scalar_prefetch=1, grid=(R // tr,),
        in_specs=([pl.BlockSpec((tr, C), lambda i, me_ref: (i, 0)) if gather else own]
                  + [pl.BlockSpec(a.shape, lambda i, me_ref: (0, 0)) for a in extra]), out_specs=own)
    return pl.pallas_call(
        body, name=name, grid_spec=grid_spec, out_shape=jax.ShapeDtypeStruct((N_CHIPS, R, C), BF16),
        compiler_params=_cparams("parallel"))(me, src, *extra)


def _exchange_copy(src_slot, land_slot, send, recv, k, j, peer, c):
    return pltpu.make_async_remote_copy(
        src_ref=src_slot, dst_ref=land_slot, send_sem=send.at[3 * k + j], recv_sem=recv.at[3 * k + j],
        device_id=(peer[0], peer[1], c), device_id_type=MESH)


def _exchange_start(lands, srcs, groups, *, name):
    n, ng = len(lands), len(groups)
    bufs = list(lands) + list(srcs)
    nb = len(bufs)

    def body(*refs):
        lnd, src, sems = refs[:n], refs[n:nb], refs[nb:nb + 2 * ng]
        token = refs[2 * nb + 2 * ng]
        x, y, c = _place()
        me = 2 * x + y
        for gi, group in enumerate(groups):
            for k, w in enumerate(group):
                for j, peer in enumerate(_other_chips(x, y)):
                    if src:
                        sent, dst = src[w].at[2 * peer[0] + peer[1]], lnd[w].at[me]
                    else:
                        sent = dst = lnd[w].at[me, c]
                    _exchange_copy(sent, dst, sems[2 * gi], sems[2 * gi + 1], k, j, peer, c).start()
        token[...] = jnp.zeros_like(token)

    sem_shapes = [pltpu.SemaphoreType.DMA((3 * len(g),)) for g in groups for _ in range(2)]
    res = pl.pallas_call(
        body, name=name,
        out_shape=sem_shapes + [pltpu.HBM(a.shape, a.dtype) for a in bufs] + [_TOKEN],
        in_specs=[_HBM] * nb,
        out_specs=[_SEM] * (2 * ng) + [_HBM] * nb + [pl.BlockSpec(memory_space=pltpu.VMEM)],
        input_output_aliases={i: 2 * ng + i for i in range(nb)},
        compiler_params=pltpu.CompilerParams(has_side_effects=_EFFECT),
    )(*[_hbm(a) for a in bufs])
    sems = [(res[2 * gi], res[2 * gi + 1]) for gi in range(ng)]
    return sems, res[2 * ng:2 * ng + n], res[2 * ng + n:2 * ng + nb], res[-1]


def _exchange_wait(lands, srcs, sems, after, *, name):
    n = len(lands)
    bufs = list(lands) + list(srcs)
    nb = len(bufs)
    send_sems, recv_sems = sems

    def body(*refs):
        lnd, src, send, recv = refs[:n], refs[n:nb], refs[nb], refs[nb + 1]
        x, y, c = _place()
        for k in range(n):
            for j, peer in enumerate(_other_chips(x, y)):
                slot = 2 * peer[0] + peer[1]
                if src:
                    copy = _exchange_copy(src[k].at[slot], lnd[k].at[slot], send, recv, k, j, peer, c)
                else:
                    copy = _exchange_copy(lnd[k].at[slot, c], lnd[k].at[slot, c], send, recv, k, j, peer, c)
                copy.wait_send()
                copy.wait_recv()

    res = pl.pallas_call(
        body, name=name, out_shape=[pltpu.HBM(a.shape, a.dtype) for a in bufs],
        in_specs=[_HBM] * nb + [_SEM, _SEM, _ANY], out_specs=[_HBM] * nb,
        input_output_aliases={i: i for i in range(nb)},
        compiler_params=pltpu.CompilerParams(has_side_effects=_EFFECT),
    )(*bufs, send_sems, recv_sems, after)
    return res[:n]


def _pair_fill(lands, *, name):
    n = len(lands)

    def body(*refs):
        ins, outs, send, recv = refs[:n], refs[n:2 * n], refs[2 * n], refs[2 * n + 1]
        x, y, c = _place()
        for w in range(n):
            for j, (px, py) in enumerate(_other_chips(x, y)):
                slot = 2 * px + py
                pltpu.make_async_remote_copy(
                    src_ref=ins[w].at[slot, c], dst_ref=outs[w].at[slot, c], send_sem=send.at[3 * w + j],
                    recv_sem=recv.at[3 * w + j], device_id=(x, y, 1 - c), device_id_type=MESH).start()
        for w in range(n):
            for j, (px, py) in enumerate(_other_chips(x, y)):
                slot = 2 * px + py
                arrival = pltpu.make_async_remote_copy(
                    src_ref=ins[w].at[slot, c], dst_ref=outs[w].at[slot, 1 - c], send_sem=send.at[3 * w + j],
                    recv_sem=recv.at[3 * w + j], device_id=(x, y, 1 - c), device_id_type=MESH)
                arrival.wait_recv()
                arrival.wait_send()

    return pl.pallas_call(
        body, name=name, out_shape=[jax.ShapeDtypeStruct(a.shape, a.dtype) for a in lands],
        in_specs=[_ANY] * n, out_specs=[_ANY] * n, input_output_aliases={i: i for i in range(n)},
        scratch_shapes=[pltpu.SemaphoreType.DMA((3 * n,)), pltpu.SemaphoreType.DMA((3 * n,))],
    )(*lands)


def _pair_copy(src, dst, send, recv, w, j, sibling):
    return pltpu.make_async_remote_copy(
        src_ref=src, dst_ref=dst, send_sem=send.at[3 * w + j], recv_sem=recv.at[3 * w + j],
        device_id=sibling, device_id_type=MESH)


def _pair_start(lands, *, name):
    n = len(lands)

    def body(*refs):
        bufs, send, recv, token = refs[:n], refs[n], refs[n + 1], refs[2 * n + 2]
        x, y, c = _place()
        for w in range(n):
            for j, (px, py) in enumerate(_other_chips(x, y)):
                half = bufs[w].at[2 * px + py, c]
                _pair_copy(half, half, send, recv, w, j, (x, y, 1 - c)).start()
        token[...] = jnp.zeros_like(token)

    res = pl.pallas_call(
        body, name=name,
        out_shape=[pltpu.SemaphoreType.DMA((3 * n,))] * 2 + [pltpu.HBM(a.shape, a.dtype) for a in lands] + [_TOKEN],
        in_specs=[_HBM] * n, out_specs=[_SEM, _SEM] + [_HBM] * n + [pl.BlockSpec(memory_space=pltpu.VMEM)],
        input_output_aliases={i: 2 + i for i in range(n)},
        compiler_params=pltpu.CompilerParams(has_side_effects=_EFFECT),
    )(*[_hbm(a) for a in lands])
    return (res[0], res[1]), res[2:2 + n], res[-1]


def _pair_wait(lands, sems, after, *, name):
    n = len(lands)

    def body(*refs):
        bufs, send, recv = refs[:n], refs[n], refs[n + 1]
        x, y, c = _place()
        for w in range(n):
            for j, (px, py) in enumerate(_other_chips(x, y)):
                slot = 2 * px + py
                copy = _pair_copy(bufs[w].at[slot, c], bufs[w].at[slot, 1 - c], send, recv, w, j, (x, y, 1 - c))
                copy.wait_send()
                copy.wait_recv()

    return pl.pallas_call(
        body, name=name, out_shape=[pltpu.HBM(a.shape, a.dtype) for a in lands],
        in_specs=[_HBM] * n + [_SEM, _SEM, _ANY], out_specs=[_HBM] * n,
        input_output_aliases={i: i for i in range(n)},
        compiler_params=pltpu.CompilerParams(has_side_effects=_EFFECT),
    )(*lands, *sems, after)


def _sum_partials(land, *, name, tr=256):
    _, R, C = land.shape
    tr = min(tr, R)

    def body(l_ref, o_ref):
        acc = l_ref[0].astype(F32)
        for k in range(1, N_CHIPS):
            acc = acc + l_ref[k].astype(F32)
        o_ref[...] = acc

    return pl.pallas_call(
        body, name=name, out_shape=jax.ShapeDtypeStruct((R, C), F32), grid=(R // tr,),
        in_specs=[pl.BlockSpec((N_CHIPS, tr, C), lambda i: (0, i, 0))], out_specs=_rows(tr, C),
        compiler_params=_cparams("parallel"))(land)


def _swap_with_sibling(sums, *, name):
    n = len(sums)

    def body(*refs):
        ins, outs = refs[:n], refs[n:2 * n]
        send_sems, recv_sems = refs[2 * n:]
        x, y, c = _place()
        copies = [pltpu.make_async_remote_copy(
            src_ref=ins[w], dst_ref=outs[w], send_sem=send_sems.at[w], recv_sem=recv_sems.at[w],
            device_id=(x, y, 1 - c), device_id_type=MESH) for w in range(n)]
        for cp in copies:
            cp.start()
        for cp in copies:
            cp.wait_recv()
            cp.wait_send()

    return pl.pallas_call(
        body, name=name,
        out_shape=[jax.ShapeDtypeStruct(s.shape, s.dtype) for s in sums],
        in_specs=[_ANY] * n, out_specs=[_ANY] * n,
        scratch_shapes=[pltpu.SemaphoreType.DMA((n,)), pltpu.SemaphoreType.DMA((n,))],
    )(*sums)


def _swap_start(sums, *, name):
    n = len(sums)
    bufs = list(sums) + [lax.empty(s.shape, s.dtype) for s in sums]

    def body(*refs):
        src, lnd, send, recv, token = refs[:n], refs[n:2 * n], refs[2 * n], refs[2 * n + 1], refs[4 * n + 2]
        x, y, c = _place()
        for w in range(n):
            pltpu.make_async_remote_copy(
                src_ref=src[w], dst_ref=lnd[w], send_sem=send.at[w], recv_sem=recv.at[w],
                device_id=(x, y, 1 - c), device_id_type=MESH).start()
        token[...] = jnp.zeros_like(token)

    res = pl.pallas_call(
        body, name=name,
        out_shape=[pltpu.SemaphoreType.DMA((n,))] * 2 + [pltpu.HBM(a.shape, a.dtype) for a in bufs] + [_TOKEN],
        in_specs=[_HBM] * (2 * n),
        out_specs=[_SEM, _SEM] + [_HBM] * (2 * n) + [pl.BlockSpec(memory_space=pltpu.VMEM)],
        input_output_aliases={i: 2 + i for i in range(2 * n)},
        compiler_params=pltpu.CompilerParams(has_side_effects=_EFFECT),
    )(*[_hbm(a) for a in bufs])
    return (res[0], res[1]), res[2:2 + n], res[2 + n:2 + 2 * n], res[-1]


def _swap_wait(sums, lands, sems, after, *, name):
    n = len(sums)

    def body(*refs):
        src, lnd, send, recv = refs[:n], refs[n:2 * n], refs[2 * n], refs[2 * n + 1]
        x, y, c = _place()
        for w in range(n):
            copy = pltpu.make_async_remote_copy(
                src_ref=src[w], dst_ref=lnd[w], send_sem=send.at[w], recv_sem=recv.at[w],
                device_id=(x, y, 1 - c), device_id_type=MESH)
            copy.wait_send()
            copy.wait_recv()

    bufs = list(sums) + list(lands)
    res = pl.pallas_call(
        body, name=name, out_shape=[pltpu.HBM(a.shape, a.dtype) for a in bufs],
        in_specs=[_HBM] * (2 * n) + [_SEM, _SEM, _ANY], out_specs=[_HBM] * (2 * n),
        input_output_aliases={i: i for i in range(2 * n)},
        compiler_params=pltpu.CompilerParams(has_side_effects=_EFFECT),
    )(*bufs, *sems, after)
    return res[:n], res[n:]


def _adamw_math(w, g, m, v):
    m = ADAM_B1 * m + (1.0 - ADAM_B1) * g
    v = ADAM_B2 * v + (1.0 - ADAM_B2) * (g * g)
    m_hat = m / (1.0 - ADAM_B1 ** ADAM_STEP)
    v_hat = v / (1.0 - ADAM_B2 ** ADAM_STEP)
    delta = -ADAM_LR * (m_hat / (jnp.sqrt(v_hat) + ADAM_EPS) + ADAM_WD * w)
    return delta, m, v


def _adamw_pair(mine, theirs, w, m, v, *, name, tr=128):
    R, C = w.shape
    tr = min(tr, R)

    def body(a_ref, b_ref, w_ref, m_ref, v_ref, g_ref, d_ref, nm_ref, nv_ref):
        g = a_ref[...] + b_ref[...]
        g_ref[...] = g
        d_ref[...], nm_ref[...], nv_ref[...] = _adamw_math(w_ref[...], g, m_ref[...], v_ref[...])

    shape = jax.ShapeDtypeStruct((R, C), F32)
    return pl.pallas_call(
        body, name=name, out_shape=(shape,) * 4, grid=(R // tr,),
        in_specs=[_rows(tr, C)] * 5, out_specs=(_rows(tr, C),) * 4,
        compiler_params=_cparams("parallel"))(mine, theirs, w, m, v)


def _all_reduce_small(packed):
    R = packed.shape[0]
    half = R // 2

    def body(x_ref, g_ref, sib_ref, pair_ref, land_ref, send_sems, recv_sems):
        x, y, c = _place()
        me = 2 * x + y
        sibling = (x, y, 1 - c)

        swap = pltpu.make_async_remote_copy(
            src_ref=x_ref, dst_ref=sib_ref, send_sem=send_sems.at[0], recv_sem=recv_sems.at[0],
            device_id=sibling, device_id_type=MESH)
        swap.start()
        swap.wait()
        mine, theirs = x_ref[...], sib_ref[...]
        south = c == 0
        pair_ref[...] = jnp.where(south, mine, theirs) + jnp.where(south, theirs, mine)

        land_ref[me] = pair_ref[c]
        for j, (px, py) in enumerate(_other_chips(x, y)):
            pltpu.make_async_remote_copy(
                src_ref=pair_ref.at[c], dst_ref=land_ref.at[me], send_sem=send_sems.at[1 + j],
                recv_sem=recv_sems.at[1 + j], device_id=(px, py, c), device_id_type=MESH).start()
        for j, (px, py) in enumerate(_other_chips(x, y)):
            arrival = pltpu.make_async_remote_copy(
                src_ref=pair_ref.at[c], dst_ref=land_ref.at[2 * px + py], send_sem=send_sems.at[1 + j],
                recv_sem=recv_sems.at[1 + j], device_id=(px, py, c), device_id_type=MESH)
            arrival.wait_recv()
            arrival.wait_send()
        total = land_ref[0]
        for k in range(1, N_CHIPS):
            total = total + land_ref[k]
        g_ref[c] = total

        give = pltpu.make_async_remote_copy(
            src_ref=g_ref.at[c], dst_ref=g_ref.at[c], send_sem=send_sems.at[4], recv_sem=recv_sems.at[4],
            device_id=sibling, device_id_type=MESH)
        give.start()
        take = pltpu.make_async_remote_copy(
            src_ref=g_ref.at[c], dst_ref=g_ref.at[1 - c], send_sem=send_sems.at[4], recv_sem=recv_sems.at[4],
            device_id=sibling, device_id_type=MESH)
        take.wait_recv()
        give.wait_send()

    vm = pl.BlockSpec(memory_space=pltpu.VMEM)
    return pl.pallas_call(
        body, name="all_reduce_small", out_shape=jax.ShapeDtypeStruct((2, half, LANES), F32),
        in_specs=[vm], out_specs=vm,
        scratch_shapes=[pltpu.VMEM((2, half, LANES), F32), pltpu.VMEM((2, half, LANES), F32),
                        pltpu.VMEM((N_CHIPS, half, LANES), F32),
                        pltpu.SemaphoreType.DMA((5,)), pltpu.SemaphoreType.DMA((5,))],
        compiler_params=pltpu.CompilerParams(vmem_limit_bytes=VMEM_LIMIT_BYTES),
    )(packed.reshape(2, half, LANES)).reshape(R, LANES)


def _adamw_small(g, w, m, v):
    R = g.shape[0]
    tr = PACK_ROWS

    def body(g_ref, w_ref, m_ref, v_ref, d_ref, nm_ref, nv_ref):
        d_ref[...], nm_ref[...], nv_ref[...] = _adamw_math(w_ref[...], g_ref[...], m_ref[...], v_ref[...])

    shape = jax.ShapeDtypeStruct((R, LANES), F32)
    return pl.pallas_call(
        body, name="adamw_small", out_shape=(shape,) * 3, grid=(R // tr,),
        in_specs=[_rows(tr, LANES)] * 4, out_specs=(_rows(tr, LANES),) * 3,
        compiler_params=_cparams("parallel"))(g, w, m, v)


def _pack(arrays):
    parts, layout = [], []
    for a in arrays:
        n = a.size
        rows = -(-n // (8 * LANES)) * 8
        flat = jnp.pad(a.reshape(-1).astype(F32), (0, rows * LANES - n))
        parts.append(flat.reshape(rows, LANES))
        layout.append((rows, n, a.shape))
    total = sum(r for r, _, _ in layout)
    parts.append(jnp.zeros((-total % PACK_ROWS, LANES), F32))
    return jnp.concatenate(parts, axis=0), layout


def _unpack(buf, layout):
    out, r0 = [], 0
    for rows, n, shape in layout:
        out.append(buf[r0:r0 + rows].reshape(-1)[:n].reshape(shape))
        r0 += rows
    return out


SMALL = ("mix_norm_pre", "lam_re", "lam_im", "log_dt", "ssm_b_re", "ssm_b_im", "ssm_c_re", "ssm_c_im",
         "ssm_d", "b_glu", "attn_out_norm", "ssm_out_norm", "mix_norm_post", "mlp_norm_pre",
         "mlp_norm_post", "ple_norm_pre", "ple_norm_post")
BIG = ("w_in", "w_glu", "w_out", "w_up", "w_down", "w_ple_gate", "w_ple_proj")
WEIGHTS = ("mix_norm_pre", "w_in", "lam_re", "lam_im", "log_dt", "ssm_b_re", "ssm_b_im", "ssm_c_re",
           "ssm_c_im", "ssm_d", "w_glu", "b_glu", "attn_out_norm", "ssm_out_norm", "w_out",
           "mix_norm_post", "mlp_norm_pre", "w_up", "w_down", "mlp_norm_post", "ple_norm_pre",
           "w_ple_gate", "w_ple_proj", "ple_norm_post")


def kernel(x, p, mix_norm_pre, w_in, lam_re, lam_im, log_dt, ssm_b_re, ssm_b_im, ssm_c_re, ssm_c_im, ssm_d, w_glu, b_glu, attn_out_norm, ssm_out_norm, w_out, mix_norm_post, mlp_norm_pre, w_up, w_down, mlp_norm_post, ple_norm_pre, w_ple_gate, w_ple_proj, ple_norm_post, loss_target, m_mix_norm_pre, m_w_in, m_lam_re, m_lam_im, m_log_dt, m_ssm_b_re, m_ssm_b_im, m_ssm_c_re, m_ssm_c_im, m_ssm_d, m_w_glu, m_b_glu, m_attn_out_norm, m_ssm_out_norm, m_w_out, m_mix_norm_post, m_mlp_norm_pre, m_w_up, m_w_down, m_mlp_norm_post, m_ple_norm_pre, m_w_ple_gate, m_w_ple_proj, m_ple_norm_post, v_mix_norm_pre, v_w_in, v_lam_re, v_lam_im, v_log_dt, v_ssm_b_re, v_ssm_b_im, v_ssm_c_re, v_ssm_c_im, v_ssm_d, v_w_glu, v_b_glu, v_attn_out_norm, v_ssm_out_norm, v_w_out, v_mix_norm_post, v_mlp_norm_pre, v_w_up, v_w_down, v_mlp_norm_post, v_ple_norm_pre, v_w_ple_gate, v_w_ple_proj, v_ple_norm_post):
    args = dict(locals())
    W = {n: args[n][0] for n in WEIGHTS}
    Mo = {n: args["m_" + n][0] for n in WEIGHTS}
    Vo = {n: args["v_" + n][0] for n in WEIGHTS}
    xs, ps, tgt = x[0], p[0, 0], loss_target[0]
    S, D = xs.shape
    SW = W["ssm_d"].shape[0]
    AW = W["attn_out_norm"].shape[0]
    heads = AW // HEAD_DIM
    G = SW // SSM_C
    nbk = SW // LANES
    assert W["w_in"].shape[1] * N_CHIPS == 3 * AW + SW and AW == SW

    row = lambda a: a.reshape(1, -1)

    ag_groups = (("w_in",), ("w_glu", "w_out"), ("w_up",), ("w_down", "w_ple_gate", "w_ple_proj"))
    ag_names = [n for g in ag_groups for n in g]
    def in_halves(a):
        return a.reshape(N_CHIPS, 2, a.shape[1] // 2, a.shape[2])

    def placed(n, after=None):
        return in_halves(_place_own(W[n], gather=True, name="ag_place_" + n, after=after))

    first_sems, first_land, _, first_token = _exchange_start([placed("w_in")], [], [[0]], name="ag_start_first")
    rest_sems, rest_land, _, ag_token = _exchange_start(
        [placed(n, first_token) for n in ag_names[1:]], [],
        [[ag_names.index(n) - 1 for n in g] for g in ag_groups[1:]], name="ag_start")
    ag_sems, ag_land = first_sems + rest_sems, list(first_land) + list(rest_land)

    def fetched(gi, after):
        return _exchange_wait([ag_land[ag_names.index(n)] for n in ag_groups[gi]], [], ag_sems[gi], after,
                              name=f"ag_wait_{gi}")

    def whole(gis, bufs):
        names = [n for gi in gis for n in ag_groups[gi]]
        return {n: a.reshape(N_CHIPS, -1, a.shape[-1]) for n, a in zip(names, bufs)}

    lr_e = W["lam_re"].reshape(nbk, 1, STATE_LANES)
    li_e = W["lam_im"].reshape(nbk, 1, STATE_LANES)
    ldt_e = jnp.repeat(W["log_dt"], SSM_P).reshape(nbk, 1, STATE_LANES)
    bre_e, bim_e = _expand_b(W["ssm_b_re"]), _expand_b(W["ssm_b_im"])
    cre_e, cim_e = _expand_c(W["ssm_c_re"]), _expand_c(W["ssm_c_im"])
    d_row = row(W["ssm_d"])

    hn1 = _norm_cast(xs, row(W["mix_norm_pre"]) + ag_token[0, 0], name="norm_in")
    w_in_f = whole([0], _pair_fill(fetched(0, hn1), name="ag_pair_0"))["w_in"]
    qkv_b = _proj_qkv(hn1, w_in_f)
    outs, lses = zip(*[_attn_fwd(qb, d, heads) for d, qb in zip(DILATIONS, qkv_b)])
    pair_a_sems, pair_a, pair_a_token = _pair_start(fetched(1, outs[-1]), name="ag_pair_start_a")
    u = _matmul(hn1, w_in_f, name="proj_u", b_shards=N_CHIPS, b_cols=(3 * AW, SW), after=pair_a_token)
    y1, y2b, st_r, st_i, ends_r, ends_i = _ssm_fwd(u, lr_e, li_e, ldt_e, bre_e, bim_e, cre_e, cim_e, d_row)
    pair_b_sems, pair_b, pair_b_token = _pair_start(fetched(2, y2b), name="ag_pair_start_b")
    full = whole([1], _pair_wait(pair_a, pair_a_sems, y2b, name="ag_pair_wait_a"))
    w_glu_f = full["w_glu"].reshape(SW, SW)
    w_out_f = full["w_out"].reshape(AW + SW, D)
    z = _matmul(y2b, w_glu_f, name="glu_z", after=pair_b_token)
    attn, lse_b, mixed = _mix_fwd(outs, lses, y1, z, row(W["b_glu"]), row(W["attn_out_norm"]), row(W["ssm_out_norm"]))
    mo = _matmul(mixed, w_out_f, name="mix_out")
    h1, hn2 = _res_norm(xs, mo, row(W["mix_norm_post"]), row(W["mlp_norm_pre"]), name="res_mix")
    w_up_f = whole([2], _pair_wait(pair_b, pair_b_sems, hn2, name="ag_pair_wait_b"))["w_up"]
    up, act = _matmul(hn2, w_up_f, name="mlp_up", b_shards=N_CHIPS, relu2=True)
    full = whole([3], _pair_fill(fetched(3, act), name="ag_pair_3"))
    w_down_f = full["w_down"].reshape(-1, D)
    w_pg_f = full["w_ple_gate"].reshape(D, D)
    w_pp_f = full["w_ple_proj"]
    ff = _matmul(act, w_down_f, name="mlp_down")
    h2, hn3 = _res_norm(h1, ff, row(W["mlp_norm_post"]), row(W["ple_norm_pre"]), name="res_mlp")
    gl = _matmul(hn3, w_pg_f, name="ple_gate")
    e = _matmul(ps.astype(BF16), w_pp_f, name="ple_proj", b_shards=N_CHIPS)

    dh3, dgl, de, loss_part, dg_ple_post = _final(h2, gl, e, row(W["ple_norm_post"]), tgt)
    gW = {}
    out_g, out_d, out_m, out_v = {}, {}, {}, {}

    def scatter_start(names, tag):
        parts = [gW[n] if gW[n].ndim == 3 else gW[n].reshape((N_CHIPS, -1, gW[n].shape[1])) for n in names]
        sems, land, src, token = _exchange_start(
            [_place_own(part, gather=False, name="rs_place_" + n) for n, part in zip(names, parts)], parts,
            [list(range(len(names)))], name=f"rs_start_{tag}")
        return (names, sems[0], land, src), token

    def scatter_sums(batches, after):
        names, sums = [], []
        for tag, (batch_names, sems, land, src) in batches:
            landed = _exchange_wait(land, src, sems, after, name=f"rs_wait_{tag}")
            names += batch_names
            sums += [_sum_partials(l, name="sum_" + n) for n, l in zip(batch_names, landed)]
        return names, sums

    def apply(names, sums, theirs):
        for n, a, b in zip(names, sums, theirs):
            out_g[n], out_d[n], out_m[n], out_v[n] = _adamw_pair(a, b, W[n], Mo[n], Vo[n], name="adamw_" + n)

    def swap_begin(batches, after, tag):
        names, sums = scatter_sums(batches, after)
        sems, sums, lands, token = _swap_start(sums, name=f"swap_start_{tag}")
        return (names, sems, sums, lands), token

    def swap_end(swap, after, tag):
        names, sems, sums, lands = swap
        sums, theirs = _swap_wait(sums, lands, sems, after, name=f"swap_wait_{tag}")
        apply(names, sums, theirs)

    def scatter_finish(batch, after, tag):
        names, sums = scatter_sums([(tag, batch)], after)
        apply(names, sums, _swap_with_sibling(sums, name=f"swap_{tag}"))

    gW["w_ple_proj"] = _matmul(ps.astype(BF16), de, name="d_w_ple_proj", ta=True, out_dtype=BF16, out_shards=N_CHIPS)
    gW["w_ple_gate"] = _matmul(hn3, dgl, name="d_w_ple_gate", ta=True, out_dtype=BF16)
    dhn3 = _matmul(dgl, w_pg_f, name="d_hn3", tb=True)
    dh2, dff, dg_ple_pre, dg_mlp_post = _bwd_res_norm(
        dh3, dhn3, h2, row(W["ple_norm_pre"]), ff, row(W["mlp_norm_post"]), name="bwd_res_mlp")
    gW["w_down"] = _matmul(act, dff, name="d_w_down", ta=True, out_dtype=BF16)
    batch1, token1 = scatter_start(("w_ple_proj", "w_ple_gate", "w_down"), 1)
    dup = _matmul(dff, w_down_f, name="d_up", tb=True, after=token1, relu2_of=up, out_dtype=BF16)
    gW["w_up"] = _matmul(hn2, dup, name="d_w_up", ta=True, out_dtype=BF16, out_shards=N_CHIPS)
    batch2, token2 = scatter_start(("w_up",), 2)
    dhn2 = _matmul(dup, w_up_f, name="d_hn2", tb=True, b_shards=N_CHIPS, after=token2)
    dh1, dmo, dg_mlp_pre, dg_mix_post = _bwd_res_norm(
        dh2, dhn2, h1, row(W["mlp_norm_pre"]), mo, row(W["mix_norm_post"]), name="bwd_res_mix")
    gW["w_out"] = _matmul(mixed, dmo, name="d_w_out", ta=True, out_dtype=BF16)
    dmixed = _matmul(dmo, w_out_f, name="d_mixed", tb=True)
    dattn_b, dd_b, dz, dy2a, dg_attn, dg_ssm, db_glu = _mix_bwd(
        dmixed, attn, y1, z, row(W["b_glu"]), row(W["attn_out_norm"]), row(W["ssm_out_norm"]))
    gW["w_glu"] = _matmul(y2b, dz, name="d_w_glu", ta=True, out_dtype=BF16)
    batch3, token3 = scatter_start(("w_out", "w_glu"), 3)
    dy2b = _matmul(dz, w_glu_f, name="d_y2", tb=True, after=token3)
    du, dar8, dai8, dcr_e, dci_e, dbr_e, dbi_e, dd8 = _ssm_bwd(
        u, y1, dy2a, dy2b, st_r, st_i, ends_r, ends_i, lr_e, li_e, ldt_e, bre_e, bim_e, cre_e, cim_e, d_row)
    swap_a, token_a = swap_begin([(1, batch1)], du, "a")
    dlr_e, dli_e, dldt_e, dbre_e, dbim_e = _ssm_param_bwd(dar8, dai8, dbr_e, dbi_e, lr_e, li_e, ldt_e, bre_e, bim_e)

    dqs, dks, dvs = zip(*[_attn_bwd(qb, da, l, dd_, d, heads, token_a)
                          for d, qb, da, l, dd_ in zip(DILATIONS, qkv_b, dattn_b, lse_b, dd_b)])
    dproj = _dproj_join(dqs, dks, dvs, du)
    swap_end(swap_a, dproj, "a")
    swap_b, token_b = swap_begin([(2, batch2), (3, batch3)], dproj, "b")
    gW["w_in"] = _matmul(hn1, dproj, name="d_w_in", ta=True, out_dtype=BF16, out_shards=N_CHIPS, after=token_b)
    batch4, token4 = scatter_start(("w_in",), 4)
    dhn1 = _matmul(dproj, w_in_f, name="d_hn1", tb=True, b_shards=N_CHIPS, after=token4)
    grad_x, dg_mix_pre = _bwd_first(dh1, dhn1, xs, row(W["mix_norm_pre"]))
    swap_end(swap_b, grad_x, "b")
    scatter_finish(batch4, grad_x, 4)

    small_g = {
        "mix_norm_pre": dg_mix_pre, "lam_re": dlr_e.reshape(G, SSM_P), "lam_im": dli_e.reshape(G, SSM_P),
        "log_dt": dldt_e.reshape(G, SSM_P)[:, 0], "ssm_b_re": _collapse_b(dbre_e), "ssm_b_im": _collapse_b(dbim_e),
        "ssm_c_re": _collapse_c(dcr_e), "ssm_c_im": _collapse_c(dci_e), "ssm_d": dd8.sum(axis=1).reshape(-1),
        "b_glu": db_glu, "attn_out_norm": dg_attn, "ssm_out_norm": dg_ssm, "mix_norm_post": dg_mix_post,
        "mlp_norm_pre": dg_mlp_pre, "mlp_norm_post": dg_mlp_post, "ple_norm_pre": dg_ple_pre,
        "ple_norm_post": dg_ple_post,
    }
    g_pack, layout = _pack([small_g[n].reshape(W[n].shape) for n in SMALL])
    w_pack, _ = _pack([W[n] for n in SMALL])
    m_pack, _ = _pack([Mo[n] for n in SMALL])
    v_pack, _ = _pack([Vo[n] for n in SMALL])
    g_sum = _all_reduce_small(g_pack)
    packed = (g_sum,) + tuple(_adamw_small(g_sum, w_pack, m_pack, v_pack))
    for dst, buf in zip((out_g, out_d, out_m, out_v), packed):
        dst.update(zip(SMALL, _unpack(buf, layout)))

    loss = lax.psum(loss_part[0, 0], ("x", "y", "c"))
    lead = lambda a: a[None]
    return (loss, grad_x[None],
            *[lead(out_g[n]) for n in WEIGHTS], *[lead(out_d[n]) for n in WEIGHTS],
            *[lead(out_m[n]) for n in WEIGHTS], *[lead(out_v[n]) for n in WEIGHTS])
```

```python
import functools
import math

import jax
import jax.numpy as jnp
from jax import lax
from jax.experimental import pallas as pl
from jax.experimental.pallas import tpu as pltpu

F32 = jnp.float32
BF16 = jnp.bfloat16
MESH = pl.DeviceIdType.MESH

RMS_EPS = 1e-6
NEG_INF = -1e30
HEAD_DIM = 128
BLK = 128
DILATIONS = (1, 4, 16)
ATTN_LOOKAHEAD = 3
SSM_C = 16
SSM_P = 64
LANES = 128
GROUPS_PER_BLOCK = LANES // SSM_C
STATE_LANES = GROUPS_PER_BLOCK * SSM_P
SSM_CHUNK = 1024
TILE = 8
ADAM_LR, ADAM_B1, ADAM_B2, ADAM_EPS, ADAM_WD, ADAM_STEP = 1e-3, 0.9, 0.999, 1e-8, 0.01, 10
VMEM_LIMIT_BYTES = 56 * 1024 * 1024
MATMUL_VMEM_BYTES = 44 * 1024 * 1024
N_CHIPS = 4
N_DEV = 8
PACK_ROWS = 256


def _cparams(*sem):
    return pltpu.CompilerParams(dimension_semantics=sem or None, vmem_limit_bytes=VMEM_LIMIT_BYTES)


def _rows(tr, w):
    return pl.BlockSpec((tr, w), lambda i: (i, 0))


def _vec(w):
    return pl.BlockSpec((1, w), lambda i: (0, 0))


def _sigmoid(x):
    return 1.0 / (1.0 + jnp.exp(-x))


def _gelu(x):
    c = math.sqrt(2.0 / math.pi)
    return 0.5 * x * (1.0 + jnp.tanh(c * (x + 0.044715 * x * x * x)))


def _gelu_grad(x):
    c = math.sqrt(2.0 / math.pi)
    th = jnp.tanh(c * (x + 0.044715 * x * x * x))
    return 0.5 * (1.0 + th) + 0.5 * x * (1.0 - th * th) * c * (1.0 + 3.0 * 0.044715 * x * x)


def _rms(x, g):
    r = lax.rsqrt(jnp.mean(x * x, axis=-1, keepdims=True) + RMS_EPS)
    return x * r * g


def _rms_bwd(dy, x, g):
    r = lax.rsqrt(jnp.mean(x * x, axis=-1, keepdims=True) + RMS_EPS)
    n = x * r
    dn = dy * g
    dx = r * (dn - n * jnp.mean(dn * n, axis=-1, keepdims=True))
    return dx, dy * n


def _colsum(a):
    return jnp.sum(a, axis=0, keepdims=True)


def _first(i):
    return i == 0


def _matmul(a, b, *, name, ta=False, tb=False, out_dtype=F32, b_shards=1, out_shards=1, b_cols=None,
            after=None, relu2=False, relu2_of=None, tm=1024, tn=2048, tk=2048):
    if ta:
        K, M = a.shape
    else:
        M, K = a.shape
    if b_shards > 1:
        rows, cols = b.shape[1], b.shape[2] * b_shards
    else:
        rows, cols = b.shape
    N, Kb = (rows, cols) if tb else (cols, rows)
    assert K == Kb, (a.shape, b.shape, ta, tb)
    col0 = 0
    if b_cols is not None:
        assert not tb
        col0, N = b_cols
    tm, tn, tk = min(tm, M), min(tn, N), min(tk, K)
    if b_shards > 1:
        shard_cols = cols // b_shards
        if tb:
            tk = min(tk, shard_cols)
        else:
            tn = min(tn, shard_cols)
    if out_shards > 1:
        tn = min(tn, N // out_shards)

    def vmem_bytes(tn_):
        out_bytes = jnp.dtype(out_dtype).itemsize + (2 if relu2 else 0)
        return (4 * (tm * tk + tk * tn_) + 2 * tm * tn_ * out_bytes
                + (8 * tm * tn_ if relu2_of is not None else 0) + (4 * tm * tn_ if K > tk else 0))

    while vmem_bytes(tn) > MATMUL_VMEM_BYTES and tn > LANES and col0 % (tn // 2) == 0:
        tn //= 2
    assert M % tm == 0 and N % tn == 0 and K % tk == 0 and col0 % tn == 0
    nk = K // tk
    j0 = col0 // tn

    a_spec = (pl.BlockSpec((tk, tm), lambda i, j, k: (k, i)) if ta
              else pl.BlockSpec((tm, tk), lambda i, j, k: (i, k)))
    if b_shards > 1:
        if tb:
            per = shard_cols // tk
            b_spec = pl.BlockSpec((None, tn, tk), lambda i, j, k: (k // per, j, k % per))
        else:
            per = shard_cols // tn
            b_spec = pl.BlockSpec((None, tk, tn), lambda i, j, k: ((j + j0) // per, k, (j + j0) % per))
    else:
        b_spec = (pl.BlockSpec((tn, tk), lambda i, j, k: (j, k)) if tb
                  else pl.BlockSpec((tk, tn), lambda i, j, k: (k, j + j0)))
    if out_shards > 1:
        per_o = (N // out_shards) // tn
        out_shape = jax.ShapeDtypeStruct((out_shards, M, N // out_shards), out_dtype)
        out_spec = pl.BlockSpec((None, tm, tn), lambda i, j, k: (j // per_o, i, j % per_o))
    else:
        out_shape = jax.ShapeDtypeStruct((M, N), out_dtype)
        out_spec = pl.BlockSpec((tm, tn), lambda i, j, k: (i, j))
    dims = (((0 if ta else 1,), (1 if tb else 0,)), ((), ()))

    extra, extra_specs = [], []
    if relu2_of is not None:
        assert out_shards == 1 and relu2_of.shape == (M, N)
        extra.append(relu2_of)
        extra_specs.append(pl.BlockSpec((tm, tn), lambda i, j, k: (i, j)))
    if after is not None:
        extra.append(after)
        extra_specs.append(pl.BlockSpec(after.shape, lambda i, j, k: (0, 0)))
    n_in = 2 + len(extra)
    if relu2:
        assert out_shards == 1
        out_shape = (out_shape, jax.ShapeDtypeStruct((M, N), BF16))
        out_spec = (out_spec, out_spec)

    def finish(acc, refs):
        o_ref = refs[n_in]
        if relu2_of is not None:
            acc = acc * (2.0 * jnp.maximum(refs[2][...], 0.0))
        o_ref[...] = acc.astype(o_ref.dtype)
        if relu2:
            r = jnp.maximum(acc, 0.0)
            refs[n_in + 1][...] = (r * r).astype(BF16)

    def body(*refs):
        prod = lax.dot_general(refs[0][...], refs[1][...], dims, preferred_element_type=F32)
        if nk == 1:
            finish(prod, refs)
            return
        acc_ref = refs[-1]
        k = pl.program_id(2)

        @pl.when(k == 0)
        def _():
            acc_ref[...] = prod

        @pl.when(k > 0)
        def _():
            acc_ref[...] += prod

        @pl.when(k == nk - 1)
        def _():
            finish(acc_ref[...], refs)

    return pl.pallas_call(
        body, name=name, out_shape=out_shape, grid=(M // tm, N // tn, nk),
        in_specs=[a_spec, b_spec] + extra_specs, out_specs=out_spec,
        scratch_shapes=[pltpu.VMEM((tm, tn), F32)] if nk > 1 else [],
        compiler_params=_cparams("parallel", "parallel", "arbitrary"),
    )(a, b, *extra)


def _norm_cast(x, g, *, name, tr=256):
    S, D = x.shape
    tr = min(tr, S)

    def body(x_ref, g_ref, o_ref):
        o_ref[...] = _rms(x_ref[...], g_ref[...]).astype(BF16)

    return pl.pallas_call(
        body, name=name, out_shape=jax.ShapeDtypeStruct((S, D), BF16), grid=(S // tr,),
        in_specs=[_rows(tr, D), _vec(D)], out_specs=_rows(tr, D),
        compiler_params=_cparams("parallel"))(x, g)


def _res_norm(res, y, g_post, g_next, *, name, tr=256):
    S, D = res.shape
    tr = min(tr, S)

    def body(res_ref, y_ref, gp_ref, gn_ref, h_ref, hn_ref):
        h = res_ref[...] + _rms(y_ref[...], gp_ref[...])
        h_ref[...] = h
        hn_ref[...] = _rms(h, gn_ref[...]).astype(BF16)

    return pl.pallas_call(
        body, name=name,
        out_shape=(jax.ShapeDtypeStruct((S, D), F32), jax.ShapeDtypeStruct((S, D), BF16)),
        grid=(S // tr,), in_specs=[_rows(tr, D), _rows(tr, D), _vec(D), _vec(D)],
        out_specs=(_rows(tr, D), _rows(tr, D)), compiler_params=_cparams("parallel"))(res, y, g_post, g_next)


def _residue_spec(tr, d, w):
    return pl.BlockSpec((tr // d, d * w), lambda i: (i, 0))


def _residue_shape(S, d, w, dtype):
    return jax.ShapeDtypeStruct((S // d, d * w), dtype)


def _residue_scratch(rows, w):
    return pltpu.VMEM((w // LANES, rows, LANES), F32)


def _fill_strips(scr, val):
    for s in range(scr.shape[0]):
        scr[s] = val[:, s * LANES:(s + 1) * LANES]


def _strips_to_residues(scr, o_ref, d):
    strips, rows, _ = scr.shape
    for r in range(d):
        for s in range(strips):
            col = (r * strips + s) * LANES
            o_ref[:, col:col + LANES] = scr[s, pl.ds(r, rows // d, stride=d), :].astype(o_ref.dtype)


def _to_residues(scr, val, o_ref, d):
    if d == 1:
        o_ref[...] = val.astype(o_ref.dtype)
        return
    _fill_strips(scr, val)
    _strips_to_residues(scr, o_ref, d)


def _from_residues(scr, in_ref, d):
    if d == 1:
        return in_ref[...].astype(F32)
    strips, rows, _ = scr.shape
    for r in range(d):
        for s in range(strips):
            col = (r * strips + s) * LANES
            scr[s, pl.ds(r, rows // d, stride=d), :] = in_ref[:, col:col + LANES].astype(F32)
    return jnp.concatenate([scr[s] for s in range(strips)], axis=1)


def _spread_heads(packed, heads, width=HEAD_DIM):
    per = LANES // heads
    return jnp.concatenate([jnp.broadcast_to(packed[:, h * per:h * per + 1], (packed.shape[0], width))
                            for h in range(heads)], axis=1)


def _mix_fwd(os, ls, y1, z, b_glu, g_attn, g_ssm, *, tr=128):
    S, SW = y1.shape
    AW = os[0].shape[1] // DILATIONS[0]
    heads = AW // HEAD_DIM
    tr = min(tr, S)
    nd = len(DILATIONS)

    def body(*refs):
        o_refs, l_refs = refs[:nd], refs[nd:2 * nd]
        y_ref, z_ref, b_ref, ga_ref, gs_ref, attn_ref = refs[2 * nd:2 * nd + 6]
        lse_refs = refs[2 * nd + 6:3 * nd + 6]
        mixed_ref, scr, scr_p = refs[3 * nd + 6:]
        ls_ = [_from_residues(scr_p, l_refs[n], d) for n, d in enumerate(DILATIONS)]
        m = functools.reduce(jnp.maximum, ls_)
        es = [jnp.exp(l - m) for l in ls_]
        tot = functools.reduce(jnp.add, es)
        attn = functools.reduce(jnp.add, [_spread_heads(e / tot, heads) * _from_residues(scr, o_refs[n], d)
                                          for n, (e, d) in enumerate(zip(es, DILATIONS))])
        attn_ref[...] = attn
        lse = m + jnp.log(tot)
        for n, d in enumerate(DILATIONS):
            _to_residues(scr_p, lse, lse_refs[n], d)
        ssm = _gelu(y_ref[...]) * _sigmoid(z_ref[...] + b_ref[...])
        mixed_ref[:, :AW] = _rms(attn, ga_ref[...]).astype(BF16)
        mixed_ref[:, AW:] = _rms(ssm, gs_ref[...]).astype(BF16)

    res_o = [_residue_spec(tr, d, AW) for d in DILATIONS]
    res_l = [_residue_spec(tr, d, LANES) for d in DILATIONS]
    res = pl.pallas_call(
        body, name="mix_fwd",
        out_shape=([jax.ShapeDtypeStruct((S, AW), F32)] + [_residue_shape(S, d, LANES, F32) for d in DILATIONS]
                   + [jax.ShapeDtypeStruct((S, AW + SW), BF16)]),
        grid=(S // tr,),
        in_specs=res_o + res_l + [_rows(tr, SW), _rows(tr, SW), _vec(SW), _vec(AW), _vec(SW)],
        out_specs=[_rows(tr, AW)] + res_l + [_rows(tr, AW + SW)],
        scratch_shapes=[_residue_scratch(tr, AW), _residue_scratch(tr, LANES)],
        compiler_params=_cparams("parallel"))(*os, *ls, y1, z, b_glu, g_attn, g_ssm)
    return res[0], res[1:1 + nd], res[1 + nd]


def _final(h2, gl, e, g_post, target, *, tr=128):
    S, D = h2.shape
    tr = min(tr, S)

    def body(h_ref, gl_ref, e_ref, g_ref, t_ref, dh_ref, dgl_ref, de_ref, loss_ref, dg_ref):
        i = pl.program_id(0)
        gate = _sigmoid(gl_ref[...])
        e_ = e_ref[...]
        ge = gate * e_
        g = g_ref[...]
        diff = h_ref[...] + _rms(ge, g) - t_ref[...]
        dh = diff * (1.0 / D)
        dh_ref[...] = dh
        dge, dgrow = _rms_bwd(dh, ge, g)
        dgl_ref[...] = (dge * e_ * gate * (1.0 - gate)).astype(BF16)
        de_ref[...] = (dge * gate).astype(BF16)
        part = _colsum(0.5 * jnp.mean(diff * diff, axis=-1, keepdims=True))

        @pl.when(_first(i))
        def _():
            loss_ref[...] = jnp.zeros_like(loss_ref)
            dg_ref[...] = jnp.zeros_like(dg_ref)

        loss_ref[...] += part + jnp.zeros((1, LANES), F32)
        dg_ref[...] += _colsum(dgrow)

    return pl.pallas_call(
        body, name="final_fwd_bwd",
        out_shape=(jax.ShapeDtypeStruct((S, D), F32), jax.ShapeDtypeStruct((S, D), BF16),
                   jax.ShapeDtypeStruct((S, D), BF16), jax.ShapeDtypeStruct((1, LANES), F32),
                   jax.ShapeDtypeStruct((1, D), F32)),
        grid=(S // tr,),
        in_specs=[_rows(tr, D), _rows(tr, D), _rows(tr, D), _vec(D), _rows(tr, D)],
        out_specs=(_rows(tr, D), _rows(tr, D), _rows(tr, D), _vec(LANES), _vec(D)),
        compiler_params=_cparams("arbitrary"))(h2, gl, e, g_post, target)


def _bwd_res_norm(dh_out, dhn, h, g_next, y, g_post, *, name, tr=128):
    S, D = h.shape
    tr = min(tr, S)

    def body(dho_ref, dhn_ref, h_ref, gn_ref, y_ref, gp_ref, dh_ref, dy_ref, dgn_ref, dgp_ref):
        i = pl.program_id(0)
        dx, dgn_rows = _rms_bwd(dhn_ref[...], h_ref[...], gn_ref[...])
        dh = dho_ref[...] + dx
        dh_ref[...] = dh
        dy, dgp_rows = _rms_bwd(dh, y_ref[...], gp_ref[...])
        dy_ref[...] = dy.astype(BF16)

        @pl.when(_first(i))
        def _():
            dgn_ref[...] = jnp.zeros_like(dgn_ref)
            dgp_ref[...] = jnp.zeros_like(dgp_ref)

        dgn_ref[...] += _colsum(dgn_rows)
        dgp_ref[...] += _colsum(dgp_rows)

    return pl.pallas_call(
        body, name=name,
        out_shape=(jax.ShapeDtypeStruct((S, D), F32), jax.ShapeDtypeStruct((S, D), BF16),
                   jax.ShapeDtypeStruct((1, D), F32), jax.ShapeDtypeStruct((1, D), F32)),
        grid=(S // tr,),
        in_specs=[_rows(tr, D), _rows(tr, D), _rows(tr, D), _vec(D), _rows(tr, D), _vec(D)],
        out_specs=(_rows(tr, D), _rows(tr, D), _vec(D), _vec(D)),
        compiler_params=_cparams("arbitrary"))(dh_out, dhn, h, g_next, y, g_post)


def _bwd_first(dh1, dhn1, x, g1, *, tr=256):
    S, D = x.shape
    tr = min(tr, S)

    def body(dh_ref, dhn_ref, x_ref, g_ref, dx_ref, dg_ref):
        i = pl.program_id(0)
        dx, dg_rows = _rms_bwd(dhn_ref[...], x_ref[...], g_ref[...])
        dx_ref[...] = dh_ref[...] + dx

        @pl.when(_first(i))
        def _():
            dg_ref[...] = jnp.zeros_like(dg_ref)

        dg_ref[...] += _colsum(dg_rows)

    return pl.pallas_call(
        body, name="bwd_first",
        out_shape=(jax.ShapeDtypeStruct((S, D), F32), jax.ShapeDtypeStruct((1, D), F32)),
        grid=(S // tr,), in_specs=[_rows(tr, D), _rows(tr, D), _rows(tr, D), _vec(D)],
        out_specs=(_rows(tr, D), _vec(D)), compiler_params=_cparams("arbitrary"))(dh1, dhn1, x, g1)


def _mix_bwd(dmixed, attn, y1, z, b_glu, g_attn, g_ssm, *, tr=256):
    S, AW = attn.shape
    SW = y1.shape[1]
    tr = min(tr, S)
    heads = AW // HEAD_DIM
    nd = len(DILATIONS)

    def body(*refs):
        dm_ref, a_ref, y_ref, z_ref, b_ref, ga_ref, gs_ref = refs[:7]
        da_refs, dd_refs = refs[7:7 + nd], refs[7 + nd:7 + 2 * nd]
        dz_ref, dy2_ref, dga_ref, dgs_ref, db_ref, scr, scr_p, dd_scr = refs[7 + 2 * nd:]
        i = pl.program_id(0)
        attn_ = a_ref[...]
        dattn, dga_rows = _rms_bwd(dm_ref[:, :AW], attn_, ga_ref[...])
        prod = dattn * attn_
        per = LANES // heads
        for h in range(heads):
            total = jnp.sum(prod[:, h * HEAD_DIM:(h + 1) * HEAD_DIM], axis=-1, keepdims=True)
            dd_scr[:, h * per:(h + 1) * per] = jnp.broadcast_to(total, (tr, per))
        for n, d in enumerate(DILATIONS):
            _to_residues(scr, dattn, da_refs[n], d)
            _to_residues(scr_p, dd_scr[...], dd_refs[n], d)
        y2 = _gelu(y_ref[...])
        gate = _sigmoid(z_ref[...] + b_ref[...])
        dssm, dgs_rows = _rms_bwd(dm_ref[:, AW:], y2 * gate, gs_ref[...])
        dz = dssm * y2 * gate * (1.0 - gate)
        dz_ref[...] = dz.astype(BF16)
        dy2_ref[...] = dssm * gate

        @pl.when(_first(i))
        def _():
            dga_ref[...] = jnp.zeros_like(dga_ref)
            dgs_ref[...] = jnp.zeros_like(dgs_ref)
            db_ref[...] = jnp.zeros_like(db_ref)

        dga_ref[...] += _colsum(dga_rows)
        dgs_ref[...] += _colsum(dgs_rows)
        db_ref[...] += _colsum(dz)

    res_a = [_residue_spec(tr, d, AW) for d in DILATIONS]
    res_d = [_residue_spec(tr, d, LANES) for d in DILATIONS]
    res = pl.pallas_call(
        body, name="mix_bwd",
        out_shape=([_residue_shape(S, d, AW, BF16) for d in DILATIONS]
                   + [_residue_shape(S, d, LANES, F32) for d in DILATIONS]
                   + [jax.ShapeDtypeStruct((S, SW), BF16), jax.ShapeDtypeStruct((S, SW), F32),
                      jax.ShapeDtypeStruct((1, AW), F32), jax.ShapeDtypeStruct((1, SW), F32),
                      jax.ShapeDtypeStruct((1, SW), F32)]),
        grid=(S // tr,),
        in_specs=[_rows(tr, AW + SW), _rows(tr, AW), _rows(tr, SW), _rows(tr, SW), _vec(SW), _vec(AW), _vec(SW)],
        out_specs=res_a + res_d + [_rows(tr, SW), _rows(tr, SW), _vec(AW), _vec(SW), _vec(SW)],
        scratch_shapes=[_residue_scratch(tr, AW), _residue_scratch(tr, LANES), pltpu.VMEM((tr, LANES), F32)],
        compiler_params=_cparams("arbitrary"))(dmixed, attn, y1, z, b_glu, g_attn, g_ssm)
    return (res[:nd], res[nd:2 * nd]) + tuple(res[2 * nd:])


def _attn_mask2(i):
    row = lax.broadcasted_iota(jnp.int32, (BLK, 2 * BLK), 0)
    col = lax.broadcasted_iota(jnp.int32, (BLK, 2 * BLK), 1)
    return jnp.logical_and(col >= row, jnp.logical_and(col <= row + BLK, jnp.logical_or(col >= BLK, i > 0)))


_NT = (((1,), (1,)), ((), ()))
_TN = (((0,), (0,)), ((), ()))


def _attn_in_specs(width, block_of):
    def at(part, prev):
        def index(r, i):
            blk = block_of(i)
            return (part, jnp.maximum(blk - 1, 0) if prev else blk, r)
        return pl.BlockSpec((None, BLK, width), index)
    return [at(0, False), at(1, False), at(1, True), at(2, False), at(2, True)]


def _proj_qkv(hn, w_in_f, *, tm=1024):
    S, D = hn.shape
    AW = w_in_f.shape[2]
    tm = min(tm, S)

    def body(a_ref, b_ref, *rest):
        o_refs, scr = rest[:-1], rest[-1]
        prod = jnp.dot(a_ref[...], b_ref[...], preferred_element_type=F32)
        _fill_strips(scr, prod)
        for o_ref, d in zip(o_refs, DILATIONS):
            if d == 1:
                o_ref[...] = prod.astype(BF16)
            else:
                _strips_to_residues(scr, o_ref, d)

    return pl.pallas_call(
        body, name="proj_qkv",
        out_shape=[jax.ShapeDtypeStruct((3, S // d, d * AW), BF16) for d in DILATIONS], grid=(S // tm, 3),
        in_specs=[pl.BlockSpec((tm, D), lambda i, j: (i, 0)), pl.BlockSpec((None, D, AW), lambda i, j: (j, 0, 0))],
        out_specs=[pl.BlockSpec((None, tm // d, d * AW), lambda i, j: (j, i, 0)) for d in DILATIONS],
        scratch_shapes=[_residue_scratch(tm, AW)],
        compiler_params=_cparams("parallel", "parallel"))(hn, w_in_f)


def _attn_fwd(qkv, d, heads):
    M = qkv.shape[1]
    nb = M // BLK
    width = heads * HEAD_DIM
    per = LANES // heads
    scale = 1.0 / math.sqrt(HEAD_DIM)

    def body(q_ref, kc_ref, kp_ref, vc_ref, vp_ref, o_ref, l_ref):
        mask = _attn_mask2(pl.program_id(1))
        ones = jnp.ones((2 * BLK, HEAD_DIM), BF16)

        def scores(h):
            sl = slice(h * HEAD_DIM, (h + 1) * HEAD_DIM)
            k2 = jnp.concatenate([kp_ref[:, sl], kc_ref[:, sl]], axis=0)
            return lax.dot_general(q_ref[:, sl], k2, _NT, preferred_element_type=F32)

        ahead = [scores(h) for h in range(min(ATTN_LOOKAHEAD, heads))]
        for h in range(heads):
            sl = slice(h * HEAD_DIM, (h + 1) * HEAD_DIM)
            s = jnp.where(mask, ahead.pop(0) * scale, NEG_INF)
            if h + ATTN_LOOKAHEAD < heads:
                ahead.append(scores(h + ATTN_LOOKAHEAD))
            v2 = jnp.concatenate([vp_ref[:, sl], vc_ref[:, sl]], axis=0)
            m = jnp.max(jnp.maximum(s[:, :BLK], s[:, BLK:]), axis=-1, keepdims=True)
            p = jnp.exp(s - m).astype(BF16)
            tot = jnp.dot(p, ones, preferred_element_type=F32)
            o_ref[:, sl] = jnp.dot(p, v2, preferred_element_type=F32) / tot
            l_ref[:, h * per:(h + 1) * per] = m + jnp.log(tot[:, :per])

    return pl.pallas_call(
        body, name=f"attn_fwd_d{d}",
        out_shape=(jax.ShapeDtypeStruct((M, d * width), F32), jax.ShapeDtypeStruct((M, d * LANES), F32)),
        grid=(d, nb), in_specs=_attn_in_specs(width, lambda i: i),
        out_specs=(pl.BlockSpec((BLK, width), lambda r, i: (i, r)), pl.BlockSpec((BLK, LANES), lambda r, i: (i, r))),
        compiler_params=_cparams("parallel", "parallel"))(qkv, qkv, qkv, qkv, qkv)


def _attn_bwd(qkv, dattn, lse, dd, d, heads, after):
    M = qkv.shape[1]
    nb = M // BLK
    width = heads * HEAD_DIM
    per = LANES // heads
    scale = 1.0 / math.sqrt(HEAD_DIM)

    def block_of(i):
        return nb - 1 - i

    def body(q_ref, kc_ref, kp_ref, vc_ref, vp_ref, da_ref, l_ref, dd_ref, after_ref,
             dq_ref, dk_ref, dv_ref, dk_carry, dv_carry):
        @pl.when(pl.program_id(1) == 0)
        def _():
            dk_carry[...] = jnp.zeros_like(dk_carry)
            dv_carry[...] = jnp.zeros_like(dv_carry)

        mask = _attn_mask2(block_of(pl.program_id(1)))

        def products(h):
            sl = slice(h * HEAD_DIM, (h + 1) * HEAD_DIM)
            k2 = jnp.concatenate([kp_ref[:, sl], kc_ref[:, sl]], axis=0)
            v2 = jnp.concatenate([vp_ref[:, sl], vc_ref[:, sl]], axis=0)
            return (lax.dot_general(q_ref[:, sl], k2, _NT, preferred_element_type=F32),
                    lax.dot_general(da_ref[:, sl], v2, _NT, preferred_element_type=F32), k2)

        ahead = [products(h) for h in range(min(ATTN_LOOKAHEAD, heads))]
        for h in range(heads):
            sl = slice(h * HEAD_DIM, (h + 1) * HEAD_DIM)
            qk, dp, k2 = ahead.pop(0)
            if h + ATTN_LOOKAHEAD < heads:
                ahead.append(products(h + ATTN_LOOKAHEAD))
            q, da = q_ref[:, sl], da_ref[:, sl]
            lse_ = jnp.broadcast_to(l_ref[:, h * per:h * per + 1], (BLK, 2 * BLK))
            dd_ = jnp.broadcast_to(dd_ref[:, h * per:h * per + 1], (BLK, 2 * BLK))
            p = jnp.where(mask, jnp.exp(jnp.where(mask, qk * scale, NEG_INF) - lse_), 0.0)
            ds = (p * (dp - dd_) * scale).astype(BF16)
            dq_ref[:, sl] = jnp.dot(ds, k2, preferred_element_type=F32).astype(BF16)
            dk2 = lax.dot_general(ds, q, _TN, preferred_element_type=F32)
            dv2 = lax.dot_general(p.astype(BF16), da, _TN, preferred_element_type=F32)
            dk_ref[:, sl] = (dk2[BLK:] + dk_carry[:, sl]).astype(BF16)
            dv_ref[:, sl] = (dv2[BLK:] + dv_carry[:, sl]).astype(BF16)
            dk_carry[:, sl] = dk2[:BLK]
            dv_carry[:, sl] = dv2[:BLK]

    blk = pl.BlockSpec((BLK, width), lambda r, i: (block_of(i), r))
    packed = pl.BlockSpec((BLK, LANES), lambda r, i: (block_of(i), r))
    shape = jax.ShapeDtypeStruct((M, d * width), BF16)
    return pl.pallas_call(
        body, name=f"attn_bwd_d{d}", out_shape=(shape,) * 3, grid=(d, nb),
        in_specs=(_attn_in_specs(width, block_of) + [blk, packed, packed]
                  + [pl.BlockSpec(after.shape, lambda r, i: (0, 0))]), out_specs=(blk,) * 3,
        scratch_shapes=[pltpu.VMEM((BLK, width), F32), pltpu.VMEM((BLK, width), F32)],
        compiler_params=_cparams("arbitrary", "arbitrary"))(qkv, qkv, qkv, qkv, qkv, dattn, lse, dd, after)


def _dproj_join(dqs, dks, dvs, du, *, tr=256):
    S, SW = du.shape
    AW = dqs[0].shape[1]
    tr = min(tr, S)
    nd = len(DILATIONS)

    def body(*refs):
        du_ref, out_ref, scr = refs[3 * nd:]
        for part in range(3):
            total = functools.reduce(jnp.add, [_from_residues(scr, refs[part * nd + n], d)
                                               for n, d in enumerate(DILATIONS)])
            out_ref[:, part * AW:(part + 1) * AW] = total.astype(BF16)
        out_ref[:, 3 * AW:] = du_ref[...].astype(BF16)

    return pl.pallas_call(
        body, name="dproj_join", out_shape=jax.ShapeDtypeStruct((S, 3 * AW + SW), BF16), grid=(S // tr,),
        in_specs=[_residue_spec(tr, d, AW) for d in DILATIONS] * 3 + [_rows(tr, SW)],
        out_specs=_rows(tr, 3 * AW + SW), scratch_shapes=[_residue_scratch(tr, AW)],
        compiler_params=_cparams("parallel"))(*dqs, *dks, *dvs, du)


def _ssm_disc(lr, li, ldt):
    dt = jnp.exp(ldt)
    mag = jnp.exp(lr * dt)
    ar = mag * jnp.cos(li * dt)
    ai = mag * jnp.sin(li * dt)
    nr = ar - 1.0
    den = lr * lr + li * li
    return ar, ai, (nr * lr + ai * li) / den, (ai * lr - nr * li) / den


def _ssm_tile_powers(lr, li, ldt, reverse):
    t = lax.broadcasted_iota(jnp.int32, (TILE, 1), 0)
    n = (TILE - t if reverse else t + 1).astype(F32)
    dt = jnp.exp(ldt)
    mag = jnp.exp(n * (lr * dt))
    ang = n * (li * dt)
    return mag * jnp.cos(ang), mag * jnp.sin(ang) * (-1.0 if reverse else 1.0)


def _cmul(ar, ai, br, bi):
    return ar * br - ai * bi, ar * bi + ai * br


LOG_STEPS = 3


def _ssm_step_tables(ar, ai, reverse):
    sub = lax.broadcasted_iota(jnp.int32, (TILE, ar.shape[-1]), 0)
    tables = []
    for k in range(LOG_STEPS):
        keep = sub < TILE - (1 << k) if reverse else sub >= (1 << k)
        tables.append((jnp.where(keep, ar, 0.0), jnp.where(keep, ai, 0.0)))
        ar, ai = _cmul(ar, ai, ar, ai)
    return tables


def _scan(xr, xi, steps, pr, pi, cr, ci, reverse):
    T, lanes = xr.shape
    n = T // TILE
    xr, xi = xr.reshape(n, TILE, lanes), xi.reshape(n, TILE, lanes)
    for k, (mr, mi) in enumerate(steps):
        shift = TILE - (1 << k) if reverse else 1 << k
        qr, qi = _cmul(mr, mi, pltpu.roll(xr, shift, 1), pltpu.roll(xi, shift, 1))
        xr, xi = xr + qr, xi + qi
    out_r, out_i = [None] * n, [None] * n
    edge = 0 if reverse else TILE - 1
    for j in (reversed(range(n)) if reverse else range(n)):
        er, ei = _cmul(pr, pi, cr, ci)
        sr, si = xr[j] + er, xi[j] + ei
        out_r[j], out_i[j] = sr, si
        cr, ci = sr[edge:edge + 1], si[edge:edge + 1]
    return jnp.concatenate(out_r, axis=0), jnp.concatenate(out_i, axis=0), cr, ci


def _ssm_specs(T, nch, rev):
    def t_of(c):
        return nch - 1 - c if rev else c
    tok = pl.BlockSpec((T, LANES), lambda j, c: (t_of(c), j))
    par = pl.BlockSpec((None, 1, STATE_LANES), lambda j, c: (j, 0, 0))
    bmat = pl.BlockSpec((None, LANES, STATE_LANES), lambda j, c: (j, 0, 0))
    cmat = pl.BlockSpec((None, STATE_LANES, LANES), lambda j, c: (j, 0, 0))
    dvec = pl.BlockSpec((1, LANES), lambda j, c: (0, j))
    return tok, par, bmat, cmat, dvec


def _ssm_fwd(u, lr_e, li_e, ldt_e, bre_e, bim_e, cre_e, cim_e, d_skip):
    S, SW = u.shape
    T = min(SSM_CHUNK, S)
    nch, nbk = S // T, SW // LANES
    tok, par, bmat, cmat, dvec = _ssm_specs(T, nch, False)
    state_spec = pl.BlockSpec((T, STATE_LANES), lambda j, c: (c, j))
    carry_spec = pl.BlockSpec((None, 1, STATE_LANES), lambda j, c: (c, 0, j))

    def body(u_ref, lr_ref, li_ref, ldt_ref, bre_ref, bim_ref, cre_ref, cim_ref, d_ref,
             y_ref, y2_ref, sr_ref, si_ref, er_ref, ei_ref, bbr, bbi, steps, pw, carry):
        c = pl.program_id(1)

        @pl.when(c == 0)
        def _():
            lr, li, ldt = lr_ref[...], li_ref[...], ldt_ref[...]
            ar, ai, kr, ki = _ssm_disc(lr, li, ldt)
            for k, (mr, mi) in enumerate(_ssm_step_tables(ar, ai, False)):
                steps[0, k], steps[1, k] = mr, mi
            bbr[...] = (kr * bre_ref[...] - ki * bim_ref[...]).astype(BF16)
            bbi[...] = (kr * bim_ref[...] + ki * bre_ref[...]).astype(BF16)
            pw[0], pw[1] = _ssm_tile_powers(lr, li, ldt, False)
            carry[...] = jnp.zeros_like(carry)

        u_ = u_ref[...]
        ub = u_.astype(BF16)
        sr, si, cr, ci = _scan(jnp.dot(ub, bbr[...], preferred_element_type=F32),
                               jnp.dot(ub, bbi[...], preferred_element_type=F32),
                               [(steps[0, k], steps[1, k]) for k in range(LOG_STEPS)],
                               pw[0], pw[1], carry[0], carry[1], False)
        carry[0], carry[1] = cr, ci
        er_ref[...], ei_ref[...] = cr, ci
        sr_ref[...], si_ref[...] = sr, si
        y0 = (jnp.dot(sr.astype(BF16), cre_ref[...].astype(BF16), preferred_element_type=F32)
              - jnp.dot(si.astype(BF16), cim_ref[...].astype(BF16), preferred_element_type=F32))
        y1 = y0 + d_ref[...] * u_
        y_ref[...] = y1
        y2_ref[...] = _gelu(y1).astype(BF16)

    states = jax.ShapeDtypeStruct((S, nbk * STATE_LANES), F32)
    ends = jax.ShapeDtypeStruct((nch, 1, nbk * STATE_LANES), F32)
    return pl.pallas_call(
        body, name="ssm_fwd",
        out_shape=(jax.ShapeDtypeStruct((S, SW), F32), jax.ShapeDtypeStruct((S, SW), BF16), states, states, ends, ends),
        grid=(nbk, nch), in_specs=[tok, par, par, par, bmat, bmat, cmat, cmat, dvec],
        out_specs=(tok, tok, state_spec, state_spec, carry_spec, carry_spec),
        scratch_shapes=[pltpu.VMEM((LANES, STATE_LANES), BF16), pltpu.VMEM((LANES, STATE_LANES), BF16),
                        pltpu.VMEM((2, LOG_STEPS, TILE, STATE_LANES), F32), pltpu.VMEM((2, TILE, STATE_LANES), F32),
                        pltpu.VMEM((2, 1, STATE_LANES), F32)],
        compiler_params=_cparams("arbitrary", "arbitrary"),
    )(u, lr_e, li_e, ldt_e, bre_e, bim_e, cre_e, cim_e, d_skip)


def _ssm_bwd(u, y1, dy2a, dy2b, st_r, st_i, ends_r, ends_i, lr_e, li_e, ldt_e, bre_e, bim_e, cre_e, cim_e, d_skip):
    S, SW = u.shape
    T = min(SSM_CHUNK, S)
    nch, nbk = S // T, SW // LANES
    tok, par, bmat, cmat, dvec = _ssm_specs(T, nch, True)
    state_spec = pl.BlockSpec((T, STATE_LANES), lambda j, c: (nch - 1 - c, j))
    prev_spec = pl.BlockSpec((None, 1, STATE_LANES), lambda j, c: (jnp.maximum(nch - 2 - c, 0), 0, j))
    acc8 = pl.BlockSpec((None, 8, STATE_LANES), lambda j, c: (j, 0, 0))
    dd8 = pl.BlockSpec((None, 8, LANES), lambda j, c: (j, 0, 0))

    def body(u_ref, y_ref, da_ref, db_ref, sr_ref, si_ref, pr_ref, pi_ref, lr_ref, li_ref, ldt_ref,
             bre_ref, bim_ref, cre_ref, cim_ref, d_ref,
             du_ref, dar_ref, dai_ref, dcr_ref, dci_ref, dbr_ref, dbi_ref, ddk_ref,
             bbr, bbi, steps, pw, carry):
        c = pl.program_id(1)

        @pl.when(c == 0)
        def _():
            lr, li, ldt = lr_ref[...], li_ref[...], ldt_ref[...]
            ar, ai, kr, ki = _ssm_disc(lr, li, ldt)
            for k, (mr, mi) in enumerate(_ssm_step_tables(ar, -ai, True)):
                steps[0, k], steps[1, k] = mr, mi
            bbr[...] = (kr * bre_ref[...] - ki * bim_ref[...]).astype(BF16)
            bbi[...] = (kr * bim_ref[...] + ki * bre_ref[...]).astype(BF16)
            pw[0], pw[1] = _ssm_tile_powers(lr, li, ldt, True)
            carry[...] = jnp.zeros_like(carry)
            for ref in (dar_ref, dai_ref, dcr_ref, dci_ref, dbr_ref, dbi_ref, ddk_ref):
                ref[...] = jnp.zeros_like(ref)

        u_ = u_ref[...]
        ub = u_.astype(BF16)
        dy1 = (da_ref[...] + db_ref[...]) * _gelu_grad(y_ref[...])
        dyb = dy1.astype(BF16)

        sr, si = sr_ref[...], si_ref[...]
        has_prev = c < nch - 1
        s0r = jnp.where(has_prev, pr_ref[...], 0.0)
        s0i = jnp.where(has_prev, pi_ref[...], 0.0)

        cre_b, cim_b = cre_ref[...].astype(BF16), cim_ref[...].astype(BF16)
        gr, gi, cr, ci = _scan(lax.dot_general(dyb, cre_b, _NT, preferred_element_type=F32),
                               -lax.dot_general(dyb, cim_b, _NT, preferred_element_type=F32),
                               [(steps[0, k], steps[1, k]) for k in range(LOG_STEPS)],
                               pw[0], pw[1], carry[0], carry[1], True)
        carry[0], carry[1] = cr, ci

        row = lax.broadcasted_iota(jnp.int32, (T, STATE_LANES), 0)
        spr = jnp.where(row == 0, s0r, pltpu.roll(sr, 1, 0))
        spi = jnp.where(row == 0, s0i, pltpu.roll(si, 1, 0))

        def fold(a):
            return jnp.sum(a.reshape(T // 8, 8, a.shape[-1]), axis=0)

        dar_ref[...] += fold(gr * spr + gi * spi)
        dai_ref[...] += fold(gi * spr - gr * spi)
        srb, sib, grb, gib = sr.astype(BF16), si.astype(BF16), gr.astype(BF16), gi.astype(BF16)
        dcr_ref[...] += lax.dot_general(srb, dyb, _TN, preferred_element_type=F32)
        dci_ref[...] -= lax.dot_general(sib, dyb, _TN, preferred_element_type=F32)
        dbr_ref[...] += lax.dot_general(ub, grb, _TN, preferred_element_type=F32)
        dbi_ref[...] += lax.dot_general(ub, gib, _TN, preferred_element_type=F32)
        du_ref[...] = (lax.dot_general(grb, bbr[...], _NT, preferred_element_type=F32)
                       + lax.dot_general(gib, bbi[...], _NT, preferred_element_type=F32)
                       + dy1 * d_ref[...])
        ddk_ref[...] += fold(dy1 * u_)

    return pl.pallas_call(
        body, name="ssm_bwd",
        out_shape=(jax.ShapeDtypeStruct((S, SW), F32),
                   jax.ShapeDtypeStruct((nbk, 8, STATE_LANES), F32), jax.ShapeDtypeStruct((nbk, 8, STATE_LANES), F32),
                   jax.ShapeDtypeStruct((nbk, STATE_LANES, LANES), F32), jax.ShapeDtypeStruct((nbk, STATE_LANES, LANES), F32),
                   jax.ShapeDtypeStruct((nbk, LANES, STATE_LANES), F32), jax.ShapeDtypeStruct((nbk, LANES, STATE_LANES), F32),
                   jax.ShapeDtypeStruct((nbk, 8, LANES), F32)),
        grid=(nbk, nch),
        in_specs=[tok, tok, tok, tok, state_spec, state_spec, prev_spec, prev_spec, par, par, par,
                  bmat, bmat, cmat, cmat, dvec],
        out_specs=(tok, acc8, acc8, cmat, cmat, bmat, bmat, dd8),
        scratch_shapes=[pltpu.VMEM((LANES, STATE_LANES), BF16), pltpu.VMEM((LANES, STATE_LANES), BF16),
                        pltpu.VMEM((2, LOG_STEPS, TILE, STATE_LANES), F32), pltpu.VMEM((2, TILE, STATE_LANES), F32),
                        pltpu.VMEM((2, 1, STATE_LANES), F32)],
        compiler_params=_cparams("arbitrary", "arbitrary"),
    )(u, y1, dy2a, dy2b, st_r, st_i, ends_r, ends_i, lr_e, li_e, ldt_e, bre_e, bim_e, cre_e, cim_e, d_skip)


def _ssm_param_bwd(dar8, dai8, dbr_e, dbi_e, lr_e, li_e, ldt_e, bre_e, bim_e):
    nbk = lr_e.shape[0]
    par = pl.BlockSpec((None, 1, STATE_LANES), lambda j: (j, 0, 0))
    acc8 = pl.BlockSpec((None, 8, STATE_LANES), lambda j: (j, 0, 0))
    bmat = pl.BlockSpec((None, LANES, STATE_LANES), lambda j: (j, 0, 0))

    def body(dar_ref, dai_ref, dbr_ref, dbi_ref, lr_ref, li_ref, ldt_ref, bre_ref, bim_ref,
             dlr_ref, dli_ref, dldt_ref, dbre_ref, dbim_ref):
        lr, li, ldt = lr_ref[...], li_ref[...], ldt_ref[...]
        (ar, ai, kr, ki), vjp = jax.vjp(_ssm_disc, lr, li, ldt)
        dbr, dbi, bre, bim = dbr_ref[...], dbi_ref[...], bre_ref[...], bim_ref[...]
        dbre_ref[...] = kr * dbr + ki * dbi
        dbim_ref[...] = kr * dbi - ki * dbr
        dkr = _colsum(dbr * bre + dbi * bim)
        dki = _colsum(dbi * bre - dbr * bim)
        dlr, dli, dldt = vjp((_colsum(dar_ref[...]), _colsum(dai_ref[...]), dkr, dki))
        dlr_ref[...] = dlr
        dli_ref[...] = dli
        tot = jnp.broadcast_to(dldt, (8, STATE_LANES))
        sh = 1
        while sh < SSM_P:
            tot = tot + pltpu.roll(tot, STATE_LANES - sh, 1)
            sh *= 2
        dldt_ref[...] = tot[:1]

    vec = jax.ShapeDtypeStruct((nbk, 1, STATE_LANES), F32)
    mat = jax.ShapeDtypeStruct((nbk, LANES, STATE_LANES), F32)
    return pl.pallas_call(
        body, name="ssm_param_bwd", out_shape=(vec, vec, vec, mat, mat), grid=(nbk,),
        in_specs=[acc8, acc8, bmat, bmat, par, par, par, bmat, bmat],
        out_specs=(par, par, par, bmat, bmat), compiler_params=_cparams("parallel"),
    )(dar8, dai8, dbr_e, dbi_e, lr_e, li_e, ldt_e, bre_e, bim_e)


def _expand_b(b):
    G = b.shape[0]
    bt = b.transpose(0, 2, 1).reshape(G // GROUPS_PER_BLOCK, GROUPS_PER_BLOCK, SSM_C, SSM_P)
    eye = jnp.eye(GROUPS_PER_BLOCK, dtype=b.dtype)
    return (bt[:, :, :, None, :] * eye[None, :, None, :, None]).reshape(G // GROUPS_PER_BLOCK, LANES, STATE_LANES)


def _collapse_b(be):
    nbk = be.shape[0]
    eye = jnp.eye(GROUPS_PER_BLOCK, dtype=be.dtype)
    d5 = be.reshape(nbk, GROUPS_PER_BLOCK, SSM_C, GROUPS_PER_BLOCK, SSM_P)
    d4 = (d5 * eye[None, :, None, :, None]).sum(axis=3)
    return d4.transpose(0, 1, 3, 2).reshape(nbk * GROUPS_PER_BLOCK, SSM_P, SSM_C)


def _expand_c(cm):
    G = cm.shape[0]
    ct = cm.transpose(0, 2, 1).reshape(G // GROUPS_PER_BLOCK, GROUPS_PER_BLOCK, SSM_P, SSM_C)
    eye = jnp.eye(GROUPS_PER_BLOCK, dtype=cm.dtype)
    return (ct[:, :, :, None, :] * eye[None, :, None, :, None]).reshape(G // GROUPS_PER_BLOCK, STATE_LANES, LANES)


def _collapse_c(ce):
    nbk = ce.shape[0]
    eye = jnp.eye(GROUPS_PER_BLOCK, dtype=ce.dtype)
    d5 = ce.reshape(nbk, GROUPS_PER_BLOCK, SSM_P, GROUPS_PER_BLOCK, SSM_C)
    d4 = (d5 * eye[None, :, None, :, None]).sum(axis=3)
    return d4.transpose(0, 1, 3, 2).reshape(nbk * GROUPS_PER_BLOCK, SSM_C, SSM_P)


def _place():
    x, y, c = lax.axis_index("x"), lax.axis_index("y"), lax.axis_index("c")
    return x, y, c


def _other_chips(x, y):
    return [(1 - x, y), (x, 1 - y), (1 - x, 1 - y)]


_ANY = pl.BlockSpec(memory_space=pl.ANY)


_HBM = pl.BlockSpec(memory_space=pltpu.HBM)
_SEM = pl.BlockSpec(memory_space=pltpu.SEMAPHORE)
_EFFECT = pltpu.SideEffectType.DATAFLOW_SIDE_EFFECTING
_TOKEN = jax.ShapeDtypeStruct((8, LANES), F32)


def _hbm(a):
    return pltpu.with_memory_space_constraint(a, pltpu.HBM)


def _place_own(src, *, gather, name, after=None, tr=512):
    R, C = src.shape[-2:]
    tr = min(tr, R)
    x, y, _ = _place()
    me = (2 * x + y).astype(jnp.int32).reshape(1)
    extra = [] if after is None else [after]

    def body(me_ref, s_ref, *rest):
        rest[-1][...] = s_ref[...].astype(BF16)

    own = pl.BlockSpec((None, tr, C), lambda i, me_ref: (me_ref[0], i, 0))
    grid_spec = pltpu.PrefetchScalarGridSpec(
        num_scalar_prefetch=1, grid=(R // tr,),
        in_specs=([pl.BlockSpec((tr, C), lambda i, me_ref: (i, 0)) if gather else own]
                  + [pl.BlockSpec(a.shape, lambda i, me_ref: (0, 0)) for a in extra]), out_specs=own)
    return pl.pallas_call(
        body, name=name, grid_spec=grid_spec, out_shape=jax.ShapeDtypeStruct((N_CHIPS, R, C), BF16),
        compiler_params=_cparams("parallel"))(me, src, *extra)


def _exchange_copy(src_slot, land_slot, send, recv, k, j, peer, c):
    return pltpu.make_async_remote_copy(
        src_ref=src_slot, dst_ref=land_slot, send_sem=send.at[3 * k + j], recv_sem=recv.at[3 * k + j],
        device_id=(peer[0], peer[1], c), device_id_type=MESH)


def _exchange_start(lands, srcs, groups, *, name):
    n, ng = len(lands), len(groups)
    bufs = list(lands) + list(srcs)
    nb = len(bufs)

    def body(*refs):
        lnd, src, sems = refs[:n], refs[n:nb], refs[nb:nb + 2 * ng]
        token = refs[2 * nb + 2 * ng]
        x, y, c = _place()
        me = 2 * x + y
        for gi, group in enumerate(groups):
            for k, w in enumerate(group):
                for j, peer in enumerate(_other_chips(x, y)):
                    if src:
                        sent, dst = src[w].at[2 * peer[0] + peer[1]], lnd[w].at[me]
                    else:
                        sent = dst = lnd[w].at[me, c]
                    _exchange_copy(sent, dst, sems[2 * gi], sems[2 * gi + 1], k, j, peer, c).start()
        token[...] = jnp.zeros_like(token)

    sem_shapes = [pltpu.SemaphoreType.DMA((3 * len(g),)) for g in groups for _ in range(2)]
    res = pl.pallas_call(
        body, name=name,
        out_shape=sem_shapes + [pltpu.HBM(a.shape, a.dtype) for a in bufs] + [_TOKEN],
        in_specs=[_HBM] * nb,
        out_specs=[_SEM] * (2 * ng) + [_HBM] * nb + [pl.BlockSpec(memory_space=pltpu.VMEM)],
        input_output_aliases={i: 2 * ng + i for i in range(nb)},
        compiler_params=pltpu.CompilerParams(has_side_effects=_EFFECT),
    )(*[_hbm(a) for a in bufs])
    sems = [(res[2 * gi], res[2 * gi + 1]) for gi in range(ng)]
    return sems, res[2 * ng:2 * ng + n], res[2 * ng + n:2 * ng + nb], res[-1]


def _exchange_wait(lands, srcs, sems, after, *, name):
    n = len(lands)
    bufs = list(lands) + list(srcs)
    nb = len(bufs)
    send_sems, recv_sems = sems

    def body(*refs):
        lnd, src, send, recv = refs[:n], refs[n:nb], refs[nb], refs[nb + 1]
        x, y, c = _place()
        for k in range(n):
            for j, peer in enumerate(_other_chips(x, y)):
                slot = 2 * peer[0] + peer[1]
                if src:
                    copy = _exchange_copy(src[k].at[slot], lnd[k].at[slot], send, recv, k, j, peer, c)
                else:
                    copy = _exchange_copy(lnd[k].at[slot, c], lnd[k].at[slot, c], send, recv, k, j, peer, c)
                copy.wait_send()
                copy.wait_recv()

    res = pl.pallas_call(
        body, name=name, out_shape=[pltpu.HBM(a.shape, a.dtype) for a in bufs],
        in_specs=[_HBM] * nb + [_SEM, _SEM, _ANY], out_specs=[_HBM] * nb,
        input_output_aliases={i: i for i in range(nb)},
        compiler_params=pltpu.CompilerParams(has_side_effects=_EFFECT),
    )(*bufs, send_sems, recv_sems, after)
    return res[:n]


def _pair_fill(lands, *, name):
    n = len(lands)

    def body(*refs):
        ins, outs, send, recv = refs[:n], refs[n:2 * n], refs[2 * n], refs[2 * n + 1]
        x, y, c = _place()
        for w in range(n):
            for j, (px, py) in enumerate(_other_chips(x, y)):
                slot = 2 * px + py
                pltpu.make_async_remote_copy(
                    src_ref=ins[w].at[slot, c], dst_ref=outs[w].at[slot, c], send_sem=send.at[3 * w + j],
                    recv_sem=recv.at[3 * w + j], device_id=(x, y, 1 - c), device_id_type=MESH).start()
        for w in range(n):
            for j, (px, py) in enumerate(_other_chips(x, y)):
                slot = 2 * px + py
                arrival = pltpu.make_async_remote_copy(
                    src_ref=ins[w].at[slot, c], dst_ref=outs[w].at[slot, 1 - c], send_sem=send.at[3 * w + j],
                    recv_sem=recv.at[3 * w + j], device_id=(x, y, 1 - c), device_id_type=MESH)
                arrival.wait_recv()
                arrival.wait_send()

    return pl.pallas_call(
        body, name=name, out_shape=[jax.ShapeDtypeStruct(a.shape, a.dtype) for a in lands],
        in_specs=[_ANY] * n, out_specs=[_ANY] * n, input_output_aliases={i: i for i in range(n)},
        scratch_shapes=[pltpu.SemaphoreType.DMA((3 * n,)), pltpu.SemaphoreType.DMA((3 * n,))],
    )(*lands)


def _pair_copy(src, dst, send, recv, w, j, sibling):
    return pltpu.make_async_remote_copy(
        src_ref=src, dst_ref=dst, send_sem=send.at[3 * w + j], recv_sem=recv.at[3 * w + j],
        device_id=sibling, device_id_type=MESH)


def _pair_start(lands, *, name):
    n = len(lands)

    def body(*refs):
        bufs, send, recv, token = refs[:n], refs[n], refs[n + 1], refs[2 * n + 2]
        x, y, c = _place()
        for w in range(n):
            for j, (px, py) in enumerate(_other_chips(x, y)):
                half = bufs[w].at[2 * px + py, c]
                _pair_copy(half, half, send, recv, w, j, (x, y, 1 - c)).start()
        token[...] = jnp.zeros_like(token)

    res = pl.pallas_call(
        body, name=name,
        out_shape=[pltpu.SemaphoreType.DMA((3 * n,))] * 2 + [pltpu.HBM(a.shape, a.dtype) for a in lands] + [_TOKEN],
        in_specs=[_HBM] * n, out_specs=[_SEM, _SEM] + [_HBM] * n + [pl.BlockSpec(memory_space=pltpu.VMEM)],
        input_output_aliases={i: 2 + i for i in range(n)},
        compiler_params=pltpu.CompilerParams(has_side_effects=_EFFECT),
    )(*[_hbm(a) for a in lands])
    return (res[0], res[1]), res[2:2 + n], res[-1]


def _pair_wait(lands, sems, after, *, name):
    n = len(lands)

    def body(*refs):
        bufs, send, recv = refs[:n], refs[n], refs[n + 1]
        x, y, c = _place()
        for w in range(n):
            for j, (px, py) in enumerate(_other_chips(x, y)):
                slot = 2 * px + py
                copy = _pair_copy(bufs[w].at[slot, c], bufs[w].at[slot, 1 - c], send, recv, w, j, (x, y, 1 - c))
                copy.wait_send()
                copy.wait_recv()

    return pl.pallas_call(
        body, name=name, out_shape=[pltpu.HBM(a.shape, a.dtype) for a in lands],
        in_specs=[_HBM] * n + [_SEM, _SEM, _ANY], out_specs=[_HBM] * n,
        input_output_aliases={i: i for i in range(n)},
        compiler_params=pltpu.CompilerParams(has_side_effects=_EFFECT),
    )(*lands, *sems, after)


def _sum_partials(land, *, name, tr=256):
    _, R, C = land.shape
    tr = min(tr, R)

    def body(l_ref, o_ref):
        acc = l_ref[0].astype(F32)
        for k in range(1, N_CHIPS):
            acc = acc + l_ref[k].astype(F32)
        o_ref[...] = acc

    return pl.pallas_call(
        body, name=name, out_shape=jax.ShapeDtypeStruct((R, C), F32), grid=(R // tr,),
        in_specs=[pl.BlockSpec((N_CHIPS, tr, C), lambda i: (0, i, 0))], out_specs=_rows(tr, C),
        compiler_params=_cparams("parallel"))(land)


def _swap_with_sibling(sums, *, name):
    n = len(sums)

    def body(*refs):
        ins, outs = refs[:n], refs[n:2 * n]
        send_sems, recv_sems = refs[2 * n:]
        x, y, c = _place()
        copies = [pltpu.make_async_remote_copy(
            src_ref=ins[w], dst_ref=outs[w], send_sem=send_sems.at[w], recv_sem=recv_sems.at[w],
            device_id=(x, y, 1 - c), device_id_type=MESH) for w in range(n)]
        for cp in copies:
            cp.start()
        for cp in copies:
            cp.wait_recv()
            cp.wait_send()

    return pl.pallas_call(
        body, name=name,
        out_shape=[jax.ShapeDtypeStruct(s.shape, s.dtype) for s in sums],
        in_specs=[_ANY] * n, out_specs=[_ANY] * n,
        scratch_shapes=[pltpu.SemaphoreType.DMA((n,)), pltpu.SemaphoreType.DMA((n,))],
    )(*sums)


def _swap_start(sums, *, name):
    n = len(sums)
    bufs = list(sums) + [lax.empty(s.shape, s.dtype) for s in sums]

    def body(*refs):
        src, lnd, send, recv, token = refs[:n], refs[n:2 * n], refs[2 * n], refs[2 * n + 1], refs[4 * n + 2]
        x, y, c = _place()
        for w in range(n):
            pltpu.make_async_remote_copy(
                src_ref=src[w], dst_ref=lnd[w], send_sem=send.at[w], recv_sem=recv.at[w],
                device_id=(x, y, 1 - c), device_id_type=MESH).start()
        token[...] = jnp.zeros_like(token)

    res = pl.pallas_call(
        body, name=name,
        out_shape=[pltpu.SemaphoreType.DMA((n,))] * 2 + [pltpu.HBM(a.shape, a.dtype) for a in bufs] + [_TOKEN],
        in_specs=[_HBM] * (2 * n),
        out_specs=[_SEM, _SEM] + [_HBM] * (2 * n) + [pl.BlockSpec(memory_space=pltpu.VMEM)],
        input_output_aliases={i: 2 + i for i in range(2 * n)},
        compiler_params=pltpu.CompilerParams(has_side_effects=_EFFECT),
    )(*[_hbm(a) for a in bufs])
    return (res[0], res[1]), res[2:2 + n], res[2 + n:2 + 2 * n], res[-1]


def _swap_wait(sums, lands, sems, after, *, name):
    n = len(sums)

    def body(*refs):
        src, lnd, send, recv = refs[:n], refs[n:2 * n], refs[2 * n], refs[2 * n + 1]
        x, y, c = _place()
        for w in range(n):
            copy = pltpu.make_async_remote_copy(
                src_ref=src[w], dst_ref=lnd[w], send_sem=send.at[w], recv_sem=recv.at[w],
                device_id=(x, y, 1 - c), device_id_type=MESH)
            copy.wait_send()
            copy.wait_recv()

    bufs = list(sums) + list(lands)
    res = pl.pallas_call(
        body, name=name, out_shape=[pltpu.HBM(a.shape, a.dtype) for a in bufs],
        in_specs=[_HBM] * (2 * n) + [_SEM, _SEM, _ANY], out_specs=[_HBM] * (2 * n),
        input_output_aliases={i: i for i in range(2 * n)},
        compiler_params=pltpu.CompilerParams(has_side_effects=_EFFECT),
    )(*bufs, *sems, after)
    return res[:n], res[n:]


def _adamw_math(w, g, m, v):
    m = ADAM_B1 * m + (1.0 - ADAM_B1) * g
    v = ADAM_B2 * v + (1.0 - ADAM_B2) * (g * g)
    m_hat = m / (1.0 - ADAM_B1 ** ADAM_STEP)
    v_hat = v / (1.0 - ADAM_B2 ** ADAM_STEP)
    delta = -ADAM_LR * (m_hat / (jnp.sqrt(v_hat) + ADAM_EPS) + ADAM_WD * w)
    return delta, m, v


def _adamw_pair(mine, theirs, w, m, v, *, name, tr=128):
    R, C = w.shape
    tr = min(tr, R)

    def body(a_ref, b_ref, w_ref, m_ref, v_ref, g_ref, d_ref, nm_ref, nv_ref):
        g = a_ref[...] + b_ref[...]
        g_ref[...] = g
        d_ref[...], nm_ref[...], nv_ref[...] = _adamw_math(w_ref[...], g, m_ref[...], v_ref[...])

    shape = jax.ShapeDtypeStruct((R, C), F32)
    return pl.pallas_call(
        body, name=name, out_shape=(shape,) * 4, grid=(R // tr,),
        in_specs=[_rows(tr, C)] * 5, out_specs=(_rows(tr, C),) * 4,
        compiler_params=_cparams("parallel"))(mine, theirs, w, m, v)


def _all_reduce_small(packed):
    R = packed.shape[0]
    half = R // 2

    def body(x_ref, g_ref, sib_ref, pair_ref, land_ref, send_sems, recv_sems):
        x, y, c = _place()
        me = 2 * x + y
        sibling = (x, y, 1 - c)

        swap = pltpu.make_async_remote_copy(
            src_ref=x_ref, dst_ref=sib_ref, send_sem=send_sems.at[0], recv_sem=recv_sems.at[0],
            device_id=sibling, device_id_type=MESH)
        swap.start()
        swap.wait()
        mine, theirs = x_ref[...], sib_ref[...]
        south = c == 0
        pair_ref[...] = jnp.where(south, mine, theirs) + jnp.where(south, theirs, mine)

        land_ref[me] = pair_ref[c]
        for j, (px, py) in enumerate(_other_chips(x, y)):
            pltpu.make_async_remote_copy(
                src_ref=pair_ref.at[c], dst_ref=land_ref.at[me], send_sem=send_sems.at[1 + j],
                recv_sem=recv_sems.at[1 + j], device_id=(px, py, c), device_id_type=MESH).start()
        for j, (px, py) in enumerate(_other_chips(x, y)):
            arrival = pltpu.make_async_remote_copy(
                src_ref=pair_ref.at[c], dst_ref=land_ref.at[2 * px + py], send_sem=send_sems.at[1 + j],
                recv_sem=recv_sems.at[1 + j], device_id=(px, py, c), device_id_type=MESH)
            arrival.wait_recv()
            arrival.wait_send()
        total = land_ref[0]
        for k in range(1, N_CHIPS):
            total = total + land_ref[k]
        g_ref[c] = total

        give = pltpu.make_async_remote_copy(
            src_ref=g_ref.at[c], dst_ref=g_ref.at[c], send_sem=send_sems.at[4], recv_sem=recv_sems.at[4],
            device_id=sibling, device_id_type=MESH)
        give.start()
        take = pltpu.make_async_remote_copy(
            src_ref=g_ref.at[c], dst_ref=g_ref.at[1 - c], send_sem=send_sems.at[4], recv_sem=recv_sems.at[4],
            device_id=sibling, device_id_type=MESH)
        take.wait_recv()
        give.wait_send()

    vm = pl.BlockSpec(memory_space=pltpu.VMEM)
    return pl.pallas_call(
        body, name="all_reduce_small", out_shape=jax.ShapeDtypeStruct((2, half, LANES), F32),
        in_specs=[vm], out_specs=vm,
        scratch_shapes=[pltpu.VMEM((2, half, LANES), F32), pltpu.VMEM((2, half, LANES), F32),
                        pltpu.VMEM((N_CHIPS, half, LANES), F32),
                        pltpu.SemaphoreType.DMA((5,)), pltpu.SemaphoreType.DMA((5,))],
        compiler_params=pltpu.CompilerParams(vmem_limit_bytes=VMEM_LIMIT_BYTES),
    )(packed.reshape(2, half, LANES)).reshape(R, LANES)


def _adamw_small(g, w, m, v):
    R = g.shape[0]
    tr = PACK_ROWS

    def body(g_ref, w_ref, m_ref, v_ref, d_ref, nm_ref, nv_ref):
        d_ref[...], nm_ref[...], nv_ref[...] = _adamw_math(w_ref[...], g_ref[...], m_ref[...], v_ref[...])

    shape = jax.ShapeDtypeStruct((R, LANES), F32)
    return pl.pallas_call(
        body, name="adamw_small", out_shape=(shape,) * 3, grid=(R // tr,),
        in_specs=[_rows(tr, LANES)] * 4, out_specs=(_rows(tr, LANES),) * 3,
        compiler_params=_cparams("parallel"))(g, w, m, v)


def _pack(arrays):
    parts, layout = [], []
    for a in arrays:
        n = a.size
        rows = -(-n // (8 * LANES)) * 8
        flat = jnp.pad(a.reshape(-1).astype(F32), (0, rows * LANES - n))
        parts.append(flat.reshape(rows, LANES))
        layout.append((rows, n, a.shape))
    total = sum(r for r, _, _ in layout)
    parts.append(jnp.zeros((-total % PACK_ROWS, LANES), F32))
    return jnp.concatenate(parts, axis=0), layout


def _unpack(buf, layout):
    out, r0 = [], 0
    for rows, n, shape in layout:
        out.append(buf[r0:r0 + rows].reshape(-1)[:n].reshape(shape))
        r0 += rows
    return out


SMALL = ("mix_norm_pre", "lam_re", "lam_im", "log_dt", "ssm_b_re", "ssm_b_im", "ssm_c_re", "ssm_c_im",
         "ssm_d", "b_glu", "attn_out_norm", "ssm_out_norm", "mix_norm_post", "mlp_norm_pre",
         "mlp_norm_post", "ple_norm_pre", "ple_norm_post")
BIG = ("w_in", "w_glu", "w_out", "w_up", "w_down", "w_ple_gate", "w_ple_proj")
WEIGHTS = ("mix_norm_pre", "w_in", "lam_re", "lam_im", "log_dt", "ssm_b_re", "ssm_b_im", "ssm_c_re",
           "ssm_c_im", "ssm_d", "w_glu", "b_glu", "attn_out_norm", "ssm_out_norm", "w_out",
           "mix_norm_post", "mlp_norm_pre", "w_up", "w_down", "mlp_norm_post", "ple_norm_pre",
           "w_ple_gate", "w_ple_proj", "ple_norm_post")


def kernel(x, p, mix_norm_pre, w_in, lam_re, lam_im, log_dt, ssm_b_re, ssm_b_im, ssm_c_re, ssm_c_im, ssm_d, w_glu, b_glu, attn_out_norm, ssm_out_norm, w_out, mix_norm_post, mlp_norm_pre, w_up, w_down, mlp_norm_post, ple_norm_pre, w_ple_gate, w_ple_proj, ple_norm_post, loss_target, m_mix_norm_pre, m_w_in, m_lam_re, m_lam_im, m_log_dt, m_ssm_b_re, m_ssm_b_im, m_ssm_c_re, m_ssm_c_im, m_ssm_d, m_w_glu, m_b_glu, m_attn_out_norm, m_ssm_out_norm, m_w_out, m_mix_norm_post, m_mlp_norm_pre, m_w_up, m_w_down, m_mlp_norm_post, m_ple_norm_pre, m_w_ple_gate, m_w_ple_proj, m_ple_norm_post, v_mix_norm_pre, v_w_in, v_lam_re, v_lam_im, v_log_dt, v_ssm_b_re, v_ssm_b_im, v_ssm_c_re, v_ssm_c_im, v_ssm_d, v_w_glu, v_b_glu, v_attn_out_norm, v_ssm_out_norm, v_w_out, v_mix_norm_post, v_mlp_norm_pre, v_w_up, v_w_down, v_mlp_norm_post, v_ple_norm_pre, v_w_ple_gate, v_w_ple_proj, v_ple_norm_post):
    args = dict(locals())
    W = {n: args[n][0] for n in WEIGHTS}
    Mo = {n: args["m_" + n][0] for n in WEIGHTS}
    Vo = {n: args["v_" + n][0] for n in WEIGHTS}
    xs, ps, tgt = x[0], p[0, 0], loss_target[0]
    S, D = xs.shape
    SW = W["ssm_d"].shape[0]
    AW = W["attn_out_norm"].shape[0]
    heads = AW // HEAD_DIM
    G = SW // SSM_C
    nbk = SW // LANES
    assert W["w_in"].shape[1] * N_CHIPS == 3 * AW + SW and AW == SW

    row = lambda a: a.reshape(1, -1)

    ag_groups = (("w_in",), ("w_glu", "w_out"), ("w_up",), ("w_down", "w_ple_gate", "w_ple_proj"))
    ag_names = [n for g in ag_groups for n in g]
    def in_halves(a):
        return a.reshape(N_CHIPS, 2, a.shape[1] // 2, a.shape[2])

    def placed(n, after=None):
        return in_halves(_place_own(W[n], gather=True, name="ag_place_" + n, after=after))

    first_sems, first_land, _, first_token = _exchange_start([placed("w_in")], [], [[0]], name="ag_start_first")
    rest_sems, rest_land, _, ag_token = _exchange_start(
        [placed(n, first_token) for n in ag_names[1:]], [],
        [[ag_names.index(n) - 1 for n in g] for g in ag_groups[1:]], name="ag_start")
    ag_sems, ag_land = first_sems + rest_sems, list(first_land) + list(rest_land)

    def fetched(gi, after):
        return _exchange_wait([ag_land[ag_names.index(n)] for n in ag_groups[gi]], [], ag_sems[gi], after,
                              name=f"ag_wait_{gi}")

    def whole(gis, bufs):
        names = [n for gi in gis for n in ag_groups[gi]]
        return {n: a.reshape(N_CHIPS, -1, a.shape[-1]) for n, a in zip(names, bufs)}

    lr_e = W["lam_re"].reshape(nbk, 1, STATE_LANES)
    li_e = W["lam_im"].reshape(nbk, 1, STATE_LANES)
    ldt_e = jnp.repeat(W["log_dt"], SSM_P).reshape(nbk, 1, STATE_LANES)
    bre_e, bim_e = _expand_b(W["ssm_b_re"]), _expand_b(W["ssm_b_im"])
    cre_e, cim_e = _expand_c(W["ssm_c_re"]), _expand_c(W["ssm_c_im"])
    d_row = row(W["ssm_d"])

    hn1 = _norm_cast(xs, row(W["mix_norm_pre"]) + ag_token[0, 0], name="norm_in")
    w_in_f = whole([0], _pair_fill(fetched(0, hn1), name="ag_pair_0"))["w_in"]
    qkv_b = _proj_qkv(hn1, w_in_f)
    outs, lses = zip(*[_attn_fwd(qb, d, heads) for d, qb in zip(DILATIONS, qkv_b)])
    pair_a_sems, pair_a, pair_a_token = _pair_start(fetched(1, outs[-1]), name="ag_pair_start_a")
    u = _matmul(hn1, w_in_f, name="proj_u", b_shards=N_CHIPS, b_cols=(3 * AW, SW), after=pair_a_token)
    y1, y2b, st_r, st_i, ends_r, ends_i = _ssm_fwd(u, lr_e, li_e, ldt_e, bre_e, bim_e, cre_e, cim_e, d_row)
    pair_b_sems, pair_b, pair_b_token = _pair_start(fetched(2, y2b), name="ag_pair_start_b")
    full = whole([1], _pair_wait(pair_a, pair_a_sems, y2b, name="ag_pair_wait_a"))
    w_glu_f = full["w_glu"].reshape(SW, SW)
    w_out_f = full["w_out"].reshape(AW + SW, D)
    z = _matmul(y2b, w_glu_f, name="glu_z", after=pair_b_token)
    attn, lse_b, mixed = _mix_fwd(outs, lses, y1, z, row(W["b_glu"]), row(W["attn_out_norm"]), row(W["ssm_out_norm"]))
    mo = _matmul(mixed, w_out_f, name="mix_out")
    h1, hn2 = _res_norm(xs, mo, row(W["mix_norm_post"]), row(W["mlp_norm_pre"]), name="res_mix")
    w_up_f = whole([2], _pair_wait(pair_b, pair_b_sems, hn2, name="ag_pair_wait_b"))["w_up"]
    up, act = _matmul(hn2, w_up_f, name="mlp_up", b_shards=N_CHIPS, relu2=True)
    full = whole([3], _pair_fill(fetched(3, act), name="ag_pair_3"))
    w_down_f = full["w_down"].reshape(-1, D)
    w_pg_f = full["w_ple_gate"].reshape(D, D)
    w_pp_f = full["w_ple_proj"]
    ff = _matmul(act, w_down_f, name="mlp_down")
    h2, hn3 = _res_norm(h1, ff, row(W["mlp_norm_post"]), row(W["ple_norm_pre"]), name="res_mlp")
    gl = _matmul(hn3, w_pg_f, name="ple_gate")
    e = _matmul(ps.astype(BF16), w_pp_f, name="ple_proj", b_shards=N_CHIPS)

    dh3, dgl, de, loss_part, dg_ple_post = _final(h2, gl, e, row(W["ple_norm_post"]), tgt)
    gW = {}
    out_g, out_d, out_m, out_v = {}, {}, {}, {}

    def scatter_start(names, tag):
        parts = [gW[n] if gW[n].ndim == 3 else gW[n].reshape((N_CHIPS, -1, gW[n].shape[1])) for n in names]
        sems, land, src, token = _exchange_start(
            [_place_own(part, gather=False, name="rs_place_" + n) for n, part in zip(names, parts)], parts,
            [list(range(len(names)))], name=f"rs_start_{tag}")
        return (names, sems[0], land, src), token

    def scatter_sums(batches, after):
        names, sums = [], []
        for tag, (batch_names, sems, land, src) in batches:
            landed = _exchange_wait(land, src, sems, after, name=f"rs_wait_{tag}")
            names += batch_names
            sums += [_sum_partials(l, name="sum_" + n) for n, l in zip(batch_names, landed)]
        return names, sums

    def apply(names, sums, theirs):
        for n, a, b in zip(names, sums, theirs):
            out_g[n], out_d[n], out_m[n], out_v[n] = _adamw_pair(a, b, W[n], Mo[n], Vo[n], name="adamw_" + n)

    def swap_begin(batches, after, tag):
        names, sums = scatter_sums(batches, after)
        sems, sums, lands, token = _swap_start(sums, name=f"swap_start_{tag}")
        return (names, sems, sums, lands), token

    def swap_end(swap, after, tag):
        names, sems, sums, lands = swap
        sums, theirs = _swap_wait(sums, lands, sems, after, name=f"swap_wait_{tag}")
        apply(names, sums, theirs)

    def scatter_finish(batch, after, tag):
        names, sums = scatter_sums([(tag, batch)], after)
        apply(names, sums, _swap_with_sibling(sums, name=f"swap_{tag}"))

    gW["w_ple_proj"] = _matmul(ps.astype(BF16), de, name="d_w_ple_proj", ta=True, out_dtype=BF16, out_shards=N_CHIPS)
    gW["w_ple_gate"] = _matmul(hn3, dgl, name="d_w_ple_gate", ta=True, out_dtype=BF16)
    dhn3 = _matmul(dgl, w_pg_f, name="d_hn3", tb=True)
    dh2, dff, dg_ple_pre, dg_mlp_post = _bwd_res_norm(
        dh3, dhn3, h2, row(W["ple_norm_pre"]), ff, row(W["mlp_norm_post"]), name="bwd_res_mlp")
    gW["w_down"] = _matmul(act, dff, name="d_w_down", ta=True, out_dtype=BF16)
    batch1, token1 = scatter_start(("w_ple_proj", "w_ple_gate", "w_down"), 1)
    dup = _matmul(dff, w_down_f, name="d_up", tb=True, after=token1, relu2_of=up, out_dtype=BF16)
    gW["w_up"] = _matmul(hn2, dup, name="d_w_up", ta=True, out_dtype=BF16, out_shards=N_CHIPS)
    batch2, token2 = scatter_start(("w_up",), 2)
    dhn2 = _matmul(dup, w_up_f, name="d_hn2", tb=True, b_shards=N_CHIPS, after=token2)
    dh1, dmo, dg_mlp_pre, dg_mix_post = _bwd_res_norm(
        dh2, dhn2, h1, row(W["mlp_norm_pre"]), mo, row(W["mix_norm_post"]), name="bwd_res_mix")
    gW["w_out"] = _matmul(mixed, dmo, name="d_w_out", ta=True, out_dtype=BF16)
    dmixed = _matmul(dmo, w_out_f, name="d_mixed", tb=True)
    dattn_b, dd_b, dz, dy2a, dg_attn, dg_ssm, db_glu = _mix_bwd(
        dmixed, attn, y1, z, row(W["b_glu"]), row(W["attn_out_norm"]), row(W["ssm_out_norm"]))
    gW["w_glu"] = _matmul(y2b, dz, name="d_w_glu", ta=True, out_dtype=BF16)
    batch3, token3 = scatter_start(("w_out", "w_glu"), 3)
    dy2b = _matmul(dz, w_glu_f, name="d_y2", tb=True, after=token3)
    du, dar8, dai8, dcr_e, dci_e, dbr_e, dbi_e, dd8 = _ssm_bwd(
        u, y1, dy2a, dy2b, st_r, st_i, ends_r, ends_i, lr_e, li_e, ldt_e, bre_e, bim_e, cre_e, cim_e, d_row)
    swap_a, token_a = swap_begin([(1, batch1)], du, "a")
    dlr_e, dli_e, dldt_e, dbre_e, dbim_e = _ssm_param_bwd(dar8, dai8, dbr_e, dbi_e, lr_e, li_e, ldt_e, bre_e, bim_e)

    dqs, dks, dvs = zip(*[_attn_bwd(qb, da, l, dd_, d, heads, token_a)
                          for d, qb, da, l, dd_ in zip(DILATIONS, qkv_b, dattn_b, lse_b, dd_b)])
    dproj = _dproj_join(dqs, dks, dvs, du)
    swap_end(swap_a, dproj, "a")
    swap_b, token_b = swap_begin([(2, batch2), (3, batch3)], dproj, "b")
    gW["w_in"] = _matmul(hn1, dproj, name="d_w_in", ta=True, out_dtype=BF16, out_shards=N_CHIPS, after=token_b)
    batch4, token4 = scatter_start(("w_in",), 4)
    dhn1 = _matmul(dproj, w_in_f, name="d_hn1", tb=True, b_shards=N_CHIPS, after=token4)
    grad_x, dg_mix_pre = _bwd_first(dh1, dhn1, xs, row(W["mix_norm_pre"]))
    swap_end(swap_b, grad_x, "b")
    scatter_finish(batch4, grad_x, 4)

    small_g = {
        "mix_norm_pre": dg_mix_pre, "lam_re": dlr_e.reshape(G, SSM_P), "lam_im": dli_e.reshape(G, SSM_P),
        "log_dt": dldt_e.reshape(G, SSM_P)[:, 0], "ssm_b_re": _collapse_b(dbre_e), "ssm_b_im": _collapse_b(dbim_e),
        "ssm_c_re": _collapse_c(dcr_e), "ssm_c_im": _collapse_c(dci_e), "ssm_d": dd8.sum(axis=1).reshape(-1),
        "b_glu": db_glu, "attn_out_norm": dg_attn, "ssm_out_norm": dg_ssm, "mix_norm_post": dg_mix_post,
        "mlp_norm_pre": dg_mlp_pre, "mlp_norm_post": dg_mlp_post, "ple_norm_pre": dg_ple_pre,
        "ple_norm_post": dg_ple_post,
    }
    g_pack, layout = _pack([small_g[n].reshape(W[n].shape) for n in SMALL])
    w_pack, _ = _pack([W[n] for n in SMALL])
    m_pack, _ = _pack([Mo[n] for n in SMALL])
    v_pack, _ = _pack([Vo[n] for n in SMALL])
    g_sum = _all_reduce_small(g_pack)
    packed = (g_sum,) + tuple(_adamw_small(g_sum, w_pack, m_pack, v_pack))
    for dst, buf in zip((out_g, out_d, out_m, out_v), packed):
        dst.update(zip(SMALL, _unpack(buf, layout)))

    loss = lax.psum(loss_part[0, 0], ("x", "y", "c"))
    lead = lambda a: a[None]
    return (loss, grad_x[None],
            *[lead(out_g[n]) for n in WEIGHTS], *[lead(out_d[n]) for n in WEIGHTS],
            *[lead(out_m[n]) for n in WEIGHTS], *[lead(out_v[n]) for n in WEIGHTS])
```

```python
import functools
import math

import jax
import jax.numpy as jnp
from jax import lax
from jax.experimental import pallas as pl
from jax.experimental.pallas import tpu as pltpu

F32 = jnp.float32
BF16 = jnp.bfloat16
MESH = pl.DeviceIdType.MESH

RMS_EPS = 1e-6
NEG_INF = -1e30
HEAD_DIM = 128
BLK = 128
DILATIONS = (1, 4, 16)
ATTN_LOOKAHEAD = 3
SSM_C = 16
SSM_P = 64
LANES = 128
GROUPS_PER_BLOCK = LANES // SSM_C
STATE_LANES = GROUPS_PER_BLOCK * SSM_P
SSM_CHUNK = 1024
TILE = 8
ADAM_LR, ADAM_B1, ADAM_B2, ADAM_EPS, ADAM_WD, ADAM_STEP = 1e-3, 0.9, 0.999, 1e-8, 0.01, 10
VMEM_LIMIT_BYTES = 56 * 1024 * 1024
MATMUL_VMEM_BYTES = 44 * 1024 * 1024
N_CHIPS = 4
N_DEV = 8
PACK_ROWS = 256


def _cparams(*sem):
    return pltpu.CompilerParams(dimension_semantics=sem or None, vmem_limit_bytes=VMEM_LIMIT_BYTES)


def _rows(tr, w):
    return pl.BlockSpec((tr, w), lambda i: (i, 0))


def _vec(w):
    return pl.BlockSpec((1, w), lambda i: (0, 0))


def _sigmoid(x):
    return 1.0 / (1.0 + jnp.exp(-x))


def _gelu(x):
    c = math.sqrt(2.0 / math.pi)
    return 0.5 * x * (1.0 + jnp.tanh(c * (x + 0.044715 * x * x * x)))


def _gelu_grad(x):
    c = math.sqrt(2.0 / math.pi)
    th = jnp.tanh(c * (x + 0.044715 * x * x * x))
    return 0.5 * (1.0 + th) + 0.5 * x * (1.0 - th * th) * c * (1.0 + 3.0 * 0.044715 * x * x)


def _rms(x, g):
    r = lax.rsqrt(jnp.mean(x * x, axis=-1, keepdims=True) + RMS_EPS)
    return x * r * g


def _rms_bwd(dy, x, g):
    r = lax.rsqrt(jnp.mean(x * x, axis=-1, keepdims=True) + RMS_EPS)
    n = x * r
    dn = dy * g
    dx = r * (dn - n * jnp.mean(dn * n, axis=-1, keepdims=True))
    return dx, dy * n


def _colsum(a):
    return jnp.sum(a, axis=0, keepdims=True)


def _first(i):
    return i == 0


def _matmul(a, b, *, name, ta=False, tb=False, out_dtype=F32, b_shards=1, out_shards=1, b_cols=None,
            after=None, relu2=False, relu2_of=None, tm=1024, tn=2048, tk=2048):
    if ta:
        K, M = a.shape
    else:
        M, K = a.shape
    if b_shards > 1:
        rows, cols = b.shape[1], b.shape[2] * b_shards
    else:
        rows, cols = b.shape
    N, Kb = (rows, cols) if tb else (cols, rows)
    assert K == Kb, (a.shape, b.shape, ta, tb)
    col0 = 0
    if b_cols is not None:
        assert not tb
        col0, N = b_cols
    tm, tn, tk = min(tm, M), min(tn, N), min(tk, K)
    if b_shards > 1:
        shard_cols = cols // b_shards
        if tb:
            tk = min(tk, shard_cols)
        else:
            tn = min(tn, shard_cols)
    if out_shards > 1:
        tn = min(tn, N // out_shards)

    def vmem_bytes(tn_):
        out_bytes = jnp.dtype(out_dtype).itemsize + (2 if relu2 else 0)
        return (4 * (tm * tk + tk * tn_) + 2 * tm * tn_ * out_bytes
                + (2 * relu2_of.dtype.itemsize * tm * tn_ if relu2_of is not None else 0)
                + (4 * tm * tn_ if K > tk else 0))

    while vmem_bytes(tn) > MATMUL_VMEM_BYTES and tn > LANES and col0 % (tn // 2) == 0:
        tn //= 2
    assert M % tm == 0 and N % tn == 0 and K % tk == 0 and col0 % tn == 0
    nk = K // tk
    j0 = col0 // tn

    a_spec = (pl.BlockSpec((tk, tm), lambda i, j, k: (k, i)) if ta
              else pl.BlockSpec((tm, tk), lambda i, j, k: (i, k)))
    if b_shards > 1:
        if tb:
            per = shard_cols // tk
            b_spec = pl.BlockSpec((None, tn, tk), lambda i, j, k: (k // per, j, k % per))
        else:
            per = shard_cols // tn
            b_spec = pl.BlockSpec((None, tk, tn), lambda i, j, k: ((j + j0) // per, k, (j + j0) % per))
    else:
        b_spec = (pl.BlockSpec((tn, tk), lambda i, j, k: (j, k)) if tb
                  else pl.BlockSpec((tk, tn), lambda i, j, k: (k, j + j0)))
    if out_shards > 1:
        per_o = (N // out_shards) // tn
        out_shape = jax.ShapeDtypeStruct((out_shards, M, N // out_shards), out_dtype)
        out_spec = pl.BlockSpec((None, tm, tn), lambda i, j, k: (j // per_o, i, j % per_o))
    else:
        out_shape = jax.ShapeDtypeStruct((M, N), out_dtype)
        out_spec = pl.BlockSpec((tm, tn), lambda i, j, k: (i, j))
    dims = (((0 if ta else 1,), (1 if tb else 0,)), ((), ()))

    extra, extra_specs = [], []
    if relu2_of is not None:
        assert out_shards == 1 and relu2_of.shape == (M, N)
        extra.append(relu2_of)
        extra_specs.append(pl.BlockSpec((tm, tn), lambda i, j, k: (i, j)))
    if after is not None:
        extra.append(after)
        extra_specs.append(pl.BlockSpec(after.shape, lambda i, j, k: (0, 0)))
    n_in = 2 + len(extra)
    if relu2:
        assert out_shards == 1
        out_shape = (out_shape, jax.ShapeDtypeStruct((M, N), BF16))
        out_spec = (out_spec, out_spec)

    def finish(acc, refs):
        o_ref = refs[n_in]
        if relu2_of is not None:
            acc = acc * (2.0 * jnp.maximum(refs[2][...].astype(F32), 0.0))
        o_ref[...] = acc.astype(o_ref.dtype)
        if relu2:
            r = jnp.maximum(acc, 0.0)
            refs[n_in + 1][...] = (r * r).astype(BF16)

    def body(*refs):
        prod = lax.dot_general(refs[0][...], refs[1][...], dims, preferred_element_type=F32)
        if nk == 1:
            finish(prod, refs)
            return
        acc_ref = refs[-1]
        k = pl.program_id(2)

        @pl.when(k == 0)
        def _():
            acc_ref[...] = prod

        @pl.when(k > 0)
        def _():
            acc_ref[...] += prod

        @pl.when(k == nk - 1)
        def _():
            finish(acc_ref[...], refs)

    return pl.pallas_call(
        body, name=name, out_shape=out_shape, grid=(M // tm, N // tn, nk),
        in_specs=[a_spec, b_spec] + extra_specs, out_specs=out_spec,
        scratch_shapes=[pltpu.VMEM((tm, tn), F32)] if nk > 1 else [],
        compiler_params=_cparams("parallel", "parallel", "arbitrary"),
    )(a, b, *extra)


def _norm_cast(x, g, *, name, tr=256):
    S, D = x.shape
    tr = min(tr, S)

    def body(x_ref, g_ref, o_ref):
        o_ref[...] = _rms(x_ref[...], g_ref[...]).astype(BF16)

    return pl.pallas_call(
        body, name=name, out_shape=jax.ShapeDtypeStruct((S, D), BF16), grid=(S // tr,),
        in_specs=[_rows(tr, D), _vec(D)], out_specs=_rows(tr, D),
        compiler_params=_cparams("parallel"))(x, g)


def _res_norm(res, y, g_post, g_next, *, name, tr=256):
    S, D = res.shape
    tr = min(tr, S)

    def body(res_ref, y_ref, gp_ref, gn_ref, h_ref, hn_ref):
        h = res_ref[...] + _rms(y_ref[...], gp_ref[...])
        h_ref[...] = h
        hn_ref[...] = _rms(h, gn_ref[...]).astype(BF16)

    return pl.pallas_call(
        body, name=name,
        out_shape=(jax.ShapeDtypeStruct((S, D), F32), jax.ShapeDtypeStruct((S, D), BF16)),
        grid=(S // tr,), in_specs=[_rows(tr, D), _rows(tr, D), _vec(D), _vec(D)],
        out_specs=(_rows(tr, D), _rows(tr, D)), compiler_params=_cparams("parallel"))(res, y, g_post, g_next)


def _residue_spec(tr, d, w):
    return pl.BlockSpec((tr // d, d * w), lambda i: (i, 0))


def _residue_shape(S, d, w, dtype):
    return jax.ShapeDtypeStruct((S // d, d * w), dtype)


def _residue_scratch(rows, w):
    return pltpu.VMEM((w // LANES, rows, LANES), F32)


def _fill_strips(scr, val):
    for s in range(scr.shape[0]):
        scr[s] = val[:, s * LANES:(s + 1) * LANES]


def _strips_to_residues(scr, o_ref, d):
    strips, rows, _ = scr.shape
    for r in range(d):
        for s in range(strips):
            col = (r * strips + s) * LANES
            o_ref[:, col:col + LANES] = scr[s, pl.ds(r, rows // d, stride=d), :].astype(o_ref.dtype)


def _to_residues(scr, val, o_ref, d):
    if d == 1:
        o_ref[...] = val.astype(o_ref.dtype)
        return
    _fill_strips(scr, val)
    _strips_to_residues(scr, o_ref, d)


def _from_residues(scr, in_ref, d):
    if d == 1:
        return in_ref[...].astype(F32)
    strips, rows, _ = scr.shape
    for r in range(d):
        for s in range(strips):
            col = (r * strips + s) * LANES
            scr[s, pl.ds(r, rows // d, stride=d), :] = in_ref[:, col:col + LANES].astype(F32)
    return jnp.concatenate([scr[s] for s in range(strips)], axis=1)


def _spread_heads(packed, heads, width=HEAD_DIM):
    per = LANES // heads
    return jnp.concatenate([jnp.broadcast_to(packed[:, h * per:h * per + 1], (packed.shape[0], width))
                            for h in range(heads)], axis=1)


def _mix_fwd(os, ls, y1, z, b_glu, g_attn, g_ssm, *, tr=128):
    S, SW = y1.shape
    AW = os[0].shape[1] // DILATIONS[0]
    heads = AW // HEAD_DIM
    tr = min(tr, S)
    nd = len(DILATIONS)

    def body(*refs):
        o_refs, l_refs = refs[:nd], refs[nd:2 * nd]
        y_ref, z_ref, b_ref, ga_ref, gs_ref, attn_ref = refs[2 * nd:2 * nd + 6]
        lse_refs = refs[2 * nd + 6:3 * nd + 6]
        mixed_ref, scr, scr_p = refs[3 * nd + 6:]
        ls_ = [_from_residues(scr_p, l_refs[n], d) for n, d in enumerate(DILATIONS)]
        m = functools.reduce(jnp.maximum, ls_)
        es = [jnp.exp(l - m) for l in ls_]
        tot = functools.reduce(jnp.add, es)
        attn = functools.reduce(jnp.add, [_spread_heads(e / tot, heads) * _from_residues(scr, o_refs[n], d)
                                          for n, (e, d) in enumerate(zip(es, DILATIONS))])
        attn_ref[...] = attn
        lse = m + jnp.log(tot)
        for n, d in enumerate(DILATIONS):
            _to_residues(scr_p, lse, lse_refs[n], d)
        ssm = _gelu(y_ref[...]) * _sigmoid(z_ref[...] + b_ref[...])
        mixed_ref[:, :AW] = _rms(attn, ga_ref[...]).astype(BF16)
        mixed_ref[:, AW:] = _rms(ssm, gs_ref[...]).astype(BF16)

    res_o = [_residue_spec(tr, d, AW) for d in DILATIONS]
    res_l = [_residue_spec(tr, d, LANES) for d in DILATIONS]
    res = pl.pallas_call(
        body, name="mix_fwd",
        out_shape=([jax.ShapeDtypeStruct((S, AW), F32)] + [_residue_shape(S, d, LANES, F32) for d in DILATIONS]
                   + [jax.ShapeDtypeStruct((S, AW + SW), BF16)]),
        grid=(S // tr,),
        in_specs=res_o + res_l + [_rows(tr, SW), _rows(tr, SW), _vec(SW), _vec(AW), _vec(SW)],
        out_specs=[_rows(tr, AW)] + res_l + [_rows(tr, AW + SW)],
        scratch_shapes=[_residue_scratch(tr, AW), _residue_scratch(tr, LANES)],
        compiler_params=_cparams("parallel"))(*os, *ls, y1, z, b_glu, g_attn, g_ssm)
    return res[0], res[1:1 + nd], res[1 + nd]


def _final(h2, gl, e, g_post, target, *, tr=128):
    S, D = h2.shape
    tr = min(tr, S)

    def body(h_ref, gl_ref, e_ref, g_ref, t_ref, dh_ref, dgl_ref, de_ref, loss_ref, dg_ref):
        i = pl.program_id(0)
        gate = _sigmoid(gl_ref[...])
        e_ = e_ref[...]
        ge = gate * e_
        g = g_ref[...]
        diff = h_ref[...] + _rms(ge, g) - t_ref[...]
        dh = diff * (1.0 / D)
        dh_ref[...] = dh
        dge, dgrow = _rms_bwd(dh, ge, g)
        dgl_ref[...] = (dge * e_ * gate * (1.0 - gate)).astype(BF16)
        de_ref[...] = (dge * gate).astype(BF16)
        part = _colsum(0.5 * jnp.mean(diff * diff, axis=-1, keepdims=True))

        @pl.when(_first(i))
        def _():
            loss_ref[...] = jnp.zeros_like(loss_ref)
            dg_ref[...] = jnp.zeros_like(dg_ref)

        loss_ref[...] += part + jnp.zeros((1, LANES), F32)
        dg_ref[...] += _colsum(dgrow)

    return pl.pallas_call(
        body, name="final_fwd_bwd",
        out_shape=(jax.ShapeDtypeStruct((S, D), F32), jax.ShapeDtypeStruct((S, D), BF16),
                   jax.ShapeDtypeStruct((S, D), BF16), jax.ShapeDtypeStruct((1, LANES), F32),
                   jax.ShapeDtypeStruct((1, D), F32)),
        grid=(S // tr,),
        in_specs=[_rows(tr, D), _rows(tr, D), _rows(tr, D), _vec(D), _rows(tr, D)],
        out_specs=(_rows(tr, D), _rows(tr, D), _rows(tr, D), _vec(LANES), _vec(D)),
        compiler_params=_cparams("arbitrary"))(h2, gl, e, g_post, target)


def _bwd_res_norm(dh_out, dhn, h, g_next, y, g_post, *, name, tr=128):
    S, D = h.shape
    tr = min(tr, S)

    def body(dho_ref, dhn_ref, h_ref, gn_ref, y_ref, gp_ref, dh_ref, dy_ref, dgn_ref, dgp_ref):
        i = pl.program_id(0)
        dx, dgn_rows = _rms_bwd(dhn_ref[...], h_ref[...], gn_ref[...])
        dh = dho_ref[...] + dx
        dh_ref[...] = dh
        dy, dgp_rows = _rms_bwd(dh, y_ref[...], gp_ref[...])
        dy_ref[...] = dy.astype(BF16)

        @pl.when(_first(i))
        def _():
            dgn_ref[...] = jnp.zeros_like(dgn_ref)
            dgp_ref[...] = jnp.zeros_like(dgp_ref)

        dgn_ref[...] += _colsum(dgn_rows)
        dgp_ref[...] += _colsum(dgp_rows)

    return pl.pallas_call(
        body, name=name,
        out_shape=(jax.ShapeDtypeStruct((S, D), F32), jax.ShapeDtypeStruct((S, D), BF16),
                   jax.ShapeDtypeStruct((1, D), F32), jax.ShapeDtypeStruct((1, D), F32)),
        grid=(S // tr,),
        in_specs=[_rows(tr, D), _rows(tr, D), _rows(tr, D), _vec(D), _rows(tr, D), _vec(D)],
        out_specs=(_rows(tr, D), _rows(tr, D), _vec(D), _vec(D)),
        compiler_params=_cparams("arbitrary"))(dh_out, dhn, h, g_next, y, g_post)


def _bwd_first(dh1, dhn1, x, g1, *, tr=256):
    S, D = x.shape
    tr = min(tr, S)

    def body(dh_ref, dhn_ref, x_ref, g_ref, dx_ref, dg_ref):
        i = pl.program_id(0)
        dx, dg_rows = _rms_bwd(dhn_ref[...], x_ref[...], g_ref[...])
        dx_ref[...] = dh_ref[...] + dx

        @pl.when(_first(i))
        def _():
            dg_ref[...] = jnp.zeros_like(dg_ref)

        dg_ref[...] += _colsum(dg_rows)

    return pl.pallas_call(
        body, name="bwd_first",
        out_shape=(jax.ShapeDtypeStruct((S, D), F32), jax.ShapeDtypeStruct((1, D), F32)),
        grid=(S // tr,), in_specs=[_rows(tr, D), _rows(tr, D), _rows(tr, D), _vec(D)],
        out_specs=(_rows(tr, D), _vec(D)), compiler_params=_cparams("arbitrary"))(dh1, dhn1, x, g1)


def _mix_bwd(dmixed, attn, y1, z, b_glu, g_attn, g_ssm, *, tr=256):
    S, AW = attn.shape
    SW = y1.shape[1]
    tr = min(tr, S)
    heads = AW // HEAD_DIM
    nd = len(DILATIONS)

    def body(*refs):
        dm_ref, a_ref, y_ref, z_ref, b_ref, ga_ref, gs_ref = refs[:7]
        da_refs, dd_refs = refs[7:7 + nd], refs[7 + nd:7 + 2 * nd]
        dz_ref, dy2_ref, dga_ref, dgs_ref, db_ref, scr, scr_p, dd_scr = refs[7 + 2 * nd:]
        i = pl.program_id(0)
        attn_ = a_ref[...]
        dattn, dga_rows = _rms_bwd(dm_ref[:, :AW], attn_, ga_ref[...])
        prod = dattn * attn_
        per = LANES // heads
        for h in range(heads):
            total = jnp.sum(prod[:, h * HEAD_DIM:(h + 1) * HEAD_DIM], axis=-1, keepdims=True)
            dd_scr[:, h * per:(h + 1) * per] = jnp.broadcast_to(total, (tr, per))
        for n, d in enumerate(DILATIONS):
            _to_residues(scr, dattn, da_refs[n], d)
            _to_residues(scr_p, dd_scr[...], dd_refs[n], d)
        y2 = _gelu(y_ref[...])
        gate = _sigmoid(z_ref[...] + b_ref[...])
        dssm, dgs_rows = _rms_bwd(dm_ref[:, AW:], y2 * gate, gs_ref[...])
        dz = dssm * y2 * gate * (1.0 - gate)
        dz_ref[...] = dz.astype(BF16)
        dy2_ref[...] = dssm * gate

        @pl.when(_first(i))
        def _():
            dga_ref[...] = jnp.zeros_like(dga_ref)
            dgs_ref[...] = jnp.zeros_like(dgs_ref)
            db_ref[...] = jnp.zeros_like(db_ref)

        dga_ref[...] += _colsum(dga_rows)
        dgs_ref[...] += _colsum(dgs_rows)
        db_ref[...] += _colsum(dz)

    res_a = [_residue_spec(tr, d, AW) for d in DILATIONS]
    res_d = [_residue_spec(tr, d, LANES) for d in DILATIONS]
    res = pl.pallas_call(
        body, name="mix_bwd",
        out_shape=([_residue_shape(S, d, AW, BF16) for d in DILATIONS]
                   + [_residue_shape(S, d, LANES, F32) for d in DILATIONS]
                   + [jax.ShapeDtypeStruct((S, SW), BF16), jax.ShapeDtypeStruct((S, SW), F32),
                      jax.ShapeDtypeStruct((1, AW), F32), jax.ShapeDtypeStruct((1, SW), F32),
                      jax.ShapeDtypeStruct((1, SW), F32)]),
        grid=(S // tr,),
        in_specs=[_rows(tr, AW + SW), _rows(tr, AW), _rows(tr, SW), _rows(tr, SW), _vec(SW), _vec(AW), _vec(SW)],
        out_specs=res_a + res_d + [_rows(tr, SW), _rows(tr, SW), _vec(AW), _vec(SW), _vec(SW)],
        scratch_shapes=[_residue_scratch(tr, AW), _residue_scratch(tr, LANES), pltpu.VMEM((tr, LANES), F32)],
        compiler_params=_cparams("arbitrary"))(dmixed, attn, y1, z, b_glu, g_attn, g_ssm)
    return (res[:nd], res[nd:2 * nd]) + tuple(res[2 * nd:])


def _attn_mask2(i):
    row = lax.broadcasted_iota(jnp.int32, (BLK, 2 * BLK), 0)
    col = lax.broadcasted_iota(jnp.int32, (BLK, 2 * BLK), 1)
    return jnp.logical_and(col >= row, jnp.logical_and(col <= row + BLK, jnp.logical_or(col >= BLK, i > 0)))


_NT = (((1,), (1,)), ((), ()))
_TN = (((0,), (0,)), ((), ()))


def _attn_in_specs(width, block_of):
    def at(part, prev):
        def index(r, i):
            blk = block_of(i)
            return (part, jnp.maximum(blk - 1, 0) if prev else blk, r)
        return pl.BlockSpec((None, BLK, width), index)
    return [at(0, False), at(1, False), at(1, True), at(2, False), at(2, True)]


def _proj_qkv(hn, w_in_f, *, tm=1024):
    S, D = hn.shape
    AW = w_in_f.shape[2]
    tm = min(tm, S)

    def body(a_ref, b_ref, *rest):
        o_refs, scr = rest[:-1], rest[-1]
        prod = jnp.dot(a_ref[...], b_ref[...], preferred_element_type=F32)
        _fill_strips(scr, prod)
        for o_ref, d in zip(o_refs, DILATIONS):
            if d == 1:
                o_ref[...] = prod.astype(BF16)
            else:
                _strips_to_residues(scr, o_ref, d)

    return pl.pallas_call(
        body, name="proj_qkv",
        out_shape=[jax.ShapeDtypeStruct((3, S // d, d * AW), BF16) for d in DILATIONS], grid=(S // tm, 3),
        in_specs=[pl.BlockSpec((tm, D), lambda i, j: (i, 0)), pl.BlockSpec((None, D, AW), lambda i, j: (j, 0, 0))],
        out_specs=[pl.BlockSpec((None, tm // d, d * AW), lambda i, j: (j, i, 0)) for d in DILATIONS],
        scratch_shapes=[_residue_scratch(tm, AW)],
        compiler_params=_cparams("parallel", "parallel"))(hn, w_in_f)


def _attn_fwd(qkv, d, heads):
    M = qkv.shape[1]
    nb = M // BLK
    width = heads * HEAD_DIM
    per = LANES // heads
    scale = 1.0 / math.sqrt(HEAD_DIM)

    def body(q_ref, kc_ref, kp_ref, vc_ref, vp_ref, o_ref, l_ref):
        mask = _attn_mask2(pl.program_id(1))
        ones = jnp.ones((2 * BLK, HEAD_DIM), BF16)

        def scores(h):
            sl = slice(h * HEAD_DIM, (h + 1) * HEAD_DIM)
            k2 = jnp.concatenate([kp_ref[:, sl], kc_ref[:, sl]], axis=0)
            return lax.dot_general(q_ref[:, sl], k2, _NT, preferred_element_type=F32)

        ahead = [scores(h) for h in range(min(ATTN_LOOKAHEAD, heads))]
        for h in range(heads):
            sl = slice(h * HEAD_DIM, (h + 1) * HEAD_DIM)
            s = jnp.where(mask, ahead.pop(0) * scale, NEG_INF)
            if h + ATTN_LOOKAHEAD < heads:
                ahead.append(scores(h + ATTN_LOOKAHEAD))
            v2 = jnp.concatenate([vp_ref[:, sl], vc_ref[:, sl]], axis=0)
            m = jnp.max(jnp.maximum(s[:, :BLK], s[:, BLK:]), axis=-1, keepdims=True)
            p = jnp.exp(s - m).astype(BF16)
            tot = jnp.dot(p, ones, preferred_element_type=F32)
            o_ref[:, sl] = jnp.dot(p, v2, preferred_element_type=F32) / tot
            l_ref[:, h * per:(h + 1) * per] = m + jnp.log(tot[:, :per])

    return pl.pallas_call(
        body, name=f"attn_fwd_d{d}",
        out_shape=(jax.ShapeDtypeStruct((M, d * width), F32), jax.ShapeDtypeStruct((M, d * LANES), F32)),
        grid=(d, nb), in_specs=_attn_in_specs(width, lambda i: i),
        out_specs=(pl.BlockSpec((BLK, width), lambda r, i: (i, r)), pl.BlockSpec((BLK, LANES), lambda r, i: (i, r))),
        compiler_params=_cparams("parallel", "parallel"))(qkv, qkv, qkv, qkv, qkv)


def _attn_bwd(qkv, dattn, lse, dd, d, heads, after):
    M = qkv.shape[1]
    nb = M // BLK
    width = heads * HEAD_DIM
    per = LANES // heads
    scale = 1.0 / math.sqrt(HEAD_DIM)

    def block_of(i):
        return nb - 1 - i

    def body(q_ref, kc_ref, kp_ref, vc_ref, vp_ref, da_ref, l_ref, dd_ref, after_ref,
             dq_ref, dk_ref, dv_ref, dk_carry, dv_carry):
        @pl.when(pl.program_id(1) == 0)
        def _():
            dk_carry[...] = jnp.zeros_like(dk_carry)
            dv_carry[...] = jnp.zeros_like(dv_carry)

        mask = _attn_mask2(block_of(pl.program_id(1)))

        def products(h):
            sl = slice(h * HEAD_DIM, (h + 1) * HEAD_DIM)
            k2 = jnp.concatenate([kp_ref[:, sl], kc_ref[:, sl]], axis=0)
            v2 = jnp.concatenate([vp_ref[:, sl], vc_ref[:, sl]], axis=0)
            return (lax.dot_general(q_ref[:, sl], k2, _NT, preferred_element_type=F32),
                    lax.dot_general(da_ref[:, sl], v2, _NT, preferred_element_type=F32), k2)

        ahead = [products(h) for h in range(min(ATTN_LOOKAHEAD, heads))]
        for h in range(heads):
            sl = slice(h * HEAD_DIM, (h + 1) * HEAD_DIM)
            qk, dp, k2 = ahead.pop(0)
            if h + ATTN_LOOKAHEAD < heads:
                ahead.append(products(h + ATTN_LOOKAHEAD))
            q, da = q_ref[:, sl], da_ref[:, sl]
            lse_ = jnp.broadcast_to(l_ref[:, h * per:h * per + 1], (BLK, 2 * BLK))
            dd_ = jnp.broadcast_to(dd_ref[:, h * per:h * per + 1], (BLK, 2 * BLK))
            p = jnp.where(mask, jnp.exp(jnp.where(mask, qk * scale, NEG_INF) - lse_), 0.0)
            ds = (p * (dp - dd_) * scale).astype(BF16)
            dq_ref[:, sl] = jnp.dot(ds, k2, preferred_element_type=F32).astype(BF16)
            dk2 = lax.dot_general(ds, q, _TN, preferred_element_type=F32)
            dv2 = lax.dot_general(p.astype(BF16), da, _TN, preferred_element_type=F32)
            dk_ref[:, sl] = (dk2[BLK:] + dk_carry[:, sl]).astype(BF16)
            dv_ref[:, sl] = (dv2[BLK:] + dv_carry[:, sl]).astype(BF16)
            dk_carry[:, sl] = dk2[:BLK]
            dv_carry[:, sl] = dv2[:BLK]

    blk = pl.BlockSpec((BLK, width), lambda r, i: (block_of(i), r))
    packed = pl.BlockSpec((BLK, LANES), lambda r, i: (block_of(i), r))
    shape = jax.ShapeDtypeStruct((M, d * width), BF16)
    return pl.pallas_call(
        body, name=f"attn_bwd_d{d}", out_shape=(shape,) * 3, grid=(d, nb),
        in_specs=(_attn_in_specs(width, block_of) + [blk, packed, packed]
                  + [pl.BlockSpec(after.shape, lambda r, i: (0, 0))]), out_specs=(blk,) * 3,
        scratch_shapes=[pltpu.VMEM((BLK, width), F32), pltpu.VMEM((BLK, width), F32)],
        compiler_params=_cparams("arbitrary", "arbitrary"))(qkv, qkv, qkv, qkv, qkv, dattn, lse, dd, after)


def _dproj_join(dqs, dks, dvs, du, *, tr=256):
    S, SW = du.shape
    AW = dqs[0].shape[1]
    tr = min(tr, S)
    nd = len(DILATIONS)

    def body(*refs):
        du_ref, out_ref, scr = refs[3 * nd:]
        for part in range(3):
            total = functools.reduce(jnp.add, [_from_residues(scr, refs[part * nd + n], d)
                                               for n, d in enumerate(DILATIONS)])
            out_ref[:, part * AW:(part + 1) * AW] = total.astype(BF16)
        out_ref[:, 3 * AW:] = du_ref[...].astype(BF16)

    return pl.pallas_call(
        body, name="dproj_join", out_shape=jax.ShapeDtypeStruct((S, 3 * AW + SW), BF16), grid=(S // tr,),
        in_specs=[_residue_spec(tr, d, AW) for d in DILATIONS] * 3 + [_rows(tr, SW)],
        out_specs=_rows(tr, 3 * AW + SW), scratch_shapes=[_residue_scratch(tr, AW)],
        compiler_params=_cparams("parallel"))(*dqs, *dks, *dvs, du)


def _ssm_disc(lr, li, ldt):
    dt = jnp.exp(ldt)
    mag = jnp.exp(lr * dt)
    ar = mag * jnp.cos(li * dt)
    ai = mag * jnp.sin(li * dt)
    nr = ar - 1.0
    den = lr * lr + li * li
    return ar, ai, (nr * lr + ai * li) / den, (ai * lr - nr * li) / den


def _ssm_tile_powers(lr, li, ldt, reverse):
    t = lax.broadcasted_iota(jnp.int32, (TILE, 1), 0)
    n = (TILE - t if reverse else t + 1).astype(F32)
    dt = jnp.exp(ldt)
    mag = jnp.exp(n * (lr * dt))
    ang = n * (li * dt)
    return mag * jnp.cos(ang), mag * jnp.sin(ang) * (-1.0 if reverse else 1.0)


def _cmul(ar, ai, br, bi):
    return ar * br - ai * bi, ar * bi + ai * br


LOG_STEPS = 3


def _ssm_step_tables(ar, ai, reverse):
    sub = lax.broadcasted_iota(jnp.int32, (TILE, ar.shape[-1]), 0)
    tables = []
    for k in range(LOG_STEPS):
        keep = sub < TILE - (1 << k) if reverse else sub >= (1 << k)
        tables.append((jnp.where(keep, ar, 0.0), jnp.where(keep, ai, 0.0)))
        ar, ai = _cmul(ar, ai, ar, ai)
    return tables


def _scan(xr, xi, steps, pr, pi, cr, ci, reverse):
    T, lanes = xr.shape
    n = T // TILE
    xr, xi = xr.reshape(n, TILE, lanes), xi.reshape(n, TILE, lanes)
    for k, (mr, mi) in enumerate(steps):
        shift = TILE - (1 << k) if reverse else 1 << k
        qr, qi = _cmul(mr, mi, pltpu.roll(xr, shift, 1), pltpu.roll(xi, shift, 1))
        xr, xi = xr + qr, xi + qi
    out_r, out_i = [None] * n, [None] * n
    edge = 0 if reverse else TILE - 1
    for j in (reversed(range(n)) if reverse else range(n)):
        er, ei = _cmul(pr, pi, cr, ci)
        sr, si = xr[j] + er, xi[j] + ei
        out_r[j], out_i[j] = sr, si
        cr, ci = sr[edge:edge + 1], si[edge:edge + 1]
    return jnp.concatenate(out_r, axis=0), jnp.concatenate(out_i, axis=0), cr, ci


def _ssm_specs(T, nch, rev):
    def t_of(c):
        return nch - 1 - c if rev else c
    tok = pl.BlockSpec((T, LANES), lambda j, c: (t_of(c), j))
    par = pl.BlockSpec((None, 1, STATE_LANES), lambda j, c: (j, 0, 0))
    bmat = pl.BlockSpec((None, LANES, STATE_LANES), lambda j, c: (j, 0, 0))
    cmat = pl.BlockSpec((None, STATE_LANES, LANES), lambda j, c: (j, 0, 0))
    dvec = pl.BlockSpec((1, LANES), lambda j, c: (0, j))
    return tok, par, bmat, cmat, dvec


def _ssm_fwd(u, lr_e, li_e, ldt_e, bre_e, bim_e, cre_e, cim_e, d_skip):
    S, SW = u.shape
    T = min(SSM_CHUNK, S)
    nch, nbk = S // T, SW // LANES
    tok, par, bmat, cmat, dvec = _ssm_specs(T, nch, False)
    state_spec = pl.BlockSpec((T, STATE_LANES), lambda j, c: (c, j))
    carry_spec = pl.BlockSpec((None, 1, STATE_LANES), lambda j, c: (c, 0, j))

    def body(u_ref, lr_ref, li_ref, ldt_ref, bre_ref, bim_ref, cre_ref, cim_ref, d_ref,
             y_ref, y2_ref, sr_ref, si_ref, er_ref, ei_ref, bbr, bbi, steps, pw, carry):
        c = pl.program_id(1)

        @pl.when(c == 0)
        def _():
            lr, li, ldt = lr_ref[...], li_ref[...], ldt_ref[...]
            ar, ai, kr, ki = _ssm_disc(lr, li, ldt)
            for k, (mr, mi) in enumerate(_ssm_step_tables(ar, ai, False)):
                steps[0, k], steps[1, k] = mr, mi
            bbr[...] = (kr * bre_ref[...] - ki * bim_ref[...]).astype(BF16)
            bbi[...] = (kr * bim_ref[...] + ki * bre_ref[...]).astype(BF16)
            pw[0], pw[1] = _ssm_tile_powers(lr, li, ldt, False)
            carry[...] = jnp.zeros_like(carry)

        u_ = u_ref[...]
        ub = u_.astype(BF16)
        sr, si, cr, ci = _scan(jnp.dot(ub, bbr[...], preferred_element_type=F32),
                               jnp.dot(ub, bbi[...], preferred_element_type=F32),
                               [(steps[0, k], steps[1, k]) for k in range(LOG_STEPS)],
                               pw[0], pw[1], carry[0], carry[1], False)
        carry[0], carry[1] = cr, ci
        er_ref[...], ei_ref[...] = cr, ci
        sr_ref[...], si_ref[...] = sr, si
        y0 = (jnp.dot(sr.astype(BF16), cre_ref[...].astype(BF16), preferred_element_type=F32)
              - jnp.dot(si.astype(BF16), cim_ref[...].astype(BF16), preferred_element_type=F32))
        y1 = y0 + d_ref[...] * u_
        y_ref[...] = y1
        y2_ref[...] = _gelu(y1).astype(BF16)

    states = jax.ShapeDtypeStruct((S, nbk * STATE_LANES), F32)
    ends = jax.ShapeDtypeStruct((nch, 1, nbk * STATE_LANES), F32)
    return pl.pallas_call(
        body, name="ssm_fwd",
        out_shape=(jax.ShapeDtypeStruct((S, SW), F32), jax.ShapeDtypeStruct((S, SW), BF16), states, states, ends, ends),
        grid=(nbk, nch), in_specs=[tok, par, par, par, bmat, bmat, cmat, cmat, dvec],
        out_specs=(tok, tok, state_spec, state_spec, carry_spec, carry_spec),
        scratch_shapes=[pltpu.VMEM((LANES, STATE_LANES), BF16), pltpu.VMEM((LANES, STATE_LANES), BF16),
                        pltpu.VMEM((2, LOG_STEPS, TILE, STATE_LANES), F32), pltpu.VMEM((2, TILE, STATE_LANES), F32),
                        pltpu.VMEM((2, 1, STATE_LANES), F32)],
        compiler_params=_cparams("arbitrary", "arbitrary"),
    )(u, lr_e, li_e, ldt_e, bre_e, bim_e, cre_e, cim_e, d_skip)


def _ssm_bwd(u, y1, dy2a, dy2b, st_r, st_i, ends_r, ends_i, lr_e, li_e, ldt_e, bre_e, bim_e, cre_e, cim_e, d_skip):
    S, SW = u.shape
    T = min(SSM_CHUNK, S)
    nch, nbk = S // T, SW // LANES
    tok, par, bmat, cmat, dvec = _ssm_specs(T, nch, True)
    state_spec = pl.BlockSpec((T, STATE_LANES), lambda j, c: (nch - 1 - c, j))
    prev_spec = pl.BlockSpec((None, 1, STATE_LANES), lambda j, c: (jnp.maximum(nch - 2 - c, 0), 0, j))
    acc8 = pl.BlockSpec((None, 8, STATE_LANES), lambda j, c: (j, 0, 0))
    dd8 = pl.BlockSpec((None, 8, LANES), lambda j, c: (j, 0, 0))

    def body(u_ref, y_ref, da_ref, db_ref, sr_ref, si_ref, pr_ref, pi_ref, lr_ref, li_ref, ldt_ref,
             bre_ref, bim_ref, cre_ref, cim_ref, d_ref,
             du_ref, dar_ref, dai_ref, dcr_ref, dci_ref, dbr_ref, dbi_ref, ddk_ref,
             bbr, bbi, steps, pw, carry):
        c = pl.program_id(1)

        @pl.when(c == 0)
        def _():
            lr, li, ldt = lr_ref[...], li_ref[...], ldt_ref[...]
            ar, ai, kr, ki = _ssm_disc(lr, li, ldt)
            for k, (mr, mi) in enumerate(_ssm_step_tables(ar, -ai, True)):
                steps[0, k], steps[1, k] = mr, mi
            bbr[...] = (kr * bre_ref[...] - ki * bim_ref[...]).astype(BF16)
            bbi[...] = (kr * bim_ref[...] + ki * bre_ref[...]).astype(BF16)
            pw[0], pw[1] = _ssm_tile_powers(lr, li, ldt, True)
            carry[...] = jnp.zeros_like(carry)
            for ref in (dar_ref, dai_ref, dcr_ref, dci_ref, dbr_ref, dbi_ref, ddk_ref):
                ref[...] = jnp.zeros_like(ref)

        u_ = u_ref[...]
        ub = u_.astype(BF16)
        dy1 = (da_ref[...] + db_ref[...]) * _gelu_grad(y_ref[...])
        dyb = dy1.astype(BF16)

        sr, si = sr_ref[...], si_ref[...]
        has_prev = c < nch - 1
        s0r = jnp.where(has_prev, pr_ref[...], 0.0)
        s0i = jnp.where(has_prev, pi_ref[...], 0.0)

        cre_b, cim_b = cre_ref[...].astype(BF16), cim_ref[...].astype(BF16)
        gr, gi, cr, ci = _scan(lax.dot_general(dyb, cre_b, _NT, preferred_element_type=F32),
                               -lax.dot_general(dyb, cim_b, _NT, preferred_element_type=F32),
                               [(steps[0, k], steps[1, k]) for k in range(LOG_STEPS)],
                               pw[0], pw[1], carry[0], carry[1], True)
        carry[0], carry[1] = cr, ci

        row = lax.broadcasted_iota(jnp.int32, (T, STATE_LANES), 0)
        spr = jnp.where(row == 0, s0r, pltpu.roll(sr, 1, 0))
        spi = jnp.where(row == 0, s0i, pltpu.roll(si, 1, 0))

        def fold(a):
            return jnp.sum(a.reshape(T // 8, 8, a.shape[-1]), axis=0)

        dar_ref[...] += fold(gr * spr + gi * spi)
        dai_ref[...] += fold(gi * spr - gr * spi)
        srb, sib, grb, gib = sr.astype(BF16), si.astype(BF16), gr.astype(BF16), gi.astype(BF16)
        dcr_ref[...] += lax.dot_general(srb, dyb, _TN, preferred_element_type=F32)
        dci_ref[...] -= lax.dot_general(sib, dyb, _TN, preferred_element_type=F32)
        dbr_ref[...] += lax.dot_general(ub, grb, _TN, preferred_element_type=F32)
        dbi_ref[...] += lax.dot_general(ub, gib, _TN, preferred_element_type=F32)
        du_ref[...] = (lax.dot_general(grb, bbr[...], _NT, preferred_element_type=F32)
                       + lax.dot_general(gib, bbi[...], _NT, preferred_element_type=F32)
                       + dy1 * d_ref[...])
        ddk_ref[...] += fold(dy1 * u_)

    return pl.pallas_call(
        body, name="ssm_bwd",
        out_shape=(jax.ShapeDtypeStruct((S, SW), F32),
                   jax.ShapeDtypeStruct((nbk, 8, STATE_LANES), F32), jax.ShapeDtypeStruct((nbk, 8, STATE_LANES), F32),
                   jax.ShapeDtypeStruct((nbk, STATE_LANES, LANES), F32), jax.ShapeDtypeStruct((nbk, STATE_LANES, LANES), F32),
                   jax.ShapeDtypeStruct((nbk, LANES, STATE_LANES), F32), jax.ShapeDtypeStruct((nbk, LANES, STATE_LANES), F32),
                   jax.ShapeDtypeStruct((nbk, 8, LANES), F32)),
        grid=(nbk, nch),
        in_specs=[tok, tok, tok, tok, state_spec, state_spec, prev_spec, prev_spec, par, par, par,
                  bmat, bmat, cmat, cmat, dvec],
        out_specs=(tok, acc8, acc8, cmat, cmat, bmat, bmat, dd8),
        scratch_shapes=[pltpu.VMEM((LANES, STATE_LANES), BF16), pltpu.VMEM((LANES, STATE_LANES), BF16),
                        pltpu.VMEM((2, LOG_STEPS, TILE, STATE_LANES), F32), pltpu.VMEM((2, TILE, STATE_LANES), F32),
                        pltpu.VMEM((2, 1, STATE_LANES), F32)],
        compiler_params=_cparams("arbitrary", "arbitrary"),
    )(u, y1, dy2a, dy2b, st_r, st_i, ends_r, ends_i, lr_e, li_e, ldt_e, bre_e, bim_e, cre_e, cim_e, d_skip)


def _ssm_param_bwd(dar8, dai8, dbr_e, dbi_e, lr_e, li_e, ldt_e, bre_e, bim_e):
    nbk = lr_e.shape[0]
    par = pl.BlockSpec((None, 1, STATE_LANES), lambda j: (j, 0, 0))
    acc8 = pl.BlockSpec((None, 8, STATE_LANES), lambda j: (j, 0, 0))
    bmat = pl.BlockSpec((None, LANES, STATE_LANES), lambda j: (j, 0, 0))

    def body(dar_ref, dai_ref, dbr_ref, dbi_ref, lr_ref, li_ref, ldt_ref, bre_ref, bim_ref,
             dlr_ref, dli_ref, dldt_ref, dbre_ref, dbim_ref):
        lr, li, ldt = lr_ref[...], li_ref[...], ldt_ref[...]
        (ar, ai, kr, ki), vjp = jax.vjp(_ssm_disc, lr, li, ldt)
        dbr, dbi, bre, bim = dbr_ref[...], dbi_ref[...], bre_ref[...], bim_ref[...]
        dbre_ref[...] = kr * dbr + ki * dbi
        dbim_ref[...] = kr * dbi - ki * dbr
        dkr = _colsum(dbr * bre + dbi * bim)
        dki = _colsum(dbi * bre - dbr * bim)
        dlr, dli, dldt = vjp((_colsum(dar_ref[...]), _colsum(dai_ref[...]), dkr, dki))
        dlr_ref[...] = dlr
        dli_ref[...] = dli
        tot = jnp.broadcast_to(dldt, (8, STATE_LANES))
        sh = 1
        while sh < SSM_P:
            tot = tot + pltpu.roll(tot, STATE_LANES - sh, 1)
            sh *= 2
        dldt_ref[...] = tot[:1]

    vec = jax.ShapeDtypeStruct((nbk, 1, STATE_LANES), F32)
    mat = jax.ShapeDtypeStruct((nbk, LANES, STATE_LANES), F32)
    return pl.pallas_call(
        body, name="ssm_param_bwd", out_shape=(vec, vec, vec, mat, mat), grid=(nbk,),
        in_specs=[acc8, acc8, bmat, bmat, par, par, par, bmat, bmat],
        out_specs=(par, par, par, bmat, bmat), compiler_params=_cparams("parallel"),
    )(dar8, dai8, dbr_e, dbi_e, lr_e, li_e, ldt_e, bre_e, bim_e)


def _expand_b(b):
    G = b.shape[0]
    bt = b.transpose(0, 2, 1).reshape(G // GROUPS_PER_BLOCK, GROUPS_PER_BLOCK, SSM_C, SSM_P)
    eye = jnp.eye(GROUPS_PER_BLOCK, dtype=b.dtype)
    return (bt[:, :, :, None, :] * eye[None, :, None, :, None]).reshape(G // GROUPS_PER_BLOCK, LANES, STATE_LANES)


def _collapse_b(be):
    nbk = be.shape[0]
    eye = jnp.eye(GROUPS_PER_BLOCK, dtype=be.dtype)
    d5 = be.reshape(nbk, GROUPS_PER_BLOCK, SSM_C, GROUPS_PER_BLOCK, SSM_P)
    d4 = (d5 * eye[None, :, None, :, None]).sum(axis=3)
    return d4.transpose(0, 1, 3, 2).reshape(nbk * GROUPS_PER_BLOCK, SSM_P, SSM_C)


def _expand_c(cm):
    G = cm.shape[0]
    ct = cm.transpose(0, 2, 1).reshape(G // GROUPS_PER_BLOCK, GROUPS_PER_BLOCK, SSM_P, SSM_C)
    eye = jnp.eye(GROUPS_PER_BLOCK, dtype=cm.dtype)
    return (ct[:, :, :, None, :] * eye[None, :, None, :, None]).reshape(G // GROUPS_PER_BLOCK, STATE_LANES, LANES)


def _collapse_c(ce):
    nbk = ce.shape[0]
    eye = jnp.eye(GROUPS_PER_BLOCK, dtype=ce.dtype)
    d5 = ce.reshape(nbk, GROUPS_PER_BLOCK, SSM_P, GROUPS_PER_BLOCK, SSM_C)
    d4 = (d5 * eye[None, :, None, :, None]).sum(axis=3)
    return d4.transpose(0, 1, 3, 2).reshape(nbk * GROUPS_PER_BLOCK, SSM_C, SSM_P)


def _place():
    x, y, c = lax.axis_index("x"), lax.axis_index("y"), lax.axis_index("c")
    return x, y, c


def _other_chips(x, y):
    return [(1 - x, y), (x, 1 - y), (1 - x, 1 - y)]


_ANY = pl.BlockSpec(memory_space=pl.ANY)


_HBM = pl.BlockSpec(memory_space=pltpu.HBM)
_SEM = pl.BlockSpec(memory_space=pltpu.SEMAPHORE)
_EFFECT = pltpu.SideEffectType.DATAFLOW_SIDE_EFFECTING
_TOKEN = jax.ShapeDtypeStruct((8, LANES), F32)


def _hbm(a):
    return pltpu.with_memory_space_constraint(a, pltpu.HBM)


def _place_own(src, *, gather, name, after=None, tr=512):
    R, C = src.shape[-2:]
    tr = min(tr, R)
    x, y, _ = _place()
    me = (2 * x + y).astype(jnp.int32).reshape(1)
    extra = [] if after is None else [after]

    def body(me_ref, s_ref, *rest):
        rest[-1][...] = s_ref[...].astype(BF16)

    own = pl.BlockSpec((None, tr, C), lambda i, me_ref: (me_ref[0], i, 0))
    grid_spec = pltpu.PrefetchScalarGridSpec(
        num_scalar_prefetch=1, grid=(R // tr,),
        in_specs=([pl.BlockSpec((tr, C), lambda i, me_ref: (i, 0)) if gather else own]
                  + [pl.BlockSpec(a.shape, lambda i, me_ref: (0, 0)) for a in extra]), out_specs=own)
    return pl.pallas_call(
        body, name=name, grid_spec=grid_spec, out_shape=jax.ShapeDtypeStruct((N_CHIPS, R, C), BF16),
        compiler_params=_cparams("parallel"))(me, src, *extra)


def _exchange_copy(src_slot, land_slot, send, recv, k, j, peer, c):
    return pltpu.make_async_remote_copy(
        src_ref=src_slot, dst_ref=land_slot, send_sem=send.at[3 * k + j], recv_sem=recv.at[3 * k + j],
        device_id=(peer[0], peer[1], c), device_id_type=MESH)


def _exchange_start(lands, srcs, groups, *, name):
    n, ng = len(lands), len(groups)
    bufs = list(lands) + list(srcs)
    nb = len(bufs)

    def body(*refs):
        lnd, src, sems = refs[:n], refs[n:nb], refs[nb:nb + 2 * ng]
        token = refs[2 * nb + 2 * ng]
        x, y, c = _place()
        me = 2 * x + y
        for gi, group in enumerate(groups):
            for k, w in enumerate(group):
                for j, peer in enumerate(_other_chips(x, y)):
                    if src:
                        sent, dst = src[w].at[2 * peer[0] + peer[1]], lnd[w].at[me]
                    else:
                        sent = dst = lnd[w].at[me, c]
                    _exchange_copy(sent, dst, sems[2 * gi], sems[2 * gi + 1], k, j, peer, c).start()
        token[...] = jnp.zeros_like(token)

    sem_shapes = [pltpu.SemaphoreType.DMA((3 * len(g),)) for g in groups for _ in range(2)]
    res = pl.pallas_call(
        body, name=name,
        out_shape=sem_shapes + [pltpu.HBM(a.shape, a.dtype) for a in bufs] + [_TOKEN],
        in_specs=[_HBM] * nb,
        out_specs=[_SEM] * (2 * ng) + [_HBM] * nb + [pl.BlockSpec(memory_space=pltpu.VMEM)],
        input_output_aliases={i: 2 * ng + i for i in range(nb)},
        compiler_params=pltpu.CompilerParams(has_side_effects=_EFFECT),
    )(*[_hbm(a) for a in bufs])
    sems = [(res[2 * gi], res[2 * gi + 1]) for gi in range(ng)]
    return sems, res[2 * ng:2 * ng + n], res[2 * ng + n:2 * ng + nb], res[-1]


def _exchange_wait(lands, srcs, sems, after, *, name):
    n = len(lands)
    bufs = list(lands) + list(srcs)
    nb = len(bufs)
    send_sems, recv_sems = sems

    def body(*refs):
        lnd, src, send, recv = refs[:n], refs[n:nb], refs[nb], refs[nb + 1]
        x, y, c = _place()
        for k in range(n):
            for j, peer in enumerate(_other_chips(x, y)):
                slot = 2 * peer[0] + peer[1]
                if src:
                    copy = _exchange_copy(src[k].at[slot], lnd[k].at[slot], send, recv, k, j, peer, c)
                else:
                    copy = _exchange_copy(lnd[k].at[slot, c], lnd[k].at[slot, c], send, recv, k, j, peer, c)
                copy.wait_send()
                copy.wait_recv()

    res = pl.pallas_call(
        body, name=name, out_shape=[pltpu.HBM(a.shape, a.dtype) for a in bufs],
        in_specs=[_HBM] * nb + [_SEM, _SEM, _ANY], out_specs=[_HBM] * nb,
        input_output_aliases={i: i for i in range(nb)},
        compiler_params=pltpu.CompilerParams(has_side_effects=_EFFECT),
    )(*bufs, send_sems, recv_sems, after)
    return res[:n]


def _pair_fill(lands, *, name):
    n = len(lands)

    def body(*refs):
        ins, outs, send, recv = refs[:n], refs[n:2 * n], refs[2 * n], refs[2 * n + 1]
        x, y, c = _place()
        for w in range(n):
            for j, (px, py) in enumerate(_other_chips(x, y)):
                slot = 2 * px + py
                pltpu.make_async_remote_copy(
                    src_ref=ins[w].at[slot, c], dst_ref=outs[w].at[slot, c], send_sem=send.at[3 * w + j],
                    recv_sem=recv.at[3 * w + j], device_id=(x, y, 1 - c), device_id_type=MESH).start()
        for w in range(n):
            for j, (px, py) in enumerate(_other_chips(x, y)):
                slot = 2 * px + py
                arrival = pltpu.make_async_remote_copy(
                    src_ref=ins[w].at[slot, c], dst_ref=outs[w].at[slot, 1 - c], send_sem=send.at[3 * w + j],
                    recv_sem=recv.at[3 * w + j], device_id=(x, y, 1 - c), device_id_type=MESH)
                arrival.wait_recv()
                arrival.wait_send()

    return pl.pallas_call(
        body, name=name, out_shape=[jax.ShapeDtypeStruct(a.shape, a.dtype) for a in lands],
        in_specs=[_ANY] * n, out_specs=[_ANY] * n, input_output_aliases={i: i for i in range(n)},
        scratch_shapes=[pltpu.SemaphoreType.DMA((3 * n,)), pltpu.SemaphoreType.DMA((3 * n,))],
    )(*lands)


def _pair_copy(src, dst, send, recv, w, j, sibling):
    return pltpu.make_async_remote_copy(
        src_ref=src, dst_ref=dst, send_sem=send.at[3 * w + j], recv_sem=recv.at[3 * w + j],
        device_id=sibling, device_id_type=MESH)


def _pair_start(lands, *, name):
    n = len(lands)

    def body(*refs):
        bufs, send, recv, token = refs[:n], refs[n], refs[n + 1], refs[2 * n + 2]
        x, y, c = _place()
        for w in range(n):
            for j, (px, py) in enumerate(_other_chips(x, y)):
                half = bufs[w].at[2 * px + py, c]
                _pair_copy(half, half, send, recv, w, j, (x, y, 1 - c)).start()
        token[...] = jnp.zeros_like(token)

    res = pl.pallas_call(
        body, name=name,
        out_shape=[pltpu.SemaphoreType.DMA((3 * n,))] * 2 + [pltpu.HBM(a.shape, a.dtype) for a in lands] + [_TOKEN],
        in_specs=[_HBM] * n, out_specs=[_SEM, _SEM] + [_HBM] * n + [pl.BlockSpec(memory_space=pltpu.VMEM)],
        input_output_aliases={i: 2 + i for i in range(n)},
        compiler_params=pltpu.CompilerParams(has_side_effects=_EFFECT),
    )(*[_hbm(a) for a in lands])
    return (res[0], res[1]), res[2:2 + n], res[-1]


def _pair_wait(lands, sems, after, *, name):
    n = len(lands)

    def body(*refs):
        bufs, send, recv = refs[:n], refs[n], refs[n + 1]
        x, y, c = _place()
        for w in range(n):
            for j, (px, py) in enumerate(_other_chips(x, y)):
                slot = 2 * px + py
                copy = _pair_copy(bufs[w].at[slot, c], bufs[w].at[slot, 1 - c], send, recv, w, j, (x, y, 1 - c))
                copy.wait_send()
                copy.wait_recv()

    return pl.pallas_call(
        body, name=name, out_shape=[pltpu.HBM(a.shape, a.dtype) for a in lands],
        in_specs=[_HBM] * n + [_SEM, _SEM, _ANY], out_specs=[_HBM] * n,
        input_output_aliases={i: i for i in range(n)},
        compiler_params=pltpu.CompilerParams(has_side_effects=_EFFECT),
    )(*lands, *sems, after)


def _sum_partials(land, *, name, tr=256):
    _, R, C = land.shape
    tr = min(tr, R)

    def body(l_ref, o_ref):
        acc = l_ref[0].astype(F32)
        for k in range(1, N_CHIPS):
            acc = acc + l_ref[k].astype(F32)
        o_ref[...] = acc

    return pl.pallas_call(
        body, name=name, out_shape=jax.ShapeDtypeStruct((R, C), F32), grid=(R // tr,),
        in_specs=[pl.BlockSpec((N_CHIPS, tr, C), lambda i: (0, i, 0))], out_specs=_rows(tr, C),
        compiler_params=_cparams("parallel"))(land)


def _swap_with_sibling(sums, *, name):
    n = len(sums)

    def body(*refs):
        ins, outs = refs[:n], refs[n:2 * n]
        send_sems, recv_sems = refs[2 * n:]
        x, y, c = _place()
        copies = [pltpu.make_async_remote_copy(
            src_ref=ins[w], dst_ref=outs[w], send_sem=send_sems.at[w], recv_sem=recv_sems.at[w],
            device_id=(x, y, 1 - c), device_id_type=MESH) for w in range(n)]
        for cp in copies:
            cp.start()
        for cp in copies:
            cp.wait_recv()
            cp.wait_send()

    return pl.pallas_call(
        body, name=name,
        out_shape=[jax.ShapeDtypeStruct(s.shape, s.dtype) for s in sums],
        in_specs=[_ANY] * n, out_specs=[_ANY] * n,
        scratch_shapes=[pltpu.SemaphoreType.DMA((n,)), pltpu.SemaphoreType.DMA((n,))],
    )(*sums)


def _swap_start(sums, *, name):
    n = len(sums)
    bufs = list(sums) + [lax.empty(s.shape, s.dtype) for s in sums]

    def body(*refs):
        src, lnd, send, recv, token = refs[:n], refs[n:2 * n], refs[2 * n], refs[2 * n + 1], refs[4 * n + 2]
        x, y, c = _place()
        for w in range(n):
            pltpu.make_async_remote_copy(
                src_ref=src[w], dst_ref=lnd[w], send_sem=send.at[w], recv_sem=recv.at[w],
                device_id=(x, y, 1 - c), device_id_type=MESH).start()
        token[...] = jnp.zeros_like(token)

    res = pl.pallas_call(
        body, name=name,
        out_shape=[pltpu.SemaphoreType.DMA((n,))] * 2 + [pltpu.HBM(a.shape, a.dtype) for a in bufs] + [_TOKEN],
        in_specs=[_HBM] * (2 * n),
        out_specs=[_SEM, _SEM] + [_HBM] * (2 * n) + [pl.BlockSpec(memory_space=pltpu.VMEM)],
        input_output_aliases={i: 2 + i for i in range(2 * n)},
        compiler_params=pltpu.CompilerParams(has_side_effects=_EFFECT),
    )(*[_hbm(a) for a in bufs])
    return (res[0], res[1]), res[2:2 + n], res[2 + n:2 + 2 * n], res[-1]


def _swap_wait(sums, lands, sems, after, *, name):
    n = len(sums)

    def body(*refs):
        src, lnd, send, recv = refs[:n], refs[n:2 * n], refs[2 * n], refs[2 * n + 1]
        x, y, c = _place()
        for w in range(n):
            copy = pltpu.make_async_remote_copy(
                src_ref=src[w], dst_ref=lnd[w], send_sem=send.at[w], recv_sem=recv.at[w],
                device_id=(x, y, 1 - c), device_id_type=MESH)
            copy.wait_send()
            copy.wait_recv()

    bufs = list(sums) + list(lands)
    res = pl.pallas_call(
        body, name=name, out_shape=[pltpu.HBM(a.shape, a.dtype) for a in bufs],
        in_specs=[_HBM] * (2 * n) + [_SEM, _SEM, _ANY], out_specs=[_HBM] * (2 * n),
        input_output_aliases={i: i for i in range(2 * n)},
        compiler_params=pltpu.CompilerParams(has_side_effects=_EFFECT),
    )(*bufs, *sems, after)
    return res[:n], res[n:]


def _adamw_math(w, g, m, v):
    m = ADAM_B1 * m + (1.0 - ADAM_B1) * g
    v = ADAM_B2 * v + (1.0 - ADAM_B2) * (g * g)
    m_hat = m / (1.0 - ADAM_B1 ** ADAM_STEP)
    v_hat = v / (1.0 - ADAM_B2 ** ADAM_STEP)
    delta = -ADAM_LR * (m_hat / (jnp.sqrt(v_hat) + ADAM_EPS) + ADAM_WD * w)
    return delta, m, v


def _adamw_pair(mine, theirs, w, m, v, *, name, tr=128):
    R, C = w.shape
    tr = min(tr, R)

    def body(a_ref, b_ref, w_ref, m_ref, v_ref, g_ref, d_ref, nm_ref, nv_ref):
        g = a_ref[...] + b_ref[...]
        g_ref[...] = g
        d_ref[...], nm_ref[...], nv_ref[...] = _adamw_math(w_ref[...], g, m_ref[...], v_ref[...])

    shape = jax.ShapeDtypeStruct((R, C), F32)
    return pl.pallas_call(
        body, name=name, out_shape=(shape,) * 4, grid=(R // tr,),
        in_specs=[_rows(tr, C)] * 5, out_specs=(_rows(tr, C),) * 4,
        compiler_params=_cparams("parallel"))(mine, theirs, w, m, v)


def _all_reduce_small(packed):
    R = packed.shape[0]
    half = R // 2

    def body(x_ref, g_ref, sib_ref, pair_ref, land_ref, send_sems, recv_sems):
        x, y, c = _place()
        me = 2 * x + y
        sibling = (x, y, 1 - c)

        swap = pltpu.make_async_remote_copy(
            src_ref=x_ref, dst_ref=sib_ref, send_sem=send_sems.at[0], recv_sem=recv_sems.at[0],
            device_id=sibling, device_id_type=MESH)
        swap.start()
        swap.wait()
        mine, theirs = x_ref[...], sib_ref[...]
        south = c == 0
        pair_ref[...] = jnp.where(south, mine, theirs) + jnp.where(south, theirs, mine)

        land_ref[me] = pair_ref[c]
        for j, (px, py) in enumerate(_other_chips(x, y)):
            pltpu.make_async_remote_copy(
                src_ref=pair_ref.at[c], dst_ref=land_ref.at[me], send_sem=send_sems.at[1 + j],
                recv_sem=recv_sems.at[1 + j], device_id=(px, py, c), device_id_type=MESH).start()
        for j, (px, py) in enumerate(_other_chips(x, y)):
            arrival = pltpu.make_async_remote_copy(
                src_ref=pair_ref.at[c], dst_ref=land_ref.at[2 * px + py], send_sem=send_sems.at[1 + j],
                recv_sem=recv_sems.at[1 + j], device_id=(px, py, c), device_id_type=MESH)
            arrival.wait_recv()
            arrival.wait_send()
        total = land_ref[0]
        for k in range(1, N_CHIPS):
            total = total + land_ref[k]
        g_ref[c] = total

        give = pltpu.make_async_remote_copy(
            src_ref=g_ref.at[c], dst_ref=g_ref.at[c], send_sem=send_sems.at[4], recv_sem=recv_sems.at[4],
            device_id=sibling, device_id_type=MESH)
        give.start()
        take = pltpu.make_async_remote_copy(
            src_ref=g_ref.at[c], dst_ref=g_ref.at[1 - c], send_sem=send_sems.at[4], recv_sem=recv_sems.at[4],
            device_id=sibling, device_id_type=MESH)
        take.wait_recv()
        give.wait_send()

    vm = pl.BlockSpec(memory_space=pltpu.VMEM)
    return pl.pallas_call(
        body, name="all_reduce_small", out_shape=jax.ShapeDtypeStruct((2, half, LANES), F32),
        in_specs=[vm], out_specs=vm,
        scratch_shapes=[pltpu.VMEM((2, half, LANES), F32), pltpu.VMEM((2, half, LANES), F32),
                        pltpu.VMEM((N_CHIPS, half, LANES), F32),
                        pltpu.SemaphoreType.DMA((5,)), pltpu.SemaphoreType.DMA((5,))],
        compiler_params=pltpu.CompilerParams(vmem_limit_bytes=VMEM_LIMIT_BYTES),
    )(packed.reshape(2, half, LANES)).reshape(R, LANES)


def _adamw_small(g, w, m, v):
    R = g.shape[0]
    tr = PACK_ROWS

    def body(g_ref, w_ref, m_ref, v_ref, d_ref, nm_ref, nv_ref):
        d_ref[...], nm_ref[...], nv_ref[...] = _adamw_math(w_ref[...], g_ref[...], m_ref[...], v_ref[...])

    shape = jax.ShapeDtypeStruct((R, LANES), F32)
    return pl.pallas_call(
        body, name="adamw_small", out_shape=(shape,) * 3, grid=(R // tr,),
        in_specs=[_rows(tr, LANES)] * 4, out_specs=(_rows(tr, LANES),) * 3,
        compiler_params=_cparams("parallel"))(g, w, m, v)


def _pack(arrays):
    parts, layout = [], []
    for a in arrays:
        n = a.size
        rows = -(-n // (8 * LANES)) * 8
        flat = jnp.pad(a.reshape(-1).astype(F32), (0, rows * LANES - n))
        parts.append(flat.reshape(rows, LANES))
        layout.append((rows, n, a.shape))
    total = sum(r for r, _, _ in layout)
    parts.append(jnp.zeros((-total % PACK_ROWS, LANES), F32))
    return jnp.concatenate(parts, axis=0), layout


def _unpack(buf, layout):
    out, r0 = [], 0
    for rows, n, shape in layout:
        out.append(buf[r0:r0 + rows].reshape(-1)[:n].reshape(shape))
        r0 += rows
    return out


SMALL = ("mix_norm_pre", "lam_re", "lam_im", "log_dt", "ssm_b_re", "ssm_b_im", "ssm_c_re", "ssm_c_im",
         "ssm_d", "b_glu", "attn_out_norm", "ssm_out_norm", "mix_norm_post", "mlp_norm_pre",
         "mlp_norm_post", "ple_norm_pre", "ple_norm_post")
BIG = ("w_in", "w_glu", "w_out", "w_up", "w_down", "w_ple_gate", "w_ple_proj")
WEIGHTS = ("mix_norm_pre", "w_in", "lam_re", "lam_im", "log_dt", "ssm_b_re", "ssm_b_im", "ssm_c_re",
           "ssm_c_im", "ssm_d", "w_glu", "b_glu", "attn_out_norm", "ssm_out_norm", "w_out",
           "mix_norm_post", "mlp_norm_pre", "w_up", "w_down", "mlp_norm_post", "ple_norm_pre",
           "w_ple_gate", "w_ple_proj", "ple_norm_post")


def kernel(x, p, mix_norm_pre, w_in, lam_re, lam_im, log_dt, ssm_b_re, ssm_b_im, ssm_c_re, ssm_c_im, ssm_d, w_glu, b_glu, attn_out_norm, ssm_out_norm, w_out, mix_norm_post, mlp_norm_pre, w_up, w_down, mlp_norm_post, ple_norm_pre, w_ple_gate, w_ple_proj, ple_norm_post, loss_target, m_mix_norm_pre, m_w_in, m_lam_re, m_lam_im, m_log_dt, m_ssm_b_re, m_ssm_b_im, m_ssm_c_re, m_ssm_c_im, m_ssm_d, m_w_glu, m_b_glu, m_attn_out_norm, m_ssm_out_norm, m_w_out, m_mix_norm_post, m_mlp_norm_pre, m_w_up, m_w_down, m_mlp_norm_post, m_ple_norm_pre, m_w_ple_gate, m_w_ple_proj, m_ple_norm_post, v_mix_norm_pre, v_w_in, v_lam_re, v_lam_im, v_log_dt, v_ssm_b_re, v_ssm_b_im, v_ssm_c_re, v_ssm_c_im, v_ssm_d, v_w_glu, v_b_glu, v_attn_out_norm, v_ssm_out_norm, v_w_out, v_mix_norm_post, v_mlp_norm_pre, v_w_up, v_w_down, v_mlp_norm_post, v_ple_norm_pre, v_w_ple_gate, v_w_ple_proj, v_ple_norm_post):
    args = dict(locals())
    W = {n: args[n][0] for n in WEIGHTS}
    Mo = {n: args["m_" + n][0] for n in WEIGHTS}
    Vo = {n: args["v_" + n][0] for n in WEIGHTS}
    xs, ps, tgt = x[0], p[0, 0], loss_target[0]
    S, D = xs.shape
    SW = W["ssm_d"].shape[0]
    AW = W["attn_out_norm"].shape[0]
    heads = AW // HEAD_DIM
    G = SW // SSM_C
    nbk = SW // LANES
    assert W["w_in"].shape[1] * N_CHIPS == 3 * AW + SW and AW == SW

    row = lambda a: a.reshape(1, -1)

    ag_groups = (("w_in",), ("w_glu", "w_out"), ("w_up",), ("w_down", "w_ple_gate", "w_ple_proj"))
    ag_names = [n for g in ag_groups for n in g]
    def in_halves(a):
        return a.reshape(N_CHIPS, 2, a.shape[1] // 2, a.shape[2])

    def placed(n, after=None):
        return in_halves(_place_own(W[n], gather=True, name="ag_place_" + n, after=after))

    first_sems, first_land, _, first_token = _exchange_start([placed("w_in")], [], [[0]], name="ag_start_first")
    rest_sems, rest_land, _, ag_token = _exchange_start(
        [placed(n, first_token) for n in ag_names[1:]], [],
        [[ag_names.index(n) - 1 for n in g] for g in ag_groups[1:]], name="ag_start")
    ag_sems, ag_land = first_sems + rest_sems, list(first_land) + list(rest_land)

    def fetched(gi, after):
        return _exchange_wait([ag_land[ag_names.index(n)] for n in ag_groups[gi]], [], ag_sems[gi], after,
                              name=f"ag_wait_{gi}")

    def whole(gis, bufs):
        names = [n for gi in gis for n in ag_groups[gi]]
        return {n: a.reshape(N_CHIPS, -1, a.shape[-1]) for n, a in zip(names, bufs)}

    lr_e = W["lam_re"].reshape(nbk, 1, STATE_LANES)
    li_e = W["lam_im"].reshape(nbk, 1, STATE_LANES)
    ldt_e = jnp.repeat(W["log_dt"], SSM_P).reshape(nbk, 1, STATE_LANES)
    bre_e, bim_e = _expand_b(W["ssm_b_re"]), _expand_b(W["ssm_b_im"])
    cre_e, cim_e = _expand_c(W["ssm_c_re"]), _expand_c(W["ssm_c_im"])
    d_row = row(W["ssm_d"])

    hn1 = _norm_cast(xs, row(W["mix_norm_pre"]) + ag_token[0, 0], name="norm_in")
    w_in_f = whole([0], _pair_fill(fetched(0, hn1), name="ag_pair_0"))["w_in"]
    qkv_b = _proj_qkv(hn1, w_in_f)
    outs, lses = zip(*[_attn_fwd(qb, d, heads) for d, qb in zip(DILATIONS, qkv_b)])
    pair_a_sems, pair_a, pair_a_token = _pair_start(fetched(1, outs[-1]), name="ag_pair_start_a")
    u = _matmul(hn1, w_in_f, name="proj_u", b_shards=N_CHIPS, b_cols=(3 * AW, SW), after=pair_a_token)
    y1, y2b, st_r, st_i, ends_r, ends_i = _ssm_fwd(u, lr_e, li_e, ldt_e, bre_e, bim_e, cre_e, cim_e, d_row)
    pair_b_sems, pair_b, pair_b_token = _pair_start(fetched(2, y2b), name="ag_pair_start_b")
    full = whole([1], _pair_wait(pair_a, pair_a_sems, y2b, name="ag_pair_wait_a"))
    w_glu_f = full["w_glu"].reshape(SW, SW)
    w_out_f = full["w_out"].reshape(AW + SW, D)
    z = _matmul(y2b, w_glu_f, name="glu_z", after=pair_b_token)
    attn, lse_b, mixed = _mix_fwd(outs, lses, y1, z, row(W["b_glu"]), row(W["attn_out_norm"]), row(W["ssm_out_norm"]))
    mo = _matmul(mixed, w_out_f, name="mix_out")
    h1, hn2 = _res_norm(xs, mo, row(W["mix_norm_post"]), row(W["mlp_norm_pre"]), name="res_mix")
    w_up_f = whole([2], _pair_wait(pair_b, pair_b_sems, hn2, name="ag_pair_wait_b"))["w_up"]
    up, act = _matmul(hn2, w_up_f, name="mlp_up", b_shards=N_CHIPS, relu2=True, out_dtype=BF16)
    full = whole([3], _pair_fill(fetched(3, act), name="ag_pair_3"))
    w_down_f = full["w_down"].reshape(-1, D)
    w_pg_f = full["w_ple_gate"].reshape(D, D)
    w_pp_f = full["w_ple_proj"]
    ff = _matmul(act, w_down_f, name="mlp_down")
    h2, hn3 = _res_norm(h1, ff, row(W["mlp_norm_post"]), row(W["ple_norm_pre"]), name="res_mlp")
    gl = _matmul(hn3, w_pg_f, name="ple_gate")
    e = _matmul(ps.astype(BF16), w_pp_f, name="ple_proj", b_shards=N_CHIPS)

    dh3, dgl, de, loss_part, dg_ple_post = _final(h2, gl, e, row(W["ple_norm_post"]), tgt)
    gW = {}
    out_g, out_d, out_m, out_v = {}, {}, {}, {}

    def scatter_start(names, tag):
        parts = [gW[n] if gW[n].ndim == 3 else gW[n].reshape((N_CHIPS, -1, gW[n].shape[1])) for n in names]
        sems, land, src, token = _exchange_start(
            [_place_own(part, gather=False, name="rs_place_" + n) for n, part in zip(names, parts)], parts,
            [list(range(len(names)))], name=f"rs_start_{tag}")
        return (names, sems[0], land, src), token

    def scatter_sums(batches, after):
        names, sums = [], []
        for tag, (batch_names, sems, land, src) in batches:
            landed = _exchange_wait(land, src, sems, after, name=f"rs_wait_{tag}")
            names += batch_names
            sums += [_sum_partials(l, name="sum_" + n) for n, l in zip(batch_names, landed)]
        return names, sums

    def apply(names, sums, theirs):
        for n, a, b in zip(names, sums, theirs):
            out_g[n], out_d[n], out_m[n], out_v[n] = _adamw_pair(a, b, W[n], Mo[n], Vo[n], name="adamw_" + n)

    def swap_begin(batches, after, tag):
        names, sums = scatter_sums(batches, after)
        sems, sums, lands, token = _swap_start(sums, name=f"swap_start_{tag}")
        return (names, sems, sums, lands), token

    def swap_end(swap, after, tag):
        names, sems, sums, lands = swap
        sums, theirs = _swap_wait(sums, lands, sems, after, name=f"swap_wait_{tag}")
        apply(names, sums, theirs)

    def scatter_finish(batch, after, tag):
        names, sums = scatter_sums([(tag, batch)], after)
        apply(names, sums, _swap_with_sibling(sums, name=f"swap_{tag}"))

    gW["w_ple_proj"] = _matmul(ps.astype(BF16), de, name="d_w_ple_proj", ta=True, out_dtype=BF16, out_shards=N_CHIPS)
    gW["w_ple_gate"] = _matmul(hn3, dgl, name="d_w_ple_gate", ta=True, out_dtype=BF16)
    dhn3 = _matmul(dgl, w_pg_f, name="d_hn3", tb=True)
    dh2, dff, dg_ple_pre, dg_mlp_post = _bwd_res_norm(
        dh3, dhn3, h2, row(W["ple_norm_pre"]), ff, row(W["mlp_norm_post"]), name="bwd_res_mlp")
    gW["w_down"] = _matmul(act, dff, name="d_w_down", ta=True, out_dtype=BF16)
    batch1, token1 = scatter_start(("w_ple_proj", "w_ple_gate", "w_down"), 1)
    dup = _matmul(dff, w_down_f, name="d_up", tb=True, after=token1, relu2_of=up, out_dtype=BF16)
    gW["w_up"] = _matmul(hn2, dup, name="d_w_up", ta=True, out_dtype=BF16, out_shards=N_CHIPS)
    batch2, token2 = scatter_start(("w_up",), 2)
    dhn2 = _matmul(dup, w_up_f, name="d_hn2", tb=True, b_shards=N_CHIPS, after=token2)
    dh1, dmo, dg_mlp_pre, dg_mix_post = _bwd_res_norm(
        dh2, dhn2, h1, row(W["mlp_norm_pre"]), mo, row(W["mix_norm_post"]), name="bwd_res_mix")
    gW["w_out"] = _matmul(mixed, dmo, name="d_w_out", ta=True, out_dtype=BF16)
    dmixed = _matmul(dmo, w_out_f, name="d_mixed", tb=True)
    dattn_b, dd_b, dz, dy2a, dg_attn, dg_ssm, db_glu = _mix_bwd(
        dmixed, attn, y1, z, row(W["b_glu"]), row(W["attn_out_norm"]), row(W["ssm_out_norm"]))
    gW["w_glu"] = _matmul(y2b, dz, name="d_w_glu", ta=True, out_dtype=BF16)
    batch3, token3 = scatter_start(("w_out", "w_glu"), 3)
    dy2b = _matmul(dz, w_glu_f, name="d_y2", tb=True, after=token3)
    du, dar8, dai8, dcr_e, dci_e, dbr_e, dbi_e, dd8 = _ssm_bwd(
        u, y1, dy2a, dy2b, st_r, st_i, ends_r, ends_i, lr_e, li_e, ldt_e, bre_e, bim_e, cre_e, cim_e, d_row)
    swap_a, token_a = swap_begin([(1, batch1)], du, "a")
    dlr_e, dli_e, dldt_e, dbre_e, dbim_e = _ssm_param_bwd(dar8, dai8, dbr_e, dbi_e, lr_e, li_e, ldt_e, bre_e, bim_e)

    dqs, dks, dvs = zip(*[_attn_bwd(qb, da, l, dd_, d, heads, token_a)
                          for d, qb, da, l, dd_ in zip(DILATIONS, qkv_b, dattn_b, lse_b, dd_b)])
    dproj = _dproj_join(dqs, dks, dvs, du)
    swap_end(swap_a, dproj, "a")
    swap_b, token_b = swap_begin([(2, batch2), (3, batch3)], dproj, "b")
    gW["w_in"] = _matmul(hn1, dproj, name="d_w_in", ta=True, out_dtype=BF16, out_shards=N_CHIPS, after=token_b)
    batch4, token4 = scatter_start(("w_in",), 4)
    dhn1 = _matmul(dproj, w_in_f, name="d_hn1", tb=True, b_shards=N_CHIPS, after=token4)
    grad_x, dg_mix_pre = _bwd_first(dh1, dhn1, xs, row(W["mix_norm_pre"]))
    swap_end(swap_b, grad_x, "b")
    scatter_finish(batch4, grad_x, 4)

    small_g = {
        "mix_norm_pre": dg_mix_pre, "lam_re": dlr_e.reshape(G, SSM_P), "lam_im": dli_e.reshape(G, SSM_P),
        "log_dt": dldt_e.reshape(G, SSM_P)[:, 0], "ssm_b_re": _collapse_b(dbre_e), "ssm_b_im": _collapse_b(dbim_e),
        "ssm_c_re": _collapse_c(dcr_e), "ssm_c_im": _collapse_c(dci_e), "ssm_d": dd8.sum(axis=1).reshape(-1),
        "b_glu": db_glu, "attn_out_norm": dg_attn, "ssm_out_norm": dg_ssm, "mix_norm_post": dg_mix_post,
        "mlp_norm_pre": dg_mlp_pre, "mlp_norm_post": dg_mlp_post, "ple_norm_pre": dg_ple_pre,
        "ple_norm_post": dg_ple_post,
    }
    g_pack, layout = _pack([small_g[n].reshape(W[n].shape) for n in SMALL])
    w_pack, _ = _pack([W[n] for n in SMALL])
    m_pack, _ = _pack([Mo[n] for n in SMALL])
    v_pack, _ = _pack([Vo[n] for n in SMALL])
    g_sum = _all_reduce_small(g_pack)
    packed = (g_sum,) + tuple(_adamw_small(g_sum, w_pack, m_pack, v_pack))
    for dst, buf in zip((out_g, out_d, out_m, out_v), packed):
        dst.update(zip(SMALL, _unpack(buf, layout)))

    loss = lax.psum(loss_part[0, 0], ("x", "y", "c"))
    lead = lambda a: a[None]
    return (loss, grad_x[None],
            *[lead(out_g[n]) for n in WEIGHTS], *[lead(out_d[n]) for n in WEIGHTS],
            *[lead(out_m[n]) for n in WEIGHTS], *[lead(out_v[n]) for n in WEIGHTS])
```

```python
import functools
import math

import jax
import jax.numpy as jnp
from jax import lax
from jax.experimental import pallas as pl
from jax.experimental.pallas import tpu as pltpu

F32 = jnp.float32
BF16 = jnp.bfloat16
MESH = pl.DeviceIdType.MESH

RMS_EPS = 1e-6
NEG_INF = -1e30
HEAD_DIM = 128
BLK = 128
DILATIONS = (1, 4, 16)
ATTN_LOOKAHEAD = 3
SSM_C = 16
SSM_P = 64
LANES = 128
GROUPS_PER_BLOCK = LANES // SSM_C
STATE_LANES = GROUPS_PER_BLOCK * SSM_P
SSM_CHUNK = 1024
TILE = 8
ADAM_LR, ADAM_B1, ADAM_B2, ADAM_EPS, ADAM_WD, ADAM_STEP = 1e-3, 0.9, 0.999, 1e-8, 0.01, 10
VMEM_LIMIT_BYTES = 56 * 1024 * 1024
MATMUL_VMEM_BYTES = 44 * 1024 * 1024
N_CHIPS = 4
N_DEV = 8
PACK_ROWS = 256


def _cparams(*sem):
    return pltpu.CompilerParams(dimension_semantics=sem or None, vmem_limit_bytes=VMEM_LIMIT_BYTES)


def _rows(tr, w):
    return pl.BlockSpec((tr, w), lambda i: (i, 0))


def _vec(w):
    return pl.BlockSpec((1, w), lambda i: (0, 0))


def _sigmoid(x):
    return 1.0 / (1.0 + jnp.exp(-x))


def _gelu(x):
    c = math.sqrt(2.0 / math.pi)
    return 0.5 * x * (1.0 + jnp.tanh(c * (x + 0.044715 * x * x * x)))


def _gelu_grad(x):
    c = math.sqrt(2.0 / math.pi)
    th = jnp.tanh(c * (x + 0.044715 * x * x * x))
    return 0.5 * (1.0 + th) + 0.5 * x * (1.0 - th * th) * c * (1.0 + 3.0 * 0.044715 * x * x)


def _rms(x, g):
    r = lax.rsqrt(jnp.mean(x * x, axis=-1, keepdims=True) + RMS_EPS)
    return x * r * g


def _rms_bwd(dy, x, g):
    r = lax.rsqrt(jnp.mean(x * x, axis=-1, keepdims=True) + RMS_EPS)
    n = x * r
    dn = dy * g
    dx = r * (dn - n * jnp.mean(dn * n, axis=-1, keepdims=True))
    return dx, dy * n


def _colsum(a):
    return jnp.sum(a, axis=0, keepdims=True)


def _first(i):
    return i == 0


def _matmul(a, b, *, name, ta=False, tb=False, out_dtype=F32, b_shards=1, out_shards=1, b_cols=None,
            after=None, relu2=False, relu2_of=None, tm=1024, tn=2048, tk=2048):
    if ta:
        K, M = a.shape
    else:
        M, K = a.shape
    if b_shards > 1:
        rows, cols = b.shape[1], b.shape[2] * b_shards
    else:
        rows, cols = b.shape
    N, Kb = (rows, cols) if tb else (cols, rows)
    assert K == Kb, (a.shape, b.shape, ta, tb)
    col0 = 0
    if b_cols is not None:
        assert not tb
        col0, N = b_cols
    tm, tn, tk = min(tm, M), min(tn, N), min(tk, K)
    if b_shards > 1:
        shard_cols = cols // b_shards
        if tb:
            tk = min(tk, shard_cols)
        else:
            tn = min(tn, shard_cols)
    if out_shards > 1:
        tn = min(tn, N // out_shards)

    def vmem_bytes(tn_):
        out_bytes = jnp.dtype(out_dtype).itemsize + (2 if relu2 else 0)
        return (4 * (tm * tk + tk * tn_) + 2 * tm * tn_ * out_bytes
                + (2 * relu2_of.dtype.itemsize * tm * tn_ if relu2_of is not None else 0)
                + (4 * tm * tn_ if K > tk else 0))

    while vmem_bytes(tn) > MATMUL_VMEM_BYTES and tn > LANES and col0 % (tn // 2) == 0:
        tn //= 2
    assert M % tm == 0 and N % tn == 0 and K % tk == 0 and col0 % tn == 0
    nk = K // tk
    j0 = col0 // tn

    a_spec = (pl.BlockSpec((tk, tm), lambda i, j, k: (k, i)) if ta
              else pl.BlockSpec((tm, tk), lambda i, j, k: (i, k)))
    if b_shards > 1:
        if tb:
            per = shard_cols // tk
            b_spec = pl.BlockSpec((None, tn, tk), lambda i, j, k: (k // per, j, k % per))
        else:
            per = shard_cols // tn
            b_spec = pl.BlockSpec((None, tk, tn), lambda i, j, k: ((j + j0) // per, k, (j + j0) % per))
    else:
        b_spec = (pl.BlockSpec((tn, tk), lambda i, j, k: (j, k)) if tb
                  else pl.BlockSpec((tk, tn), lambda i, j, k: (k, j + j0)))
    if out_shards > 1:
        per_o = (N // out_shards) // tn
        out_shape = jax.ShapeDtypeStruct((out_shards, M, N // out_shards), out_dtype)
        out_spec = pl.BlockSpec((None, tm, tn), lambda i, j, k: (j // per_o, i, j % per_o))
    else:
        out_shape = jax.ShapeDtypeStruct((M, N), out_dtype)
        out_spec = pl.BlockSpec((tm, tn), lambda i, j, k: (i, j))
    dims = (((0 if ta else 1,), (1 if tb else 0,)), ((), ()))

    extra, extra_specs = [], []
    if relu2_of is not None:
        assert out_shards == 1 and relu2_of.shape == (M, N)
        extra.append(relu2_of)
        extra_specs.append(pl.BlockSpec((tm, tn), lambda i, j, k: (i, j)))
    if after is not None:
        extra.append(after)
        extra_specs.append(pl.BlockSpec(after.shape, lambda i, j, k: (0, 0)))
    n_in = 2 + len(extra)
    if relu2:
        assert out_shards == 1
        out_shape = (out_shape, jax.ShapeDtypeStruct((M, N), BF16))
        out_spec = (out_spec, out_spec)

    def finish(acc, refs):
        o_ref = refs[n_in]
        if relu2_of is not None:
            acc = acc * (2.0 * jnp.maximum(refs[2][...].astype(F32), 0.0))
        o_ref[...] = acc.astype(o_ref.dtype)
        if relu2:
            r = jnp.maximum(acc, 0.0)
            refs[n_in + 1][...] = (r * r).astype(BF16)

    def body(*refs):
        prod = lax.dot_general(refs[0][...], refs[1][...], dims, preferred_element_type=F32)
        if nk == 1:
            finish(prod, refs)
            return
        acc_ref = refs[-1]
        k = pl.program_id(2)

        @pl.when(k == 0)
        def _():
            acc_ref[...] = prod

        @pl.when(k > 0)
        def _():
            acc_ref[...] += prod

        @pl.when(k == nk - 1)
        def _():
            finish(acc_ref[...], refs)

    return pl.pallas_call(
        body, name=name, out_shape=out_shape, grid=(M // tm, N // tn, nk),
        in_specs=[a_spec, b_spec] + extra_specs, out_specs=out_spec,
        scratch_shapes=[pltpu.VMEM((tm, tn), F32)] if nk > 1 else [],
        compiler_params=_cparams("parallel", "parallel", "arbitrary"),
    )(a, b, *extra)


def _norm_cast(x, g, *, name, tr=256):
    S, D = x.shape
    tr = min(tr, S)

    def body(x_ref, g_ref, o_ref):
        o_ref[...] = _rms(x_ref[...], g_ref[...]).astype(BF16)

    return pl.pallas_call(
        body, name=name, out_shape=jax.ShapeDtypeStruct((S, D), BF16), grid=(S // tr,),
        in_specs=[_rows(tr, D), _vec(D)], out_specs=_rows(tr, D),
        compiler_params=_cparams("parallel"))(x, g)


def _res_norm(res, y, g_post, g_next, *, name, tr=256):
    S, D = res.shape
    tr = min(tr, S)

    def body(res_ref, y_ref, gp_ref, gn_ref, h_ref, hn_ref):
        h = res_ref[...] + _rms(y_ref[...], gp_ref[...])
        h_ref[...] = h
        hn_ref[...] = _rms(h, gn_ref[...]).astype(BF16)

    return pl.pallas_call(
        body, name=name,
        out_shape=(jax.ShapeDtypeStruct((S, D), F32), jax.ShapeDtypeStruct((S, D), BF16)),
        grid=(S // tr,), in_specs=[_rows(tr, D), _rows(tr, D), _vec(D), _vec(D)],
        out_specs=(_rows(tr, D), _rows(tr, D)), compiler_params=_cparams("parallel"))(res, y, g_post, g_next)


def _residue_spec(tr, d, w):
    return pl.BlockSpec((tr // d, d * w), lambda i: (i, 0))


def _residue_shape(S, d, w, dtype):
    return jax.ShapeDtypeStruct((S // d, d * w), dtype)


def _residue_scratch(rows, w):
    return pltpu.VMEM((w // LANES, rows, LANES), F32)


def _fill_strips(scr, val):
    for s in range(scr.shape[0]):
        scr[s] = val[:, s * LANES:(s + 1) * LANES]


def _strips_to_residues(scr, o_ref, d):
    strips, rows, _ = scr.shape
    for r in range(d):
        for s in range(strips):
            col = (r * strips + s) * LANES
            o_ref[:, col:col + LANES] = scr[s, pl.ds(r, rows // d, stride=d), :].astype(o_ref.dtype)


def _to_residues(scr, val, o_ref, d):
    if d == 1:
        o_ref[...] = val.astype(o_ref.dtype)
        return
    _fill_strips(scr, val)
    _strips_to_residues(scr, o_ref, d)


def _from_residues(scr, in_ref, d):
    if d == 1:
        return in_ref[...].astype(F32)
    strips, rows, _ = scr.shape
    for r in range(d):
        for s in range(strips):
            col = (r * strips + s) * LANES
            scr[s, pl.ds(r, rows // d, stride=d), :] = in_ref[:, col:col + LANES].astype(F32)
    return jnp.concatenate([scr[s] for s in range(strips)], axis=1)


def _spread_heads(packed, heads, width=HEAD_DIM):
    per = LANES // heads
    return jnp.concatenate([jnp.broadcast_to(packed[:, h * per:h * per + 1], (packed.shape[0], width))
                            for h in range(heads)], axis=1)


def _mix_fwd(os, ls, y1, z, b_glu, g_attn, g_ssm, *, tr=256):
    S, SW = y1.shape
    AW = os[0].shape[1] // DILATIONS[0]
    heads = AW // HEAD_DIM
    tr = min(tr, S)
    nd = len(DILATIONS)

    def body(*refs):
        o_refs, l_refs = refs[:nd], refs[nd:2 * nd]
        y_ref, z_ref, b_ref, ga_ref, gs_ref, attn_ref = refs[2 * nd:2 * nd + 6]
        lse_refs = refs[2 * nd + 6:3 * nd + 6]
        mixed_ref, scr, scr_p = refs[3 * nd + 6:]
        ls_ = [_from_residues(scr_p, l_refs[n], d) for n, d in enumerate(DILATIONS)]
        m = functools.reduce(jnp.maximum, ls_)
        es = [jnp.exp(l - m) for l in ls_]
        tot = functools.reduce(jnp.add, es)
        attn = functools.reduce(jnp.add, [_spread_heads(e / tot, heads) * _from_residues(scr, o_refs[n], d)
                                          for n, (e, d) in enumerate(zip(es, DILATIONS))])
        attn_ref[...] = attn
        lse = m + jnp.log(tot)
        for n, d in enumerate(DILATIONS):
            _to_residues(scr_p, lse, lse_refs[n], d)
        ssm = _gelu(y_ref[...]) * _sigmoid(z_ref[...] + b_ref[...])
        mixed_ref[:, :AW] = _rms(attn, ga_ref[...]).astype(BF16)
        mixed_ref[:, AW:] = _rms(ssm, gs_ref[...]).astype(BF16)

    res_o = [_residue_spec(tr, d, AW) for d in DILATIONS]
    res_l = [_residue_spec(tr, d, LANES) for d in DILATIONS]
    res = pl.pallas_call(
        body, name="mix_fwd",
        out_shape=([jax.ShapeDtypeStruct((S, AW), F32)] + [_residue_shape(S, d, LANES, F32) for d in DILATIONS]
                   + [jax.ShapeDtypeStruct((S, AW + SW), BF16)]),
        grid=(S // tr,),
        in_specs=res_o + res_l + [_rows(tr, SW), _rows(tr, SW), _vec(SW), _vec(AW), _vec(SW)],
        out_specs=[_rows(tr, AW)] + res_l + [_rows(tr, AW + SW)],
        scratch_shapes=[_residue_scratch(tr, AW), _residue_scratch(tr, LANES)],
        compiler_params=_cparams("parallel"))(*os, *ls, y1, z, b_glu, g_attn, g_ssm)
    return res[0], res[1:1 + nd], res[1 + nd]


def _final(h2, gl, e, g_post, target, *, tr=128):
    S, D = h2.shape
    tr = min(tr, S)

    def body(h_ref, gl_ref, e_ref, g_ref, t_ref, dh_ref, dgl_ref, de_ref, loss_ref, dg_ref):
        i = pl.program_id(0)
        gate = _sigmoid(gl_ref[...])
        e_ = e_ref[...]
        ge = gate * e_
        g = g_ref[...]
        diff = h_ref[...] + _rms(ge, g) - t_ref[...]
        dh = diff * (1.0 / D)
        dh_ref[...] = dh
        dge, dgrow = _rms_bwd(dh, ge, g)
        dgl_ref[...] = (dge * e_ * gate * (1.0 - gate)).astype(BF16)
        de_ref[...] = (dge * gate).astype(BF16)
        part = _colsum(0.5 * jnp.mean(diff * diff, axis=-1, keepdims=True))

        @pl.when(_first(i))
        def _():
            loss_ref[...] = jnp.zeros_like(loss_ref)
            dg_ref[...] = jnp.zeros_like(dg_ref)

        loss_ref[...] += part + jnp.zeros((1, LANES), F32)
        dg_ref[...] += _colsum(dgrow)

    return pl.pallas_call(
        body, name="final_fwd_bwd",
        out_shape=(jax.ShapeDtypeStruct((S, D), F32), jax.ShapeDtypeStruct((S, D), BF16),
                   jax.ShapeDtypeStruct((S, D), BF16), jax.ShapeDtypeStruct((1, LANES), F32),
                   jax.ShapeDtypeStruct((1, D), F32)),
        grid=(S // tr,),
        in_specs=[_rows(tr, D), _rows(tr, D), _rows(tr, D), _vec(D), _rows(tr, D)],
        out_specs=(_rows(tr, D), _rows(tr, D), _rows(tr, D), _vec(LANES), _vec(D)),
        compiler_params=_cparams("arbitrary"))(h2, gl, e, g_post, target)


def _bwd_res_norm(dh_out, dhn, h, g_next, y, g_post, *, name, tr=128):
    S, D = h.shape
    tr = min(tr, S)

    def body(dho_ref, dhn_ref, h_ref, gn_ref, y_ref, gp_ref, dh_ref, dy_ref, dgn_ref, dgp_ref):
        i = pl.program_id(0)
        dx, dgn_rows = _rms_bwd(dhn_ref[...], h_ref[...], gn_ref[...])
        dh = dho_ref[...] + dx
        dh_ref[...] = dh
        dy, dgp_rows = _rms_bwd(dh, y_ref[...], gp_ref[...])
        dy_ref[...] = dy.astype(BF16)

        @pl.when(_first(i))
        def _():
            dgn_ref[...] = jnp.zeros_like(dgn_ref)
            dgp_ref[...] = jnp.zeros_like(dgp_ref)

        dgn_ref[...] += _colsum(dgn_rows)
        dgp_ref[...] += _colsum(dgp_rows)

    return pl.pallas_call(
        body, name=name,
        out_shape=(jax.ShapeDtypeStruct((S, D), F32), jax.ShapeDtypeStruct((S, D), BF16),
                   jax.ShapeDtypeStruct((1, D), F32), jax.ShapeDtypeStruct((1, D), F32)),
        grid=(S // tr,),
        in_specs=[_rows(tr, D), _rows(tr, D), _rows(tr, D), _vec(D), _rows(tr, D), _vec(D)],
        out_specs=(_rows(tr, D), _rows(tr, D), _vec(D), _vec(D)),
        compiler_params=_cparams("arbitrary"))(dh_out, dhn, h, g_next, y, g_post)


def _bwd_first(dh1, dhn1, x, g1, *, tr=256):
    S, D = x.shape
    tr = min(tr, S)

    def body(dh_ref, dhn_ref, x_ref, g_ref, dx_ref, dg_ref):
        i = pl.program_id(0)
        dx, dg_rows = _rms_bwd(dhn_ref[...], x_ref[...], g_ref[...])
        dx_ref[...] = dh_ref[...] + dx

        @pl.when(_first(i))
        def _():
            dg_ref[...] = jnp.zeros_like(dg_ref)

        dg_ref[...] += _colsum(dg_rows)

    return pl.pallas_call(
        body, name="bwd_first",
        out_shape=(jax.ShapeDtypeStruct((S, D), F32), jax.ShapeDtypeStruct((1, D), F32)),
        grid=(S // tr,), in_specs=[_rows(tr, D), _rows(tr, D), _rows(tr, D), _vec(D)],
        out_specs=(_rows(tr, D), _vec(D)), compiler_params=_cparams("arbitrary"))(dh1, dhn1, x, g1)


def _mix_bwd(dmixed, attn, y1, z, b_glu, g_attn, g_ssm, *, tr=256):
    S, AW = attn.shape
    SW = y1.shape[1]
    tr = min(tr, S)
    heads = AW // HEAD_DIM
    nd = len(DILATIONS)

    def body(*refs):
        dm_ref, a_ref, y_ref, z_ref, b_ref, ga_ref, gs_ref = refs[:7]
        da_refs, dd_refs = refs[7:7 + nd], refs[7 + nd:7 + 2 * nd]
        dz_ref, dy2_ref, dga_ref, dgs_ref, db_ref, scr, scr_p, dd_scr = refs[7 + 2 * nd:]
        i = pl.program_id(0)
        attn_ = a_ref[...]
        dattn, dga_rows = _rms_bwd(dm_ref[:, :AW], attn_, ga_ref[...])
        prod = dattn * attn_
        per = LANES // heads
        for h in range(heads):
            total = jnp.sum(prod[:, h * HEAD_DIM:(h + 1) * HEAD_DIM], axis=-1, keepdims=True)
            dd_scr[:, h * per:(h + 1) * per] = jnp.broadcast_to(total, (tr, per))
        for n, d in enumerate(DILATIONS):
            _to_residues(scr, dattn, da_refs[n], d)
            _to_residues(scr_p, dd_scr[...], dd_refs[n], d)
        y2 = _gelu(y_ref[...])
        gate = _sigmoid(z_ref[...] + b_ref[...])
        dssm, dgs_rows = _rms_bwd(dm_ref[:, AW:], y2 * gate, gs_ref[...])
        dz = dssm * y2 * gate * (1.0 - gate)
        dz_ref[...] = dz.astype(BF16)
        dy2_ref[...] = dssm * gate

        @pl.when(_first(i))
        def _():
            dga_ref[...] = jnp.zeros_like(dga_ref)
            dgs_ref[...] = jnp.zeros_like(dgs_ref)
            db_ref[...] = jnp.zeros_like(db_ref)

        dga_ref[...] += _colsum(dga_rows)
        dgs_ref[...] += _colsum(dgs_rows)
        db_ref[...] += _colsum(dz)

    res_a = [_residue_spec(tr, d, AW) for d in DILATIONS]
    res_d = [_residue_spec(tr, d, LANES) for d in DILATIONS]
    res = pl.pallas_call(
        body, name="mix_bwd",
        out_shape=([_residue_shape(S, d, AW, BF16) for d in DILATIONS]
                   + [_residue_shape(S, d, LANES, F32) for d in DILATIONS]
                   + [jax.ShapeDtypeStruct((S, SW), BF16), jax.ShapeDtypeStruct((S, SW), F32),
                      jax.ShapeDtypeStruct((1, AW), F32), jax.ShapeDtypeStruct((1, SW), F32),
                      jax.ShapeDtypeStruct((1, SW), F32)]),
        grid=(S // tr,),
        in_specs=[_rows(tr, AW + SW), _rows(tr, AW), _rows(tr, SW), _rows(tr, SW), _vec(SW), _vec(AW), _vec(SW)],
        out_specs=res_a + res_d + [_rows(tr, SW), _rows(tr, SW), _vec(AW), _vec(SW), _vec(SW)],
        scratch_shapes=[_residue_scratch(tr, AW), _residue_scratch(tr, LANES), pltpu.VMEM((tr, LANES), F32)],
        compiler_params=_cparams("arbitrary"))(dmixed, attn, y1, z, b_glu, g_attn, g_ssm)
    return (res[:nd], res[nd:2 * nd]) + tuple(res[2 * nd:])


def _attn_mask2(i):
    row = lax.broadcasted_iota(jnp.int32, (BLK, 2 * BLK), 0)
    col = lax.broadcasted_iota(jnp.int32, (BLK, 2 * BLK), 1)
    return jnp.logical_and(col >= row, jnp.logical_and(col <= row + BLK, jnp.logical_or(col >= BLK, i > 0)))


_NT = (((1,), (1,)), ((), ()))
_TN = (((0,), (0,)), ((), ()))


def _attn_in_specs(width, block_of):
    def at(part, prev):
        def index(r, i):
            blk = block_of(i)
            return (part, jnp.maximum(blk - 1, 0) if prev else blk, r)
        return pl.BlockSpec((None, BLK, width), index)
    return [at(0, False), at(1, False), at(1, True), at(2, False), at(2, True)]


def _proj_qkv(hn, w_in_f, *, tm=1024):
    S, D = hn.shape
    AW = w_in_f.shape[2]
    tm = min(tm, S)

    def body(a_ref, b_ref, *rest):
        o_refs, scr = rest[:-1], rest[-1]
        prod = jnp.dot(a_ref[...], b_ref[...], preferred_element_type=F32)
        _fill_strips(scr, prod)
        for o_ref, d in zip(o_refs, DILATIONS):
            if d == 1:
                o_ref[...] = prod.astype(BF16)
            else:
                _strips_to_residues(scr, o_ref, d)

    return pl.pallas_call(
        body, name="proj_qkv",
        out_shape=[jax.ShapeDtypeStruct((3, S // d, d * AW), BF16) for d in DILATIONS], grid=(S // tm, 3),
        in_specs=[pl.BlockSpec((tm, D), lambda i, j: (i, 0)), pl.BlockSpec((None, D, AW), lambda i, j: (j, 0, 0))],
        out_specs=[pl.BlockSpec((None, tm // d, d * AW), lambda i, j: (j, i, 0)) for d in DILATIONS],
        scratch_shapes=[_residue_scratch(tm, AW)],
        compiler_params=_cparams("parallel", "parallel"))(hn, w_in_f)


def _attn_fwd(qkv, d, heads):
    M = qkv.shape[1]
    nb = M // BLK
    width = heads * HEAD_DIM
    per = LANES // heads
    scale = 1.0 / math.sqrt(HEAD_DIM)

    def body(q_ref, kc_ref, kp_ref, vc_ref, vp_ref, o_ref, l_ref):
        mask = _attn_mask2(pl.program_id(1))
        ones = jnp.ones((2 * BLK, HEAD_DIM), BF16)

        def scores(h):
            sl = slice(h * HEAD_DIM, (h + 1) * HEAD_DIM)
            k2 = jnp.concatenate([kp_ref[:, sl], kc_ref[:, sl]], axis=0)
            return lax.dot_general(q_ref[:, sl], k2, _NT, preferred_element_type=F32)

        ahead = [scores(h) for h in range(min(ATTN_LOOKAHEAD, heads))]
        for h in range(heads):
            sl = slice(h * HEAD_DIM, (h + 1) * HEAD_DIM)
            s = jnp.where(mask, ahead.pop(0) * scale, NEG_INF)
            if h + ATTN_LOOKAHEAD < heads:
                ahead.append(scores(h + ATTN_LOOKAHEAD))
            v2 = jnp.concatenate([vp_ref[:, sl], vc_ref[:, sl]], axis=0)
            m = jnp.max(jnp.maximum(s[:, :BLK], s[:, BLK:]), axis=-1, keepdims=True)
            p = jnp.exp(s - m).astype(BF16)
            tot = jnp.dot(p, ones, preferred_element_type=F32)
            o_ref[:, sl] = (jnp.dot(p, v2, preferred_element_type=F32) / tot).astype(BF16)
            l_ref[:, h * per:(h + 1) * per] = m + jnp.log(tot[:, :per])

    return pl.pallas_call(
        body, name=f"attn_fwd_d{d}",
        out_shape=(jax.ShapeDtypeStruct((M, d * width), BF16), jax.ShapeDtypeStruct((M, d * LANES), F32)),
        grid=(d, nb), in_specs=_attn_in_specs(width, lambda i: i),
        out_specs=(pl.BlockSpec((BLK, width), lambda r, i: (i, r)), pl.BlockSpec((BLK, LANES), lambda r, i: (i, r))),
        compiler_params=_cparams("parallel", "parallel"))(qkv, qkv, qkv, qkv, qkv)


def _attn_bwd(qkv, dattn, lse, dd, d, heads, after):
    M = qkv.shape[1]
    nb = M // BLK
    width = heads * HEAD_DIM
    per = LANES // heads
    scale = 1.0 / math.sqrt(HEAD_DIM)

    def block_of(i):
        return nb - 1 - i

    def body(q_ref, kc_ref, kp_ref, vc_ref, vp_ref, da_ref, l_ref, dd_ref, after_ref,
             dq_ref, dk_ref, dv_ref, dk_carry, dv_carry):
        @pl.when(pl.program_id(1) == 0)
        def _():
            dk_carry[...] = jnp.zeros_like(dk_carry)
            dv_carry[...] = jnp.zeros_like(dv_carry)

        mask = _attn_mask2(block_of(pl.program_id(1)))

        def products(h):
            sl = slice(h * HEAD_DIM, (h + 1) * HEAD_DIM)
            k2 = jnp.concatenate([kp_ref[:, sl], kc_ref[:, sl]], axis=0)
            v2 = jnp.concatenate([vp_ref[:, sl], vc_ref[:, sl]], axis=0)
            return (lax.dot_general(q_ref[:, sl], k2, _NT, preferred_element_type=F32),
                    lax.dot_general(da_ref[:, sl], v2, _NT, preferred_element_type=F32), k2)

        ahead = [products(h) for h in range(min(ATTN_LOOKAHEAD, heads))]
        for h in range(heads):
            sl = slice(h * HEAD_DIM, (h + 1) * HEAD_DIM)
            qk, dp, k2 = ahead.pop(0)
            if h + ATTN_LOOKAHEAD < heads:
                ahead.append(products(h + ATTN_LOOKAHEAD))
            q, da = q_ref[:, sl], da_ref[:, sl]
            lse_ = jnp.broadcast_to(l_ref[:, h * per:h * per + 1], (BLK, 2 * BLK))
            dd_ = jnp.broadcast_to(dd_ref[:, h * per:h * per + 1], (BLK, 2 * BLK))
            p = jnp.where(mask, jnp.exp(jnp.where(mask, qk * scale, NEG_INF) - lse_), 0.0)
            ds = (p * (dp - dd_) * scale).astype(BF16)
            dq_ref[:, sl] = jnp.dot(ds, k2, preferred_element_type=F32).astype(BF16)
            dk2 = lax.dot_general(ds, q, _TN, preferred_element_type=F32)
            dv2 = lax.dot_general(p.astype(BF16), da, _TN, preferred_element_type=F32)
            dk_ref[:, sl] = (dk2[BLK:] + dk_carry[:, sl]).astype(BF16)
            dv_ref[:, sl] = (dv2[BLK:] + dv_carry[:, sl]).astype(BF16)
            dk_carry[:, sl] = dk2[:BLK]
            dv_carry[:, sl] = dv2[:BLK]

    blk = pl.BlockSpec((BLK, width), lambda r, i: (block_of(i), r))
    packed = pl.BlockSpec((BLK, LANES), lambda r, i: (block_of(i), r))
    shape = jax.ShapeDtypeStruct((M, d * width), BF16)
    return pl.pallas_call(
        body, name=f"attn_bwd_d{d}", out_shape=(shape,) * 3, grid=(d, nb),
        in_specs=(_attn_in_specs(width, block_of) + [blk, packed, packed]
                  + [pl.BlockSpec(after.shape, lambda r, i: (0, 0))]), out_specs=(blk,) * 3,
        scratch_shapes=[pltpu.VMEM((BLK, width), F32), pltpu.VMEM((BLK, width), F32)],
        compiler_params=_cparams("arbitrary", "arbitrary"))(qkv, qkv, qkv, qkv, qkv, dattn, lse, dd, after)


def _dproj_join(dqs, dks, dvs, du, *, tr=256):
    S, SW = du.shape
    AW = dqs[0].shape[1]
    tr = min(tr, S)
    nd = len(DILATIONS)

    def body(*refs):
        du_ref, out_ref, scr = refs[3 * nd:]
        for part in range(3):
            total = functools.reduce(jnp.add, [_from_residues(scr, refs[part * nd + n], d)
                                               for n, d in enumerate(DILATIONS)])
            out_ref[:, part * AW:(part + 1) * AW] = total.astype(BF16)
        out_ref[:, 3 * AW:] = du_ref[...].astype(BF16)

    return pl.pallas_call(
        body, name="dproj_join", out_shape=jax.ShapeDtypeStruct((S, 3 * AW + SW), BF16), grid=(S // tr,),
        in_specs=[_residue_spec(tr, d, AW) for d in DILATIONS] * 3 + [_rows(tr, SW)],
        out_specs=_rows(tr, 3 * AW + SW), scratch_shapes=[_residue_scratch(tr, AW)],
        compiler_params=_cparams("parallel"))(*dqs, *dks, *dvs, du)


def _ssm_disc(lr, li, ldt):
    dt = jnp.exp(ldt)
    mag = jnp.exp(lr * dt)
    ar = mag * jnp.cos(li * dt)
    ai = mag * jnp.sin(li * dt)
    nr = ar - 1.0
    den = lr * lr + li * li
    return ar, ai, (nr * lr + ai * li) / den, (ai * lr - nr * li) / den


def _ssm_tile_powers(lr, li, ldt, reverse):
    t = lax.broadcasted_iota(jnp.int32, (TILE, 1), 0)
    n = (TILE - t if reverse else t + 1).astype(F32)
    dt = jnp.exp(ldt)
    mag = jnp.exp(n * (lr * dt))
    ang = n * (li * dt)
    return mag * jnp.cos(ang), mag * jnp.sin(ang) * (-1.0 if reverse else 1.0)


def _cmul(ar, ai, br, bi):
    return ar * br - ai * bi, ar * bi + ai * br


LOG_STEPS = 3


def _ssm_step_tables(ar, ai, reverse):
    sub = lax.broadcasted_iota(jnp.int32, (TILE, ar.shape[-1]), 0)
    tables = []
    for k in range(LOG_STEPS):
        keep = sub < TILE - (1 << k) if reverse else sub >= (1 << k)
        tables.append((jnp.where(keep, ar, 0.0), jnp.where(keep, ai, 0.0)))
        ar, ai = _cmul(ar, ai, ar, ai)
    return tables


def _scan(xr, xi, steps, pr, pi, cr, ci, reverse):
    T, lanes = xr.shape
    n = T // TILE
    xr, xi = xr.reshape(n, TILE, lanes), xi.reshape(n, TILE, lanes)
    for k, (mr, mi) in enumerate(steps):
        shift = TILE - (1 << k) if reverse else 1 << k
        qr, qi = _cmul(mr, mi, pltpu.roll(xr, shift, 1), pltpu.roll(xi, shift, 1))
        xr, xi = xr + qr, xi + qi
    out_r, out_i = [None] * n, [None] * n
    edge = 0 if reverse else TILE - 1
    for j in (reversed(range(n)) if reverse else range(n)):
        er, ei = _cmul(pr, pi, cr, ci)
        sr, si = xr[j] + er, xi[j] + ei
        out_r[j], out_i[j] = sr, si
        cr, ci = sr[edge:edge + 1], si[edge:edge + 1]
    return jnp.concatenate(out_r, axis=0), jnp.concatenate(out_i, axis=0), cr, ci


def _ssm_specs(T, nch, rev):
    def t_of(c):
        return nch - 1 - c if rev else c
    tok = pl.BlockSpec((T, LANES), lambda j, c: (t_of(c), j))
    par = pl.BlockSpec((None, 1, STATE_LANES), lambda j, c: (j, 0, 0))
    bmat = pl.BlockSpec((None, LANES, STATE_LANES), lambda j, c: (j, 0, 0))
    cmat = pl.BlockSpec((None, STATE_LANES, LANES), lambda j, c: (j, 0, 0))
    dvec = pl.BlockSpec((1, LANES), lambda j, c: (0, j))
    return tok, par, bmat, cmat, dvec


def _ssm_fwd(u, lr_e, li_e, ldt_e, bre_e, bim_e, cre_e, cim_e, d_skip):
    S, SW = u.shape
    T = min(SSM_CHUNK, S)
    nch, nbk = S // T, SW // LANES
    tok, par, bmat, cmat, dvec = _ssm_specs(T, nch, False)
    state_spec = pl.BlockSpec((T, STATE_LANES), lambda j, c: (c, j))
    carry_spec = pl.BlockSpec((None, 1, STATE_LANES), lambda j, c: (c, 0, j))

    def body(u_ref, lr_ref, li_ref, ldt_ref, bre_ref, bim_ref, cre_ref, cim_ref, d_ref,
             y_ref, y2_ref, sr_ref, si_ref, er_ref, ei_ref, bbr, bbi, steps, pw, carry):
        c = pl.program_id(1)

        @pl.when(c == 0)
        def _():
            lr, li, ldt = lr_ref[...], li_ref[...], ldt_ref[...]
            ar, ai, kr, ki = _ssm_disc(lr, li, ldt)
            for k, (mr, mi) in enumerate(_ssm_step_tables(ar, ai, False)):
                steps[0, k], steps[1, k] = mr, mi
            bbr[...] = (kr * bre_ref[...] - ki * bim_ref[...]).astype(BF16)
            bbi[...] = (kr * bim_ref[...] + ki * bre_ref[...]).astype(BF16)
            pw[0], pw[1] = _ssm_tile_powers(lr, li, ldt, False)
            carry[...] = jnp.zeros_like(carry)

        u_ = u_ref[...]
        ub = u_.astype(BF16)
        sr, si, cr, ci = _scan(jnp.dot(ub, bbr[...], preferred_element_type=F32),
                               jnp.dot(ub, bbi[...], preferred_element_type=F32),
                               [(steps[0, k], steps[1, k]) for k in range(LOG_STEPS)],
                               pw[0], pw[1], carry[0], carry[1], False)
        carry[0], carry[1] = cr, ci
        er_ref[...], ei_ref[...] = cr, ci
        sr_ref[...], si_ref[...] = sr, si
        y0 = (jnp.dot(sr.astype(BF16), cre_ref[...].astype(BF16), preferred_element_type=F32)
              - jnp.dot(si.astype(BF16), cim_ref[...].astype(BF16), preferred_element_type=F32))
        y1 = y0 + d_ref[...] * u_
        y_ref[...] = y1
        y2_ref[...] = _gelu(y1).astype(BF16)

    states = jax.ShapeDtypeStruct((S, nbk * STATE_LANES), F32)
    ends = jax.ShapeDtypeStruct((nch, 1, nbk * STATE_LANES), F32)
    return pl.pallas_call(
        body, name="ssm_fwd",
        out_shape=(jax.ShapeDtypeStruct((S, SW), F32), jax.ShapeDtypeStruct((S, SW), BF16), states, states, ends, ends),
        grid=(nbk, nch), in_specs=[tok, par, par, par, bmat, bmat, cmat, cmat, dvec],
        out_specs=(tok, tok, state_spec, state_spec, carry_spec, carry_spec),
        scratch_shapes=[pltpu.VMEM((LANES, STATE_LANES), BF16), pltpu.VMEM((LANES, STATE_LANES), BF16),
                        pltpu.VMEM((2, LOG_STEPS, TILE, STATE_LANES), F32), pltpu.VMEM((2, TILE, STATE_LANES), F32),
                        pltpu.VMEM((2, 1, STATE_LANES), F32)],
        compiler_params=_cparams("arbitrary", "arbitrary"),
    )(u, lr_e, li_e, ldt_e, bre_e, bim_e, cre_e, cim_e, d_skip)


def _ssm_bwd(u, y1, dy2a, dy2b, st_r, st_i, ends_r, ends_i, lr_e, li_e, ldt_e, bre_e, bim_e, cre_e, cim_e, d_skip):
    S, SW = u.shape
    T = min(SSM_CHUNK, S)
    nch, nbk = S // T, SW // LANES
    tok, par, bmat, cmat, dvec = _ssm_specs(T, nch, True)
    state_spec = pl.BlockSpec((T, STATE_LANES), lambda j, c: (nch - 1 - c, j))
    prev_spec = pl.BlockSpec((None, 1, STATE_LANES), lambda j, c: (jnp.maximum(nch - 2 - c, 0), 0, j))
    acc8 = pl.BlockSpec((None, 8, STATE_LANES), lambda j, c: (j, 0, 0))
    dd8 = pl.BlockSpec((None, 8, LANES), lambda j, c: (j, 0, 0))

    def body(u_ref, y_ref, da_ref, db_ref, sr_ref, si_ref, pr_ref, pi_ref, lr_ref, li_ref, ldt_ref,
             bre_ref, bim_ref, cre_ref, cim_ref, d_ref,
             du_ref, dar_ref, dai_ref, dcr_ref, dci_ref, dbr_ref, dbi_ref, ddk_ref,
             bbr, bbi, steps, pw, carry):
        c = pl.program_id(1)

        @pl.when(c == 0)
        def _():
            lr, li, ldt = lr_ref[...], li_ref[...], ldt_ref[...]
            ar, ai, kr, ki = _ssm_disc(lr, li, ldt)
            for k, (mr, mi) in enumerate(_ssm_step_tables(ar, -ai, True)):
                steps[0, k], steps[1, k] = mr, mi
            bbr[...] = (kr * bre_ref[...] - ki * bim_ref[...]).astype(BF16)
            bbi[...] = (kr * bim_ref[...] + ki * bre_ref[...]).astype(BF16)
            pw[0], pw[1] = _ssm_tile_powers(lr, li, ldt, True)
            carry[...] = jnp.zeros_like(carry)
            for ref in (dar_ref, dai_ref, dcr_ref, dci_ref, dbr_ref, dbi_ref, ddk_ref):
                ref[...] = jnp.zeros_like(ref)

        u_ = u_ref[...]
        ub = u_.astype(BF16)
        dy1 = (da_ref[...] + db_ref[...]) * _gelu_grad(y_ref[...])
        dyb = dy1.astype(BF16)

        sr, si = sr_ref[...], si_ref[...]
        has_prev = c < nch - 1
        s0r = jnp.where(has_prev, pr_ref[...], 0.0)
        s0i = jnp.where(has_prev, pi_ref[...], 0.0)

        cre_b, cim_b = cre_ref[...].astype(BF16), cim_ref[...].astype(BF16)
        gr, gi, cr, ci = _scan(lax.dot_general(dyb, cre_b, _NT, preferred_element_type=F32),
                               -lax.dot_general(dyb, cim_b, _NT, preferred_element_type=F32),
                               [(steps[0, k], steps[1, k]) for k in range(LOG_STEPS)],
                               pw[0], pw[1], carry[0], carry[1], True)
        carry[0], carry[1] = cr, ci

        row = lax.broadcasted_iota(jnp.int32, (T, STATE_LANES), 0)
        spr = jnp.where(row == 0, s0r, pltpu.roll(sr, 1, 0))
        spi = jnp.where(row == 0, s0i, pltpu.roll(si, 1, 0))

        def fold(a):
            return jnp.sum(a.reshape(T // 8, 8, a.shape[-1]), axis=0)

        dar_ref[...] += fold(gr * spr + gi * spi)
        dai_ref[...] += fold(gi * spr - gr * spi)
        srb, sib, grb, gib = sr.astype(BF16), si.astype(BF16), gr.astype(BF16), gi.astype(BF16)
        dcr_ref[...] += lax.dot_general(srb, dyb, _TN, preferred_element_type=F32)
        dci_ref[...] -= lax.dot_general(sib, dyb, _TN, preferred_element_type=F32)
        dbr_ref[...] += lax.dot_general(ub, grb, _TN, preferred_element_type=F32)
        dbi_ref[...] += lax.dot_general(ub, gib, _TN, preferred_element_type=F32)
        du_ref[...] = (lax.dot_general(grb, bbr[...], _NT, preferred_element_type=F32)
                       + lax.dot_general(gib, bbi[...], _NT, preferred_element_type=F32)
                       + dy1 * d_ref[...])
        ddk_ref[...] += fold(dy1 * u_)

    return pl.pallas_call(
        body, name="ssm_bwd",
        out_shape=(jax.ShapeDtypeStruct((S, SW), F32),
                   jax.ShapeDtypeStruct((nbk, 8, STATE_LANES), F32), jax.ShapeDtypeStruct((nbk, 8, STATE_LANES), F32),
                   jax.ShapeDtypeStruct((nbk, STATE_LANES, LANES), F32), jax.ShapeDtypeStruct((nbk, STATE_LANES, LANES), F32),
                   jax.ShapeDtypeStruct((nbk, LANES, STATE_LANES), F32), jax.ShapeDtypeStruct((nbk, LANES, STATE_LANES), F32),
                   jax.ShapeDtypeStruct((nbk, 8, LANES), F32)),
        grid=(nbk, nch),
        in_specs=[tok, tok, tok, tok, state_spec, state_spec, prev_spec, prev_spec, par, par, par,
                  bmat, bmat, cmat, cmat, dvec],
        out_specs=(tok, acc8, acc8, cmat, cmat, bmat, bmat, dd8),
        scratch_shapes=[pltpu.VMEM((LANES, STATE_LANES), BF16), pltpu.VMEM((LANES, STATE_LANES), BF16),
                        pltpu.VMEM((2, LOG_STEPS, TILE, STATE_LANES), F32), pltpu.VMEM((2, TILE, STATE_LANES), F32),
                        pltpu.VMEM((2, 1, STATE_LANES), F32)],
        compiler_params=_cparams("arbitrary", "arbitrary"),
    )(u, y1, dy2a, dy2b, st_r, st_i, ends_r, ends_i, lr_e, li_e, ldt_e, bre_e, bim_e, cre_e, cim_e, d_skip)


def _ssm_param_bwd(dar8, dai8, dbr_e, dbi_e, lr_e, li_e, ldt_e, bre_e, bim_e):
    nbk = lr_e.shape[0]
    par = pl.BlockSpec((None, 1, STATE_LANES), lambda j: (j, 0, 0))
    acc8 = pl.BlockSpec((None, 8, STATE_LANES), lambda j: (j, 0, 0))
    bmat = pl.BlockSpec((None, LANES, STATE_LANES), lambda j: (j, 0, 0))

    def body(dar_ref, dai_ref, dbr_ref, dbi_ref, lr_ref, li_ref, ldt_ref, bre_ref, bim_ref,
             dlr_ref, dli_ref, dldt_ref, dbre_ref, dbim_ref):
        lr, li, ldt = lr_ref[...], li_ref[...], ldt_ref[...]
        (ar, ai, kr, ki), vjp = jax.vjp(_ssm_disc, lr, li, ldt)
        dbr, dbi, bre, bim = dbr_ref[...], dbi_ref[...], bre_ref[...], bim_ref[...]
        dbre_ref[...] = kr * dbr + ki * dbi
        dbim_ref[...] = kr * dbi - ki * dbr
        dkr = _colsum(dbr * bre + dbi * bim)
        dki = _colsum(dbi * bre - dbr * bim)
        dlr, dli, dldt = vjp((_colsum(dar_ref[...]), _colsum(dai_ref[...]), dkr, dki))
        dlr_ref[...] = dlr
        dli_ref[...] = dli
        tot = jnp.broadcast_to(dldt, (8, STATE_LANES))
        sh = 1
        while sh < SSM_P:
            tot = tot + pltpu.roll(tot, STATE_LANES - sh, 1)
            sh *= 2
        dldt_ref[...] = tot[:1]

    vec = jax.ShapeDtypeStruct((nbk, 1, STATE_LANES), F32)
    mat = jax.ShapeDtypeStruct((nbk, LANES, STATE_LANES), F32)
    return pl.pallas_call(
        body, name="ssm_param_bwd", out_shape=(vec, vec, vec, mat, mat), grid=(nbk,),
        in_specs=[acc8, acc8, bmat, bmat, par, par, par, bmat, bmat],
        out_specs=(par, par, par, bmat, bmat), compiler_params=_cparams("parallel"),
    )(dar8, dai8, dbr_e, dbi_e, lr_e, li_e, ldt_e, bre_e, bim_e)


def _expand_b(b):
    G = b.shape[0]
    bt = b.transpose(0, 2, 1).reshape(G // GROUPS_PER_BLOCK, GROUPS_PER_BLOCK, SSM_C, SSM_P)
    eye = jnp.eye(GROUPS_PER_BLOCK, dtype=b.dtype)
    return (bt[:, :, :, None, :] * eye[None, :, None, :, None]).reshape(G // GROUPS_PER_BLOCK, LANES, STATE_LANES)


def _collapse_b(be):
    nbk = be.shape[0]
    eye = jnp.eye(GROUPS_PER_BLOCK, dtype=be.dtype)
    d5 = be.reshape(nbk, GROUPS_PER_BLOCK, SSM_C, GROUPS_PER_BLOCK, SSM_P)
    d4 = (d5 * eye[None, :, None, :, None]).sum(axis=3)
    return d4.transpose(0, 1, 3, 2).reshape(nbk * GROUPS_PER_BLOCK, SSM_P, SSM_C)


def _expand_c(cm):
    G = cm.shape[0]
    ct = cm.transpose(0, 2, 1).reshape(G // GROUPS_PER_BLOCK, GROUPS_PER_BLOCK, SSM_P, SSM_C)
    eye = jnp.eye(GROUPS_PER_BLOCK, dtype=cm.dtype)
    return (ct[:, :, :, None, :] * eye[None, :, None, :, None]).reshape(G // GROUPS_PER_BLOCK, STATE_LANES, LANES)


def _collapse_c(ce):
    nbk = ce.shape[0]
    eye = jnp.eye(GROUPS_PER_BLOCK, dtype=ce.dtype)
    d5 = ce.reshape(nbk, GROUPS_PER_BLOCK, SSM_P, GROUPS_PER_BLOCK, SSM_C)
    d4 = (d5 * eye[None, :, None, :, None]).sum(axis=3)
    return d4.transpose(0, 1, 3, 2).reshape(nbk * GROUPS_PER_BLOCK, SSM_C, SSM_P)


def _place():
    x, y, c = lax.axis_index("x"), lax.axis_index("y"), lax.axis_index("c")
    return x, y, c


def _other_chips(x, y):
    return [(1 - x, y), (x, 1 - y), (1 - x, 1 - y)]


_ANY = pl.BlockSpec(memory_space=pl.ANY)


_HBM = pl.BlockSpec(memory_space=pltpu.HBM)
_SEM = pl.BlockSpec(memory_space=pltpu.SEMAPHORE)
_EFFECT = pltpu.SideEffectType.DATAFLOW_SIDE_EFFECTING
_TOKEN = jax.ShapeDtypeStruct((8, LANES), F32)


def _hbm(a):
    return pltpu.with_memory_space_constraint(a, pltpu.HBM)


def _place_own(src, *, gather, name, after=None, tr=512):
    R, C = src.shape[-2:]
    tr = min(tr, R)
    x, y, _ = _place()
    me = (2 * x + y).astype(jnp.int32).reshape(1)
    extra = [] if after is None else [after]

    def body(me_ref, s_ref, *rest):
        rest[-1][...] = s_ref[...].astype(BF16)

    own = pl.BlockSpec((None, tr, C), lambda i, me_ref: (me_ref[0], i, 0))
    grid_spec = pltpu.PrefetchScalarGridSpec(
        num_scalar_prefetch=1, grid=(R // tr,),
        in_specs=([pl.BlockSpec((tr, C), lambda i, me_ref: (i, 0)) if gather else own]
                  + [pl.BlockSpec(a.shape, lambda i, me_ref: (0, 0)) for a in extra]), out_specs=own)
    return pl.pallas_call(
        body, name=name, grid_spec=grid_spec, out_shape=jax.ShapeDtypeStruct((N_CHIPS, R, C), BF16),
        compiler_params=_cparams("parallel"))(me, src, *extra)


def _exchange_copy(src_slot, land_slot, send, recv, k, j, peer, c):
    return pltpu.make_async_remote_copy(
        src_ref=src_slot, dst_ref=land_slot, send_sem=send.at[3 * k + j], recv_sem=recv.at[3 * k + j],
        device_id=(peer[0], peer[1], c), device_id_type=MESH)


def _exchange_start(lands, srcs, groups, *, name):
    n, ng = len(lands), len(groups)
    bufs = list(lands) + list(srcs)
    nb = len(bufs)

    def body(*refs):
        lnd, src, sems = refs[:n], refs[n:nb], refs[nb:nb + 2 * ng]
        token = refs[2 * nb + 2 * ng]
        x, y, c = _place()
        me = 2 * x + y
        for gi, group in enumerate(groups):
            for k, w in enumerate(group):
                for j, peer in enumerate(_other_chips(x, y)):
                    if src:
                        sent, dst = src[w].at[2 * peer[0] + peer[1]], lnd[w].at[me]
                    else:
                        sent = dst = lnd[w].at[me, c]
                    _exchange_copy(sent, dst, sems[2 * gi], sems[2 * gi + 1], k, j, peer, c).start()
        token[...] = jnp.zeros_like(token)

    sem_shapes = [pltpu.SemaphoreType.DMA((3 * len(g),)) for g in groups for _ in range(2)]
    res = pl.pallas_call(
        body, name=name,
        out_shape=sem_shapes + [pltpu.HBM(a.shape, a.dtype) for a in bufs] + [_TOKEN],
        in_specs=[_HBM] * nb,
        out_specs=[_SEM] * (2 * ng) + [_HBM] * nb + [pl.BlockSpec(memory_space=pltpu.VMEM)],
        input_output_aliases={i: 2 * ng + i for i in range(nb)},
        compiler_params=pltpu.CompilerParams(has_side_effects=_EFFECT),
    )(*[_hbm(a) for a in bufs])
    sems = [(res[2 * gi], res[2 * gi + 1]) for gi in range(ng)]
    return sems, res[2 * ng:2 * ng + n], res[2 * ng + n:2 * ng + nb], res[-1]


def _exchange_wait(lands, srcs, sems, after, *, name):
    n = len(lands)
    bufs = list(lands) + list(srcs)
    nb = len(bufs)
    send_sems, recv_sems = sems

    def body(*refs):
        lnd, src, send, recv = refs[:n], refs[n:nb], refs[nb], refs[nb + 1]
        x, y, c = _place()
        for k in range(n):
            for j, peer in enumerate(_other_chips(x, y)):
                slot = 2 * peer[0] + peer[1]
                if src:
                    copy = _exchange_copy(src[k].at[slot], lnd[k].at[slot], send, recv, k, j, peer, c)
                else:
                    copy = _exchange_copy(lnd[k].at[slot, c], lnd[k].at[slot, c], send, recv, k, j, peer, c)
                copy.wait_send()
                copy.wait_recv()

    res = pl.pallas_call(
        body, name=name, out_shape=[pltpu.HBM(a.shape, a.dtype) for a in bufs],
        in_specs=[_HBM] * nb + [_SEM, _SEM, _ANY], out_specs=[_HBM] * nb,
        input_output_aliases={i: i for i in range(nb)},
        compiler_params=pltpu.CompilerParams(has_side_effects=_EFFECT),
    )(*bufs, send_sems, recv_sems, after)
    return res[:n]


def _pair_fill(lands, *, name):
    n = len(lands)

    def body(*refs):
        ins, outs, send, recv = refs[:n], refs[n:2 * n], refs[2 * n], refs[2 * n + 1]
        x, y, c = _place()
        for w in range(n):
            for j, (px, py) in enumerate(_other_chips(x, y)):
                slot = 2 * px + py
                pltpu.make_async_remote_copy(
                    src_ref=ins[w].at[slot, c], dst_ref=outs[w].at[slot, c], send_sem=send.at[3 * w + j],
                    recv_sem=recv.at[3 * w + j], device_id=(x, y, 1 - c), device_id_type=MESH).start()
        for w in range(n):
            for j, (px, py) in enumerate(_other_chips(x, y)):
                slot = 2 * px + py
                arrival = pltpu.make_async_remote_copy(
                    src_ref=ins[w].at[slot, c], dst_ref=outs[w].at[slot, 1 - c], send_sem=send.at[3 * w + j],
                    recv_sem=recv.at[3 * w + j], device_id=(x, y, 1 - c), device_id_type=MESH)
                arrival.wait_recv()
                arrival.wait_send()

    return pl.pallas_call(
        body, name=name, out_shape=[jax.ShapeDtypeStruct(a.shape, a.dtype) for a in lands],
        in_specs=[_ANY] * n, out_specs=[_ANY] * n, input_output_aliases={i: i for i in range(n)},
        scratch_shapes=[pltpu.SemaphoreType.DMA((3 * n,)), pltpu.SemaphoreType.DMA((3 * n,))],
    )(*lands)


def _pair_copy(src, dst, send, recv, w, j, sibling):
    return pltpu.make_async_remote_copy(
        src_ref=src, dst_ref=dst, send_sem=send.at[3 * w + j], recv_sem=recv.at[3 * w + j],
        device_id=sibling, device_id_type=MESH)


def _pair_start(lands, *, name):
    n = len(lands)

    def body(*refs):
        bufs, send, recv, token = refs[:n], refs[n], refs[n + 1], refs[2 * n + 2]
        x, y, c = _place()
        for w in range(n):
            for j, (px, py) in enumerate(_other_chips(x, y)):
                half = bufs[w].at[2 * px + py, c]
                _pair_copy(half, half, send, recv, w, j, (x, y, 1 - c)).start()
        token[...] = jnp.zeros_like(token)

    res = pl.pallas_call(
        body, name=name,
        out_shape=[pltpu.SemaphoreType.DMA((3 * n,))] * 2 + [pltpu.HBM(a.shape, a.dtype) for a in lands] + [_TOKEN],
        in_specs=[_HBM] * n, out_specs=[_SEM, _SEM] + [_HBM] * n + [pl.BlockSpec(memory_space=pltpu.VMEM)],
        input_output_aliases={i: 2 + i for i in range(n)},
        compiler_params=pltpu.CompilerParams(has_side_effects=_EFFECT),
    )(*[_hbm(a) for a in lands])
    return (res[0], res[1]), res[2:2 + n], res[-1]


def _pair_wait(lands, sems, after, *, name):
    n = len(lands)

    def body(*refs):
        bufs, send, recv = refs[:n], refs[n], refs[n + 1]
        x, y, c = _place()
        for w in range(n):
            for j, (px, py) in enumerate(_other_chips(x, y)):
                slot = 2 * px + py
                copy = _pair_copy(bufs[w].at[slot, c], bufs[w].at[slot, 1 - c], send, recv, w, j, (x, y, 1 - c))
                copy.wait_send()
                copy.wait_recv()

    return pl.pallas_call(
        body, name=name, out_shape=[pltpu.HBM(a.shape, a.dtype) for a in lands],
        in_specs=[_HBM] * n + [_SEM, _SEM, _ANY], out_specs=[_HBM] * n,
        input_output_aliases={i: i for i in range(n)},
        compiler_params=pltpu.CompilerParams(has_side_effects=_EFFECT),
    )(*lands, *sems, after)


def _sum_partials(land, *, name, tr=256):
    _, R, C = land.shape
    tr = min(tr, R)

    def body(l_ref, o_ref):
        acc = l_ref[0].astype(F32)
        for k in range(1, N_CHIPS):
            acc = acc + l_ref[k].astype(F32)
        o_ref[...] = acc

    return pl.pallas_call(
        body, name=name, out_shape=jax.ShapeDtypeStruct((R, C), F32), grid=(R // tr,),
        in_specs=[pl.BlockSpec((N_CHIPS, tr, C), lambda i: (0, i, 0))], out_specs=_rows(tr, C),
        compiler_params=_cparams("parallel"))(land)


def _swap_with_sibling(sums, *, name):
    n = len(sums)

    def body(*refs):
        ins, outs = refs[:n], refs[n:2 * n]
        send_sems, recv_sems = refs[2 * n:]
        x, y, c = _place()
        copies = [pltpu.make_async_remote_copy(
            src_ref=ins[w], dst_ref=outs[w], send_sem=send_sems.at[w], recv_sem=recv_sems.at[w],
            device_id=(x, y, 1 - c), device_id_type=MESH) for w in range(n)]
        for cp in copies:
            cp.start()
        for cp in copies:
            cp.wait_recv()
            cp.wait_send()

    return pl.pallas_call(
        body, name=name,
        out_shape=[jax.ShapeDtypeStruct(s.shape, s.dtype) for s in sums],
        in_specs=[_ANY] * n, out_specs=[_ANY] * n,
        scratch_shapes=[pltpu.SemaphoreType.DMA((n,)), pltpu.SemaphoreType.DMA((n,))],
    )(*sums)


def _swap_start(sums, *, name):
    n = len(sums)
    bufs = list(sums) + [lax.empty(s.shape, s.dtype) for s in sums]

    def body(*refs):
        src, lnd, send, recv, token = refs[:n], refs[n:2 * n], refs[2 * n], refs[2 * n + 1], refs[4 * n + 2]
        x, y, c = _place()
        for w in range(n):
            pltpu.make_async_remote_copy(
                src_ref=src[w], dst_ref=lnd[w], send_sem=send.at[w], recv_sem=recv.at[w],
                device_id=(x, y, 1 - c), device_id_type=MESH).start()
        token[...] = jnp.zeros_like(token)

    res = pl.pallas_call(
        body, name=name,
        out_shape=[pltpu.SemaphoreType.DMA((n,))] * 2 + [pltpu.HBM(a.shape, a.dtype) for a in bufs] + [_TOKEN],
        in_specs=[_HBM] * (2 * n),
        out_specs=[_SEM, _SEM] + [_HBM] * (2 * n) + [pl.BlockSpec(memory_space=pltpu.VMEM)],
        input_output_aliases={i: 2 + i for i in range(2 * n)},
        compiler_params=pltpu.CompilerParams(has_side_effects=_EFFECT),
    )(*[_hbm(a) for a in bufs])
    return (res[0], res[1]), res[2:2 + n], res[2 + n:2 + 2 * n], res[-1]


def _swap_wait(sums, lands, sems, after, *, name):
    n = len(sums)

    def body(*refs):
        src, lnd, send, recv = refs[:n], refs[n:2 * n], refs[2 * n], refs[2 * n + 1]
        x, y, c = _place()
        for w in range(n):
            copy = pltpu.make_async_remote_copy(
                src_ref=src[w], dst_ref=lnd[w], send_sem=send.at[w], recv_sem=recv.at[w],
                device_id=(x, y, 1 - c), device_id_type=MESH)
            copy.wait_send()
            copy.wait_recv()

    bufs = list(sums) + list(lands)
    res = pl.pallas_call(
        body, name=name, out_shape=[pltpu.HBM(a.shape, a.dtype) for a in bufs],
        in_specs=[_HBM] * (2 * n) + [_SEM, _SEM, _ANY], out_specs=[_HBM] * (2 * n),
        input_output_aliases={i: i for i in range(2 * n)},
        compiler_params=pltpu.CompilerParams(has_side_effects=_EFFECT),
    )(*bufs, *sems, after)
    return res[:n], res[n:]


def _adamw_math(w, g, m, v):
    m = ADAM_B1 * m + (1.0 - ADAM_B1) * g
    v = ADAM_B2 * v + (1.0 - ADAM_B2) * (g * g)
    m_hat = m / (1.0 - ADAM_B1 ** ADAM_STEP)
    v_hat = v / (1.0 - ADAM_B2 ** ADAM_STEP)
    delta = -ADAM_LR * (m_hat / (jnp.sqrt(v_hat) + ADAM_EPS) + ADAM_WD * w)
    return delta, m, v


def _adamw_pair(mine, theirs, w, m, v, *, name, tr=128):
    R, C = w.shape
    tr = min(tr, R)

    def body(a_ref, b_ref, w_ref, m_ref, v_ref, g_ref, d_ref, nm_ref, nv_ref):
        g = a_ref[...] + b_ref[...]
        g_ref[...] = g
        d_ref[...], nm_ref[...], nv_ref[...] = _adamw_math(w_ref[...], g, m_ref[...], v_ref[...])

    shape = jax.ShapeDtypeStruct((R, C), F32)
    return pl.pallas_call(
        body, name=name, out_shape=(shape,) * 4, grid=(R // tr,),
        in_specs=[_rows(tr, C)] * 5, out_specs=(_rows(tr, C),) * 4,
        compiler_params=_cparams("parallel"))(mine, theirs, w, m, v)


def _all_reduce_small(packed):
    R = packed.shape[0]
    half = R // 2

    def body(x_ref, g_ref, sib_ref, pair_ref, land_ref, send_sems, recv_sems):
        x, y, c = _place()
        me = 2 * x + y
        sibling = (x, y, 1 - c)

        swap = pltpu.make_async_remote_copy(
            src_ref=x_ref, dst_ref=sib_ref, send_sem=send_sems.at[0], recv_sem=recv_sems.at[0],
            device_id=sibling, device_id_type=MESH)
        swap.start()
        swap.wait()
        mine, theirs = x_ref[...], sib_ref[...]
        south = c == 0
        pair_ref[...] = jnp.where(south, mine, theirs) + jnp.where(south, theirs, mine)

        land_ref[me] = pair_ref[c]
        for j, (px, py) in enumerate(_other_chips(x, y)):
            pltpu.make_async_remote_copy(
                src_ref=pair_ref.at[c], dst_ref=land_ref.at[me], send_sem=send_sems.at[1 + j],
                recv_sem=recv_sems.at[1 + j], device_id=(px, py, c), device_id_type=MESH).start()
        for j, (px, py) in enumerate(_other_chips(x, y)):
            arrival = pltpu.make_async_remote_copy(
                src_ref=pair_ref.at[c], dst_ref=land_ref.at[2 * px + py], send_sem=send_sems.at[1 + j],
                recv_sem=recv_sems.at[1 + j], device_id=(px, py, c), device_id_type=MESH)
            arrival.wait_recv()
            arrival.wait_send()
        total = land_ref[0]
        for k in range(1, N_CHIPS):
            total = total + land_ref[k]
        g_ref[c] = total

        give = pltpu.make_async_remote_copy(
            src_ref=g_ref.at[c], dst_ref=g_ref.at[c], send_sem=send_sems.at[4], recv_sem=recv_sems.at[4],
            device_id=sibling, device_id_type=MESH)
        give.start()
        take = pltpu.make_async_remote_copy(
            src_ref=g_ref.at[c], dst_ref=g_ref.at[1 - c], send_sem=send_sems.at[4], recv_sem=recv_sems.at[4],
            device_id=sibling, device_id_type=MESH)
        take.wait_recv()
        give.wait_send()

    vm = pl.BlockSpec(memory_space=pltpu.VMEM)
    return pl.pallas_call(
        body, name="all_reduce_small", out_shape=jax.ShapeDtypeStruct((2, half, LANES), F32),
        in_specs=[vm], out_specs=vm,
        scratch_shapes=[pltpu.VMEM((2, half, LANES), F32), pltpu.VMEM((2, half, LANES), F32),
                        pltpu.VMEM((N_CHIPS, half, LANES), F32),
                        pltpu.SemaphoreType.DMA((5,)), pltpu.SemaphoreType.DMA((5,))],
        compiler_params=pltpu.CompilerParams(vmem_limit_bytes=VMEM_LIMIT_BYTES),
    )(packed.reshape(2, half, LANES)).reshape(R, LANES)


def _adamw_small(g, w, m, v):
    R = g.shape[0]
    tr = PACK_ROWS

    def body(g_ref, w_ref, m_ref, v_ref, d_ref, nm_ref, nv_ref):
        d_ref[...], nm_ref[...], nv_ref[...] = _adamw_math(w_ref[...], g_ref[...], m_ref[...], v_ref[...])

    shape = jax.ShapeDtypeStruct((R, LANES), F32)
    return pl.pallas_call(
        body, name="adamw_small", out_shape=(shape,) * 3, grid=(R // tr,),
        in_specs=[_rows(tr, LANES)] * 4, out_specs=(_rows(tr, LANES),) * 3,
        compiler_params=_cparams("parallel"))(g, w, m, v)


def _pack(arrays):
    parts, layout = [], []
    for a in arrays:
        n = a.size
        rows = -(-n // (8 * LANES)) * 8
        flat = jnp.pad(a.reshape(-1).astype(F32), (0, rows * LANES - n))
        parts.append(flat.reshape(rows, LANES))
        layout.append((rows, n, a.shape))
    total = sum(r for r, _, _ in layout)
    parts.append(jnp.zeros((-total % PACK_ROWS, LANES), F32))
    return jnp.concatenate(parts, axis=0), layout


def _unpack(buf, layout):
    out, r0 = [], 0
    for rows, n, shape in layout:
        out.append(buf[r0:r0 + rows].reshape(-1)[:n].reshape(shape))
        r0 += rows
    return out


SMALL = ("mix_norm_pre", "lam_re", "lam_im", "log_dt", "ssm_b_re", "ssm_b_im", "ssm_c_re", "ssm_c_im",
         "ssm_d", "b_glu", "attn_out_norm", "ssm_out_norm", "mix_norm_post", "mlp_norm_pre",
         "mlp_norm_post", "ple_norm_pre", "ple_norm_post")
BIG = ("w_in", "w_glu", "w_out", "w_up", "w_down", "w_ple_gate", "w_ple_proj")
WEIGHTS = ("mix_norm_pre", "w_in", "lam_re", "lam_im", "log_dt", "ssm_b_re", "ssm_b_im", "ssm_c_re",
           "ssm_c_im", "ssm_d", "w_glu", "b_glu", "attn_out_norm", "ssm_out_norm", "w_out",
           "mix_norm_post", "mlp_norm_pre", "w_up", "w_down", "mlp_norm_post", "ple_norm_pre",
           "w_ple_gate", "w_ple_proj", "ple_norm_post")


def kernel(x, p, mix_norm_pre, w_in, lam_re, lam_im, log_dt, ssm_b_re, ssm_b_im, ssm_c_re, ssm_c_im, ssm_d, w_glu, b_glu, attn_out_norm, ssm_out_norm, w_out, mix_norm_post, mlp_norm_pre, w_up, w_down, mlp_norm_post, ple_norm_pre, w_ple_gate, w_ple_proj, ple_norm_post, loss_target, m_mix_norm_pre, m_w_in, m_lam_re, m_lam_im, m_log_dt, m_ssm_b_re, m_ssm_b_im, m_ssm_c_re, m_ssm_c_im, m_ssm_d, m_w_glu, m_b_glu, m_attn_out_norm, m_ssm_out_norm, m_w_out, m_mix_norm_post, m_mlp_norm_pre, m_w_up, m_w_down, m_mlp_norm_post, m_ple_norm_pre, m_w_ple_gate, m_w_ple_proj, m_ple_norm_post, v_mix_norm_pre, v_w_in, v_lam_re, v_lam_im, v_log_dt, v_ssm_b_re, v_ssm_b_im, v_ssm_c_re, v_ssm_c_im, v_ssm_d, v_w_glu, v_b_glu, v_attn_out_norm, v_ssm_out_norm, v_w_out, v_mix_norm_post, v_mlp_norm_pre, v_w_up, v_w_down, v_mlp_norm_post, v_ple_norm_pre, v_w_ple_gate, v_w_ple_proj, v_ple_norm_post):
    args = dict(locals())
    W = {n: args[n][0] for n in WEIGHTS}
    Mo = {n: args["m_" + n][0] for n in WEIGHTS}
    Vo = {n: args["v_" + n][0] for n in WEIGHTS}
    xs, ps, tgt = x[0], p[0, 0], loss_target[0]
    S, D = xs.shape
    SW = W["ssm_d"].shape[0]
    AW = W["attn_out_norm"].shape[0]
    heads = AW // HEAD_DIM
    G = SW // SSM_C
    nbk = SW // LANES
    assert W["w_in"].shape[1] * N_CHIPS == 3 * AW + SW and AW == SW

    row = lambda a: a.reshape(1, -1)

    ag_groups = (("w_in",), ("w_glu", "w_out"), ("w_up",), ("w_down", "w_ple_gate", "w_ple_proj"))
    ag_names = [n for g in ag_groups for n in g]
    def in_halves(a):
        return a.reshape(N_CHIPS, 2, a.shape[1] // 2, a.shape[2])

    def placed(n, after=None):
        return in_halves(_place_own(W[n], gather=True, name="ag_place_" + n, after=after))

    first_sems, first_land, _, first_token = _exchange_start([placed("w_in")], [], [[0]], name="ag_start_first")
    rest_sems, rest_land, _, ag_token = _exchange_start(
        [placed(n, first_token) for n in ag_names[1:]], [],
        [[ag_names.index(n) - 1 for n in g] for g in ag_groups[1:]], name="ag_start")
    ag_sems, ag_land = first_sems + rest_sems, list(first_land) + list(rest_land)

    def fetched(gi, after):
        return _exchange_wait([ag_land[ag_names.index(n)] for n in ag_groups[gi]], [], ag_sems[gi], after,
                              name=f"ag_wait_{gi}")

    def whole(gis, bufs):
        names = [n for gi in gis for n in ag_groups[gi]]
        return {n: a.reshape(N_CHIPS, -1, a.shape[-1]) for n, a in zip(names, bufs)}

    lr_e = W["lam_re"].reshape(nbk, 1, STATE_LANES)
    li_e = W["lam_im"].reshape(nbk, 1, STATE_LANES)
    ldt_e = jnp.repeat(W["log_dt"], SSM_P).reshape(nbk, 1, STATE_LANES)
    bre_e, bim_e = _expand_b(W["ssm_b_re"]), _expand_b(W["ssm_b_im"])
    cre_e, cim_e = _expand_c(W["ssm_c_re"]), _expand_c(W["ssm_c_im"])
    d_row = row(W["ssm_d"])

    hn1 = _norm_cast(xs, row(W["mix_norm_pre"]) + ag_token[0, 0], name="norm_in")
    w_in_f = whole([0], _pair_fill(fetched(0, hn1), name="ag_pair_0"))["w_in"]
    qkv_b = _proj_qkv(hn1, w_in_f)
    outs, lses = zip(*[_attn_fwd(qb, d, heads) for d, qb in zip(DILATIONS, qkv_b)])
    pair_a_sems, pair_a, pair_a_token = _pair_start(fetched(1, outs[-1]), name="ag_pair_start_a")
    u = _matmul(hn1, w_in_f, name="proj_u", b_shards=N_CHIPS, b_cols=(3 * AW, SW), after=pair_a_token)
    y1, y2b, st_r, st_i, ends_r, ends_i = _ssm_fwd(u, lr_e, li_e, ldt_e, bre_e, bim_e, cre_e, cim_e, d_row)
    pair_b_sems, pair_b, pair_b_token = _pair_start(fetched(2, y2b), name="ag_pair_start_b")
    full = whole([1], _pair_wait(pair_a, pair_a_sems, y2b, name="ag_pair_wait_a"))
    w_glu_f = full["w_glu"].reshape(SW, SW)
    w_out_f = full["w_out"].reshape(AW + SW, D)
    z = _matmul(y2b, w_glu_f, name="glu_z", after=pair_b_token)
    attn, lse_b, mixed = _mix_fwd(outs, lses, y1, z, row(W["b_glu"]), row(W["attn_out_norm"]), row(W["ssm_out_norm"]))
    mo = _matmul(mixed, w_out_f, name="mix_out")
    h1, hn2 = _res_norm(xs, mo, row(W["mix_norm_post"]), row(W["mlp_norm_pre"]), name="res_mix")
    w_up_f = whole([2], _pair_wait(pair_b, pair_b_sems, hn2, name="ag_pair_wait_b"))["w_up"]
    up, act = _matmul(hn2, w_up_f, name="mlp_up", b_shards=N_CHIPS, relu2=True, out_dtype=BF16)
    full = whole([3], _pair_fill(fetched(3, act), name="ag_pair_3"))
    w_down_f = full["w_down"].reshape(-1, D)
    w_pg_f = full["w_ple_gate"].reshape(D, D)
    w_pp_f = full["w_ple_proj"]
    ff = _matmul(act, w_down_f, name="mlp_down")
    h2, hn3 = _res_norm(h1, ff, row(W["mlp_norm_post"]), row(W["ple_norm_pre"]), name="res_mlp")
    gl = _matmul(hn3, w_pg_f, name="ple_gate")
    e = _matmul(ps.astype(BF16), w_pp_f, name="ple_proj", b_shards=N_CHIPS)

    dh3, dgl, de, loss_part, dg_ple_post = _final(h2, gl, e, row(W["ple_norm_post"]), tgt)
    gW = {}
    out_g, out_d, out_m, out_v = {}, {}, {}, {}

    def scatter_start(names, tag):
        parts = [gW[n] if gW[n].ndim == 3 else gW[n].reshape((N_CHIPS, -1, gW[n].shape[1])) for n in names]
        sems, land, src, token = _exchange_start(
            [_place_own(part, gather=False, name="rs_place_" + n) for n, part in zip(names, parts)], parts,
            [list(range(len(names)))], name=f"rs_start_{tag}")
        return (names, sems[0], land, src), token

    def scatter_sums(batches, after):
        names, sums = [], []
        for tag, (batch_names, sems, land, src) in batches:
            landed = _exchange_wait(land, src, sems, after, name=f"rs_wait_{tag}")
            names += batch_names
            sums += [_sum_partials(l, name="sum_" + n) for n, l in zip(batch_names, landed)]
        return names, sums

    def apply(names, sums, theirs):
        for n, a, b in zip(names, sums, theirs):
            out_g[n], out_d[n], out_m[n], out_v[n] = _adamw_pair(a, b, W[n], Mo[n], Vo[n], name="adamw_" + n)

    def swap_begin(batches, after, tag):
        names, sums = scatter_sums(batches, after)
        sems, sums, lands, token = _swap_start(sums, name=f"swap_start_{tag}")
        return (names, sems, sums, lands), token

    def swap_end(swap, after, tag):
        names, sems, sums, lands = swap
        sums, theirs = _swap_wait(sums, lands, sems, after, name=f"swap_wait_{tag}")
        apply(names, sums, theirs)

    def scatter_finish(batch, after, tag):
        names, sums = scatter_sums([(tag, batch)], after)
        apply(names, sums, _swap_with_sibling(sums, name=f"swap_{tag}"))

    gW["w_ple_proj"] = _matmul(ps.astype(BF16), de, name="d_w_ple_proj", ta=True, out_dtype=BF16, out_shards=N_CHIPS)
    gW["w_ple_gate"] = _matmul(hn3, dgl, name="d_w_ple_gate", ta=True, out_dtype=BF16)
    dhn3 = _matmul(dgl, w_pg_f, name="d_hn3", tb=True)
    dh2, dff, dg_ple_pre, dg_mlp_post = _bwd_res_norm(
        dh3, dhn3, h2, row(W["ple_norm_pre"]), ff, row(W["mlp_norm_post"]), name="bwd_res_mlp")
    gW["w_down"] = _matmul(act, dff, name="d_w_down", ta=True, out_dtype=BF16)
    batch1, token1 = scatter_start(("w_ple_proj", "w_ple_gate", "w_down"), 1)
    dup = _matmul(dff, w_down_f, name="d_up", tb=True, after=token1, relu2_of=up, out_dtype=BF16)
    gW["w_up"] = _matmul(hn2, dup, name="d_w_up", ta=True, out_dtype=BF16, out_shards=N_CHIPS)
    batch2, token2 = scatter_start(("w_up",), 2)
    dhn2 = _matmul(dup, w_up_f, name="d_hn2", tb=True, b_shards=N_CHIPS, after=token2)
    dh1, dmo, dg_mlp_pre, dg_mix_post = _bwd_res_norm(
        dh2, dhn2, h1, row(W["mlp_norm_pre"]), mo, row(W["mix_norm_post"]), name="bwd_res_mix")
    gW["w_out"] = _matmul(mixed, dmo, name="d_w_out", ta=True, out_dtype=BF16)
    dmixed = _matmul(dmo, w_out_f, name="d_mixed", tb=True)
    dattn_b, dd_b, dz, dy2a, dg_attn, dg_ssm, db_glu = _mix_bwd(
        dmixed, attn, y1, z, row(W["b_glu"]), row(W["attn_out_norm"]), row(W["ssm_out_norm"]))
    gW["w_glu"] = _matmul(y2b, dz, name="d_w_glu", ta=True, out_dtype=BF16)
    batch3, token3 = scatter_start(("w_out", "w_glu"), 3)
    dy2b = _matmul(dz, w_glu_f, name="d_y2", tb=True, after=token3)
    du, dar8, dai8, dcr_e, dci_e, dbr_e, dbi_e, dd8 = _ssm_bwd(
        u, y1, dy2a, dy2b, st_r, st_i, ends_r, ends_i, lr_e, li_e, ldt_e, bre_e, bim_e, cre_e, cim_e, d_row)
    swap_a, token_a = swap_begin([(1, batch1)], du, "a")
    dlr_e, dli_e, dldt_e, dbre_e, dbim_e = _ssm_param_bwd(dar8, dai8, dbr_e, dbi_e, lr_e, li_e, ldt_e, bre_e, bim_e)

    dqs, dks, dvs = zip(*[_attn_bwd(qb, da, l, dd_, d, heads, token_a)
                          for d, qb, da, l, dd_ in zip(DILATIONS, qkv_b, dattn_b, lse_b, dd_b)])
    dproj = _dproj_join(dqs, dks, dvs, du)
    swap_end(swap_a, dproj, "a")
    swap_b, token_b = swap_begin([(2, batch2), (3, batch3)], dproj, "b")
    gW["w_in"] = _matmul(hn1, dproj, name="d_w_in", ta=True, out_dtype=BF16, out_shards=N_CHIPS, after=token_b)
    batch4, token4 = scatter_start(("w_in",), 4)
    dhn1 = _matmul(dproj, w_in_f, name="d_hn1", tb=True, b_shards=N_CHIPS, after=token4)
    grad_x, dg_mix_pre = _bwd_first(dh1, dhn1, xs, row(W["mix_norm_pre"]))
    swap_end(swap_b, grad_x, "b")
    scatter_finish(batch4, grad_x, 4)

    small_g = {
        "mix_norm_pre": dg_mix_pre, "lam_re": dlr_e.reshape(G, SSM_P), "lam_im": dli_e.reshape(G, SSM_P),
        "log_dt": dldt_e.reshape(G, SSM_P)[:, 0], "ssm_b_re": _collapse_b(dbre_e), "ssm_b_im": _collapse_b(dbim_e),
        "ssm_c_re": _collapse_c(dcr_e), "ssm_c_im": _collapse_c(dci_e), "ssm_d": dd8.sum(axis=1).reshape(-1),
        "b_glu": db_glu, "attn_out_norm": dg_attn, "ssm_out_norm": dg_ssm, "mix_norm_post": dg_mix_post,
        "mlp_norm_pre": dg_mlp_pre, "mlp_norm_post": dg_mlp_post, "ple_norm_pre": dg_ple_pre,
        "ple_norm_post": dg_ple_post,
    }
    g_pack, layout = _pack([small_g[n].reshape(W[n].shape) for n in SMALL])
    w_pack, _ = _pack([W[n] for n in SMALL])
    m_pack, _ = _pack([Mo[n] for n in SMALL])
    v_pack, _ = _pack([Vo[n] for n in SMALL])
    g_sum = _all_reduce_small(g_pack)
    packed = (g_sum,) + tuple(_adamw_small(g_sum, w_pack, m_pack, v_pack))
    for dst, buf in zip((out_g, out_d, out_m, out_v), packed):
        dst.update(zip(SMALL, _unpack(buf, layout)))

    loss = lax.psum(loss_part[0, 0], ("x", "y", "c"))
    lead = lambda a: a[None]
    return (loss, grad_x[None],
            *[lead(out_g[n]) for n in WEIGHTS], *[lead(out_d[n]) for n in WEIGHTS],
            *[lead(out_m[n]) for n in WEIGHTS], *[lead(out_v[n]) for n in WEIGHTS])
```

```python
import functools
import math

import jax
import jax.numpy as jnp
from jax import lax
from jax.experimental import pallas as pl
from jax.experimental.pallas import tpu as pltpu

F32 = jnp.float32
BF16 = jnp.bfloat16
MESH = pl.DeviceIdType.MESH

RMS_EPS = 1e-6
NEG_INF = -1e30
HEAD_DIM = 128
BLK = 128
DILATIONS = (1, 4, 16)
ATTN_LOOKAHEAD = 3
SSM_C = 16
SSM_P = 64
LANES = 128
GROUPS_PER_BLOCK = LANES // SSM_C
STATE_LANES = GROUPS_PER_BLOCK * SSM_P
SSM_CHUNK = 1024
TILE = 8
ADAM_LR, ADAM_B1, ADAM_B2, ADAM_EPS, ADAM_WD, ADAM_STEP = 1e-3, 0.9, 0.999, 1e-8, 0.01, 10
VMEM_LIMIT_BYTES = 56 * 1024 * 1024
MATMUL_VMEM_BYTES = 44 * 1024 * 1024
N_CHIPS = 4
N_DEV = 8
PACK_ROWS = 256


def _cparams(*sem):
    return pltpu.CompilerParams(dimension_semantics=sem or None, vmem_limit_bytes=VMEM_LIMIT_BYTES)


def _rows(tr, w):
    return pl.BlockSpec((tr, w), lambda i: (i, 0))


def _vec(w):
    return pl.BlockSpec((1, w), lambda i: (0, 0))


def _sigmoid(x):
    return 1.0 / (1.0 + jnp.exp(-x))


def _gelu(x):
    c = math.sqrt(2.0 / math.pi)
    return 0.5 * x * (1.0 + jnp.tanh(c * (x + 0.044715 * x * x * x)))


def _gelu_grad(x):
    c = math.sqrt(2.0 / math.pi)
    th = jnp.tanh(c * (x + 0.044715 * x * x * x))
    return 0.5 * (1.0 + th) + 0.5 * x * (1.0 - th * th) * c * (1.0 + 3.0 * 0.044715 * x * x)


def _rms(x, g):
    r = lax.rsqrt(jnp.mean(x * x, axis=-1, keepdims=True) + RMS_EPS)
    return x * r * g


def _rms_bwd(dy, x, g):
    r = lax.rsqrt(jnp.mean(x * x, axis=-1, keepdims=True) + RMS_EPS)
    n = x * r
    dn = dy * g
    dx = r * (dn - n * jnp.mean(dn * n, axis=-1, keepdims=True))
    return dx, dy * n


def _colsum(a):
    return jnp.sum(a, axis=0, keepdims=True)


def _first(i):
    return i == 0


def _matmul(a, b, *, name, ta=False, tb=False, out_dtype=F32, b_shards=1, out_shards=1, b_cols=None,
            after=None, relu2=False, relu2_of=None, tm=1024, tn=2048, tk=2048):
    if ta:
        K, M = a.shape
    else:
        M, K = a.shape
    if b_shards > 1:
        rows, cols = b.shape[1], b.shape[2] * b_shards
    else:
        rows, cols = b.shape
    N, Kb = (rows, cols) if tb else (cols, rows)
    assert K == Kb, (a.shape, b.shape, ta, tb)
    col0 = 0
    if b_cols is not None:
        assert not tb
        col0, N = b_cols
    tm, tn, tk = min(tm, M), min(tn, N), min(tk, K)
    if b_shards > 1:
        shard_cols = cols // b_shards
        if tb:
            tk = min(tk, shard_cols)
        else:
            tn = min(tn, shard_cols)
    if out_shards > 1:
        tn = min(tn, N // out_shards)

    def vmem_bytes(tn_):
        out_bytes = jnp.dtype(out_dtype).itemsize + (2 if relu2 else 0)
        return (4 * (tm * tk + tk * tn_) + 2 * tm * tn_ * out_bytes
                + (2 * relu2_of.dtype.itemsize * tm * tn_ if relu2_of is not None else 0)
                + (4 * tm * tn_ if K > tk else 0))

    while vmem_bytes(tn) > MATMUL_VMEM_BYTES and tn > LANES and col0 % (tn // 2) == 0:
        tn //= 2
    assert M % tm == 0 and N % tn == 0 and K % tk == 0 and col0 % tn == 0
    nk = K // tk
    j0 = col0 // tn

    a_spec = (pl.BlockSpec((tk, tm), lambda i, j, k: (k, i)) if ta
              else pl.BlockSpec((tm, tk), lambda i, j, k: (i, k)))
    if b_shards > 1:
        if tb:
            per = shard_cols // tk
            b_spec = pl.BlockSpec((None, tn, tk), lambda i, j, k: (k // per, j, k % per))
        else:
            per = shard_cols // tn
            b_spec = pl.BlockSpec((None, tk, tn), lambda i, j, k: ((j + j0) // per, k, (j + j0) % per))
    else:
        b_spec = (pl.BlockSpec((tn, tk), lambda i, j, k: (j, k)) if tb
                  else pl.BlockSpec((tk, tn), lambda i, j, k: (k, j + j0)))
    if out_shards > 1:
        per_o = (N // out_shards) // tn
        out_shape = jax.ShapeDtypeStruct((out_shards, M, N // out_shards), out_dtype)
        out_spec = pl.BlockSpec((None, tm, tn), lambda i, j, k: (j // per_o, i, j % per_o))
    else:
        out_shape = jax.ShapeDtypeStruct((M, N), out_dtype)
        out_spec = pl.BlockSpec((tm, tn), lambda i, j, k: (i, j))
    dims = (((0 if ta else 1,), (1 if tb else 0,)), ((), ()))

    extra, extra_specs = [], []
    if relu2_of is not None:
        assert out_shards == 1 and relu2_of.shape == (M, N)
        extra.append(relu2_of)
        extra_specs.append(pl.BlockSpec((tm, tn), lambda i, j, k: (i, j)))
    if after is not None:
        extra.append(after)
        extra_specs.append(pl.BlockSpec(after.shape, lambda i, j, k: (0, 0)))
    n_in = 2 + len(extra)
    if relu2:
        assert out_shards == 1
        out_shape = (out_shape, jax.ShapeDtypeStruct((M, N), BF16))
        out_spec = (out_spec, out_spec)

    def finish(acc, refs):
        o_ref = refs[n_in]
        if relu2_of is not None:
            acc = acc * (2.0 * jnp.maximum(refs[2][...].astype(F32), 0.0))
        o_ref[...] = acc.astype(o_ref.dtype)
        if relu2:
            r = jnp.maximum(acc, 0.0)
            refs[n_in + 1][...] = (r * r).astype(BF16)

    def body(*refs):
        prod = lax.dot_general(refs[0][...], refs[1][...], dims, preferred_element_type=F32)
        if nk == 1:
            finish(prod, refs)
            return
        acc_ref = refs[-1]
        k = pl.program_id(2)

        @pl.when(k == 0)
        def _():
            acc_ref[...] = prod

        @pl.when(k > 0)
        def _():
            acc_ref[...] += prod

        @pl.when(k == nk - 1)
        def _():
            finish(acc_ref[...], refs)

    return pl.pallas_call(
        body, name=name, out_shape=out_shape, grid=(M // tm, N // tn, nk),
        in_specs=[a_spec, b_spec] + extra_specs, out_specs=out_spec,
        scratch_shapes=[pltpu.VMEM((tm, tn), F32)] if nk > 1 else [],
        compiler_params=_cparams("parallel", "parallel", "arbitrary"),
    )(a, b, *extra)


def _norm_cast(x, g, *, name, tr=256):
    S, D = x.shape
    tr = min(tr, S)

    def body(x_ref, g_ref, o_ref):
        o_ref[...] = _rms(x_ref[...], g_ref[...]).astype(BF16)

    return pl.pallas_call(
        body, name=name, out_shape=jax.ShapeDtypeStruct((S, D), BF16), grid=(S // tr,),
        in_specs=[_rows(tr, D), _vec(D)], out_specs=_rows(tr, D),
        compiler_params=_cparams("parallel"))(x, g)


def _res_norm(res, y, g_post, g_next, *, name, tr=256):
    S, D = res.shape
    tr = min(tr, S)

    def body(res_ref, y_ref, gp_ref, gn_ref, h_ref, hn_ref):
        h = res_ref[...] + _rms(y_ref[...], gp_ref[...])
        h_ref[...] = h
        hn_ref[...] = _rms(h, gn_ref[...]).astype(BF16)

    return pl.pallas_call(
        body, name=name,
        out_shape=(jax.ShapeDtypeStruct((S, D), F32), jax.ShapeDtypeStruct((S, D), BF16)),
        grid=(S // tr,), in_specs=[_rows(tr, D), _rows(tr, D), _vec(D), _vec(D)],
        out_specs=(_rows(tr, D), _rows(tr, D)), compiler_params=_cparams("parallel"))(res, y, g_post, g_next)


def _residue_spec(tr, d, w):
    return pl.BlockSpec((tr // d, d * w), lambda i: (i, 0))


def _residue_shape(S, d, w, dtype):
    return jax.ShapeDtypeStruct((S // d, d * w), dtype)


def _residue_scratch(rows, w):
    return pltpu.VMEM((w // LANES, rows, LANES), F32)


def _fill_strips(scr, val):
    for s in range(scr.shape[0]):
        scr[s] = val[:, s * LANES:(s + 1) * LANES]


def _strips_to_residues(scr, o_ref, d):
    strips, rows, _ = scr.shape
    for r in range(d):
        for s in range(strips):
            col = (r * strips + s) * LANES
            o_ref[:, col:col + LANES] = scr[s, pl.ds(r, rows // d, stride=d), :].astype(o_ref.dtype)


def _to_residues(scr, val, o_ref, d):
    if d == 1:
        o_ref[...] = val.astype(o_ref.dtype)
        return
    _fill_strips(scr, val)
    _strips_to_residues(scr, o_ref, d)


def _from_residues(scr, in_ref, d):
    if d == 1:
        return in_ref[...].astype(F32)
    strips, rows, _ = scr.shape
    for r in range(d):
        for s in range(strips):
            col = (r * strips + s) * LANES
            scr[s, pl.ds(r, rows // d, stride=d), :] = in_ref[:, col:col + LANES].astype(F32)
    return jnp.concatenate([scr[s] for s in range(strips)], axis=1)


def _spread_heads(packed, heads, width=HEAD_DIM):
    per = LANES // heads
    return jnp.concatenate([jnp.broadcast_to(packed[:, h * per:h * per + 1], (packed.shape[0], width))
                            for h in range(heads)], axis=1)


def _mix_fwd(os, ls, y1, z, b_glu, g_attn, g_ssm, *, tr=256):
    S, SW = y1.shape
    AW = os[0].shape[1] // DILATIONS[0]
    heads = AW // HEAD_DIM
    tr = min(tr, S)
    nd = len(DILATIONS)

    def body(*refs):
        o_refs, l_refs = refs[:nd], refs[nd:2 * nd]
        y_ref, z_ref, b_ref, ga_ref, gs_ref, attn_ref = refs[2 * nd:2 * nd + 6]
        lse_refs = refs[2 * nd + 6:3 * nd + 6]
        mixed_ref, scr, scr_p = refs[3 * nd + 6:]
        ls_ = [_from_residues(scr_p, l_refs[n], d) for n, d in enumerate(DILATIONS)]
        m = functools.reduce(jnp.maximum, ls_)
        es = [jnp.exp(l - m) for l in ls_]
        tot = functools.reduce(jnp.add, es)
        attn = functools.reduce(jnp.add, [_spread_heads(e / tot, heads) * _from_residues(scr, o_refs[n], d)
                                          for n, (e, d) in enumerate(zip(es, DILATIONS))])
        attn_ref[...] = attn
        lse = m + jnp.log(tot)
        for n, d in enumerate(DILATIONS):
            _to_residues(scr_p, lse, lse_refs[n], d)
        ssm = _gelu(y_ref[...]) * _sigmoid(z_ref[...] + b_ref[...])
        mixed_ref[:, :AW] = _rms(attn, ga_ref[...]).astype(BF16)
        mixed_ref[:, AW:] = _rms(ssm, gs_ref[...]).astype(BF16)

    res_o = [_residue_spec(tr, d, AW) for d in DILATIONS]
    res_l = [_residue_spec(tr, d, LANES) for d in DILATIONS]
    res = pl.pallas_call(
        body, name="mix_fwd",
        out_shape=([jax.ShapeDtypeStruct((S, AW), F32)] + [_residue_shape(S, d, LANES, F32) for d in DILATIONS]
                   + [jax.ShapeDtypeStruct((S, AW + SW), BF16)]),
        grid=(S // tr,),
        in_specs=res_o + res_l + [_rows(tr, SW), _rows(tr, SW), _vec(SW), _vec(AW), _vec(SW)],
        out_specs=[_rows(tr, AW)] + res_l + [_rows(tr, AW + SW)],
        scratch_shapes=[_residue_scratch(tr, AW), _residue_scratch(tr, LANES)],
        compiler_params=_cparams("parallel"))(*os, *ls, y1, z, b_glu, g_attn, g_ssm)
    return res[0], res[1:1 + nd], res[1 + nd]


def _final(h2, gl, e, g_post, target, *, tr=256):
    S, D = h2.shape
    tr = min(tr, S)

    def body(h_ref, gl_ref, e_ref, g_ref, t_ref, dh_ref, dgl_ref, de_ref, loss_ref, dg_ref):
        i = pl.program_id(0)
        gate = _sigmoid(gl_ref[...])
        e_ = e_ref[...]
        ge = gate * e_
        g = g_ref[...]
        diff = h_ref[...] + _rms(ge, g) - t_ref[...]
        dh = diff * (1.0 / D)
        dh_ref[...] = dh
        dge, dgrow = _rms_bwd(dh, ge, g)
        dgl_ref[...] = (dge * e_ * gate * (1.0 - gate)).astype(BF16)
        de_ref[...] = (dge * gate).astype(BF16)
        part = _colsum(0.5 * jnp.mean(diff * diff, axis=-1, keepdims=True))

        @pl.when(_first(i))
        def _():
            loss_ref[...] = jnp.zeros_like(loss_ref)
            dg_ref[...] = jnp.zeros_like(dg_ref)

        loss_ref[...] += part + jnp.zeros((1, LANES), F32)
        dg_ref[...] += _colsum(dgrow)

    return pl.pallas_call(
        body, name="final_fwd_bwd",
        out_shape=(jax.ShapeDtypeStruct((S, D), F32), jax.ShapeDtypeStruct((S, D), BF16),
                   jax.ShapeDtypeStruct((S, D), BF16), jax.ShapeDtypeStruct((1, LANES), F32),
                   jax.ShapeDtypeStruct((1, D), F32)),
        grid=(S // tr,),
        in_specs=[_rows(tr, D), _rows(tr, D), _rows(tr, D), _vec(D), _rows(tr, D)],
        out_specs=(_rows(tr, D), _rows(tr, D), _rows(tr, D), _vec(LANES), _vec(D)),
        compiler_params=_cparams("arbitrary"))(h2, gl, e, g_post, target)


def _bwd_res_norm(dh_out, dhn, h, g_next, y, g_post, *, name, tr=256):
    S, D = h.shape
    tr = min(tr, S)

    def body(dho_ref, dhn_ref, h_ref, gn_ref, y_ref, gp_ref, dh_ref, dy_ref, dgn_ref, dgp_ref):
        i = pl.program_id(0)
        dx, dgn_rows = _rms_bwd(dhn_ref[...], h_ref[...], gn_ref[...])
        dh = dho_ref[...] + dx
        dh_ref[...] = dh
        dy, dgp_rows = _rms_bwd(dh, y_ref[...], gp_ref[...])
        dy_ref[...] = dy.astype(BF16)

        @pl.when(_first(i))
        def _():
            dgn_ref[...] = jnp.zeros_like(dgn_ref)
            dgp_ref[...] = jnp.zeros_like(dgp_ref)

        dgn_ref[...] += _colsum(dgn_rows)
        dgp_ref[...] += _colsum(dgp_rows)

    return pl.pallas_call(
        body, name=name,
        out_shape=(jax.ShapeDtypeStruct((S, D), F32), jax.ShapeDtypeStruct((S, D), BF16),
                   jax.ShapeDtypeStruct((1, D), F32), jax.ShapeDtypeStruct((1, D), F32)),
        grid=(S // tr,),
        in_specs=[_rows(tr, D), _rows(tr, D), _rows(tr, D), _vec(D), _rows(tr, D), _vec(D)],
        out_specs=(_rows(tr, D), _rows(tr, D), _vec(D), _vec(D)),
        compiler_params=_cparams("arbitrary"))(dh_out, dhn, h, g_next, y, g_post)


def _bwd_first(dh1, dhn1, x, g1, *, tr=256):
    S, D = x.shape
    tr = min(tr, S)

    def body(dh_ref, dhn_ref, x_ref, g_ref, dx_ref, dg_ref):
        i = pl.program_id(0)
        dx, dg_rows = _rms_bwd(dhn_ref[...], x_ref[...], g_ref[...])
        dx_ref[...] = dh_ref[...] + dx

        @pl.when(_first(i))
        def _():
            dg_ref[...] = jnp.zeros_like(dg_ref)

        dg_ref[...] += _colsum(dg_rows)

    return pl.pallas_call(
        body, name="bwd_first",
        out_shape=(jax.ShapeDtypeStruct((S, D), F32), jax.ShapeDtypeStruct((1, D), F32)),
        grid=(S // tr,), in_specs=[_rows(tr, D), _rows(tr, D), _rows(tr, D), _vec(D)],
        out_specs=(_rows(tr, D), _vec(D)), compiler_params=_cparams("arbitrary"))(dh1, dhn1, x, g1)


def _mix_bwd(dmixed, attn, y1, z, b_glu, g_attn, g_ssm, *, tr=256):
    S, AW = attn.shape
    SW = y1.shape[1]
    tr = min(tr, S)
    heads = AW // HEAD_DIM
    nd = len(DILATIONS)

    def body(*refs):
        dm_ref, a_ref, y_ref, z_ref, b_ref, ga_ref, gs_ref = refs[:7]
        da_refs, dd_refs = refs[7:7 + nd], refs[7 + nd:7 + 2 * nd]
        dz_ref, dy2_ref, dga_ref, dgs_ref, db_ref, scr, scr_p, dd_scr = refs[7 + 2 * nd:]
        i = pl.program_id(0)
        attn_ = a_ref[...]
        dattn, dga_rows = _rms_bwd(dm_ref[:, :AW], attn_, ga_ref[...])
        prod = dattn * attn_
        per = LANES // heads
        for h in range(heads):
            total = jnp.sum(prod[:, h * HEAD_DIM:(h + 1) * HEAD_DIM], axis=-1, keepdims=True)
            dd_scr[:, h * per:(h + 1) * per] = jnp.broadcast_to(total, (tr, per))
        for n, d in enumerate(DILATIONS):
            _to_residues(scr, dattn, da_refs[n], d)
            _to_residues(scr_p, dd_scr[...], dd_refs[n], d)
        y2 = _gelu(y_ref[...])
        gate = _sigmoid(z_ref[...] + b_ref[...])
        dssm, dgs_rows = _rms_bwd(dm_ref[:, AW:], y2 * gate, gs_ref[...])
        dz = dssm * y2 * gate * (1.0 - gate)
        dz_ref[...] = dz.astype(BF16)
        dy2_ref[...] = dssm * gate

        @pl.when(_first(i))
        def _():
            dga_ref[...] = jnp.zeros_like(dga_ref)
            dgs_ref[...] = jnp.zeros_like(dgs_ref)
            db_ref[...] = jnp.zeros_like(db_ref)

        dga_ref[...] += _colsum(dga_rows)
        dgs_ref[...] += _colsum(dgs_rows)
        db_ref[...] += _colsum(dz)

    res_a = [_residue_spec(tr, d, AW) for d in DILATIONS]
    res_d = [_residue_spec(tr, d, LANES) for d in DILATIONS]
    res = pl.pallas_call(
        body, name="mix_bwd",
        out_shape=([_residue_shape(S, d, AW, BF16) for d in DILATIONS]
                   + [_residue_shape(S, d, LANES, F32) for d in DILATIONS]
                   + [jax.ShapeDtypeStruct((S, SW), BF16), jax.ShapeDtypeStruct((S, SW), F32),
                      jax.ShapeDtypeStruct((1, AW), F32), jax.ShapeDtypeStruct((1, SW), F32),
                      jax.ShapeDtypeStruct((1, SW), F32)]),
        grid=(S // tr,),
        in_specs=[_rows(tr, AW + SW), _rows(tr, AW), _rows(tr, SW), _rows(tr, SW), _vec(SW), _vec(AW), _vec(SW)],
        out_specs=res_a + res_d + [_rows(tr, SW), _rows(tr, SW), _vec(AW), _vec(SW), _vec(SW)],
        scratch_shapes=[_residue_scratch(tr, AW), _residue_scratch(tr, LANES), pltpu.VMEM((tr, LANES), F32)],
        compiler_params=_cparams("arbitrary"))(dmixed, attn, y1, z, b_glu, g_attn, g_ssm)
    return (res[:nd], res[nd:2 * nd]) + tuple(res[2 * nd:])


def _attn_mask2(i):
    row = lax.broadcasted_iota(jnp.int32, (BLK, 2 * BLK), 0)
    col = lax.broadcasted_iota(jnp.int32, (BLK, 2 * BLK), 1)
    return jnp.logical_and(col >= row, jnp.logical_and(col <= row + BLK, jnp.logical_or(col >= BLK, i > 0)))


_NT = (((1,), (1,)), ((), ()))
_TN = (((0,), (0,)), ((), ()))


def _attn_in_specs(width, block_of):
    def at(part, prev):
        def index(r, i):
            blk = block_of(i)
            return (part, jnp.maximum(blk - 1, 0) if prev else blk, r)
        return pl.BlockSpec((None, BLK, width), index)
    return [at(0, False), at(1, False), at(1, True), at(2, False), at(2, True)]


def _proj_qkv(hn, w_in_f, *, tm=1024):
    S, D = hn.shape
    AW = w_in_f.shape[2]
    tm = min(tm, S)

    def body(a_ref, b_ref, *rest):
        o_refs, scr = rest[:-1], rest[-1]
        prod = jnp.dot(a_ref[...], b_ref[...], preferred_element_type=F32)
        _fill_strips(scr, prod)
        for o_ref, d in zip(o_refs, DILATIONS):
            if d == 1:
                o_ref[...] = prod.astype(BF16)
            else:
                _strips_to_residues(scr, o_ref, d)

    return pl.pallas_call(
        body, name="proj_qkv",
        out_shape=[jax.ShapeDtypeStruct((3, S // d, d * AW), BF16) for d in DILATIONS], grid=(S // tm, 3),
        in_specs=[pl.BlockSpec((tm, D), lambda i, j: (i, 0)), pl.BlockSpec((None, D, AW), lambda i, j: (j, 0, 0))],
        out_specs=[pl.BlockSpec((None, tm // d, d * AW), lambda i, j: (j, i, 0)) for d in DILATIONS],
        scratch_shapes=[_residue_scratch(tm, AW)],
        compiler_params=_cparams("parallel", "parallel"))(hn, w_in_f)


def _attn_fwd(qkv, d, heads):
    M = qkv.shape[1]
    nb = M // BLK
    width = heads * HEAD_DIM
    per = LANES // heads
    scale = 1.0 / math.sqrt(HEAD_DIM)

    def body(q_ref, kc_ref, kp_ref, vc_ref, vp_ref, o_ref, l_ref):
        mask = _attn_mask2(pl.program_id(1))
        ones = jnp.ones((2 * BLK, HEAD_DIM), BF16)

        def scores(h):
            sl = slice(h * HEAD_DIM, (h + 1) * HEAD_DIM)
            k2 = jnp.concatenate([kp_ref[:, sl], kc_ref[:, sl]], axis=0)
            return lax.dot_general(q_ref[:, sl], k2, _NT, preferred_element_type=F32)

        ahead = [scores(h) for h in range(min(ATTN_LOOKAHEAD, heads))]
        for h in range(heads):
            sl = slice(h * HEAD_DIM, (h + 1) * HEAD_DIM)
            s = jnp.where(mask, ahead.pop(0) * scale, NEG_INF)
            if h + ATTN_LOOKAHEAD < heads:
                ahead.append(scores(h + ATTN_LOOKAHEAD))
            v2 = jnp.concatenate([vp_ref[:, sl], vc_ref[:, sl]], axis=0)
            m = jnp.max(jnp.maximum(s[:, :BLK], s[:, BLK:]), axis=-1, keepdims=True)
            p = jnp.exp(s - m).astype(BF16)
            tot = jnp.dot(p, ones, preferred_element_type=F32)
            o_ref[:, sl] = (jnp.dot(p, v2, preferred_element_type=F32) / tot).astype(BF16)
            l_ref[:, h * per:(h + 1) * per] = m + jnp.log(tot[:, :per])

    return pl.pallas_call(
        body, name=f"attn_fwd_d{d}",
        out_shape=(jax.ShapeDtypeStruct((M, d * width), BF16), jax.ShapeDtypeStruct((M, d * LANES), F32)),
        grid=(d, nb), in_specs=_attn_in_specs(width, lambda i: i),
        out_specs=(pl.BlockSpec((BLK, width), lambda r, i: (i, r)), pl.BlockSpec((BLK, LANES), lambda r, i: (i, r))),
        compiler_params=_cparams("parallel", "parallel"))(qkv, qkv, qkv, qkv, qkv)


def _attn_bwd(qkv, dattn, lse, dd, d, heads, after):
    M = qkv.shape[1]
    nb = M // BLK
    width = heads * HEAD_DIM
    per = LANES // heads
    scale = 1.0 / math.sqrt(HEAD_DIM)

    def block_of(i):
        return nb - 1 - i

    def body(q_ref, kc_ref, kp_ref, vc_ref, vp_ref, da_ref, l_ref, dd_ref, after_ref,
             dq_ref, dk_ref, dv_ref, dk_carry, dv_carry):
        @pl.when(pl.program_id(1) == 0)
        def _():
            dk_carry[...] = jnp.zeros_like(dk_carry)
            dv_carry[...] = jnp.zeros_like(dv_carry)

        mask = _attn_mask2(block_of(pl.program_id(1)))

        def products(h):
            sl = slice(h * HEAD_DIM, (h + 1) * HEAD_DIM)
            k2 = jnp.concatenate([kp_ref[:, sl], kc_ref[:, sl]], axis=0)
            v2 = jnp.concatenate([vp_ref[:, sl], vc_ref[:, sl]], axis=0)
            return (lax.dot_general(q_ref[:, sl], k2, _NT, preferred_element_type=F32),
                    lax.dot_general(da_ref[:, sl], v2, _NT, preferred_element_type=F32), k2)

        ahead = [products(h) for h in range(min(ATTN_LOOKAHEAD, heads))]
        for h in range(heads):
            sl = slice(h * HEAD_DIM, (h + 1) * HEAD_DIM)
            qk, dp, k2 = ahead.pop(0)
            if h + ATTN_LOOKAHEAD < heads:
                ahead.append(products(h + ATTN_LOOKAHEAD))
            q, da = q_ref[:, sl], da_ref[:, sl]
            lse_ = jnp.broadcast_to(l_ref[:, h * per:h * per + 1], (BLK, 2 * BLK))
            dd_ = jnp.broadcast_to(dd_ref[:, h * per:h * per + 1], (BLK, 2 * BLK))
            p = jnp.where(mask, jnp.exp(jnp.where(mask, qk * scale, NEG_INF) - lse_), 0.0)
            ds = (p * (dp - dd_) * scale).astype(BF16)
            dq_ref[:, sl] = jnp.dot(ds, k2, preferred_element_type=F32).astype(BF16)
            dk2 = lax.dot_general(ds, q, _TN, preferred_element_type=F32)
            dv2 = lax.dot_general(p.astype(BF16), da, _TN, preferred_element_type=F32)
            dk_ref[:, sl] = (dk2[BLK:] + dk_carry[:, sl]).astype(BF16)
            dv_ref[:, sl] = (dv2[BLK:] + dv_carry[:, sl]).astype(BF16)
            dk_carry[:, sl] = dk2[:BLK]
            dv_carry[:, sl] = dv2[:BLK]

    blk = pl.BlockSpec((BLK, width), lambda r, i: (block_of(i), r))
    packed = pl.BlockSpec((BLK, LANES), lambda r, i: (block_of(i), r))
    shape = jax.ShapeDtypeStruct((M, d * width), BF16)
    return pl.pallas_call(
        body, name=f"attn_bwd_d{d}", out_shape=(shape,) * 3, grid=(d, nb),
        in_specs=(_attn_in_specs(width, block_of) + [blk, packed, packed]
                  + [pl.BlockSpec(after.shape, lambda r, i: (0, 0))]), out_specs=(blk,) * 3,
        scratch_shapes=[pltpu.VMEM((BLK, width), F32), pltpu.VMEM((BLK, width), F32)],
        compiler_params=_cparams("arbitrary", "arbitrary"))(qkv, qkv, qkv, qkv, qkv, dattn, lse, dd, after)


def _dproj_join(dqs, dks, dvs, du, *, tr=256):
    S, SW = du.shape
    AW = dqs[0].shape[1]
    tr = min(tr, S)
    nd = len(DILATIONS)

    def body(*refs):
        du_ref, out_ref, scr = refs[3 * nd:]
        for part in range(3):
            total = functools.reduce(jnp.add, [_from_residues(scr, refs[part * nd + n], d)
                                               for n, d in enumerate(DILATIONS)])
            out_ref[:, part * AW:(part + 1) * AW] = total.astype(BF16)
        out_ref[:, 3 * AW:] = du_ref[...].astype(BF16)

    return pl.pallas_call(
        body, name="dproj_join", out_shape=jax.ShapeDtypeStruct((S, 3 * AW + SW), BF16), grid=(S // tr,),
        in_specs=[_residue_spec(tr, d, AW) for d in DILATIONS] * 3 + [_rows(tr, SW)],
        out_specs=_rows(tr, 3 * AW + SW), scratch_shapes=[_residue_scratch(tr, AW)],
        compiler_params=_cparams("parallel"))(*dqs, *dks, *dvs, du)


def _ssm_disc(lr, li, ldt):
    dt = jnp.exp(ldt)
    mag = jnp.exp(lr * dt)
    ar = mag * jnp.cos(li * dt)
    ai = mag * jnp.sin(li * dt)
    nr = ar - 1.0
    den = lr * lr + li * li
    return ar, ai, (nr * lr + ai * li) / den, (ai * lr - nr * li) / den


def _ssm_tile_powers(lr, li, ldt, reverse):
    t = lax.broadcasted_iota(jnp.int32, (TILE, 1), 0)
    n = (TILE - t if reverse else t + 1).astype(F32)
    dt = jnp.exp(ldt)
    mag = jnp.exp(n * (lr * dt))
    ang = n * (li * dt)
    return mag * jnp.cos(ang), mag * jnp.sin(ang) * (-1.0 if reverse else 1.0)


def _cmul(ar, ai, br, bi):
    return ar * br - ai * bi, ar * bi + ai * br


LOG_STEPS = 3


def _ssm_step_tables(ar, ai, reverse):
    sub = lax.broadcasted_iota(jnp.int32, (TILE, ar.shape[-1]), 0)
    tables = []
    for k in range(LOG_STEPS):
        keep = sub < TILE - (1 << k) if reverse else sub >= (1 << k)
        tables.append((jnp.where(keep, ar, 0.0), jnp.where(keep, ai, 0.0)))
        ar, ai = _cmul(ar, ai, ar, ai)
    return tables


def _scan(xr, xi, steps, pr, pi, cr, ci, reverse):
    T, lanes = xr.shape
    n = T // TILE
    xr, xi = xr.reshape(n, TILE, lanes), xi.reshape(n, TILE, lanes)
    for k, (mr, mi) in enumerate(steps):
        shift = TILE - (1 << k) if reverse else 1 << k
        qr, qi = _cmul(mr, mi, pltpu.roll(xr, shift, 1), pltpu.roll(xi, shift, 1))
        xr, xi = xr + qr, xi + qi
    out_r, out_i = [None] * n, [None] * n
    edge = 0 if reverse else TILE - 1
    for j in (reversed(range(n)) if reverse else range(n)):
        er, ei = _cmul(pr, pi, cr, ci)
        sr, si = xr[j] + er, xi[j] + ei
        out_r[j], out_i[j] = sr, si
        cr, ci = sr[edge:edge + 1], si[edge:edge + 1]
    return jnp.concatenate(out_r, axis=0), jnp.concatenate(out_i, axis=0), cr, ci


def _ssm_specs(T, nch, rev):
    def t_of(c):
        return nch - 1 - c if rev else c
    tok = pl.BlockSpec((T, LANES), lambda j, c: (t_of(c), j))
    par = pl.BlockSpec((None, 1, STATE_LANES), lambda j, c: (j, 0, 0))
    bmat = pl.BlockSpec((None, LANES, STATE_LANES), lambda j, c: (j, 0, 0))
    cmat = pl.BlockSpec((None, STATE_LANES, LANES), lambda j, c: (j, 0, 0))
    dvec = pl.BlockSpec((1, LANES), lambda j, c: (0, j))
    return tok, par, bmat, cmat, dvec


def _ssm_fwd(u, lr_e, li_e, ldt_e, bre_e, bim_e, cre_e, cim_e, d_skip):
    S, SW = u.shape
    T = min(SSM_CHUNK, S)
    nch, nbk = S // T, SW // LANES
    tok, par, bmat, cmat, dvec = _ssm_specs(T, nch, False)
    state_spec = pl.BlockSpec((T, STATE_LANES), lambda j, c: (c, j))
    carry_spec = pl.BlockSpec((None, 1, STATE_LANES), lambda j, c: (c, 0, j))

    def body(u_ref, lr_ref, li_ref, ldt_ref, bre_ref, bim_ref, cre_ref, cim_ref, d_ref,
             y_ref, y2_ref, sr_ref, si_ref, er_ref, ei_ref, bbr, bbi, steps, pw, carry):
        c = pl.program_id(1)

        @pl.when(c == 0)
        def _():
            lr, li, ldt = lr_ref[...], li_ref[...], ldt_ref[...]
            ar, ai, kr, ki = _ssm_disc(lr, li, ldt)
            for k, (mr, mi) in enumerate(_ssm_step_tables(ar, ai, False)):
                steps[0, k], steps[1, k] = mr, mi
            bbr[...] = (kr * bre_ref[...] - ki * bim_ref[...]).astype(BF16)
            bbi[...] = (kr * bim_ref[...] + ki * bre_ref[...]).astype(BF16)
            pw[0], pw[1] = _ssm_tile_powers(lr, li, ldt, False)
            carry[...] = jnp.zeros_like(carry)

        u_ = u_ref[...]
        ub = u_.astype(BF16)
        sr, si, cr, ci = _scan(jnp.dot(ub, bbr[...], preferred_element_type=F32),
                               jnp.dot(ub, bbi[...], preferred_element_type=F32),
                               [(steps[0, k], steps[1, k]) for k in range(LOG_STEPS)],
                               pw[0], pw[1], carry[0], carry[1], False)
        carry[0], carry[1] = cr, ci
        er_ref[...], ei_ref[...] = cr, ci
        sr_ref[...], si_ref[...] = sr, si
        y0 = (jnp.dot(sr.astype(BF16), cre_ref[...].astype(BF16), preferred_element_type=F32)
              - jnp.dot(si.astype(BF16), cim_ref[...].astype(BF16), preferred_element_type=F32))
        y1 = y0 + d_ref[...] * u_
        y_ref[...] = y1
        y2_ref[...] = _gelu(y1).astype(BF16)

    states = jax.ShapeDtypeStruct((S, nbk * STATE_LANES), F32)
    ends = jax.ShapeDtypeStruct((nch, 1, nbk * STATE_LANES), F32)
    return pl.pallas_call(
        body, name="ssm_fwd",
        out_shape=(jax.ShapeDtypeStruct((S, SW), F32), jax.ShapeDtypeStruct((S, SW), BF16), states, states, ends, ends),
        grid=(nbk, nch), in_specs=[tok, par, par, par, bmat, bmat, cmat, cmat, dvec],
        out_specs=(tok, tok, state_spec, state_spec, carry_spec, carry_spec),
        scratch_shapes=[pltpu.VMEM((LANES, STATE_LANES), BF16), pltpu.VMEM((LANES, STATE_LANES), BF16),
                        pltpu.VMEM((2, LOG_STEPS, TILE, STATE_LANES), F32), pltpu.VMEM((2, TILE, STATE_LANES), F32),
                        pltpu.VMEM((2, 1, STATE_LANES), F32)],
        compiler_params=_cparams("arbitrary", "arbitrary"),
    )(u, lr_e, li_e, ldt_e, bre_e, bim_e, cre_e, cim_e, d_skip)


def _ssm_bwd(u, y1, dy2a, dy2b, st_r, st_i, ends_r, ends_i, lr_e, li_e, ldt_e, bre_e, bim_e, cre_e, cim_e, d_skip):
    S, SW = u.shape
    T = min(SSM_CHUNK, S)
    nch, nbk = S // T, SW // LANES
    tok, par, bmat, cmat, dvec = _ssm_specs(T, nch, True)
    state_spec = pl.BlockSpec((T, STATE_LANES), lambda j, c: (nch - 1 - c, j))
    prev_spec = pl.BlockSpec((None, 1, STATE_LANES), lambda j, c: (jnp.maximum(nch - 2 - c, 0), 0, j))
    acc8 = pl.BlockSpec((None, 8, STATE_LANES), lambda j, c: (j, 0, 0))
    dd8 = pl.BlockSpec((None, 8, LANES), lambda j, c: (j, 0, 0))

    def body(u_ref, y_ref, da_ref, db_ref, sr_ref, si_ref, pr_ref, pi_ref, lr_ref, li_ref, ldt_ref,
             bre_ref, bim_ref, cre_ref, cim_ref, d_ref,
             du_ref, dar_ref, dai_ref, dcr_ref, dci_ref, dbr_ref, dbi_ref, ddk_ref,
             bbr, bbi, steps, pw, carry):
        c = pl.program_id(1)

        @pl.when(c == 0)
        def _():
            lr, li, ldt = lr_ref[...], li_ref[...], ldt_ref[...]
            ar, ai, kr, ki = _ssm_disc(lr, li, ldt)
            for k, (mr, mi) in enumerate(_ssm_step_tables(ar, -ai, True)):
                steps[0, k], steps[1, k] = mr, mi
            bbr[...] = (kr * bre_ref[...] - ki * bim_ref[...]).astype(BF16)
            bbi[...] = (kr * bim_ref[...] + ki * bre_ref[...]).astype(BF16)
            pw[0], pw[1] = _ssm_tile_powers(lr, li, ldt, True)
            carry[...] = jnp.zeros_like(carry)
            for ref in (dar_ref, dai_ref, dcr_ref, dci_ref, dbr_ref, dbi_ref, ddk_ref):
                ref[...] = jnp.zeros_like(ref)

        u_ = u_ref[...]
        ub = u_.astype(BF16)
        dy1 = (da_ref[...] + db_ref[...]) * _gelu_grad(y_ref[...])
        dyb = dy1.astype(BF16)

        sr, si = sr_ref[...], si_ref[...]
        has_prev = c < nch - 1
        s0r = jnp.where(has_prev, pr_ref[...], 0.0)
        s0i = jnp.where(has_prev, pi_ref[...], 0.0)

        cre_b, cim_b = cre_ref[...].astype(BF16), cim_ref[...].astype(BF16)
        gr, gi, cr, ci = _scan(lax.dot_general(dyb, cre_b, _NT, preferred_element_type=F32),
                               -lax.dot_general(dyb, cim_b, _NT, preferred_element_type=F32),
                               [(steps[0, k], steps[1, k]) for k in range(LOG_STEPS)],
                               pw[0], pw[1], carry[0], carry[1], True)
        carry[0], carry[1] = cr, ci

        row = lax.broadcasted_iota(jnp.int32, (T, STATE_LANES), 0)
        spr = jnp.where(row == 0, s0r, pltpu.roll(sr, 1, 0))
        spi = jnp.where(row == 0, s0i, pltpu.roll(si, 1, 0))

        def fold(a):
            return jnp.sum(a.reshape(T // 8, 8, a.shape[-1]), axis=0)

        dar_ref[...] += fold(gr * spr + gi * spi)
        dai_ref[...] += fold(gi * spr - gr * spi)
        srb, sib, grb, gib = sr.astype(BF16), si.astype(BF16), gr.astype(BF16), gi.astype(BF16)
        dcr_ref[...] += lax.dot_general(srb, dyb, _TN, preferred_element_type=F32)
        dci_ref[...] -= lax.dot_general(sib, dyb, _TN, preferred_element_type=F32)
        dbr_ref[...] += lax.dot_general(ub, grb, _TN, preferred_element_type=F32)
        dbi_ref[...] += lax.dot_general(ub, gib, _TN, preferred_element_type=F32)
        du_ref[...] = (lax.dot_general(grb, bbr[...], _NT, preferred_element_type=F32)
                       + lax.dot_general(gib, bbi[...], _NT, preferred_element_type=F32)
                       + dy1 * d_ref[...])
        ddk_ref[...] += fold(dy1 * u_)

    return pl.pallas_call(
        body, name="ssm_bwd",
        out_shape=(jax.ShapeDtypeStruct((S, SW), F32),
                   jax.ShapeDtypeStruct((nbk, 8, STATE_LANES), F32), jax.ShapeDtypeStruct((nbk, 8, STATE_LANES), F32),
                   jax.ShapeDtypeStruct((nbk, STATE_LANES, LANES), F32), jax.ShapeDtypeStruct((nbk, STATE_LANES, LANES), F32),
                   jax.ShapeDtypeStruct((nbk, LANES, STATE_LANES), F32), jax.ShapeDtypeStruct((nbk, LANES, STATE_LANES), F32),
                   jax.ShapeDtypeStruct((nbk, 8, LANES), F32)),
        grid=(nbk, nch),
        in_specs=[tok, tok, tok, tok, state_spec, state_spec, prev_spec, prev_spec, par, par, par,
                  bmat, bmat, cmat, cmat, dvec],
        out_specs=(tok, acc8, acc8, cmat, cmat, bmat, bmat, dd8),
        scratch_shapes=[pltpu.VMEM((LANES, STATE_LANES), BF16), pltpu.VMEM((LANES, STATE_LANES), BF16),
                        pltpu.VMEM((2, LOG_STEPS, TILE, STATE_LANES), F32), pltpu.VMEM((2, TILE, STATE_LANES), F32),
                        pltpu.VMEM((2, 1, STATE_LANES), F32)],
        compiler_params=_cparams("arbitrary", "arbitrary"),
    )(u, y1, dy2a, dy2b, st_r, st_i, ends_r, ends_i, lr_e, li_e, ldt_e, bre_e, bim_e, cre_e, cim_e, d_skip)


def _ssm_param_bwd(dar8, dai8, dbr_e, dbi_e, lr_e, li_e, ldt_e, bre_e, bim_e):
    nbk = lr_e.shape[0]
    par = pl.BlockSpec((None, 1, STATE_LANES), lambda j: (j, 0, 0))
    acc8 = pl.BlockSpec((None, 8, STATE_LANES), lambda j: (j, 0, 0))
    bmat = pl.BlockSpec((None, LANES, STATE_LANES), lambda j: (j, 0, 0))

    def body(dar_ref, dai_ref, dbr_ref, dbi_ref, lr_ref, li_ref, ldt_ref, bre_ref, bim_ref,
             dlr_ref, dli_ref, dldt_ref, dbre_ref, dbim_ref):
        lr, li, ldt = lr_ref[...], li_ref[...], ldt_ref[...]
        (ar, ai, kr, ki), vjp = jax.vjp(_ssm_disc, lr, li, ldt)
        dbr, dbi, bre, bim = dbr_ref[...], dbi_ref[...], bre_ref[...], bim_ref[...]
        dbre_ref[...] = kr * dbr + ki * dbi
        dbim_ref[...] = kr * dbi - ki * dbr
        dkr = _colsum(dbr * bre + dbi * bim)
        dki = _colsum(dbi * bre - dbr * bim)
        dlr, dli, dldt = vjp((_colsum(dar_ref[...]), _colsum(dai_ref[...]), dkr, dki))
        dlr_ref[...] = dlr
        dli_ref[...] = dli
        tot = jnp.broadcast_to(dldt, (8, STATE_LANES))
        sh = 1
        while sh < SSM_P:
            tot = tot + pltpu.roll(tot, STATE_LANES - sh, 1)
            sh *= 2
        dldt_ref[...] = tot[:1]

    vec = jax.ShapeDtypeStruct((nbk, 1, STATE_LANES), F32)
    mat = jax.ShapeDtypeStruct((nbk, LANES, STATE_LANES), F32)
    return pl.pallas_call(
        body, name="ssm_param_bwd", out_shape=(vec, vec, vec, mat, mat), grid=(nbk,),
        in_specs=[acc8, acc8, bmat, bmat, par, par, par, bmat, bmat],
        out_specs=(par, par, par, bmat, bmat), compiler_params=_cparams("parallel"),
    )(dar8, dai8, dbr_e, dbi_e, lr_e, li_e, ldt_e, bre_e, bim_e)


def _expand_b(b):
    G = b.shape[0]
    bt = b.transpose(0, 2, 1).reshape(G // GROUPS_PER_BLOCK, GROUPS_PER_BLOCK, SSM_C, SSM_P)
    eye = jnp.eye(GROUPS_PER_BLOCK, dtype=b.dtype)
    return (bt[:, :, :, None, :] * eye[None, :, None, :, None]).reshape(G // GROUPS_PER_BLOCK, LANES, STATE_LANES)


def _collapse_b(be):
    nbk = be.shape[0]
    eye = jnp.eye(GROUPS_PER_BLOCK, dtype=be.dtype)
    d5 = be.reshape(nbk, GROUPS_PER_BLOCK, SSM_C, GROUPS_PER_BLOCK, SSM_P)
    d4 = (d5 * eye[None, :, None, :, None]).sum(axis=3)
    return d4.transpose(0, 1, 3, 2).reshape(nbk * GROUPS_PER_BLOCK, SSM_P, SSM_C)


def _expand_c(cm):
    G = cm.shape[0]
    ct = cm.transpose(0, 2, 1).reshape(G // GROUPS_PER_BLOCK, GROUPS_PER_BLOCK, SSM_P, SSM_C)
    eye = jnp.eye(GROUPS_PER_BLOCK, dtype=cm.dtype)
    return (ct[:, :, :, None, :] * eye[None, :, None, :, None]).reshape(G // GROUPS_PER_BLOCK, STATE_LANES, LANES)


def _collapse_c(ce):
    nbk = ce.shape[0]
    eye = jnp.eye(GROUPS_PER_BLOCK, dtype=ce.dtype)
    d5 = ce.reshape(nbk, GROUPS_PER_BLOCK, SSM_P, GROUPS_PER_BLOCK, SSM_C)
    d4 = (d5 * eye[None, :, None, :, None]).sum(axis=3)
    return d4.transpose(0, 1, 3, 2).reshape(nbk * GROUPS_PER_BLOCK, SSM_C, SSM_P)


def _place():
    x, y, c = lax.axis_index("x"), lax.axis_index("y"), lax.axis_index("c")
    return x, y, c


def _other_chips(x, y):
    return [(1 - x, y), (x, 1 - y), (1 - x, 1 - y)]


_ANY = pl.BlockSpec(memory_space=pl.ANY)


_HBM = pl.BlockSpec(memory_space=pltpu.HBM)
_SEM = pl.BlockSpec(memory_space=pltpu.SEMAPHORE)
_EFFECT = pltpu.SideEffectType.DATAFLOW_SIDE_EFFECTING
_TOKEN = jax.ShapeDtypeStruct((8, LANES), F32)


def _hbm(a):
    return pltpu.with_memory_space_constraint(a, pltpu.HBM)


def _place_own(src, *, gather, name, after=None, tr=512):
    R, C = src.shape[-2:]
    tr = min(tr, R)
    x, y, _ = _place()
    me = (2 * x + y).astype(jnp.int32).reshape(1)
    extra = [] if after is None else [after]

    def body(me_ref, s_ref, *rest):
        rest[-1][...] = s_ref[...].astype(BF16)

    own = pl.BlockSpec((None, tr, C), lambda i, me_ref: (me_ref[0], i, 0))
    grid_spec = pltpu.PrefetchScalarGridSpec(
        num_scalar_prefetch=1, grid=(R // tr,),
        in_specs=([pl.BlockSpec((tr, C), lambda i, me_ref: (i, 0)) if gather else own]
                  + [pl.BlockSpec(a.shape, lambda i, me_ref: (0, 0)) for a in extra]), out_specs=own)
    return pl.pallas_call(
        body, name=name, grid_spec=grid_spec, out_shape=jax.ShapeDtypeStruct((N_CHIPS, R, C), BF16),
        compiler_params=_cparams("parallel"))(me, src, *extra)


def _exchange_copy(src_slot, land_slot, send, recv, k, j, peer, c):
    return pltpu.make_async_remote_copy(
        src_ref=src_slot, dst_ref=land_slot, send_sem=send.at[3 * k + j], recv_sem=recv.at[3 * k + j],
        device_id=(peer[0], peer[1], c), device_id_type=MESH)


def _exchange_start(lands, srcs, groups, *, name):
    n, ng = len(lands), len(groups)
    bufs = list(lands) + list(srcs)
    nb = len(bufs)

    def body(*refs):
        lnd, src, sems = refs[:n], refs[n:nb], refs[nb:nb + 2 * ng]
        token = refs[2 * nb + 2 * ng]
        x, y, c = _place()
        me = 2 * x + y
        for gi, group in enumerate(groups):
            for k, w in enumerate(group):
                for j, peer in enumerate(_other_chips(x, y)):
                    if src:
                        sent, dst = src[w].at[2 * peer[0] + peer[1]], lnd[w].at[me]
                    else:
                        sent = dst = lnd[w].at[me, c]
                    _exchange_copy(sent, dst, sems[2 * gi], sems[2 * gi + 1], k, j, peer, c).start()
        token[...] = jnp.zeros_like(token)

    sem_shapes = [pltpu.SemaphoreType.DMA((3 * len(g),)) for g in groups for _ in range(2)]
    res = pl.pallas_call(
        body, name=name,
        out_shape=sem_shapes + [pltpu.HBM(a.shape, a.dtype) for a in bufs] + [_TOKEN],
        in_specs=[_HBM] * nb,
        out_specs=[_SEM] * (2 * ng) + [_HBM] * nb + [pl.BlockSpec(memory_space=pltpu.VMEM)],
        input_output_aliases={i: 2 * ng + i for i in range(nb)},
        compiler_params=pltpu.CompilerParams(has_side_effects=_EFFECT),
    )(*[_hbm(a) for a in bufs])
    sems = [(res[2 * gi], res[2 * gi + 1]) for gi in range(ng)]
    return sems, res[2 * ng:2 * ng + n], res[2 * ng + n:2 * ng + nb], res[-1]


def _exchange_wait(lands, srcs, sems, after, *, name):
    n = len(lands)
    bufs = list(lands) + list(srcs)
    nb = len(bufs)
    send_sems, recv_sems = sems

    def body(*refs):
        lnd, src, send, recv = refs[:n], refs[n:nb], refs[nb], refs[nb + 1]
        x, y, c = _place()
        for k in range(n):
            for j, peer in enumerate(_other_chips(x, y)):
                slot = 2 * peer[0] + peer[1]
                if src:
                    copy = _exchange_copy(src[k].at[slot], lnd[k].at[slot], send, recv, k, j, peer, c)
                else:
                    copy = _exchange_copy(lnd[k].at[slot, c], lnd[k].at[slot, c], send, recv, k, j, peer, c)
                copy.wait_send()
                copy.wait_recv()

    res = pl.pallas_call(
        body, name=name, out_shape=[pltpu.HBM(a.shape, a.dtype) for a in bufs],
        in_specs=[_HBM] * nb + [_SEM, _SEM, _ANY], out_specs=[_HBM] * nb,
        input_output_aliases={i: i for i in range(nb)},
        compiler_params=pltpu.CompilerParams(has_side_effects=_EFFECT),
    )(*bufs, send_sems, recv_sems, after)
    return res[:n]


def _pair_fill(lands, *, name):
    n = len(lands)

    def body(*refs):
        ins, outs, send, recv = refs[:n], refs[n:2 * n], refs[2 * n], refs[2 * n + 1]
        x, y, c = _place()
        for w in range(n):
            for j, (px, py) in enumerate(_other_chips(x, y)):
                slot = 2 * px + py
                pltpu.make_async_remote_copy(
                    src_ref=ins[w].at[slot, c], dst_ref=outs[w].at[slot, c], send_sem=send.at[3 * w + j],
                    recv_sem=recv.at[3 * w + j], device_id=(x, y, 1 - c), device_id_type=MESH).start()
        for w in range(n):
            for j, (px, py) in enumerate(_other_chips(x, y)):
                slot = 2 * px + py
                arrival = pltpu.make_async_remote_copy(
                    src_ref=ins[w].at[slot, c], dst_ref=outs[w].at[slot, 1 - c], send_sem=send.at[3 * w + j],
                    recv_sem=recv.at[3 * w + j], device_id=(x, y, 1 - c), device_id_type=MESH)
                arrival.wait_recv()
                arrival.wait_send()

    return pl.pallas_call(
        body, name=name, out_shape=[jax.ShapeDtypeStruct(a.shape, a.dtype) for a in lands],
        in_specs=[_ANY] * n, out_specs=[_ANY] * n, input_output_aliases={i: i for i in range(n)},
        scratch_shapes=[pltpu.SemaphoreType.DMA((3 * n,)), pltpu.SemaphoreType.DMA((3 * n,))],
    )(*lands)


def _pair_copy(src, dst, send, recv, w, j, sibling):
    return pltpu.make_async_remote_copy(
        src_ref=src, dst_ref=dst, send_sem=send.at[3 * w + j], recv_sem=recv.at[3 * w + j],
        device_id=sibling, device_id_type=MESH)


def _pair_start(lands, *, name):
    n = len(lands)

    def body(*refs):
        bufs, send, recv, token = refs[:n], refs[n], refs[n + 1], refs[2 * n + 2]
        x, y, c = _place()
        for w in range(n):
            for j, (px, py) in enumerate(_other_chips(x, y)):
                half = bufs[w].at[2 * px + py, c]
                _pair_copy(half, half, send, recv, w, j, (x, y, 1 - c)).start()
        token[...] = jnp.zeros_like(token)

    res = pl.pallas_call(
        body, name=name,
        out_shape=[pltpu.SemaphoreType.DMA((3 * n,))] * 2 + [pltpu.HBM(a.shape, a.dtype) for a in lands] + [_TOKEN],
        in_specs=[_HBM] * n, out_specs=[_SEM, _SEM] + [_HBM] * n + [pl.BlockSpec(memory_space=pltpu.VMEM)],
        input_output_aliases={i: 2 + i for i in range(n)},
        compiler_params=pltpu.CompilerParams(has_side_effects=_EFFECT),
    )(*[_hbm(a) for a in lands])
    return (res[0], res[1]), res[2:2 + n], res[-1]


def _pair_wait(lands, sems, after, *, name):
    n = len(lands)

    def body(*refs):
        bufs, send, recv = refs[:n], refs[n], refs[n + 1]
        x, y, c = _place()
        for w in range(n):
            for j, (px, py) in enumerate(_other_chips(x, y)):
                slot = 2 * px + py
                copy = _pair_copy(bufs[w].at[slot, c], bufs[w].at[slot, 1 - c], send, recv, w, j, (x, y, 1 - c))
                copy.wait_send()
                copy.wait_recv()

    return pl.pallas_call(
        body, name=name, out_shape=[pltpu.HBM(a.shape, a.dtype) for a in lands],
        in_specs=[_HBM] * n + [_SEM, _SEM, _ANY], out_specs=[_HBM] * n,
        input_output_aliases={i: i for i in range(n)},
        compiler_params=pltpu.CompilerParams(has_side_effects=_EFFECT),
    )(*lands, *sems, after)


def _sum_partials(land, *, name, tr=256):
    _, R, C = land.shape
    tr = min(tr, R)

    def body(l_ref, o_ref):
        acc = l_ref[0].astype(F32)
        for k in range(1, N_CHIPS):
            acc = acc + l_ref[k].astype(F32)
        o_ref[...] = acc

    return pl.pallas_call(
        body, name=name, out_shape=jax.ShapeDtypeStruct((R, C), F32), grid=(R // tr,),
        in_specs=[pl.BlockSpec((N_CHIPS, tr, C), lambda i: (0, i, 0))], out_specs=_rows(tr, C),
        compiler_params=_cparams("parallel"))(land)


def _swap_with_sibling(sums, *, name):
    n = len(sums)

    def body(*refs):
        ins, outs = refs[:n], refs[n:2 * n]
        send_sems, recv_sems = refs[2 * n:]
        x, y, c = _place()
        copies = [pltpu.make_async_remote_copy(
            src_ref=ins[w], dst_ref=outs[w], send_sem=send_sems.at[w], recv_sem=recv_sems.at[w],
            device_id=(x, y, 1 - c), device_id_type=MESH) for w in range(n)]
        for cp in copies:
            cp.start()
        for cp in copies:
            cp.wait_recv()
            cp.wait_send()

    return pl.pallas_call(
        body, name=name,
        out_shape=[jax.ShapeDtypeStruct(s.shape, s.dtype) for s in sums],
        in_specs=[_ANY] * n, out_specs=[_ANY] * n,
        scratch_shapes=[pltpu.SemaphoreType.DMA((n,)), pltpu.SemaphoreType.DMA((n,))],
    )(*sums)


def _swap_start(sums, *, name):
    n = len(sums)
    bufs = list(sums) + [lax.empty(s.shape, s.dtype) for s in sums]

    def body(*refs):
        src, lnd, send, recv, token = refs[:n], refs[n:2 * n], refs[2 * n], refs[2 * n + 1], refs[4 * n + 2]
        x, y, c = _place()
        for w in range(n):
            pltpu.make_async_remote_copy(
                src_ref=src[w], dst_ref=lnd[w], send_sem=send.at[w], recv_sem=recv.at[w],
                device_id=(x, y, 1 - c), device_id_type=MESH).start()
        token[...] = jnp.zeros_like(token)

    res = pl.pallas_call(
        body, name=name,
        out_shape=[pltpu.SemaphoreType.DMA((n,))] * 2 + [pltpu.HBM(a.shape, a.dtype) for a in bufs] + [_TOKEN],
        in_specs=[_HBM] * (2 * n),
        out_specs=[_SEM, _SEM] + [_HBM] * (2 * n) + [pl.BlockSpec(memory_space=pltpu.VMEM)],
        input_output_aliases={i: 2 + i for i in range(2 * n)},
        compiler_params=pltpu.CompilerParams(has_side_effects=_EFFECT),
    )(*[_hbm(a) for a in bufs])
    return (res[0], res[1]), res[2:2 + n], res[2 + n:2 + 2 * n], res[-1]


def _swap_wait(sums, lands, sems, after, *, name):
    n = len(sums)

    def body(*refs):
        src, lnd, send, recv = refs[:n], refs[n:2 * n], refs[2 * n], refs[2 * n + 1]
        x, y, c = _place()
        for w in range(n):
            copy = pltpu.make_async_remote_copy(
                src_ref=src[w], dst_ref=lnd[w], send_sem=send.at[w], recv_sem=recv.at[w],
                device_id=(x, y, 1 - c), device_id_type=MESH)
            copy.wait_send()
            copy.wait_recv()

    bufs = list(sums) + list(lands)
    res = pl.pallas_call(
        body, name=name, out_shape=[pltpu.HBM(a.shape, a.dtype) for a in bufs],
        in_specs=[_HBM] * (2 * n) + [_SEM, _SEM, _ANY], out_specs=[_HBM] * (2 * n),
        input_output_aliases={i: i for i in range(2 * n)},
        compiler_params=pltpu.CompilerParams(has_side_effects=_EFFECT),
    )(*bufs, *sems, after)
    return res[:n], res[n:]


def _adamw_math(w, g, m, v):
    m = ADAM_B1 * m + (1.0 - ADAM_B1) * g
    v = ADAM_B2 * v + (1.0 - ADAM_B2) * (g * g)
    m_hat = m / (1.0 - ADAM_B1 ** ADAM_STEP)
    v_hat = v / (1.0 - ADAM_B2 ** ADAM_STEP)
    delta = -ADAM_LR * (m_hat / (jnp.sqrt(v_hat) + ADAM_EPS) + ADAM_WD * w)
    return delta, m, v


def _adamw_pair(mine, theirs, w, m, v, *, name, tr=128):
    R, C = w.shape
    tr = min(tr, R)

    def body(a_ref, b_ref, w_ref, m_ref, v_ref, g_ref, d_ref, nm_ref, nv_ref):
        g = a_ref[...] + b_ref[...]
        g_ref[...] = g
        d_ref[...], nm_ref[...], nv_ref[...] = _adamw_math(w_ref[...], g, m_ref[...], v_ref[...])

    shape = jax.ShapeDtypeStruct((R, C), F32)
    return pl.pallas_call(
        body, name=name, out_shape=(shape,) * 4, grid=(R // tr,),
        in_specs=[_rows(tr, C)] * 5, out_specs=(_rows(tr, C),) * 4,
        compiler_params=_cparams("parallel"))(mine, theirs, w, m, v)


def _all_reduce_small(packed):
    R = packed.shape[0]
    half = R // 2

    def body(x_ref, g_ref, sib_ref, pair_ref, land_ref, send_sems, recv_sems):
        x, y, c = _place()
        me = 2 * x + y
        sibling = (x, y, 1 - c)

        swap = pltpu.make_async_remote_copy(
            src_ref=x_ref, dst_ref=sib_ref, send_sem=send_sems.at[0], recv_sem=recv_sems.at[0],
            device_id=sibling, device_id_type=MESH)
        swap.start()
        swap.wait()
        mine, theirs = x_ref[...], sib_ref[...]
        south = c == 0
        pair_ref[...] = jnp.where(south, mine, theirs) + jnp.where(south, theirs, mine)

        land_ref[me] = pair_ref[c]
        for j, (px, py) in enumerate(_other_chips(x, y)):
            pltpu.make_async_remote_copy(
                src_ref=pair_ref.at[c], dst_ref=land_ref.at[me], send_sem=send_sems.at[1 + j],
                recv_sem=recv_sems.at[1 + j], device_id=(px, py, c), device_id_type=MESH).start()
        for j, (px, py) in enumerate(_other_chips(x, y)):
            arrival = pltpu.make_async_remote_copy(
                src_ref=pair_ref.at[c], dst_ref=land_ref.at[2 * px + py], send_sem=send_sems.at[1 + j],
                recv_sem=recv_sems.at[1 + j], device_id=(px, py, c), device_id_type=MESH)
            arrival.wait_recv()
            arrival.wait_send()
        total = land_ref[0]
        for k in range(1, N_CHIPS):
            total = total + land_ref[k]
        g_ref[c] = total

        give = pltpu.make_async_remote_copy(
            src_ref=g_ref.at[c], dst_ref=g_ref.at[c], send_sem=send_sems.at[4], recv_sem=recv_sems.at[4],
            device_id=sibling, device_id_type=MESH)
        give.start()
        take = pltpu.make_async_remote_copy(
            src_ref=g_ref.at[c], dst_ref=g_ref.at[1 - c], send_sem=send_sems.at[4], recv_sem=recv_sems.at[4],
            device_id=sibling, device_id_type=MESH)
        take.wait_recv()
        give.wait_send()

    vm = pl.BlockSpec(memory_space=pltpu.VMEM)
    return pl.pallas_call(
        body, name="all_reduce_small", out_shape=jax.ShapeDtypeStruct((2, half, LANES), F32),
        in_specs=[vm], out_specs=vm,
        scratch_shapes=[pltpu.VMEM((2, half, LANES), F32), pltpu.VMEM((2, half, LANES), F32),
                        pltpu.VMEM((N_CHIPS, half, LANES), F32),
                        pltpu.SemaphoreType.DMA((5,)), pltpu.SemaphoreType.DMA((5,))],
        compiler_params=pltpu.CompilerParams(vmem_limit_bytes=VMEM_LIMIT_BYTES),
    )(packed.reshape(2, half, LANES)).reshape(R, LANES)


def _adamw_small(g, w, m, v):
    R = g.shape[0]
    tr = PACK_ROWS

    def body(g_ref, w_ref, m_ref, v_ref, d_ref, nm_ref, nv_ref):
        d_ref[...], nm_ref[...], nv_ref[...] = _adamw_math(w_ref[...], g_ref[...], m_ref[...], v_ref[...])

    shape = jax.ShapeDtypeStruct((R, LANES), F32)
    return pl.pallas_call(
        body, name="adamw_small", out_shape=(shape,) * 3, grid=(R // tr,),
        in_specs=[_rows(tr, LANES)] * 4, out_specs=(_rows(tr, LANES),) * 3,
        compiler_params=_cparams("parallel"))(g, w, m, v)


def _pack(arrays):
    parts, layout = [], []
    for a in arrays:
        n = a.size
        rows = -(-n // (8 * LANES)) * 8
        flat = jnp.pad(a.reshape(-1).astype(F32), (0, rows * LANES - n))
        parts.append(flat.reshape(rows, LANES))
        layout.append((rows, n, a.shape))
    total = sum(r for r, _, _ in layout)
    parts.append(jnp.zeros((-total % PACK_ROWS, LANES), F32))
    return jnp.concatenate(parts, axis=0), layout


def _unpack(buf, layout):
    out, r0 = [], 0
    for rows, n, shape in layout:
        out.append(buf[r0:r0 + rows].reshape(-1)[:n].reshape(shape))
        r0 += rows
    return out


SMALL = ("mix_norm_pre", "lam_re", "lam_im", "log_dt", "ssm_b_re", "ssm_b_im", "ssm_c_re", "ssm_c_im",
         "ssm_d", "b_glu", "attn_out_norm", "ssm_out_norm", "mix_norm_post", "mlp_norm_pre",
         "mlp_norm_post", "ple_norm_pre", "ple_norm_post")
BIG = ("w_in", "w_glu", "w_out", "w_up", "w_down", "w_ple_gate", "w_ple_proj")
WEIGHTS = ("mix_norm_pre", "w_in", "lam_re", "lam_im", "log_dt", "ssm_b_re", "ssm_b_im", "ssm_c_re",
           "ssm_c_im", "ssm_d", "w_glu", "b_glu", "attn_out_norm", "ssm_out_norm", "w_out",
           "mix_norm_post", "mlp_norm_pre", "w_up", "w_down", "mlp_norm_post", "ple_norm_pre",
           "w_ple_gate", "w_ple_proj", "ple_norm_post")


def kernel(x, p, mix_norm_pre, w_in, lam_re, lam_im, log_dt, ssm_b_re, ssm_b_im, ssm_c_re, ssm_c_im, ssm_d, w_glu, b_glu, attn_out_norm, ssm_out_norm, w_out, mix_norm_post, mlp_norm_pre, w_up, w_down, mlp_norm_post, ple_norm_pre, w_ple_gate, w_ple_proj, ple_norm_post, loss_target, m_mix_norm_pre, m_w_in, m_lam_re, m_lam_im, m_log_dt, m_ssm_b_re, m_ssm_b_im, m_ssm_c_re, m_ssm_c_im, m_ssm_d, m_w_glu, m_b_glu, m_attn_out_norm, m_ssm_out_norm, m_w_out, m_mix_norm_post, m_mlp_norm_pre, m_w_up, m_w_down, m_mlp_norm_post, m_ple_norm_pre, m_w_ple_gate, m_w_ple_proj, m_ple_norm_post, v_mix_norm_pre, v_w_in, v_lam_re, v_lam_im, v_log_dt, v_ssm_b_re, v_ssm_b_im, v_ssm_c_re, v_ssm_c_im, v_ssm_d, v_w_glu, v_b_glu, v_attn_out_norm, v_ssm_out_norm, v_w_out, v_mix_norm_post, v_mlp_norm_pre, v_w_up, v_w_down, v_mlp_norm_post, v_ple_norm_pre, v_w_ple_gate, v_w_ple_proj, v_ple_norm_post):
    args = dict(locals())
    W = {n: args[n][0] for n in WEIGHTS}
    Mo = {n: args["m_" + n][0] for n in WEIGHTS}
    Vo = {n: args["v_" + n][0] for n in WEIGHTS}
    xs, ps, tgt = x[0], p[0, 0], loss_target[0]
    S, D = xs.shape
    SW = W["ssm_d"].shape[0]
    AW = W["attn_out_norm"].shape[0]
    heads = AW // HEAD_DIM
    G = SW // SSM_C
    nbk = SW // LANES
    assert W["w_in"].shape[1] * N_CHIPS == 3 * AW + SW and AW == SW

    row = lambda a: a.reshape(1, -1)

    ag_groups = (("w_in",), ("w_glu", "w_out"), ("w_up",), ("w_down", "w_ple_gate", "w_ple_proj"))
    ag_names = [n for g in ag_groups for n in g]
    def in_halves(a):
        return a.reshape(N_CHIPS, 2, a.shape[1] // 2, a.shape[2])

    def placed(n, after=None):
        return in_halves(_place_own(W[n], gather=True, name="ag_place_" + n, after=after))

    first_sems, first_land, _, first_token = _exchange_start([placed("w_in")], [], [[0]], name="ag_start_first")
    rest_sems, rest_land, _, ag_token = _exchange_start(
        [placed(n, first_token) for n in ag_names[1:]], [],
        [[ag_names.index(n) - 1 for n in g] for g in ag_groups[1:]], name="ag_start")
    ag_sems, ag_land = first_sems + rest_sems, list(first_land) + list(rest_land)

    def fetched(gi, after):
        return _exchange_wait([ag_land[ag_names.index(n)] for n in ag_groups[gi]], [], ag_sems[gi], after,
                              name=f"ag_wait_{gi}")

    def whole(gis, bufs):
        names = [n for gi in gis for n in ag_groups[gi]]
        return {n: a.reshape(N_CHIPS, -1, a.shape[-1]) for n, a in zip(names, bufs)}

    lr_e = W["lam_re"].reshape(nbk, 1, STATE_LANES)
    li_e = W["lam_im"].reshape(nbk, 1, STATE_LANES)
    ldt_e = jnp.repeat(W["log_dt"], SSM_P).reshape(nbk, 1, STATE_LANES)
    bre_e, bim_e = _expand_b(W["ssm_b_re"]), _expand_b(W["ssm_b_im"])
    cre_e, cim_e = _expand_c(W["ssm_c_re"]), _expand_c(W["ssm_c_im"])
    d_row = row(W["ssm_d"])

    hn1 = _norm_cast(xs, row(W["mix_norm_pre"]) + ag_token[0, 0], name="norm_in")
    w_in_f = whole([0], _pair_fill(fetched(0, hn1), name="ag_pair_0"))["w_in"]
    qkv_b = _proj_qkv(hn1, w_in_f)
    outs, lses = zip(*[_attn_fwd(qb, d, heads) for d, qb in zip(DILATIONS, qkv_b)])
    pair_a_sems, pair_a, pair_a_token = _pair_start(fetched(1, outs[-1]), name="ag_pair_start_a")
    u = _matmul(hn1, w_in_f, name="proj_u", b_shards=N_CHIPS, b_cols=(3 * AW, SW), after=pair_a_token)
    y1, y2b, st_r, st_i, ends_r, ends_i = _ssm_fwd(u, lr_e, li_e, ldt_e, bre_e, bim_e, cre_e, cim_e, d_row)
    pair_b_sems, pair_b, pair_b_token = _pair_start(fetched(2, y2b), name="ag_pair_start_b")
    full = whole([1], _pair_wait(pair_a, pair_a_sems, y2b, name="ag_pair_wait_a"))
    w_glu_f = full["w_glu"].reshape(SW, SW)
    w_out_f = full["w_out"].reshape(AW + SW, D)
    z = _matmul(y2b, w_glu_f, name="glu_z", after=pair_b_token)
    attn, lse_b, mixed = _mix_fwd(outs, lses, y1, z, row(W["b_glu"]), row(W["attn_out_norm"]), row(W["ssm_out_norm"]))
    mo = _matmul(mixed, w_out_f, name="mix_out")
    h1, hn2 = _res_norm(xs, mo, row(W["mix_norm_post"]), row(W["mlp_norm_pre"]), name="res_mix")
    w_up_f = whole([2], _pair_wait(pair_b, pair_b_sems, hn2, name="ag_pair_wait_b"))["w_up"]
    up, act = _matmul(hn2, w_up_f, name="mlp_up", b_shards=N_CHIPS, relu2=True, out_dtype=BF16)
    full = whole([3], _pair_fill(fetched(3, act), name="ag_pair_3"))
    w_down_f = full["w_down"].reshape(-1, D)
    w_pg_f = full["w_ple_gate"].reshape(D, D)
    w_pp_f = full["w_ple_proj"]
    ff = _matmul(act, w_down_f, name="mlp_down")
    h2, hn3 = _res_norm(h1, ff, row(W["mlp_norm_post"]), row(W["ple_norm_pre"]), name="res_mlp")
    gl = _matmul(hn3, w_pg_f, name="ple_gate")
    e = _matmul(ps.astype(BF16), w_pp_f, name="ple_proj", b_shards=N_CHIPS)

    dh3, dgl, de, loss_part, dg_ple_post = _final(h2, gl, e, row(W["ple_norm_post"]), tgt)
    gW = {}
    out_g, out_d, out_m, out_v = {}, {}, {}, {}

    def scatter_start(names, tag):
        parts = [gW[n] if gW[n].ndim == 3 else gW[n].reshape((N_CHIPS, -1, gW[n].shape[1])) for n in names]
        sems, land, src, token = _exchange_start(
            [_place_own(part, gather=False, name="rs_place_" + n) for n, part in zip(names, parts)], parts,
            [list(range(len(names)))], name=f"rs_start_{tag}")
        return (names, sems[0], land, src), token

    def scatter_sums(batches, after):
        names, sums = [], []
        for tag, (batch_names, sems, land, src) in batches:
            landed = _exchange_wait(land, src, sems, after, name=f"rs_wait_{tag}")
            names += batch_names
            sums += [_sum_partials(l, name="sum_" + n) for n, l in zip(batch_names, landed)]
        return names, sums

    def apply(names, sums, theirs):
        for n, a, b in zip(names, sums, theirs):
            out_g[n], out_d[n], out_m[n], out_v[n] = _adamw_pair(a, b, W[n], Mo[n], Vo[n], name="adamw_" + n)

    def swap_begin(batches, after, tag):
        names, sums = scatter_sums(batches, after)
        sems, sums, lands, token = _swap_start(sums, name=f"swap_start_{tag}")
        return (names, sems, sums, lands), token

    def swap_end(swap, after, tag):
        names, sems, sums, lands = swap
        sums, theirs = _swap_wait(sums, lands, sems, after, name=f"swap_wait_{tag}")
        apply(names, sums, theirs)

    def scatter_finish(batch, after, tag):
        names, sums = scatter_sums([(tag, batch)], after)
        apply(names, sums, _swap_with_sibling(sums, name=f"swap_{tag}"))

    gW["w_ple_proj"] = _matmul(ps.astype(BF16), de, name="d_w_ple_proj", ta=True, out_dtype=BF16, out_shards=N_CHIPS)
    gW["w_ple_gate"] = _matmul(hn3, dgl, name="d_w_ple_gate", ta=True, out_dtype=BF16)
    dhn3 = _matmul(dgl, w_pg_f, name="d_hn3", tb=True)
    dh2, dff, dg_ple_pre, dg_mlp_post = _bwd_res_norm(
        dh3, dhn3, h2, row(W["ple_norm_pre"]), ff, row(W["mlp_norm_post"]), name="bwd_res_mlp")
    gW["w_down"] = _matmul(act, dff, name="d_w_down", ta=True, out_dtype=BF16)
    batch1, token1 = scatter_start(("w_ple_proj", "w_ple_gate", "w_down"), 1)
    dup = _matmul(dff, w_down_f, name="d_up", tb=True, after=token1, relu2_of=up, out_dtype=BF16)
    gW["w_up"] = _matmul(hn2, dup, name="d_w_up", ta=True, out_dtype=BF16, out_shards=N_CHIPS)
    batch2, token2 = scatter_start(("w_up",), 2)
    dhn2 = _matmul(dup, w_up_f, name="d_hn2", tb=True, b_shards=N_CHIPS, after=token2)
    dh1, dmo, dg_mlp_pre, dg_mix_post = _bwd_res_norm(
        dh2, dhn2, h1, row(W["mlp_norm_pre"]), mo, row(W["mix_norm_post"]), name="bwd_res_mix")
    gW["w_out"] = _matmul(mixed, dmo, name="d_w_out", ta=True, out_dtype=BF16)
    dmixed = _matmul(dmo, w_out_f, name="d_mixed", tb=True)
    dattn_b, dd_b, dz, dy2a, dg_attn, dg_ssm, db_glu = _mix_bwd(
        dmixed, attn, y1, z, row(W["b_glu"]), row(W["attn_out_norm"]), row(W["ssm_out_norm"]))
    gW["w_glu"] = _matmul(y2b, dz, name="d_w_glu", ta=True, out_dtype=BF16)
    batch3, token3 = scatter_start(("w_out", "w_glu"), 3)
    dy2b = _matmul(dz, w_glu_f, name="d_y2", tb=True, after=token3)
    du, dar8, dai8, dcr_e, dci_e, dbr_e, dbi_e, dd8 = _ssm_bwd(
        u, y1, dy2a, dy2b, st_r, st_i, ends_r, ends_i, lr_e, li_e, ldt_e, bre_e, bim_e, cre_e, cim_e, d_row)
    swap_a, token_a = swap_begin([(1, batch1)], du, "a")
    dlr_e, dli_e, dldt_e, dbre_e, dbim_e = _ssm_param_bwd(dar8, dai8, dbr_e, dbi_e, lr_e, li_e, ldt_e, bre_e, bim_e)

    dqs, dks, dvs = zip(*[_attn_bwd(qb, da, l, dd_, d, heads, token_a)
                          for d, qb, da, l, dd_ in zip(DILATIONS, qkv_b, dattn_b, lse_b, dd_b)])
    dproj = _dproj_join(dqs, dks, dvs, du)
    swap_end(swap_a, dproj, "a")
    swap_b, token_b = swap_begin([(2, batch2), (3, batch3)], dproj, "b")
    gW["w_in"] = _matmul(hn1, dproj, name="d_w_in", ta=True, out_dtype=BF16, out_shards=N_CHIPS, after=token_b)
    batch4, token4 = scatter_start(("w_in",), 4)
    dhn1 = _matmul(dproj, w_in_f, name="d_hn1", tb=True, b_shards=N_CHIPS, after=token4)
    grad_x, dg_mix_pre = _bwd_first(dh1, dhn1, xs, row(W["mix_norm_pre"]))
    swap_end(swap_b, grad_x, "b")
    scatter_finish(batch4, grad_x, 4)

    small_g = {
        "mix_norm_pre": dg_mix_pre, "lam_re": dlr_e.reshape(G, SSM_P), "lam_im": dli_e.reshape(G, SSM_P),
        "log_dt": dldt_e.reshape(G, SSM_P)[:, 0], "ssm_b_re": _collapse_b(dbre_e), "ssm_b_im": _collapse_b(dbim_e),
        "ssm_c_re": _collapse_c(dcr_e), "ssm_c_im": _collapse_c(dci_e), "ssm_d": dd8.sum(axis=1).reshape(-1),
        "b_glu": db_glu, "attn_out_norm": dg_attn, "ssm_out_norm": dg_ssm, "mix_norm_post": dg_mix_post,
        "mlp_norm_pre": dg_mlp_pre, "mlp_norm_post": dg_mlp_post, "ple_norm_pre": dg_ple_pre,
        "ple_norm_post": dg_ple_post,
    }
    g_pack, layout = _pack([small_g[n].reshape(W[n].shape) for n in SMALL])
    w_pack, _ = _pack([W[n] for n in SMALL])
    m_pack, _ = _pack([Mo[n] for n in SMALL])
    v_pack, _ = _pack([Vo[n] for n in SMALL])
    g_sum = _all_reduce_small(g_pack)
    packed = (g_sum,) + tuple(_adamw_small(g_sum, w_pack, m_pack, v_pack))
    for dst, buf in zip((out_g, out_d, out_m, out_v), packed):
        dst.update(zip(SMALL, _unpack(buf, layout)))

    loss = lax.psum(loss_part[0, 0], ("x", "y", "c"))
    lead = lambda a: a[None]
    return (loss, grad_x[None],
            *[lead(out_g[n]) for n in WEIGHTS], *[lead(out_d[n]) for n in WEIGHTS],
            *[lead(out_m[n]) for n in WEIGHTS], *[lead(out_v[n]) for n in WEIGHTS])
```

```python
import functools
import math

import jax
import jax.numpy as jnp
from jax import lax
from jax.experimental import pallas as pl
from jax.experimental.pallas import tpu as pltpu

F32 = jnp.float32
BF16 = jnp.bfloat16
MESH = pl.DeviceIdType.MESH

RMS_EPS = 1e-6
NEG_INF = -1e30
HEAD_DIM = 128
BLK = 128
DILATIONS = (1, 4, 16)
ATTN_LOOKAHEAD = 3
SSM_C = 16
SSM_P = 64
LANES = 128
GROUPS_PER_BLOCK = LANES // SSM_C
STATE_LANES = GROUPS_PER_BLOCK * SSM_P
SSM_CHUNK = 1024
TILE = 8
ADAM_LR, ADAM_B1, ADAM_B2, ADAM_EPS, ADAM_WD, ADAM_STEP = 1e-3, 0.9, 0.999, 1e-8, 0.01, 10
VMEM_LIMIT_BYTES = 56 * 1024 * 1024
MATMUL_VMEM_BYTES = 44 * 1024 * 1024
N_CHIPS = 4
N_DEV = 8
PACK_ROWS = 256


def _cparams(*sem):
    return pltpu.CompilerParams(dimension_semantics=sem or None, vmem_limit_bytes=VMEM_LIMIT_BYTES)


def _rows(tr, w):
    return pl.BlockSpec((tr, w), lambda i: (i, 0))


def _vec(w):
    return pl.BlockSpec((1, w), lambda i: (0, 0))


def _sigmoid(x):
    return 1.0 / (1.0 + jnp.exp(-x))


def _gelu(x):
    c = math.sqrt(2.0 / math.pi)
    return 0.5 * x * (1.0 + jnp.tanh(c * (x + 0.044715 * x * x * x)))


def _gelu_grad(x):
    c = math.sqrt(2.0 / math.pi)
    th = jnp.tanh(c * (x + 0.044715 * x * x * x))
    return 0.5 * (1.0 + th) + 0.5 * x * (1.0 - th * th) * c * (1.0 + 3.0 * 0.044715 * x * x)


def _rms(x, g):
    r = lax.rsqrt(jnp.mean(x * x, axis=-1, keepdims=True) + RMS_EPS)
    return x * r * g


def _rms_bwd(dy, x, g):
    r = lax.rsqrt(jnp.mean(x * x, axis=-1, keepdims=True) + RMS_EPS)
    n = x * r
    dn = dy * g
    dx = r * (dn - n * jnp.mean(dn * n, axis=-1, keepdims=True))
    return dx, dy * n


def _colsum(a):
    return jnp.sum(a, axis=0, keepdims=True)


def _first(i):
    return i == 0


def _matmul(a, b, *, name, ta=False, tb=False, out_dtype=F32, b_shards=1, out_shards=1, b_cols=None,
            after=None, relu2=False, relu2_of=None, tm=1024, tn=2048, tk=2048):
    if ta:
        K, M = a.shape
    else:
        M, K = a.shape
    if b_shards > 1:
        rows, cols = b.shape[1], b.shape[2] * b_shards
    else:
        rows, cols = b.shape
    N, Kb = (rows, cols) if tb else (cols, rows)
    assert K == Kb, (a.shape, b.shape, ta, tb)
    col0 = 0
    if b_cols is not None:
        assert not tb
        col0, N = b_cols
    tm, tn, tk = min(tm, M), min(tn, N), min(tk, K)
    if b_shards > 1:
        shard_cols = cols // b_shards
        if tb:
            tk = min(tk, shard_cols)
        else:
            tn = min(tn, shard_cols)
    if out_shards > 1:
        tn = min(tn, N // out_shards)

    def vmem_bytes(tn_):
        out_bytes = jnp.dtype(out_dtype).itemsize + (2 if relu2 else 0)
        return (4 * (tm * tk + tk * tn_) + 2 * tm * tn_ * out_bytes
                + (2 * relu2_of.dtype.itemsize * tm * tn_ if relu2_of is not None else 0)
                + (4 * tm * tn_ if K > tk else 0))

    while vmem_bytes(tn) > MATMUL_VMEM_BYTES and tn > LANES and col0 % (tn // 2) == 0:
        tn //= 2
    assert M % tm == 0 and N % tn == 0 and K % tk == 0 and col0 % tn == 0
    nk = K // tk
    j0 = col0 // tn

    a_spec = (pl.BlockSpec((tk, tm), lambda i, j, k: (k, i)) if ta
              else pl.BlockSpec((tm, tk), lambda i, j, k: (i, k)))
    if b_shards > 1:
        if tb:
            per = shard_cols // tk
            b_spec = pl.BlockSpec((None, tn, tk), lambda i, j, k: (k // per, j, k % per))
        else:
            per = shard_cols // tn
            b_spec = pl.BlockSpec((None, tk, tn), lambda i, j, k: ((j + j0) // per, k, (j + j0) % per))
    else:
        b_spec = (pl.BlockSpec((tn, tk), lambda i, j, k: (j, k)) if tb
                  else pl.BlockSpec((tk, tn), lambda i, j, k: (k, j + j0)))
    if out_shards > 1:
        per_o = (N // out_shards) // tn
        out_shape = jax.ShapeDtypeStruct((out_shards, M, N // out_shards), out_dtype)
        out_spec = pl.BlockSpec((None, tm, tn), lambda i, j, k: (j // per_o, i, j % per_o))
    else:
        out_shape = jax.ShapeDtypeStruct((M, N), out_dtype)
        out_spec = pl.BlockSpec((tm, tn), lambda i, j, k: (i, j))
    dims = (((0 if ta else 1,), (1 if tb else 0,)), ((), ()))

    extra, extra_specs = [], []
    if relu2_of is not None:
        assert out_shards == 1 and relu2_of.shape == (M, N)
        extra.append(relu2_of)
        extra_specs.append(pl.BlockSpec((tm, tn), lambda i, j, k: (i, j)))
    if after is not None:
        extra.append(after)
        extra_specs.append(pl.BlockSpec(after.shape, lambda i, j, k: (0, 0)))
    n_in = 2 + len(extra)
    if relu2:
        assert out_shards == 1
        out_shape = (out_shape, jax.ShapeDtypeStruct((M, N), BF16))
        out_spec = (out_spec, out_spec)

    def finish(acc, refs):
        o_ref = refs[n_in]
        if relu2_of is not None:
            acc = acc * (2.0 * jnp.maximum(refs[2][...].astype(F32), 0.0))
        o_ref[...] = acc.astype(o_ref.dtype)
        if relu2:
            r = jnp.maximum(acc, 0.0)
            refs[n_in + 1][...] = (r * r).astype(BF16)

    def body(*refs):
        prod = lax.dot_general(refs[0][...], refs[1][...], dims, preferred_element_type=F32)
        if nk == 1:
            finish(prod, refs)
            return
        acc_ref = refs[-1]
        k = pl.program_id(2)

        @pl.when(k == 0)
        def _():
            acc_ref[...] = prod

        @pl.when(k > 0)
        def _():
            acc_ref[...] += prod

        @pl.when(k == nk - 1)
        def _():
            finish(acc_ref[...], refs)

    return pl.pallas_call(
        body, name=name, out_shape=out_shape, grid=(M // tm, N // tn, nk),
        in_specs=[a_spec, b_spec] + extra_specs, out_specs=out_spec,
        scratch_shapes=[pltpu.VMEM((tm, tn), F32)] if nk > 1 else [],
        compiler_params=_cparams("parallel", "parallel", "arbitrary"),
    )(a, b, *extra)


def _norm_cast(x, g, *, name, tr=512):
    S, D = x.shape
    tr = min(tr, S)

    def body(x_ref, g_ref, o_ref):
        o_ref[...] = _rms(x_ref[...], g_ref[...]).astype(BF16)

    return pl.pallas_call(
        body, name=name, out_shape=jax.ShapeDtypeStruct((S, D), BF16), grid=(S // tr,),
        in_specs=[_rows(tr, D), _vec(D)], out_specs=_rows(tr, D),
        compiler_params=_cparams("parallel"))(x, g)


def _res_norm(res, y, g_post, g_next, *, name, tr=512):
    S, D = res.shape
    tr = min(tr, S)

    def body(res_ref, y_ref, gp_ref, gn_ref, h_ref, hn_ref):
        h = res_ref[...] + _rms(y_ref[...], gp_ref[...])
        h_ref[...] = h
        hn_ref[...] = _rms(h, gn_ref[...]).astype(BF16)

    return pl.pallas_call(
        body, name=name,
        out_shape=(jax.ShapeDtypeStruct((S, D), F32), jax.ShapeDtypeStruct((S, D), BF16)),
        grid=(S // tr,), in_specs=[_rows(tr, D), _rows(tr, D), _vec(D), _vec(D)],
        out_specs=(_rows(tr, D), _rows(tr, D)), compiler_params=_cparams("parallel"))(res, y, g_post, g_next)


def _residue_spec(tr, d, w):
    return pl.BlockSpec((tr // d, d * w), lambda i: (i, 0))


def _residue_shape(S, d, w, dtype):
    return jax.ShapeDtypeStruct((S // d, d * w), dtype)


def _residue_scratch(rows, w):
    return pltpu.VMEM((w // LANES, rows, LANES), F32)


def _fill_strips(scr, val):
    for s in range(scr.shape[0]):
        scr[s] = val[:, s * LANES:(s + 1) * LANES]


def _strips_to_residues(scr, o_ref, d):
    strips, rows, _ = scr.shape
    for r in range(d):
        for s in range(strips):
            col = (r * strips + s) * LANES
            o_ref[:, col:col + LANES] = scr[s, pl.ds(r, rows // d, stride=d), :].astype(o_ref.dtype)


def _to_residues(scr, val, o_ref, d):
    if d == 1:
        o_ref[...] = val.astype(o_ref.dtype)
        return
    _fill_strips(scr, val)
    _strips_to_residues(scr, o_ref, d)


def _from_residues(scr, in_ref, d):
    if d == 1:
        return in_ref[...].astype(F32)
    strips, rows, _ = scr.shape
    for r in range(d):
        for s in range(strips):
            col = (r * strips + s) * LANES
            scr[s, pl.ds(r, rows // d, stride=d), :] = in_ref[:, col:col + LANES].astype(F32)
    return jnp.concatenate([scr[s] for s in range(strips)], axis=1)


def _spread_heads(packed, heads, width=HEAD_DIM):
    per = LANES // heads
    return jnp.concatenate([jnp.broadcast_to(packed[:, h * per:h * per + 1], (packed.shape[0], width))
                            for h in range(heads)], axis=1)


def _mix_fwd(os, ls, y1, z, b_glu, g_attn, g_ssm, *, tr=256):
    S, SW = y1.shape
    AW = os[0].shape[1] // DILATIONS[0]
    heads = AW // HEAD_DIM
    tr = min(tr, S)
    nd = len(DILATIONS)

    def body(*refs):
        o_refs, l_refs = refs[:nd], refs[nd:2 * nd]
        y_ref, z_ref, b_ref, ga_ref, gs_ref, attn_ref = refs[2 * nd:2 * nd + 6]
        lse_refs = refs[2 * nd + 6:3 * nd + 6]
        mixed_ref, scr, scr_p = refs[3 * nd + 6:]
        ls_ = [_from_residues(scr_p, l_refs[n], d) for n, d in enumerate(DILATIONS)]
        m = functools.reduce(jnp.maximum, ls_)
        es = [jnp.exp(l - m) for l in ls_]
        tot = functools.reduce(jnp.add, es)
        attn = functools.reduce(jnp.add, [_spread_heads(e / tot, heads) * _from_residues(scr, o_refs[n], d)
                                          for n, (e, d) in enumerate(zip(es, DILATIONS))])
        attn_ref[...] = attn
        lse = m + jnp.log(tot)
        for n, d in enumerate(DILATIONS):
            _to_residues(scr_p, lse, lse_refs[n], d)
        ssm = _gelu(y_ref[...]) * _sigmoid(z_ref[...] + b_ref[...])
        mixed_ref[:, :AW] = _rms(attn, ga_ref[...]).astype(BF16)
        mixed_ref[:, AW:] = _rms(ssm, gs_ref[...]).astype(BF16)

    res_o = [_residue_spec(tr, d, AW) for d in DILATIONS]
    res_l = [_residue_spec(tr, d, LANES) for d in DILATIONS]
    res = pl.pallas_call(
        body, name="mix_fwd",
        out_shape=([jax.ShapeDtypeStruct((S, AW), F32)] + [_residue_shape(S, d, LANES, F32) for d in DILATIONS]
                   + [jax.ShapeDtypeStruct((S, AW + SW), BF16)]),
        grid=(S // tr,),
        in_specs=res_o + res_l + [_rows(tr, SW), _rows(tr, SW), _vec(SW), _vec(AW), _vec(SW)],
        out_specs=[_rows(tr, AW)] + res_l + [_rows(tr, AW + SW)],
        scratch_shapes=[_residue_scratch(tr, AW), _residue_scratch(tr, LANES)],
        compiler_params=_cparams("parallel"))(*os, *ls, y1, z, b_glu, g_attn, g_ssm)
    return res[0], res[1:1 + nd], res[1 + nd]


def _final(h2, gl, e, g_post, target, *, tr=256):
    S, D = h2.shape
    tr = min(tr, S)

    def body(h_ref, gl_ref, e_ref, g_ref, t_ref, dh_ref, dgl_ref, de_ref, loss_ref, dg_ref):
        i = pl.program_id(0)
        gate = _sigmoid(gl_ref[...])
        e_ = e_ref[...]
        ge = gate * e_
        g = g_ref[...]
        diff = h_ref[...] + _rms(ge, g) - t_ref[...]
        dh = diff * (1.0 / D)
        dh_ref[...] = dh
        dge, dgrow = _rms_bwd(dh, ge, g)
        dgl_ref[...] = (dge * e_ * gate * (1.0 - gate)).astype(BF16)
        de_ref[...] = (dge * gate).astype(BF16)
        part = _colsum(0.5 * jnp.mean(diff * diff, axis=-1, keepdims=True))

        @pl.when(_first(i))
        def _():
            loss_ref[...] = jnp.zeros_like(loss_ref)
            dg_ref[...] = jnp.zeros_like(dg_ref)

        loss_ref[...] += part + jnp.zeros((1, LANES), F32)
        dg_ref[...] += _colsum(dgrow)

    return pl.pallas_call(
        body, name="final_fwd_bwd",
        out_shape=(jax.ShapeDtypeStruct((S, D), F32), jax.ShapeDtypeStruct((S, D), BF16),
                   jax.ShapeDtypeStruct((S, D), BF16), jax.ShapeDtypeStruct((1, LANES), F32),
                   jax.ShapeDtypeStruct((1, D), F32)),
        grid=(S // tr,),
        in_specs=[_rows(tr, D), _rows(tr, D), _rows(tr, D), _vec(D), _rows(tr, D)],
        out_specs=(_rows(tr, D), _rows(tr, D), _rows(tr, D), _vec(LANES), _vec(D)),
        compiler_params=_cparams("arbitrary"))(h2, gl, e, g_post, target)


def _bwd_res_norm(dh_out, dhn, h, g_next, y, g_post, *, name, tr=256):
    S, D = h.shape
    tr = min(tr, S)

    def body(dho_ref, dhn_ref, h_ref, gn_ref, y_ref, gp_ref, dh_ref, dy_ref, dgn_ref, dgp_ref):
        i = pl.program_id(0)
        dx, dgn_rows = _rms_bwd(dhn_ref[...], h_ref[...], gn_ref[...])
        dh = dho_ref[...] + dx
        dh_ref[...] = dh
        dy, dgp_rows = _rms_bwd(dh, y_ref[...], gp_ref[...])
        dy_ref[...] = dy.astype(BF16)

        @pl.when(_first(i))
        def _():
            dgn_ref[...] = jnp.zeros_like(dgn_ref)
            dgp_ref[...] = jnp.zeros_like(dgp_ref)

        dgn_ref[...] += _colsum(dgn_rows)
        dgp_ref[...] += _colsum(dgp_rows)

    return pl.pallas_call(
        body, name=name,
        out_shape=(jax.ShapeDtypeStruct((S, D), F32), jax.ShapeDtypeStruct((S, D), BF16),
                   jax.ShapeDtypeStruct((1, D), F32), jax.ShapeDtypeStruct((1, D), F32)),
        grid=(S // tr,),
        in_specs=[_rows(tr, D), _rows(tr, D), _rows(tr, D), _vec(D), _rows(tr, D), _vec(D)],
        out_specs=(_rows(tr, D), _rows(tr, D), _vec(D), _vec(D)),
        compiler_params=_cparams("arbitrary"))(dh_out, dhn, h, g_next, y, g_post)


def _bwd_first(dh1, dhn1, x, g1, *, tr=256):
    S, D = x.shape
    tr = min(tr, S)

    def body(dh_ref, dhn_ref, x_ref, g_ref, dx_ref, dg_ref):
        i = pl.program_id(0)
        dx, dg_rows = _rms_bwd(dhn_ref[...], x_ref[...], g_ref[...])
        dx_ref[...] = dh_ref[...] + dx

        @pl.when(_first(i))
        def _():
            dg_ref[...] = jnp.zeros_like(dg_ref)

        dg_ref[...] += _colsum(dg_rows)

    return pl.pallas_call(
        body, name="bwd_first",
        out_shape=(jax.ShapeDtypeStruct((S, D), F32), jax.ShapeDtypeStruct((1, D), F32)),
        grid=(S // tr,), in_specs=[_rows(tr, D), _rows(tr, D), _rows(tr, D), _vec(D)],
        out_specs=(_rows(tr, D), _vec(D)), compiler_params=_cparams("arbitrary"))(dh1, dhn1, x, g1)


def _mix_bwd(dmixed, attn, y1, z, b_glu, g_attn, g_ssm, *, tr=256):
    S, AW = attn.shape
    SW = y1.shape[1]
    tr = min(tr, S)
    heads = AW // HEAD_DIM
    nd = len(DILATIONS)

    def body(*refs):
        dm_ref, a_ref, y_ref, z_ref, b_ref, ga_ref, gs_ref = refs[:7]
        da_refs, dd_refs = refs[7:7 + nd], refs[7 + nd:7 + 2 * nd]
        dz_ref, dy2_ref, dga_ref, dgs_ref, db_ref, scr, scr_p, dd_scr = refs[7 + 2 * nd:]
        i = pl.program_id(0)
        attn_ = a_ref[...]
        dattn, dga_rows = _rms_bwd(dm_ref[:, :AW], attn_, ga_ref[...])
        prod = dattn * attn_
        per = LANES // heads
        for h in range(heads):
            total = jnp.sum(prod[:, h * HEAD_DIM:(h + 1) * HEAD_DIM], axis=-1, keepdims=True)
            dd_scr[:, h * per:(h + 1) * per] = jnp.broadcast_to(total, (tr, per))
        for n, d in enumerate(DILATIONS):
            _to_residues(scr, dattn, da_refs[n], d)
            _to_residues(scr_p, dd_scr[...], dd_refs[n], d)
        y2 = _gelu(y_ref[...])
        gate = _sigmoid(z_ref[...] + b_ref[...])
        dssm, dgs_rows = _rms_bwd(dm_ref[:, AW:], y2 * gate, gs_ref[...])
        dz = dssm * y2 * gate * (1.0 - gate)
        dz_ref[...] = dz.astype(BF16)
        dy2_ref[...] = dssm * gate

        @pl.when(_first(i))
        def _():
            dga_ref[...] = jnp.zeros_like(dga_ref)
            dgs_ref[...] = jnp.zeros_like(dgs_ref)
            db_ref[...] = jnp.zeros_like(db_ref)

        dga_ref[...] += _colsum(dga_rows)
        dgs_ref[...] += _colsum(dgs_rows)
        db_ref[...] += _colsum(dz)

    res_a = [_residue_spec(tr, d, AW) for d in DILATIONS]
    res_d = [_residue_spec(tr, d, LANES) for d in DILATIONS]
    res = pl.pallas_call(
        body, name="mix_bwd",
        out_shape=([_residue_shape(S, d, AW, BF16) for d in DILATIONS]
                   + [_residue_shape(S, d, LANES, F32) for d in DILATIONS]
                   + [jax.ShapeDtypeStruct((S, SW), BF16), jax.ShapeDtypeStruct((S, SW), F32),
                      jax.ShapeDtypeStruct((1, AW), F32), jax.ShapeDtypeStruct((1, SW), F32),
                      jax.ShapeDtypeStruct((1, SW), F32)]),
        grid=(S // tr,),
        in_specs=[_rows(tr, AW + SW), _rows(tr, AW), _rows(tr, SW), _rows(tr, SW), _vec(SW), _vec(AW), _vec(SW)],
        out_specs=res_a + res_d + [_rows(tr, SW), _rows(tr, SW), _vec(AW), _vec(SW), _vec(SW)],
        scratch_shapes=[_residue_scratch(tr, AW), _residue_scratch(tr, LANES), pltpu.VMEM((tr, LANES), F32)],
        compiler_params=_cparams("arbitrary"))(dmixed, attn, y1, z, b_glu, g_attn, g_ssm)
    return (res[:nd], res[nd:2 * nd]) + tuple(res[2 * nd:])


def _attn_mask2(i):
    row = lax.broadcasted_iota(jnp.int32, (BLK, 2 * BLK), 0)
    col = lax.broadcasted_iota(jnp.int32, (BLK, 2 * BLK), 1)
    return jnp.logical_and(col >= row, jnp.logical_and(col <= row + BLK, jnp.logical_or(col >= BLK, i > 0)))


_NT = (((1,), (1,)), ((), ()))
_TN = (((0,), (0,)), ((), ()))


def _attn_in_specs(width, block_of):
    def at(part, prev):
        def index(r, i):
            blk = block_of(i)
            return (part, jnp.maximum(blk - 1, 0) if prev else blk, r)
        return pl.BlockSpec((None, BLK, width), index)
    return [at(0, False), at(1, False), at(1, True), at(2, False), at(2, True)]


def _proj_qkv(hn, w_in_f, *, tm=1024):
    S, D = hn.shape
    AW = w_in_f.shape[2]
    tm = min(tm, S)

    def body(a_ref, b_ref, *rest):
        o_refs, scr = rest[:-1], rest[-1]
        prod = jnp.dot(a_ref[...], b_ref[...], preferred_element_type=F32)
        _fill_strips(scr, prod)
        for o_ref, d in zip(o_refs, DILATIONS):
            if d == 1:
                o_ref[...] = prod.astype(BF16)
            else:
                _strips_to_residues(scr, o_ref, d)

    return pl.pallas_call(
        body, name="proj_qkv",
        out_shape=[jax.ShapeDtypeStruct((3, S // d, d * AW), BF16) for d in DILATIONS], grid=(S // tm, 3),
        in_specs=[pl.BlockSpec((tm, D), lambda i, j: (i, 0)), pl.BlockSpec((None, D, AW), lambda i, j: (j, 0, 0))],
        out_specs=[pl.BlockSpec((None, tm // d, d * AW), lambda i, j: (j, i, 0)) for d in DILATIONS],
        scratch_shapes=[_residue_scratch(tm, AW)],
        compiler_params=_cparams("parallel", "parallel"))(hn, w_in_f)


def _attn_fwd(qkv, d, heads):
    M = qkv.shape[1]
    nb = M // BLK
    width = heads * HEAD_DIM
    per = LANES // heads
    scale = 1.0 / math.sqrt(HEAD_DIM)

    def body(q_ref, kc_ref, kp_ref, vc_ref, vp_ref, o_ref, l_ref):
        mask = _attn_mask2(pl.program_id(1))
        ones = jnp.ones((2 * BLK, HEAD_DIM), BF16)

        def scores(h):
            sl = slice(h * HEAD_DIM, (h + 1) * HEAD_DIM)
            k2 = jnp.concatenate([kp_ref[:, sl], kc_ref[:, sl]], axis=0)
            return lax.dot_general(q_ref[:, sl], k2, _NT, preferred_element_type=F32)

        ahead = [scores(h) for h in range(min(ATTN_LOOKAHEAD, heads))]
        for h in range(heads):
            sl = slice(h * HEAD_DIM, (h + 1) * HEAD_DIM)
            s = jnp.where(mask, ahead.pop(0) * scale, NEG_INF)
            if h + ATTN_LOOKAHEAD < heads:
                ahead.append(scores(h + ATTN_LOOKAHEAD))
            v2 = jnp.concatenate([vp_ref[:, sl], vc_ref[:, sl]], axis=0)
            m = jnp.max(jnp.maximum(s[:, :BLK], s[:, BLK:]), axis=-1, keepdims=True)
            p = jnp.exp(s - m).astype(BF16)
            tot = jnp.dot(p, ones, preferred_element_type=F32)
            o_ref[:, sl] = (jnp.dot(p, v2, preferred_element_type=F32) / tot).astype(BF16)
            l_ref[:, h * per:(h + 1) * per] = m + jnp.log(tot[:, :per])

    return pl.pallas_call(
        body, name=f"attn_fwd_d{d}",
        out_shape=(jax.ShapeDtypeStruct((M, d * width), BF16), jax.ShapeDtypeStruct((M, d * LANES), F32)),
        grid=(d, nb), in_specs=_attn_in_specs(width, lambda i: i),
        out_specs=(pl.BlockSpec((BLK, width), lambda r, i: (i, r)), pl.BlockSpec((BLK, LANES), lambda r, i: (i, r))),
        compiler_params=_cparams("parallel", "parallel"))(qkv, qkv, qkv, qkv, qkv)


def _attn_bwd(qkv, dattn, lse, dd, d, heads, after):
    M = qkv.shape[1]
    nb = M // BLK
    width = heads * HEAD_DIM
    per = LANES // heads
    scale = 1.0 / math.sqrt(HEAD_DIM)

    def block_of(i):
        return nb - 1 - i

    def body(q_ref, kc_ref, kp_ref, vc_ref, vp_ref, da_ref, l_ref, dd_ref, after_ref,
             dq_ref, dk_ref, dv_ref, dk_carry, dv_carry):
        @pl.when(pl.program_id(1) == 0)
        def _():
            dk_carry[...] = jnp.zeros_like(dk_carry)
            dv_carry[...] = jnp.zeros_like(dv_carry)

        mask = _attn_mask2(block_of(pl.program_id(1)))

        def products(h):
            sl = slice(h * HEAD_DIM, (h + 1) * HEAD_DIM)
            k2 = jnp.concatenate([kp_ref[:, sl], kc_ref[:, sl]], axis=0)
            v2 = jnp.concatenate([vp_ref[:, sl], vc_ref[:, sl]], axis=0)
            return (lax.dot_general(q_ref[:, sl], k2, _NT, preferred_element_type=F32),
                    lax.dot_general(da_ref[:, sl], v2, _NT, preferred_element_type=F32), k2)

        ahead = [products(h) for h in range(min(ATTN_LOOKAHEAD, heads))]
        for h in range(heads):
            sl = slice(h * HEAD_DIM, (h + 1) * HEAD_DIM)
            qk, dp, k2 = ahead.pop(0)
            if h + ATTN_LOOKAHEAD < heads:
                ahead.append(products(h + ATTN_LOOKAHEAD))
            q, da = q_ref[:, sl], da_ref[:, sl]
            lse_ = jnp.broadcast_to(l_ref[:, h * per:h * per + 1], (BLK, 2 * BLK))
            dd_ = jnp.broadcast_to(dd_ref[:, h * per:h * per + 1], (BLK, 2 * BLK))
            p = jnp.where(mask, jnp.exp(jnp.where(mask, qk * scale, NEG_INF) - lse_), 0.0)
            ds = (p * (dp - dd_) * scale).astype(BF16)
            dq_ref[:, sl] = jnp.dot(ds, k2, preferred_element_type=F32).astype(BF16)
            dk2 = lax.dot_general(ds, q, _TN, preferred_element_type=F32)
            dv2 = lax.dot_general(p.astype(BF16), da, _TN, preferred_element_type=F32)
            dk_ref[:, sl] = (dk2[BLK:] + dk_carry[:, sl]).astype(BF16)
            dv_ref[:, sl] = (dv2[BLK:] + dv_carry[:, sl]).astype(BF16)
            dk_carry[:, sl] = dk2[:BLK]
            dv_carry[:, sl] = dv2[:BLK]

    blk = pl.BlockSpec((BLK, width), lambda r, i: (block_of(i), r))
    packed = pl.BlockSpec((BLK, LANES), lambda r, i: (block_of(i), r))
    shape = jax.ShapeDtypeStruct((M, d * width), BF16)
    return pl.pallas_call(
        body, name=f"attn_bwd_d{d}", out_shape=(shape,) * 3, grid=(d, nb),
        in_specs=(_attn_in_specs(width, block_of) + [blk, packed, packed]
                  + [pl.BlockSpec(after.shape, lambda r, i: (0, 0))]), out_specs=(blk,) * 3,
        scratch_shapes=[pltpu.VMEM((BLK, width), F32), pltpu.VMEM((BLK, width), F32)],
        compiler_params=_cparams("arbitrary", "arbitrary"))(qkv, qkv, qkv, qkv, qkv, dattn, lse, dd, after)


def _dproj_join(dqs, dks, dvs, du, *, tr=512):
    S, SW = du.shape
    AW = dqs[0].shape[1]
    tr = min(tr, S)
    nd = len(DILATIONS)

    def body(*refs):
        du_ref, out_ref, scr = refs[3 * nd:]
        for part in range(3):
            total = functools.reduce(jnp.add, [_from_residues(scr, refs[part * nd + n], d)
                                               for n, d in enumerate(DILATIONS)])
            out_ref[:, part * AW:(part + 1) * AW] = total.astype(BF16)
        out_ref[:, 3 * AW:] = du_ref[...].astype(BF16)

    return pl.pallas_call(
        body, name="dproj_join", out_shape=jax.ShapeDtypeStruct((S, 3 * AW + SW), BF16), grid=(S // tr,),
        in_specs=[_residue_spec(tr, d, AW) for d in DILATIONS] * 3 + [_rows(tr, SW)],
        out_specs=_rows(tr, 3 * AW + SW), scratch_shapes=[_residue_scratch(tr, AW)],
        compiler_params=_cparams("parallel"))(*dqs, *dks, *dvs, du)


def _ssm_disc(lr, li, ldt):
    dt = jnp.exp(ldt)
    mag = jnp.exp(lr * dt)
    ar = mag * jnp.cos(li * dt)
    ai = mag * jnp.sin(li * dt)
    nr = ar - 1.0
    den = lr * lr + li * li
    return ar, ai, (nr * lr + ai * li) / den, (ai * lr - nr * li) / den


def _ssm_tile_powers(lr, li, ldt, reverse):
    t = lax.broadcasted_iota(jnp.int32, (TILE, 1), 0)
    n = (TILE - t if reverse else t + 1).astype(F32)
    dt = jnp.exp(ldt)
    mag = jnp.exp(n * (lr * dt))
    ang = n * (li * dt)
    return mag * jnp.cos(ang), mag * jnp.sin(ang) * (-1.0 if reverse else 1.0)


def _cmul(ar, ai, br, bi):
    return ar * br - ai * bi, ar * bi + ai * br


LOG_STEPS = 3


def _ssm_step_tables(ar, ai, reverse):
    sub = lax.broadcasted_iota(jnp.int32, (TILE, ar.shape[-1]), 0)
    tables = []
    for k in range(LOG_STEPS):
        keep = sub < TILE - (1 << k) if reverse else sub >= (1 << k)
        tables.append((jnp.where(keep, ar, 0.0), jnp.where(keep, ai, 0.0)))
        ar, ai = _cmul(ar, ai, ar, ai)
    return tables


def _scan(xr, xi, steps, pr, pi, cr, ci, reverse):
    T, lanes = xr.shape
    n = T // TILE
    xr, xi = xr.reshape(n, TILE, lanes), xi.reshape(n, TILE, lanes)
    for k, (mr, mi) in enumerate(steps):
        shift = TILE - (1 << k) if reverse else 1 << k
        qr, qi = _cmul(mr, mi, pltpu.roll(xr, shift, 1), pltpu.roll(xi, shift, 1))
        xr, xi = xr + qr, xi + qi
    out_r, out_i = [None] * n, [None] * n
    edge = 0 if reverse else TILE - 1
    for j in (reversed(range(n)) if reverse else range(n)):
        er, ei = _cmul(pr, pi, cr, ci)
        sr, si = xr[j] + er, xi[j] + ei
        out_r[j], out_i[j] = sr, si
        cr, ci = sr[edge:edge + 1], si[edge:edge + 1]
    return jnp.concatenate(out_r, axis=0), jnp.concatenate(out_i, axis=0), cr, ci


def _ssm_specs(T, nch, rev):
    def t_of(c):
        return nch - 1 - c if rev else c
    tok = pl.BlockSpec((T, LANES), lambda j, c: (t_of(c), j))
    par = pl.BlockSpec((None, 1, STATE_LANES), lambda j, c: (j, 0, 0))
    bmat = pl.BlockSpec((None, LANES, STATE_LANES), lambda j, c: (j, 0, 0))
    cmat = pl.BlockSpec((None, STATE_LANES, LANES), lambda j, c: (j, 0, 0))
    dvec = pl.BlockSpec((1, LANES), lambda j, c: (0, j))
    return tok, par, bmat, cmat, dvec


def _ssm_fwd(u, lr_e, li_e, ldt_e, bre_e, bim_e, cre_e, cim_e, d_skip):
    S, SW = u.shape
    T = min(SSM_CHUNK, S)
    nch, nbk = S // T, SW // LANES
    tok, par, bmat, cmat, dvec = _ssm_specs(T, nch, False)
    state_spec = pl.BlockSpec((T, STATE_LANES), lambda j, c: (c, j))
    carry_spec = pl.BlockSpec((None, 1, STATE_LANES), lambda j, c: (c, 0, j))

    def body(u_ref, lr_ref, li_ref, ldt_ref, bre_ref, bim_ref, cre_ref, cim_ref, d_ref,
             y_ref, y2_ref, sr_ref, si_ref, er_ref, ei_ref, bbr, bbi, steps, pw, carry):
        c = pl.program_id(1)

        @pl.when(c == 0)
        def _():
            lr, li, ldt = lr_ref[...], li_ref[...], ldt_ref[...]
            ar, ai, kr, ki = _ssm_disc(lr, li, ldt)
            for k, (mr, mi) in enumerate(_ssm_step_tables(ar, ai, False)):
                steps[0, k], steps[1, k] = mr, mi
            bbr[...] = (kr * bre_ref[...] - ki * bim_ref[...]).astype(BF16)
            bbi[...] = (kr * bim_ref[...] + ki * bre_ref[...]).astype(BF16)
            pw[0], pw[1] = _ssm_tile_powers(lr, li, ldt, False)
            carry[...] = jnp.zeros_like(carry)

        u_ = u_ref[...]
        ub = u_.astype(BF16)
        sr, si, cr, ci = _scan(jnp.dot(ub, bbr[...], preferred_element_type=F32),
                               jnp.dot(ub, bbi[...], preferred_element_type=F32),
                               [(steps[0, k], steps[1, k]) for k in range(LOG_STEPS)],
                               pw[0], pw[1], carry[0], carry[1], False)
        carry[0], carry[1] = cr, ci
        er_ref[...], ei_ref[...] = cr, ci
        sr_ref[...], si_ref[...] = sr, si
        y0 = (jnp.dot(sr.astype(BF16), cre_ref[...].astype(BF16), preferred_element_type=F32)
              - jnp.dot(si.astype(BF16), cim_ref[...].astype(BF16), preferred_element_type=F32))
        y1 = y0 + d_ref[...] * u_
        y_ref[...] = y1
        y2_ref[...] = _gelu(y1).astype(BF16)

    states = jax.ShapeDtypeStruct((S, nbk * STATE_LANES), F32)
    ends = jax.ShapeDtypeStruct((nch, 1, nbk * STATE_LANES), F32)
    return pl.pallas_call(
        body, name="ssm_fwd",
        out_shape=(jax.ShapeDtypeStruct((S, SW), F32), jax.ShapeDtypeStruct((S, SW), BF16), states, states, ends, ends),
        grid=(nbk, nch), in_specs=[tok, par, par, par, bmat, bmat, cmat, cmat, dvec],
        out_specs=(tok, tok, state_spec, state_spec, carry_spec, carry_spec),
        scratch_shapes=[pltpu.VMEM((LANES, STATE_LANES), BF16), pltpu.VMEM((LANES, STATE_LANES), BF16),
                        pltpu.VMEM((2, LOG_STEPS, TILE, STATE_LANES), F32), pltpu.VMEM((2, TILE, STATE_LANES), F32),
                        pltpu.VMEM((2, 1, STATE_LANES), F32)],
        compiler_params=_cparams("arbitrary", "arbitrary"),
    )(u, lr_e, li_e, ldt_e, bre_e, bim_e, cre_e, cim_e, d_skip)


def _ssm_bwd(u, y1, dy2a, dy2b, st_r, st_i, ends_r, ends_i, lr_e, li_e, ldt_e, bre_e, bim_e, cre_e, cim_e, d_skip):
    S, SW = u.shape
    T = min(SSM_CHUNK, S)
    nch, nbk = S // T, SW // LANES
    tok, par, bmat, cmat, dvec = _ssm_specs(T, nch, True)
    state_spec = pl.BlockSpec((T, STATE_LANES), lambda j, c: (nch - 1 - c, j))
    prev_spec = pl.BlockSpec((None, 1, STATE_LANES), lambda j, c: (jnp.maximum(nch - 2 - c, 0), 0, j))
    acc8 = pl.BlockSpec((None, 8, STATE_LANES), lambda j, c: (j, 0, 0))
    dd8 = pl.BlockSpec((None, 8, LANES), lambda j, c: (j, 0, 0))

    def body(u_ref, y_ref, da_ref, db_ref, sr_ref, si_ref, pr_ref, pi_ref, lr_ref, li_ref, ldt_ref,
             bre_ref, bim_ref, cre_ref, cim_ref, d_ref,
             du_ref, dar_ref, dai_ref, dcr_ref, dci_ref, dbr_ref, dbi_ref, ddk_ref,
             bbr, bbi, steps, pw, carry):
        c = pl.program_id(1)

        @pl.when(c == 0)
        def _():
            lr, li, ldt = lr_ref[...], li_ref[...], ldt_ref[...]
            ar, ai, kr, ki = _ssm_disc(lr, li, ldt)
            for k, (mr, mi) in enumerate(_ssm_step_tables(ar, -ai, True)):
                steps[0, k], steps[1, k] = mr, mi
            bbr[...] = (kr * bre_ref[...] - ki * bim_ref[...]).astype(BF16)
            bbi[...] = (kr * bim_ref[...] + ki * bre_ref[...]).astype(BF16)
            pw[0], pw[1] = _ssm_tile_powers(lr, li, ldt, True)
            carry[...] = jnp.zeros_like(carry)
            for ref in (dar_ref, dai_ref, dcr_ref, dci_ref, dbr_ref, dbi_ref, ddk_ref):
                ref[...] = jnp.zeros_like(ref)

        u_ = u_ref[...]
        ub = u_.astype(BF16)
        dy1 = (da_ref[...] + db_ref[...]) * _gelu_grad(y_ref[...])
        dyb = dy1.astype(BF16)

        sr, si = sr_ref[...], si_ref[...]
        has_prev = c < nch - 1
        s0r = jnp.where(has_prev, pr_ref[...], 0.0)
        s0i = jnp.where(has_prev, pi_ref[...], 0.0)

        cre_b, cim_b = cre_ref[...].astype(BF16), cim_ref[...].astype(BF16)
        gr, gi, cr, ci = _scan(lax.dot_general(dyb, cre_b, _NT, preferred_element_type=F32),
                               -lax.dot_general(dyb, cim_b, _NT, preferred_element_type=F32),
                               [(steps[0, k], steps[1, k]) for k in range(LOG_STEPS)],
                               pw[0], pw[1], carry[0], carry[1], True)
        carry[0], carry[1] = cr, ci

        row = lax.broadcasted_iota(jnp.int32, (T, STATE_LANES), 0)
        spr = jnp.where(row == 0, s0r, pltpu.roll(sr, 1, 0))
        spi = jnp.where(row == 0, s0i, pltpu.roll(si, 1, 0))

        def fold(a):
            return jnp.sum(a.reshape(T // 8, 8, a.shape[-1]), axis=0)

        dar_ref[...] += fold(gr * spr + gi * spi)
        dai_ref[...] += fold(gi * spr - gr * spi)
        srb, sib, grb, gib = sr.astype(BF16), si.astype(BF16), gr.astype(BF16), gi.astype(BF16)
        dcr_ref[...] += lax.dot_general(srb, dyb, _TN, preferred_element_type=F32)
        dci_ref[...] -= lax.dot_general(sib, dyb, _TN, preferred_element_type=F32)
        dbr_ref[...] += lax.dot_general(ub, grb, _TN, preferred_element_type=F32)
        dbi_ref[...] += lax.dot_general(ub, gib, _TN, preferred_element_type=F32)
        du_ref[...] = (lax.dot_general(grb, bbr[...], _NT, preferred_element_type=F32)
                       + lax.dot_general(gib, bbi[...], _NT, preferred_element_type=F32)
                       + dy1 * d_ref[...])
        ddk_ref[...] += fold(dy1 * u_)

    return pl.pallas_call(
        body, name="ssm_bwd",
        out_shape=(jax.ShapeDtypeStruct((S, SW), F32),
                   jax.ShapeDtypeStruct((nbk, 8, STATE_LANES), F32), jax.ShapeDtypeStruct((nbk, 8, STATE_LANES), F32),
                   jax.ShapeDtypeStruct((nbk, STATE_LANES, LANES), F32), jax.ShapeDtypeStruct((nbk, STATE_LANES, LANES), F32),
                   jax.ShapeDtypeStruct((nbk, LANES, STATE_LANES), F32), jax.ShapeDtypeStruct((nbk, LANES, STATE_LANES), F32),
                   jax.ShapeDtypeStruct((nbk, 8, LANES), F32)),
        grid=(nbk, nch),
        in_specs=[tok, tok, tok, tok, state_spec, state_spec, prev_spec, prev_spec, par, par, par,
                  bmat, bmat, cmat, cmat, dvec],
        out_specs=(tok, acc8, acc8, cmat, cmat, bmat, bmat, dd8),
        scratch_shapes=[pltpu.VMEM((LANES, STATE_LANES), BF16), pltpu.VMEM((LANES, STATE_LANES), BF16),
                        pltpu.VMEM((2, LOG_STEPS, TILE, STATE_LANES), F32), pltpu.VMEM((2, TILE, STATE_LANES), F32),
                        pltpu.VMEM((2, 1, STATE_LANES), F32)],
        compiler_params=_cparams("arbitrary", "arbitrary"),
    )(u, y1, dy2a, dy2b, st_r, st_i, ends_r, ends_i, lr_e, li_e, ldt_e, bre_e, bim_e, cre_e, cim_e, d_skip)


def _ssm_param_bwd(dar8, dai8, dbr_e, dbi_e, lr_e, li_e, ldt_e, bre_e, bim_e):
    nbk = lr_e.shape[0]
    par = pl.BlockSpec((None, 1, STATE_LANES), lambda j: (j, 0, 0))
    acc8 = pl.BlockSpec((None, 8, STATE_LANES), lambda j: (j, 0, 0))
    bmat = pl.BlockSpec((None, LANES, STATE_LANES), lambda j: (j, 0, 0))

    def body(dar_ref, dai_ref, dbr_ref, dbi_ref, lr_ref, li_ref, ldt_ref, bre_ref, bim_ref,
             dlr_ref, dli_ref, dldt_ref, dbre_ref, dbim_ref):
        lr, li, ldt = lr_ref[...], li_ref[...], ldt_ref[...]
        (ar, ai, kr, ki), vjp = jax.vjp(_ssm_disc, lr, li, ldt)
        dbr, dbi, bre, bim = dbr_ref[...], dbi_ref[...], bre_ref[...], bim_ref[...]
        dbre_ref[...] = kr * dbr + ki * dbi
        dbim_ref[...] = kr * dbi - ki * dbr
        dkr = _colsum(dbr * bre + dbi * bim)
        dki = _colsum(dbi * bre - dbr * bim)
        dlr, dli, dldt = vjp((_colsum(dar_ref[...]), _colsum(dai_ref[...]), dkr, dki))
        dlr_ref[...] = dlr
        dli_ref[...] = dli
        tot = jnp.broadcast_to(dldt, (8, STATE_LANES))
        sh = 1
        while sh < SSM_P:
            tot = tot + pltpu.roll(tot, STATE_LANES - sh, 1)
            sh *= 2
        dldt_ref[...] = tot[:1]

    vec = jax.ShapeDtypeStruct((nbk, 1, STATE_LANES), F32)
    mat = jax.ShapeDtypeStruct((nbk, LANES, STATE_LANES), F32)
    return pl.pallas_call(
        body, name="ssm_param_bwd", out_shape=(vec, vec, vec, mat, mat), grid=(nbk,),
        in_specs=[acc8, acc8, bmat, bmat, par, par, par, bmat, bmat],
        out_specs=(par, par, par, bmat, bmat), compiler_params=_cparams("parallel"),
    )(dar8, dai8, dbr_e, dbi_e, lr_e, li_e, ldt_e, bre_e, bim_e)


def _expand_b(b):
    G = b.shape[0]
    bt = b.transpose(0, 2, 1).reshape(G // GROUPS_PER_BLOCK, GROUPS_PER_BLOCK, SSM_C, SSM_P)
    eye = jnp.eye(GROUPS_PER_BLOCK, dtype=b.dtype)
    return (bt[:, :, :, None, :] * eye[None, :, None, :, None]).reshape(G // GROUPS_PER_BLOCK, LANES, STATE_LANES)


def _collapse_b(be):
    nbk = be.shape[0]
    eye = jnp.eye(GROUPS_PER_BLOCK, dtype=be.dtype)
    d5 = be.reshape(nbk, GROUPS_PER_BLOCK, SSM_C, GROUPS_PER_BLOCK, SSM_P)
    d4 = (d5 * eye[None, :, None, :, None]).sum(axis=3)
    return d4.transpose(0, 1, 3, 2).reshape(nbk * GROUPS_PER_BLOCK, SSM_P, SSM_C)


def _expand_c(cm):
    G = cm.shape[0]
    ct = cm.transpose(0, 2, 1).reshape(G // GROUPS_PER_BLOCK, GROUPS_PER_BLOCK, SSM_P, SSM_C)
    eye = jnp.eye(GROUPS_PER_BLOCK, dtype=cm.dtype)
    return (ct[:, :, :, None, :] * eye[None, :, None, :, None]).reshape(G // GROUPS_PER_BLOCK, STATE_LANES, LANES)


def _collapse_c(ce):
    nbk = ce.shape[0]
    eye = jnp.eye(GROUPS_PER_BLOCK, dtype=ce.dtype)
    d5 = ce.reshape(nbk, GROUPS_PER_BLOCK, SSM_P, GROUPS_PER_BLOCK, SSM_C)
    d4 = (d5 * eye[None, :, None, :, None]).sum(axis=3)
    return d4.transpose(0, 1, 3, 2).reshape(nbk * GROUPS_PER_BLOCK, SSM_C, SSM_P)


def _place():
    x, y, c = lax.axis_index("x"), lax.axis_index("y"), lax.axis_index("c")
    return x, y, c


def _other_chips(x, y):
    return [(1 - x, y), (x, 1 - y), (1 - x, 1 - y)]


_ANY = pl.BlockSpec(memory_space=pl.ANY)


_HBM = pl.BlockSpec(memory_space=pltpu.HBM)
_SEM = pl.BlockSpec(memory_space=pltpu.SEMAPHORE)
_EFFECT = pltpu.SideEffectType.DATAFLOW_SIDE_EFFECTING
_TOKEN = jax.ShapeDtypeStruct((8, LANES), F32)


def _hbm(a):
    return pltpu.with_memory_space_constraint(a, pltpu.HBM)


def _place_own(src, *, gather, name, after=None, tr=512):
    R, C = src.shape[-2:]
    tr = min(tr, R)
    x, y, _ = _place()
    me = (2 * x + y).astype(jnp.int32).reshape(1)
    extra = [] if after is None else [after]

    def body(me_ref, s_ref, *rest):
        rest[-1][...] = s_ref[...].astype(BF16)

    own = pl.BlockSpec((None, tr, C), lambda i, me_ref: (me_ref[0], i, 0))
    grid_spec = pltpu.PrefetchScalarGridSpec(
        num_scalar_prefetch=1, grid=(R // tr,),
        in_specs=([pl.BlockSpec((tr, C), lambda i, me_ref: (i, 0)) if gather else own]
                  + [pl.BlockSpec(a.shape, lambda i, me_ref: (0, 0)) for a in extra]), out_specs=own)
    return pl.pallas_call(
        body, name=name, grid_spec=grid_spec, out_shape=jax.ShapeDtypeStruct((N_CHIPS, R, C), BF16),
        compiler_params=_cparams("parallel"))(me, src, *extra)


def _exchange_copy(src_slot, land_slot, send, recv, k, j, peer, c):
    return pltpu.make_async_remote_copy(
        src_ref=src_slot, dst_ref=land_slot, send_sem=send.at[3 * k + j], recv_sem=recv.at[3 * k + j],
        device_id=(peer[0], peer[1], c), device_id_type=MESH)


def _exchange_start(lands, srcs, groups, *, name):
    n, ng = len(lands), len(groups)
    bufs = list(lands) + list(srcs)
    nb = len(bufs)

    def body(*refs):
        lnd, src, sems = refs[:n], refs[n:nb], refs[nb:nb + 2 * ng]
        token = refs[2 * nb + 2 * ng]
        x, y, c = _place()
        me = 2 * x + y
        for gi, group in enumerate(groups):
            for k, w in enumerate(group):
                for j, peer in enumerate(_other_chips(x, y)):
                    if src:
                        sent, dst = src[w].at[2 * peer[0] + peer[1]], lnd[w].at[me]
                    else:
                        sent = dst = lnd[w].at[me, c]
                    _exchange_copy(sent, dst, sems[2 * gi], sems[2 * gi + 1], k, j, peer, c).start()
        token[...] = jnp.zeros_like(token)

    sem_shapes = [pltpu.SemaphoreType.DMA((3 * len(g),)) for g in groups for _ in range(2)]
    res = pl.pallas_call(
        body, name=name,
        out_shape=sem_shapes + [pltpu.HBM(a.shape, a.dtype) for a in bufs] + [_TOKEN],
        in_specs=[_HBM] * nb,
        out_specs=[_SEM] * (2 * ng) + [_HBM] * nb + [pl.BlockSpec(memory_space=pltpu.VMEM)],
        input_output_aliases={i: 2 * ng + i for i in range(nb)},
        compiler_params=pltpu.CompilerParams(has_side_effects=_EFFECT),
    )(*[_hbm(a) for a in bufs])
    sems = [(res[2 * gi], res[2 * gi + 1]) for gi in range(ng)]
    return sems, res[2 * ng:2 * ng + n], res[2 * ng + n:2 * ng + nb], res[-1]


def _exchange_wait(lands, srcs, sems, after, *, name):
    n = len(lands)
    bufs = list(lands) + list(srcs)
    nb = len(bufs)
    send_sems, recv_sems = sems

    def body(*refs):
        lnd, src, send, recv = refs[:n], refs[n:nb], refs[nb], refs[nb + 1]
        x, y, c = _place()
        for k in range(n):
            for j, peer in enumerate(_other_chips(x, y)):
                slot = 2 * peer[0] + peer[1]
                if src:
                    copy = _exchange_copy(src[k].at[slot], lnd[k].at[slot], send, recv, k, j, peer, c)
                else:
                    copy = _exchange_copy(lnd[k].at[slot, c], lnd[k].at[slot, c], send, recv, k, j, peer, c)
                copy.wait_send()
                copy.wait_recv()

    res = pl.pallas_call(
        body, name=name, out_shape=[pltpu.HBM(a.shape, a.dtype) for a in bufs],
        in_specs=[_HBM] * nb + [_SEM, _SEM, _ANY], out_specs=[_HBM] * nb,
        input_output_aliases={i: i for i in range(nb)},
        compiler_params=pltpu.CompilerParams(has_side_effects=_EFFECT),
    )(*bufs, send_sems, recv_sems, after)
    return res[:n]


def _pair_fill(lands, *, name):
    n = len(lands)

    def body(*refs):
        ins, outs, send, recv = refs[:n], refs[n:2 * n], refs[2 * n], refs[2 * n + 1]
        x, y, c = _place()
        for w in range(n):
            for j, (px, py) in enumerate(_other_chips(x, y)):
                slot = 2 * px + py
                pltpu.make_async_remote_copy(
                    src_ref=ins[w].at[slot, c], dst_ref=outs[w].at[slot, c], send_sem=send.at[3 * w + j],
                    recv_sem=recv.at[3 * w + j], device_id=(x, y, 1 - c), device_id_type=MESH).start()
        for w in range(n):
            for j, (px, py) in enumerate(_other_chips(x, y)):
                slot = 2 * px + py
                arrival = pltpu.make_async_remote_copy(
                    src_ref=ins[w].at[slot, c], dst_ref=outs[w].at[slot, 1 - c], send_sem=send.at[3 * w + j],
                    recv_sem=recv.at[3 * w + j], device_id=(x, y, 1 - c), device_id_type=MESH)
                arrival.wait_recv()
                arrival.wait_send()

    return pl.pallas_call(
        body, name=name, out_shape=[jax.ShapeDtypeStruct(a.shape, a.dtype) for a in lands],
        in_specs=[_ANY] * n, out_specs=[_ANY] * n, input_output_aliases={i: i for i in range(n)},
        scratch_shapes=[pltpu.SemaphoreType.DMA((3 * n,)), pltpu.SemaphoreType.DMA((3 * n,))],
    )(*lands)


def _pair_copy(src, dst, send, recv, w, j, sibling):
    return pltpu.make_async_remote_copy(
        src_ref=src, dst_ref=dst, send_sem=send.at[3 * w + j], recv_sem=recv.at[3 * w + j],
        device_id=sibling, device_id_type=MESH)


def _pair_start(lands, *, name):
    n = len(lands)

    def body(*refs):
        bufs, send, recv, token = refs[:n], refs[n], refs[n + 1], refs[2 * n + 2]
        x, y, c = _place()
        for w in range(n):
            for j, (px, py) in enumerate(_other_chips(x, y)):
                half = bufs[w].at[2 * px + py, c]
                _pair_copy(half, half, send, recv, w, j, (x, y, 1 - c)).start()
        token[...] = jnp.zeros_like(token)

    res = pl.pallas_call(
        body, name=name,
        out_shape=[pltpu.SemaphoreType.DMA((3 * n,))] * 2 + [pltpu.HBM(a.shape, a.dtype) for a in lands] + [_TOKEN],
        in_specs=[_HBM] * n, out_specs=[_SEM, _SEM] + [_HBM] * n + [pl.BlockSpec(memory_space=pltpu.VMEM)],
        input_output_aliases={i: 2 + i for i in range(n)},
        compiler_params=pltpu.CompilerParams(has_side_effects=_EFFECT),
    )(*[_hbm(a) for a in lands])
    return (res[0], res[1]), res[2:2 + n], res[-1]


def _pair_wait(lands, sems, after, *, name):
    n = len(lands)

    def body(*refs):
        bufs, send, recv = refs[:n], refs[n], refs[n + 1]
        x, y, c = _place()
        for w in range(n):
            for j, (px, py) in enumerate(_other_chips(x, y)):
                slot = 2 * px + py
                copy = _pair_copy(bufs[w].at[slot, c], bufs[w].at[slot, 1 - c], send, recv, w, j, (x, y, 1 - c))
                copy.wait_send()
                copy.wait_recv()

    return pl.pallas_call(
        body, name=name, out_shape=[pltpu.HBM(a.shape, a.dtype) for a in lands],
        in_specs=[_HBM] * n + [_SEM, _SEM, _ANY], out_specs=[_HBM] * n,
        input_output_aliases={i: i for i in range(n)},
        compiler_params=pltpu.CompilerParams(has_side_effects=_EFFECT),
    )(*lands, *sems, after)


def _sum_partials(land, *, name, tr=256):
    _, R, C = land.shape
    tr = min(tr, R)

    def body(l_ref, o_ref):
        acc = l_ref[0].astype(F32)
        for k in range(1, N_CHIPS):
            acc = acc + l_ref[k].astype(F32)
        o_ref[...] = acc

    return pl.pallas_call(
        body, name=name, out_shape=jax.ShapeDtypeStruct((R, C), F32), grid=(R // tr,),
        in_specs=[pl.BlockSpec((N_CHIPS, tr, C), lambda i: (0, i, 0))], out_specs=_rows(tr, C),
        compiler_params=_cparams("parallel"))(land)


def _swap_with_sibling(sums, *, name):
    n = len(sums)

    def body(*refs):
        ins, outs = refs[:n], refs[n:2 * n]
        send_sems, recv_sems = refs[2 * n:]
        x, y, c = _place()
        copies = [pltpu.make_async_remote_copy(
            src_ref=ins[w], dst_ref=outs[w], send_sem=send_sems.at[w], recv_sem=recv_sems.at[w],
            device_id=(x, y, 1 - c), device_id_type=MESH) for w in range(n)]
        for cp in copies:
            cp.start()
        for cp in copies:
            cp.wait_recv()
            cp.wait_send()

    return pl.pallas_call(
        body, name=name,
        out_shape=[jax.ShapeDtypeStruct(s.shape, s.dtype) for s in sums],
        in_specs=[_ANY] * n, out_specs=[_ANY] * n,
        scratch_shapes=[pltpu.SemaphoreType.DMA((n,)), pltpu.SemaphoreType.DMA((n,))],
    )(*sums)


def _swap_start(sums, *, name):
    n = len(sums)
    bufs = list(sums) + [lax.empty(s.shape, s.dtype) for s in sums]

    def body(*refs):
        src, lnd, send, recv, token = refs[:n], refs[n:2 * n], refs[2 * n], refs[2 * n + 1], refs[4 * n + 2]
        x, y, c = _place()
        for w in range(n):
            pltpu.make_async_remote_copy(
                src_ref=src[w], dst_ref=lnd[w], send_sem=send.at[w], recv_sem=recv.at[w],
                device_id=(x, y, 1 - c), device_id_type=MESH).start()
        token[...] = jnp.zeros_like(token)

    res = pl.pallas_call(
        body, name=name,
        out_shape=[pltpu.SemaphoreType.DMA((n,))] * 2 + [pltpu.HBM(a.shape, a.dtype) for a in bufs] + [_TOKEN],
        in_specs=[_HBM] * (2 * n),
        out_specs=[_SEM, _SEM] + [_HBM] * (2 * n) + [pl.BlockSpec(memory_space=pltpu.VMEM)],
        input_output_aliases={i: 2 + i for i in range(2 * n)},
        compiler_params=pltpu.CompilerParams(has_side_effects=_EFFECT),
    )(*[_hbm(a) for a in bufs])
    return (res[0], res[1]), res[2:2 + n], res[2 + n:2 + 2 * n], res[-1]


def _swap_wait(sums, lands, sems, after, *, name):
    n = len(sums)

    def body(*refs):
        src, lnd, send, recv = refs[:n], refs[n:2 * n], refs[2 * n], refs[2 * n + 1]
        x, y, c = _place()
        for w in range(n):
            copy = pltpu.make_async_remote_copy(
                src_ref=src[w], dst_ref=lnd[w], send_sem=send.at[w], recv_sem=recv.at[w],
                device_id=(x, y, 1 - c), device_id_type=MESH)
            copy.wait_send()
            copy.wait_recv()

    bufs = list(sums) + list(lands)
    res = pl.pallas_call(
        body, name=name, out_shape=[pltpu.HBM(a.shape, a.dtype) for a in bufs],
        in_specs=[_HBM] * (2 * n) + [_SEM, _SEM, _ANY], out_specs=[_HBM] * (2 * n),
        input_output_aliases={i: i for i in range(2 * n)},
        compiler_params=pltpu.CompilerParams(has_side_effects=_EFFECT),
    )(*bufs, *sems, after)
    return res[:n], res[n:]


def _adamw_math(w, g, m, v):
    m = ADAM_B1 * m + (1.0 - ADAM_B1) * g
    v = ADAM_B2 * v + (1.0 - ADAM_B2) * (g * g)
    m_hat = m / (1.0 - ADAM_B1 ** ADAM_STEP)
    v_hat = v / (1.0 - ADAM_B2 ** ADAM_STEP)
    delta = -ADAM_LR * (m_hat / (jnp.sqrt(v_hat) + ADAM_EPS) + ADAM_WD * w)
    return delta, m, v


def _adamw_pair(mine, theirs, w, m, v, *, name, tr=128):
    R, C = w.shape
    tr = min(tr, R)

    def body(a_ref, b_ref, w_ref, m_ref, v_ref, g_ref, d_ref, nm_ref, nv_ref):
        g = a_ref[...] + b_ref[...]
        g_ref[...] = g
        d_ref[...], nm_ref[...], nv_ref[...] = _adamw_math(w_ref[...], g, m_ref[...], v_ref[...])

    shape = jax.ShapeDtypeStruct((R, C), F32)
    return pl.pallas_call(
        body, name=name, out_shape=(shape,) * 4, grid=(R // tr,),
        in_specs=[_rows(tr, C)] * 5, out_specs=(_rows(tr, C),) * 4,
        compiler_params=_cparams("parallel"))(mine, theirs, w, m, v)


def _all_reduce_small(packed):
    R = packed.shape[0]
    half = R // 2

    def body(x_ref, g_ref, sib_ref, pair_ref, land_ref, send_sems, recv_sems):
        x, y, c = _place()
        me = 2 * x + y
        sibling = (x, y, 1 - c)

        swap = pltpu.make_async_remote_copy(
            src_ref=x_ref, dst_ref=sib_ref, send_sem=send_sems.at[0], recv_sem=recv_sems.at[0],
            device_id=sibling, device_id_type=MESH)
        swap.start()
        swap.wait()
        mine, theirs = x_ref[...], sib_ref[...]
        south = c == 0
        pair_ref[...] = jnp.where(south, mine, theirs) + jnp.where(south, theirs, mine)

        land_ref[me] = pair_ref[c]
        for j, (px, py) in enumerate(_other_chips(x, y)):
            pltpu.make_async_remote_copy(
                src_ref=pair_ref.at[c], dst_ref=land_ref.at[me], send_sem=send_sems.at[1 + j],
                recv_sem=recv_sems.at[1 + j], device_id=(px, py, c), device_id_type=MESH).start()
        for j, (px, py) in enumerate(_other_chips(x, y)):
            arrival = pltpu.make_async_remote_copy(
                src_ref=pair_ref.at[c], dst_ref=land_ref.at[2 * px + py], send_sem=send_sems.at[1 + j],
                recv_sem=recv_sems.at[1 + j], device_id=(px, py, c), device_id_type=MESH)
            arrival.wait_recv()
            arrival.wait_send()
        total = land_ref[0]
        for k in range(1, N_CHIPS):
            total = total + land_ref[k]
        g_ref[c] = total

        give = pltpu.make_async_remote_copy(
            src_ref=g_ref.at[c], dst_ref=g_ref.at[c], send_sem=send_sems.at[4], recv_sem=recv_sems.at[4],
            device_id=sibling, device_id_type=MESH)
        give.start()
        take = pltpu.make_async_remote_copy(
            src_ref=g_ref.at[c], dst_ref=g_ref.at[1 - c], send_sem=send_sems.at[4], recv_sem=recv_sems.at[4],
            device_id=sibling, device_id_type=MESH)
        take.wait_recv()
        give.wait_send()

    vm = pl.BlockSpec(memory_space=pltpu.VMEM)
    return pl.pallas_call(
        body, name="all_reduce_small", out_shape=jax.ShapeDtypeStruct((2, half, LANES), F32),
        in_specs=[vm], out_specs=vm,
        scratch_shapes=[pltpu.VMEM((2, half, LANES), F32), pltpu.VMEM((2, half, LANES), F32),
                        pltpu.VMEM((N_CHIPS, half, LANES), F32),
                        pltpu.SemaphoreType.DMA((5,)), pltpu.SemaphoreType.DMA((5,))],
        compiler_params=pltpu.CompilerParams(vmem_limit_bytes=VMEM_LIMIT_BYTES),
    )(packed.reshape(2, half, LANES)).reshape(R, LANES)


def _adamw_small(g, w, m, v):
    R = g.shape[0]
    tr = PACK_ROWS

    def body(g_ref, w_ref, m_ref, v_ref, d_ref, nm_ref, nv_ref):
        d_ref[...], nm_ref[...], nv_ref[...] = _adamw_math(w_ref[...], g_ref[...], m_ref[...], v_ref[...])

    shape = jax.ShapeDtypeStruct((R, LANES), F32)
    return pl.pallas_call(
        body, name="adamw_small", out_shape=(shape,) * 3, grid=(R // tr,),
        in_specs=[_rows(tr, LANES)] * 4, out_specs=(_rows(tr, LANES),) * 3,
        compiler_params=_cparams("parallel"))(g, w, m, v)


def _pack(arrays):
    parts, layout = [], []
    for a in arrays:
        n = a.size
        rows = -(-n // (8 * LANES)) * 8
        flat = jnp.pad(a.reshape(-1).astype(F32), (0, rows * LANES - n))
        parts.append(flat.reshape(rows, LANES))
        layout.append((rows, n, a.shape))
    total = sum(r for r, _, _ in layout)
    parts.append(jnp.zeros((-total % PACK_ROWS, LANES), F32))
    return jnp.concatenate(parts, axis=0), layout


def _unpack(buf, layout):
    out, r0 = [], 0
    for rows, n, shape in layout:
        out.append(buf[r0:r0 + rows].reshape(-1)[:n].reshape(shape))
        r0 += rows
    return out


SMALL = ("mix_norm_pre", "lam_re", "lam_im", "log_dt", "ssm_b_re", "ssm_b_im", "ssm_c_re", "ssm_c_im",
         "ssm_d", "b_glu", "attn_out_norm", "ssm_out_norm", "mix_norm_post", "mlp_norm_pre",
         "mlp_norm_post", "ple_norm_pre", "ple_norm_post")
BIG = ("w_in", "w_glu", "w_out", "w_up", "w_down", "w_ple_gate", "w_ple_proj")
WEIGHTS = ("mix_norm_pre", "w_in", "lam_re", "lam_im", "log_dt", "ssm_b_re", "ssm_b_im", "ssm_c_re",
           "ssm_c_im", "ssm_d", "w_glu", "b_glu", "attn_out_norm", "ssm_out_norm", "w_out",
           "mix_norm_post", "mlp_norm_pre", "w_up", "w_down", "mlp_norm_post", "ple_norm_pre",
           "w_ple_gate", "w_ple_proj", "ple_norm_post")


def kernel(x, p, mix_norm_pre, w_in, lam_re, lam_im, log_dt, ssm_b_re, ssm_b_im, ssm_c_re, ssm_c_im, ssm_d, w_glu, b_glu, attn_out_norm, ssm_out_norm, w_out, mix_norm_post, mlp_norm_pre, w_up, w_down, mlp_norm_post, ple_norm_pre, w_ple_gate, w_ple_proj, ple_norm_post, loss_target, m_mix_norm_pre, m_w_in, m_lam_re, m_lam_im, m_log_dt, m_ssm_b_re, m_ssm_b_im, m_ssm_c_re, m_ssm_c_im, m_ssm_d, m_w_glu, m_b_glu, m_attn_out_norm, m_ssm_out_norm, m_w_out, m_mix_norm_post, m_mlp_norm_pre, m_w_up, m_w_down, m_mlp_norm_post, m_ple_norm_pre, m_w_ple_gate, m_w_ple_proj, m_ple_norm_post, v_mix_norm_pre, v_w_in, v_lam_re, v_lam_im, v_log_dt, v_ssm_b_re, v_ssm_b_im, v_ssm_c_re, v_ssm_c_im, v_ssm_d, v_w_glu, v_b_glu, v_attn_out_norm, v_ssm_out_norm, v_w_out, v_mix_norm_post, v_mlp_norm_pre, v_w_up, v_w_down, v_mlp_norm_post, v_ple_norm_pre, v_w_ple_gate, v_w_ple_proj, v_ple_norm_post):
    args = dict(locals())
    W = {n: args[n][0] for n in WEIGHTS}
    Mo = {n: args["m_" + n][0] for n in WEIGHTS}
    Vo = {n: args["v_" + n][0] for n in WEIGHTS}
    xs, ps, tgt = x[0], p[0, 0], loss_target[0]
    S, D = xs.shape
    SW = W["ssm_d"].shape[0]
    AW = W["attn_out_norm"].shape[0]
    heads = AW // HEAD_DIM
    G = SW // SSM_C
    nbk = SW // LANES
    assert W["w_in"].shape[1] * N_CHIPS == 3 * AW + SW and AW == SW

    row = lambda a: a.reshape(1, -1)

    ag_groups = (("w_in",), ("w_glu", "w_out"), ("w_up",), ("w_down", "w_ple_gate", "w_ple_proj"))
    ag_names = [n for g in ag_groups for n in g]
    def in_halves(a):
        return a.reshape(N_CHIPS, 2, a.shape[1] // 2, a.shape[2])

    def placed(n, after=None):
        return in_halves(_place_own(W[n], gather=True, name="ag_place_" + n, after=after))

    first_sems, first_land, _, first_token = _exchange_start([placed("w_in")], [], [[0]], name="ag_start_first")
    rest_sems, rest_land, _, ag_token = _exchange_start(
        [placed(n, first_token) for n in ag_names[1:]], [],
        [[ag_names.index(n) - 1 for n in g] for g in ag_groups[1:]], name="ag_start")
    ag_sems, ag_land = first_sems + rest_sems, list(first_land) + list(rest_land)

    def fetched(gi, after):
        return _exchange_wait([ag_land[ag_names.index(n)] for n in ag_groups[gi]], [], ag_sems[gi], after,
                              name=f"ag_wait_{gi}")

    def whole(gis, bufs):
        names = [n for gi in gis for n in ag_groups[gi]]
        return {n: a.reshape(N_CHIPS, -1, a.shape[-1]) for n, a in zip(names, bufs)}

    lr_e = W["lam_re"].reshape(nbk, 1, STATE_LANES)
    li_e = W["lam_im"].reshape(nbk, 1, STATE_LANES)
    ldt_e = jnp.repeat(W["log_dt"], SSM_P).reshape(nbk, 1, STATE_LANES)
    bre_e, bim_e = _expand_b(W["ssm_b_re"]), _expand_b(W["ssm_b_im"])
    cre_e, cim_e = _expand_c(W["ssm_c_re"]), _expand_c(W["ssm_c_im"])
    d_row = row(W["ssm_d"])

    hn1 = _norm_cast(xs, row(W["mix_norm_pre"]) + ag_token[0, 0], name="norm_in")
    w_in_f = whole([0], _pair_fill(fetched(0, hn1), name="ag_pair_0"))["w_in"]
    qkv_b = _proj_qkv(hn1, w_in_f)
    outs, lses = zip(*[_attn_fwd(qb, d, heads) for d, qb in zip(DILATIONS, qkv_b)])
    pair_a_sems, pair_a, pair_a_token = _pair_start(fetched(1, outs[-1]), name="ag_pair_start_a")
    u = _matmul(hn1, w_in_f, name="proj_u", b_shards=N_CHIPS, b_cols=(3 * AW, SW), after=pair_a_token)
    y1, y2b, st_r, st_i, ends_r, ends_i = _ssm_fwd(u, lr_e, li_e, ldt_e, bre_e, bim_e, cre_e, cim_e, d_row)
    pair_b_sems, pair_b, pair_b_token = _pair_start(fetched(2, y2b), name="ag_pair_start_b")
    full = whole([1], _pair_wait(pair_a, pair_a_sems, y2b, name="ag_pair_wait_a"))
    w_glu_f = full["w_glu"].reshape(SW, SW)
    w_out_f = full["w_out"].reshape(AW + SW, D)
    z = _matmul(y2b, w_glu_f, name="glu_z", after=pair_b_token)
    attn, lse_b, mixed = _mix_fwd(outs, lses, y1, z, row(W["b_glu"]), row(W["attn_out_norm"]), row(W["ssm_out_norm"]))
    mo = _matmul(mixed, w_out_f, name="mix_out")
    h1, hn2 = _res_norm(xs, mo, row(W["mix_norm_post"]), row(W["mlp_norm_pre"]), name="res_mix")
    w_up_f = whole([2], _pair_wait(pair_b, pair_b_sems, hn2, name="ag_pair_wait_b"))["w_up"]
    up, act = _matmul(hn2, w_up_f, name="mlp_up", b_shards=N_CHIPS, relu2=True, out_dtype=BF16)
    full = whole([3], _pair_fill(fetched(3, act), name="ag_pair_3"))
    w_down_f = full["w_down"].reshape(-1, D)
    w_pg_f = full["w_ple_gate"].reshape(D, D)
    w_pp_f = full["w_ple_proj"]
    ff = _matmul(act, w_down_f, name="mlp_down")
    h2, hn3 = _res_norm(h1, ff, row(W["mlp_norm_post"]), row(W["ple_norm_pre"]), name="res_mlp")
    gl = _matmul(hn3, w_pg_f, name="ple_gate")
    e = _matmul(ps.astype(BF16), w_pp_f, name="ple_proj", b_shards=N_CHIPS)

    dh3, dgl, de, loss_part, dg_ple_post = _final(h2, gl, e, row(W["ple_norm_post"]), tgt)
    gW = {}
    out_g, out_d, out_m, out_v = {}, {}, {}, {}

    def scatter_start(names, tag):
        parts = [gW[n] if gW[n].ndim == 3 else gW[n].reshape((N_CHIPS, -1, gW[n].shape[1])) for n in names]
        sems, land, src, token = _exchange_start(
            [_place_own(part, gather=False, name="rs_place_" + n) for n, part in zip(names, parts)], parts,
            [list(range(len(names)))], name=f"rs_start_{tag}")
        return (names, sems[0], land, src), token

    def scatter_sums(batches, after):
        names, sums = [], []
        for tag, (batch_names, sems, land, src) in batches:
            landed = _exchange_wait(land, src, sems, after, name=f"rs_wait_{tag}")
            names += batch_names
            sums += [_sum_partials(l, name="sum_" + n) for n, l in zip(batch_names, landed)]
        return names, sums

    def apply(names, sums, theirs):
        for n, a, b in zip(names, sums, theirs):
            out_g[n], out_d[n], out_m[n], out_v[n] = _adamw_pair(a, b, W[n], Mo[n], Vo[n], name="adamw_" + n)

    def swap_begin(batches, after, tag):
        names, sums = scatter_sums(batches, after)
        sems, sums, lands, token = _swap_start(sums, name=f"swap_start_{tag}")
        return (names, sems, sums, lands), token

    def swap_end(swap, after, tag):
        names, sems, sums, lands = swap
        sums, theirs = _swap_wait(sums, lands, sems, after, name=f"swap_wait_{tag}")
        apply(names, sums, theirs)

    def scatter_finish(batch, after, tag):
        names, sums = scatter_sums([(tag, batch)], after)
        apply(names, sums, _swap_with_sibling(sums, name=f"swap_{tag}"))

    gW["w_ple_proj"] = _matmul(ps.astype(BF16), de, name="d_w_ple_proj", ta=True, out_dtype=BF16, out_shards=N_CHIPS)
    gW["w_ple_gate"] = _matmul(hn3, dgl, name="d_w_ple_gate", ta=True, out_dtype=BF16)
    dhn3 = _matmul(dgl, w_pg_f, name="d_hn3", tb=True)
    dh2, dff, dg_ple_pre, dg_mlp_post = _bwd_res_norm(
        dh3, dhn3, h2, row(W["ple_norm_pre"]), ff, row(W["mlp_norm_post"]), name="bwd_res_mlp")
    gW["w_down"] = _matmul(act, dff, name="d_w_down", ta=True, out_dtype=BF16)
    batch1, token1 = scatter_start(("w_ple_proj", "w_ple_gate", "w_down"), 1)
    dup = _matmul(dff, w_down_f, name="d_up", tb=True, after=token1, relu2_of=up, out_dtype=BF16)
    gW["w_up"] = _matmul(hn2, dup, name="d_w_up", ta=True, out_dtype=BF16, out_shards=N_CHIPS)
    batch2, token2 = scatter_start(("w_up",), 2)
    dhn2 = _matmul(dup, w_up_f, name="d_hn2", tb=True, b_shards=N_CHIPS, after=token2)
    dh1, dmo, dg_mlp_pre, dg_mix_post = _bwd_res_norm(
        dh2, dhn2, h1, row(W["mlp_norm_pre"]), mo, row(W["mix_norm_post"]), name="bwd_res_mix")
    gW["w_out"] = _matmul(mixed, dmo, name="d_w_out", ta=True, out_dtype=BF16)
    dmixed = _matmul(dmo, w_out_f, name="d_mixed", tb=True)
    dattn_b, dd_b, dz, dy2a, dg_attn, dg_ssm, db_glu = _mix_bwd(
        dmixed, attn, y1, z, row(W["b_glu"]), row(W["attn_out_norm"]), row(W["ssm_out_norm"]))
    gW["w_glu"] = _matmul(y2b, dz, name="d_w_glu", ta=True, out_dtype=BF16)
    batch3, token3 = scatter_start(("w_out", "w_glu"), 3)
    dy2b = _matmul(dz, w_glu_f, name="d_y2", tb=True, after=token3)
    du, dar8, dai8, dcr_e, dci_e, dbr_e, dbi_e, dd8 = _ssm_bwd(
        u, y1, dy2a, dy2b, st_r, st_i, ends_r, ends_i, lr_e, li_e, ldt_e, bre_e, bim_e, cre_e, cim_e, d_row)
    swap_a, token_a = swap_begin([(1, batch1)], du, "a")
    dlr_e, dli_e, dldt_e, dbre_e, dbim_e = _ssm_param_bwd(dar8, dai8, dbr_e, dbi_e, lr_e, li_e, ldt_e, bre_e, bim_e)

    dqs, dks, dvs = zip(*[_attn_bwd(qb, da, l, dd_, d, heads, token_a)
                          for d, qb, da, l, dd_ in zip(DILATIONS, qkv_b, dattn_b, lse_b, dd_b)])
    dproj = _dproj_join(dqs, dks, dvs, du)
    swap_end(swap_a, dproj, "a")
    swap_b, token_b = swap_begin([(2, batch2), (3, batch3)], dproj, "b")
    gW["w_in"] = _matmul(hn1, dproj, name="d_w_in", ta=True, out_dtype=BF16, out_shards=N_CHIPS, after=token_b)
    batch4, token4 = scatter_start(("w_in",), 4)
    dhn1 = _matmul(dproj, w_in_f, name="d_hn1", tb=True, b_shards=N_CHIPS, after=token4)
    grad_x, dg_mix_pre = _bwd_first(dh1, dhn1, xs, row(W["mix_norm_pre"]))
    swap_end(swap_b, grad_x, "b")
    scatter_finish(batch4, grad_x, 4)

    small_g = {
        "mix_norm_pre": dg_mix_pre, "lam_re": dlr_e.reshape(G, SSM_P), "lam_im": dli_e.reshape(G, SSM_P),
        "log_dt": dldt_e.reshape(G, SSM_P)[:, 0], "ssm_b_re": _collapse_b(dbre_e), "ssm_b_im": _collapse_b(dbim_e),
        "ssm_c_re": _collapse_c(dcr_e), "ssm_c_im": _collapse_c(dci_e), "ssm_d": dd8.sum(axis=1).reshape(-1),
        "b_glu": db_glu, "attn_out_norm": dg_attn, "ssm_out_norm": dg_ssm, "mix_norm_post": dg_mix_post,
        "mlp_norm_pre": dg_mlp_pre, "mlp_norm_post": dg_mlp_post, "ple_norm_pre": dg_ple_pre,
        "ple_norm_post": dg_ple_post,
    }
    g_pack, layout = _pack([small_g[n].reshape(W[n].shape) for n in SMALL])
    w_pack, _ = _pack([W[n] for n in SMALL])
    m_pack, _ = _pack([Mo[n] for n in SMALL])
    v_pack, _ = _pack([Vo[n] for n in SMALL])
    g_sum = _all_reduce_small(g_pack)
    packed = (g_sum,) + tuple(_adamw_small(g_sum, w_pack, m_pack, v_pack))
    for dst, buf in zip((out_g, out_d, out_m, out_v), packed):
        dst.update(zip(SMALL, _unpack(buf, layout)))

    loss = lax.psum(loss_part[0, 0], ("x", "y", "c"))
    lead = lambda a: a[None]
    return (loss, grad_x[None],
            *[lead(out_g[n]) for n in WEIGHTS], *[lead(out_d[n]) for n in WEIGHTS],
            *[lead(out_m[n]) for n in WEIGHTS], *[lead(out_v[n]) for n in WEIGHTS])
```

```python
import functools
import math

import jax
import jax.numpy as jnp
from jax import lax
from jax.experimental import pallas as pl
from jax.experimental.pallas import tpu as pltpu

F32 = jnp.float32
BF16 = jnp.bfloat16
MESH = pl.DeviceIdType.MESH

RMS_EPS = 1e-6
NEG_INF = -1e30
HEAD_DIM = 128
BLK = 128
DILATIONS = (1, 4, 16)
ATTN_LOOKAHEAD = 3
SSM_C = 16
SSM_P = 64
LANES = 128
GROUPS_PER_BLOCK = LANES // SSM_C
STATE_LANES = GROUPS_PER_BLOCK * SSM_P
SSM_CHUNK = 1024
TILE = 8
ADAM_LR, ADAM_B1, ADAM_B2, ADAM_EPS, ADAM_WD, ADAM_STEP = 1e-3, 0.9, 0.999, 1e-8, 0.01, 10
VMEM_LIMIT_BYTES = 56 * 1024 * 1024
MATMUL_VMEM_BYTES = 44 * 1024 * 1024
N_CHIPS = 4
N_DEV = 8
PACK_ROWS = 256


def _cparams(*sem):
    return pltpu.CompilerParams(dimension_semantics=sem or None, vmem_limit_bytes=VMEM_LIMIT_BYTES)


def _rows(tr, w):
    return pl.BlockSpec((tr, w), lambda i: (i, 0))


def _vec(w):
    return pl.BlockSpec((1, w), lambda i: (0, 0))


def _sigmoid(x):
    return 1.0 / (1.0 + jnp.exp(-x))


def _gelu(x):
    c = math.sqrt(2.0 / math.pi)
    return 0.5 * x * (1.0 + jnp.tanh(c * (x + 0.044715 * x * x * x)))


def _gelu_grad(x):
    c = math.sqrt(2.0 / math.pi)
    th = jnp.tanh(c * (x + 0.044715 * x * x * x))
    return 0.5 * (1.0 + th) + 0.5 * x * (1.0 - th * th) * c * (1.0 + 3.0 * 0.044715 * x * x)


def _rms(x, g):
    r = lax.rsqrt(jnp.mean(x * x, axis=-1, keepdims=True) + RMS_EPS)
    return x * r * g


def _rms_bwd(dy, x, g):
    r = lax.rsqrt(jnp.mean(x * x, axis=-1, keepdims=True) + RMS_EPS)
    n = x * r
    dn = dy * g
    dx = r * (dn - n * jnp.mean(dn * n, axis=-1, keepdims=True))
    return dx, dy * n


def _colsum(a):
    return jnp.sum(a, axis=0, keepdims=True)


def _first(i):
    return i == 0


def _matmul(a, b, *, name, ta=False, tb=False, out_dtype=F32, b_shards=1, out_shards=1, b_cols=None,
            after=None, relu2=False, relu2_of=None, tm=1024, tn=2048, tk=2048):
    if ta:
        K, M = a.shape
    else:
        M, K = a.shape
    if b_shards > 1:
        rows, cols = b.shape[1], b.shape[2] * b_shards
    else:
        rows, cols = b.shape
    N, Kb = (rows, cols) if tb else (cols, rows)
    assert K == Kb, (a.shape, b.shape, ta, tb)
    col0 = 0
    if b_cols is not None:
        assert not tb
        col0, N = b_cols
    tm, tn, tk = min(tm, M), min(tn, N), min(tk, K)
    if b_shards > 1:
        shard_cols = cols // b_shards
        if tb:
            tk = min(tk, shard_cols)
        else:
            tn = min(tn, shard_cols)
    if out_shards > 1:
        tn = min(tn, N // out_shards)

    def vmem_bytes(tn_):
        out_bytes = jnp.dtype(out_dtype).itemsize + (2 if relu2 else 0)
        return (4 * (tm * tk + tk * tn_) + 2 * tm * tn_ * out_bytes
                + (2 * relu2_of.dtype.itemsize * tm * tn_ if relu2_of is not None else 0)
                + (4 * tm * tn_ if K > tk else 0))

    while vmem_bytes(tn) > MATMUL_VMEM_BYTES and tn > LANES and col0 % (tn // 2) == 0:
        tn //= 2
    assert M % tm == 0 and N % tn == 0 and K % tk == 0 and col0 % tn == 0
    nk = K // tk
    j0 = col0 // tn

    a_spec = (pl.BlockSpec((tk, tm), lambda i, j, k: (k, i)) if ta
              else pl.BlockSpec((tm, tk), lambda i, j, k: (i, k)))
    if b_shards > 1:
        if tb:
            per = shard_cols // tk
            b_spec = pl.BlockSpec((None, tn, tk), lambda i, j, k: (k // per, j, k % per))
        else:
            per = shard_cols // tn
            b_spec = pl.BlockSpec((None, tk, tn), lambda i, j, k: ((j + j0) // per, k, (j + j0) % per))
    else:
        b_spec = (pl.BlockSpec((tn, tk), lambda i, j, k: (j, k)) if tb
                  else pl.BlockSpec((tk, tn), lambda i, j, k: (k, j + j0)))
    if out_shards > 1:
        per_o = (N // out_shards) // tn
        out_shape = jax.ShapeDtypeStruct((out_shards, M, N // out_shards), out_dtype)
        out_spec = pl.BlockSpec((None, tm, tn), lambda i, j, k: (j // per_o, i, j % per_o))
    else:
        out_shape = jax.ShapeDtypeStruct((M, N), out_dtype)
        out_spec = pl.BlockSpec((tm, tn), lambda i, j, k: (i, j))
    dims = (((0 if ta else 1,), (1 if tb else 0,)), ((), ()))

    extra, extra_specs = [], []
    if relu2_of is not None:
        assert out_shards == 1 and relu2_of.shape == (M, N)
        extra.append(relu2_of)
        extra_specs.append(pl.BlockSpec((tm, tn), lambda i, j, k: (i, j)))
    if after is not None:
        extra.append(after)
        extra_specs.append(pl.BlockSpec(after.shape, lambda i, j, k: (0, 0)))
    n_in = 2 + len(extra)
    if relu2:
        assert out_shards == 1
        out_shape = (out_shape, jax.ShapeDtypeStruct((M, N), BF16))
        out_spec = (out_spec, out_spec)

    def finish(acc, refs):
        o_ref = refs[n_in]
        if relu2_of is not None:
            acc = acc * (2.0 * jnp.maximum(refs[2][...].astype(F32), 0.0))
        o_ref[...] = acc.astype(o_ref.dtype)
        if relu2:
            r = jnp.maximum(acc, 0.0)
            refs[n_in + 1][...] = (r * r).astype(BF16)

    def body(*refs):
        prod = lax.dot_general(refs[0][...], refs[1][...], dims, preferred_element_type=F32)
        if nk == 1:
            finish(prod, refs)
            return
        acc_ref = refs[-1]
        k = pl.program_id(2)

        @pl.when(k == 0)
        def _():
            acc_ref[...] = prod

        @pl.when(k > 0)
        def _():
            acc_ref[...] += prod

        @pl.when(k == nk - 1)
        def _():
            finish(acc_ref[...], refs)

    return pl.pallas_call(
        body, name=name, out_shape=out_shape, grid=(M // tm, N // tn, nk),
        in_specs=[a_spec, b_spec] + extra_specs, out_specs=out_spec,
        scratch_shapes=[pltpu.VMEM((tm, tn), F32)] if nk > 1 else [],
        compiler_params=_cparams("parallel", "parallel", "arbitrary"),
    )(a, b, *extra)


def _norm_cast(x, g, *, name, tr=256):
    S, D = x.shape
    tr = min(tr, S)

    def body(x_ref, g_ref, o_ref):
        o_ref[...] = _rms(x_ref[...], g_ref[...]).astype(BF16)

    return pl.pallas_call(
        body, name=name, out_shape=jax.ShapeDtypeStruct((S, D), BF16), grid=(S // tr,),
        in_specs=[_rows(tr, D), _vec(D)], out_specs=_rows(tr, D),
        compiler_params=_cparams("parallel"))(x, g)


def _res_norm(res, y, g_post, g_next, *, name, tr=256):
    S, D = res.shape
    tr = min(tr, S)

    def body(res_ref, y_ref, gp_ref, gn_ref, h_ref, hn_ref):
        h = res_ref[...] + _rms(y_ref[...], gp_ref[...])
        h_ref[...] = h
        hn_ref[...] = _rms(h, gn_ref[...]).astype(BF16)

    return pl.pallas_call(
        body, name=name,
        out_shape=(jax.ShapeDtypeStruct((S, D), F32), jax.ShapeDtypeStruct((S, D), BF16)),
        grid=(S // tr,), in_specs=[_rows(tr, D), _rows(tr, D), _vec(D), _vec(D)],
        out_specs=(_rows(tr, D), _rows(tr, D)), compiler_params=_cparams("parallel"))(res, y, g_post, g_next)


def _residue_spec(tr, d, w):
    return pl.BlockSpec((tr // d, d * w), lambda i: (i, 0))


def _residue_shape(S, d, w, dtype):
    return jax.ShapeDtypeStruct((S // d, d * w), dtype)


def _residue_scratch(rows, w):
    return pltpu.VMEM((w // LANES, rows, LANES), F32)


def _fill_strips(scr, val):
    for s in range(scr.shape[0]):
        scr[s] = val[:, s * LANES:(s + 1) * LANES]


def _strips_to_residues(scr, o_ref, d):
    strips, rows, _ = scr.shape
    for r in range(d):
        for s in range(strips):
            col = (r * strips + s) * LANES
            o_ref[:, col:col + LANES] = scr[s, pl.ds(r, rows // d, stride=d), :].astype(o_ref.dtype)


def _to_residues(scr, val, o_ref, d):
    if d == 1:
        o_ref[...] = val.astype(o_ref.dtype)
        return
    _fill_strips(scr, val)
    _strips_to_residues(scr, o_ref, d)


def _from_residues(scr, in_ref, d):
    if d == 1:
        return in_ref[...].astype(F32)
    strips, rows, _ = scr.shape
    for r in range(d):
        for s in range(strips):
            col = (r * strips + s) * LANES
            scr[s, pl.ds(r, rows // d, stride=d), :] = in_ref[:, col:col + LANES].astype(F32)
    return jnp.concatenate([scr[s] for s in range(strips)], axis=1)


def _spread_heads(packed, heads, width=HEAD_DIM):
    per = LANES // heads
    return jnp.concatenate([jnp.broadcast_to(packed[:, h * per:h * per + 1], (packed.shape[0], width))
                            for h in range(heads)], axis=1)


def _mix_fwd(os, ls, y1, z, b_glu, g_attn, g_ssm, *, tr=256):
    S, SW = y1.shape
    AW = os[0].shape[1] // DILATIONS[0]
    heads = AW // HEAD_DIM
    tr = min(tr, S)
    nd = len(DILATIONS)

    def body(*refs):
        o_refs, l_refs = refs[:nd], refs[nd:2 * nd]
        y_ref, z_ref, b_ref, ga_ref, gs_ref, attn_ref = refs[2 * nd:2 * nd + 6]
        lse_refs = refs[2 * nd + 6:3 * nd + 6]
        mixed_ref, scr, scr_p = refs[3 * nd + 6:]
        ls_ = [_from_residues(scr_p, l_refs[n], d) for n, d in enumerate(DILATIONS)]
        m = functools.reduce(jnp.maximum, ls_)
        es = [jnp.exp(l - m) for l in ls_]
        tot = functools.reduce(jnp.add, es)
        attn = functools.reduce(jnp.add, [_spread_heads(e / tot, heads) * _from_residues(scr, o_refs[n], d)
                                          for n, (e, d) in enumerate(zip(es, DILATIONS))])
        attn_ref[...] = attn
        lse = m + jnp.log(tot)
        for n, d in enumerate(DILATIONS):
            _to_residues(scr_p, lse, lse_refs[n], d)
        ssm = _gelu(y_ref[...]) * _sigmoid(z_ref[...] + b_ref[...])
        mixed_ref[:, :AW] = _rms(attn, ga_ref[...]).astype(BF16)
        mixed_ref[:, AW:] = _rms(ssm, gs_ref[...]).astype(BF16)

    res_o = [_residue_spec(tr, d, AW) for d in DILATIONS]
    res_l = [_residue_spec(tr, d, LANES) for d in DILATIONS]
    res = pl.pallas_call(
        body, name="mix_fwd",
        out_shape=([jax.ShapeDtypeStruct((S, AW), F32)] + [_residue_shape(S, d, LANES, F32) for d in DILATIONS]
                   + [jax.ShapeDtypeStruct((S, AW + SW), BF16)]),
        grid=(S // tr,),
        in_specs=res_o + res_l + [_rows(tr, SW), _rows(tr, SW), _vec(SW), _vec(AW), _vec(SW)],
        out_specs=[_rows(tr, AW)] + res_l + [_rows(tr, AW + SW)],
        scratch_shapes=[_residue_scratch(tr, AW), _residue_scratch(tr, LANES)],
        compiler_params=_cparams("parallel"))(*os, *ls, y1, z, b_glu, g_attn, g_ssm)
    return res[0], res[1:1 + nd], res[1 + nd]


def _final(h2, gl, e, g_post, target, *, tr=256):
    S, D = h2.shape
    tr = min(tr, S)

    def body(h_ref, gl_ref, e_ref, g_ref, t_ref, dh_ref, dgl_ref, de_ref, loss_ref, dg_ref):
        i = pl.program_id(0)
        gate = _sigmoid(gl_ref[...])
        e_ = e_ref[...]
        ge = gate * e_
        g = g_ref[...]
        diff = h_ref[...] + _rms(ge, g) - t_ref[...]
        dh = diff * (1.0 / D)
        dh_ref[...] = dh
        dge, dgrow = _rms_bwd(dh, ge, g)
        dgl_ref[...] = (dge * e_ * gate * (1.0 - gate)).astype(BF16)
        de_ref[...] = (dge * gate).astype(BF16)
        part = _colsum(0.5 * jnp.mean(diff * diff, axis=-1, keepdims=True))

        @pl.when(_first(i))
        def _():
            loss_ref[...] = jnp.zeros_like(loss_ref)
            dg_ref[...] = jnp.zeros_like(dg_ref)

        loss_ref[...] += part + jnp.zeros((1, LANES), F32)
        dg_ref[...] += _colsum(dgrow)

    return pl.pallas_call(
        body, name="final_fwd_bwd",
        out_shape=(jax.ShapeDtypeStruct((S, D), F32), jax.ShapeDtypeStruct((S, D), BF16),
                   jax.ShapeDtypeStruct((S, D), BF16), jax.ShapeDtypeStruct((1, LANES), F32),
                   jax.ShapeDtypeStruct((1, D), F32)),
        grid=(S // tr,),
        in_specs=[_rows(tr, D), _rows(tr, D), _rows(tr, D), _vec(D), _rows(tr, D)],
        out_specs=(_rows(tr, D), _rows(tr, D), _rows(tr, D), _vec(LANES), _vec(D)),
        compiler_params=_cparams("arbitrary"))(h2, gl, e, g_post, target)


def _bwd_res_norm(dh_out, dhn, h, g_next, y, g_post, *, name, tr=256):
    S, D = h.shape
    tr = min(tr, S)

    def body(dho_ref, dhn_ref, h_ref, gn_ref, y_ref, gp_ref, dh_ref, dy_ref, dgn_ref, dgp_ref):
        i = pl.program_id(0)
        dx, dgn_rows = _rms_bwd(dhn_ref[...], h_ref[...], gn_ref[...])
        dh = dho_ref[...] + dx
        dh_ref[...] = dh
        dy, dgp_rows = _rms_bwd(dh, y_ref[...], gp_ref[...])
        dy_ref[...] = dy.astype(BF16)

        @pl.when(_first(i))
        def _():
            dgn_ref[...] = jnp.zeros_like(dgn_ref)
            dgp_ref[...] = jnp.zeros_like(dgp_ref)

        dgn_ref[...] += _colsum(dgn_rows)
        dgp_ref[...] += _colsum(dgp_rows)

    return pl.pallas_call(
        body, name=name,
        out_shape=(jax.ShapeDtypeStruct((S, D), F32), jax.ShapeDtypeStruct((S, D), BF16),
                   jax.ShapeDtypeStruct((1, D), F32), jax.ShapeDtypeStruct((1, D), F32)),
        grid=(S // tr,),
        in_specs=[_rows(tr, D), _rows(tr, D), _rows(tr, D), _vec(D), _rows(tr, D), _vec(D)],
        out_specs=(_rows(tr, D), _rows(tr, D), _vec(D), _vec(D)),
        compiler_params=_cparams("arbitrary"))(dh_out, dhn, h, g_next, y, g_post)


def _bwd_first(dh1, dhn1, x, g1, *, tr=256):
    S, D = x.shape
    tr = min(tr, S)

    def body(dh_ref, dhn_ref, x_ref, g_ref, dx_ref, dg_ref):
        i = pl.program_id(0)
        dx, dg_rows = _rms_bwd(dhn_ref[...], x_ref[...], g_ref[...])
        dx_ref[...] = dh_ref[...] + dx

        @pl.when(_first(i))
        def _():
            dg_ref[...] = jnp.zeros_like(dg_ref)

        dg_ref[...] += _colsum(dg_rows)

    return pl.pallas_call(
        body, name="bwd_first",
        out_shape=(jax.ShapeDtypeStruct((S, D), F32), jax.ShapeDtypeStruct((1, D), F32)),
        grid=(S // tr,), in_specs=[_rows(tr, D), _rows(tr, D), _rows(tr, D), _vec(D)],
        out_specs=(_rows(tr, D), _vec(D)), compiler_params=_cparams("arbitrary"))(dh1, dhn1, x, g1)


def _mix_bwd(dmixed, attn, y1, z, b_glu, g_attn, g_ssm, *, tr=256):
    S, AW = attn.shape
    SW = y1.shape[1]
    tr = min(tr, S)
    heads = AW // HEAD_DIM
    nd = len(DILATIONS)

    def body(*refs):
        dm_ref, a_ref, y_ref, z_ref, b_ref, ga_ref, gs_ref = refs[:7]
        da_refs, dd_refs = refs[7:7 + nd], refs[7 + nd:7 + 2 * nd]
        dz_ref, dy2_ref, dga_ref, dgs_ref, db_ref, scr, scr_p, dd_scr = refs[7 + 2 * nd:]
        i = pl.program_id(0)
        attn_ = a_ref[...]
        dattn, dga_rows = _rms_bwd(dm_ref[:, :AW], attn_, ga_ref[...])
        prod = dattn * attn_
        per = LANES // heads
        for h in range(heads):
            total = jnp.sum(prod[:, h * HEAD_DIM:(h + 1) * HEAD_DIM], axis=-1, keepdims=True)
            dd_scr[:, h * per:(h + 1) * per] = jnp.broadcast_to(total, (tr, per))
        for n, d in enumerate(DILATIONS):
            _to_residues(scr, dattn, da_refs[n], d)
            _to_residues(scr_p, dd_scr[...], dd_refs[n], d)
        y2 = _gelu(y_ref[...])
        gate = _sigmoid(z_ref[...] + b_ref[...])
        dssm, dgs_rows = _rms_bwd(dm_ref[:, AW:], y2 * gate, gs_ref[...])
        dz = dssm * y2 * gate * (1.0 - gate)
        dz_ref[...] = dz.astype(BF16)
        dy2_ref[...] = dssm * gate

        @pl.when(_first(i))
        def _():
            dga_ref[...] = jnp.zeros_like(dga_ref)
            dgs_ref[...] = jnp.zeros_like(dgs_ref)
            db_ref[...] = jnp.zeros_like(db_ref)

        dga_ref[...] += _colsum(dga_rows)
        dgs_ref[...] += _colsum(dgs_rows)
        db_ref[...] += _colsum(dz)

    res_a = [_residue_spec(tr, d, AW) for d in DILATIONS]
    res_d = [_residue_spec(tr, d, LANES) for d in DILATIONS]
    res = pl.pallas_call(
        body, name="mix_bwd",
        out_shape=([_residue_shape(S, d, AW, BF16) for d in DILATIONS]
                   + [_residue_shape(S, d, LANES, F32) for d in DILATIONS]
                   + [jax.ShapeDtypeStruct((S, SW), BF16), jax.ShapeDtypeStruct((S, SW), F32),
                      jax.ShapeDtypeStruct((1, AW), F32), jax.ShapeDtypeStruct((1, SW), F32),
                      jax.ShapeDtypeStruct((1, SW), F32)]),
        grid=(S // tr,),
        in_specs=[_rows(tr, AW + SW), _rows(tr, AW), _rows(tr, SW), _rows(tr, SW), _vec(SW), _vec(AW), _vec(SW)],
        out_specs=res_a + res_d + [_rows(tr, SW), _rows(tr, SW), _vec(AW), _vec(SW), _vec(SW)],
        scratch_shapes=[_residue_scratch(tr, AW), _residue_scratch(tr, LANES), pltpu.VMEM((tr, LANES), F32)],
        compiler_params=_cparams("arbitrary"))(dmixed, attn, y1, z, b_glu, g_attn, g_ssm)
    return (res[:nd], res[nd:2 * nd]) + tuple(res[2 * nd:])


def _attn_mask2(i):
    row = lax.broadcasted_iota(jnp.int32, (BLK, 2 * BLK), 0)
    col = lax.broadcasted_iota(jnp.int32, (BLK, 2 * BLK), 1)
    return jnp.logical_and(col >= row, jnp.logical_and(col <= row + BLK, jnp.logical_or(col >= BLK, i > 0)))


_NT = (((1,), (1,)), ((), ()))
_TN = (((0,), (0,)), ((), ()))


def _attn_in_specs(width, block_of):
    def at(part, prev):
        def index(r, i):
            blk = block_of(i)
            return (part, jnp.maximum(blk - 1, 0) if prev else blk, r)
        return pl.BlockSpec((None, BLK, width), index)
    return [at(0, False), at(1, False), at(1, True), at(2, False), at(2, True)]


def _proj_qkv(hn, w_in_f, *, tm=1024):
    S, D = hn.shape
    AW = w_in_f.shape[2]
    tm = min(tm, S)

    half = tm // 2
    strips = AW // LANES

    def body(a_ref, b_ref, *rest):
        o_refs, scrs = rest[:-2], rest[-2:]
        prods = [jnp.dot(a_ref[h * half:(h + 1) * half, :], b_ref[...], preferred_element_type=F32) for h in range(2)]
        for h, (prod, scr) in enumerate(zip(prods, scrs)):
            _fill_strips(scr, prod)
            for o_ref, d in zip(o_refs, DILATIONS):
                rows = slice(h * half // d, (h + 1) * half // d)
                if d == 1:
                    o_ref[rows, :] = prod.astype(BF16)
                    continue
                for r in range(d):
                    for s in range(strips):
                        col = (r * strips + s) * LANES
                        o_ref[rows, col:col + LANES] = scr[s, pl.ds(r, half // d, stride=d), :].astype(BF16)

    return pl.pallas_call(
        body, name="proj_qkv",
        out_shape=[jax.ShapeDtypeStruct((3, S // d, d * AW), BF16) for d in DILATIONS], grid=(S // tm, 3),
        in_specs=[pl.BlockSpec((tm, D), lambda i, j: (i, 0)), pl.BlockSpec((None, D, AW), lambda i, j: (j, 0, 0))],
        out_specs=[pl.BlockSpec((None, tm // d, d * AW), lambda i, j: (j, i, 0)) for d in DILATIONS],
        scratch_shapes=[_residue_scratch(half, AW), _residue_scratch(half, AW)],
        compiler_params=_cparams("parallel", "parallel"))(hn, w_in_f)


def _attn_fwd(qkv, d, heads):
    M = qkv.shape[1]
    nb = M // BLK
    width = heads * HEAD_DIM
    per = LANES // heads
    scale = 1.0 / math.sqrt(HEAD_DIM)

    def body(q_ref, kc_ref, kp_ref, vc_ref, vp_ref, o_ref, l_ref):
        mask = _attn_mask2(pl.program_id(1))
        ones = jnp.ones((2 * BLK, HEAD_DIM), BF16)

        def scores(h):
            sl = slice(h * HEAD_DIM, (h + 1) * HEAD_DIM)
            k2 = jnp.concatenate([kp_ref[:, sl], kc_ref[:, sl]], axis=0)
            return lax.dot_general(q_ref[:, sl], k2, _NT, preferred_element_type=F32)

        ahead = [scores(h) for h in range(min(ATTN_LOOKAHEAD, heads))]
        for h in range(heads):
            sl = slice(h * HEAD_DIM, (h + 1) * HEAD_DIM)
            s = jnp.where(mask, ahead.pop(0) * scale, NEG_INF)
            if h + ATTN_LOOKAHEAD < heads:
                ahead.append(scores(h + ATTN_LOOKAHEAD))
            v2 = jnp.concatenate([vp_ref[:, sl], vc_ref[:, sl]], axis=0)
            m = jnp.max(jnp.maximum(s[:, :BLK], s[:, BLK:]), axis=-1, keepdims=True)
            p = jnp.exp(s - m).astype(BF16)
            tot = jnp.dot(p, ones, preferred_element_type=F32)
            o_ref[:, sl] = (jnp.dot(p, v2, preferred_element_type=F32) / tot).astype(BF16)
            l_ref[:, h * per:(h + 1) * per] = m + jnp.log(tot[:, :per])

    return pl.pallas_call(
        body, name=f"attn_fwd_d{d}",
        out_shape=(jax.ShapeDtypeStruct((M, d * width), BF16), jax.ShapeDtypeStruct((M, d * LANES), F32)),
        grid=(d, nb), in_specs=_attn_in_specs(width, lambda i: i),
        out_specs=(pl.BlockSpec((BLK, width), lambda r, i: (i, r)), pl.BlockSpec((BLK, LANES), lambda r, i: (i, r))),
        compiler_params=_cparams("parallel", "parallel"))(qkv, qkv, qkv, qkv, qkv)


def _attn_bwd(qkv, dattn, lse, dd, d, heads, after):
    M = qkv.shape[1]
    nb = M // BLK
    width = heads * HEAD_DIM
    per = LANES // heads
    scale = 1.0 / math.sqrt(HEAD_DIM)

    def block_of(i):
        return nb - 1 - i

    def body(q_ref, kc_ref, kp_ref, vc_ref, vp_ref, da_ref, l_ref, dd_ref, after_ref,
             dq_ref, dk_ref, dv_ref, dk_carry, dv_carry):
        @pl.when(pl.program_id(1) == 0)
        def _():
            dk_carry[...] = jnp.zeros_like(dk_carry)
            dv_carry[...] = jnp.zeros_like(dv_carry)

        mask = _attn_mask2(block_of(pl.program_id(1)))

        def products(h):
            sl = slice(h * HEAD_DIM, (h + 1) * HEAD_DIM)
            k2 = jnp.concatenate([kp_ref[:, sl], kc_ref[:, sl]], axis=0)
            v2 = jnp.concatenate([vp_ref[:, sl], vc_ref[:, sl]], axis=0)
            return (lax.dot_general(q_ref[:, sl], k2, _NT, preferred_element_type=F32),
                    lax.dot_general(da_ref[:, sl], v2, _NT, preferred_element_type=F32), k2)

        ahead = [products(h) for h in range(min(ATTN_LOOKAHEAD, heads))]
        for h in range(heads):
            sl = slice(h * HEAD_DIM, (h + 1) * HEAD_DIM)
            qk, dp, k2 = ahead.pop(0)
            if h + ATTN_LOOKAHEAD < heads:
                ahead.append(products(h + ATTN_LOOKAHEAD))
            q, da = q_ref[:, sl], da_ref[:, sl]
            lse_ = jnp.broadcast_to(l_ref[:, h * per:h * per + 1], (BLK, 2 * BLK))
            dd_ = jnp.broadcast_to(dd_ref[:, h * per:h * per + 1], (BLK, 2 * BLK))
            p = jnp.where(mask, jnp.exp(jnp.where(mask, qk * scale, NEG_INF) - lse_), 0.0)
            ds = (p * (dp - dd_) * scale).astype(BF16)
            dq_ref[:, sl] = jnp.dot(ds, k2, preferred_element_type=F32).astype(BF16)
            dk2 = lax.dot_general(ds, q, _TN, preferred_element_type=F32)
            dv2 = lax.dot_general(p.astype(BF16), da, _TN, preferred_element_type=F32)
            dk_ref[:, sl] = (dk2[BLK:] + dk_carry[:, sl]).astype(BF16)
            dv_ref[:, sl] = (dv2[BLK:] + dv_carry[:, sl]).astype(BF16)
            dk_carry[:, sl] = dk2[:BLK]
            dv_carry[:, sl] = dv2[:BLK]

    blk = pl.BlockSpec((BLK, width), lambda r, i: (block_of(i), r))
    packed = pl.BlockSpec((BLK, LANES), lambda r, i: (block_of(i), r))
    shape = jax.ShapeDtypeStruct((M, d * width), BF16)
    return pl.pallas_call(
        body, name=f"attn_bwd_d{d}", out_shape=(shape,) * 3, grid=(d, nb),
        in_specs=(_attn_in_specs(width, block_of) + [blk, packed, packed]
                  + [pl.BlockSpec(after.shape, lambda r, i: (0, 0))]), out_specs=(blk,) * 3,
        scratch_shapes=[pltpu.VMEM((BLK, width), F32), pltpu.VMEM((BLK, width), F32)],
        compiler_params=_cparams("arbitrary", "arbitrary"))(qkv, qkv, qkv, qkv, qkv, dattn, lse, dd, after)


def _dproj_join(dqs, dks, dvs, du, *, tr=256):
    S, SW = du.shape
    AW = dqs[0].shape[1]
    tr = min(tr, S)
    nd = len(DILATIONS)

    def body(*refs):
        du_ref, out_ref, scr = refs[3 * nd:]
        for part in range(3):
            total = functools.reduce(jnp.add, [_from_residues(scr, refs[part * nd + n], d)
                                               for n, d in enumerate(DILATIONS)])
            out_ref[:, part * AW:(part + 1) * AW] = total.astype(BF16)
        out_ref[:, 3 * AW:] = du_ref[...].astype(BF16)

    return pl.pallas_call(
        body, name="dproj_join", out_shape=jax.ShapeDtypeStruct((S, 3 * AW + SW), BF16), grid=(S // tr,),
        in_specs=[_residue_spec(tr, d, AW) for d in DILATIONS] * 3 + [_rows(tr, SW)],
        out_specs=_rows(tr, 3 * AW + SW), scratch_shapes=[_residue_scratch(tr, AW)],
        compiler_params=_cparams("parallel"))(*dqs, *dks, *dvs, du)


def _ssm_disc(lr, li, ldt):
    dt = jnp.exp(ldt)
    mag = jnp.exp(lr * dt)
    ar = mag * jnp.cos(li * dt)
    ai = mag * jnp.sin(li * dt)
    nr = ar - 1.0
    den = lr * lr + li * li
    return ar, ai, (nr * lr + ai * li) / den, (ai * lr - nr * li) / den


def _ssm_tile_powers(lr, li, ldt, reverse):
    t = lax.broadcasted_iota(jnp.int32, (TILE, 1), 0)
    n = (TILE - t if reverse else t + 1).astype(F32)
    dt = jnp.exp(ldt)
    mag = jnp.exp(n * (lr * dt))
    ang = n * (li * dt)
    return mag * jnp.cos(ang), mag * jnp.sin(ang) * (-1.0 if reverse else 1.0)


def _cmul(ar, ai, br, bi):
    return ar * br - ai * bi, ar * bi + ai * br


LOG_STEPS = 3


def _ssm_step_tables(ar, ai, reverse):
    sub = lax.broadcasted_iota(jnp.int32, (TILE, ar.shape[-1]), 0)
    tables = []
    for k in range(LOG_STEPS):
        keep = sub < TILE - (1 << k) if reverse else sub >= (1 << k)
        tables.append((jnp.where(keep, ar, 0.0), jnp.where(keep, ai, 0.0)))
        ar, ai = _cmul(ar, ai, ar, ai)
    return tables


def _scan(xr, xi, steps, pr, pi, cr, ci, reverse):
    T, lanes = xr.shape
    n = T // TILE
    xr, xi = xr.reshape(n, TILE, lanes), xi.reshape(n, TILE, lanes)
    for k, (mr, mi) in enumerate(steps):
        shift = TILE - (1 << k) if reverse else 1 << k
        qr, qi = _cmul(mr, mi, pltpu.roll(xr, shift, 1), pltpu.roll(xi, shift, 1))
        xr, xi = xr + qr, xi + qi
    out_r, out_i = [None] * n, [None] * n
    edge = 0 if reverse else TILE - 1
    for j in (reversed(range(n)) if reverse else range(n)):
        er, ei = _cmul(pr, pi, cr, ci)
        sr, si = xr[j] + er, xi[j] + ei
        out_r[j], out_i[j] = sr, si
        cr, ci = sr[edge:edge + 1], si[edge:edge + 1]
    return jnp.concatenate(out_r, axis=0), jnp.concatenate(out_i, axis=0), cr, ci


def _ssm_specs(T, nch, rev):
    def t_of(c):
        return nch - 1 - c if rev else c
    tok = pl.BlockSpec((T, LANES), lambda j, c: (t_of(c), j))
    par = pl.BlockSpec((None, 1, STATE_LANES), lambda j, c: (j, 0, 0))
    bmat = pl.BlockSpec((None, LANES, STATE_LANES), lambda j, c: (j, 0, 0))
    cmat = pl.BlockSpec((None, STATE_LANES, LANES), lambda j, c: (j, 0, 0))
    dvec = pl.BlockSpec((1, LANES), lambda j, c: (0, j))
    return tok, par, bmat, cmat, dvec


def _ssm_fwd(u, lr_e, li_e, ldt_e, bre_e, bim_e, cre_e, cim_e, d_skip):
    S, SW = u.shape
    T = min(SSM_CHUNK, S)
    nch, nbk = S // T, SW // LANES
    tok, par, bmat, cmat, dvec = _ssm_specs(T, nch, False)
    state_spec = pl.BlockSpec((T, STATE_LANES), lambda j, c: (c, j))
    carry_spec = pl.BlockSpec((None, 1, STATE_LANES), lambda j, c: (c, 0, j))

    def body(u_ref, lr_ref, li_ref, ldt_ref, bre_ref, bim_ref, cre_ref, cim_ref, d_ref,
             y_ref, y2_ref, sr_ref, si_ref, er_ref, ei_ref, bbr, bbi, steps, pw, carry):
        c = pl.program_id(1)

        @pl.when(c == 0)
        def _():
            lr, li, ldt = lr_ref[...], li_ref[...], ldt_ref[...]
            ar, ai, kr, ki = _ssm_disc(lr, li, ldt)
            for k, (mr, mi) in enumerate(_ssm_step_tables(ar, ai, False)):
                steps[0, k], steps[1, k] = mr, mi
            bbr[...] = (kr * bre_ref[...] - ki * bim_ref[...]).astype(BF16)
            bbi[...] = (kr * bim_ref[...] + ki * bre_ref[...]).astype(BF16)
            pw[0], pw[1] = _ssm_tile_powers(lr, li, ldt, False)
            carry[...] = jnp.zeros_like(carry)

        u_ = u_ref[...]
        ub = u_.astype(BF16)
        sr, si, cr, ci = _scan(jnp.dot(ub, bbr[...], preferred_element_type=F32),
                               jnp.dot(ub, bbi[...], preferred_element_type=F32),
                               [(steps[0, k], steps[1, k]) for k in range(LOG_STEPS)],
                               pw[0], pw[1], carry[0], carry[1], False)
        carry[0], carry[1] = cr, ci
        er_ref[...], ei_ref[...] = cr, ci
        sr_ref[...], si_ref[...] = sr, si
        y0 = (jnp.dot(sr.astype(BF16), cre_ref[...].astype(BF16), preferred_element_type=F32)
              - jnp.dot(si.astype(BF16), cim_ref[...].astype(BF16), preferred_element_type=F32))
        y1 = y0 + d_ref[...] * u_
        y_ref[...] = y1
        y2_ref[...] = _gelu(y1).astype(BF16)

    states = jax.ShapeDtypeStruct((S, nbk * STATE_LANES), F32)
    ends = jax.ShapeDtypeStruct((nch, 1, nbk * STATE_LANES), F32)
    return pl.pallas_call(
        body, name="ssm_fwd",
        out_shape=(jax.ShapeDtypeStruct((S, SW), F32), jax.ShapeDtypeStruct((S, SW), BF16), states, states, ends, ends),
        grid=(nbk, nch), in_specs=[tok, par, par, par, bmat, bmat, cmat, cmat, dvec],
        out_specs=(tok, tok, state_spec, state_spec, carry_spec, carry_spec),
        scratch_shapes=[pltpu.VMEM((LANES, STATE_LANES), BF16), pltpu.VMEM((LANES, STATE_LANES), BF16),
                        pltpu.VMEM((2, LOG_STEPS, TILE, STATE_LANES), F32), pltpu.VMEM((2, TILE, STATE_LANES), F32),
                        pltpu.VMEM((2, 1, STATE_LANES), F32)],
        compiler_params=_cparams("arbitrary", "arbitrary"),
    )(u, lr_e, li_e, ldt_e, bre_e, bim_e, cre_e, cim_e, d_skip)


def _ssm_bwd(u, y1, dy2a, dy2b, st_r, st_i, ends_r, ends_i, lr_e, li_e, ldt_e, bre_e, bim_e, cre_e, cim_e, d_skip):
    S, SW = u.shape
    T = min(SSM_CHUNK, S)
    nch, nbk = S // T, SW // LANES
    tok, par, bmat, cmat, dvec = _ssm_specs(T, nch, True)
    state_spec = pl.BlockSpec((T, STATE_LANES), lambda j, c: (nch - 1 - c, j))
    prev_spec = pl.BlockSpec((None, 1, STATE_LANES), lambda j, c: (jnp.maximum(nch - 2 - c, 0), 0, j))
    acc8 = pl.BlockSpec((None, 8, STATE_LANES), lambda j, c: (j, 0, 0))
    dd8 = pl.BlockSpec((None, 8, LANES), lambda j, c: (j, 0, 0))

    def body(u_ref, y_ref, da_ref, db_ref, sr_ref, si_ref, pr_ref, pi_ref, lr_ref, li_ref, ldt_ref,
             bre_ref, bim_ref, cre_ref, cim_ref, d_ref,
             du_ref, dar_ref, dai_ref, dcr_ref, dci_ref, dbr_ref, dbi_ref, ddk_ref,
             bbr, bbi, steps, pw, carry):
        c = pl.program_id(1)

        @pl.when(c == 0)
        def _():
            lr, li, ldt = lr_ref[...], li_ref[...], ldt_ref[...]
            ar, ai, kr, ki = _ssm_disc(lr, li, ldt)
            for k, (mr, mi) in enumerate(_ssm_step_tables(ar, -ai, True)):
                steps[0, k], steps[1, k] = mr, mi
            bbr[...] = (kr * bre_ref[...] - ki * bim_ref[...]).astype(BF16)
            bbi[...] = (kr * bim_ref[...] + ki * bre_ref[...]).astype(BF16)
            pw[0], pw[1] = _ssm_tile_powers(lr, li, ldt, True)
            carry[...] = jnp.zeros_like(carry)
            for ref in (dar_ref, dai_ref, dcr_ref, dci_ref, dbr_ref, dbi_ref, ddk_ref):
                ref[...] = jnp.zeros_like(ref)

        u_ = u_ref[...]
        ub = u_.astype(BF16)
        dy1 = (da_ref[...] + db_ref[...]) * _gelu_grad(y_ref[...])
        dyb = dy1.astype(BF16)

        sr, si = sr_ref[...], si_ref[...]
        has_prev = c < nch - 1
        s0r = jnp.where(has_prev, pr_ref[...], 0.0)
        s0i = jnp.where(has_prev, pi_ref[...], 0.0)

        cre_b, cim_b = cre_ref[...].astype(BF16), cim_ref[...].astype(BF16)
        gr, gi, cr, ci = _scan(lax.dot_general(dyb, cre_b, _NT, preferred_element_type=F32),
                               -lax.dot_general(dyb, cim_b, _NT, preferred_element_type=F32),
                               [(steps[0, k], steps[1, k]) for k in range(LOG_STEPS)],
                               pw[0], pw[1], carry[0], carry[1], True)
        carry[0], carry[1] = cr, ci

        row = lax.broadcasted_iota(jnp.int32, (T, STATE_LANES), 0)
        spr = jnp.where(row == 0, s0r, pltpu.roll(sr, 1, 0))
        spi = jnp.where(row == 0, s0i, pltpu.roll(si, 1, 0))

        def fold(a):
            return jnp.sum(a.reshape(T // 8, 8, a.shape[-1]), axis=0)

        dar_ref[...] += fold(gr * spr + gi * spi)
        dai_ref[...] += fold(gi * spr - gr * spi)
        srb, sib, grb, gib = sr.astype(BF16), si.astype(BF16), gr.astype(BF16), gi.astype(BF16)
        dcr_ref[...] += lax.dot_general(srb, dyb, _TN, preferred_element_type=F32)
        dci_ref[...] -= lax.dot_general(sib, dyb, _TN, preferred_element_type=F32)
        dbr_ref[...] += lax.dot_general(ub, grb, _TN, preferred_element_type=F32)
        dbi_ref[...] += lax.dot_general(ub, gib, _TN, preferred_element_type=F32)
        du_ref[...] = (lax.dot_general(grb, bbr[...], _NT, preferred_element_type=F32)
                       + lax.dot_general(gib, bbi[...], _NT, preferred_element_type=F32)
                       + dy1 * d_ref[...])
        ddk_ref[...] += fold(dy1 * u_)

    return pl.pallas_call(
        body, name="ssm_bwd",
        out_shape=(jax.ShapeDtypeStruct((S, SW), F32),
                   jax.ShapeDtypeStruct((nbk, 8, STATE_LANES), F32), jax.ShapeDtypeStruct((nbk, 8, STATE_LANES), F32),
                   jax.ShapeDtypeStruct((nbk, STATE_LANES, LANES), F32), jax.ShapeDtypeStruct((nbk, STATE_LANES, LANES), F32),
                   jax.ShapeDtypeStruct((nbk, LANES, STATE_LANES), F32), jax.ShapeDtypeStruct((nbk, LANES, STATE_LANES), F32),
                   jax.ShapeDtypeStruct((nbk, 8, LANES), F32)),
        grid=(nbk, nch),
        in_specs=[tok, tok, tok, tok, state_spec, state_spec, prev_spec, prev_spec, par, par, par,
                  bmat, bmat, cmat, cmat, dvec],
        out_specs=(tok, acc8, acc8, cmat, cmat, bmat, bmat, dd8),
        scratch_shapes=[pltpu.VMEM((LANES, STATE_LANES), BF16), pltpu.VMEM((LANES, STATE_LANES), BF16),
                        pltpu.VMEM((2, LOG_STEPS, TILE, STATE_LANES), F32), pltpu.VMEM((2, TILE, STATE_LANES), F32),
                        pltpu.VMEM((2, 1, STATE_LANES), F32)],
        compiler_params=_cparams("arbitrary", "arbitrary"),
    )(u, y1, dy2a, dy2b, st_r, st_i, ends_r, ends_i, lr_e, li_e, ldt_e, bre_e, bim_e, cre_e, cim_e, d_skip)


def _ssm_param_bwd(dar8, dai8, dbr_e, dbi_e, lr_e, li_e, ldt_e, bre_e, bim_e):
    nbk = lr_e.shape[0]
    par = pl.BlockSpec((None, 1, STATE_LANES), lambda j: (j, 0, 0))
    acc8 = pl.BlockSpec((None, 8, STATE_LANES), lambda j: (j, 0, 0))
    bmat = pl.BlockSpec((None, LANES, STATE_LANES), lambda j: (j, 0, 0))

    def body(dar_ref, dai_ref, dbr_ref, dbi_ref, lr_ref, li_ref, ldt_ref, bre_ref, bim_ref,
             dlr_ref, dli_ref, dldt_ref, dbre_ref, dbim_ref):
        lr, li, ldt = lr_ref[...], li_ref[...], ldt_ref[...]
        (ar, ai, kr, ki), vjp = jax.vjp(_ssm_disc, lr, li, ldt)
        dbr, dbi, bre, bim = dbr_ref[...], dbi_ref[...], bre_ref[...], bim_ref[...]
        dbre_ref[...] = kr * dbr + ki * dbi
        dbim_ref[...] = kr * dbi - ki * dbr
        dkr = _colsum(dbr * bre + dbi * bim)
        dki = _colsum(dbi * bre - dbr * bim)
        dlr, dli, dldt = vjp((_colsum(dar_ref[...]), _colsum(dai_ref[...]), dkr, dki))
        dlr_ref[...] = dlr
        dli_ref[...] = dli
        tot = jnp.broadcast_to(dldt, (8, STATE_LANES))
        sh = 1
        while sh < SSM_P:
            tot = tot + pltpu.roll(tot, STATE_LANES - sh, 1)
            sh *= 2
        dldt_ref[...] = tot[:1]

    vec = jax.ShapeDtypeStruct((nbk, 1, STATE_LANES), F32)
    mat = jax.ShapeDtypeStruct((nbk, LANES, STATE_LANES), F32)
    return pl.pallas_call(
        body, name="ssm_param_bwd", out_shape=(vec, vec, vec, mat, mat), grid=(nbk,),
        in_specs=[acc8, acc8, bmat, bmat, par, par, par, bmat, bmat],
        out_specs=(par, par, par, bmat, bmat), compiler_params=_cparams("parallel"),
    )(dar8, dai8, dbr_e, dbi_e, lr_e, li_e, ldt_e, bre_e, bim_e)


def _expand_b(b):
    G = b.shape[0]
    bt = b.transpose(0, 2, 1).reshape(G // GROUPS_PER_BLOCK, GROUPS_PER_BLOCK, SSM_C, SSM_P)
    eye = jnp.eye(GROUPS_PER_BLOCK, dtype=b.dtype)
    return (bt[:, :, :, None, :] * eye[None, :, None, :, None]).reshape(G // GROUPS_PER_BLOCK, LANES, STATE_LANES)


def _collapse_b(be):
    nbk = be.shape[0]
    eye = jnp.eye(GROUPS_PER_BLOCK, dtype=be.dtype)
    d5 = be.reshape(nbk, GROUPS_PER_BLOCK, SSM_C, GROUPS_PER_BLOCK, SSM_P)
    d4 = (d5 * eye[None, :, None, :, None]).sum(axis=3)
    return d4.transpose(0, 1, 3, 2).reshape(nbk * GROUPS_PER_BLOCK, SSM_P, SSM_C)


def _expand_c(cm):
    G = cm.shape[0]
    ct = cm.transpose(0, 2, 1).reshape(G // GROUPS_PER_BLOCK, GROUPS_PER_BLOCK, SSM_P, SSM_C)
    eye = jnp.eye(GROUPS_PER_BLOCK, dtype=cm.dtype)
    return (ct[:, :, :, None, :] * eye[None, :, None, :, None]).reshape(G // GROUPS_PER_BLOCK, STATE_LANES, LANES)


def _collapse_c(ce):
    nbk = ce.shape[0]
    eye = jnp.eye(GROUPS_PER_BLOCK, dtype=ce.dtype)
    d5 = ce.reshape(nbk, GROUPS_PER_BLOCK, SSM_P, GROUPS_PER_BLOCK, SSM_C)
    d4 = (d5 * eye[None, :, None, :, None]).sum(axis=3)
    return d4.transpose(0, 1, 3, 2).reshape(nbk * GROUPS_PER_BLOCK, SSM_C, SSM_P)


def _place():
    x, y, c = lax.axis_index("x"), lax.axis_index("y"), lax.axis_index("c")
    return x, y, c


def _other_chips(x, y):
    return [(1 - x, y), (x, 1 - y), (1 - x, 1 - y)]


_ANY = pl.BlockSpec(memory_space=pl.ANY)


_HBM = pl.BlockSpec(memory_space=pltpu.HBM)
_SEM = pl.BlockSpec(memory_space=pltpu.SEMAPHORE)
_EFFECT = pltpu.SideEffectType.DATAFLOW_SIDE_EFFECTING
_TOKEN = jax.ShapeDtypeStruct((8, LANES), F32)


def _hbm(a):
    return pltpu.with_memory_space_constraint(a, pltpu.HBM)


def _place_own(src, *, gather, name, after=None, tr=512):
    R, C = src.shape[-2:]
    tr = min(tr, R)
    x, y, _ = _place()
    me = (2 * x + y).astype(jnp.int32).reshape(1)
    extra = [] if after is None else [after]

    def body(me_ref, s_ref, *rest):
        rest[-1][...] = s_ref[...].astype(BF16)

    own = pl.BlockSpec((None, tr, C), lambda i, me_ref: (me_ref[0], i, 0))
    grid_spec = pltpu.PrefetchScalarGridSpec(
        num_scalar_prefetch=1, grid=(R // tr,),
        in_specs=([pl.BlockSpec((tr, C), lambda i, me_ref: (i, 0)) if gather else own]
                  + [pl.BlockSpec(a.shape, lambda i, me_ref: (0, 0)) for a in extra]), out_specs=own)
    return pl.pallas_call(
        body, name=name, grid_spec=grid_spec, out_shape=jax.ShapeDtypeStruct((N_CHIPS, R, C), BF16),
        compiler_params=_cparams("parallel"))(me, src, *extra)


def _exchange_copy(src_slot, land_slot, send, recv, k, j, peer, c):
    return pltpu.make_async_remote_copy(
        src_ref=src_slot, dst_ref=land_slot, send_sem=send.at[3 * k + j], recv_sem=recv.at[3 * k + j],
        device_id=(peer[0], peer[1], c), device_id_type=MESH)


def _exchange_start(lands, srcs, groups, *, name):
    n, ng = len(lands), len(groups)
    bufs = list(lands) + list(srcs)
    nb = len(bufs)

    def body(*refs):
        lnd, src, sems = refs[:n], refs[n:nb], refs[nb:nb + 2 * ng]
        token = refs[2 * nb + 2 * ng]
        x, y, c = _place()
        me = 2 * x + y
        for gi, group in enumerate(groups):
            for k, w in enumerate(group):
                for j, peer in enumerate(_other_chips(x, y)):
                    if src:
                        sent, dst = src[w].at[2 * peer[0] + peer[1]], lnd[w].at[me]
                    else:
                        sent = dst = lnd[w].at[me, c]
                    _exchange_copy(sent, dst, sems[2 * gi], sems[2 * gi + 1], k, j, peer, c).start()
        token[...] = jnp.zeros_like(token)

    sem_shapes = [pltpu.SemaphoreType.DMA((3 * len(g),)) for g in groups for _ in range(2)]
    res = pl.pallas_call(
        body, name=name,
        out_shape=sem_shapes + [pltpu.HBM(a.shape, a.dtype) for a in bufs] + [_TOKEN],
        in_specs=[_HBM] * nb,
        out_specs=[_SEM] * (2 * ng) + [_HBM] * nb + [pl.BlockSpec(memory_space=pltpu.VMEM)],
        input_output_aliases={i: 2 * ng + i for i in range(nb)},
        compiler_params=pltpu.CompilerParams(has_side_effects=_EFFECT),
    )(*[_hbm(a) for a in bufs])
    sems = [(res[2 * gi], res[2 * gi + 1]) for gi in range(ng)]
    return sems, res[2 * ng:2 * ng + n], res[2 * ng + n:2 * ng + nb], res[-1]


def _exchange_wait(lands, srcs, sems, after, *, name):
    n = len(lands)
    bufs = list(lands) + list(srcs)
    nb = len(bufs)
    send_sems, recv_sems = sems

    def body(*refs):
        lnd, src, send, recv = refs[:n], refs[n:nb], refs[nb], refs[nb + 1]
        x, y, c = _place()
        for k in range(n):
            for j, peer in enumerate(_other_chips(x, y)):
                slot = 2 * peer[0] + peer[1]
                if src:
                    copy = _exchange_copy(src[k].at[slot], lnd[k].at[slot], send, recv, k, j, peer, c)
                else:
                    copy = _exchange_copy(lnd[k].at[slot, c], lnd[k].at[slot, c], send, recv, k, j, peer, c)
                copy.wait_send()
                copy.wait_recv()

    res = pl.pallas_call(
        body, name=name, out_shape=[pltpu.HBM(a.shape, a.dtype) for a in bufs],
        in_specs=[_HBM] * nb + [_SEM, _SEM, _ANY], out_specs=[_HBM] * nb,
        input_output_aliases={i: i for i in range(nb)},
        compiler_params=pltpu.CompilerParams(has_side_effects=_EFFECT),
    )(*bufs, send_sems, recv_sems, after)
    return res[:n]


def _pair_fill(lands, *, name):
    n = len(lands)

    def body(*refs):
        ins, outs, send, recv = refs[:n], refs[n:2 * n], refs[2 * n], refs[2 * n + 1]
        x, y, c = _place()
        for w in range(n):
            for j, (px, py) in enumerate(_other_chips(x, y)):
                slot = 2 * px + py
                pltpu.make_async_remote_copy(
                    src_ref=ins[w].at[slot, c], dst_ref=outs[w].at[slot, c], send_sem=send.at[3 * w + j],
                    recv_sem=recv.at[3 * w + j], device_id=(x, y, 1 - c), device_id_type=MESH).start()
        for w in range(n):
            for j, (px, py) in enumerate(_other_chips(x, y)):
                slot = 2 * px + py
                arrival = pltpu.make_async_remote_copy(
                    src_ref=ins[w].at[slot, c], dst_ref=outs[w].at[slot, 1 - c], send_sem=send.at[3 * w + j],
                    recv_sem=recv.at[3 * w + j], device_id=(x, y, 1 - c), device_id_type=MESH)
                arrival.wait_recv()
                arrival.wait_send()

    return pl.pallas_call(
        body, name=name, out_shape=[jax.ShapeDtypeStruct(a.shape, a.dtype) for a in lands],
        in_specs=[_ANY] * n, out_specs=[_ANY] * n, input_output_aliases={i: i for i in range(n)},
        scratch_shapes=[pltpu.SemaphoreType.DMA((3 * n,)), pltpu.SemaphoreType.DMA((3 * n,))],
    )(*lands)


def _pair_copy(src, dst, send, recv, w, j, sibling):
    return pltpu.make_async_remote_copy(
        src_ref=src, dst_ref=dst, send_sem=send.at[3 * w + j], recv_sem=recv.at[3 * w + j],
        device_id=sibling, device_id_type=MESH)


def _pair_start(lands, *, name):
    n = len(lands)

    def body(*refs):
        bufs, send, recv, token = refs[:n], refs[n], refs[n + 1], refs[2 * n + 2]
        x, y, c = _place()
        for w in range(n):
            for j, (px, py) in enumerate(_other_chips(x, y)):
                half = bufs[w].at[2 * px + py, c]
                _pair_copy(half, half, send, recv, w, j, (x, y, 1 - c)).start()
        token[...] = jnp.zeros_like(token)

    res = pl.pallas_call(
        body, name=name,
        out_shape=[pltpu.SemaphoreType.DMA((3 * n,))] * 2 + [pltpu.HBM(a.shape, a.dtype) for a in lands] + [_TOKEN],
        in_specs=[_HBM] * n, out_specs=[_SEM, _SEM] + [_HBM] * n + [pl.BlockSpec(memory_space=pltpu.VMEM)],
        input_output_aliases={i: 2 + i for i in range(n)},
        compiler_params=pltpu.CompilerParams(has_side_effects=_EFFECT),
    )(*[_hbm(a) for a in lands])
    return (res[0], res[1]), res[2:2 + n], res[-1]


def _pair_wait(lands, sems, after, *, name):
    n = len(lands)

    def body(*refs):
        bufs, send, recv = refs[:n], refs[n], refs[n + 1]
        x, y, c = _place()
        for w in range(n):
            for j, (px, py) in enumerate(_other_chips(x, y)):
                slot = 2 * px + py
                copy = _pair_copy(bufs[w].at[slot, c], bufs[w].at[slot, 1 - c], send, recv, w, j, (x, y, 1 - c))
                copy.wait_send()
                copy.wait_recv()

    return pl.pallas_call(
        body, name=name, out_shape=[pltpu.HBM(a.shape, a.dtype) for a in lands],
        in_specs=[_HBM] * n + [_SEM, _SEM, _ANY], out_specs=[_HBM] * n,
        input_output_aliases={i: i for i in range(n)},
        compiler_params=pltpu.CompilerParams(has_side_effects=_EFFECT),
    )(*lands, *sems, after)


def _sum_partials(land, *, name, tr=256):
    _, R, C = land.shape
    tr = min(tr, R)

    def body(l_ref, o_ref):
        acc = l_ref[0].astype(F32)
        for k in range(1, N_CHIPS):
            acc = acc + l_ref[k].astype(F32)
        o_ref[...] = acc

    return pl.pallas_call(
        body, name=name, out_shape=jax.ShapeDtypeStruct((R, C), F32), grid=(R // tr,),
        in_specs=[pl.BlockSpec((N_CHIPS, tr, C), lambda i: (0, i, 0))], out_specs=_rows(tr, C),
        compiler_params=_cparams("parallel"))(land)


def _swap_with_sibling(sums, *, name):
    n = len(sums)

    def body(*refs):
        ins, outs = refs[:n], refs[n:2 * n]
        send_sems, recv_sems = refs[2 * n:]
        x, y, c = _place()
        copies = [pltpu.make_async_remote_copy(
            src_ref=ins[w], dst_ref=outs[w], send_sem=send_sems.at[w], recv_sem=recv_sems.at[w],
            device_id=(x, y, 1 - c), device_id_type=MESH) for w in range(n)]
        for cp in copies:
            cp.start()
        for cp in copies:
            cp.wait_recv()
            cp.wait_send()

    return pl.pallas_call(
        body, name=name,
        out_shape=[jax.ShapeDtypeStruct(s.shape, s.dtype) for s in sums],
        in_specs=[_ANY] * n, out_specs=[_ANY] * n,
        scratch_shapes=[pltpu.SemaphoreType.DMA((n,)), pltpu.SemaphoreType.DMA((n,))],
    )(*sums)


def _swap_start(sums, *, name):
    n = len(sums)
    bufs = list(sums) + [lax.empty(s.shape, s.dtype) for s in sums]

    def body(*refs):
        src, lnd, send, recv, token = refs[:n], refs[n:2 * n], refs[2 * n], refs[2 * n + 1], refs[4 * n + 2]
        x, y, c = _place()
        for w in range(n):
            pltpu.make_async_remote_copy(
                src_ref=src[w], dst_ref=lnd[w], send_sem=send.at[w], recv_sem=recv.at[w],
                device_id=(x, y, 1 - c), device_id_type=MESH).start()
        token[...] = jnp.zeros_like(token)

    res = pl.pallas_call(
        body, name=name,
        out_shape=[pltpu.SemaphoreType.DMA((n,))] * 2 + [pltpu.HBM(a.shape, a.dtype) for a in bufs] + [_TOKEN],
        in_specs=[_HBM] * (2 * n),
        out_specs=[_SEM, _SEM] + [_HBM] * (2 * n) + [pl.BlockSpec(memory_space=pltpu.VMEM)],
        input_output_aliases={i: 2 + i for i in range(2 * n)},
        compiler_params=pltpu.CompilerParams(has_side_effects=_EFFECT),
    )(*[_hbm(a) for a in bufs])
    return (res[0], res[1]), res[2:2 + n], res[2 + n:2 + 2 * n], res[-1]


def _swap_wait(sums, lands, sems, after, *, name):
    n = len(sums)

    def body(*refs):
        src, lnd, send, recv = refs[:n], refs[n:2 * n], refs[2 * n], refs[2 * n + 1]
        x, y, c = _place()
        for w in range(n):
            copy = pltpu.make_async_remote_copy(
                src_ref=src[w], dst_ref=lnd[w], send_sem=send.at[w], recv_sem=recv.at[w],
                device_id=(x, y, 1 - c), device_id_type=MESH)
            copy.wait_send()
            copy.wait_recv()

    bufs = list(sums) + list(lands)
    res = pl.pallas_call(
        body, name=name, out_shape=[pltpu.HBM(a.shape, a.dtype) for a in bufs],
        in_specs=[_HBM] * (2 * n) + [_SEM, _SEM, _ANY], out_specs=[_HBM] * (2 * n),
        input_output_aliases={i: i for i in range(2 * n)},
        compiler_params=pltpu.CompilerParams(has_side_effects=_EFFECT),
    )(*bufs, *sems, after)
    return res[:n], res[n:]


def _adamw_math(w, g, m, v):
    m = ADAM_B1 * m + (1.0 - ADAM_B1) * g
    v = ADAM_B2 * v + (1.0 - ADAM_B2) * (g * g)
    m_hat = m / (1.0 - ADAM_B1 ** ADAM_STEP)
    v_hat = v / (1.0 - ADAM_B2 ** ADAM_STEP)
    delta = -ADAM_LR * (m_hat / (jnp.sqrt(v_hat) + ADAM_EPS) + ADAM_WD * w)
    return delta, m, v


def _adamw_pair(mine, theirs, w, m, v, *, name, tr=128):
    R, C = w.shape
    tr = min(tr, R)

    def body(a_ref, b_ref, w_ref, m_ref, v_ref, g_ref, d_ref, nm_ref, nv_ref):
        g = a_ref[...] + b_ref[...]
        g_ref[...] = g
        d_ref[...], nm_ref[...], nv_ref[...] = _adamw_math(w_ref[...], g, m_ref[...], v_ref[...])

    shape = jax.ShapeDtypeStruct((R, C), F32)
    return pl.pallas_call(
        body, name=name, out_shape=(shape,) * 4, grid=(R // tr,),
        in_specs=[_rows(tr, C)] * 5, out_specs=(_rows(tr, C),) * 4,
        compiler_params=_cparams("parallel"))(mine, theirs, w, m, v)


def _all_reduce_small(packed):
    R = packed.shape[0]
    half = R // 2

    def body(x_ref, g_ref, sib_ref, pair_ref, land_ref, send_sems, recv_sems):
        x, y, c = _place()
        me = 2 * x + y
        sibling = (x, y, 1 - c)

        swap = pltpu.make_async_remote_copy(
            src_ref=x_ref, dst_ref=sib_ref, send_sem=send_sems.at[0], recv_sem=recv_sems.at[0],
            device_id=sibling, device_id_type=MESH)
        swap.start()
        swap.wait()
        mine, theirs = x_ref[...], sib_ref[...]
        south = c == 0
        pair_ref[...] = jnp.where(south, mine, theirs) + jnp.where(south, theirs, mine)

        land_ref[me] = pair_ref[c]
        for j, (px, py) in enumerate(_other_chips(x, y)):
            pltpu.make_async_remote_copy(
                src_ref=pair_ref.at[c], dst_ref=land_ref.at[me], send_sem=send_sems.at[1 + j],
                recv_sem=recv_sems.at[1 + j], device_id=(px, py, c), device_id_type=MESH).start()
        for j, (px, py) in enumerate(_other_chips(x, y)):
            arrival = pltpu.make_async_remote_copy(
                src_ref=pair_ref.at[c], dst_ref=land_ref.at[2 * px + py], send_sem=send_sems.at[1 + j],
                recv_sem=recv_sems.at[1 + j], device_id=(px, py, c), device_id_type=MESH)
            arrival.wait_recv()
            arrival.wait_send()
        total = land_ref[0]
        for k in range(1, N_CHIPS):
            total = total + land_ref[k]
        g_ref[c] = total

        give = pltpu.make_async_remote_copy(
            src_ref=g_ref.at[c], dst_ref=g_ref.at[c], send_sem=send_sems.at[4], recv_sem=recv_sems.at[4],
            device_id=sibling, device_id_type=MESH)
        give.start()
        take = pltpu.make_async_remote_copy(
            src_ref=g_ref.at[c], dst_ref=g_ref.at[1 - c], send_sem=send_sems.at[4], recv_sem=recv_sems.at[4],
            device_id=sibling, device_id_type=MESH)
        take.wait_recv()
        give.wait_send()

    vm = pl.BlockSpec(memory_space=pltpu.VMEM)
    return pl.pallas_call(
        body, name="all_reduce_small", out_shape=jax.ShapeDtypeStruct((2, half, LANES), F32),
        in_specs=[vm], out_specs=vm,
        scratch_shapes=[pltpu.VMEM((2, half, LANES), F32), pltpu.VMEM((2, half, LANES), F32),
                        pltpu.VMEM((N_CHIPS, half, LANES), F32),
                        pltpu.SemaphoreType.DMA((5,)), pltpu.SemaphoreType.DMA((5,))],
        compiler_params=pltpu.CompilerParams(vmem_limit_bytes=VMEM_LIMIT_BYTES),
    )(packed.reshape(2, half, LANES)).reshape(R, LANES)


def _adamw_small(g, w, m, v):
    R = g.shape[0]
    tr = PACK_ROWS

    def body(g_ref, w_ref, m_ref, v_ref, d_ref, nm_ref, nv_ref):
        d_ref[...], nm_ref[...], nv_ref[...] = _adamw_math(w_ref[...], g_ref[...], m_ref[...], v_ref[...])

    shape = jax.ShapeDtypeStruct((R, LANES), F32)
    return pl.pallas_call(
        body, name="adamw_small", out_shape=(shape,) * 3, grid=(R // tr,),
        in_specs=[_rows(tr, LANES)] * 4, out_specs=(_rows(tr, LANES),) * 3,
        compiler_params=_cparams("parallel"))(g, w, m, v)


def _pack(arrays):
    parts, layout = [], []
    for a in arrays:
        n = a.size
        rows = -(-n // (8 * LANES)) * 8
        flat = jnp.pad(a.reshape(-1).astype(F32), (0, rows * LANES - n))
        parts.append(flat.reshape(rows, LANES))
        layout.append((rows, n, a.shape))
    total = sum(r for r, _, _ in layout)
    parts.append(jnp.zeros((-total % PACK_ROWS, LANES), F32))
    return jnp.concatenate(parts, axis=0), layout


def _unpack(buf, layout):
    out, r0 = [], 0
    for rows, n, shape in layout:
        out.append(buf[r0:r0 + rows].reshape(-1)[:n].reshape(shape))
        r0 += rows
    return out


SMALL = ("mix_norm_pre", "lam_re", "lam_im", "log_dt", "ssm_b_re", "ssm_b_im", "ssm_c_re", "ssm_c_im",
         "ssm_d", "b_glu", "attn_out_norm", "ssm_out_norm", "mix_norm_post", "mlp_norm_pre",
         "mlp_norm_post", "ple_norm_pre", "ple_norm_post")
BIG = ("w_in", "w_glu", "w_out", "w_up", "w_down", "w_ple_gate", "w_ple_proj")
WEIGHTS = ("mix_norm_pre", "w_in", "lam_re", "lam_im", "log_dt", "ssm_b_re", "ssm_b_im", "ssm_c_re",
           "ssm_c_im", "ssm_d", "w_glu", "b_glu", "attn_out_norm", "ssm_out_norm", "w_out",
           "mix_norm_post", "mlp_norm_pre", "w_up", "w_down", "mlp_norm_post", "ple_norm_pre",
           "w_ple_gate", "w_ple_proj", "ple_norm_post")


def kernel(x, p, mix_norm_pre, w_in, lam_re, lam_im, log_dt, ssm_b_re, ssm_b_im, ssm_c_re, ssm_c_im, ssm_d, w_glu, b_glu, attn_out_norm, ssm_out_norm, w_out, mix_norm_post, mlp_norm_pre, w_up, w_down, mlp_norm_post, ple_norm_pre, w_ple_gate, w_ple_proj, ple_norm_post, loss_target, m_mix_norm_pre, m_w_in, m_lam_re, m_lam_im, m_log_dt, m_ssm_b_re, m_ssm_b_im, m_ssm_c_re, m_ssm_c_im, m_ssm_d, m_w_glu, m_b_glu, m_attn_out_norm, m_ssm_out_norm, m_w_out, m_mix_norm_post, m_mlp_norm_pre, m_w_up, m_w_down, m_mlp_norm_post, m_ple_norm_pre, m_w_ple_gate, m_w_ple_proj, m_ple_norm_post, v_mix_norm_pre, v_w_in, v_lam_re, v_lam_im, v_log_dt, v_ssm_b_re, v_ssm_b_im, v_ssm_c_re, v_ssm_c_im, v_ssm_d, v_w_glu, v_b_glu, v_attn_out_norm, v_ssm_out_norm, v_w_out, v_mix_norm_post, v_mlp_norm_pre, v_w_up, v_w_down, v_mlp_norm_post, v_ple_norm_pre, v_w_ple_gate, v_w_ple_proj, v_ple_norm_post):
    args = dict(locals())
    W = {n: args[n][0] for n in WEIGHTS}
    Mo = {n: args["m_" + n][0] for n in WEIGHTS}
    Vo = {n: args["v_" + n][0] for n in WEIGHTS}
    xs, ps, tgt = x[0], p[0, 0], loss_target[0]
    S, D = xs.shape
    SW = W["ssm_d"].shape[0]
    AW = W["attn_out_norm"].shape[0]
    heads = AW // HEAD_DIM
    G = SW // SSM_C
    nbk = SW // LANES
    assert W["w_in"].shape[1] * N_CHIPS == 3 * AW + SW and AW == SW

    row = lambda a: a.reshape(1, -1)

    ag_groups = (("w_in",), ("w_glu", "w_out"), ("w_up",), ("w_down", "w_ple_gate", "w_ple_proj"))
    ag_names = [n for g in ag_groups for n in g]
    def in_halves(a):
        return a.reshape(N_CHIPS, 2, a.shape[1] // 2, a.shape[2])

    def placed(n, after=None):
        return in_halves(_place_own(W[n], gather=True, name="ag_place_" + n, after=after))

    first_sems, first_land, _, first_token = _exchange_start([placed("w_in")], [], [[0]], name="ag_start_first")
    rest_sems, rest_land, _, ag_token = _exchange_start(
        [placed(n, first_token) for n in ag_names[1:]], [],
        [[ag_names.index(n) - 1 for n in g] for g in ag_groups[1:]], name="ag_start")
    ag_sems, ag_land = first_sems + rest_sems, list(first_land) + list(rest_land)

    def fetched(gi, after):
        return _exchange_wait([ag_land[ag_names.index(n)] for n in ag_groups[gi]], [], ag_sems[gi], after,
                              name=f"ag_wait_{gi}")

    def whole(gis, bufs):
        names = [n for gi in gis for n in ag_groups[gi]]
        return {n: a.reshape(N_CHIPS, -1, a.shape[-1]) for n, a in zip(names, bufs)}

    lr_e = W["lam_re"].reshape(nbk, 1, STATE_LANES)
    li_e = W["lam_im"].reshape(nbk, 1, STATE_LANES)
    ldt_e = jnp.repeat(W["log_dt"], SSM_P).reshape(nbk, 1, STATE_LANES)
    bre_e, bim_e = _expand_b(W["ssm_b_re"]), _expand_b(W["ssm_b_im"])
    cre_e, cim_e = _expand_c(W["ssm_c_re"]), _expand_c(W["ssm_c_im"])
    d_row = row(W["ssm_d"])

    hn1 = _norm_cast(xs, row(W["mix_norm_pre"]) + ag_token[0, 0], name="norm_in")
    w_in_f = whole([0], _pair_fill(fetched(0, hn1), name="ag_pair_0"))["w_in"]
    qkv_b = _proj_qkv(hn1, w_in_f)
    outs, lses = zip(*[_attn_fwd(qb, d, heads) for d, qb in zip(DILATIONS, qkv_b)])
    pair_a_sems, pair_a, pair_a_token = _pair_start(fetched(1, outs[-1]), name="ag_pair_start_a")
    u = _matmul(hn1, w_in_f, name="proj_u", b_shards=N_CHIPS, b_cols=(3 * AW, SW), after=pair_a_token)
    y1, y2b, st_r, st_i, ends_r, ends_i = _ssm_fwd(u, lr_e, li_e, ldt_e, bre_e, bim_e, cre_e, cim_e, d_row)
    pair_b_sems, pair_b, pair_b_token = _pair_start(fetched(2, y2b), name="ag_pair_start_b")
    full = whole([1], _pair_wait(pair_a, pair_a_sems, y2b, name="ag_pair_wait_a"))
    w_glu_f = full["w_glu"].reshape(SW, SW)
    w_out_f = full["w_out"].reshape(AW + SW, D)
    z = _matmul(y2b, w_glu_f, name="glu_z", after=pair_b_token)
    attn, lse_b, mixed = _mix_fwd(outs, lses, y1, z, row(W["b_glu"]), row(W["attn_out_norm"]), row(W["ssm_out_norm"]))
    mo = _matmul(mixed, w_out_f, name="mix_out")
    h1, hn2 = _res_norm(xs, mo, row(W["mix_norm_post"]), row(W["mlp_norm_pre"]), name="res_mix")
    w_up_f = whole([2], _pair_wait(pair_b, pair_b_sems, hn2, name="ag_pair_wait_b"))["w_up"]
    up, act = _matmul(hn2, w_up_f, name="mlp_up", b_shards=N_CHIPS, relu2=True, out_dtype=BF16)
    full = whole([3], _pair_fill(fetched(3, act), name="ag_pair_3"))
    w_down_f = full["w_down"].reshape(-1, D)
    w_pg_f = full["w_ple_gate"].reshape(D, D)
    w_pp_f = full["w_ple_proj"]
    ff = _matmul(act, w_down_f, name="mlp_down")
    h2, hn3 = _res_norm(h1, ff, row(W["mlp_norm_post"]), row(W["ple_norm_pre"]), name="res_mlp")
    gl = _matmul(hn3, w_pg_f, name="ple_gate")
    e = _matmul(ps.astype(BF16), w_pp_f, name="ple_proj", b_shards=N_CHIPS)

    dh3, dgl, de, loss_part, dg_ple_post = _final(h2, gl, e, row(W["ple_norm_post"]), tgt)
    gW = {}
    out_g, out_d, out_m, out_v = {}, {}, {}, {}

    def scatter_start(names, tag):
        parts = [gW[n] if gW[n].ndim == 3 else gW[n].reshape((N_CHIPS, -1, gW[n].shape[1])) for n in names]
        sems, land, src, token = _exchange_start(
            [_place_own(part, gather=False, name="rs_place_" + n) for n, part in zip(names, parts)], parts,
            [list(range(len(names)))], name=f"rs_start_{tag}")
        return (names, sems[0], land, src), token

    def scatter_sums(batches, after):
        names, sums = [], []
        for tag, (batch_names, sems, land, src) in batches:
            landed = _exchange_wait(land, src, sems, after, name=f"rs_wait_{tag}")
            names += batch_names
            sums += [_sum_partials(l, name="sum_" + n) for n, l in zip(batch_names, landed)]
        return names, sums

    def apply(names, sums, theirs):
        for n, a, b in zip(names, sums, theirs):
            out_g[n], out_d[n], out_m[n], out_v[n] = _adamw_pair(a, b, W[n], Mo[n], Vo[n], name="adamw_" + n)

    def swap_begin(batches, after, tag):
        names, sums = scatter_sums(batches, after)
        sems, sums, lands, token = _swap_start(sums, name=f"swap_start_{tag}")
        return (names, sems, sums, lands), token

    def swap_end(swap, after, tag):
        names, sems, sums, lands = swap
        sums, theirs = _swap_wait(sums, lands, sems, after, name=f"swap_wait_{tag}")
        apply(names, sums, theirs)

    def scatter_finish(batch, after, tag):
        names, sums = scatter_sums([(tag, batch)], after)
        apply(names, sums, _swap_with_sibling(sums, name=f"swap_{tag}"))

    gW["w_ple_proj"] = _matmul(ps.astype(BF16), de, name="d_w_ple_proj", ta=True, out_dtype=BF16, out_shards=N_CHIPS)
    gW["w_ple_gate"] = _matmul(hn3, dgl, name="d_w_ple_gate", ta=True, out_dtype=BF16)
    dhn3 = _matmul(dgl, w_pg_f, name="d_hn3", tb=True)
    dh2, dff, dg_ple_pre, dg_mlp_post = _bwd_res_norm(
        dh3, dhn3, h2, row(W["ple_norm_pre"]), ff, row(W["mlp_norm_post"]), name="bwd_res_mlp")
    gW["w_down"] = _matmul(act, dff, name="d_w_down", ta=True, out_dtype=BF16)
    batch1, token1 = scatter_start(("w_ple_proj", "w_ple_gate", "w_down"), 1)
    dup = _matmul(dff, w_down_f, name="d_up", tb=True, after=token1, relu2_of=up, out_dtype=BF16)
    gW["w_up"] = _matmul(hn2, dup, name="d_w_up", ta=True, out_dtype=BF16, out_shards=N_CHIPS)
    batch2, token2 = scatter_start(("w_up",), 2)
    dhn2 = _matmul(dup, w_up_f, name="d_hn2", tb=True, b_shards=N_CHIPS, after=token2)
    dh1, dmo, dg_mlp_pre, dg_mix_post = _bwd_res_norm(
        dh2, dhn2, h1, row(W["mlp_norm_pre"]), mo, row(W["mix_norm_post"]), name="bwd_res_mix")
    gW["w_out"] = _matmul(mixed, dmo, name="d_w_out", ta=True, out_dtype=BF16)
    dmixed = _matmul(dmo, w_out_f, name="d_mixed", tb=True)
    dattn_b, dd_b, dz, dy2a, dg_attn, dg_ssm, db_glu = _mix_bwd(
        dmixed, attn, y1, z, row(W["b_glu"]), row(W["attn_out_norm"]), row(W["ssm_out_norm"]))
    gW["w_glu"] = _matmul(y2b, dz, name="d_w_glu", ta=True, out_dtype=BF16)
    batch3, token3 = scatter_start(("w_out", "w_glu"), 3)
    dy2b = _matmul(dz, w_glu_f, name="d_y2", tb=True, after=token3)
    du, dar8, dai8, dcr_e, dci_e, dbr_e, dbi_e, dd8 = _ssm_bwd(
        u, y1, dy2a, dy2b, st_r, st_i, ends_r, ends_i, lr_e, li_e, ldt_e, bre_e, bim_e, cre_e, cim_e, d_row)
    swap_a, token_a = swap_begin([(1, batch1)], du, "a")
    dlr_e, dli_e, dldt_e, dbre_e, dbim_e = _ssm_param_bwd(dar8, dai8, dbr_e, dbi_e, lr_e, li_e, ldt_e, bre_e, bim_e)

    dqs, dks, dvs = zip(*[_attn_bwd(qb, da, l, dd_, d, heads, token_a)
                          for d, qb, da, l, dd_ in zip(DILATIONS, qkv_b, dattn_b, lse_b, dd_b)])
    dproj = _dproj_join(dqs, dks, dvs, du)
    swap_end(swap_a, dproj, "a")
    swap_b, token_b = swap_begin([(2, batch2), (3, batch3)], dproj, "b")
    gW["w_in"] = _matmul(hn1, dproj, name="d_w_in", ta=True, out_dtype=BF16, out_shards=N_CHIPS, after=token_b)
    batch4, token4 = scatter_start(("w_in",), 4)
    dhn1 = _matmul(dproj, w_in_f, name="d_hn1", tb=True, b_shards=N_CHIPS, after=token4)
    grad_x, dg_mix_pre = _bwd_first(dh1, dhn1, xs, row(W["mix_norm_pre"]))
    swap_end(swap_b, grad_x, "b")
    scatter_finish(batch4, grad_x, 4)

    small_g = {
        "mix_norm_pre": dg_mix_pre, "lam_re": dlr_e.reshape(G, SSM_P), "lam_im": dli_e.reshape(G, SSM_P),
        "log_dt": dldt_e.reshape(G, SSM_P)[:, 0], "ssm_b_re": _collapse_b(dbre_e), "ssm_b_im": _collapse_b(dbim_e),
        "ssm_c_re": _collapse_c(dcr_e), "ssm_c_im": _collapse_c(dci_e), "ssm_d": dd8.sum(axis=1).reshape(-1),
        "b_glu": db_glu, "attn_out_norm": dg_attn, "ssm_out_norm": dg_ssm, "mix_norm_post": dg_mix_post,
        "mlp_norm_pre": dg_mlp_pre, "mlp_norm_post": dg_mlp_post, "ple_norm_pre": dg_ple_pre,
        "ple_norm_post": dg_ple_post,
    }
    g_pack, layout = _pack([small_g[n].reshape(W[n].shape) for n in SMALL])
    w_pack, _ = _pack([W[n] for n in SMALL])
    m_pack, _ = _pack([Mo[n] for n in SMALL])
    v_pack, _ = _pack([Vo[n] for n in SMALL])
    g_sum = _all_reduce_small(g_pack)
    packed = (g_sum,) + tuple(_adamw_small(g_sum, w_pack, m_pack, v_pack))
    for dst, buf in zip((out_g, out_d, out_m, out_v), packed):
        dst.update(zip(SMALL, _unpack(buf, layout)))

    loss = lax.psum(loss_part[0, 0], ("x", "y", "c"))
    lead = lambda a: a[None]
    return (loss, grad_x[None],
            *[lead(out_g[n]) for n in WEIGHTS], *[lead(out_d[n]) for n in WEIGHTS],
            *[lead(out_m[n]) for n in WEIGHTS], *[lead(out_v[n]) for n in WEIGHTS])
```
